```python
import jax, jax.numpy as jnp
from jax import lax
import numpy as np

D_MODEL = 1024
BATCH = 8
SEQ = 8192
DEPTH = 4

N_MIXERS = 2
N_ATTN_LAYERS = (DEPTH + 1) // 2
N_CONV_LAYERS = DEPTH // 2
N_HEADS = 8
QK_NOPE_DIM = 128
QK_ROPE_DIM = 64
QK_HEAD_DIM = QK_NOPE_DIM + QK_ROPE_DIM
V_HEAD_DIM = 128
Q_LORA_RANK = 256
KV_LORA_RANK = 128
DOWN_DIM = Q_LORA_RANK + KV_LORA_RANK + QK_ROPE_DIM
ROPE_THETA = 10000.0
Q_BLOCK = 128
CONV_WIDTH = 3
D_FF = 4 * D_MODEL
EPS = 1e-6
MAX_POS_OFFSET = 1024

kernel_name = 'hybrid_mla_shortconv_sqrelu'


def rms_norm(x, g):
    xf = x.astype(jnp.float32)
    xf = xf * lax.rsqrt(jnp.mean(xf * xf, axis=-1, keepdims=True) + EPS)
    return xf.astype(x.dtype) * g


def rope_tables(positions):
    inv_freq = ROPE_THETA ** (-jnp.arange(0, QK_ROPE_DIM, 2, dtype=jnp.float32) / QK_ROPE_DIM)
    ang = positions.astype(jnp.float32)[..., None] * inv_freq
    return jnp.cos(ang)[:, :, None, :], jnp.sin(ang)[:, :, None, :]


def apply_rope(x, cos, sin):
    x1, x2 = jnp.split(x, 2, axis=-1)
    cos = cos.astype(x.dtype)
    sin = sin.astype(x.dtype)
    return jnp.concatenate([x1 * cos - x2 * sin, x2 * cos + x1 * sin], axis=-1)


def mla_mixer(h, positions, cos, sin, w_down, g_q_a, g_kv_a, w_uq, w_ukv, g_qnorm, g_knorm, w_o):
    b, s, _ = h.shape
    a = h @ w_down
    c_q = rms_norm(a[..., :Q_LORA_RANK], g_q_a)
    c_kv = rms_norm(a[..., Q_LORA_RANK:Q_LORA_RANK + KV_LORA_RANK], g_kv_a)
    k_pe = a[..., Q_LORA_RANK + KV_LORA_RANK:]
    q = (c_q @ w_uq).reshape(b, s, N_HEADS, QK_HEAD_DIM)
    kv = (c_kv @ w_ukv).reshape(b, s, N_HEADS, QK_NOPE_DIM + V_HEAD_DIM)
    k_nope, v = kv[..., :QK_NOPE_DIM], kv[..., QK_NOPE_DIM:]
    k_pe = jnp.broadcast_to(k_pe[:, :, None, :], (b, s, N_HEADS, QK_ROPE_DIM))
    k = jnp.concatenate([k_nope, k_pe], axis=-1)
    q = rms_norm(q, g_qnorm)
    k = rms_norm(k, g_knorm)
    q = jnp.concatenate([q[..., :QK_NOPE_DIM], apply_rope(q[..., QK_NOPE_DIM:], cos, sin)], axis=-1)
    k = jnp.concatenate([k[..., :QK_NOPE_DIM], apply_rope(k[..., QK_NOPE_DIM:], cos, sin)], axis=-1)
    scale = QK_HEAD_DIM ** -0.5
    nb = s // Q_BLOCK
    q_blocks = q.reshape(b, nb, Q_BLOCK, N_HEADS, QK_HEAD_DIM).transpose(1, 0, 2, 3, 4)
    pos_blocks = positions.reshape(b, nb, Q_BLOCK).transpose(1, 0, 2)

    def attend(args):
        q_blk, pos_blk = args
        sc = jnp.einsum('bqhd,bkhd->bhqk', q_blk, k).astype(jnp.float32) * scale
        mask = pos_blk[:, None, :, None] >= positions[:, None, None, :]
        sc = jnp.where(mask, sc, jnp.finfo(jnp.float32).min)
        p = jax.nn.softmax(sc, axis=-1).astype(v.dtype)
        return jnp.einsum('bhqk,bkhd->bqhd', p, v)

    o = lax.map(attend, (q_blocks, pos_blocks))
    o = o.transpose(1, 0, 2, 3, 4).reshape(b, s, N_HEADS * V_HEAD_DIM)
    return o @ w_o


def short_conv_mixer(h, w_in, conv_w, w_out):
    bcu = h @ w_in
    gate_b, gate_c, u = jnp.split(bcu, 3, axis=-1)
    u = gate_c * u
    u = lax.conv_general_dilated(
        u, conv_w[:, None, :], window_strides=(1,), padding=((CONV_WIDTH - 1, 0),),
        dimension_numbers=('NWC', 'WIO', 'NWC'), feature_group_count=D_MODEL)
    return (gate_b * u) @ w_out


def sqrelu_mlp(h, w1, w2):
    return jnp.square(jax.nn.relu(h @ w1)) @ w2


def _fwd_setup_inputs(seed: int = 0) -> dict:
    key = jax.random.key(seed)
    ks = jax.random.split(key, 20)
    nrm = lambda k, shape, fan_in: jax.random.normal(k, shape, jnp.float32) * fan_in ** -0.5
    gain = lambda k, shape: 1.0 + 0.02 * jax.random.normal(k, shape, jnp.float32)
    x = jax.random.normal(ks[0], (BATCH, SEQ, D_MODEL), jnp.float32)
    offset = jax.random.randint(ks[1], (BATCH, 1), 0, MAX_POS_OFFSET, dtype=jnp.int32)
    positions = offset + jnp.arange(SEQ, dtype=jnp.int32)[None, :]
    return {
        'x': x,
        'positions': positions,
        'g_mix': gain(ks[2], (DEPTH, D_MODEL)),
        'g_mlp': gain(ks[3], (DEPTH, D_MODEL)),
        'attn_w_down': nrm(ks[4], (N_ATTN_LAYERS, D_MODEL, DOWN_DIM), D_MODEL),
        'attn_g_q_a': gain(ks[5], (N_ATTN_LAYERS, Q_LORA_RANK)),
        'attn_g_kv_a': gain(ks[6], (N_ATTN_LAYERS, KV_LORA_RANK)),
        'attn_w_uq': nrm(ks[7], (N_ATTN_LAYERS, Q_LORA_RANK, N_HEADS * QK_HEAD_DIM), Q_LORA_RANK),
        'attn_w_ukv': nrm(ks[8], (N_ATTN_LAYERS, KV_LORA_RANK, N_HEADS * (QK_NOPE_DIM + V_HEAD_DIM)), KV_LORA_RANK),
        'attn_g_qnorm': gain(ks[9], (N_ATTN_LAYERS, QK_HEAD_DIM)),
        'attn_g_knorm': gain(ks[10], (N_ATTN_LAYERS, QK_HEAD_DIM)),
        'attn_w_o': nrm(ks[11], (N_ATTN_LAYERS, N_HEADS * V_HEAD_DIM, D_MODEL), N_HEADS * V_HEAD_DIM),
        'conv_w_in': nrm(ks[12], (N_CONV_LAYERS, D_MODEL, 3 * D_MODEL), D_MODEL),
        'conv_w': nrm(ks[13], (N_CONV_LAYERS, CONV_WIDTH, D_MODEL), CONV_WIDTH),
        'conv_w_out': nrm(ks[14], (N_CONV_LAYERS, D_MODEL, D_MODEL), D_MODEL),
        'mlp_w1': nrm(ks[15], (DEPTH, D_MODEL, D_FF), D_MODEL),
        'mlp_w2': nrm(ks[16], (DEPTH, D_FF, D_MODEL), D_FF),
    }


def _fwd_reference(x, positions, g_mix, g_mlp, attn_w_down, attn_g_q_a, attn_g_kv_a, attn_w_uq,
              attn_w_ukv, attn_g_qnorm, attn_g_knorm, attn_w_o, conv_w_in, conv_w, conv_w_out,
              mlp_w1, mlp_w2):
    cos, sin = rope_tables(positions)
    for i in range(DEPTH):
        h = rms_norm(x, g_mix[i])
        j = i // N_MIXERS
        if i % N_MIXERS == 0:
            x = x + mla_mixer(h, positions, cos, sin, attn_w_down[j], attn_g_q_a[j], attn_g_kv_a[j],
                              attn_w_uq[j], attn_w_ukv[j], attn_g_qnorm[j], attn_g_knorm[j], attn_w_o[j])
        else:
            x = x + short_conv_mixer(h, conv_w_in[j], conv_w[j], conv_w_out[j])
        h = rms_norm(x, g_mlp[i])
        x = x + sqrelu_mlp(h, mlp_w1[i], mlp_w2[i])
    return x


import jax as _jax
import jax.numpy as _jnp

TWIN_FORMAT = 'train_step'
FWD_PARAMS = ['x', 'positions', 'g_mix', 'g_mlp', 'attn_w_down', 'attn_g_q_a', 'attn_g_kv_a', 'attn_w_uq', 'attn_w_ukv', 'attn_g_qnorm', 'attn_g_knorm', 'attn_w_o', 'conv_w_in', 'conv_w', 'conv_w_out', 'mlp_w1', 'mlp_w2']
TWIN_WEIGHTS = ['g_mix', 'g_mlp', 'attn_w_down', 'attn_g_q_a', 'attn_g_kv_a', 'attn_w_uq', 'attn_w_ukv', 'attn_g_qnorm', 'attn_g_knorm', 'attn_w_o', 'conv_w_in', 'conv_w', 'conv_w_out', 'mlp_w1', 'mlp_w2']
TWIN_DIFF_INPUT = 'x'
TWIN_INPUTS = ['x', 'positions', 'g_mix', 'g_mlp', 'attn_w_down', 'attn_g_q_a', 'attn_g_kv_a', 'attn_w_uq', 'attn_w_ukv', 'attn_g_qnorm', 'attn_g_knorm', 'attn_w_o', 'conv_w_in', 'conv_w', 'conv_w_out', 'mlp_w1', 'mlp_w2', 'loss_target', 'm_g_mix', 'm_g_mlp', 'm_attn_w_down', 'm_attn_g_q_a', 'm_attn_g_kv_a', 'm_attn_w_uq', 'm_attn_w_ukv', 'm_attn_g_qnorm', 'm_attn_g_knorm', 'm_attn_w_o', 'm_conv_w_in', 'm_conv_w', 'm_conv_w_out', 'm_mlp_w1', 'm_mlp_w2', 'v_g_mix', 'v_g_mlp', 'v_attn_w_down', 'v_attn_g_q_a', 'v_attn_g_kv_a', 'v_attn_w_uq', 'v_attn_w_ukv', 'v_attn_g_qnorm', 'v_attn_g_knorm', 'v_attn_w_o', 'v_conv_w_in', 'v_conv_w', 'v_conv_w_out', 'v_mlp_w1', 'v_mlp_w2']
TWIN_OUTPUTS = ['loss', 'grad_x', 'grad_g_mix', 'grad_g_mlp', 'grad_attn_w_down', 'grad_attn_g_q_a', 'grad_attn_g_kv_a', 'grad_attn_w_uq', 'grad_attn_w_ukv', 'grad_attn_g_qnorm', 'grad_attn_g_knorm', 'grad_attn_w_o', 'grad_conv_w_in', 'grad_conv_w', 'grad_conv_w_out', 'grad_mlp_w1', 'grad_mlp_w2', 'delta_g_mix', 'delta_g_mlp', 'delta_attn_w_down', 'delta_attn_g_q_a', 'delta_attn_g_kv_a', 'delta_attn_w_uq', 'delta_attn_w_ukv', 'delta_attn_g_qnorm', 'delta_attn_g_knorm', 'delta_attn_w_o', 'delta_conv_w_in', 'delta_conv_w', 'delta_conv_w_out', 'delta_mlp_w1', 'delta_mlp_w2', 'new_m_g_mix', 'new_m_g_mlp', 'new_m_attn_w_down', 'new_m_attn_g_q_a', 'new_m_attn_g_kv_a', 'new_m_attn_w_uq', 'new_m_attn_w_ukv', 'new_m_attn_g_qnorm', 'new_m_attn_g_knorm', 'new_m_attn_w_o', 'new_m_conv_w_in', 'new_m_conv_w', 'new_m_conv_w_out', 'new_m_mlp_w1', 'new_m_mlp_w2', 'new_v_g_mix', 'new_v_g_mlp', 'new_v_attn_w_down', 'new_v_attn_g_q_a', 'new_v_attn_g_kv_a', 'new_v_attn_w_uq', 'new_v_attn_w_ukv', 'new_v_attn_g_qnorm', 'new_v_attn_g_knorm', 'new_v_attn_w_o', 'new_v_conv_w_in', 'new_v_conv_w', 'new_v_conv_w_out', 'new_v_mlp_w1', 'new_v_mlp_w2']
TWIN_LEAF_KINDS = {'loss': 'loss', 'grad_x': 'grad_x', 'grad_g_mix': 'grad_w', 'grad_g_mlp': 'grad_w', 'grad_attn_w_down': 'grad_w', 'grad_attn_g_q_a': 'grad_w', 'grad_attn_g_kv_a': 'grad_w', 'grad_attn_w_uq': 'grad_w', 'grad_attn_w_ukv': 'grad_w', 'grad_attn_g_qnorm': 'grad_w', 'grad_attn_g_knorm': 'grad_w', 'grad_attn_w_o': 'grad_w', 'grad_conv_w_in': 'grad_w', 'grad_conv_w': 'grad_w', 'grad_conv_w_out': 'grad_w', 'grad_mlp_w1': 'grad_w', 'grad_mlp_w2': 'grad_w', 'delta_g_mix': 'delta_w', 'delta_g_mlp': 'delta_w', 'delta_attn_w_down': 'delta_w', 'delta_attn_g_q_a': 'delta_w', 'delta_attn_g_kv_a': 'delta_w', 'delta_attn_w_uq': 'delta_w', 'delta_attn_w_ukv': 'delta_w', 'delta_attn_g_qnorm': 'delta_w', 'delta_attn_g_knorm': 'delta_w', 'delta_attn_w_o': 'delta_w', 'delta_conv_w_in': 'delta_w', 'delta_conv_w': 'delta_w', 'delta_conv_w_out': 'delta_w', 'delta_mlp_w1': 'delta_w', 'delta_mlp_w2': 'delta_w', 'new_m_g_mix': 'new_m', 'new_m_g_mlp': 'new_m', 'new_m_attn_w_down': 'new_m', 'new_m_attn_g_q_a': 'new_m', 'new_m_attn_g_kv_a': 'new_m', 'new_m_attn_w_uq': 'new_m', 'new_m_attn_w_ukv': 'new_m', 'new_m_attn_g_qnorm': 'new_m', 'new_m_attn_g_knorm': 'new_m', 'new_m_attn_w_o': 'new_m', 'new_m_conv_w_in': 'new_m', 'new_m_conv_w': 'new_m', 'new_m_conv_w_out': 'new_m', 'new_m_mlp_w1': 'new_m', 'new_m_mlp_w2': 'new_m', 'new_v_g_mix': 'new_v', 'new_v_g_mlp': 'new_v', 'new_v_attn_w_down': 'new_v', 'new_v_attn_g_q_a': 'new_v', 'new_v_attn_g_kv_a': 'new_v', 'new_v_attn_w_uq': 'new_v', 'new_v_attn_w_ukv': 'new_v', 'new_v_attn_g_qnorm': 'new_v', 'new_v_attn_g_knorm': 'new_v', 'new_v_attn_w_o': 'new_v', 'new_v_conv_w_in': 'new_v', 'new_v_conv_w': 'new_v', 'new_v_conv_w_out': 'new_v', 'new_v_mlp_w1': 'new_v', 'new_v_mlp_w2': 'new_v'}


def _forward(args):
    return _fwd_reference(*[args[k] for k in FWD_PARAMS])


def _output_shape():
    def fwd():
        inp = _fwd_setup_inputs(0)
        return _fwd_reference(*[inp[k] for k in FWD_PARAMS])
    out = _jax.eval_shape(fwd)
    return out.shape, out.dtype

N_MICROBATCH = 1
ADAM_LR = 0.001
ADAM_B1 = 0.9
ADAM_B2 = 0.999
ADAM_EPS = 1e-08
ADAM_WD = 0.01
ADAM_STEP = 10
PER_EXAMPLE_BATCH_AXIS = {'x': 0, 'positions': 0, 'loss_target': 0}
SHARED_INPUTS = []
_WEIGHT_DTYPES = {'g_mix': _jnp.float32, 'g_mlp': _jnp.float32, 'attn_w_down': _jnp.float32, 'attn_g_q_a': _jnp.float32, 'attn_g_kv_a': _jnp.float32, 'attn_w_uq': _jnp.float32, 'attn_w_ukv': _jnp.float32, 'attn_g_qnorm': _jnp.float32, 'attn_g_knorm': _jnp.float32, 'attn_w_o': _jnp.float32, 'conv_w_in': _jnp.float32, 'conv_w': _jnp.float32, 'conv_w_out': _jnp.float32, 'mlp_w1': _jnp.float32, 'mlp_w2': _jnp.float32}
MOMENT_SCALE = {'g_mix': 1.404510e+02, 'g_mlp': 1.996569e+02, 'attn_w_down': 4.180963e+01, 'attn_g_q_a': 1.771416e+00, 'attn_g_kv_a': 1.128526e+02, 'attn_w_uq': 7.241403e-01, 'attn_w_ukv': 1.710203e+01, 'attn_g_qnorm': 2.858138e+00, 'attn_g_knorm': 2.847964e+00, 'attn_w_o': 2.332404e+01, 'conv_w_in': 1.173034e+01, 'conv_w': 3.995000e+01, 'conv_w_out': 5.100191e+00, 'mlp_w1': 1.380436e+01, 'mlp_w2': 5.708950e+01}


def _to_microbatches(a, axis):
    t = _jnp.moveaxis(a, axis, 0)
    t = t.reshape((N_MICROBATCH, t.shape[0] // N_MICROBATCH) + t.shape[1:])
    return _jnp.moveaxis(t, 1, axis + 1)


def setup_inputs(seed: int = 0) -> dict:
    inp = _fwd_setup_inputs(seed)
    key = _jax.random.fold_in(_jax.random.key(seed), 7919)
    shape, _ = _output_shape()
    out = dict(inp)
    out["loss_target"] = _jax.random.normal(_jax.random.fold_in(key, 0), shape, _jnp.float32)
    for i, name in enumerate(TWIN_WEIGHTS):
        w = inp[name].astype(_jnp.float32)
        if MOMENT_SCALE is None:
            s = _jnp.sqrt(_jnp.mean(_jnp.square(w)) + 1e-30)
        else:
            s = MOMENT_SCALE[name]
        km, kv = _jax.random.split(_jax.random.fold_in(key, i + 1))
        out[name] = w
        out["m_" + name] = s * _jax.random.normal(km, w.shape, _jnp.float32)
        out["v_" + name] = (s * s) * _jax.random.uniform(kv, w.shape, _jnp.float32, 0.5, 1.5)
    if N_MICROBATCH > 1:
        for name, axis in PER_EXAMPLE_BATCH_AXIS.items():
            out[name] = _to_microbatches(out[name], axis)
    return {'x': out['x'], 'positions': out['positions'], 'g_mix': out['g_mix'], 'g_mlp': out['g_mlp'], 'attn_w_down': out['attn_w_down'], 'attn_g_q_a': out['attn_g_q_a'], 'attn_g_kv_a': out['attn_g_kv_a'], 'attn_w_uq': out['attn_w_uq'], 'attn_w_ukv': out['attn_w_ukv'], 'attn_g_qnorm': out['attn_g_qnorm'], 'attn_g_knorm': out['attn_g_knorm'], 'attn_w_o': out['attn_w_o'], 'conv_w_in': out['conv_w_in'], 'conv_w': out['conv_w'], 'conv_w_out': out['conv_w_out'], 'mlp_w1': out['mlp_w1'], 'mlp_w2': out['mlp_w2'], 'loss_target': out['loss_target'], 'm_g_mix': out['m_g_mix'], 'm_g_mlp': out['m_g_mlp'], 'm_attn_w_down': out['m_attn_w_down'], 'm_attn_g_q_a': out['m_attn_g_q_a'], 'm_attn_g_kv_a': out['m_attn_g_kv_a'], 'm_attn_w_uq': out['m_attn_w_uq'], 'm_attn_w_ukv': out['m_attn_w_ukv'], 'm_attn_g_qnorm': out['m_attn_g_qnorm'], 'm_attn_g_knorm': out['m_attn_g_knorm'], 'm_attn_w_o': out['m_attn_w_o'], 'm_conv_w_in': out['m_conv_w_in'], 'm_conv_w': out['m_conv_w'], 'm_conv_w_out': out['m_conv_w_out'], 'm_mlp_w1': out['m_mlp_w1'], 'm_mlp_w2': out['m_mlp_w2'], 'v_g_mix': out['v_g_mix'], 'v_g_mlp': out['v_g_mlp'], 'v_attn_w_down': out['v_attn_w_down'], 'v_attn_g_q_a': out['v_attn_g_q_a'], 'v_attn_g_kv_a': out['v_attn_g_kv_a'], 'v_attn_w_uq': out['v_attn_w_uq'], 'v_attn_w_ukv': out['v_attn_w_ukv'], 'v_attn_g_qnorm': out['v_attn_g_qnorm'], 'v_attn_g_knorm': out['v_attn_g_knorm'], 'v_attn_w_o': out['v_attn_w_o'], 'v_conv_w_in': out['v_conv_w_in'], 'v_conv_w': out['v_conv_w'], 'v_conv_w_out': out['v_conv_w_out'], 'v_mlp_w1': out['v_mlp_w1'], 'v_mlp_w2': out['v_mlp_w2']}


def _loss(weights, diff, rest, loss_target):
    with _jax.named_scope("forward"):
        args = {**rest, TWIN_DIFF_INPUT: diff, **{k: w.astype(_WEIGHT_DTYPES[k]) for k, w in weights.items()}}
        y = _forward(args)
    with _jax.named_scope("loss_head"):
        err = _jnp.square(y.astype(_jnp.float32) - loss_target)
        return 0.5 * _jnp.sum(_jnp.mean(err, axis=-1)) if err.ndim else 0.5 * err


def _adamw(w, g, m, v):
    m = ADAM_B1 * m + (1.0 - ADAM_B1) * g
    v = ADAM_B2 * v + (1.0 - ADAM_B2) * _jnp.square(g)
    m_hat = m / (1.0 - ADAM_B1 ** ADAM_STEP)
    v_hat = v / (1.0 - ADAM_B2 ** ADAM_STEP)
    delta = -ADAM_LR * (m_hat / (_jnp.sqrt(v_hat) + ADAM_EPS) + ADAM_WD * w)
    return delta, m, v


def reference(x, positions, g_mix, g_mlp, attn_w_down, attn_g_q_a, attn_g_kv_a, attn_w_uq, attn_w_ukv, attn_g_qnorm, attn_g_knorm, attn_w_o, conv_w_in, conv_w, conv_w_out, mlp_w1, mlp_w2, loss_target, m_g_mix, m_g_mlp, m_attn_w_down, m_attn_g_q_a, m_attn_g_kv_a, m_attn_w_uq, m_attn_w_ukv, m_attn_g_qnorm, m_attn_g_knorm, m_attn_w_o, m_conv_w_in, m_conv_w, m_conv_w_out, m_mlp_w1, m_mlp_w2, v_g_mix, v_g_mlp, v_attn_w_down, v_attn_g_q_a, v_attn_g_kv_a, v_attn_w_uq, v_attn_w_ukv, v_attn_g_qnorm, v_attn_g_knorm, v_attn_w_o, v_conv_w_in, v_conv_w, v_conv_w_out, v_mlp_w1, v_mlp_w2):
    given = dict(x=x, positions=positions, g_mix=g_mix, g_mlp=g_mlp, attn_w_down=attn_w_down, attn_g_q_a=attn_g_q_a, attn_g_kv_a=attn_g_kv_a, attn_w_uq=attn_w_uq, attn_w_ukv=attn_w_ukv, attn_g_qnorm=attn_g_qnorm, attn_g_knorm=attn_g_knorm, attn_w_o=attn_w_o, conv_w_in=conv_w_in, conv_w=conv_w, conv_w_out=conv_w_out, mlp_w1=mlp_w1, mlp_w2=mlp_w2, loss_target=loss_target, m_g_mix=m_g_mix, m_g_mlp=m_g_mlp, m_attn_w_down=m_attn_w_down, m_attn_g_q_a=m_attn_g_q_a, m_attn_g_kv_a=m_attn_g_kv_a, m_attn_w_uq=m_attn_w_uq, m_attn_w_ukv=m_attn_w_ukv, m_attn_g_qnorm=m_attn_g_qnorm, m_attn_g_knorm=m_attn_g_knorm, m_attn_w_o=m_attn_w_o, m_conv_w_in=m_conv_w_in, m_conv_w=m_conv_w, m_conv_w_out=m_conv_w_out, m_mlp_w1=m_mlp_w1, m_mlp_w2=m_mlp_w2, v_g_mix=v_g_mix, v_g_mlp=v_g_mlp, v_attn_w_down=v_attn_w_down, v_attn_g_q_a=v_attn_g_q_a, v_attn_g_kv_a=v_attn_g_kv_a, v_attn_w_uq=v_attn_w_uq, v_attn_w_ukv=v_attn_w_ukv, v_attn_g_qnorm=v_attn_g_qnorm, v_attn_g_knorm=v_attn_g_knorm, v_attn_w_o=v_attn_w_o, v_conv_w_in=v_conv_w_in, v_conv_w=v_conv_w, v_conv_w_out=v_conv_w_out, v_mlp_w1=v_mlp_w1, v_mlp_w2=v_mlp_w2)
    weights = {n: given[n] for n in TWIN_WEIGHTS}
    shared = {n: given[n] for n in SHARED_INPUTS}
    per_example = {n: given[n] for n in ['x', 'positions']}
    grad_fn = _jax.value_and_grad(_loss, argnums=(0, 1))

    def one_microbatch(ex, loss_target):
        ex = dict(ex)
        diff = ex.pop(TWIN_DIFF_INPUT)
        return grad_fn(weights, diff, {**shared, **ex}, loss_target)

    if N_MICROBATCH == 1:
        loss, (grad_w, grad_x) = one_microbatch(per_example, given["loss_target"])
    else:
        def body(carry, xs):
            loss_sum, grad_sum = carry
            l_k, (gw_k, gx_k) = one_microbatch(xs[0], xs[1])
            with _jax.named_scope("update"):
                return (loss_sum + l_k, _jax.tree.map(_jnp.add, grad_sum, gw_k)), gx_k

        init = (_jnp.zeros((), _jnp.float32), _jax.tree.map(_jnp.zeros_like, weights))
        (loss, grad_w), grad_x = _jax.lax.scan(body, init, (per_example, given["loss_target"]))
    with _jax.named_scope("update"):
        delta_w, new_m, new_v = {}, {}, {}
        for n in TWIN_WEIGHTS:
            delta_w[n], new_m[n], new_v[n] = _adamw(weights[n], grad_w[n], given["m_" + n], given["v_" + n])
    return (loss, grad_x, *[grad_w[n] for n in TWIN_WEIGHTS], *[delta_w[n] for n in TWIN_WEIGHTS],
            *[new_m[n] for n in TWIN_WEIGHTS], *[new_v[n] for n in TWIN_WEIGHTS])
```

```python
import jax
import jax.numpy as jnp
from jax import lax
from jax.experimental import pallas as pl
from jax.experimental.pallas import tpu as pltpu

F32 = jnp.float32
MXU_DTYPE = jnp.bfloat16
WIRE_DTYPE = jnp.bfloat16

D_MODEL = 1024
N_HEADS = 8
NOPE = 128
ROPE = 64
QK_DIM = NOPE + ROPE
QK_PAD = 256
V_DIM = 128
Q_LORA = 256
KV_LORA = 128
DOWN = Q_LORA + KV_LORA + ROPE
DOWN_PAD = 512
ROPE_THETA = 10000.0
EPS = 1e-6
ADAM_LR, ADAM_B1, ADAM_B2, ADAM_EPS, ADAM_WD, ADAM_STEP = 0.001, 0.9, 0.999, 1e-08, 0.01, 10
N_DEV = 8
MESH_AXES = ("x", "y", "c")

TM = 512
TQ = 512
TROW = 512
TCH = 512
VMEM_LIMIT = 48 << 20

NN = (((1,), (0,)), ((), ()))
NT = (((1,), (1,)), ((), ()))
TN = (((0,), (0,)), ((), ()))


def _dot(a, b, dims=NN):
    return lax.dot_general(a.astype(MXU_DTYPE), b.astype(MXU_DTYPE), dims, preferred_element_type=F32)


def _params(n_axes):
    return pltpu.CompilerParams(dimension_semantics=("arbitrary",) * n_axes, vmem_limit_bytes=VMEM_LIMIT)


def _rms(xv, n):
    r = lax.rsqrt(jnp.sum(xv * xv, axis=-1, keepdims=True) / n + EPS)
    return xv * r, r


def _rms_bwd(dy, xhat, r, g, n):
    dg = jnp.sum(dy * xhat, axis=0, keepdims=True)
    dxh = dy * g
    dx = r * (dxh - xhat * (jnp.sum(dxh * xhat, axis=-1, keepdims=True) / n))
    return dx, dg


def _swap_halves(t):
    lane = lax.broadcasted_iota(jnp.int32, t.shape, 1)
    return jnp.where(lane < ROPE // 2, pltpu.roll(t, 128 - ROPE // 2, 1), pltpu.roll(t, ROPE // 2, 1))


def _rope(t, cos_t, sin_t):
    return t * cos_t + _swap_halves(t) * sin_t


def _rope_bwd(dout, cos_t, sin_t):
    return dout * cos_t + _swap_halves(dout * sin_t)


def rms_mm(x, g, w3, *, name, mlp=False):
    S, D = x.shape
    G, _, Nb = w3.shape
    tm = min(TM, S)
    tn = Nb if Nb <= 512 else 512
    nj = Nb // tn
    N = G * Nb

    def body(x_ref, g_ref, w_ref, h_ref, *rest):
        hs = rest[-1]

        @pl.when(pl.program_id(1) == 0)
        def _():
            xv = x_ref[...]
            r = lax.rsqrt(jnp.mean(xv * xv, axis=-1, keepdims=True) + EPS)
            h = (xv * r * g_ref[...]).astype(hs.dtype)
            hs[...] = h
            h_ref[...] = h

        acc = lax.dot_general(hs[...], w_ref[...].astype(hs.dtype), NN, preferred_element_type=F32)
        rest[0][...] = acc
        if mlp:
            rl = jnp.maximum(acc, 0.0)
            rest[1][...] = (rl * rl).astype(rest[1].dtype)

    out_shape = [jax.ShapeDtypeStruct((S, D), MXU_DTYPE), jax.ShapeDtypeStruct((S, N), F32)]
    out_specs = [pl.BlockSpec((tm, D), lambda i, j: (i, 0)), pl.BlockSpec((tm, tn), lambda i, j: (i, j))]
    if mlp:
        out_shape.append(jax.ShapeDtypeStruct((S, N), MXU_DTYPE))
        out_specs.append(pl.BlockSpec((tm, tn), lambda i, j: (i, j)))
    return pl.pallas_call(
        body, name=name, grid=(S // tm, G * nj),
        in_specs=[pl.BlockSpec((tm, D), lambda i, j: (i, 0)),
                  pl.BlockSpec((1, D), lambda i, j: (0, 0)),
                  pl.BlockSpec((None, D, tn), lambda i, j: (j // nj, 0, j % nj))],
        out_specs=out_specs, out_shape=out_shape,
        scratch_shapes=[pltpu.VMEM((tm, D), MXU_DTYPE)],
        compiler_params=_params(2),
    )(x, g.reshape(1, D), w3)


def mm_res(a, w, res, *, name):
    S, K = a.shape
    _, N = w.shape
    tm = min(TM, S)
    tk = min(K, 1024)
    nk = K // tk

    def body(a_ref, w_ref, r_ref, o_ref, acc):
        k = pl.program_id(1)

        @pl.when(k == 0)
        def _():
            acc[...] = jnp.zeros_like(acc)

        acc[...] += _dot(a_ref[...], w_ref[...])

        @pl.when(k == nk - 1)
        def _():
            o_ref[...] = r_ref[...] + acc[...]

    return pl.pallas_call(
        body, name=name, grid=(S // tm, nk),
        in_specs=[pl.BlockSpec((tm, tk), lambda i, k: (i, k)),
                  pl.BlockSpec((tk, N), lambda i, k: (k, 0)),
                  pl.BlockSpec((tm, N), lambda i, k: (i, 0))],
        out_specs=pl.BlockSpec((tm, N), lambda i, k: (i, 0)),
        out_shape=jax.ShapeDtypeStruct((S, N), F32),
        scratch_shapes=[pltpu.VMEM((tm, N), F32)],
        compiler_params=_params(2),
    )(a, w, res)


def mm_nt(a, w3, *, name, epi="plain", u=None, x=None, g=None, dx=None):
    S, N = a.shape
    G, Ko, Nb = w3.shape
    assert N == G * Nb
    tm = min(TM, S)
    tkk = Nb if Nb <= 1024 else 1024
    nk = Nb // tkk
    ks = G * nk
    tko = Ko if epi == "rms_bwd" else min(Ko, 512)

    def body(a_ref, w_ref, *rest):
        acc = rest[-1]
        i, k = pl.program_id(0), pl.program_id(2)

        @pl.when(k == 0)
        def _():
            acc[...] = jnp.zeros_like(acc)

        acc[...] += _dot(a_ref[...], w_ref[...], NT)

        @pl.when(k == ks - 1)
        def _():
            if epi == "plain":
                rest[0][...] = acc[...]
            elif epi == "mlp_du":
                u_ref, o_ref = rest[0], rest[1]
                o_ref[...] = (acc[...] * (2.0 * jnp.maximum(u_ref[...], 0.0))).astype(o_ref.dtype)
            else:
                x_ref, g_ref, dx_ref, o_ref, dg_ref = rest[:5]
                xhat, r = _rms(x_ref[...], Ko)
                dxb, dg = _rms_bwd(acc[...], xhat, r, g_ref[...], Ko)
                o_ref[...] = dx_ref[...] + dxb

                @pl.when(i == 0)
                def _():
                    dg_ref[...] = dg

                @pl.when(i != 0)
                def _():
                    dg_ref[...] += dg

    in_specs = [pl.BlockSpec((tm, tkk), lambda i, j, k: (i, k)),
                pl.BlockSpec((None, tko, tkk), lambda i, j, k: (k // nk, j, k % nk))]
    args = [a, w3]
    tile = pl.BlockSpec((tm, tko), lambda i, j, k: (i, j))
    if epi == "plain":
        out_shape, out_specs = jax.ShapeDtypeStruct((S, Ko), F32), tile
    elif epi == "mlp_du":
        in_specs.append(tile)
        args.append(u)
        out_shape, out_specs = jax.ShapeDtypeStruct((S, Ko), MXU_DTYPE), tile
    else:
        vec = pl.BlockSpec((1, Ko), lambda i, j, k: (0, 0))
        in_specs += [tile, vec, tile]
        args += [x, g.reshape(1, Ko), dx]
        out_shape = [jax.ShapeDtypeStruct((S, Ko), F32), jax.ShapeDtypeStruct((1, Ko), F32)]
        out_specs = [tile, vec]
    return pl.pallas_call(
        body, name=name, grid=(S // tm, Ko // tko, ks),
        in_specs=in_specs, out_specs=out_specs, out_shape=out_shape,
        scratch_shapes=[pltpu.VMEM((tm, tko), F32)],
        compiler_params=_params(3),
    )(*args)


def mm_tn(a, b, *, name, G, out_dtype):
    S, Ka = a.shape
    _, N = b.shape
    Nb = N // G
    tm = min(TM, S)
    tka = min(Ka, 512)
    tnb = Nb if Nb <= 512 else 512
    nj = Nb // tnb
    ns = S // tm

    def body(a_ref, b_ref, o_ref, acc):
        s = pl.program_id(2)

        @pl.when(s == 0)
        def _():
            acc[...] = jnp.zeros_like(acc)

        acc[...] += _dot(a_ref[...], b_ref[...], TN)

        @pl.when(s == ns - 1)
        def _():
            o_ref[...] = acc[...].astype(o_ref.dtype)

    return pl.pallas_call(
        body, name=name, grid=(Ka // tka, G * nj, ns),
        in_specs=[pl.BlockSpec((tm, tka), lambda i, j, s: (s, i)),
                  pl.BlockSpec((tm, tnb), lambda i, j, s: (s, j))],
        out_specs=pl.BlockSpec((None, tka, tnb), lambda i, j, s: (j // nj, i, j % nj)),
        out_shape=jax.ShapeDtypeStruct((G, Ka, Nb), out_dtype),
        scratch_shapes=[pltpu.VMEM((tka, tnb), F32)],
        compiler_params=_params(3),
    )(a, b)


def mla_pre_fwd(a, gqa, gkva, wuq3, wukv3, gqn, gkn, cos_t, sin_t):
    S = a.shape[0]
    tm = min(TM, S)

    def body(a_ref, gqa_ref, gkva_ref, wuq_ref, wukv_ref, gqn_ref, gkn_ref, cos_ref, sin_ref,
             q_ref, k_ref, v_ref, cq_ref, ckv_ref, cqs, ckvs):
        @pl.when(pl.program_id(1) == 0)
        def _():
            av = a_ref[...]
            cq = (_rms(av[:, :Q_LORA], Q_LORA)[0] * gqa_ref[...]).astype(cqs.dtype)
            ckv = (_rms(av[:, Q_LORA:Q_LORA + KV_LORA], KV_LORA)[0] * gkva_ref[...]).astype(ckvs.dtype)
            cqs[...] = cq
            cq_ref[...] = cq
            ckvs[...] = ckv
            ckv_ref[...] = ckv

        cos_v, sin_v = cos_ref[...], sin_ref[...]
        qn = _rms(_dot(cqs[...], wuq_ref[...]), QK_DIM)[0] * gqn_ref[...]
        q_ref[...] = jnp.concatenate([qn[:, :NOPE], _rope(qn[:, NOPE:], cos_v, sin_v)], axis=1).astype(q_ref.dtype)
        kvp = _dot(ckvs[...], wukv_ref[...])
        kk = jnp.concatenate([kvp[:, :NOPE], a_ref[:, Q_LORA + KV_LORA:]], axis=1)
        kn = _rms(kk, QK_DIM)[0] * gkn_ref[...]
        k_ref[...] = jnp.concatenate([kn[:, :NOPE], _rope(kn[:, NOPE:], cos_v, sin_v)], axis=1).astype(k_ref.dtype)
        v_ref[...] = kvp[:, NOPE:].astype(v_ref.dtype)

    row = lambda w: pl.BlockSpec((tm, w), lambda i, h: (i, 0))
    vec = lambda w: pl.BlockSpec((1, w), lambda i, h: (0, 0))
    head = lambda w: pl.BlockSpec((None, tm, w), lambda i, h: (h, i, 0))
    return pl.pallas_call(
        body, name="mla_pre_fwd", grid=(S // tm, N_HEADS),
        in_specs=[row(DOWN_PAD), vec(Q_LORA), vec(KV_LORA),
                  pl.BlockSpec((None, Q_LORA, QK_PAD), lambda i, h: (h, 0, 0)),
                  pl.BlockSpec((None, KV_LORA, NOPE + V_DIM), lambda i, h: (h, 0, 0)),
                  vec(QK_PAD), vec(QK_PAD), row(128), row(128)],
        out_specs=[head(QK_PAD), head(QK_PAD), head(V_DIM), row(Q_LORA), row(KV_LORA)],
        out_shape=[jax.ShapeDtypeStruct((N_HEADS, S, QK_PAD), MXU_DTYPE),
                   jax.ShapeDtypeStruct((N_HEADS, S, QK_PAD), MXU_DTYPE),
                   jax.ShapeDtypeStruct((N_HEADS, S, V_DIM), MXU_DTYPE),
                   jax.ShapeDtypeStruct((S, Q_LORA), MXU_DTYPE),
                   jax.ShapeDtypeStruct((S, KV_LORA), MXU_DTYPE)],
        scratch_shapes=[pltpu.VMEM((tm, Q_LORA), MXU_DTYPE), pltpu.VMEM((tm, KV_LORA), MXU_DTYPE)],
        compiler_params=_params(2),
    )(a, gqa, gkva, wuq3, wukv3, gqn, gkn, cos_t, sin_t)


def mla_pre_bwd(dq, dk, dv, a, cq, ckv, gqa, gkva, wuq3, wukv3, gqn, gkn, cos_t, sin_t):
    S = a.shape[0]
    tm = min(TM, S)
    H = N_HEADS

    def body(dq_ref, dk_ref, dv_ref, a_ref, cq_ref, ckv_ref, gqa_ref, gkva_ref, wuq_ref, wukv_ref, gqn_ref, gkn_ref,
             cos_ref, sin_ref, da_ref, dwuq_ref, dwukv_ref, dgqn_ref, dgkn_ref, dgqa_ref, dgkva_ref,
             dcq, dckv, dkpe):
        i, h = pl.program_id(0), pl.program_id(1)

        @pl.when((i == 0) & (h == 0))
        def _():
            for ref in (dwuq_ref, dwukv_ref, dgqn_ref, dgkn_ref, dgqa_ref, dgkva_ref):
                ref[...] = jnp.zeros_like(ref)

        @pl.when(h == 0)
        def _():
            dcq[...] = jnp.zeros_like(dcq)
            dckv[...] = jnp.zeros_like(dckv)
            dkpe[...] = jnp.zeros_like(dkpe)

        cos_v, sin_v = cos_ref[...], sin_ref[...]
        cqv, ckvv = cq_ref[...], ckv_ref[...]
        wuq, wukv = wuq_ref[...], wukv_ref[...]

        qhat, rq = _rms(_dot(cqv, wuq), QK_DIM)
        dqr = dq_ref[...]
        dqn = jnp.concatenate([dqr[:, :NOPE], _rope_bwd(dqr[:, NOPE:], cos_v, sin_v)], axis=1)
        dqp, dg = _rms_bwd(dqn, qhat, rq, gqn_ref[...], QK_DIM)
        dgqn_ref[...] += dg
        dqp = dqp.astype(MXU_DTYPE)
        dwuq_ref[h] += _dot(cqv, dqp, TN)
        dcq[...] += _dot(dqp, wuq, NT)

        kvp = _dot(ckvv, wukv)
        kk = jnp.concatenate([kvp[:, :NOPE], a_ref[:, Q_LORA + KV_LORA:]], axis=1)
        khat, rk = _rms(kk, QK_DIM)
        dkr = dk_ref[...]
        dkn = jnp.concatenate([dkr[:, :NOPE], _rope_bwd(dkr[:, NOPE:], cos_v, sin_v)], axis=1)
        dkk, dg = _rms_bwd(dkn, khat, rk, gkn_ref[...], QK_DIM)
        dgkn_ref[...] += dg
        dkpe[...] += dkk[:, NOPE:]
        dkvp = jnp.concatenate([dkk[:, :NOPE], dv_ref[...]], axis=1).astype(MXU_DTYPE)
        dwukv_ref[h] += _dot(ckvv, dkvp, TN)
        dckv[...] += _dot(dkvp, wukv, NT)

        @pl.when(h == H - 1)
        def _():
            av = a_ref[...]
            ahat, r = _rms(av[:, :Q_LORA], Q_LORA)
            daq, dg = _rms_bwd(dcq[...], ahat, r, gqa_ref[...], Q_LORA)
            dgqa_ref[...] += dg
            ahat, r = _rms(av[:, Q_LORA:Q_LORA + KV_LORA], KV_LORA)
            dakv, dg = _rms_bwd(dckv[...], ahat, r, gkva_ref[...], KV_LORA)
            dgkva_ref[...] += dg
            da_ref[...] = jnp.concatenate([daq, dakv, dkpe[...]], axis=1)

    row = lambda w: pl.BlockSpec((tm, w), lambda i, h: (i, 0))
    vec = lambda w: pl.BlockSpec((1, w), lambda i, h: (0, 0))
    head = lambda w: pl.BlockSpec((None, tm, w), lambda i, h: (h, i, 0))
    full3 = lambda s: pl.BlockSpec(s, lambda i, h: (0, 0, 0))
    return pl.pallas_call(
        body, name="mla_pre_bwd", grid=(S // tm, H),
        in_specs=[head(QK_PAD), head(QK_PAD), head(V_DIM), row(DOWN_PAD), row(Q_LORA), row(KV_LORA),
                  vec(Q_LORA), vec(KV_LORA),
                  pl.BlockSpec((None, Q_LORA, QK_PAD), lambda i, h: (h, 0, 0)),
                  pl.BlockSpec((None, KV_LORA, NOPE + V_DIM), lambda i, h: (h, 0, 0)),
                  vec(QK_PAD), vec(QK_PAD), row(128), row(128)],
        out_specs=[row(DOWN_PAD), full3((H, Q_LORA, QK_PAD)), full3((H, KV_LORA, NOPE + V_DIM)),
                   vec(QK_PAD), vec(QK_PAD), vec(Q_LORA), vec(KV_LORA)],
        out_shape=[jax.ShapeDtypeStruct((S, DOWN_PAD), F32),
                   jax.ShapeDtypeStruct((H, Q_LORA, QK_PAD), F32),
                   jax.ShapeDtypeStruct((H, KV_LORA, NOPE + V_DIM), F32),
                   jax.ShapeDtypeStruct((1, QK_PAD), F32), jax.ShapeDtypeStruct((1, QK_PAD), F32),
                   jax.ShapeDtypeStruct((1, Q_LORA), F32), jax.ShapeDtypeStruct((1, KV_LORA), F32)],
        scratch_shapes=[pltpu.VMEM((tm, Q_LORA), F32), pltpu.VMEM((tm, KV_LORA), F32), pltpu.VMEM((tm, 128), F32)],
        compiler_params=_params(2),
    )(dq, dk, dv, a, cq, ckv, gqa, gkva, wuq3, wukv3, gqn, gkn, cos_t, sin_t)


def _scores(q, k, pc, pr):
    s = _dot(q, k, NT) * (QK_DIM ** -0.5)
    return jnp.where(pc >= pr, s, jnp.finfo(F32).min)


def attn_fwd(q, k, v, pos_col, pos_row):
    H, S, _ = q.shape
    t = min(TQ, S)
    nb = S // t

    def body(q_ref, k_ref, v_ref, pc_ref, pr_ref, o_ref, lse_ref, m_s, l_s, acc):
        qi, ki = pl.program_id(1), pl.program_id(2)

        @pl.when(ki == 0)
        def _():
            m_s[...] = jnp.full_like(m_s, -jnp.inf)
            l_s[...] = jnp.zeros_like(l_s)
            acc[...] = jnp.zeros_like(acc)

        @pl.when(ki <= qi)
        def _():
            s = _scores(q_ref[...], k_ref[...], pc_ref[...], pr_ref[...])
            m_old = m_s[...]
            m_new = jnp.maximum(m_old, jnp.max(s, axis=-1, keepdims=True))
            p = jnp.exp(s - m_new)
            alpha = jnp.exp(m_old - m_new)
            l_s[...] = alpha * l_s[...] + jnp.sum(p, axis=-1, keepdims=True)
            acc[...] = alpha * acc[...] + _dot(p, v_ref[...])
            m_s[...] = m_new

        @pl.when(ki == qi)
        def _():
            o_ref[...] = acc[...] / l_s[...]
            lse_ref[...] = m_s[...] + jnp.log(l_s[...])

    kv_idx = lambda h, qi, ki: (h, jnp.minimum(ki, qi), 0)
    return pl.pallas_call(
        body, name="attn_fwd", grid=(H, nb, nb),
        in_specs=[pl.BlockSpec((None, t, QK_PAD), lambda h, qi, ki: (h, qi, 0)),
                  pl.BlockSpec((None, t, QK_PAD), kv_idx),
                  pl.BlockSpec((None, t, V_DIM), kv_idx),
                  pl.BlockSpec((t, 1), lambda h, qi, ki: (qi, 0)),
                  pl.BlockSpec((1, t), lambda h, qi, ki: (0, jnp.minimum(ki, qi)))],
        out_specs=[pl.BlockSpec((t, V_DIM), lambda h, qi, ki: (qi, h)),
                   pl.BlockSpec((None, t, 1), lambda h, qi, ki: (h, qi, 0))],
        out_shape=[jax.ShapeDtypeStruct((S, H * V_DIM), F32), jax.ShapeDtypeStruct((H, S, 1), F32)],
        scratch_shapes=[pltpu.VMEM((t, 1), F32), pltpu.VMEM((t, 1), F32), pltpu.VMEM((t, V_DIM), F32)],
        compiler_params=_params(3),
    )(q, k, v, pos_col, pos_row)


def attn_bwd(q, k, v, o, do, lse, pos_col, pos_row):
    H, S, _ = q.shape
    t = min(TQ, S)
    nb = S // t

    def body(q_ref, k_ref, v_ref, o_ref, do_ref, lse_ref, pc_ref, pr_ref, dq_ref, dk_ref, dv_ref):
        kj, qi = pl.program_id(1), pl.program_id(2)

        @pl.when((kj == 0) & (qi == 0))
        def _():
            dq_ref[...] = jnp.zeros_like(dq_ref)

        @pl.when(qi == 0)
        def _():
            dk_ref[...] = jnp.zeros_like(dk_ref)
            dv_ref[...] = jnp.zeros_like(dv_ref)

        @pl.when(qi >= kj)
        def _():
            qv, kv_, dov = q_ref[...], k_ref[...], do_ref[...]
            p = jnp.exp(_scores(qv, kv_, pc_ref[...], pr_ref[...]) - lse_ref[...])
            dob = dov.astype(MXU_DTYPE)
            dv_ref[...] += _dot(p, dob, TN)
            dp = _dot(dob, v_ref[...], NT)
            delta = jnp.sum(dov * o_ref[...], axis=-1, keepdims=True)
            ds = (p * (dp - delta) * (QK_DIM ** -0.5)).astype(MXU_DTYPE)
            dk_ref[...] += _dot(ds, qv, TN)
            rows = pl.ds(pl.multiple_of(qi * t, t), t)
            dq_ref[rows, :] += _dot(ds, kv_)

    q_idx = lambda h, kj, qi: (h, jnp.maximum(qi, kj), 0)
    k_idx = lambda h, kj, qi: (h, kj, 0)
    o_idx = lambda h, kj, qi: (jnp.maximum(qi, kj), h)
    return pl.pallas_call(
        body, name="attn_bwd", grid=(H, nb, nb),
        in_specs=[pl.BlockSpec((None, t, QK_PAD), q_idx),
                  pl.BlockSpec((None, t, QK_PAD), k_idx),
                  pl.BlockSpec((None, t, V_DIM), k_idx),
                  pl.BlockSpec((t, V_DIM), o_idx),
                  pl.BlockSpec((t, V_DIM), o_idx),
                  pl.BlockSpec((None, t, 1), q_idx),
                  pl.BlockSpec((t, 1), lambda h, kj, qi: (jnp.maximum(qi, kj), 0)),
                  pl.BlockSpec((1, t), lambda h, kj, qi: (0, kj))],
        out_specs=[pl.BlockSpec((None, S, QK_PAD), lambda h, kj, qi: (h, 0, 0)),
                   pl.BlockSpec((None, t, QK_PAD), k_idx),
                   pl.BlockSpec((None, t, V_DIM), k_idx)],
        out_shape=[jax.ShapeDtypeStruct((H, S, QK_PAD), F32), jax.ShapeDtypeStruct((H, S, QK_PAD), F32),
                   jax.ShapeDtypeStruct((H, S, V_DIM), F32)],
        compiler_params=_params(3),
    )(q, k, v, o, do, lse, pos_col, pos_row)


def _conv_specs(S, tr, tc):
    nc = D_MODEL // tc
    hb = tr // 8
    main = lambda third: pl.BlockSpec((tr, tc), lambda c, r: (r, third * nc + c))
    prev = lambda third: pl.BlockSpec((8, tc), lambda c, r: (jnp.maximum(r * hb - 1, 0), third * nc + c))
    nxt = lambda third: pl.BlockSpec((8, tc), lambda c, r: (jnp.minimum((r + 1) * hb, S // 8 - 1), third * nc + c))
    return main, prev, nxt


def _conv_taps(gc, uu, w_ref, first):
    u2 = gc * uu
    rows = lax.broadcasted_iota(jnp.int32, u2.shape, 0)
    u2 = jnp.where((rows < 8) & first, 0.0, u2)
    s1 = pltpu.roll(u2, 1, 0)
    s2 = pltpu.roll(u2, 2, 0)
    u3 = w_ref[2:3, :] * u2 + w_ref[1:2, :] * s1 + w_ref[0:1, :] * s2
    return u2, s1, s2, u3


def conv_fwd(bcu, cw):
    S = bcu.shape[0]
    tr, tc = min(TROW, S), TCH
    main, prev, _ = _conv_specs(S, tr, tc)

    def body(gb_ref, gc_ref, u_ref, gch_ref, uh_ref, w_ref, z_ref):
        gc = jnp.concatenate([gch_ref[...], gc_ref[...]], axis=0)
        uu = jnp.concatenate([uh_ref[...], u_ref[...]], axis=0)
        u3 = _conv_taps(gc, uu, w_ref, pl.program_id(1) == 0)[3]
        z_ref[...] = (gb_ref[...] * u3[8:]).astype(z_ref.dtype)

    return pl.pallas_call(
        body, name="conv_fwd", grid=(D_MODEL // tc, S // tr),
        in_specs=[main(0), main(1), main(2), prev(1), prev(2), pl.BlockSpec((3, tc), lambda c, r: (0, c))],
        out_specs=pl.BlockSpec((tr, tc), lambda c, r: (r, c)),
        out_shape=jax.ShapeDtypeStruct((S, D_MODEL), MXU_DTYPE),
        compiler_params=_params(2),
    )(bcu, bcu, bcu, bcu, bcu, cw)


def conv_bwd(dz, bcu, cw):
    S = bcu.shape[0]
    tr, tc = min(TROW, S), TCH
    nr = S // tr
    main, prev, nxt = _conv_specs(S, tr, tc)

    def body(dz_ref, dzn_ref, gb_ref, gbn_ref, gc_ref, u_ref, gch_ref, uh_ref, w_ref,
             dgb_ref, dgc_ref, du_ref, dw_ref):
        r = pl.program_id(1)
        gcv, uv = gc_ref[...], u_ref[...]
        gc = jnp.concatenate([gch_ref[...], gcv], axis=0)
        uu = jnp.concatenate([uh_ref[...], uv], axis=0)
        u2, s1, s2, u3 = _conv_taps(gc, uu, w_ref, r == 0)
        dzv = dz_ref[...]
        du3 = jnp.concatenate([dzv * gb_ref[...], dzn_ref[...] * gbn_ref[...]], axis=0)
        rows = lax.broadcasted_iota(jnp.int32, du3.shape, 0)
        du3 = jnp.where((rows >= tr) & (r == nr - 1), 0.0, du3)
        n1 = pltpu.roll(du3, tr + 8 - 1, 0)
        n2 = pltpu.roll(du3, tr + 8 - 2, 0)
        du2 = (w_ref[2:3, :] * du3 + w_ref[1:2, :] * n1 + w_ref[0:1, :] * n2)[:tr]
        dgb_ref[...] = (dzv * u3[8:]).astype(dgb_ref.dtype)
        dgc_ref[...] = (du2 * uv).astype(dgc_ref.dtype)
        du_ref[...] = (du2 * gcv).astype(du_ref.dtype)
        d3 = du3[:tr]
        taps = [jnp.sum(d3 * t[8:], axis=0, keepdims=True) for t in (s2, s1, u2)]

        @pl.when(r == 0)
        def _():
            for kk in range(3):
                dw_ref[kk:kk + 1, :] = taps[kk]

        @pl.when(r != 0)
        def _():
            for kk in range(3):
                dw_ref[kk:kk + 1, :] += taps[kk]

    out = pl.BlockSpec((tr, tc), lambda c, r: (r, c))
    nxt_dz = pl.BlockSpec((8, tc), lambda c, r: (jnp.minimum((r + 1) * (tr // 8), S // 8 - 1), c))
    act = jax.ShapeDtypeStruct((S, D_MODEL), MXU_DTYPE)
    return pl.pallas_call(
        body, name="conv_bwd", grid=(D_MODEL // tc, nr),
        in_specs=[out, nxt_dz, main(0), nxt(0), main(1), main(2), prev(1), prev(2),
                  pl.BlockSpec((3, tc), lambda c, r: (0, c))],
        out_specs=[out, out, out, pl.BlockSpec((3, tc), lambda c, r: (0, c))],
        out_shape=[act, act, act, jax.ShapeDtypeStruct((3, D_MODEL), F32)],
        compiler_params=_params(2),
    )(dz, dz, bcu, bcu, bcu, bcu, bcu, bcu, cw)


def loss_head(y, target):
    S, D = y.shape
    tm = min(TM, S)

    def body(y_ref, t_ref, l_ref, dy_ref):
        err = y_ref[...] - t_ref[...]
        dy_ref[...] = err / D
        part = jnp.full((1, 128), jnp.sum(err * err), F32)

        @pl.when(pl.program_id(0) == 0)
        def _():
            l_ref[...] = part

        @pl.when(pl.program_id(0) != 0)
        def _():
            l_ref[...] += part

    blk = pl.BlockSpec((tm, D), lambda i: (i, 0))
    return pl.pallas_call(
        body, name="loss_head", grid=(S // tm,),
        in_specs=[blk, blk],
        out_specs=[pl.BlockSpec((1, 128), lambda i: (0, 0)), blk],
        out_shape=[jax.ShapeDtypeStruct((1, 128), F32), jax.ShapeDtypeStruct((S, D), F32)],
        compiler_params=_params(1),
    )(y, target)


def adamw(parts, w, m, v, *, name):
    R, C = w.shape
    tr = R
    while tr * C * 4 > (1 << 20) and tr % 32 == 0:
        tr //= 2

    def body(p_ref, w_ref, m_ref, v_ref, g_ref, d_ref, mo_ref, vo_ref):
        g = p_ref[0].astype(F32)
        for d in range(1, N_DEV):
            g = g + p_ref[d].astype(F32)
        m_new = ADAM_B1 * m_ref[...] + (1.0 - ADAM_B1) * g
        v_new = ADAM_B2 * v_ref[...] + (1.0 - ADAM_B2) * (g * g)
        m_hat = m_new / (1.0 - ADAM_B1 ** ADAM_STEP)
        v_hat = v_new / (1.0 - ADAM_B2 ** ADAM_STEP)
        g_ref[...] = g
        d_ref[...] = -ADAM_LR * (m_hat / (jnp.sqrt(v_hat) + ADAM_EPS) + ADAM_WD * w_ref[...])
        mo_ref[...] = m_new
        vo_ref[...] = v_new

    blk = pl.BlockSpec((tr, C), lambda i: (i, 0))
    return pl.pallas_call(
        body, name=name, grid=(R // tr,),
        in_specs=[pl.BlockSpec((N_DEV, tr, C), lambda i: (0, i, 0)), blk, blk, blk],
        out_specs=[blk, blk, blk, blk],
        out_shape=[jax.ShapeDtypeStruct((R, C), F32)] * 4,
        compiler_params=_params(1),
    )(parts, w, m, v)


def _mesh_place():
    x, y, c = (lax.axis_index(n) for n in MESH_AXES)
    return x, y, c, 4 * x + 2 * y + c


def _peer(x, y, c, d):
    px = 1 - x if d & 4 else x
    py = 1 - y if d & 2 else y
    pc = 1 - c if d & 1 else c
    return (px, py, pc), 4 * px + 2 * py + pc


def _exchange(srcs, name, scatter):
    n = len(srcs)
    any_spec = pl.BlockSpec(memory_space=pl.ANY)

    def body(*refs):
        ins, outs = refs[:n], refs[n:2 * n]
        send_sems, recv_sems, local_sems = refs[2 * n:]
        x, y, c, me = _mesh_place()
        for a in range(n):
            mine = ins[a].at[me] if scatter else ins[a]
            pltpu.make_async_copy(mine, outs[a].at[me], local_sems.at[a]).start()
            for d in range(1, N_DEV):
                peer, peer_lin = _peer(x, y, c, d)
                pltpu.make_async_remote_copy(
                    src_ref=ins[a].at[peer_lin] if scatter else ins[a], dst_ref=outs[a].at[me],
                    send_sem=send_sems.at[a], recv_sem=recv_sems.at[a],
                    device_id=peer, device_id_type=pl.DeviceIdType.MESH).start()
        for a in range(n):
            mine = ins[a].at[me] if scatter else ins[a]
            pltpu.make_async_copy(mine, outs[a].at[me], local_sems.at[a]).wait()
            seven = outs[a].at[pl.ds(0, N_DEV - 1)]
            drain = pltpu.make_async_remote_copy(
                src_ref=seven, dst_ref=seven, send_sem=send_sems.at[a], recv_sem=recv_sems.at[a],
                device_id=(x, y, c), device_id_type=pl.DeviceIdType.MESH)
            drain.wait_send()
            drain.wait_recv()

    block = (lambda s: s.shape[1:]) if scatter else (lambda s: s.shape)
    return pl.pallas_call(
        body, name=name,
        in_specs=[any_spec] * n, out_specs=[any_spec] * n,
        out_shape=[jax.ShapeDtypeStruct((N_DEV,) + tuple(block(s)), s.dtype) for s in srcs],
        scratch_shapes=[pltpu.SemaphoreType.DMA((n,)), pltpu.SemaphoreType.DMA((n,)), pltpu.SemaphoreType.DMA((n,))],
    )(*srcs)


def _rope_tables(pos):
    inv_freq = ROPE_THETA ** (-jnp.arange(0, ROPE, 2, dtype=F32) / ROPE)
    ang = pos.astype(F32)[:, None] * inv_freq
    cos, sin = jnp.cos(ang), jnp.sin(ang)
    pad = jnp.zeros((pos.shape[0], 128 - ROPE), F32)
    return jnp.concatenate([cos, cos, pad + 1.0], axis=1), jnp.concatenate([-sin, sin, pad], axis=1)


def _pad_last(w, n):
    return jnp.pad(w, [(0, 0)] * (w.ndim - 1) + [(0, n - w.shape[-1])])


def kernel(x, positions, g_mix, g_mlp, attn_w_down, attn_g_q_a, attn_g_kv_a, attn_w_uq, attn_w_ukv, attn_g_qnorm, attn_g_knorm, attn_w_o, conv_w_in, conv_w, conv_w_out, mlp_w1, mlp_w2, loss_target, m_g_mix, m_g_mlp, m_attn_w_down, m_attn_g_q_a, m_attn_g_kv_a, m_attn_w_uq, m_attn_w_ukv, m_attn_g_qnorm, m_attn_g_knorm, m_attn_w_o, m_conv_w_in, m_conv_w, m_conv_w_out, m_mlp_w1, m_mlp_w2, v_g_mix, v_g_mlp, v_attn_w_down, v_attn_g_q_a, v_attn_g_kv_a, v_attn_w_uq, v_attn_w_ukv, v_attn_g_qnorm, v_attn_g_knorm, v_attn_w_o, v_conv_w_in, v_conv_w, v_conv_w_out, v_mlp_w1, v_mlp_w2):
    weights = dict(g_mix=g_mix, g_mlp=g_mlp, attn_w_down=attn_w_down, attn_g_q_a=attn_g_q_a, attn_g_kv_a=attn_g_kv_a,
                   attn_w_uq=attn_w_uq, attn_w_ukv=attn_w_ukv, attn_g_qnorm=attn_g_qnorm, attn_g_knorm=attn_g_knorm,
                   attn_w_o=attn_w_o, conv_w_in=conv_w_in, conv_w=conv_w, conv_w_out=conv_w_out, mlp_w1=mlp_w1, mlp_w2=mlp_w2)
    mom1 = dict(g_mix=m_g_mix, g_mlp=m_g_mlp, attn_w_down=m_attn_w_down, attn_g_q_a=m_attn_g_q_a, attn_g_kv_a=m_attn_g_kv_a,
                attn_w_uq=m_attn_w_uq, attn_w_ukv=m_attn_w_ukv, attn_g_qnorm=m_attn_g_qnorm, attn_g_knorm=m_attn_g_knorm,
                attn_w_o=m_attn_w_o, conv_w_in=m_conv_w_in, conv_w=m_conv_w, conv_w_out=m_conv_w_out, mlp_w1=m_mlp_w1, mlp_w2=m_mlp_w2)
    mom2 = dict(g_mix=v_g_mix, g_mlp=v_g_mlp, attn_w_down=v_attn_w_down, attn_g_q_a=v_attn_g_q_a, attn_g_kv_a=v_attn_g_kv_a,
                attn_w_uq=v_attn_w_uq, attn_w_ukv=v_attn_w_ukv, attn_g_qnorm=v_attn_g_qnorm, attn_g_knorm=v_attn_g_knorm,
                attn_w_o=v_attn_w_o, conv_w_in=v_conv_w_in, conv_w=v_conv_w, conv_w_out=v_conv_w_out, mlp_w1=v_mlp_w1, mlp_w2=v_mlp_w2)
    big = ["attn_w_down", "attn_w_uq", "attn_w_ukv", "attn_w_o", "conv_w_in", "conv_w", "conv_w_out", "mlp_w1", "mlp_w2"]
    small = ["g_mix", "g_mlp", "attn_g_q_a", "attn_g_kv_a", "attn_g_qnorm", "attn_g_knorm"]
    order = ["g_mix", "g_mlp", "attn_w_down", "attn_g_q_a", "attn_g_kv_a", "attn_w_uq", "attn_w_ukv", "attn_g_qnorm",
             "attn_g_knorm", "attn_w_o", "conv_w_in", "conv_w", "conv_w_out", "mlp_w1", "mlp_w2"]

    xs = x[0]
    pos = positions[0]
    target = loss_target[0]
    S = xs.shape[0]
    depth = g_mix.shape[0]
    cos_t, sin_t = _rope_tables(pos)
    pos_col, pos_row = pos.reshape(S, 1), pos.reshape(1, S)

    keys, shards = [], []
    for name in big:
        for l in range(weights[name].shape[0]):
            keys.append((name, l))
            shards.append(weights[name][l] if name == "conv_w" else weights[name][l].astype(WIRE_DTYPE))
    full = dict(zip(keys, _exchange(shards, "gather_weights", scatter=False)))

    def rows(name, l):
        g = full[(name, l)]
        return g.reshape(g.shape[0] * g.shape[1], g.shape[2])

    saved = []
    for i in range(depth):
        l = i // 2
        rec = {"x0": xs}
        if i % 2 == 0:
            wd3 = _pad_last(rows("attn_w_down", l), DOWN_PAD)[None]
            wuq3 = _pad_last(full[("attn_w_uq", l)], QK_PAD)
            wukv3 = full[("attn_w_ukv", l)]
            gqn = _pad_last(attn_g_qnorm[l][None], QK_PAD)
            gkn = _pad_last(attn_g_knorm[l][None], QK_PAD)
            gqa, gkva = attn_g_q_a[l][None], attn_g_kv_a[l][None]
            h, a = rms_mm(xs, g_mix[i], wd3, name="attn_down")
            q, k, v, cq, ckv = mla_pre_fwd(a, gqa, gkva, wuq3, wukv3, gqn, gkn, cos_t, sin_t)
            o, lse = attn_fwd(q, k, v, pos_col, pos_row)
            x1 = mm_res(o, rows("attn_w_o", l), xs, name="attn_out")
            rec.update(h=h, a=a, q=q, k=k, v=v, cq=cq, ckv=ckv, o=o, lse=lse, wd3=wd3, wuq3=wuq3, wukv3=wukv3,
                       gqn=gqn, gkn=gkn, gqa=gqa, gkva=gkva)
        else:
            cw = full[("conv_w", l)].transpose(1, 0, 2).reshape(3, D_MODEL)
            h, bcu = rms_mm(xs, g_mix[i], full[("conv_w_in", l)], name="conv_in")
            z = conv_fwd(bcu, cw)
            x1 = mm_res(z, rows("conv_w_out", l), xs, name="conv_out")
            rec.update(h=h, bcu=bcu, z=z, cw=cw)
        h2, u, act = rms_mm(x1, g_mlp[i], full[("mlp_w1", i)], name="mlp_up", mlp=True)
        xs = mm_res(act, rows("mlp_w2", i), x1, name="mlp_down")
        rec.update(x1=x1, h2=h2, u=u, act=act)
        saved.append(rec)

    sq, dx = loss_head(xs, target)
    loss = lax.psum(sq[0, 0] * (0.5 / D_MODEL), MESH_AXES)

    grads = {name: [None] * weights[name].shape[0] for name in order}
    for i in reversed(range(depth)):
        l = i // 2
        rec = saved[i]
        grads["mlp_w2"][i] = mm_tn(rec["act"], dx, name="mlp_down_dw", G=1, out_dtype=WIRE_DTYPE).reshape(N_DEV, -1, D_MODEL)
        du = mm_nt(dx, rows("mlp_w2", i)[None], name="mlp_down_dx", epi="mlp_du", u=rec["u"])
        grads["mlp_w1"][i] = mm_tn(rec["h2"], du, name="mlp_up_dw", G=N_DEV, out_dtype=WIRE_DTYPE)
        dx1, dg = mm_nt(du, full[("mlp_w1", i)], name="mlp_up_dx", epi="rms_bwd", x=rec["x1"], g=g_mlp[i], dx=dx)
        grads["g_mlp"][i] = dg[0]
        if i % 2 == 0:
            grads["attn_w_o"][l] = mm_tn(rec["o"], dx1, name="attn_out_dw", G=1, out_dtype=WIRE_DTYPE).reshape(N_DEV, -1, D_MODEL)
            do = mm_nt(dx1, rows("attn_w_o", l)[None], name="attn_out_dx")
            dq, dk, dv = attn_bwd(rec["q"], rec["k"], rec["v"], rec["o"], do, rec["lse"], pos_col, pos_row)
            da, dwuq, dwukv, dgqn, dgkn, dgqa, dgkva = mla_pre_bwd(
                dq, dk, dv, rec["a"], rec["cq"], rec["ckv"], rec["gqa"], rec["gkva"], rec["wuq3"], rec["wukv3"],
                rec["gqn"], rec["gkn"], cos_t, sin_t)
            grads["attn_w_uq"][l] = dwuq[:, :, :QK_DIM].astype(WIRE_DTYPE)
            grads["attn_w_ukv"][l] = dwukv.astype(WIRE_DTYPE)
            grads["attn_g_qnorm"][l] = dgqn[0, :QK_DIM]
            grads["attn_g_knorm"][l] = dgkn[0, :QK_DIM]
            grads["attn_g_q_a"][l] = dgqa[0]
            grads["attn_g_kv_a"][l] = dgkva[0]
            dwd = mm_tn(rec["h"], da, name="attn_down_dw", G=1, out_dtype=WIRE_DTYPE)
            grads["attn_w_down"][l] = dwd[0, :, :DOWN].reshape(N_DEV, -1, DOWN)
            dx, dg = mm_nt(da, rec["wd3"], name="attn_down_dx", epi="rms_bwd", x=rec["x0"], g=g_mix[i], dx=dx1)
        else:
            grads["conv_w_out"][l] = mm_tn(rec["z"], dx1, name="conv_out_dw", G=1, out_dtype=WIRE_DTYPE).reshape(N_DEV, -1, D_MODEL)
            dz = mm_nt(dx1, rows("conv_w_out", l)[None], name="conv_out_dx")
            dgb, dgc, du_, dcw = conv_bwd(dz, rec["bcu"], rec["cw"])
            grads["conv_w"][l] = dcw.reshape(3, N_DEV, -1).transpose(1, 0, 2)
            dbcu = jnp.concatenate([dgb, dgc, du_], axis=1)
            grads["conv_w_in"][l] = mm_tn(rec["h"], dbcu, name="conv_in_dw", G=N_DEV, out_dtype=WIRE_DTYPE)
            dx, dg = mm_nt(dbcu, full[("conv_w_in", l)], name="conv_in_dx", epi="rms_bwd", x=rec["x0"], g=g_mix[i], dx=dx1)
        grads["g_mix"][i] = dg[0]

    sizes = [weights[name].size for name in small]
    n_small = sum(sizes)
    rows_small = -(-n_small // (8 * 128)) * 8

    def pack(tree):
        flat = jnp.concatenate([jnp.stack(tree[name]).reshape(-1) if isinstance(tree[name], list) else tree[name].reshape(-1)
                                for name in small])
        return jnp.pad(flat, (0, rows_small * 128 - n_small)).reshape(rows_small, 128)

    sends = [jnp.stack(grads[name], axis=1) for name in big]
    sends.append(jnp.broadcast_to(pack(grads)[None], (N_DEV, rows_small, 128)))
    parts = _exchange(sends, "scatter_grads", scatter=True)

    out = {}
    for name, part in zip(big, parts[:-1]):
        w = weights[name]
        flat = lambda t: t.reshape(-1, t.shape[-1])
        res = adamw(part.reshape(N_DEV, -1, w.shape[-1]), flat(w), flat(mom1[name]), flat(mom2[name]), name="adamw_" + name)
        out[name] = [r.reshape(w.shape) for r in res]
    res = adamw(parts[-1], pack(weights), pack(mom1), pack(mom2), name="adamw_gains")
    offset = 0
    for name, size in zip(small, sizes):
        out[name] = [r.reshape(-1)[offset:offset + size].reshape(weights[name].shape) for r in res]
        offset += size

    return (loss, dx[None], *[out[n][0] for n in order], *[out[n][1] for n in order],
            *[out[n][2] for n in order], *[out[n][3] for n in order])
```

```python
import jax
import jax.numpy as jnp
import numpy as np
from jax import lax
from jax.experimental import pallas as pl
from jax.experimental.pallas import tpu as pltpu

F32 = jnp.float32
MXU_DTYPE = jnp.bfloat16
WIRE_DTYPE = jnp.bfloat16

D_MODEL = 1024
N_HEADS = 8
NOPE = 128
ROPE = 64
QK_DIM = NOPE + ROPE
QK_PAD = 256
V_DIM = 128
Q_LORA = 256
KV_LORA = 128
DOWN = Q_LORA + KV_LORA + ROPE
DOWN_PAD = 512
ROPE_THETA = 10000.0
EPS = 1e-6
ADAM_LR, ADAM_B1, ADAM_B2, ADAM_EPS, ADAM_WD, ADAM_STEP = 0.001, 0.9, 0.999, 1e-08, 0.01, 10
N_DEV = 8
MESH_AXES = ("x", "y", "c")

TM = 512
TM_WIDE = 1024
TQ = 512
HEADS_FWD = 2
TROW = 512
TCH = 512
VMEM_LIMIT = 48 << 20

NN = (((1,), (0,)), ((), ()))
NT = (((1,), (1,)), ((), ()))
TN = (((0,), (0,)), ((), ()))


def _dot(a, b, dims=NN):
    return lax.dot_general(a.astype(MXU_DTYPE), b.astype(MXU_DTYPE), dims, preferred_element_type=F32)


def _params(n_axes):
    return pltpu.CompilerParams(dimension_semantics=("arbitrary",) * n_axes, vmem_limit_bytes=VMEM_LIMIT)


def _rms(xv, n):
    r = lax.rsqrt(jnp.sum(xv * xv, axis=-1, keepdims=True) / n + EPS)
    return xv * r, r


def _rms_bwd(dy, xhat, r, g, n):
    dg = jnp.sum(dy * xhat, axis=0, keepdims=True)
    dxh = dy * g
    dx = r * (dxh - xhat * (jnp.sum(dxh * xhat, axis=-1, keepdims=True) / n))
    return dx, dg


def _swap_halves(t):
    lane = lax.broadcasted_iota(jnp.int32, t.shape, 1)
    return jnp.where(lane < ROPE // 2, pltpu.roll(t, 128 - ROPE // 2, 1), pltpu.roll(t, ROPE // 2, 1))


def _rope(t, cos_t, sin_t):
    return t * cos_t + _swap_halves(t) * sin_t


def _rope_bwd(dout, cos_t, sin_t):
    return dout * cos_t + _swap_halves(dout * sin_t)


def rms_mm(x, g, w3, *, name, mlp=False):
    S, D = x.shape
    G, _, Nb = w3.shape
    tm = min(TM_WIDE, S)
    tn = Nb if Nb <= 512 else 512
    nj = Nb // tn
    N = G * Nb

    def body(x_ref, g_ref, w_ref, h_ref, *rest):
        hs = rest[-1]

        @pl.when(pl.program_id(1) == 0)
        def _():
            xv = x_ref[...]
            r = lax.rsqrt(jnp.mean(xv * xv, axis=-1, keepdims=True) + EPS)
            h = (xv * r * g_ref[...]).astype(hs.dtype)
            hs[...] = h
            h_ref[...] = h

        acc = lax.dot_general(hs[...], w_ref[...].astype(hs.dtype), NN, preferred_element_type=F32)
        if mlp:
            rl = jnp.maximum(acc, 0.0)
            rest[0][...] = (rl * rl).astype(rest[0].dtype)
        else:
            rest[0][...] = acc

    out_shape = [jax.ShapeDtypeStruct((S, D), MXU_DTYPE), jax.ShapeDtypeStruct((S, N), MXU_DTYPE if mlp else F32)]
    out_specs = [pl.BlockSpec((tm, D), lambda i, j: (i, 0)), pl.BlockSpec((tm, tn), lambda i, j: (i, j))]
    return pl.pallas_call(
        body, name=name, grid=(S // tm, G * nj),
        in_specs=[pl.BlockSpec((tm, D), lambda i, j: (i, 0)),
                  pl.BlockSpec((1, D), lambda i, j: (0, 0)),
                  pl.BlockSpec((None, D, tn), lambda i, j: (j // nj, 0, j % nj))],
        out_specs=out_specs, out_shape=out_shape,
        scratch_shapes=[pltpu.VMEM((tm, D), MXU_DTYPE)],
        compiler_params=_params(2),
    )(x, g.reshape(1, D), w3)


def mm_res(a, w, res, *, name):
    S, K = a.shape
    _, N = w.shape
    tm = min(TM_WIDE, S)
    tk = min(K, 1024)
    nk = K // tk

    def body(a_ref, w_ref, r_ref, o_ref, acc):
        k = pl.program_id(1)

        @pl.when(k == 0)
        def _():
            acc[...] = jnp.zeros_like(acc)

        acc[...] += _dot(a_ref[...], w_ref[...])

        @pl.when(k == nk - 1)
        def _():
            o_ref[...] = r_ref[...] + acc[...]

    return pl.pallas_call(
        body, name=name, grid=(S // tm, nk),
        in_specs=[pl.BlockSpec((tm, tk), lambda i, k: (i, k)),
                  pl.BlockSpec((tk, N), lambda i, k: (k, 0)),
                  pl.BlockSpec((tm, N), lambda i, k: (i, 0))],
        out_specs=pl.BlockSpec((tm, N), lambda i, k: (i, 0)),
        out_shape=jax.ShapeDtypeStruct((S, N), F32),
        scratch_shapes=[pltpu.VMEM((tm, N), F32)],
        compiler_params=_params(2),
    )(a, w, res)


def mm_nt(a, w3, *, name, epi="plain", u=None, x=None, g=None, dx=None):
    S, N = a.shape
    G, Ko, Nb = w3.shape
    assert N == G * Nb
    tm = min(TM if epi == "rms_bwd" else TM_WIDE, S)
    tkk = Nb if Nb <= 1024 else 1024
    nk = Nb // tkk
    ks = G * nk
    tko = Ko if epi == "rms_bwd" else min(Ko, 512)

    def body(a_ref, w_ref, *rest):
        acc = rest[-1]
        i, k = pl.program_id(0), pl.program_id(2)

        @pl.when(k == 0)
        def _():
            acc[...] = jnp.zeros_like(acc)

        acc[...] += _dot(a_ref[...], w_ref[...], NT)

        @pl.when(k == ks - 1)
        def _():
            if epi == "plain":
                rest[0][...] = acc[...]
            elif epi == "mlp_du":
                u_ref, o_ref = rest[0], rest[1]
                o_ref[...] = (acc[...] * (2.0 * jnp.sqrt(u_ref[...].astype(F32)))).astype(o_ref.dtype)
            else:
                x_ref, g_ref, dx_ref, o_ref, dg_ref = rest[:5]
                xhat, r = _rms(x_ref[...], Ko)
                dxb, dg = _rms_bwd(acc[...], xhat, r, g_ref[...], Ko)
                o_ref[...] = dx_ref[...] + dxb

                @pl.when(i == 0)
                def _():
                    dg_ref[...] = dg

                @pl.when(i != 0)
                def _():
                    dg_ref[...] += dg

    in_specs = [pl.BlockSpec((tm, tkk), lambda i, j, k: (i, k)),
                pl.BlockSpec((None, tko, tkk), lambda i, j, k: (k // nk, j, k % nk))]
    args = [a, w3]
    tile = pl.BlockSpec((tm, tko), lambda i, j, k: (i, j))
    if epi == "plain":
        out_shape, out_specs = jax.ShapeDtypeStruct((S, Ko), F32), tile
    elif epi == "mlp_du":
        in_specs.append(tile)
        args.append(u)
        out_shape, out_specs = jax.ShapeDtypeStruct((S, Ko), MXU_DTYPE), tile
    else:
        vec = pl.BlockSpec((1, Ko), lambda i, j, k: (0, 0))
        in_specs += [tile, vec, tile]
        args += [x, g.reshape(1, Ko), dx]
        out_shape = [jax.ShapeDtypeStruct((S, Ko), F32), jax.ShapeDtypeStruct((1, Ko), F32)]
        out_specs = [tile, vec]
    return pl.pallas_call(
        body, name=name, grid=(S // tm, Ko // tko, ks),
        in_specs=in_specs, out_specs=out_specs, out_shape=out_shape,
        scratch_shapes=[pltpu.VMEM((tm, tko), F32)],
        compiler_params=_params(3),
    )(*args)


def mm_tn(a, b, *, name, G, out_dtype):
    S, Ka = a.shape
    _, N = b.shape
    Nb = N // G
    tm = min(TM, S)
    tka = min(Ka, 1024)
    tnb = Nb if Nb <= 1024 else 1024
    nj = Nb // tnb
    ns = S // tm

    def body(a_ref, b_ref, o_ref, acc):
        s = pl.program_id(2)

        @pl.when(s == 0)
        def _():
            acc[...] = jnp.zeros_like(acc)

        acc[...] += _dot(a_ref[...], b_ref[...], TN)

        @pl.when(s == ns - 1)
        def _():
            o_ref[...] = acc[...].astype(o_ref.dtype)

    return pl.pallas_call(
        body, name=name, grid=(Ka // tka, G * nj, ns),
        in_specs=[pl.BlockSpec((tm, tka), lambda i, j, s: (s, i)),
                  pl.BlockSpec((tm, tnb), lambda i, j, s: (s, j))],
        out_specs=pl.BlockSpec((None, tka, tnb), lambda i, j, s: (j // nj, i, j % nj)),
        out_shape=jax.ShapeDtypeStruct((G, Ka, Nb), out_dtype),
        scratch_shapes=[pltpu.VMEM((tka, tnb), F32)],
        compiler_params=_params(3),
    )(a, b)


def mla_pre_fwd(a, gqa, gkva, wuq3, wukv3, gqn, gkn, cos_t, sin_t):
    S = a.shape[0]
    tm = min(TM, S)

    def body(a_ref, gqa_ref, gkva_ref, wuq_ref, wukv_ref, gqn_ref, gkn_ref, cos_ref, sin_ref,
             q_ref, k_ref, v_ref, cq_ref, ckv_ref, cqs, ckvs):
        @pl.when(pl.program_id(1) == 0)
        def _():
            av = a_ref[...]
            cq = (_rms(av[:, :Q_LORA], Q_LORA)[0] * gqa_ref[...]).astype(cqs.dtype)
            ckv = (_rms(av[:, Q_LORA:Q_LORA + KV_LORA], KV_LORA)[0] * gkva_ref[...]).astype(ckvs.dtype)
            cqs[...] = cq
            cq_ref[...] = cq
            ckvs[...] = ckv
            ckv_ref[...] = ckv

        cos_v, sin_v = cos_ref[...], sin_ref[...]
        qn = _rms(_dot(cqs[...], wuq_ref[...]), QK_DIM)[0] * gqn_ref[...]
        q_ref[...] = jnp.concatenate([qn[:, :NOPE], _rope(qn[:, NOPE:], cos_v, sin_v)], axis=1).astype(q_ref.dtype)
        kvp = _dot(ckvs[...], wukv_ref[...])
        kk = jnp.concatenate([kvp[:, :NOPE], a_ref[:, Q_LORA + KV_LORA:]], axis=1)
        kn = _rms(kk, QK_DIM)[0] * gkn_ref[...]
        k_ref[...] = jnp.concatenate([kn[:, :NOPE], _rope(kn[:, NOPE:], cos_v, sin_v)], axis=1).astype(k_ref.dtype)
        v_ref[...] = kvp[:, NOPE:].astype(v_ref.dtype)

    row = lambda w: pl.BlockSpec((tm, w), lambda i, h: (i, 0))
    vec = lambda w: pl.BlockSpec((1, w), lambda i, h: (0, 0))
    head = lambda w: pl.BlockSpec((None, tm, w), lambda i, h: (h, i, 0))
    return pl.pallas_call(
        body, name="mla_pre_fwd", grid=(S // tm, N_HEADS),
        in_specs=[row(DOWN_PAD), vec(Q_LORA), vec(KV_LORA),
                  pl.BlockSpec((None, Q_LORA, QK_PAD), lambda i, h: (h, 0, 0)),
                  pl.BlockSpec((None, KV_LORA, NOPE + V_DIM), lambda i, h: (h, 0, 0)),
                  vec(QK_PAD), vec(QK_PAD), row(128), row(128)],
        out_specs=[head(QK_PAD), head(QK_PAD), head(V_DIM), row(Q_LORA), row(KV_LORA)],
        out_shape=[jax.ShapeDtypeStruct((N_HEADS, S, QK_PAD), MXU_DTYPE),
                   jax.ShapeDtypeStruct((N_HEADS, S, QK_PAD), MXU_DTYPE),
                   jax.ShapeDtypeStruct((N_HEADS, S, V_DIM), MXU_DTYPE),
                   jax.ShapeDtypeStruct((S, Q_LORA), MXU_DTYPE),
                   jax.ShapeDtypeStruct((S, KV_LORA), MXU_DTYPE)],
        scratch_shapes=[pltpu.VMEM((tm, Q_LORA), MXU_DTYPE), pltpu.VMEM((tm, KV_LORA), MXU_DTYPE)],
        compiler_params=_params(2),
    )(a, gqa, gkva, wuq3, wukv3, gqn, gkn, cos_t, sin_t)


def mla_pre_bwd(dq, dk, dv, a, cq, ckv, gqa, gkva, wuq3, wukv3, gqn, gkn, cos_t, sin_t):
    S = a.shape[0]
    tm = min(TM, S)
    H = N_HEADS

    def body(dq_ref, dk_ref, dv_ref, a_ref, cq_ref, ckv_ref, gqa_ref, gkva_ref, wuq_ref, wukv_ref, gqn_ref, gkn_ref,
             cos_ref, sin_ref, da_ref, dwuq_ref, dwukv_ref, dgqn_ref, dgkn_ref, dgqa_ref, dgkva_ref,
             dcq, dckv, dkpe):
        i, h = pl.program_id(0), pl.program_id(1)

        @pl.when((i == 0) & (h == 0))
        def _():
            for ref in (dwuq_ref, dwukv_ref, dgqn_ref, dgkn_ref, dgqa_ref, dgkva_ref):
                ref[...] = jnp.zeros_like(ref)

        @pl.when(h == 0)
        def _():
            dcq[...] = jnp.zeros_like(dcq)
            dckv[...] = jnp.zeros_like(dckv)
            dkpe[...] = jnp.zeros_like(dkpe)

        cos_v, sin_v = cos_ref[...], sin_ref[...]
        cqv, ckvv = cq_ref[...], ckv_ref[...]
        wuq, wukv = wuq_ref[...], wukv_ref[...]

        qhat, rq = _rms(_dot(cqv, wuq), QK_DIM)
        dqr = dq_ref[...]
        dqn = jnp.concatenate([dqr[:, :NOPE], _rope_bwd(dqr[:, NOPE:], cos_v, sin_v)], axis=1)
        dqp, dg = _rms_bwd(dqn, qhat, rq, gqn_ref[...], QK_DIM)
        dgqn_ref[...] += dg
        dqp = dqp.astype(MXU_DTYPE)
        dwuq_ref[h] += _dot(cqv, dqp, TN)
        dcq[...] += _dot(dqp, wuq, NT)

        kvp = _dot(ckvv, wukv)
        kk = jnp.concatenate([kvp[:, :NOPE], a_ref[:, Q_LORA + KV_LORA:]], axis=1)
        khat, rk = _rms(kk, QK_DIM)
        dkr = dk_ref[...]
        dkn = jnp.concatenate([dkr[:, :NOPE], _rope_bwd(dkr[:, NOPE:], cos_v, sin_v)], axis=1)
        dkk, dg = _rms_bwd(dkn, khat, rk, gkn_ref[...], QK_DIM)
        dgkn_ref[...] += dg
        dkpe[...] += dkk[:, NOPE:]
        dkvp = jnp.concatenate([dkk[:, :NOPE], dv_ref[...]], axis=1).astype(MXU_DTYPE)
        dwukv_ref[h] += _dot(ckvv, dkvp, TN)
        dckv[...] += _dot(dkvp, wukv, NT)

        @pl.when(h == H - 1)
        def _():
            av = a_ref[...]
            ahat, r = _rms(av[:, :Q_LORA], Q_LORA)
            daq, dg = _rms_bwd(dcq[...], ahat, r, gqa_ref[...], Q_LORA)
            dgqa_ref[...] += dg
            ahat, r = _rms(av[:, Q_LORA:Q_LORA + KV_LORA], KV_LORA)
            dakv, dg = _rms_bwd(dckv[...], ahat, r, gkva_ref[...], KV_LORA)
            dgkva_ref[...] += dg
            da_ref[...] = jnp.concatenate([daq, dakv, dkpe[...]], axis=1)

    row = lambda w: pl.BlockSpec((tm, w), lambda i, h: (i, 0))
    vec = lambda w: pl.BlockSpec((1, w), lambda i, h: (0, 0))
    head = lambda w: pl.BlockSpec((None, tm, w), lambda i, h: (h, i, 0))
    full3 = lambda s: pl.BlockSpec(s, lambda i, h: (0, 0, 0))
    return pl.pallas_call(
        body, name="mla_pre_bwd", grid=(S // tm, H),
        in_specs=[head(QK_PAD), head(QK_PAD), head(V_DIM), row(DOWN_PAD), row(Q_LORA), row(KV_LORA),
                  vec(Q_LORA), vec(KV_LORA),
                  pl.BlockSpec((None, Q_LORA, QK_PAD), lambda i, h: (h, 0, 0)),
                  pl.BlockSpec((None, KV_LORA, NOPE + V_DIM), lambda i, h: (h, 0, 0)),
                  vec(QK_PAD), vec(QK_PAD), row(128), row(128)],
        out_specs=[row(DOWN_PAD), full3((H, Q_LORA, QK_PAD)), full3((H, KV_LORA, NOPE + V_DIM)),
                   vec(QK_PAD), vec(QK_PAD), vec(Q_LORA), vec(KV_LORA)],
        out_shape=[jax.ShapeDtypeStruct((S, DOWN_PAD), F32),
                   jax.ShapeDtypeStruct((H, Q_LORA, QK_PAD), F32),
                   jax.ShapeDtypeStruct((H, KV_LORA, NOPE + V_DIM), F32),
                   jax.ShapeDtypeStruct((1, QK_PAD), F32), jax.ShapeDtypeStruct((1, QK_PAD), F32),
                   jax.ShapeDtypeStruct((1, Q_LORA), F32), jax.ShapeDtypeStruct((1, KV_LORA), F32)],
        scratch_shapes=[pltpu.VMEM((tm, Q_LORA), F32), pltpu.VMEM((tm, KV_LORA), F32), pltpu.VMEM((tm, 128), F32)],
        compiler_params=_params(2),
    )(dq, dk, dv, a, cq, ckv, gqa, gkva, wuq3, wukv3, gqn, gkn, cos_t, sin_t)


def _pair_tables(nb, key_major):
    if key_major:
        pairs = [(qi, kj) for kj in range(nb) for qi in range(kj, nb)]
    else:
        pairs = [(qi, ki) for qi in range(nb) for ki in range(qi + 1)]
    return (jnp.asarray(np.array([p[0] for p in pairs], np.int32)),
            jnp.asarray(np.array([p[1] for p in pairs], np.int32)))


def _scores_t(k, q, pk_col, pq_row, masked):
    s = _dot(k, q, NT) * (QK_DIM ** -0.5)
    return jnp.where(pq_row >= pk_col, s, jnp.finfo(F32).min) if masked else s


def attn_fwd(q, k, v, pos_col, pos_row):
    H, S, _ = q.shape
    t = min(TQ, S)
    nb = S // t
    hb = HEADS_FWD
    qt, kt = _pair_tables(nb, key_major=False)

    def body(qt_ref, kt_ref, q_ref, k_ref, v_ref, pk_ref, pq_ref, o_ref, lse_ref, m_s, l_s, acc):
        step = pl.program_id(1)
        qi, ki = qt_ref[step], kt_ref[step]

        @pl.when(ki == 0)
        def _():
            m_s[...] = jnp.full_like(m_s, -jnp.inf)
            l_s[...] = jnp.zeros_like(l_s)
            acc[...] = jnp.zeros_like(acc)

        def update(masked):
            for hh in range(hb):
                s = _scores_t(k_ref[hh], q_ref[hh], pk_ref[...], pq_ref[...], masked)
                m_old = m_s[hh]
                m_new = jnp.maximum(m_old, jnp.max(s, axis=0, keepdims=True))
                p = jnp.exp(s - m_new)
                alpha = jnp.exp(m_old - m_new)
                l_s[hh] = alpha * l_s[hh] + jnp.sum(p, axis=0, keepdims=True)
                acc[hh] = alpha * acc[hh] + _dot(v_ref[hh], p, TN)
                m_s[hh] = m_new

        @pl.when(ki < qi)
        def _():
            update(False)

        @pl.when(ki == qi)
        def _():
            update(True)
            for hh in range(hb):
                o_ref[:, hh * V_DIM:(hh + 1) * V_DIM] = (acc[hh] / l_s[hh]).T
                lse_ref[hh] = m_s[hh] + jnp.log(l_s[hh])

    grid_spec = pltpu.PrefetchScalarGridSpec(
        num_scalar_prefetch=2, grid=(H // hb, qt.shape[0]),
        in_specs=[pl.BlockSpec((hb, t, QK_PAD), lambda h, s, qt, kt: (h, qt[s], 0)),
                  pl.BlockSpec((hb, t, QK_PAD), lambda h, s, qt, kt: (h, kt[s], 0)),
                  pl.BlockSpec((hb, t, V_DIM), lambda h, s, qt, kt: (h, kt[s], 0)),
                  pl.BlockSpec((t, 1), lambda h, s, qt, kt: (kt[s], 0)),
                  pl.BlockSpec((1, t), lambda h, s, qt, kt: (0, qt[s]))],
        out_specs=[pl.BlockSpec((t, hb * V_DIM), lambda h, s, qt, kt: (qt[s], h)),
                   pl.BlockSpec((hb, 1, t), lambda h, s, qt, kt: (h, 0, qt[s]))],
        scratch_shapes=[pltpu.VMEM((hb, 1, t), F32), pltpu.VMEM((hb, 1, t), F32), pltpu.VMEM((hb, V_DIM, t), F32)])
    return pl.pallas_call(
        body, name="attn_fwd", grid_spec=grid_spec,
        out_shape=[jax.ShapeDtypeStruct((S, H * V_DIM), F32), jax.ShapeDtypeStruct((H, 1, S), F32)],
        compiler_params=_params(2),
    )(qt, kt, q, k, v, pos_col, pos_row)


def attn_delta(o, do):
    S = o.shape[0]
    t = min(TQ, S)

    def body(o_ref, do_ref, d_ref):
        d_ref[...] = jnp.sum((o_ref[...] * do_ref[...]).T, axis=0, keepdims=True)

    blk = pl.BlockSpec((t, V_DIM), lambda h, i: (i, h))
    return pl.pallas_call(
        body, name="attn_delta", grid=(N_HEADS, S // t),
        in_specs=[blk, blk],
        out_specs=pl.BlockSpec((None, 1, t), lambda h, i: (h, 0, i)),
        out_shape=jax.ShapeDtypeStruct((N_HEADS, 1, S), F32),
        compiler_params=_params(2),
    )(o, do)


def attn_bwd(q, k, v, do, lse, delta, pos_col, pos_row):
    H, S, _ = q.shape
    t = min(TQ, S)
    nb = S // t
    qt, kt = _pair_tables(nb, key_major=True)

    def body(qt_ref, kt_ref, q_ref, k_ref, v_ref, do_ref, lse_ref, dl_ref, pk_ref, pq_ref, dq_ref, dk_ref, dv_ref):
        step = pl.program_id(1)
        qi, kj = qt_ref[step], kt_ref[step]

        @pl.when(step == 0)
        def _():
            dq_ref[...] = jnp.zeros_like(dq_ref)

        @pl.when(qi == kj)
        def _():
            dk_ref[...] = jnp.zeros_like(dk_ref)
            dv_ref[...] = jnp.zeros_like(dv_ref)

        def update(masked):
            qv, kv_ = q_ref[...], k_ref[...]
            dob = do_ref[...].astype(MXU_DTYPE)
            p = jnp.exp(_scores_t(kv_, qv, pk_ref[...], pq_ref[...], masked) - lse_ref[...])
            dv_ref[...] += _dot(p, dob)
            dp = _dot(v_ref[...], dob, NT)
            ds = (p * (dp - dl_ref[...]) * (QK_DIM ** -0.5)).astype(MXU_DTYPE)
            dk_ref[...] += _dot(ds, qv)
            rows = pl.ds(pl.multiple_of(qi * t, t), t)
            dq_ref[rows, :] += _dot(ds, kv_, TN)

        @pl.when(qi == kj)
        def _():
            update(True)

        @pl.when(qi != kj)
        def _():
            update(False)

    q_idx = lambda h, s, qt, kt: (h, qt[s], 0)
    k_idx = lambda h, s, qt, kt: (h, kt[s], 0)
    row_idx = lambda h, s, qt, kt: (h, 0, qt[s])
    grid_spec = pltpu.PrefetchScalarGridSpec(
        num_scalar_prefetch=2, grid=(H, qt.shape[0]),
        in_specs=[pl.BlockSpec((None, t, QK_PAD), q_idx),
                  pl.BlockSpec((None, t, QK_PAD), k_idx),
                  pl.BlockSpec((None, t, V_DIM), k_idx),
                  pl.BlockSpec((t, V_DIM), lambda h, s, qt, kt: (qt[s], h)),
                  pl.BlockSpec((None, 1, t), row_idx),
                  pl.BlockSpec((None, 1, t), row_idx),
                  pl.BlockSpec((t, 1), lambda h, s, qt, kt: (kt[s], 0)),
                  pl.BlockSpec((1, t), lambda h, s, qt, kt: (0, qt[s]))],
        out_specs=[pl.BlockSpec((None, S, QK_PAD), lambda h, s, qt, kt: (h, 0, 0)),
                   pl.BlockSpec((None, t, QK_PAD), k_idx),
                   pl.BlockSpec((None, t, V_DIM), k_idx)])
    return pl.pallas_call(
        body, name="attn_bwd", grid_spec=grid_spec,
        out_shape=[jax.ShapeDtypeStruct((H, S, QK_PAD), F32), jax.ShapeDtypeStruct((H, S, QK_PAD), F32),
                   jax.ShapeDtypeStruct((H, S, V_DIM), F32)],
        compiler_params=_params(2),
    )(qt, kt, q, k, v, do, lse, delta, pos_col, pos_row)


def _conv_specs(S, tr, tc):
    nc = D_MODEL // tc
    hb = tr // 8
    main = lambda third: pl.BlockSpec((tr, tc), lambda c, r: (r, third * nc + c))
    prev = lambda third: pl.BlockSpec((8, tc), lambda c, r: (jnp.maximum(r * hb - 1, 0), third * nc + c))
    nxt = lambda third: pl.BlockSpec((8, tc), lambda c, r: (jnp.minimum((r + 1) * hb, S // 8 - 1), third * nc + c))
    return main, prev, nxt


def _conv_taps(gc, uu, w_ref, first):
    u2 = gc * uu
    rows = lax.broadcasted_iota(jnp.int32, u2.shape, 0)
    u2 = jnp.where((rows < 8) & first, 0.0, u2)
    s1 = pltpu.roll(u2, 1, 0)
    s2 = pltpu.roll(u2, 2, 0)
    u3 = w_ref[2:3, :] * u2 + w_ref[1:2, :] * s1 + w_ref[0:1, :] * s2
    return u2, s1, s2, u3


def conv_fwd(bcu, cw):
    S = bcu.shape[0]
    tr, tc = min(TROW, S), TCH
    main, prev, _ = _conv_specs(S, tr, tc)

    def body(gb_ref, gc_ref, u_ref, gch_ref, uh_ref, w_ref, z_ref):
        gc = jnp.concatenate([gch_ref[...], gc_ref[...]], axis=0)
        uu = jnp.concatenate([uh_ref[...], u_ref[...]], axis=0)
        u3 = _conv_taps(gc, uu, w_ref, pl.program_id(1) == 0)[3]
        z_ref[...] = (gb_ref[...] * u3[8:]).astype(z_ref.dtype)

    return pl.pallas_call(
        body, name="conv_fwd", grid=(D_MODEL // tc, S // tr),
        in_specs=[main(0), main(1), main(2), prev(1), prev(2), pl.BlockSpec((3, tc), lambda c, r: (0, c))],
        out_specs=pl.BlockSpec((tr, tc), lambda c, r: (r, c)),
        out_shape=jax.ShapeDtypeStruct((S, D_MODEL), MXU_DTYPE),
        compiler_params=_params(2),
    )(bcu, bcu, bcu, bcu, bcu, cw)


def conv_bwd(dz, bcu, cw):
    S = bcu.shape[0]
    tr, tc = min(TROW, S), TCH
    nr = S // tr
    main, prev, nxt = _conv_specs(S, tr, tc)

    def body(dz_ref, dzn_ref, gb_ref, gbn_ref, gc_ref, u_ref, gch_ref, uh_ref, w_ref,
             dgb_ref, dgc_ref, du_ref, dw_ref):
        r = pl.program_id(1)
        gcv, uv = gc_ref[...], u_ref[...]
        gc = jnp.concatenate([gch_ref[...], gcv], axis=0)
        uu = jnp.concatenate([uh_ref[...], uv], axis=0)
        u2, s1, s2, u3 = _conv_taps(gc, uu, w_ref, r == 0)
        dzv = dz_ref[...]
        du3 = jnp.concatenate([dzv * gb_ref[...], dzn_ref[...] * gbn_ref[...]], axis=0)
        rows = lax.broadcasted_iota(jnp.int32, du3.shape, 0)
        du3 = jnp.where((rows >= tr) & (r == nr - 1), 0.0, du3)
        n1 = pltpu.roll(du3, tr + 8 - 1, 0)
        n2 = pltpu.roll(du3, tr + 8 - 2, 0)
        du2 = (w_ref[2:3, :] * du3 + w_ref[1:2, :] * n1 + w_ref[0:1, :] * n2)[:tr]
        dgb_ref[...] = (dzv * u3[8:]).astype(dgb_ref.dtype)
        dgc_ref[...] = (du2 * uv).astype(dgc_ref.dtype)
        du_ref[...] = (du2 * gcv).astype(du_ref.dtype)
        d3 = du3[:tr]
        taps = [jnp.sum(d3 * t[8:], axis=0, keepdims=True) for t in (s2, s1, u2)]

        @pl.when(r == 0)
        def _():
            for kk in range(3):
                dw_ref[kk:kk + 1, :] = taps[kk]

        @pl.when(r != 0)
        def _():
            for kk in range(3):
                dw_ref[kk:kk + 1, :] += taps[kk]

    out = pl.BlockSpec((tr, tc), lambda c, r: (r, c))
    nxt_dz = pl.BlockSpec((8, tc), lambda c, r: (jnp.minimum((r + 1) * (tr // 8), S // 8 - 1), c))
    act = jax.ShapeDtypeStruct((S, D_MODEL), MXU_DTYPE)
    return pl.pallas_call(
        body, name="conv_bwd", grid=(D_MODEL // tc, nr),
        in_specs=[out, nxt_dz, main(0), nxt(0), main(1), main(2), prev(1), prev(2),
                  pl.BlockSpec((3, tc), lambda c, r: (0, c))],
        out_specs=[out, out, out, pl.BlockSpec((3, tc), lambda c, r: (0, c))],
        out_shape=[act, act, act, jax.ShapeDtypeStruct((3, D_MODEL), F32)],
        compiler_params=_params(2),
    )(dz, dz, bcu, bcu, bcu, bcu, bcu, bcu, cw)


def loss_head(y, target):
    S, D = y.shape
    tm = min(TM, S)

    def body(y_ref, t_ref, l_ref, dy_ref):
        err = y_ref[...] - t_ref[...]
        dy_ref[...] = err / D
        part = jnp.full((1, 128), jnp.sum(err * err), F32)

        @pl.when(pl.program_id(0) == 0)
        def _():
            l_ref[...] = part

        @pl.when(pl.program_id(0) != 0)
        def _():
            l_ref[...] += part

    blk = pl.BlockSpec((tm, D), lambda i: (i, 0))
    return pl.pallas_call(
        body, name="loss_head", grid=(S // tm,),
        in_specs=[blk, blk],
        out_specs=[pl.BlockSpec((1, 128), lambda i: (0, 0)), blk],
        out_shape=[jax.ShapeDtypeStruct((1, 128), F32), jax.ShapeDtypeStruct((S, D), F32)],
        compiler_params=_params(1),
    )(y, target)


def adamw(parts, w, m, v, *, name):
    R, C = w.shape
    tr = R
    while tr * C * 4 > (1 << 20) and tr % 32 == 0:
        tr //= 2

    def body(p_ref, w_ref, m_ref, v_ref, g_ref, d_ref, mo_ref, vo_ref):
        g = p_ref[0].astype(F32)
        for d in range(1, N_DEV):
            g = g + p_ref[d].astype(F32)
        m_new = ADAM_B1 * m_ref[...] + (1.0 - ADAM_B1) * g
        v_new = ADAM_B2 * v_ref[...] + (1.0 - ADAM_B2) * (g * g)
        m_hat = m_new / (1.0 - ADAM_B1 ** ADAM_STEP)
        v_hat = v_new / (1.0 - ADAM_B2 ** ADAM_STEP)
        g_ref[...] = g
        d_ref[...] = -ADAM_LR * (m_hat / (jnp.sqrt(v_hat) + ADAM_EPS) + ADAM_WD * w_ref[...])
        mo_ref[...] = m_new
        vo_ref[...] = v_new

    blk = pl.BlockSpec((tr, C), lambda i: (i, 0))
    return pl.pallas_call(
        body, name=name, grid=(R // tr,),
        in_specs=[pl.BlockSpec((N_DEV, tr, C), lambda i: (0, i, 0)), blk, blk, blk],
        out_specs=[blk, blk, blk, blk],
        out_shape=[jax.ShapeDtypeStruct((R, C), F32)] * 4,
        compiler_params=_params(1),
    )(parts, w, m, v)


def _mesh_place():
    x, y, c = (lax.axis_index(n) for n in MESH_AXES)
    return x, y, c, 4 * x + 2 * y + c


def _peer(x, y, c, d):
    px = 1 - x if d & 4 else x
    py = 1 - y if d & 2 else y
    pc = 1 - c if d & 1 else c
    return (px, py, pc), 4 * px + 2 * py + pc


def _exchange(srcs, name, scatter):
    n = len(srcs)
    any_spec = pl.BlockSpec(memory_space=pl.ANY)

    def body(*refs):
        ins, outs = refs[:n], refs[n:2 * n]
        send_sems, recv_sems, local_sems = refs[2 * n:]
        x, y, c, me = _mesh_place()
        for a in range(n):
            mine = ins[a].at[me] if scatter else ins[a]
            pltpu.make_async_copy(mine, outs[a].at[me], local_sems.at[a]).start()
            for d in range(1, N_DEV):
                peer, peer_lin = _peer(x, y, c, d)
                pltpu.make_async_remote_copy(
                    src_ref=ins[a].at[peer_lin] if scatter else ins[a], dst_ref=outs[a].at[me],
                    send_sem=send_sems.at[a], recv_sem=recv_sems.at[a],
                    device_id=peer, device_id_type=pl.DeviceIdType.MESH).start()
        for a in range(n):
            mine = ins[a].at[me] if scatter else ins[a]
            pltpu.make_async_copy(mine, outs[a].at[me], local_sems.at[a]).wait()
            seven = outs[a].at[pl.ds(0, N_DEV - 1)]
            drain = pltpu.make_async_remote_copy(
                src_ref=seven, dst_ref=seven, send_sem=send_sems.at[a], recv_sem=recv_sems.at[a],
                device_id=(x, y, c), device_id_type=pl.DeviceIdType.MESH)
            drain.wait_send()
            drain.wait_recv()

    block = (lambda s: s.shape[1:]) if scatter else (lambda s: s.shape)
    return pl.pallas_call(
        body, name=name,
        in_specs=[any_spec] * n, out_specs=[any_spec] * n,
        out_shape=[jax.ShapeDtypeStruct((N_DEV,) + tuple(block(s)), s.dtype) for s in srcs],
        scratch_shapes=[pltpu.SemaphoreType.DMA((n,)), pltpu.SemaphoreType.DMA((n,)), pltpu.SemaphoreType.DMA((n,))],
    )(*srcs)


def _rope_tables(pos):
    inv_freq = ROPE_THETA ** (-jnp.arange(0, ROPE, 2, dtype=F32) / ROPE)
    ang = pos.astype(F32)[:, None] * inv_freq
    cos, sin = jnp.cos(ang), jnp.sin(ang)
    pad = jnp.zeros((pos.shape[0], 128 - ROPE), F32)
    return jnp.concatenate([cos, cos, pad + 1.0], axis=1), jnp.concatenate([-sin, sin, pad], axis=1)


def _pad_last(w, n):
    return jnp.pad(w, [(0, 0)] * (w.ndim - 1) + [(0, n - w.shape[-1])])


def kernel(x, positions, g_mix, g_mlp, attn_w_down, attn_g_q_a, attn_g_kv_a, attn_w_uq, attn_w_ukv, attn_g_qnorm, attn_g_knorm, attn_w_o, conv_w_in, conv_w, conv_w_out, mlp_w1, mlp_w2, loss_target, m_g_mix, m_g_mlp, m_attn_w_down, m_attn_g_q_a, m_attn_g_kv_a, m_attn_w_uq, m_attn_w_ukv, m_attn_g_qnorm, m_attn_g_knorm, m_attn_w_o, m_conv_w_in, m_conv_w, m_conv_w_out, m_mlp_w1, m_mlp_w2, v_g_mix, v_g_mlp, v_attn_w_down, v_attn_g_q_a, v_attn_g_kv_a, v_attn_w_uq, v_attn_w_ukv, v_attn_g_qnorm, v_attn_g_knorm, v_attn_w_o, v_conv_w_in, v_conv_w, v_conv_w_out, v_mlp_w1, v_mlp_w2):
    weights = dict(g_mix=g_mix, g_mlp=g_mlp, attn_w_down=attn_w_down, attn_g_q_a=attn_g_q_a, attn_g_kv_a=attn_g_kv_a,
                   attn_w_uq=attn_w_uq, attn_w_ukv=attn_w_ukv, attn_g_qnorm=attn_g_qnorm, attn_g_knorm=attn_g_knorm,
                   attn_w_o=attn_w_o, conv_w_in=conv_w_in, conv_w=conv_w, conv_w_out=conv_w_out, mlp_w1=mlp_w1, mlp_w2=mlp_w2)
    mom1 = dict(g_mix=m_g_mix, g_mlp=m_g_mlp, attn_w_down=m_attn_w_down, attn_g_q_a=m_attn_g_q_a, attn_g_kv_a=m_attn_g_kv_a,
                attn_w_uq=m_attn_w_uq, attn_w_ukv=m_attn_w_ukv, attn_g_qnorm=m_attn_g_qnorm, attn_g_knorm=m_attn_g_knorm,
                attn_w_o=m_attn_w_o, conv_w_in=m_conv_w_in, conv_w=m_conv_w, conv_w_out=m_conv_w_out, mlp_w1=m_mlp_w1, mlp_w2=m_mlp_w2)
    mom2 = dict(g_mix=v_g_mix, g_mlp=v_g_mlp, attn_w_down=v_attn_w_down, attn_g_q_a=v_attn_g_q_a, attn_g_kv_a=v_attn_g_kv_a,
                attn_w_uq=v_attn_w_uq, attn_w_ukv=v_attn_w_ukv, attn_g_qnorm=v_attn_g_qnorm, attn_g_knorm=v_attn_g_knorm,
                attn_w_o=v_attn_w_o, conv_w_in=v_conv_w_in, conv_w=v_conv_w, conv_w_out=v_conv_w_out, mlp_w1=v_mlp_w1, mlp_w2=v_mlp_w2)
    big = ["attn_w_down", "attn_w_uq", "attn_w_ukv", "attn_w_o", "conv_w_in", "conv_w", "conv_w_out", "mlp_w1", "mlp_w2"]
    small = ["g_mix", "g_mlp", "attn_g_q_a", "attn_g_kv_a", "attn_g_qnorm", "attn_g_knorm"]
    order = ["g_mix", "g_mlp", "attn_w_down", "attn_g_q_a", "attn_g_kv_a", "attn_w_uq", "attn_w_ukv", "attn_g_qnorm",
             "attn_g_knorm", "attn_w_o", "conv_w_in", "conv_w", "conv_w_out", "mlp_w1", "mlp_w2"]

    xs = x[0]
    pos = positions[0]
    target = loss_target[0]
    S = xs.shape[0]
    depth = g_mix.shape[0]
    cos_t, sin_t = _rope_tables(pos)
    pos_col, pos_row = pos.reshape(S, 1), pos.reshape(1, S)

    keys, shards = [], []
    for name in big:
        for l in range(weights[name].shape[0]):
            keys.append((name, l))
            shards.append(weights[name][l] if name == "conv_w" else weights[name][l].astype(WIRE_DTYPE))
    full = dict(zip(keys, _exchange(shards, "gather_weights", scatter=False)))

    def rows(name, l):
        g = full[(name, l)]
        return g.reshape(g.shape[0] * g.shape[1], g.shape[2])

    saved = []
    for i in range(depth):
        l = i // 2
        rec = {"x0": xs}
        if i % 2 == 0:
            wd3 = _pad_last(rows("attn_w_down", l), DOWN_PAD)[None]
            wuq3 = _pad_last(full[("attn_w_uq", l)], QK_PAD)
            wukv3 = full[("attn_w_ukv", l)]
            gqn = _pad_last(attn_g_qnorm[l][None], QK_PAD)
            gkn = _pad_last(attn_g_knorm[l][None], QK_PAD)
            gqa, gkva = attn_g_q_a[l][None], attn_g_kv_a[l][None]
            h, a = rms_mm(xs, g_mix[i], wd3, name="attn_down")
            q, k, v, cq, ckv = mla_pre_fwd(a, gqa, gkva, wuq3, wukv3, gqn, gkn, cos_t, sin_t)
            o, lse = attn_fwd(q, k, v, pos_col, pos_row)
            x1 = mm_res(o, rows("attn_w_o", l), xs, name="attn_out")
            rec.update(h=h, a=a, q=q, k=k, v=v, cq=cq, ckv=ckv, o=o, lse=lse, wd3=wd3, wuq3=wuq3, wukv3=wukv3,
                       gqn=gqn, gkn=gkn, gqa=gqa, gkva=gkva)
        else:
            cw = full[("conv_w", l)].transpose(1, 0, 2).reshape(3, D_MODEL)
            h, bcu = rms_mm(xs, g_mix[i], full[("conv_w_in", l)], name="conv_in")
            z = conv_fwd(bcu, cw)
            x1 = mm_res(z, rows("conv_w_out", l), xs, name="conv_out")
            rec.update(h=h, bcu=bcu, z=z, cw=cw)
        h2, act = rms_mm(x1, g_mlp[i], full[("mlp_w1", i)], name="mlp_up", mlp=True)
        xs = mm_res(act, rows("mlp_w2", i), x1, name="mlp_down")
        rec.update(x1=x1, h2=h2, act=act)
        saved.append(rec)

    sq, dx = loss_head(xs, target)
    loss = lax.psum(sq[0, 0] * (0.5 / D_MODEL), MESH_AXES)

    grads = {name: [None] * weights[name].shape[0] for name in order}
    for i in reversed(range(depth)):
        l = i // 2
        rec = saved[i]
        grads["mlp_w2"][i] = mm_tn(rec["act"], dx, name="mlp_down_dw", G=1, out_dtype=WIRE_DTYPE).reshape(N_DEV, -1, D_MODEL)
        du = mm_nt(dx, rows("mlp_w2", i)[None], name="mlp_down_dx", epi="mlp_du", u=rec["act"])
        grads["mlp_w1"][i] = mm_tn(rec["h2"], du, name="mlp_up_dw", G=N_DEV, out_dtype=WIRE_DTYPE)
        dx1, dg = mm_nt(du, full[("mlp_w1", i)], name="mlp_up_dx", epi="rms_bwd", x=rec["x1"], g=g_mlp[i], dx=dx)
        grads["g_mlp"][i] = dg[0]
        if i % 2 == 0:
            grads["attn_w_o"][l] = mm_tn(rec["o"], dx1, name="attn_out_dw", G=1, out_dtype=WIRE_DTYPE).reshape(N_DEV, -1, D_MODEL)
            do = mm_nt(dx1, rows("attn_w_o", l)[None], name="attn_out_dx")
            dq, dk, dv = attn_bwd(rec["q"], rec["k"], rec["v"], do, rec["lse"], attn_delta(rec["o"], do), pos_col, pos_row)
            da, dwuq, dwukv, dgqn, dgkn, dgqa, dgkva = mla_pre_bwd(
                dq, dk, dv, rec["a"], rec["cq"], rec["ckv"], rec["gqa"], rec["gkva"], rec["wuq3"], rec["wukv3"],
                rec["gqn"], rec["gkn"], cos_t, sin_t)
            grads["attn_w_uq"][l] = dwuq[:, :, :QK_DIM].astype(WIRE_DTYPE)
            grads["attn_w_ukv"][l] = dwukv.astype(WIRE_DTYPE)
            grads["attn_g_qnorm"][l] = dgqn[0, :QK_DIM]
            grads["attn_g_knorm"][l] = dgkn[0, :QK_DIM]
            grads["attn_g_q_a"][l] = dgqa[0]
            grads["attn_g_kv_a"][l] = dgkva[0]
            dwd = mm_tn(rec["h"], da, name="attn_down_dw", G=1, out_dtype=WIRE_DTYPE)
            grads["attn_w_down"][l] = dwd[0, :, :DOWN].reshape(N_DEV, -1, DOWN)
            dx, dg = mm_nt(da, rec["wd3"], name="attn_down_dx", epi="rms_bwd", x=rec["x0"], g=g_mix[i], dx=dx1)
        else:
            grads["conv_w_out"][l] = mm_tn(rec["z"], dx1, name="conv_out_dw", G=1, out_dtype=WIRE_DTYPE).reshape(N_DEV, -1, D_MODEL)
            dz = mm_nt(dx1, rows("conv_w_out", l)[None], name="conv_out_dx")
            dgb, dgc, du_, dcw = conv_bwd(dz, rec["bcu"], rec["cw"])
            grads["conv_w"][l] = dcw.reshape(3, N_DEV, -1).transpose(1, 0, 2)
            dbcu = jnp.concatenate([dgb, dgc, du_], axis=1)
            grads["conv_w_in"][l] = mm_tn(rec["h"], dbcu, name="conv_in_dw", G=N_DEV, out_dtype=WIRE_DTYPE)
            dx, dg = mm_nt(dbcu, full[("conv_w_in", l)], name="conv_in_dx", epi="rms_bwd", x=rec["x0"], g=g_mix[i], dx=dx1)
        grads["g_mix"][i] = dg[0]

    sizes = [weights[name].size for name in small]
    n_small = sum(sizes)
    rows_small = -(-n_small // (8 * 128)) * 8

    def pack(tree):
        flat = jnp.concatenate([jnp.stack(tree[name]).reshape(-1) if isinstance(tree[name], list) else tree[name].reshape(-1)
                                for name in small])
        return jnp.pad(flat, (0, rows_small * 128 - n_small)).reshape(rows_small, 128)

    sends = [jnp.stack(grads[name], axis=1) for name in big]
    sends.append(jnp.broadcast_to(pack(grads)[None], (N_DEV, rows_small, 128)))
    parts = _exchange(sends, "scatter_grads", scatter=True)

    out = {}
    for name, part in zip(big, parts[:-1]):
        w = weights[name]
        flat = lambda t: t.reshape(-1, t.shape[-1])
        res = adamw(part.reshape(N_DEV, -1, w.shape[-1]), flat(w), flat(mom1[name]), flat(mom2[name]), name="adamw_" + name)
        out[name] = [r.reshape(w.shape) for r in res]
    res = adamw(parts[-1], pack(weights), pack(mom1), pack(mom2), name="adamw_gains")
    offset = 0
    for name, size in zip(small, sizes):
        out[name] = [r.reshape(-1)[offset:offset + size].reshape(weights[name].shape) for r in res]
        offset += size

    return (loss, dx[None], *[out[n][0] for n in order], *[out[n][1] for n in order],
            *[out[n][2] for n in order], *[out[n][3] for n in order])
```

```python
import jax
import jax.numpy as jnp
import numpy as np
from jax import lax
from jax.experimental import pallas as pl
from jax.experimental.pallas import tpu as pltpu

F32 = jnp.float32
MXU_DTYPE = jnp.bfloat16
WIRE_DTYPE = jnp.bfloat16

D_MODEL = 1024
N_HEADS = 8
NOPE = 128
ROPE = 64
QK_DIM = NOPE + ROPE
QK_PAD = 256
V_DIM = 128
Q_LORA = 256
KV_LORA = 128
DOWN = Q_LORA + KV_LORA + ROPE
DOWN_PAD = 512
ROPE_THETA = 10000.0
EPS = 1e-6
SM_SCALE = QK_DIM ** -0.5
LOG2E = 1.4426950408889634
Q_PRESCALE = SM_SCALE * LOG2E
ADAM_LR, ADAM_B1, ADAM_B2, ADAM_EPS, ADAM_WD, ADAM_STEP = 0.001, 0.9, 0.999, 1e-08, 0.01, 10
N_DEV = 8
MESH_AXES = ("x", "y", "c")

TM = 512
TM_WIDE = 1024
TQ = 512
HEADS_FWD = 8
BWD_CHUNK = 256
TROW = 512
TCH = 512
VMEM_LIMIT = 48 << 20

NN = (((1,), (0,)), ((), ()))
NT = (((1,), (1,)), ((), ()))
TN = (((0,), (0,)), ((), ()))


def _dot(a, b, dims=NN):
    return lax.dot_general(a.astype(MXU_DTYPE), b.astype(MXU_DTYPE), dims, preferred_element_type=F32)


def _params(n_axes):
    return pltpu.CompilerParams(dimension_semantics=("arbitrary",) * n_axes, vmem_limit_bytes=VMEM_LIMIT)


def _rms(xv, n):
    r = lax.rsqrt(jnp.sum(xv * xv, axis=-1, keepdims=True) / n + EPS)
    return xv * r, r


def _rms_bwd(dy, xhat, r, g, n):
    dg = jnp.sum(dy * xhat, axis=0, keepdims=True)
    dxh = dy * g
    dx = r * (dxh - xhat * (jnp.sum(dxh * xhat, axis=-1, keepdims=True) / n))
    return dx, dg


def _swap_halves(t):
    lane = lax.broadcasted_iota(jnp.int32, t.shape, 1)
    return jnp.where(lane < ROPE // 2, pltpu.roll(t, 128 - ROPE // 2, 1), pltpu.roll(t, ROPE // 2, 1))


def _rope(t, cos_t, sin_t):
    return t * cos_t + _swap_halves(t) * sin_t


def _rope_bwd(dout, cos_t, sin_t):
    return dout * cos_t + _swap_halves(dout * sin_t)


def rms_mm(x, g, w3, *, name, mlp=False):
    S, D = x.shape
    G, _, Nb = w3.shape
    tm = min(TM_WIDE, S)
    tn = Nb if Nb <= 512 else 512
    nj = Nb // tn
    N = G * Nb

    def body(x_ref, g_ref, w_ref, h_ref, *rest):
        hs = rest[-1]

        @pl.when(pl.program_id(1) == 0)
        def _():
            xv = x_ref[...]
            r = lax.rsqrt(jnp.mean(xv * xv, axis=-1, keepdims=True) + EPS)
            h = (xv * r * g_ref[...]).astype(hs.dtype)
            hs[...] = h
            h_ref[...] = h

        acc = lax.dot_general(hs[...], w_ref[...].astype(hs.dtype), NN, preferred_element_type=F32)
        if mlp:
            rl = jnp.maximum(acc, 0.0)
            rest[0][...] = (rl * rl).astype(rest[0].dtype)
        else:
            rest[0][...] = acc

    out_shape = [jax.ShapeDtypeStruct((S, D), MXU_DTYPE), jax.ShapeDtypeStruct((S, N), MXU_DTYPE if mlp else F32)]
    out_specs = [pl.BlockSpec((tm, D), lambda i, j: (i, 0)), pl.BlockSpec((tm, tn), lambda i, j: (i, j))]
    return pl.pallas_call(
        body, name=name, grid=(S // tm, G * nj),
        in_specs=[pl.BlockSpec((tm, D), lambda i, j: (i, 0)),
                  pl.BlockSpec((1, D), lambda i, j: (0, 0)),
                  pl.BlockSpec((None, D, tn), lambda i, j: (j // nj, 0, j % nj))],
        out_specs=out_specs, out_shape=out_shape,
        scratch_shapes=[pltpu.VMEM((tm, D), MXU_DTYPE)],
        compiler_params=_params(2),
    )(x, g.reshape(1, D), w3)


def mm_res(a, w, res, *, name):
    S, K = a.shape
    _, N = w.shape
    tm = min(TM_WIDE, S)
    tk = min(K, 1024)
    nk = K // tk

    def body(a_ref, w_ref, r_ref, o_ref, acc):
        k = pl.program_id(1)

        @pl.when(k == 0)
        def _():
            acc[...] = jnp.zeros_like(acc)

        acc[...] += _dot(a_ref[...], w_ref[...])

        @pl.when(k == nk - 1)
        def _():
            o_ref[...] = r_ref[...] + acc[...]

    return pl.pallas_call(
        body, name=name, grid=(S // tm, nk),
        in_specs=[pl.BlockSpec((tm, tk), lambda i, k: (i, k)),
                  pl.BlockSpec((tk, N), lambda i, k: (k, 0)),
                  pl.BlockSpec((tm, N), lambda i, k: (i, 0))],
        out_specs=pl.BlockSpec((tm, N), lambda i, k: (i, 0)),
        out_shape=jax.ShapeDtypeStruct((S, N), F32),
        scratch_shapes=[pltpu.VMEM((tm, N), F32)],
        compiler_params=_params(2),
    )(a, w, res)


def mm_nt(a, w3, *, name, epi="plain", u=None, x=None, g=None, dx=None):
    S, N = a.shape
    G, Ko, Nb = w3.shape
    assert N == G * Nb
    tm = min(TM if epi == "rms_bwd" else TM_WIDE, S)
    tkk = Nb if Nb <= 1024 else 1024
    nk = Nb // tkk
    ks = G * nk
    tko = Ko if epi == "rms_bwd" else min(Ko, 512)

    def body(a_ref, w_ref, *rest):
        acc = rest[-1]
        i, k = pl.program_id(0), pl.program_id(2)

        @pl.when(k == 0)
        def _():
            acc[...] = jnp.zeros_like(acc)

        acc[...] += _dot(a_ref[...], w_ref[...], NT)

        @pl.when(k == ks - 1)
        def _():
            if epi == "plain":
                rest[0][...] = acc[...]
            elif epi == "mlp_du":
                u_ref, o_ref = rest[0], rest[1]
                o_ref[...] = (acc[...] * (2.0 * jnp.sqrt(u_ref[...].astype(F32)))).astype(o_ref.dtype)
            else:
                x_ref, g_ref, dx_ref, o_ref, dg_ref = rest[:5]
                xhat, r = _rms(x_ref[...], Ko)
                dxb, dg = _rms_bwd(acc[...], xhat, r, g_ref[...], Ko)
                o_ref[...] = dx_ref[...] + dxb

                @pl.when(i == 0)
                def _():
                    dg_ref[...] = dg

                @pl.when(i != 0)
                def _():
                    dg_ref[...] += dg

    in_specs = [pl.BlockSpec((tm, tkk), lambda i, j, k: (i, k)),
                pl.BlockSpec((None, tko, tkk), lambda i, j, k: (k // nk, j, k % nk))]
    args = [a, w3]
    tile = pl.BlockSpec((tm, tko), lambda i, j, k: (i, j))
    if epi == "plain":
        out_shape, out_specs = jax.ShapeDtypeStruct((S, Ko), F32), tile
    elif epi == "mlp_du":
        in_specs.append(tile)
        args.append(u)
        out_shape, out_specs = jax.ShapeDtypeStruct((S, Ko), MXU_DTYPE), tile
    else:
        vec = pl.BlockSpec((1, Ko), lambda i, j, k: (0, 0))
        in_specs += [tile, vec, tile]
        args += [x, g.reshape(1, Ko), dx]
        out_shape = [jax.ShapeDtypeStruct((S, Ko), F32), jax.ShapeDtypeStruct((1, Ko), F32)]
        out_specs = [tile, vec]
    return pl.pallas_call(
        body, name=name, grid=(S // tm, Ko // tko, ks),
        in_specs=in_specs, out_specs=out_specs, out_shape=out_shape,
        scratch_shapes=[pltpu.VMEM((tm, tko), F32)],
        compiler_params=_params(3),
    )(*args)


def mm_tn(a, b, *, name, G, out_dtype):
    S, Ka = a.shape
    _, N = b.shape
    Nb = N // G
    tm = min(TM, S)
    tka = min(Ka, 1024)
    tnb = Nb if Nb <= 1024 else 1024
    nj = Nb // tnb
    ns = S // tm

    def body(a_ref, b_ref, o_ref, acc):
        s = pl.program_id(2)

        @pl.when(s == 0)
        def _():
            acc[...] = jnp.zeros_like(acc)

        acc[...] += _dot(a_ref[...], b_ref[...], TN)

        @pl.when(s == ns - 1)
        def _():
            o_ref[...] = acc[...].astype(o_ref.dtype)

    return pl.pallas_call(
        body, name=name, grid=(Ka // tka, G * nj, ns),
        in_specs=[pl.BlockSpec((tm, tka), lambda i, j, s: (s, i)),
                  pl.BlockSpec((tm, tnb), lambda i, j, s: (s, j))],
        out_specs=pl.BlockSpec((None, tka, tnb), lambda i, j, s: (j // nj, i, j % nj)),
        out_shape=jax.ShapeDtypeStruct((G, Ka, Nb), out_dtype),
        scratch_shapes=[pltpu.VMEM((tka, tnb), F32)],
        compiler_params=_params(3),
    )(a, b)


def mla_pre_fwd(a, gqa, gkva, wuq3, wukv3, gqn, gkn, cos_t, sin_t):
    S = a.shape[0]
    tm = min(TM, S)

    def body(a_ref, gqa_ref, gkva_ref, wuq_ref, wukv_ref, gqn_ref, gkn_ref, cos_ref, sin_ref,
             q_ref, k_ref, v_ref, cq_ref, ckv_ref, cqs, ckvs):
        @pl.when(pl.program_id(1) == 0)
        def _():
            av = a_ref[...]
            cq = (_rms(av[:, :Q_LORA], Q_LORA)[0] * gqa_ref[...]).astype(cqs.dtype)
            ckv = (_rms(av[:, Q_LORA:Q_LORA + KV_LORA], KV_LORA)[0] * gkva_ref[...]).astype(ckvs.dtype)
            cqs[...] = cq
            cq_ref[...] = cq
            ckvs[...] = ckv
            ckv_ref[...] = ckv

        cos_v, sin_v = cos_ref[...], sin_ref[...]
        qn = _rms(_dot(cqs[...], wuq_ref[...]), QK_DIM)[0] * gqn_ref[...]
        qr = jnp.concatenate([qn[:, :NOPE], _rope(qn[:, NOPE:], cos_v, sin_v)], axis=1)
        q_ref[...] = (qr * Q_PRESCALE).astype(q_ref.dtype)
        kvp = _dot(ckvs[...], wukv_ref[...])
        kk = jnp.concatenate([kvp[:, :NOPE], a_ref[:, Q_LORA + KV_LORA:]], axis=1)
        kn = _rms(kk, QK_DIM)[0] * gkn_ref[...]
        k_ref[...] = jnp.concatenate([kn[:, :NOPE], _rope(kn[:, NOPE:], cos_v, sin_v)], axis=1).astype(k_ref.dtype)
        v_ref[...] = kvp[:, NOPE:].astype(v_ref.dtype)

    row = lambda w: pl.BlockSpec((tm, w), lambda i, h: (i, 0))
    vec = lambda w: pl.BlockSpec((1, w), lambda i, h: (0, 0))
    head = lambda w: pl.BlockSpec((None, tm, w), lambda i, h: (h, i, 0))
    return pl.pallas_call(
        body, name="mla_pre_fwd", grid=(S // tm, N_HEADS),
        in_specs=[row(DOWN_PAD), vec(Q_LORA), vec(KV_LORA),
                  pl.BlockSpec((None, Q_LORA, QK_PAD), lambda i, h: (h, 0, 0)),
                  pl.BlockSpec((None, KV_LORA, NOPE + V_DIM), lambda i, h: (h, 0, 0)),
                  vec(QK_PAD), vec(QK_PAD), row(128), row(128)],
        out_specs=[head(QK_PAD), head(QK_PAD), head(V_DIM), row(Q_LORA), row(KV_LORA)],
        out_shape=[jax.ShapeDtypeStruct((N_HEADS, S, QK_PAD), MXU_DTYPE),
                   jax.ShapeDtypeStruct((N_HEADS, S, QK_PAD), MXU_DTYPE),
                   jax.ShapeDtypeStruct((N_HEADS, S, V_DIM), MXU_DTYPE),
                   jax.ShapeDtypeStruct((S, Q_LORA), MXU_DTYPE),
                   jax.ShapeDtypeStruct((S, KV_LORA), MXU_DTYPE)],
        scratch_shapes=[pltpu.VMEM((tm, Q_LORA), MXU_DTYPE), pltpu.VMEM((tm, KV_LORA), MXU_DTYPE)],
        compiler_params=_params(2),
    )(a, gqa, gkva, wuq3, wukv3, gqn, gkn, cos_t, sin_t)


def mla_pre_bwd(dq, dk, dv, a, cq, ckv, gqa, gkva, wuq3, wukv3, gqn, gkn, cos_t, sin_t):
    S = a.shape[0]
    tm = min(TM, S)
    H = N_HEADS

    def body(dq_ref, dk_ref, dv_ref, a_ref, cq_ref, ckv_ref, gqa_ref, gkva_ref, wuq_ref, wukv_ref, gqn_ref, gkn_ref,
             cos_ref, sin_ref, da_ref, dwuq_ref, dwukv_ref, dgqn_ref, dgkn_ref, dgqa_ref, dgkva_ref,
             dcq, dckv, dkpe):
        i, h = pl.program_id(0), pl.program_id(1)

        @pl.when((i == 0) & (h == 0))
        def _():
            for ref in (dwuq_ref, dwukv_ref, dgqn_ref, dgkn_ref, dgqa_ref, dgkva_ref):
                ref[...] = jnp.zeros_like(ref)

        @pl.when(h == 0)
        def _():
            dcq[...] = jnp.zeros_like(dcq)
            dckv[...] = jnp.zeros_like(dckv)
            dkpe[...] = jnp.zeros_like(dkpe)

        cos_v, sin_v = cos_ref[...], sin_ref[...]
        cqv, ckvv = cq_ref[...], ckv_ref[...]
        wuq, wukv = wuq_ref[...], wukv_ref[...]

        qhat, rq = _rms(_dot(cqv, wuq), QK_DIM)
        dqr = dq_ref[...] * SM_SCALE
        dqn = jnp.concatenate([dqr[:, :NOPE], _rope_bwd(dqr[:, NOPE:], cos_v, sin_v)], axis=1)
        dqp, dg = _rms_bwd(dqn, qhat, rq, gqn_ref[...], QK_DIM)
        dgqn_ref[...] += dg
        dqp = dqp.astype(MXU_DTYPE)
        dwuq_ref[h] += _dot(cqv, dqp, TN)
        dcq[...] += _dot(dqp, wuq, NT)

        kvp = _dot(ckvv, wukv)
        kk = jnp.concatenate([kvp[:, :NOPE], a_ref[:, Q_LORA + KV_LORA:]], axis=1)
        khat, rk = _rms(kk, QK_DIM)
        dkr = dk_ref[...] * (1.0 / LOG2E)
        dkn = jnp.concatenate([dkr[:, :NOPE], _rope_bwd(dkr[:, NOPE:], cos_v, sin_v)], axis=1)
        dkk, dg = _rms_bwd(dkn, khat, rk, gkn_ref[...], QK_DIM)
        dgkn_ref[...] += dg
        dkpe[...] += dkk[:, NOPE:]
        dkvp = jnp.concatenate([dkk[:, :NOPE], dv_ref[...]], axis=1).astype(MXU_DTYPE)
        dwukv_ref[h] += _dot(ckvv, dkvp, TN)
        dckv[...] += _dot(dkvp, wukv, NT)

        @pl.when(h == H - 1)
        def _():
            av = a_ref[...]
            ahat, r = _rms(av[:, :Q_LORA], Q_LORA)
            daq, dg = _rms_bwd(dcq[...], ahat, r, gqa_ref[...], Q_LORA)
            dgqa_ref[...] += dg
            ahat, r = _rms(av[:, Q_LORA:Q_LORA + KV_LORA], KV_LORA)
            dakv, dg = _rms_bwd(dckv[...], ahat, r, gkva_ref[...], KV_LORA)
            dgkva_ref[...] += dg
            da_ref[...] = jnp.concatenate([daq, dakv, dkpe[...]], axis=1)

    row = lambda w: pl.BlockSpec((tm, w), lambda i, h: (i, 0))
    vec = lambda w: pl.BlockSpec((1, w), lambda i, h: (0, 0))
    head = lambda w: pl.BlockSpec((None, tm, w), lambda i, h: (h, i, 0))
    full3 = lambda s: pl.BlockSpec(s, lambda i, h: (0, 0, 0))
    return pl.pallas_call(
        body, name="mla_pre_bwd", grid=(S // tm, H),
        in_specs=[head(QK_PAD), head(QK_PAD), head(V_DIM), row(DOWN_PAD), row(Q_LORA), row(KV_LORA),
                  vec(Q_LORA), vec(KV_LORA),
                  pl.BlockSpec((None, Q_LORA, QK_PAD), lambda i, h: (h, 0, 0)),
                  pl.BlockSpec((None, KV_LORA, NOPE + V_DIM), lambda i, h: (h, 0, 0)),
                  vec(QK_PAD), vec(QK_PAD), row(128), row(128)],
        out_specs=[row(DOWN_PAD), full3((H, Q_LORA, QK_PAD)), full3((H, KV_LORA, NOPE + V_DIM)),
                   vec(QK_PAD), vec(QK_PAD), vec(Q_LORA), vec(KV_LORA)],
        out_shape=[jax.ShapeDtypeStruct((S, DOWN_PAD), F32),
                   jax.ShapeDtypeStruct((H, Q_LORA, QK_PAD), F32),
                   jax.ShapeDtypeStruct((H, KV_LORA, NOPE + V_DIM), F32),
                   jax.ShapeDtypeStruct((1, QK_PAD), F32), jax.ShapeDtypeStruct((1, QK_PAD), F32),
                   jax.ShapeDtypeStruct((1, Q_LORA), F32), jax.ShapeDtypeStruct((1, KV_LORA), F32)],
        scratch_shapes=[pltpu.VMEM((tm, Q_LORA), F32), pltpu.VMEM((tm, KV_LORA), F32), pltpu.VMEM((tm, 128), F32)],
        compiler_params=_params(2),
    )(dq, dk, dv, a, cq, ckv, gqa, gkva, wuq3, wukv3, gqn, gkn, cos_t, sin_t)


def _pair_tables(nb, key_major):
    if key_major:
        pairs = [(qi, kj) for kj in range(nb) for qi in range(kj, nb)]
    else:
        pairs = [(qi, ki) for qi in range(nb) for ki in range(qi + 1)]
    return (jnp.asarray(np.array([p[0] for p in pairs], np.int32)),
            jnp.asarray(np.array([p[1] for p in pairs], np.int32)))


def _scores_t(k, q, pk_col, pq_row, masked):
    s = _dot(k, q, NT)
    return jnp.where(pq_row >= pk_col, s, jnp.finfo(F32).min) if masked else s


def attn_fwd(q, k, v, pos_col, pos_row):
    H, S, _ = q.shape
    t = min(TQ, S)
    nb = S // t
    hb = HEADS_FWD
    qt, kt = _pair_tables(nb, key_major=False)

    def body(qt_ref, kt_ref, q_ref, k_ref, v_ref, pk_ref, pq_ref, o_ref, lse_ref, m_s, l_s, acc):
        step = pl.program_id(1)
        qi, ki = qt_ref[step], kt_ref[step]

        @pl.when(ki == 0)
        def _():
            m_s[...] = jnp.full_like(m_s, -jnp.inf)
            l_s[...] = jnp.zeros_like(l_s)
            acc[...] = jnp.zeros_like(acc)

        def update(masked):
            scores = lambda hh: _scores_t(k_ref[hh], q_ref[hh], pk_ref[...], pq_ref[...], masked)
            s_next = scores(0)
            for hh in range(hb):
                s = s_next
                if hh + 1 < hb:
                    s_next = scores(hh + 1)
                m_old = m_s[hh]
                m_new = jnp.maximum(m_old, jnp.max(s, axis=0, keepdims=True))
                p = jnp.exp2(s - m_new)
                alpha = jnp.exp2(m_old - m_new)
                l_s[hh] = alpha * l_s[hh] + jnp.sum(p, axis=0, keepdims=True)
                acc[hh] = alpha * acc[hh] + _dot(v_ref[hh], p, TN)
                m_s[hh] = m_new

        @pl.when(ki < qi)
        def _():
            update(False)

        @pl.when(ki == qi)
        def _():
            update(True)
            for hh in range(hb):
                o_ref[:, hh * V_DIM:(hh + 1) * V_DIM] = (acc[hh] / l_s[hh]).T
                lse_ref[hh] = m_s[hh] + jnp.log(l_s[hh]) * LOG2E

    grid_spec = pltpu.PrefetchScalarGridSpec(
        num_scalar_prefetch=2, grid=(H // hb, qt.shape[0]),
        in_specs=[pl.BlockSpec((hb, t, QK_PAD), lambda h, s, qt, kt: (h, qt[s], 0)),
                  pl.BlockSpec((hb, t, QK_PAD), lambda h, s, qt, kt: (h, kt[s], 0)),
                  pl.BlockSpec((hb, t, V_DIM), lambda h, s, qt, kt: (h, kt[s], 0)),
                  pl.BlockSpec((t, 1), lambda h, s, qt, kt: (kt[s], 0)),
                  pl.BlockSpec((1, t), lambda h, s, qt, kt: (0, qt[s]))],
        out_specs=[pl.BlockSpec((t, hb * V_DIM), lambda h, s, qt, kt: (qt[s], h)),
                   pl.BlockSpec((hb, 1, t), lambda h, s, qt, kt: (h, 0, qt[s]))],
        scratch_shapes=[pltpu.VMEM((hb, 1, t), F32), pltpu.VMEM((hb, 1, t), F32), pltpu.VMEM((hb, V_DIM, t), F32)])
    return pl.pallas_call(
        body, name="attn_fwd", grid_spec=grid_spec,
        out_shape=[jax.ShapeDtypeStruct((S, H * V_DIM), F32), jax.ShapeDtypeStruct((H, 1, S), F32)],
        compiler_params=_params(2),
    )(qt, kt, q, k, v, pos_col, pos_row)


def attn_delta(o, do):
    S = o.shape[0]
    t = min(TQ, S)

    def body(o_ref, do_ref, d_ref):
        d_ref[...] = jnp.sum((o_ref[...] * do_ref[...]).T, axis=0, keepdims=True)

    blk = pl.BlockSpec((t, V_DIM), lambda h, i: (i, h))
    return pl.pallas_call(
        body, name="attn_delta", grid=(N_HEADS, S // t),
        in_specs=[blk, blk],
        out_specs=pl.BlockSpec((None, 1, t), lambda h, i: (h, 0, i)),
        out_shape=jax.ShapeDtypeStruct((N_HEADS, 1, S), F32),
        compiler_params=_params(2),
    )(o, do)


def attn_bwd(q, k, v, do, lse, delta, pos_col, pos_row):
    H, S, _ = q.shape
    t = min(TQ, S)
    nb = S // t
    qt, kt = _pair_tables(nb, key_major=True)
    tc = min(BWD_CHUNK, t)

    def body(qt_ref, kt_ref, q_ref, k_ref, v_ref, do_ref, lse_ref, dl_ref, pk_ref, pq_ref, dq_ref, dk_ref, dv_ref):
        step = pl.program_id(1)
        qi, kj = qt_ref[step], kt_ref[step]

        @pl.when(step == 0)
        def _():
            dq_ref[...] = jnp.zeros_like(dq_ref)

        @pl.when(qi == kj)
        def _():
            dk_ref[...] = jnp.zeros_like(dk_ref)
            dv_ref[...] = jnp.zeros_like(dv_ref)

        def update(masked):
            kv_, vv = k_ref[...], v_ref[...]

            def first_matmuls(c):
                cols = slice(c * tc, (c + 1) * tc)
                qc = q_ref[cols, :]
                doc = do_ref[cols, :].astype(MXU_DTYPE)
                return qc, doc, _scores_t(kv_, qc, pk_ref[...], pq_ref[:, cols], masked), _dot(vv, doc, NT)

            nxt = first_matmuls(0)
            for c in range(t // tc):
                qc, doc, s, dp = nxt
                if c + 1 < t // tc:
                    nxt = first_matmuls(c + 1)
                cols = slice(c * tc, (c + 1) * tc)
                p = jnp.exp2(s - lse_ref[:, cols])
                ds = (p * (dp - dl_ref[:, cols])).astype(MXU_DTYPE)
                dv_ref[...] += _dot(p, doc)
                dk_ref[...] += _dot(ds, qc)
                rows = pl.ds(pl.multiple_of(qi * t + c * tc, tc), tc)
                dq_ref[rows, :] += _dot(ds, kv_, TN)

        @pl.when(qi == kj)
        def _():
            update(True)

        @pl.when(qi != kj)
        def _():
            update(False)

    q_idx = lambda h, s, qt, kt: (h, qt[s], 0)
    k_idx = lambda h, s, qt, kt: (h, kt[s], 0)
    row_idx = lambda h, s, qt, kt: (h, 0, qt[s])
    grid_spec = pltpu.PrefetchScalarGridSpec(
        num_scalar_prefetch=2, grid=(H, qt.shape[0]),
        in_specs=[pl.BlockSpec((None, t, QK_PAD), q_idx),
                  pl.BlockSpec((None, t, QK_PAD), k_idx),
                  pl.BlockSpec((None, t, V_DIM), k_idx),
                  pl.BlockSpec((t, V_DIM), lambda h, s, qt, kt: (qt[s], h)),
                  pl.BlockSpec((None, 1, t), row_idx),
                  pl.BlockSpec((None, 1, t), row_idx),
                  pl.BlockSpec((t, 1), lambda h, s, qt, kt: (kt[s], 0)),
                  pl.BlockSpec((1, t), lambda h, s, qt, kt: (0, qt[s]))],
        out_specs=[pl.BlockSpec((None, S, QK_PAD), lambda h, s, qt, kt: (h, 0, 0)),
                   pl.BlockSpec((None, t, QK_PAD), k_idx),
                   pl.BlockSpec((None, t, V_DIM), k_idx)])
    return pl.pallas_call(
        body, name="attn_bwd", grid_spec=grid_spec,
        out_shape=[jax.ShapeDtypeStruct((H, S, QK_PAD), F32), jax.ShapeDtypeStruct((H, S, QK_PAD), F32),
                   jax.ShapeDtypeStruct((H, S, V_DIM), F32)],
        compiler_params=_params(2),
    )(qt, kt, q, k, v, do, lse, delta, pos_col, pos_row)


def _conv_specs(S, tr, tc):
    nc = D_MODEL // tc
    hb = tr // 8
    main = lambda third: pl.BlockSpec((tr, tc), lambda c, r: (r, third * nc + c))
    prev = lambda third: pl.BlockSpec((8, tc), lambda c, r: (jnp.maximum(r * hb - 1, 0), third * nc + c))
    nxt = lambda third: pl.BlockSpec((8, tc), lambda c, r: (jnp.minimum((r + 1) * hb, S // 8 - 1), third * nc + c))
    return main, prev, nxt


def _conv_taps(gc, uu, w_ref, first):
    u2 = gc * uu
    rows = lax.broadcasted_iota(jnp.int32, u2.shape, 0)
    u2 = jnp.where((rows < 8) & first, 0.0, u2)
    s1 = pltpu.roll(u2, 1, 0)
    s2 = pltpu.roll(u2, 2, 0)
    u3 = w_ref[2:3, :] * u2 + w_ref[1:2, :] * s1 + w_ref[0:1, :] * s2
    return u2, s1, s2, u3


def conv_fwd(bcu, cw):
    S = bcu.shape[0]
    tr, tc = min(TROW, S), TCH
    main, prev, _ = _conv_specs(S, tr, tc)

    def body(gb_ref, gc_ref, u_ref, gch_ref, uh_ref, w_ref, z_ref):
        gc = jnp.concatenate([gch_ref[...], gc_ref[...]], axis=0)
        uu = jnp.concatenate([uh_ref[...], u_ref[...]], axis=0)
        u3 = _conv_taps(gc, uu, w_ref, pl.program_id(1) == 0)[3]
        z_ref[...] = (gb_ref[...] * u3[8:]).astype(z_ref.dtype)

    return pl.pallas_call(
        body, name="conv_fwd", grid=(D_MODEL // tc, S // tr),
        in_specs=[main(0), main(1), main(2), prev(1), prev(2), pl.BlockSpec((3, tc), lambda c, r: (0, c))],
        out_specs=pl.BlockSpec((tr, tc), lambda c, r: (r, c)),
        out_shape=jax.ShapeDtypeStruct((S, D_MODEL), MXU_DTYPE),
        compiler_params=_params(2),
    )(bcu, bcu, bcu, bcu, bcu, cw)


def conv_bwd(dz, bcu, cw):
    S = bcu.shape[0]
    tr, tc = min(TROW, S), TCH
    nr = S // tr
    main, prev, nxt = _conv_specs(S, tr, tc)

    def body(dz_ref, dzn_ref, gb_ref, gbn_ref, gc_ref, u_ref, gch_ref, uh_ref, w_ref,
             dgb_ref, dgc_ref, du_ref, dw_ref):
        r = pl.program_id(1)
        gcv, uv = gc_ref[...], u_ref[...]
        gc = jnp.concatenate([gch_ref[...], gcv], axis=0)
        uu = jnp.concatenate([uh_ref[...], uv], axis=0)
        u2, s1, s2, u3 = _conv_taps(gc, uu, w_ref, r == 0)
        dzv = dz_ref[...]
        du3 = jnp.concatenate([dzv * gb_ref[...], dzn_ref[...] * gbn_ref[...]], axis=0)
        rows = lax.broadcasted_iota(jnp.int32, du3.shape, 0)
        du3 = jnp.where((rows >= tr) & (r == nr - 1), 0.0, du3)
        n1 = pltpu.roll(du3, tr + 8 - 1, 0)
        n2 = pltpu.roll(du3, tr + 8 - 2, 0)
        du2 = (w_ref[2:3, :] * du3 + w_ref[1:2, :] * n1 + w_ref[0:1, :] * n2)[:tr]
        dgb_ref[...] = (dzv * u3[8:]).astype(dgb_ref.dtype)
        dgc_ref[...] = (du2 * uv).astype(dgc_ref.dtype)
        du_ref[...] = (du2 * gcv).astype(du_ref.dtype)
        d3 = du3[:tr]
        taps = [jnp.sum(d3 * t[8:], axis=0, keepdims=True) for t in (s2, s1, u2)]

        @pl.when(r == 0)
        def _():
            for kk in range(3):
                dw_ref[kk:kk + 1, :] = taps[kk]

        @pl.when(r != 0)
        def _():
            for kk in range(3):
                dw_ref[kk:kk + 1, :] += taps[kk]

    out = pl.BlockSpec((tr, tc), lambda c, r: (r, c))
    nxt_dz = pl.BlockSpec((8, tc), lambda c, r: (jnp.minimum((r + 1) * (tr // 8), S // 8 - 1), c))
    act = jax.ShapeDtypeStruct((S, D_MODEL), MXU_DTYPE)
    return pl.pallas_call(
        body, name="conv_bwd", grid=(D_MODEL // tc, nr),
        in_specs=[out, nxt_dz, main(0), nxt(0), main(1), main(2), prev(1), prev(2),
                  pl.BlockSpec((3, tc), lambda c, r: (0, c))],
        out_specs=[out, out, out, pl.BlockSpec((3, tc), lambda c, r: (0, c))],
        out_shape=[act, act, act, jax.ShapeDtypeStruct((3, D_MODEL), F32)],
        compiler_params=_params(2),
    )(dz, dz, bcu, bcu, bcu, bcu, bcu, bcu, cw)


def loss_head(y, target):
    S, D = y.shape
    tm = min(TM, S)

    def body(y_ref, t_ref, l_ref, dy_ref):
        err = y_ref[...] - t_ref[...]
        dy_ref[...] = err / D
        part = jnp.full((1, 128), jnp.sum(err * err), F32)

        @pl.when(pl.program_id(0) == 0)
        def _():
            l_ref[...] = part

        @pl.when(pl.program_id(0) != 0)
        def _():
            l_ref[...] += part

    blk = pl.BlockSpec((tm, D), lambda i: (i, 0))
    return pl.pallas_call(
        body, name="loss_head", grid=(S // tm,),
        in_specs=[blk, blk],
        out_specs=[pl.BlockSpec((1, 128), lambda i: (0, 0)), blk],
        out_shape=[jax.ShapeDtypeStruct((1, 128), F32), jax.ShapeDtypeStruct((S, D), F32)],
        compiler_params=_params(1),
    )(y, target)


def adamw(parts, w, m, v, *, name):
    R, C = w.shape
    tr = R
    while tr * C * 4 > (1 << 20) and tr % 32 == 0:
        tr //= 2

    def body(p_ref, w_ref, m_ref, v_ref, g_ref, d_ref, mo_ref, vo_ref):
        g = p_ref[0].astype(F32)
        for d in range(1, N_DEV):
            g = g + p_ref[d].astype(F32)
        m_new = ADAM_B1 * m_ref[...] + (1.0 - ADAM_B1) * g
        v_new = ADAM_B2 * v_ref[...] + (1.0 - ADAM_B2) * (g * g)
        m_hat = m_new / (1.0 - ADAM_B1 ** ADAM_STEP)
        v_hat = v_new / (1.0 - ADAM_B2 ** ADAM_STEP)
        g_ref[...] = g
        d_ref[...] = -ADAM_LR * (m_hat / (jnp.sqrt(v_hat) + ADAM_EPS) + ADAM_WD * w_ref[...])
        mo_ref[...] = m_new
        vo_ref[...] = v_new

    blk = pl.BlockSpec((tr, C), lambda i: (i, 0))
    return pl.pallas_call(
        body, name=name, grid=(R // tr,),
        in_specs=[pl.BlockSpec((N_DEV, tr, C), lambda i: (0, i, 0)), blk, blk, blk],
        out_specs=[blk, blk, blk, blk],
        out_shape=[jax.ShapeDtypeStruct((R, C), F32)] * 4,
        compiler_params=_params(1),
    )(parts, w, m, v)


def _mesh_place():
    x, y, c = (lax.axis_index(n) for n in MESH_AXES)
    return x, y, c, 4 * x + 2 * y + c


def _peer(x, y, c, d):
    px = 1 - x if d & 4 else x
    py = 1 - y if d & 2 else y
    pc = 1 - c if d & 1 else c
    return (px, py, pc), 4 * px + 2 * py + pc


def _exchange(srcs, name, scatter):
    n = len(srcs)
    any_spec = pl.BlockSpec(memory_space=pl.ANY)

    def body(*refs):
        ins, outs = refs[:n], refs[n:2 * n]
        send_sems, recv_sems, local_sems = refs[2 * n:]
        x, y, c, me = _mesh_place()
        for a in range(n):
            mine = ins[a].at[me] if scatter else ins[a]
            pltpu.make_async_copy(mine, outs[a].at[me], local_sems.at[a]).start()
            for d in range(1, N_DEV):
                peer, peer_lin = _peer(x, y, c, d)
                pltpu.make_async_remote_copy(
                    src_ref=ins[a].at[peer_lin] if scatter else ins[a], dst_ref=outs[a].at[me],
                    send_sem=send_sems.at[a], recv_sem=recv_sems.at[a],
                    device_id=peer, device_id_type=pl.DeviceIdType.MESH).start()
        for a in range(n):
            mine = ins[a].at[me] if scatter else ins[a]
            pltpu.make_async_copy(mine, outs[a].at[me], local_sems.at[a]).wait()
            seven = outs[a].at[pl.ds(0, N_DEV - 1)]
            drain = pltpu.make_async_remote_copy(
                src_ref=seven, dst_ref=seven, send_sem=send_sems.at[a], recv_sem=recv_sems.at[a],
                device_id=(x, y, c), device_id_type=pl.DeviceIdType.MESH)
            drain.wait_send()
            drain.wait_recv()

    block = (lambda s: s.shape[1:]) if scatter else (lambda s: s.shape)
    return pl.pallas_call(
        body, name=name,
        in_specs=[any_spec] * n, out_specs=[any_spec] * n,
        out_shape=[jax.ShapeDtypeStruct((N_DEV,) + tuple(block(s)), s.dtype) for s in srcs],
        scratch_shapes=[pltpu.SemaphoreType.DMA((n,)), pltpu.SemaphoreType.DMA((n,)), pltpu.SemaphoreType.DMA((n,))],
    )(*srcs)


def _rope_tables(pos):
    inv_freq = ROPE_THETA ** (-jnp.arange(0, ROPE, 2, dtype=F32) / ROPE)
    ang = pos.astype(F32)[:, None] * inv_freq
    cos, sin = jnp.cos(ang), jnp.sin(ang)
    pad = jnp.zeros((pos.shape[0], 128 - ROPE), F32)
    return jnp.concatenate([cos, cos, pad + 1.0], axis=1), jnp.concatenate([-sin, sin, pad], axis=1)


def _pad_last(w, n):
    return jnp.pad(w, [(0, 0)] * (w.ndim - 1) + [(0, n - w.shape[-1])])


def kernel(x, positions, g_mix, g_mlp, attn_w_down, attn_g_q_a, attn_g_kv_a, attn_w_uq, attn_w_ukv, attn_g_qnorm, attn_g_knorm, attn_w_o, conv_w_in, conv_w, conv_w_out, mlp_w1, mlp_w2, loss_target, m_g_mix, m_g_mlp, m_attn_w_down, m_attn_g_q_a, m_attn_g_kv_a, m_attn_w_uq, m_attn_w_ukv, m_attn_g_qnorm, m_attn_g_knorm, m_attn_w_o, m_conv_w_in, m_conv_w, m_conv_w_out, m_mlp_w1, m_mlp_w2, v_g_mix, v_g_mlp, v_attn_w_down, v_attn_g_q_a, v_attn_g_kv_a, v_attn_w_uq, v_attn_w_ukv, v_attn_g_qnorm, v_attn_g_knorm, v_attn_w_o, v_conv_w_in, v_conv_w, v_conv_w_out, v_mlp_w1, v_mlp_w2):
    weights = dict(g_mix=g_mix, g_mlp=g_mlp, attn_w_down=attn_w_down, attn_g_q_a=attn_g_q_a, attn_g_kv_a=attn_g_kv_a,
                   attn_w_uq=attn_w_uq, attn_w_ukv=attn_w_ukv, attn_g_qnorm=attn_g_qnorm, attn_g_knorm=attn_g_knorm,
                   attn_w_o=attn_w_o, conv_w_in=conv_w_in, conv_w=conv_w, conv_w_out=conv_w_out, mlp_w1=mlp_w1, mlp_w2=mlp_w2)
    mom1 = dict(g_mix=m_g_mix, g_mlp=m_g_mlp, attn_w_down=m_attn_w_down, attn_g_q_a=m_attn_g_q_a, attn_g_kv_a=m_attn_g_kv_a,
                attn_w_uq=m_attn_w_uq, attn_w_ukv=m_attn_w_ukv, attn_g_qnorm=m_attn_g_qnorm, attn_g_knorm=m_attn_g_knorm,
                attn_w_o=m_attn_w_o, conv_w_in=m_conv_w_in, conv_w=m_conv_w, conv_w_out=m_conv_w_out, mlp_w1=m_mlp_w1, mlp_w2=m_mlp_w2)
    mom2 = dict(g_mix=v_g_mix, g_mlp=v_g_mlp, attn_w_down=v_attn_w_down, attn_g_q_a=v_attn_g_q_a, attn_g_kv_a=v_attn_g_kv_a,
                attn_w_uq=v_attn_w_uq, attn_w_ukv=v_attn_w_ukv, attn_g_qnorm=v_attn_g_qnorm, attn_g_knorm=v_attn_g_knorm,
                attn_w_o=v_attn_w_o, conv_w_in=v_conv_w_in, conv_w=v_conv_w, conv_w_out=v_conv_w_out, mlp_w1=v_mlp_w1, mlp_w2=v_mlp_w2)
    big = ["attn_w_down", "attn_w_uq", "attn_w_ukv", "attn_w_o", "conv_w_in", "conv_w", "conv_w_out", "mlp_w1", "mlp_w2"]
    small = ["g_mix", "g_mlp", "attn_g_q_a", "attn_g_kv_a", "attn_g_qnorm", "attn_g_knorm"]
    order = ["g_mix", "g_mlp", "attn_w_down", "attn_g_q_a", "attn_g_kv_a", "attn_w_uq", "attn_w_ukv", "attn_g_qnorm",
             "attn_g_knorm", "attn_w_o", "conv_w_in", "conv_w", "conv_w_out", "mlp_w1", "mlp_w2"]

    xs = x[0]
    pos = positions[0]
    target = loss_target[0]
    S = xs.shape[0]
    depth = g_mix.shape[0]
    cos_t, sin_t = _rope_tables(pos)
    pos_col, pos_row = pos.reshape(S, 1), pos.reshape(1, S)

    keys, shards = [], []
    for name in big:
        for l in range(weights[name].shape[0]):
            keys.append((name, l))
            shards.append(weights[name][l] if name == "conv_w" else weights[name][l].astype(WIRE_DTYPE))
    full = dict(zip(keys, _exchange(shards, "gather_weights", scatter=False)))

    def rows(name, l):
        g = full[(name, l)]
        return g.reshape(g.shape[0] * g.shape[1], g.shape[2])

    saved = []
    for i in range(depth):
        l = i // 2
        rec = {"x0": xs}
        if i % 2 == 0:
            wd3 = _pad_last(rows("attn_w_down", l), DOWN_PAD)[None]
            wuq3 = _pad_last(full[("attn_w_uq", l)], QK_PAD)
            wukv3 = full[("attn_w_ukv", l)]
            gqn = _pad_last(attn_g_qnorm[l][None], QK_PAD)
            gkn = _pad_last(attn_g_knorm[l][None], QK_PAD)
            gqa, gkva = attn_g_q_a[l][None], attn_g_kv_a[l][None]
            h, a = rms_mm(xs, g_mix[i], wd3, name="attn_down")
            q, k, v, cq, ckv = mla_pre_fwd(a, gqa, gkva, wuq3, wukv3, gqn, gkn, cos_t, sin_t)
            o, lse = attn_fwd(q, k, v, pos_col, pos_row)
            x1 = mm_res(o, rows("attn_w_o", l), xs, name="attn_out")
            rec.update(h=h, a=a, q=q, k=k, v=v, cq=cq, ckv=ckv, o=o, lse=lse, wd3=wd3, wuq3=wuq3, wukv3=wukv3,
                       gqn=gqn, gkn=gkn, gqa=gqa, gkva=gkva)
        else:
            cw = full[("conv_w", l)].transpose(1, 0, 2).reshape(3, D_MODEL)
            h, bcu = rms_mm(xs, g_mix[i], full[("conv_w_in", l)], name="conv_in")
            z = conv_fwd(bcu, cw)
            x1 = mm_res(z, rows("conv_w_out", l), xs, name="conv_out")
            rec.update(h=h, bcu=bcu, z=z, cw=cw)
        h2, act = rms_mm(x1, g_mlp[i], full[("mlp_w1", i)], name="mlp_up", mlp=True)
        xs = mm_res(act, rows("mlp_w2", i), x1, name="mlp_down")
        rec.update(x1=x1, h2=h2, act=act)
        saved.append(rec)

    sq, dx = loss_head(xs, target)
    loss = lax.psum(sq[0, 0] * (0.5 / D_MODEL), MESH_AXES)

    grads = {name: [None] * weights[name].shape[0] for name in order}
    for i in reversed(range(depth)):
        l = i // 2
        rec = saved[i]
        grads["mlp_w2"][i] = mm_tn(rec["act"], dx, name="mlp_down_dw", G=1, out_dtype=WIRE_DTYPE).reshape(N_DEV, -1, D_MODEL)
        du = mm_nt(dx, rows("mlp_w2", i)[None], name="mlp_down_dx", epi="mlp_du", u=rec["act"])
        grads["mlp_w1"][i] = mm_tn(rec["h2"], du, name="mlp_up_dw", G=N_DEV, out_dtype=WIRE_DTYPE)
        dx1, dg = mm_nt(du, full[("mlp_w1", i)], name="mlp_up_dx", epi="rms_bwd", x=rec["x1"], g=g_mlp[i], dx=dx)
        grads["g_mlp"][i] = dg[0]
        if i % 2 == 0:
            grads["attn_w_o"][l] = mm_tn(rec["o"], dx1, name="attn_out_dw", G=1, out_dtype=WIRE_DTYPE).reshape(N_DEV, -1, D_MODEL)
            do = mm_nt(dx1, rows("attn_w_o", l)[None], name="attn_out_dx")
            dq, dk, dv = attn_bwd(rec["q"], rec["k"], rec["v"], do, rec["lse"], attn_delta(rec["o"], do), pos_col, pos_row)
            da, dwuq, dwukv, dgqn, dgkn, dgqa, dgkva = mla_pre_bwd(
                dq, dk, dv, rec["a"], rec["cq"], rec["ckv"], rec["gqa"], rec["gkva"], rec["wuq3"], rec["wukv3"],
                rec["gqn"], rec["gkn"], cos_t, sin_t)
            grads["attn_w_uq"][l] = dwuq[:, :, :QK_DIM].astype(WIRE_DTYPE)
            grads["attn_w_ukv"][l] = dwukv.astype(WIRE_DTYPE)
            grads["attn_g_qnorm"][l] = dgqn[0, :QK_DIM]
            grads["attn_g_knorm"][l] = dgkn[0, :QK_DIM]
            grads["attn_g_q_a"][l] = dgqa[0]
            grads["attn_g_kv_a"][l] = dgkva[0]
            dwd = mm_tn(rec["h"], da, name="attn_down_dw", G=1, out_dtype=WIRE_DTYPE)
            grads["attn_w_down"][l] = dwd[0, :, :DOWN].reshape(N_DEV, -1, DOWN)
            dx, dg = mm_nt(da, rec["wd3"], name="attn_down_dx", epi="rms_bwd", x=rec["x0"], g=g_mix[i], dx=dx1)
        else:
            grads["conv_w_out"][l] = mm_tn(rec["z"], dx1, name="conv_out_dw", G=1, out_dtype=WIRE_DTYPE).reshape(N_DEV, -1, D_MODEL)
            dz = mm_nt(dx1, rows("conv_w_out", l)[None], name="conv_out_dx")
            dgb, dgc, du_, dcw = conv_bwd(dz, rec["bcu"], rec["cw"])
            grads["conv_w"][l] = dcw.reshape(3, N_DEV, -1).transpose(1, 0, 2)
            dbcu = jnp.concatenate([dgb, dgc, du_], axis=1)
            grads["conv_w_in"][l] = mm_tn(rec["h"], dbcu, name="conv_in_dw", G=N_DEV, out_dtype=WIRE_DTYPE)
            dx, dg = mm_nt(dbcu, full[("conv_w_in", l)], name="conv_in_dx", epi="rms_bwd", x=rec["x0"], g=g_mix[i], dx=dx1)
        grads["g_mix"][i] = dg[0]

    sizes = [weights[name].size for name in small]
    n_small = sum(sizes)
    rows_small = -(-n_small // (8 * 128)) * 8

    def pack(tree):
        flat = jnp.concatenate([jnp.stack(tree[name]).reshape(-1) if isinstance(tree[name], list) else tree[name].reshape(-1)
                                for name in small])
        return jnp.pad(flat, (0, rows_small * 128 - n_small)).reshape(rows_small, 128)

    sends = [jnp.stack(grads[name], axis=1) for name in big]
    sends.append(jnp.broadcast_to(pack(grads)[None], (N_DEV, rows_small, 128)))
    parts = _exchange(sends, "scatter_grads", scatter=True)

    out = {}
    for name, part in zip(big, parts[:-1]):
        w = weights[name]
        flat = lambda t: t.reshape(-1, t.shape[-1])
        res = adamw(part.reshape(N_DEV, -1, w.shape[-1]), flat(w), flat(mom1[name]), flat(mom2[name]), name="adamw_" + name)
        out[name] = [r.reshape(w.shape) for r in res]
    res = adamw(parts[-1], pack(weights), pack(mom1), pack(mom2), name="adamw_gains")
    offset = 0
    for name, size in zip(small, sizes):
        out[name] = [r.reshape(-1)[offset:offset + size].reshape(weights[name].shape) for r in res]
        offset += size

    return (loss, dx[None], *[out[n][0] for n in order], *[out[n][1] for n in order],
            *[out[n][2] for n in order], *[out[n][3] for n in order])
```

```python
import jax
import jax.numpy as jnp
import numpy as np
from jax import lax
from jax.experimental import pallas as pl
from jax.experimental.pallas import tpu as pltpu

F32 = jnp.float32
MXU_DTYPE = jnp.bfloat16
WIRE_DTYPE = jnp.bfloat16

D_MODEL = 1024
N_HEADS = 8
NOPE = 128
ROPE = 64
QK_DIM = NOPE + ROPE
QK_PAD = 256
V_DIM = 128
Q_LORA = 256
KV_LORA = 128
DOWN = Q_LORA + KV_LORA + ROPE
DOWN_PAD = 512
ROPE_THETA = 10000.0
EPS = 1e-6
SM_SCALE = QK_DIM ** -0.5
LOG2E = 1.4426950408889634
Q_PRESCALE = SM_SCALE * LOG2E
ADAM_LR, ADAM_B1, ADAM_B2, ADAM_EPS, ADAM_WD, ADAM_STEP = 0.001, 0.9, 0.999, 1e-08, 0.01, 10
N_DEV = 8
MESH_AXES = ("x", "y", "c")

TM = 512
TM_WIDE = 1024
TQ = 512
HEADS_FWD = 8
TQ_BWD = 1024
BWD_CHUNK = 256
TROW = 512
TCH = 512
VMEM_LIMIT = 48 << 20

NN = (((1,), (0,)), ((), ()))
NT = (((1,), (1,)), ((), ()))
TN = (((0,), (0,)), ((), ()))


def _dot(a, b, dims=NN):
    return lax.dot_general(a.astype(MXU_DTYPE), b.astype(MXU_DTYPE), dims, preferred_element_type=F32)


def _params(n_axes):
    return pltpu.CompilerParams(dimension_semantics=("arbitrary",) * n_axes, vmem_limit_bytes=VMEM_LIMIT)


def _rms(xv, n):
    r = lax.rsqrt(jnp.sum(xv * xv, axis=-1, keepdims=True) / n + EPS)
    return xv * r, r


def _rms_bwd(dy, xhat, r, g, n):
    dg = jnp.sum(dy * xhat, axis=0, keepdims=True)
    dxh = dy * g
    dx = r * (dxh - xhat * (jnp.sum(dxh * xhat, axis=-1, keepdims=True) / n))
    return dx, dg


def _swap_halves(t):
    lane = lax.broadcasted_iota(jnp.int32, t.shape, 1)
    return jnp.where(lane < ROPE // 2, pltpu.roll(t, 128 - ROPE // 2, 1), pltpu.roll(t, ROPE // 2, 1))


def _rope(t, cos_t, sin_t):
    return t * cos_t + _swap_halves(t) * sin_t


def _rope_bwd(dout, cos_t, sin_t):
    return dout * cos_t + _swap_halves(dout * sin_t)


def rms_mm(x, g, w3, *, name, mlp=False):
    S, D = x.shape
    G, _, Nb = w3.shape
    tm = min(TM_WIDE, S)
    tn = Nb if Nb <= 512 else 512
    nj = Nb // tn
    N = G * Nb

    def body(x_ref, g_ref, w_ref, h_ref, *rest):
        hs = rest[-1]

        @pl.when(pl.program_id(1) == 0)
        def _():
            xv = x_ref[...]
            r = lax.rsqrt(jnp.mean(xv * xv, axis=-1, keepdims=True) + EPS)
            h = (xv * r * g_ref[...]).astype(hs.dtype)
            hs[...] = h
            h_ref[...] = h

        acc = lax.dot_general(hs[...], w_ref[...].astype(hs.dtype), NN, preferred_element_type=F32)
        if mlp:
            rl = jnp.maximum(acc, 0.0)
            rest[0][...] = (rl * rl).astype(rest[0].dtype)
        else:
            rest[0][...] = acc

    out_shape = [jax.ShapeDtypeStruct((S, D), MXU_DTYPE), jax.ShapeDtypeStruct((S, N), MXU_DTYPE if mlp else F32)]
    out_specs = [pl.BlockSpec((tm, D), lambda i, j: (i, 0)), pl.BlockSpec((tm, tn), lambda i, j: (i, j))]
    return pl.pallas_call(
        body, name=name, grid=(S // tm, G * nj),
        in_specs=[pl.BlockSpec((tm, D), lambda i, j: (i, 0)),
                  pl.BlockSpec((1, D), lambda i, j: (0, 0)),
                  pl.BlockSpec((None, D, tn), lambda i, j: (j // nj, 0, j % nj))],
        out_specs=out_specs, out_shape=out_shape,
        scratch_shapes=[pltpu.VMEM((tm, D), MXU_DTYPE)],
        compiler_params=_params(2),
    )(x, g.reshape(1, D), w3)


def mm_res(a, w, res, *, name):
    S, K = a.shape
    _, N = w.shape
    tm = min(TM_WIDE, S)
    tk = min(K, 1024)
    nk = K // tk

    def body(a_ref, w_ref, r_ref, o_ref, acc):
        k = pl.program_id(1)

        @pl.when(k == 0)
        def _():
            acc[...] = jnp.zeros_like(acc)

        acc[...] += _dot(a_ref[...], w_ref[...])

        @pl.when(k == nk - 1)
        def _():
            o_ref[...] = r_ref[...] + acc[...]

    return pl.pallas_call(
        body, name=name, grid=(S // tm, nk),
        in_specs=[pl.BlockSpec((tm, tk), lambda i, k: (i, k)),
                  pl.BlockSpec((tk, N), lambda i, k: (k, 0)),
                  pl.BlockSpec((tm, N), lambda i, k: (i, 0))],
        out_specs=pl.BlockSpec((tm, N), lambda i, k: (i, 0)),
        out_shape=jax.ShapeDtypeStruct((S, N), F32),
        scratch_shapes=[pltpu.VMEM((tm, N), F32)],
        compiler_params=_params(2),
    )(a, w, res)


def mm_nt(a, w3, *, name, epi="plain", u=None, x=None, g=None, dx=None):
    S, N = a.shape
    G, Ko, Nb = w3.shape
    assert N == G * Nb
    tm = min(TM if epi == "rms_bwd" else TM_WIDE, S)
    tkk = Nb if Nb <= 1024 else 1024
    nk = Nb // tkk
    ks = G * nk
    tko = Ko if epi == "rms_bwd" else min(Ko, 512)

    def body(a_ref, w_ref, *rest):
        acc = rest[-1]
        i, k = pl.program_id(0), pl.program_id(2)

        @pl.when(k == 0)
        def _():
            acc[...] = jnp.zeros_like(acc)

        acc[...] += _dot(a_ref[...], w_ref[...], NT)

        @pl.when(k == ks - 1)
        def _():
            if epi == "plain":
                rest[0][...] = acc[...]
            elif epi == "mlp_du":
                u_ref, o_ref = rest[0], rest[1]
                o_ref[...] = (acc[...] * (2.0 * jnp.sqrt(u_ref[...].astype(F32)))).astype(o_ref.dtype)
            else:
                x_ref, g_ref, dx_ref, o_ref, dg_ref = rest[:5]
                xhat, r = _rms(x_ref[...], Ko)
                dxb, dg = _rms_bwd(acc[...], xhat, r, g_ref[...], Ko)
                o_ref[...] = dx_ref[...] + dxb

                @pl.when(i == 0)
                def _():
                    dg_ref[...] = dg

                @pl.when(i != 0)
                def _():
                    dg_ref[...] += dg

    in_specs = [pl.BlockSpec((tm, tkk), lambda i, j, k: (i, k)),
                pl.BlockSpec((None, tko, tkk), lambda i, j, k: (k // nk, j, k % nk))]
    args = [a, w3]
    tile = pl.BlockSpec((tm, tko), lambda i, j, k: (i, j))
    if epi == "plain":
        out_shape, out_specs = jax.ShapeDtypeStruct((S, Ko), F32), tile
    elif epi == "mlp_du":
        in_specs.append(tile)
        args.append(u)
        out_shape, out_specs = jax.ShapeDtypeStruct((S, Ko), MXU_DTYPE), tile
    else:
        vec = pl.BlockSpec((1, Ko), lambda i, j, k: (0, 0))
        in_specs += [tile, vec, tile]
        args += [x, g.reshape(1, Ko), dx]
        out_shape = [jax.ShapeDtypeStruct((S, Ko), F32), jax.ShapeDtypeStruct((1, Ko), F32)]
        out_specs = [tile, vec]
    return pl.pallas_call(
        body, name=name, grid=(S // tm, Ko // tko, ks),
        in_specs=in_specs, out_specs=out_specs, out_shape=out_shape,
        scratch_shapes=[pltpu.VMEM((tm, tko), F32)],
        compiler_params=_params(3),
    )(*args)


def _resident(shape):
    return pl.BlockSpec(shape, lambda i: (0,) * len(shape))


def norm_matmul(x, g, w3, *, name, mlp=False):
    S, D = x.shape
    G, _, Nb = w3.shape
    tm = min(TM, S)
    N = G * Nb

    def body(x_ref, g_ref, w_ref, h_ref, o_ref):
        xv = x_ref[...]
        r = lax.rsqrt(jnp.mean(xv * xv, axis=-1, keepdims=True) + EPS)
        h = (xv * r * g_ref[...]).astype(h_ref.dtype)
        h_ref[...] = h
        for gi in range(G):
            acc = _dot(h, w_ref[gi])
            if mlp:
                acc = jnp.square(jnp.maximum(acc, 0.0))
            o_ref[:, gi * Nb:(gi + 1) * Nb] = acc.astype(o_ref.dtype)

    rows = lambda w: pl.BlockSpec((tm, w), lambda i: (i, 0))
    return pl.pallas_call(
        body, name=name, grid=(S // tm,),
        in_specs=[rows(D), _resident((1, D)), _resident((G, D, Nb))],
        out_specs=[rows(D), rows(N)],
        out_shape=[jax.ShapeDtypeStruct((S, D), MXU_DTYPE), jax.ShapeDtypeStruct((S, N), MXU_DTYPE if mlp else F32)],
        compiler_params=_params(1),
    )(x, g.reshape(1, D), w3)


def matmul_residual(a, w, res, *, name):
    S, K = a.shape
    _, N = w.shape
    tm = min(TM, S)

    def body(a_ref, w_ref, r_ref, o_ref):
        o_ref[...] = r_ref[...] + _dot(a_ref[...], w_ref[...])

    rows = lambda w_: pl.BlockSpec((tm, w_), lambda i: (i, 0))
    return pl.pallas_call(
        body, name=name, grid=(S // tm,),
        in_specs=[rows(K), _resident((K, N)), rows(N)],
        out_specs=rows(N), out_shape=jax.ShapeDtypeStruct((S, N), F32),
        compiler_params=_params(1),
    )(a, w, res)


def matmul_nt(a, w3, *, name, epi="plain", u=None, x=None, g=None, dx=None):
    S, N = a.shape
    G, Ko, Nb = w3.shape
    assert N == G * Nb
    tm = min(TM, S)
    tko = min(Ko, 512)

    def body(a_ref, w_ref, *rest):
        if epi == "mlp_du":
            u_ref, o_ref = rest
            av = a_ref[...].astype(MXU_DTYPE)
            for j in range(Ko // tko):
                cols = slice(j * tko, (j + 1) * tko)
                da = _dot(av, w_ref[0, cols, :], NT)
                o_ref[:, cols] = (da * (2.0 * jnp.sqrt(u_ref[:, cols].astype(F32)))).astype(o_ref.dtype)
            return
        acc = _dot(a_ref[:, :Nb], w_ref[0], NT)
        for gi in range(1, G):
            acc = acc + _dot(a_ref[:, gi * Nb:(gi + 1) * Nb], w_ref[gi], NT)
        if epi == "plain":
            rest[0][...] = acc
        else:
            x_ref, g_ref, dx_ref, o_ref, dg_ref = rest
            xhat, r = _rms(x_ref[...], Ko)
            dxb, dg = _rms_bwd(acc, xhat, r, g_ref[...], Ko)
            o_ref[...] = dx_ref[...] + dxb

            @pl.when(pl.program_id(0) == 0)
            def _():
                dg_ref[...] = dg

            @pl.when(pl.program_id(0) != 0)
            def _():
                dg_ref[...] += dg

    rows = lambda w_: pl.BlockSpec((tm, w_), lambda i: (i, 0))
    in_specs = [rows(N), _resident((G, Ko, Nb))]
    args = [a, w3]
    if epi == "plain":
        out_shape, out_specs = jax.ShapeDtypeStruct((S, Ko), F32), rows(Ko)
    elif epi == "mlp_du":
        in_specs.append(rows(Ko))
        args.append(u)
        out_shape, out_specs = jax.ShapeDtypeStruct((S, Ko), MXU_DTYPE), rows(Ko)
    else:
        in_specs += [rows(Ko), _resident((1, Ko)), rows(Ko)]
        args += [x, g.reshape(1, Ko), dx]
        out_shape = [jax.ShapeDtypeStruct((S, Ko), F32), jax.ShapeDtypeStruct((1, Ko), F32)]
        out_specs = [rows(Ko), _resident((1, Ko))]
    return pl.pallas_call(
        body, name=name, grid=(S // tm,),
        in_specs=in_specs, out_specs=out_specs, out_shape=out_shape,
        compiler_params=_params(1),
    )(*args)


def mm_tn(a, b, *, name, G, out_dtype):
    S, Ka = a.shape
    _, N = b.shape
    Nb = N // G
    tm = min(TM, S)
    tka = min(Ka, 1024)
    tnb = Nb if Nb <= 1024 else 1024
    nj = Nb // tnb
    ns = S // tm

    def body(a_ref, b_ref, o_ref, acc):
        s = pl.program_id(2)

        @pl.when(s == 0)
        def _():
            acc[...] = jnp.zeros_like(acc)

        acc[...] += _dot(a_ref[...], b_ref[...], TN)

        @pl.when(s == ns - 1)
        def _():
            o_ref[...] = acc[...].astype(o_ref.dtype)

    return pl.pallas_call(
        body, name=name, grid=(Ka // tka, G * nj, ns),
        in_specs=[pl.BlockSpec((tm, tka), lambda i, j, s: (s, i)),
                  pl.BlockSpec((tm, tnb), lambda i, j, s: (s, j))],
        out_specs=pl.BlockSpec((None, tka, tnb), lambda i, j, s: (j // nj, i, j % nj)),
        out_shape=jax.ShapeDtypeStruct((G, Ka, Nb), out_dtype),
        scratch_shapes=[pltpu.VMEM((tka, tnb), F32)],
        compiler_params=_params(3),
    )(a, b)


def mla_pre_fwd(a, gqa, gkva, wuq3, wukv3, gqn, gkn, cos_t, sin_t):
    S = a.shape[0]
    tm = min(TM, S)

    def body(a_ref, gqa_ref, gkva_ref, wuq_ref, wukv_ref, gqn_ref, gkn_ref, cos_ref, sin_ref,
             q_ref, k_ref, v_ref, cq_ref, ckv_ref, cqs, ckvs):
        @pl.when(pl.program_id(1) == 0)
        def _():
            av = a_ref[...]
            cq = (_rms(av[:, :Q_LORA], Q_LORA)[0] * gqa_ref[...]).astype(cqs.dtype)
            ckv = (_rms(av[:, Q_LORA:Q_LORA + KV_LORA], KV_LORA)[0] * gkva_ref[...]).astype(ckvs.dtype)
            cqs[...] = cq
            cq_ref[...] = cq
            ckvs[...] = ckv
            ckv_ref[...] = ckv

        cos_v, sin_v = cos_ref[...], sin_ref[...]
        qn = _rms(_dot(cqs[...], wuq_ref[...]), QK_DIM)[0] * gqn_ref[...]
        qr = jnp.concatenate([qn[:, :NOPE], _rope(qn[:, NOPE:], cos_v, sin_v)], axis=1)
        q_ref[...] = (qr * Q_PRESCALE).astype(q_ref.dtype)
        kvp = _dot(ckvs[...], wukv_ref[...])
        kk = jnp.concatenate([kvp[:, :NOPE], a_ref[:, Q_LORA + KV_LORA:]], axis=1)
        kn = _rms(kk, QK_DIM)[0] * gkn_ref[...]
        k_ref[...] = jnp.concatenate([kn[:, :NOPE], _rope(kn[:, NOPE:], cos_v, sin_v)], axis=1).astype(k_ref.dtype)
        v_ref[...] = kvp[:, NOPE:].astype(v_ref.dtype)

    row = lambda w: pl.BlockSpec((tm, w), lambda i, h: (i, 0))
    vec = lambda w: pl.BlockSpec((1, w), lambda i, h: (0, 0))
    head = lambda w: pl.BlockSpec((None, tm, w), lambda i, h: (h, i, 0))
    return pl.pallas_call(
        body, name="mla_pre_fwd", grid=(S // tm, N_HEADS),
        in_specs=[row(DOWN_PAD), vec(Q_LORA), vec(KV_LORA),
                  pl.BlockSpec((None, Q_LORA, QK_PAD), lambda i, h: (h, 0, 0)),
                  pl.BlockSpec((None, KV_LORA, NOPE + V_DIM), lambda i, h: (h, 0, 0)),
                  vec(QK_PAD), vec(QK_PAD), row(128), row(128)],
        out_specs=[head(QK_PAD), head(QK_PAD), head(V_DIM), row(Q_LORA), row(KV_LORA)],
        out_shape=[jax.ShapeDtypeStruct((N_HEADS, S, QK_PAD), MXU_DTYPE),
                   jax.ShapeDtypeStruct((N_HEADS, S, QK_PAD), MXU_DTYPE),
                   jax.ShapeDtypeStruct((N_HEADS, S, V_DIM), MXU_DTYPE),
                   jax.ShapeDtypeStruct((S, Q_LORA), MXU_DTYPE),
                   jax.ShapeDtypeStruct((S, KV_LORA), MXU_DTYPE)],
        scratch_shapes=[pltpu.VMEM((tm, Q_LORA), MXU_DTYPE), pltpu.VMEM((tm, KV_LORA), MXU_DTYPE)],
        compiler_params=_params(2),
    )(a, gqa, gkva, wuq3, wukv3, gqn, gkn, cos_t, sin_t)


def mla_pre_bwd(dq, dk, dv, a, cq, ckv, gqa, gkva, wuq3, wukv3, gqn, gkn, cos_t, sin_t):
    S = a.shape[0]
    tm = min(TM, S)
    H = N_HEADS

    def body(dq_ref, dk_ref, dv_ref, a_ref, cq_ref, ckv_ref, gqa_ref, gkva_ref, wuq_ref, wukv_ref, gqn_ref, gkn_ref,
             cos_ref, sin_ref, da_ref, dwuq_ref, dwukv_ref, dgqn_ref, dgkn_ref, dgqa_ref, dgkva_ref,
             dcq, dckv, dkpe):
        i, h = pl.program_id(0), pl.program_id(1)

        @pl.when((i == 0) & (h == 0))
        def _():
            for ref in (dwuq_ref, dwukv_ref, dgqn_ref, dgkn_ref, dgqa_ref, dgkva_ref):
                ref[...] = jnp.zeros_like(ref)

        @pl.when(h == 0)
        def _():
            dcq[...] = jnp.zeros_like(dcq)
            dckv[...] = jnp.zeros_like(dckv)
            dkpe[...] = jnp.zeros_like(dkpe)

        cos_v, sin_v = cos_ref[...], sin_ref[...]
        cqv, ckvv = cq_ref[...], ckv_ref[...]
        wuq, wukv = wuq_ref[...], wukv_ref[...]

        qhat, rq = _rms(_dot(cqv, wuq), QK_DIM)
        dqr = dq_ref[...] * SM_SCALE
        dqn = jnp.concatenate([dqr[:, :NOPE], _rope_bwd(dqr[:, NOPE:], cos_v, sin_v)], axis=1)
        dqp, dg = _rms_bwd(dqn, qhat, rq, gqn_ref[...], QK_DIM)
        dgqn_ref[...] += dg
        dqp = dqp.astype(MXU_DTYPE)
        dwuq_ref[h] += _dot(cqv, dqp, TN)
        dcq[...] += _dot(dqp, wuq, NT)

        kvp = _dot(ckvv, wukv)
        kk = jnp.concatenate([kvp[:, :NOPE], a_ref[:, Q_LORA + KV_LORA:]], axis=1)
        khat, rk = _rms(kk, QK_DIM)
        dkr = dk_ref[...] * (1.0 / LOG2E)
        dkn = jnp.concatenate([dkr[:, :NOPE], _rope_bwd(dkr[:, NOPE:], cos_v, sin_v)], axis=1)
        dkk, dg = _rms_bwd(dkn, khat, rk, gkn_ref[...], QK_DIM)
        dgkn_ref[...] += dg
        dkpe[...] += dkk[:, NOPE:]
        dkvp = jnp.concatenate([dkk[:, :NOPE], dv_ref[...]], axis=1).astype(MXU_DTYPE)
        dwukv_ref[h] += _dot(ckvv, dkvp, TN)
        dckv[...] += _dot(dkvp, wukv, NT)

        @pl.when(h == H - 1)
        def _():
            av = a_ref[...]
            ahat, r = _rms(av[:, :Q_LORA], Q_LORA)
            daq, dg = _rms_bwd(dcq[...], ahat, r, gqa_ref[...], Q_LORA)
            dgqa_ref[...] += dg
            ahat, r = _rms(av[:, Q_LORA:Q_LORA + KV_LORA], KV_LORA)
            dakv, dg = _rms_bwd(dckv[...], ahat, r, gkva_ref[...], KV_LORA)
            dgkva_ref[...] += dg
            da_ref[...] = jnp.concatenate([daq, dakv, dkpe[...]], axis=1)

    row = lambda w: pl.BlockSpec((tm, w), lambda i, h: (i, 0))
    vec = lambda w: pl.BlockSpec((1, w), lambda i, h: (0, 0))
    head = lambda w: pl.BlockSpec((None, tm, w), lambda i, h: (h, i, 0))
    full3 = lambda s: pl.BlockSpec(s, lambda i, h: (0, 0, 0))
    return pl.pallas_call(
        body, name="mla_pre_bwd", grid=(S // tm, H),
        in_specs=[head(QK_PAD), head(QK_PAD), head(V_DIM), row(DOWN_PAD), row(Q_LORA), row(KV_LORA),
                  vec(Q_LORA), vec(KV_LORA),
                  pl.BlockSpec((None, Q_LORA, QK_PAD), lambda i, h: (h, 0, 0)),
                  pl.BlockSpec((None, KV_LORA, NOPE + V_DIM), lambda i, h: (h, 0, 0)),
                  vec(QK_PAD), vec(QK_PAD), row(128), row(128)],
        out_specs=[row(DOWN_PAD), full3((H, Q_LORA, QK_PAD)), full3((H, KV_LORA, NOPE + V_DIM)),
                   vec(QK_PAD), vec(QK_PAD), vec(Q_LORA), vec(KV_LORA)],
        out_shape=[jax.ShapeDtypeStruct((S, DOWN_PAD), F32),
                   jax.ShapeDtypeStruct((H, Q_LORA, QK_PAD), F32),
                   jax.ShapeDtypeStruct((H, KV_LORA, NOPE + V_DIM), F32),
                   jax.ShapeDtypeStruct((1, QK_PAD), F32), jax.ShapeDtypeStruct((1, QK_PAD), F32),
                   jax.ShapeDtypeStruct((1, Q_LORA), F32), jax.ShapeDtypeStruct((1, KV_LORA), F32)],
        scratch_shapes=[pltpu.VMEM((tm, Q_LORA), F32), pltpu.VMEM((tm, KV_LORA), F32), pltpu.VMEM((tm, 128), F32)],
        compiler_params=_params(2),
    )(dq, dk, dv, a, cq, ckv, gqa, gkva, wuq3, wukv3, gqn, gkn, cos_t, sin_t)


def _pair_tables(nb, key_major):
    if key_major:
        pairs = [(qi, kj) for kj in range(nb) for qi in range(kj, nb)]
    else:
        pairs = [(qi, ki) for qi in range(nb) for ki in range(qi + 1)]
    return (jnp.asarray(np.array([p[0] for p in pairs], np.int32)),
            jnp.asarray(np.array([p[1] for p in pairs], np.int32)))


def _scores_t(k, q, pk_col, pq_row, masked):
    s = _dot(k, q, NT)
    return jnp.where(pq_row >= pk_col, s, jnp.finfo(F32).min) if masked else s


def attn_fwd(q, k, v, pos_col, pos_row):
    H, S, _ = q.shape
    t = min(TQ, S)
    nb = S // t
    hb = HEADS_FWD
    qt, kt = _pair_tables(nb, key_major=False)

    def body(qt_ref, kt_ref, q_ref, k_ref, v_ref, pk_ref, pq_ref, o_ref, lse_ref, m_s, l_s, acc):
        step = pl.program_id(1)
        qi, ki = qt_ref[step], kt_ref[step]

        @pl.when(ki == 0)
        def _():
            m_s[...] = jnp.full_like(m_s, -jnp.inf)
            l_s[...] = jnp.zeros_like(l_s)
            acc[...] = jnp.zeros_like(acc)

        def update(masked):
            scores = lambda hh: _scores_t(k_ref[hh], q_ref[hh], pk_ref[...], pq_ref[...], masked)
            s_next = scores(0)
            for hh in range(hb):
                s = s_next
                if hh + 1 < hb:
                    s_next = scores(hh + 1)
                m_old = m_s[hh]
                m_new = jnp.maximum(m_old, jnp.max(s, axis=0, keepdims=True))
                p = jnp.exp2(s - m_new)
                alpha = jnp.exp2(m_old - m_new)
                l_s[hh] = alpha * l_s[hh] + jnp.sum(p, axis=0, keepdims=True)
                acc[hh] = alpha * acc[hh] + _dot(v_ref[hh], p, TN)
                m_s[hh] = m_new

        @pl.when(ki < qi)
        def _():
            update(False)

        @pl.when(ki == qi)
        def _():
            update(True)
            for hh in range(hb):
                o_ref[:, hh * V_DIM:(hh + 1) * V_DIM] = (acc[hh] / l_s[hh]).T
                lse_ref[hh] = m_s[hh] + jnp.log(l_s[hh]) * LOG2E

    grid_spec = pltpu.PrefetchScalarGridSpec(
        num_scalar_prefetch=2, grid=(H // hb, qt.shape[0]),
        in_specs=[pl.BlockSpec((hb, t, QK_PAD), lambda h, s, qt, kt: (h, qt[s], 0)),
                  pl.BlockSpec((hb, t, QK_PAD), lambda h, s, qt, kt: (h, kt[s], 0)),
                  pl.BlockSpec((hb, t, V_DIM), lambda h, s, qt, kt: (h, kt[s], 0)),
                  pl.BlockSpec((t, 1), lambda h, s, qt, kt: (kt[s], 0)),
                  pl.BlockSpec((1, t), lambda h, s, qt, kt: (0, qt[s]))],
        out_specs=[pl.BlockSpec((t, hb * V_DIM), lambda h, s, qt, kt: (qt[s], h)),
                   pl.BlockSpec((hb, 1, t), lambda h, s, qt, kt: (h, 0, qt[s]))],
        scratch_shapes=[pltpu.VMEM((hb, 1, t), F32), pltpu.VMEM((hb, 1, t), F32), pltpu.VMEM((hb, V_DIM, t), F32)])
    return pl.pallas_call(
        body, name="attn_fwd", grid_spec=grid_spec,
        out_shape=[jax.ShapeDtypeStruct((S, H * V_DIM), F32), jax.ShapeDtypeStruct((H, 1, S), F32)],
        compiler_params=_params(2),
    )(qt, kt, q, k, v, pos_col, pos_row)


def attn_delta(o, do):
    S = o.shape[0]
    t = min(TQ, S)

    def body(o_ref, do_ref, d_ref):
        d_ref[...] = jnp.sum((o_ref[...] * do_ref[...]).T, axis=0, keepdims=True)

    blk = pl.BlockSpec((t, V_DIM), lambda h, i: (i, h))
    return pl.pallas_call(
        body, name="attn_delta", grid=(N_HEADS, S // t),
        in_specs=[blk, blk],
        out_specs=pl.BlockSpec((None, 1, t), lambda h, i: (h, 0, i)),
        out_shape=jax.ShapeDtypeStruct((N_HEADS, 1, S), F32),
        compiler_params=_params(2),
    )(o, do)


def attn_bwd(q, k, v, do, lse, delta, pos_col, pos_row):
    H, S, _ = q.shape
    t = min(TQ_BWD, S)
    nb = S // t
    qt, kt = _pair_tables(nb, key_major=True)
    tc = min(BWD_CHUNK, t)

    def body(qt_ref, kt_ref, q_ref, k_ref, v_ref, do_ref, lse_ref, dl_ref, pk_ref, pq_ref, dq_ref, dk_ref, dv_ref):
        step = pl.program_id(1)
        qi, kj = qt_ref[step], kt_ref[step]

        @pl.when(step == 0)
        def _():
            dq_ref[...] = jnp.zeros_like(dq_ref)

        @pl.when(qi == kj)
        def _():
            dk_ref[...] = jnp.zeros_like(dk_ref)
            dv_ref[...] = jnp.zeros_like(dv_ref)

        def update(masked):
            kv_, vv = k_ref[...], v_ref[...]

            def first_matmuls(c):
                cols = slice(c * tc, (c + 1) * tc)
                qc = q_ref[cols, :]
                doc = do_ref[cols, :].astype(MXU_DTYPE)
                return qc, doc, _scores_t(kv_, qc, pk_ref[...], pq_ref[:, cols], masked), _dot(vv, doc, NT)

            nxt = first_matmuls(0)
            for c in range(t // tc):
                qc, doc, s, dp = nxt
                if c + 1 < t // tc:
                    nxt = first_matmuls(c + 1)
                cols = slice(c * tc, (c + 1) * tc)
                p = jnp.exp2(s - lse_ref[:, cols])
                ds = (p * (dp - dl_ref[:, cols])).astype(MXU_DTYPE)
                dv_ref[...] += _dot(p, doc)
                dk_ref[...] += _dot(ds, qc)
                rows = pl.ds(pl.multiple_of(qi * t + c * tc, tc), tc)
                dq_ref[rows, :] += _dot(ds, kv_, TN)

        @pl.when(qi == kj)
        def _():
            update(True)

        @pl.when(qi != kj)
        def _():
            update(False)

    q_idx = lambda h, s, qt, kt: (h, qt[s], 0)
    k_idx = lambda h, s, qt, kt: (h, kt[s], 0)
    row_idx = lambda h, s, qt, kt: (h, 0, qt[s])
    grid_spec = pltpu.PrefetchScalarGridSpec(
        num_scalar_prefetch=2, grid=(H, qt.shape[0]),
        in_specs=[pl.BlockSpec((None, t, QK_PAD), q_idx),
                  pl.BlockSpec((None, t, QK_PAD), k_idx),
                  pl.BlockSpec((None, t, V_DIM), k_idx),
                  pl.BlockSpec((t, V_DIM), lambda h, s, qt, kt: (qt[s], h)),
                  pl.BlockSpec((None, 1, t), row_idx),
                  pl.BlockSpec((None, 1, t), row_idx),
                  pl.BlockSpec((t, 1), lambda h, s, qt, kt: (kt[s], 0)),
                  pl.BlockSpec((1, t), lambda h, s, qt, kt: (0, qt[s]))],
        out_specs=[pl.BlockSpec((None, S, QK_PAD), lambda h, s, qt, kt: (h, 0, 0)),
                   pl.BlockSpec((None, t, QK_PAD), k_idx),
                   pl.BlockSpec((None, t, V_DIM), k_idx)])
    return pl.pallas_call(
        body, name="attn_bwd", grid_spec=grid_spec,
        out_shape=[jax.ShapeDtypeStruct((H, S, QK_PAD), F32), jax.ShapeDtypeStruct((H, S, QK_PAD), F32),
                   jax.ShapeDtypeStruct((H, S, V_DIM), F32)],
        compiler_params=_params(2),
    )(qt, kt, q, k, v, do, lse, delta, pos_col, pos_row)


def _conv_specs(S, tr, tc):
    nc = D_MODEL // tc
    hb = tr // 8
    main = lambda third: pl.BlockSpec((tr, tc), lambda c, r: (r, third * nc + c))
    prev = lambda third: pl.BlockSpec((8, tc), lambda c, r: (jnp.maximum(r * hb - 1, 0), third * nc + c))
    nxt = lambda third: pl.BlockSpec((8, tc), lambda c, r: (jnp.minimum((r + 1) * hb, S // 8 - 1), third * nc + c))
    return main, prev, nxt


def _conv_taps(gc, uu, w_ref, first):
    u2 = gc * uu
    rows = lax.broadcasted_iota(jnp.int32, u2.shape, 0)
    u2 = jnp.where((rows < 8) & first, 0.0, u2)
    s1 = pltpu.roll(u2, 1, 0)
    s2 = pltpu.roll(u2, 2, 0)
    u3 = w_ref[2:3, :] * u2 + w_ref[1:2, :] * s1 + w_ref[0:1, :] * s2
    return u2, s1, s2, u3


def conv_fwd(bcu, cw):
    S = bcu.shape[0]
    tr, tc = min(TROW, S), TCH
    main, prev, _ = _conv_specs(S, tr, tc)

    def body(gb_ref, gc_ref, u_ref, gch_ref, uh_ref, w_ref, z_ref):
        gc = jnp.concatenate([gch_ref[...], gc_ref[...]], axis=0)
        uu = jnp.concatenate([uh_ref[...], u_ref[...]], axis=0)
        u3 = _conv_taps(gc, uu, w_ref, pl.program_id(1) == 0)[3]
        z_ref[...] = (gb_ref[...] * u3[8:]).astype(z_ref.dtype)

    return pl.pallas_call(
        body, name="conv_fwd", grid=(D_MODEL // tc, S // tr),
        in_specs=[main(0), main(1), main(2), prev(1), prev(2), pl.BlockSpec((3, tc), lambda c, r: (0, c))],
        out_specs=pl.BlockSpec((tr, tc), lambda c, r: (r, c)),
        out_shape=jax.ShapeDtypeStruct((S, D_MODEL), MXU_DTYPE),
        compiler_params=_params(2),
    )(bcu, bcu, bcu, bcu, bcu, cw)


def conv_bwd(dz, bcu, cw):
    S = bcu.shape[0]
    tr, tc = min(TROW, S), TCH
    nr = S // tr
    main, prev, nxt = _conv_specs(S, tr, tc)

    def body(dz_ref, dzn_ref, gb_ref, gbn_ref, gc_ref, u_ref, gch_ref, uh_ref, w_ref,
             dgb_ref, dgc_ref, du_ref, dw_ref):
        r = pl.program_id(1)
        gcv, uv = gc_ref[...], u_ref[...]
        gc = jnp.concatenate([gch_ref[...], gcv], axis=0)
        uu = jnp.concatenate([uh_ref[...], uv], axis=0)
        u2, s1, s2, u3 = _conv_taps(gc, uu, w_ref, r == 0)
        dzv = dz_ref[...]
        du3 = jnp.concatenate([dzv * gb_ref[...], dzn_ref[...] * gbn_ref[...]], axis=0)
        rows = lax.broadcasted_iota(jnp.int32, du3.shape, 0)
        du3 = jnp.where((rows >= tr) & (r == nr - 1), 0.0, du3)
        n1 = pltpu.roll(du3, tr + 8 - 1, 0)
        n2 = pltpu.roll(du3, tr + 8 - 2, 0)
        du2 = (w_ref[2:3, :] * du3 + w_ref[1:2, :] * n1 + w_ref[0:1, :] * n2)[:tr]
        dgb_ref[...] = (dzv * u3[8:]).astype(dgb_ref.dtype)
        dgc_ref[...] = (du2 * uv).astype(dgc_ref.dtype)
        du_ref[...] = (du2 * gcv).astype(du_ref.dtype)
        d3 = du3[:tr]
        taps = [jnp.sum(d3 * t[8:], axis=0, keepdims=True) for t in (s2, s1, u2)]

        @pl.when(r == 0)
        def _():
            for kk in range(3):
                dw_ref[kk:kk + 1, :] = taps[kk]

        @pl.when(r != 0)
        def _():
            for kk in range(3):
                dw_ref[kk:kk + 1, :] += taps[kk]

    out = pl.BlockSpec((tr, tc), lambda c, r: (r, c))
    nxt_dz = pl.BlockSpec((8, tc), lambda c, r: (jnp.minimum((r + 1) * (tr // 8), S // 8 - 1), c))
    act = jax.ShapeDtypeStruct((S, D_MODEL), MXU_DTYPE)
    return pl.pallas_call(
        body, name="conv_bwd", grid=(D_MODEL // tc, nr),
        in_specs=[out, nxt_dz, main(0), nxt(0), main(1), main(2), prev(1), prev(2),
                  pl.BlockSpec((3, tc), lambda c, r: (0, c))],
        out_specs=[out, out, out, pl.BlockSpec((3, tc), lambda c, r: (0, c))],
        out_shape=[act, act, act, jax.ShapeDtypeStruct((3, D_MODEL), F32)],
        compiler_params=_params(2),
    )(dz, dz, bcu, bcu, bcu, bcu, bcu, bcu, cw)


def loss_head(y, target):
    S, D = y.shape
    tm = min(TM, S)

    def body(y_ref, t_ref, l_ref, dy_ref):
        err = y_ref[...] - t_ref[...]
        dy_ref[...] = err / D
        part = jnp.full((1, 128), jnp.sum(err * err), F32)

        @pl.when(pl.program_id(0) == 0)
        def _():
            l_ref[...] = part

        @pl.when(pl.program_id(0) != 0)
        def _():
            l_ref[...] += part

    blk = pl.BlockSpec((tm, D), lambda i: (i, 0))
    return pl.pallas_call(
        body, name="loss_head", grid=(S // tm,),
        in_specs=[blk, blk],
        out_specs=[pl.BlockSpec((1, 128), lambda i: (0, 0)), blk],
        out_shape=[jax.ShapeDtypeStruct((1, 128), F32), jax.ShapeDtypeStruct((S, D), F32)],
        compiler_params=_params(1),
    )(y, target)


def adamw(parts, w, m, v, *, name):
    R, C = w.shape
    tr = R
    while tr * C * 4 > (1 << 20) and tr % 32 == 0:
        tr //= 2

    def body(p_ref, w_ref, m_ref, v_ref, g_ref, d_ref, mo_ref, vo_ref):
        g = p_ref[0].astype(F32)
        for d in range(1, N_DEV):
            g = g + p_ref[d].astype(F32)
        m_new = ADAM_B1 * m_ref[...] + (1.0 - ADAM_B1) * g
        v_new = ADAM_B2 * v_ref[...] + (1.0 - ADAM_B2) * (g * g)
        m_hat = m_new / (1.0 - ADAM_B1 ** ADAM_STEP)
        v_hat = v_new / (1.0 - ADAM_B2 ** ADAM_STEP)
        g_ref[...] = g
        d_ref[...] = -ADAM_LR * (m_hat / (jnp.sqrt(v_hat) + ADAM_EPS) + ADAM_WD * w_ref[...])
        mo_ref[...] = m_new
        vo_ref[...] = v_new

    blk = pl.BlockSpec((tr, C), lambda i: (i, 0))
    return pl.pallas_call(
        body, name=name, grid=(R // tr,),
        in_specs=[pl.BlockSpec((N_DEV, tr, C), lambda i: (0, i, 0)), blk, blk, blk],
        out_specs=[blk, blk, blk, blk],
        out_shape=[jax.ShapeDtypeStruct((R, C), F32)] * 4,
        compiler_params=_params(1),
    )(parts, w, m, v)


def _mesh_place():
    x, y, c = (lax.axis_index(n) for n in MESH_AXES)
    return x, y, c, 4 * x + 2 * y + c


def _peer(x, y, c, d):
    px = 1 - x if d & 4 else x
    py = 1 - y if d & 2 else y
    pc = 1 - c if d & 1 else c
    return (px, py, pc), 4 * px + 2 * py + pc


def _exchange(srcs, name, scatter):
    n = len(srcs)
    any_spec = pl.BlockSpec(memory_space=pl.ANY)

    def body(*refs):
        ins, outs = refs[:n], refs[n:2 * n]
        send_sems, recv_sems, local_sems = refs[2 * n:]
        x, y, c, me = _mesh_place()
        for a in range(n):
            mine = ins[a].at[me] if scatter else ins[a]
            pltpu.make_async_copy(mine, outs[a].at[me], local_sems.at[a]).start()
            for d in range(1, N_DEV):
                peer, peer_lin = _peer(x, y, c, d)
                pltpu.make_async_remote_copy(
                    src_ref=ins[a].at[peer_lin] if scatter else ins[a], dst_ref=outs[a].at[me],
                    send_sem=send_sems.at[a], recv_sem=recv_sems.at[a],
                    device_id=peer, device_id_type=pl.DeviceIdType.MESH).start()
        for a in range(n):
            mine = ins[a].at[me] if scatter else ins[a]
            pltpu.make_async_copy(mine, outs[a].at[me], local_sems.at[a]).wait()
            seven = outs[a].at[pl.ds(0, N_DEV - 1)]
            drain = pltpu.make_async_remote_copy(
                src_ref=seven, dst_ref=seven, send_sem=send_sems.at[a], recv_sem=recv_sems.at[a],
                device_id=(x, y, c), device_id_type=pl.DeviceIdType.MESH)
            drain.wait_send()
            drain.wait_recv()

    block = (lambda s: s.shape[1:]) if scatter else (lambda s: s.shape)
    return pl.pallas_call(
        body, name=name,
        in_specs=[any_spec] * n, out_specs=[any_spec] * n,
        out_shape=[jax.ShapeDtypeStruct((N_DEV,) + tuple(block(s)), s.dtype) for s in srcs],
        scratch_shapes=[pltpu.SemaphoreType.DMA((n,)), pltpu.SemaphoreType.DMA((n,)), pltpu.SemaphoreType.DMA((n,))],
    )(*srcs)


def _rope_tables(pos):
    inv_freq = ROPE_THETA ** (-jnp.arange(0, ROPE, 2, dtype=F32) / ROPE)
    ang = pos.astype(F32)[:, None] * inv_freq
    cos, sin = jnp.cos(ang), jnp.sin(ang)
    pad = jnp.zeros((pos.shape[0], 128 - ROPE), F32)
    return jnp.concatenate([cos, cos, pad + 1.0], axis=1), jnp.concatenate([-sin, sin, pad], axis=1)


def _pad_last(w, n):
    return jnp.pad(w, [(0, 0)] * (w.ndim - 1) + [(0, n - w.shape[-1])])


def kernel(x, positions, g_mix, g_mlp, attn_w_down, attn_g_q_a, attn_g_kv_a, attn_w_uq, attn_w_ukv, attn_g_qnorm, attn_g_knorm, attn_w_o, conv_w_in, conv_w, conv_w_out, mlp_w1, mlp_w2, loss_target, m_g_mix, m_g_mlp, m_attn_w_down, m_attn_g_q_a, m_attn_g_kv_a, m_attn_w_uq, m_attn_w_ukv, m_attn_g_qnorm, m_attn_g_knorm, m_attn_w_o, m_conv_w_in, m_conv_w, m_conv_w_out, m_mlp_w1, m_mlp_w2, v_g_mix, v_g_mlp, v_attn_w_down, v_attn_g_q_a, v_attn_g_kv_a, v_attn_w_uq, v_attn_w_ukv, v_attn_g_qnorm, v_attn_g_knorm, v_attn_w_o, v_conv_w_in, v_conv_w, v_conv_w_out, v_mlp_w1, v_mlp_w2):
    weights = dict(g_mix=g_mix, g_mlp=g_mlp, attn_w_down=attn_w_down, attn_g_q_a=attn_g_q_a, attn_g_kv_a=attn_g_kv_a,
                   attn_w_uq=attn_w_uq, attn_w_ukv=attn_w_ukv, attn_g_qnorm=attn_g_qnorm, attn_g_knorm=attn_g_knorm,
                   attn_w_o=attn_w_o, conv_w_in=conv_w_in, conv_w=conv_w, conv_w_out=conv_w_out, mlp_w1=mlp_w1, mlp_w2=mlp_w2)
    mom1 = dict(g_mix=m_g_mix, g_mlp=m_g_mlp, attn_w_down=m_attn_w_down, attn_g_q_a=m_attn_g_q_a, attn_g_kv_a=m_attn_g_kv_a,
                attn_w_uq=m_attn_w_uq, attn_w_ukv=m_attn_w_ukv, attn_g_qnorm=m_attn_g_qnorm, attn_g_knorm=m_attn_g_knorm,
                attn_w_o=m_attn_w_o, conv_w_in=m_conv_w_in, conv_w=m_conv_w, conv_w_out=m_conv_w_out, mlp_w1=m_mlp_w1, mlp_w2=m_mlp_w2)
    mom2 = dict(g_mix=v_g_mix, g_mlp=v_g_mlp, attn_w_down=v_attn_w_down, attn_g_q_a=v_attn_g_q_a, attn_g_kv_a=v_attn_g_kv_a,
                attn_w_uq=v_attn_w_uq, attn_w_ukv=v_attn_w_ukv, attn_g_qnorm=v_attn_g_qnorm, attn_g_knorm=v_attn_g_knorm,
                attn_w_o=v_attn_w_o, conv_w_in=v_conv_w_in, conv_w=v_conv_w, conv_w_out=v_conv_w_out, mlp_w1=v_mlp_w1, mlp_w2=v_mlp_w2)
    big = ["attn_w_down", "attn_w_uq", "attn_w_ukv", "attn_w_o", "conv_w_in", "conv_w", "conv_w_out", "mlp_w1", "mlp_w2"]
    small = ["g_mix", "g_mlp", "attn_g_q_a", "attn_g_kv_a", "attn_g_qnorm", "attn_g_knorm"]
    order = ["g_mix", "g_mlp", "attn_w_down", "attn_g_q_a", "attn_g_kv_a", "attn_w_uq", "attn_w_ukv", "attn_g_qnorm",
             "attn_g_knorm", "attn_w_o", "conv_w_in", "conv_w", "conv_w_out", "mlp_w1", "mlp_w2"]

    xs = x[0]
    pos = positions[0]
    target = loss_target[0]
    S = xs.shape[0]
    depth = g_mix.shape[0]
    cos_t, sin_t = _rope_tables(pos)
    pos_col, pos_row = pos.reshape(S, 1), pos.reshape(1, S)

    keys, shards = [], []
    for name in big:
        for l in range(weights[name].shape[0]):
            keys.append((name, l))
            shards.append(weights[name][l] if name == "conv_w" else weights[name][l].astype(WIRE_DTYPE))
    full = dict(zip(keys, _exchange(shards, "gather_weights", scatter=False)))

    def rows(name, l):
        g = full[(name, l)]
        return g.reshape(g.shape[0] * g.shape[1], g.shape[2])

    saved = []
    for i in range(depth):
        l = i // 2
        rec = {"x0": xs}
        if i % 2 == 0:
            wd3 = _pad_last(rows("attn_w_down", l), DOWN_PAD)[None]
            wuq3 = _pad_last(full[("attn_w_uq", l)], QK_PAD)
            wukv3 = full[("attn_w_ukv", l)]
            gqn = _pad_last(attn_g_qnorm[l][None], QK_PAD)
            gkn = _pad_last(attn_g_knorm[l][None], QK_PAD)
            gqa, gkva = attn_g_q_a[l][None], attn_g_kv_a[l][None]
            h, a = norm_matmul(xs, g_mix[i], wd3, name="attn_down")
            q, k, v, cq, ckv = mla_pre_fwd(a, gqa, gkva, wuq3, wukv3, gqn, gkn, cos_t, sin_t)
            o, lse = attn_fwd(q, k, v, pos_col, pos_row)
            x1 = matmul_residual(o, rows("attn_w_o", l), xs, name="attn_out")
            rec.update(h=h, a=a, q=q, k=k, v=v, cq=cq, ckv=ckv, o=o, lse=lse, wd3=wd3, wuq3=wuq3, wukv3=wukv3,
                       gqn=gqn, gkn=gkn, gqa=gqa, gkva=gkva)
        else:
            cw = full[("conv_w", l)].transpose(1, 0, 2).reshape(3, D_MODEL)
            h, bcu = norm_matmul(xs, g_mix[i], full[("conv_w_in", l)], name="conv_in")
            z = conv_fwd(bcu, cw)
            x1 = matmul_residual(z, rows("conv_w_out", l), xs, name="conv_out")
            rec.update(h=h, bcu=bcu, z=z, cw=cw)
        h2, act = norm_matmul(x1, g_mlp[i], full[("mlp_w1", i)], name="mlp_up", mlp=True)
        xs = matmul_residual(act, rows("mlp_w2", i), x1, name="mlp_down")
        rec.update(x1=x1, h2=h2, act=act)
        saved.append(rec)

    sq, dx = loss_head(xs, target)
    loss = lax.psum(sq[0, 0] * (0.5 / D_MODEL), MESH_AXES)

    grads = {name: [None] * weights[name].shape[0] for name in order}
    for i in reversed(range(depth)):
        l = i // 2
        rec = saved[i]
        grads["mlp_w2"][i] = mm_tn(rec["act"], dx, name="mlp_down_dw", G=1, out_dtype=WIRE_DTYPE).reshape(N_DEV, -1, D_MODEL)
        du = matmul_nt(dx, rows("mlp_w2", i)[None], name="mlp_down_dx", epi="mlp_du", u=rec["act"])
        grads["mlp_w1"][i] = mm_tn(rec["h2"], du, name="mlp_up_dw", G=N_DEV, out_dtype=WIRE_DTYPE)
        dx1, dg = matmul_nt(du, full[("mlp_w1", i)], name="mlp_up_dx", epi="rms_bwd", x=rec["x1"], g=g_mlp[i], dx=dx)
        grads["g_mlp"][i] = dg[0]
        if i % 2 == 0:
            grads["attn_w_o"][l] = mm_tn(rec["o"], dx1, name="attn_out_dw", G=1, out_dtype=WIRE_DTYPE).reshape(N_DEV, -1, D_MODEL)
            do = matmul_nt(dx1, rows("attn_w_o", l)[None], name="attn_out_dx")
            dq, dk, dv = attn_bwd(rec["q"], rec["k"], rec["v"], do, rec["lse"], attn_delta(rec["o"], do), pos_col, pos_row)
            da, dwuq, dwukv, dgqn, dgkn, dgqa, dgkva = mla_pre_bwd(
                dq, dk, dv, rec["a"], rec["cq"], rec["ckv"], rec["gqa"], rec["gkva"], rec["wuq3"], rec["wukv3"],
                rec["gqn"], rec["gkn"], cos_t, sin_t)
            grads["attn_w_uq"][l] = dwuq[:, :, :QK_DIM].astype(WIRE_DTYPE)
            grads["attn_w_ukv"][l] = dwukv.astype(WIRE_DTYPE)
            grads["attn_g_qnorm"][l] = dgqn[0, :QK_DIM]
            grads["attn_g_knorm"][l] = dgkn[0, :QK_DIM]
            grads["attn_g_q_a"][l] = dgqa[0]
            grads["attn_g_kv_a"][l] = dgkva[0]
            dwd = mm_tn(rec["h"], da, name="attn_down_dw", G=1, out_dtype=WIRE_DTYPE)
            grads["attn_w_down"][l] = dwd[0, :, :DOWN].reshape(N_DEV, -1, DOWN)
            dx, dg = matmul_nt(da, rec["wd3"], name="attn_down_dx", epi="rms_bwd", x=rec["x0"], g=g_mix[i], dx=dx1)
        else:
            grads["conv_w_out"][l] = mm_tn(rec["z"], dx1, name="conv_out_dw", G=1, out_dtype=WIRE_DTYPE).reshape(N_DEV, -1, D_MODEL)
            dz = matmul_nt(dx1, rows("conv_w_out", l)[None], name="conv_out_dx")
            dgb, dgc, du_, dcw = conv_bwd(dz, rec["bcu"], rec["cw"])
            grads["conv_w"][l] = dcw.reshape(3, N_DEV, -1).transpose(1, 0, 2)
            dbcu = jnp.concatenate([dgb, dgc, du_], axis=1)
            grads["conv_w_in"][l] = mm_tn(rec["h"], dbcu, name="conv_in_dw", G=N_DEV, out_dtype=WIRE_DTYPE)
            dx, dg = matmul_nt(dbcu, full[("conv_w_in", l)], name="conv_in_dx", epi="rms_bwd", x=rec["x0"], g=g_mix[i], dx=dx1)
        grads["g_mix"][i] = dg[0]

    sizes = [weights[name].size for name in small]
    n_small = sum(sizes)
    rows_small = -(-n_small // (8 * 128)) * 8

    def pack(tree):
        flat = jnp.concatenate([jnp.stack(tree[name]).reshape(-1) if isinstance(tree[name], list) else tree[name].reshape(-1)
                                for name in small])
        return jnp.pad(flat, (0, rows_small * 128 - n_small)).reshape(rows_small, 128)

    sends = [jnp.stack(grads[name], axis=1) for name in big]
    sends.append(jnp.broadcast_to(pack(grads)[None], (N_DEV, rows_small, 128)))
    parts = _exchange(sends, "scatter_grads", scatter=True)

    out = {}
    for name, part in zip(big, parts[:-1]):
        w = weights[name]
        flat = lambda t: t.reshape(-1, t.shape[-1])
        res = adamw(part.reshape(N_DEV, -1, w.shape[-1]), flat(w), flat(mom1[name]), flat(mom2[name]), name="adamw_" + name)
        out[name] = [r.reshape(w.shape) for r in res]
    res = adamw(parts[-1], pack(weights), pack(mom1), pack(mom2), name="adamw_gains")
    offset = 0
    for name, size in zip(small, sizes):
        out[name] = [r.reshape(-1)[offset:offset + size].reshape(weights[name].shape) for r in res]
        offset += size

    return (loss, dx[None], *[out[n][0] for n in order], *[out[n][1] for n in order],
            *[out[n][2] for n in order], *[out[n][3] for n in order])
```

```python
import jax
import jax.numpy as jnp
import numpy as np
from jax import lax
from jax.experimental import pallas as pl
from jax.experimental.pallas import tpu as pltpu

F32 = jnp.float32
MXU_DTYPE = jnp.bfloat16
WIRE_DTYPE = jnp.bfloat16

D_MODEL = 1024
N_HEADS = 8
NOPE = 128
ROPE = 64
QK_DIM = NOPE + ROPE
QK_PAD = 256
V_DIM = 128
Q_LORA = 256
KV_LORA = 128
DOWN = Q_LORA + KV_LORA + ROPE
DOWN_PAD = 512
ROPE_THETA = 10000.0
EPS = 1e-6
SM_SCALE = QK_DIM ** -0.5
LOG2E = 1.4426950408889634
Q_PRESCALE = SM_SCALE * LOG2E
ADAM_LR, ADAM_B1, ADAM_B2, ADAM_EPS, ADAM_WD, ADAM_STEP = 0.001, 0.9, 0.999, 1e-08, 0.01, 10
N_DEV = 8
MESH_AXES = ("x", "y", "c")

TM = 512
TM_WIDE = 1024
TQ = 512
HEADS_FWD = 8
TQ_BWD = 1024
BWD_CHUNK = 256
TROW = 512
TCH = 512
VMEM_LIMIT = 48 << 20

NN = (((1,), (0,)), ((), ()))
NT = (((1,), (1,)), ((), ()))
TN = (((0,), (0,)), ((), ()))


def _dot(a, b, dims=NN):
    return lax.dot_general(a.astype(MXU_DTYPE), b.astype(MXU_DTYPE), dims, preferred_element_type=F32)


def _params(n_axes):
    return pltpu.CompilerParams(dimension_semantics=("arbitrary",) * n_axes, vmem_limit_bytes=VMEM_LIMIT)


def _rms(xv, n):
    r = lax.rsqrt(jnp.sum(xv * xv, axis=-1, keepdims=True) / n + EPS)
    return xv * r, r


def _rms_bwd(dy, xhat, r, g, n):
    dg = jnp.sum(dy * xhat, axis=0, keepdims=True)
    dxh = dy * g
    dx = r * (dxh - xhat * (jnp.sum(dxh * xhat, axis=-1, keepdims=True) / n))
    return dx, dg


def _swap_halves(t):
    lane = lax.broadcasted_iota(jnp.int32, t.shape, 1)
    return jnp.where(lane < ROPE // 2, pltpu.roll(t, 128 - ROPE // 2, 1), pltpu.roll(t, ROPE // 2, 1))


def _rope(t, cos_t, sin_t):
    return t * cos_t + _swap_halves(t) * sin_t


def _rope_bwd(dout, cos_t, sin_t):
    return dout * cos_t + _swap_halves(dout * sin_t)


def rms_mm(x, g, w3, *, name, mlp=False):
    S, D = x.shape
    G, _, Nb = w3.shape
    tm = min(TM_WIDE, S)
    tn = Nb if Nb <= 512 else 512
    nj = Nb // tn
    N = G * Nb

    def body(x_ref, g_ref, w_ref, h_ref, *rest):
        hs = rest[-1]

        @pl.when(pl.program_id(1) == 0)
        def _():
            xv = x_ref[...]
            r = lax.rsqrt(jnp.mean(xv * xv, axis=-1, keepdims=True) + EPS)
            h = (xv * r * g_ref[...]).astype(hs.dtype)
            hs[...] = h
            h_ref[...] = h

        acc = lax.dot_general(hs[...], w_ref[...].astype(hs.dtype), NN, preferred_element_type=F32)
        if mlp:
            rl = jnp.maximum(acc, 0.0)
            rest[0][...] = (rl * rl).astype(rest[0].dtype)
        else:
            rest[0][...] = acc

    out_shape = [jax.ShapeDtypeStruct((S, D), MXU_DTYPE), jax.ShapeDtypeStruct((S, N), MXU_DTYPE if mlp else F32)]
    out_specs = [pl.BlockSpec((tm, D), lambda i, j: (i, 0)), pl.BlockSpec((tm, tn), lambda i, j: (i, j))]
    return pl.pallas_call(
        body, name=name, grid=(S // tm, G * nj),
        in_specs=[pl.BlockSpec((tm, D), lambda i, j: (i, 0)),
                  pl.BlockSpec((1, D), lambda i, j: (0, 0)),
                  pl.BlockSpec((None, D, tn), lambda i, j: (j // nj, 0, j % nj))],
        out_specs=out_specs, out_shape=out_shape,
        scratch_shapes=[pltpu.VMEM((tm, D), MXU_DTYPE)],
        compiler_params=_params(2),
    )(x, g.reshape(1, D), w3)


def mm_res(a, w, res, *, name):
    S, K = a.shape
    _, N = w.shape
    tm = min(TM_WIDE, S)
    tk = min(K, 1024)
    nk = K // tk

    def body(a_ref, w_ref, r_ref, o_ref, acc):
        k = pl.program_id(1)

        @pl.when(k == 0)
        def _():
            acc[...] = jnp.zeros_like(acc)

        acc[...] += _dot(a_ref[...], w_ref[...])

        @pl.when(k == nk - 1)
        def _():
            o_ref[...] = r_ref[...] + acc[...]

    return pl.pallas_call(
        body, name=name, grid=(S // tm, nk),
        in_specs=[pl.BlockSpec((tm, tk), lambda i, k: (i, k)),
                  pl.BlockSpec((tk, N), lambda i, k: (k, 0)),
                  pl.BlockSpec((tm, N), lambda i, k: (i, 0))],
        out_specs=pl.BlockSpec((tm, N), lambda i, k: (i, 0)),
        out_shape=jax.ShapeDtypeStruct((S, N), F32),
        scratch_shapes=[pltpu.VMEM((tm, N), F32)],
        compiler_params=_params(2),
    )(a, w, res)


def mm_nt(a, w3, *, name, epi="plain", u=None, x=None, g=None, dx=None):
    S, N = a.shape
    G, Ko, Nb = w3.shape
    assert N == G * Nb
    tm = min(TM if epi == "rms_bwd" else TM_WIDE, S)
    tkk = Nb if Nb <= 1024 else 1024
    nk = Nb // tkk
    ks = G * nk
    tko = Ko if epi == "rms_bwd" else min(Ko, 512)

    def body(a_ref, w_ref, *rest):
        acc = rest[-1]
        i, k = pl.program_id(0), pl.program_id(2)

        @pl.when(k == 0)
        def _():
            acc[...] = jnp.zeros_like(acc)

        acc[...] += _dot(a_ref[...], w_ref[...], NT)

        @pl.when(k == ks - 1)
        def _():
            if epi == "plain":
                rest[0][...] = acc[...]
            elif epi == "mlp_du":
                u_ref, o_ref = rest[0], rest[1]
                o_ref[...] = (acc[...] * (2.0 * jnp.sqrt(u_ref[...].astype(F32)))).astype(o_ref.dtype)
            else:
                x_ref, g_ref, dx_ref, o_ref, dg_ref = rest[:5]
                xhat, r = _rms(x_ref[...], Ko)
                dxb, dg = _rms_bwd(acc[...], xhat, r, g_ref[...], Ko)
                o_ref[...] = dx_ref[...] + dxb

                @pl.when(i == 0)
                def _():
                    dg_ref[...] = dg

                @pl.when(i != 0)
                def _():
                    dg_ref[...] += dg

    in_specs = [pl.BlockSpec((tm, tkk), lambda i, j, k: (i, k)),
                pl.BlockSpec((None, tko, tkk), lambda i, j, k: (k // nk, j, k % nk))]
    args = [a, w3]
    tile = pl.BlockSpec((tm, tko), lambda i, j, k: (i, j))
    if epi == "plain":
        out_shape, out_specs = jax.ShapeDtypeStruct((S, Ko), F32), tile
    elif epi == "mlp_du":
        in_specs.append(tile)
        args.append(u)
        out_shape, out_specs = jax.ShapeDtypeStruct((S, Ko), MXU_DTYPE), tile
    else:
        vec = pl.BlockSpec((1, Ko), lambda i, j, k: (0, 0))
        in_specs += [tile, vec, tile]
        args += [x, g.reshape(1, Ko), dx]
        out_shape = [jax.ShapeDtypeStruct((S, Ko), F32), jax.ShapeDtypeStruct((1, Ko), F32)]
        out_specs = [tile, vec]
    return pl.pallas_call(
        body, name=name, grid=(S // tm, Ko // tko, ks),
        in_specs=in_specs, out_specs=out_specs, out_shape=out_shape,
        scratch_shapes=[pltpu.VMEM((tm, tko), F32)],
        compiler_params=_params(3),
    )(*args)


def _resident(shape):
    return pl.BlockSpec(shape, lambda i: (0,) * len(shape))


def norm_matmul(x, g, w3, *, name, mlp=False):
    S, D = x.shape
    G, _, Nb = w3.shape
    tm = min(TM, S)
    N = G * Nb

    def body(x_ref, g_ref, w_ref, h_ref, o_ref):
        xv = x_ref[...]
        r = lax.rsqrt(jnp.mean(xv * xv, axis=-1, keepdims=True) + EPS)
        h = (xv * r * g_ref[...]).astype(h_ref.dtype)
        h_ref[...] = h
        for gi in range(G):
            acc = _dot(h, w_ref[gi])
            if mlp:
                acc = jnp.square(jnp.maximum(acc, 0.0))
            o_ref[:, gi * Nb:(gi + 1) * Nb] = acc.astype(o_ref.dtype)

    rows = lambda w: pl.BlockSpec((tm, w), lambda i: (i, 0))
    return pl.pallas_call(
        body, name=name, grid=(S // tm,),
        in_specs=[rows(D), _resident((1, D)), _resident((G, D, Nb))],
        out_specs=[rows(D), rows(N)],
        out_shape=[jax.ShapeDtypeStruct((S, D), MXU_DTYPE), jax.ShapeDtypeStruct((S, N), MXU_DTYPE if mlp else F32)],
        compiler_params=_params(1),
    )(x, g.reshape(1, D), w3)


def matmul_residual(a, w, res, *, name):
    S, K = a.shape
    _, N = w.shape
    tm = min(TM, S)

    def body(a_ref, w_ref, r_ref, o_ref):
        o_ref[...] = r_ref[...] + _dot(a_ref[...], w_ref[...])

    rows = lambda w_: pl.BlockSpec((tm, w_), lambda i: (i, 0))
    return pl.pallas_call(
        body, name=name, grid=(S // tm,),
        in_specs=[rows(K), _resident((K, N)), rows(N)],
        out_specs=rows(N), out_shape=jax.ShapeDtypeStruct((S, N), F32),
        compiler_params=_params(1),
    )(a, w, res)


def matmul_nt(a, w3, *, name, epi="plain", u=None, x=None, g=None, dx=None):
    S, N = a.shape
    G, Ko, Nb = w3.shape
    assert N == G * Nb
    tm = min(TM, S)
    tko = min(Ko, 512)

    def body(a_ref, w_ref, *rest):
        if epi == "mlp_du":
            u_ref, o_ref = rest
            av = a_ref[...].astype(MXU_DTYPE)
            for j in range(Ko // tko):
                cols = slice(j * tko, (j + 1) * tko)
                da = _dot(av, w_ref[0, cols, :], NT)
                o_ref[:, cols] = (da * (2.0 * jnp.sqrt(u_ref[:, cols].astype(F32)))).astype(o_ref.dtype)
            return
        acc = _dot(a_ref[:, :Nb], w_ref[0], NT)
        for gi in range(1, G):
            acc = acc + _dot(a_ref[:, gi * Nb:(gi + 1) * Nb], w_ref[gi], NT)
        if epi == "plain":
            rest[0][...] = acc
        else:
            x_ref, g_ref, dx_ref, o_ref, dg_ref = rest
            xhat, r = _rms(x_ref[...], Ko)
            dxb, dg = _rms_bwd(acc, xhat, r, g_ref[...], Ko)
            o_ref[...] = dx_ref[...] + dxb

            @pl.when(pl.program_id(0) == 0)
            def _():
                dg_ref[...] = dg

            @pl.when(pl.program_id(0) != 0)
            def _():
                dg_ref[...] += dg

    rows = lambda w_: pl.BlockSpec((tm, w_), lambda i: (i, 0))
    in_specs = [rows(N), _resident((G, Ko, Nb))]
    args = [a, w3]
    if epi == "plain":
        out_shape, out_specs = jax.ShapeDtypeStruct((S, Ko), F32), rows(Ko)
    elif epi == "mlp_du":
        in_specs.append(rows(Ko))
        args.append(u)
        out_shape, out_specs = jax.ShapeDtypeStruct((S, Ko), MXU_DTYPE), rows(Ko)
    else:
        in_specs += [rows(Ko), _resident((1, Ko)), rows(Ko)]
        args += [x, g.reshape(1, Ko), dx]
        out_shape = [jax.ShapeDtypeStruct((S, Ko), F32), jax.ShapeDtypeStruct((1, Ko), F32)]
        out_specs = [rows(Ko), _resident((1, Ko))]
    return pl.pallas_call(
        body, name=name, grid=(S // tm,),
        in_specs=in_specs, out_specs=out_specs, out_shape=out_shape,
        compiler_params=_params(1),
    )(*args)


def mm_tn(a, b, *, name, G, out_dtype, after=None):
    order = [] if after is None else [after]
    S, Ka = a.shape
    _, N = b.shape
    Nb = N // G
    tm = min(TM, S)
    tka = min(Ka, 1024)
    tnb = Nb if Nb <= 1024 else 1024
    nj = Nb // tnb
    ns = S // tm

    def body(a_ref, b_ref, *rest):
        o_ref, acc = rest[-2:]
        s = pl.program_id(2)

        @pl.when(s == 0)
        def _():
            acc[...] = jnp.zeros_like(acc)

        acc[...] += _dot(a_ref[...], b_ref[...], TN)

        @pl.when(s == ns - 1)
        def _():
            o_ref[...] = acc[...].astype(o_ref.dtype)

    return pl.pallas_call(
        body, name=name, grid=(Ka // tka, G * nj, ns),
        in_specs=[pl.BlockSpec((tm, tka), lambda i, j, s: (s, i)),
                  pl.BlockSpec((tm, tnb), lambda i, j, s: (s, j))] + [pl.BlockSpec(memory_space=pl.ANY)] * len(order),
        out_specs=pl.BlockSpec((None, tka, tnb), lambda i, j, s: (j // nj, i, j % nj)),
        out_shape=jax.ShapeDtypeStruct((G, Ka, Nb), out_dtype),
        scratch_shapes=[pltpu.VMEM((tka, tnb), F32)],
        compiler_params=_params(3),
    )(a, b, *order)


def mla_pre_fwd(a, gqa, gkva, wuq3, wukv3, gqn, gkn, cos_t, sin_t):
    S = a.shape[0]
    tm = min(TM, S)

    def body(a_ref, gqa_ref, gkva_ref, wuq_ref, wukv_ref, gqn_ref, gkn_ref, cos_ref, sin_ref,
             q_ref, k_ref, v_ref, cq_ref, ckv_ref, cqs, ckvs):
        @pl.when(pl.program_id(1) == 0)
        def _():
            av = a_ref[...]
            cq = (_rms(av[:, :Q_LORA], Q_LORA)[0] * gqa_ref[...]).astype(cqs.dtype)
            ckv = (_rms(av[:, Q_LORA:Q_LORA + KV_LORA], KV_LORA)[0] * gkva_ref[...]).astype(ckvs.dtype)
            cqs[...] = cq
            cq_ref[...] = cq
            ckvs[...] = ckv
            ckv_ref[...] = ckv

        cos_v, sin_v = cos_ref[...], sin_ref[...]
        qn = _rms(_dot(cqs[...], wuq_ref[...]), QK_DIM)[0] * gqn_ref[...]
        qr = jnp.concatenate([qn[:, :NOPE], _rope(qn[:, NOPE:], cos_v, sin_v)], axis=1)
        q_ref[...] = (qr * Q_PRESCALE).astype(q_ref.dtype)
        kvp = _dot(ckvs[...], wukv_ref[...])
        kk = jnp.concatenate([kvp[:, :NOPE], a_ref[:, Q_LORA + KV_LORA:]], axis=1)
        kn = _rms(kk, QK_DIM)[0] * gkn_ref[...]
        k_ref[...] = jnp.concatenate([kn[:, :NOPE], _rope(kn[:, NOPE:], cos_v, sin_v)], axis=1).astype(k_ref.dtype)
        v_ref[...] = kvp[:, NOPE:].astype(v_ref.dtype)

    row = lambda w: pl.BlockSpec((tm, w), lambda i, h: (i, 0))
    vec = lambda w: pl.BlockSpec((1, w), lambda i, h: (0, 0))
    head = lambda w: pl.BlockSpec((None, tm, w), lambda i, h: (h, i, 0))
    return pl.pallas_call(
        body, name="mla_pre_fwd", grid=(S // tm, N_HEADS),
        in_specs=[row(DOWN_PAD), vec(Q_LORA), vec(KV_LORA),
                  pl.BlockSpec((None, Q_LORA, QK_PAD), lambda i, h: (h, 0, 0)),
                  pl.BlockSpec((None, KV_LORA, NOPE + V_DIM), lambda i, h: (h, 0, 0)),
                  vec(QK_PAD), vec(QK_PAD), row(128), row(128)],
        out_specs=[head(QK_PAD), head(QK_PAD), head(V_DIM), row(Q_LORA), row(KV_LORA)],
        out_shape=[jax.ShapeDtypeStruct((N_HEADS, S, QK_PAD), MXU_DTYPE),
                   jax.ShapeDtypeStruct((N_HEADS, S, QK_PAD), MXU_DTYPE),
                   jax.ShapeDtypeStruct((N_HEADS, S, V_DIM), MXU_DTYPE),
                   jax.ShapeDtypeStruct((S, Q_LORA), MXU_DTYPE),
                   jax.ShapeDtypeStruct((S, KV_LORA), MXU_DTYPE)],
        scratch_shapes=[pltpu.VMEM((tm, Q_LORA), MXU_DTYPE), pltpu.VMEM((tm, KV_LORA), MXU_DTYPE)],
        compiler_params=_params(2),
    )(a, gqa, gkva, wuq3, wukv3, gqn, gkn, cos_t, sin_t)


def mla_pre_bwd(dq, dk, dv, a, cq, ckv, gqa, gkva, wuq3, wukv3, gqn, gkn, cos_t, sin_t):
    S = a.shape[0]
    tm = min(TM, S)
    H = N_HEADS

    def body(dq_ref, dk_ref, dv_ref, a_ref, cq_ref, ckv_ref, gqa_ref, gkva_ref, wuq_ref, wukv_ref, gqn_ref, gkn_ref,
             cos_ref, sin_ref, da_ref, dwuq_ref, dwukv_ref, dgqn_ref, dgkn_ref, dgqa_ref, dgkva_ref,
             dcq, dckv, dkpe):
        i, h = pl.program_id(0), pl.program_id(1)

        @pl.when((i == 0) & (h == 0))
        def _():
            for ref in (dwuq_ref, dwukv_ref, dgqn_ref, dgkn_ref, dgqa_ref, dgkva_ref):
                ref[...] = jnp.zeros_like(ref)

        @pl.when(h == 0)
        def _():
            dcq[...] = jnp.zeros_like(dcq)
            dckv[...] = jnp.zeros_like(dckv)
            dkpe[...] = jnp.zeros_like(dkpe)

        cos_v, sin_v = cos_ref[...], sin_ref[...]
        cqv, ckvv = cq_ref[...], ckv_ref[...]
        wuq, wukv = wuq_ref[...], wukv_ref[...]

        qhat, rq = _rms(_dot(cqv, wuq), QK_DIM)
        dqr = dq_ref[...] * SM_SCALE
        dqn = jnp.concatenate([dqr[:, :NOPE], _rope_bwd(dqr[:, NOPE:], cos_v, sin_v)], axis=1)
        dqp, dg = _rms_bwd(dqn, qhat, rq, gqn_ref[...], QK_DIM)
        dgqn_ref[...] += dg
        dqp = dqp.astype(MXU_DTYPE)
        dwuq_ref[h] += _dot(cqv, dqp, TN)
        dcq[...] += _dot(dqp, wuq, NT)

        kvp = _dot(ckvv, wukv)
        kk = jnp.concatenate([kvp[:, :NOPE], a_ref[:, Q_LORA + KV_LORA:]], axis=1)
        khat, rk = _rms(kk, QK_DIM)
        dkr = dk_ref[...] * (1.0 / LOG2E)
        dkn = jnp.concatenate([dkr[:, :NOPE], _rope_bwd(dkr[:, NOPE:], cos_v, sin_v)], axis=1)
        dkk, dg = _rms_bwd(dkn, khat, rk, gkn_ref[...], QK_DIM)
        dgkn_ref[...] += dg
        dkpe[...] += dkk[:, NOPE:]
        dkvp = jnp.concatenate([dkk[:, :NOPE], dv_ref[...]], axis=1).astype(MXU_DTYPE)
        dwukv_ref[h] += _dot(ckvv, dkvp, TN)
        dckv[...] += _dot(dkvp, wukv, NT)

        @pl.when(h == H - 1)
        def _():
            av = a_ref[...]
            ahat, r = _rms(av[:, :Q_LORA], Q_LORA)
            daq, dg = _rms_bwd(dcq[...], ahat, r, gqa_ref[...], Q_LORA)
            dgqa_ref[...] += dg
            ahat, r = _rms(av[:, Q_LORA:Q_LORA + KV_LORA], KV_LORA)
            dakv, dg = _rms_bwd(dckv[...], ahat, r, gkva_ref[...], KV_LORA)
            dgkva_ref[...] += dg
            da_ref[...] = jnp.concatenate([daq, dakv, dkpe[...]], axis=1)

    row = lambda w: pl.BlockSpec((tm, w), lambda i, h: (i, 0))
    vec = lambda w: pl.BlockSpec((1, w), lambda i, h: (0, 0))
    head = lambda w: pl.BlockSpec((None, tm, w), lambda i, h: (h, i, 0))
    full3 = lambda s: pl.BlockSpec(s, lambda i, h: (0, 0, 0))
    return pl.pallas_call(
        body, name="mla_pre_bwd", grid=(S // tm, H),
        in_specs=[head(QK_PAD), head(QK_PAD), head(V_DIM), row(DOWN_PAD), row(Q_LORA), row(KV_LORA),
                  vec(Q_LORA), vec(KV_LORA),
                  pl.BlockSpec((None, Q_LORA, QK_PAD), lambda i, h: (h, 0, 0)),
                  pl.BlockSpec((None, KV_LORA, NOPE + V_DIM), lambda i, h: (h, 0, 0)),
                  vec(QK_PAD), vec(QK_PAD), row(128), row(128)],
        out_specs=[row(DOWN_PAD), full3((H, Q_LORA, QK_PAD)), full3((H, KV_LORA, NOPE + V_DIM)),
                   vec(QK_PAD), vec(QK_PAD), vec(Q_LORA), vec(KV_LORA)],
        out_shape=[jax.ShapeDtypeStruct((S, DOWN_PAD), F32),
                   jax.ShapeDtypeStruct((H, Q_LORA, QK_PAD), F32),
                   jax.ShapeDtypeStruct((H, KV_LORA, NOPE + V_DIM), F32),
                   jax.ShapeDtypeStruct((1, QK_PAD), F32), jax.ShapeDtypeStruct((1, QK_PAD), F32),
                   jax.ShapeDtypeStruct((1, Q_LORA), F32), jax.ShapeDtypeStruct((1, KV_LORA), F32)],
        scratch_shapes=[pltpu.VMEM((tm, Q_LORA), F32), pltpu.VMEM((tm, KV_LORA), F32), pltpu.VMEM((tm, 128), F32)],
        compiler_params=_params(2),
    )(dq, dk, dv, a, cq, ckv, gqa, gkva, wuq3, wukv3, gqn, gkn, cos_t, sin_t)


def _pair_tables(nb, key_major):
    if key_major:
        pairs = [(qi, kj) for kj in range(nb) for qi in range(kj, nb)]
    else:
        pairs = [(qi, ki) for qi in range(nb) for ki in range(qi + 1)]
    return (jnp.asarray(np.array([p[0] for p in pairs], np.int32)),
            jnp.asarray(np.array([p[1] for p in pairs], np.int32)))


def _scores_t(k, q, pk_col, pq_row, masked):
    s = _dot(k, q, NT)
    return jnp.where(pq_row >= pk_col, s, jnp.finfo(F32).min) if masked else s


def attn_fwd(q, k, v, pos_col, pos_row):
    H, S, _ = q.shape
    t = min(TQ, S)
    nb = S // t
    hb = HEADS_FWD
    qt, kt = _pair_tables(nb, key_major=False)

    def body(qt_ref, kt_ref, q_ref, k_ref, v_ref, pk_ref, pq_ref, o_ref, lse_ref, m_s, l_s, acc):
        step = pl.program_id(1)
        qi, ki = qt_ref[step], kt_ref[step]

        @pl.when(ki == 0)
        def _():
            m_s[...] = jnp.full_like(m_s, -jnp.inf)
            l_s[...] = jnp.zeros_like(l_s)
            acc[...] = jnp.zeros_like(acc)

        def update(masked):
            scores = lambda hh: _scores_t(k_ref[hh], q_ref[hh], pk_ref[...], pq_ref[...], masked)
            s_next = scores(0)
            for hh in range(hb):
                s = s_next
                if hh + 1 < hb:
                    s_next = scores(hh + 1)
                m_old = m_s[hh]
                m_new = jnp.maximum(m_old, jnp.max(s, axis=0, keepdims=True))
                p = jnp.exp2(s - m_new)
                alpha = jnp.exp2(m_old - m_new)
                l_s[hh] = alpha * l_s[hh] + jnp.sum(p, axis=0, keepdims=True)
                acc[hh] = alpha * acc[hh] + _dot(v_ref[hh], p, TN)
                m_s[hh] = m_new

        @pl.when(ki < qi)
        def _():
            update(False)

        @pl.when(ki == qi)
        def _():
            update(True)
            for hh in range(hb):
                o_ref[:, hh * V_DIM:(hh + 1) * V_DIM] = (acc[hh] / l_s[hh]).T
                lse_ref[hh] = m_s[hh] + jnp.log(l_s[hh]) * LOG2E

    grid_spec = pltpu.PrefetchScalarGridSpec(
        num_scalar_prefetch=2, grid=(H // hb, qt.shape[0]),
        in_specs=[pl.BlockSpec((hb, t, QK_PAD), lambda h, s, qt, kt: (h, qt[s], 0)),
                  pl.BlockSpec((hb, t, QK_PAD), lambda h, s, qt, kt: (h, kt[s], 0)),
                  pl.BlockSpec((hb, t, V_DIM), lambda h, s, qt, kt: (h, kt[s], 0)),
                  pl.BlockSpec((t, 1), lambda h, s, qt, kt: (kt[s], 0)),
                  pl.BlockSpec((1, t), lambda h, s, qt, kt: (0, qt[s]))],
        out_specs=[pl.BlockSpec((t, hb * V_DIM), lambda h, s, qt, kt: (qt[s], h)),
                   pl.BlockSpec((hb, 1, t), lambda h, s, qt, kt: (h, 0, qt[s]))],
        scratch_shapes=[pltpu.VMEM((hb, 1, t), F32), pltpu.VMEM((hb, 1, t), F32), pltpu.VMEM((hb, V_DIM, t), F32)])
    return pl.pallas_call(
        body, name="attn_fwd", grid_spec=grid_spec,
        out_shape=[jax.ShapeDtypeStruct((S, H * V_DIM), F32), jax.ShapeDtypeStruct((H, 1, S), F32)],
        compiler_params=_params(2),
    )(qt, kt, q, k, v, pos_col, pos_row)


def attn_delta(o, do):
    S = o.shape[0]
    t = min(TQ, S)

    def body(o_ref, do_ref, d_ref):
        d_ref[...] = jnp.sum((o_ref[...] * do_ref[...]).T, axis=0, keepdims=True)

    blk = pl.BlockSpec((t, V_DIM), lambda h, i: (i, h))
    return pl.pallas_call(
        body, name="attn_delta", grid=(N_HEADS, S // t),
        in_specs=[blk, blk],
        out_specs=pl.BlockSpec((None, 1, t), lambda h, i: (h, 0, i)),
        out_shape=jax.ShapeDtypeStruct((N_HEADS, 1, S), F32),
        compiler_params=_params(2),
    )(o, do)


def attn_bwd(q, k, v, do, lse, delta, pos_col, pos_row):
    H, S, _ = q.shape
    t = min(TQ_BWD, S)
    nb = S // t
    qt, kt = _pair_tables(nb, key_major=True)
    tc = min(BWD_CHUNK, t)

    def body(qt_ref, kt_ref, q_ref, k_ref, v_ref, do_ref, lse_ref, dl_ref, pk_ref, pq_ref, dq_ref, dk_ref, dv_ref):
        step = pl.program_id(1)
        qi, kj = qt_ref[step], kt_ref[step]

        @pl.when(step == 0)
        def _():
            dq_ref[...] = jnp.zeros_like(dq_ref)

        @pl.when(qi == kj)
        def _():
            dk_ref[...] = jnp.zeros_like(dk_ref)
            dv_ref[...] = jnp.zeros_like(dv_ref)

        def update(masked):
            kv_, vv = k_ref[...], v_ref[...]

            def first_matmuls(c):
                cols = slice(c * tc, (c + 1) * tc)
                qc = q_ref[cols, :]
                doc = do_ref[cols, :].astype(MXU_DTYPE)
                return qc, doc, _scores_t(kv_, qc, pk_ref[...], pq_ref[:, cols], masked), _dot(vv, doc, NT)

            nxt = first_matmuls(0)
            for c in range(t // tc):
                qc, doc, s, dp = nxt
                if c + 1 < t // tc:
                    nxt = first_matmuls(c + 1)
                cols = slice(c * tc, (c + 1) * tc)
                p = jnp.exp2(s - lse_ref[:, cols])
                ds = (p * (dp - dl_ref[:, cols])).astype(MXU_DTYPE)
                dv_ref[...] += _dot(p, doc)
                dk_ref[...] += _dot(ds, qc)
                rows = pl.ds(pl.multiple_of(qi * t + c * tc, tc), tc)
                dq_ref[rows, :] += _dot(ds, kv_, TN)

        @pl.when(qi == kj)
        def _():
            update(True)

        @pl.when(qi != kj)
        def _():
            update(False)

    q_idx = lambda h, s, qt, kt: (h, qt[s], 0)
    k_idx = lambda h, s, qt, kt: (h, kt[s], 0)
    row_idx = lambda h, s, qt, kt: (h, 0, qt[s])
    grid_spec = pltpu.PrefetchScalarGridSpec(
        num_scalar_prefetch=2, grid=(H, qt.shape[0]),
        in_specs=[pl.BlockSpec((None, t, QK_PAD), q_idx),
                  pl.BlockSpec((None, t, QK_PAD), k_idx),
                  pl.BlockSpec((None, t, V_DIM), k_idx),
                  pl.BlockSpec((t, V_DIM), lambda h, s, qt, kt: (qt[s], h)),
                  pl.BlockSpec((None, 1, t), row_idx),
                  pl.BlockSpec((None, 1, t), row_idx),
                  pl.BlockSpec((t, 1), lambda h, s, qt, kt: (kt[s], 0)),
                  pl.BlockSpec((1, t), lambda h, s, qt, kt: (0, qt[s]))],
        out_specs=[pl.BlockSpec((None, S, QK_PAD), lambda h, s, qt, kt: (h, 0, 0)),
                   pl.BlockSpec((None, t, QK_PAD), k_idx),
                   pl.BlockSpec((None, t, V_DIM), k_idx)])
    return pl.pallas_call(
        body, name="attn_bwd", grid_spec=grid_spec,
        out_shape=[jax.ShapeDtypeStruct((H, S, QK_PAD), F32), jax.ShapeDtypeStruct((H, S, QK_PAD), F32),
                   jax.ShapeDtypeStruct((H, S, V_DIM), F32)],
        compiler_params=_params(2),
    )(qt, kt, q, k, v, do, lse, delta, pos_col, pos_row)


def _conv_specs(S, tr, tc):
    nc = D_MODEL // tc
    hb = tr // 8
    main = lambda third: pl.BlockSpec((tr, tc), lambda c, r: (r, third * nc + c))
    prev = lambda third: pl.BlockSpec((8, tc), lambda c, r: (jnp.maximum(r * hb - 1, 0), third * nc + c))
    nxt = lambda third: pl.BlockSpec((8, tc), lambda c, r: (jnp.minimum((r + 1) * hb, S // 8 - 1), third * nc + c))
    return main, prev, nxt


def _conv_taps(gc, uu, w_ref, first):
    u2 = gc * uu
    rows = lax.broadcasted_iota(jnp.int32, u2.shape, 0)
    u2 = jnp.where((rows < 8) & first, 0.0, u2)
    s1 = pltpu.roll(u2, 1, 0)
    s2 = pltpu.roll(u2, 2, 0)
    u3 = w_ref[2:3, :] * u2 + w_ref[1:2, :] * s1 + w_ref[0:1, :] * s2
    return u2, s1, s2, u3


def conv_fwd(bcu, cw):
    S = bcu.shape[0]
    tr, tc = min(TROW, S), TCH
    main, prev, _ = _conv_specs(S, tr, tc)

    def body(gb_ref, gc_ref, u_ref, gch_ref, uh_ref, w_ref, z_ref):
        gc = jnp.concatenate([gch_ref[...], gc_ref[...]], axis=0)
        uu = jnp.concatenate([uh_ref[...], u_ref[...]], axis=0)
        u3 = _conv_taps(gc, uu, w_ref, pl.program_id(1) == 0)[3]
        z_ref[...] = (gb_ref[...] * u3[8:]).astype(z_ref.dtype)

    return pl.pallas_call(
        body, name="conv_fwd", grid=(D_MODEL // tc, S // tr),
        in_specs=[main(0), main(1), main(2), prev(1), prev(2), pl.BlockSpec((3, tc), lambda c, r: (0, c))],
        out_specs=pl.BlockSpec((tr, tc), lambda c, r: (r, c)),
        out_shape=jax.ShapeDtypeStruct((S, D_MODEL), MXU_DTYPE),
        compiler_params=_params(2),
    )(bcu, bcu, bcu, bcu, bcu, cw)


def conv_bwd(dz, bcu, cw):
    S = bcu.shape[0]
    tr, tc = min(TROW, S), TCH
    nr = S // tr
    main, prev, nxt = _conv_specs(S, tr, tc)

    def body(dz_ref, dzn_ref, gb_ref, gbn_ref, gc_ref, u_ref, gch_ref, uh_ref, w_ref,
             dgb_ref, dgc_ref, du_ref, dw_ref):
        r = pl.program_id(1)
        gcv, uv = gc_ref[...], u_ref[...]
        gc = jnp.concatenate([gch_ref[...], gcv], axis=0)
        uu = jnp.concatenate([uh_ref[...], uv], axis=0)
        u2, s1, s2, u3 = _conv_taps(gc, uu, w_ref, r == 0)
        dzv = dz_ref[...]
        du3 = jnp.concatenate([dzv * gb_ref[...], dzn_ref[...] * gbn_ref[...]], axis=0)
        rows = lax.broadcasted_iota(jnp.int32, du3.shape, 0)
        du3 = jnp.where((rows >= tr) & (r == nr - 1), 0.0, du3)
        n1 = pltpu.roll(du3, tr + 8 - 1, 0)
        n2 = pltpu.roll(du3, tr + 8 - 2, 0)
        du2 = (w_ref[2:3, :] * du3 + w_ref[1:2, :] * n1 + w_ref[0:1, :] * n2)[:tr]
        dgb_ref[...] = (dzv * u3[8:]).astype(dgb_ref.dtype)
        dgc_ref[...] = (du2 * uv).astype(dgc_ref.dtype)
        du_ref[...] = (du2 * gcv).astype(du_ref.dtype)
        d3 = du3[:tr]
        taps = [jnp.sum(d3 * t[8:], axis=0, keepdims=True) for t in (s2, s1, u2)]

        @pl.when(r == 0)
        def _():
            for kk in range(3):
                dw_ref[kk:kk + 1, :] = taps[kk]

        @pl.when(r != 0)
        def _():
            for kk in range(3):
                dw_ref[kk:kk + 1, :] += taps[kk]

    out = pl.BlockSpec((tr, tc), lambda c, r: (r, c))
    nxt_dz = pl.BlockSpec((8, tc), lambda c, r: (jnp.minimum((r + 1) * (tr // 8), S // 8 - 1), c))
    act = jax.ShapeDtypeStruct((S, D_MODEL), MXU_DTYPE)
    return pl.pallas_call(
        body, name="conv_bwd", grid=(D_MODEL // tc, nr),
        in_specs=[out, nxt_dz, main(0), nxt(0), main(1), main(2), prev(1), prev(2),
                  pl.BlockSpec((3, tc), lambda c, r: (0, c))],
        out_specs=[out, out, out, pl.BlockSpec((3, tc), lambda c, r: (0, c))],
        out_shape=[act, act, act, jax.ShapeDtypeStruct((3, D_MODEL), F32)],
        compiler_params=_params(2),
    )(dz, dz, bcu, bcu, bcu, bcu, bcu, bcu, cw)


def loss_head(y, target):
    S, D = y.shape
    tm = min(TM, S)

    def body(y_ref, t_ref, l_ref, dy_ref):
        err = y_ref[...] - t_ref[...]
        dy_ref[...] = err / D
        part = jnp.full((1, 128), jnp.sum(err * err), F32)

        @pl.when(pl.program_id(0) == 0)
        def _():
            l_ref[...] = part

        @pl.when(pl.program_id(0) != 0)
        def _():
            l_ref[...] += part

    blk = pl.BlockSpec((tm, D), lambda i: (i, 0))
    return pl.pallas_call(
        body, name="loss_head", grid=(S // tm,),
        in_specs=[blk, blk],
        out_specs=[pl.BlockSpec((1, 128), lambda i: (0, 0)), blk],
        out_shape=[jax.ShapeDtypeStruct((1, 128), F32), jax.ShapeDtypeStruct((S, D), F32)],
        compiler_params=_params(1),
    )(y, target)


def adamw(parts, w, m, v, *, name):
    R, C = w.shape
    tr = R
    while tr * C * 4 > (1 << 20) and tr % 32 == 0:
        tr //= 2

    def body(p_ref, w_ref, m_ref, v_ref, g_ref, d_ref, mo_ref, vo_ref):
        g = p_ref[0].astype(F32)
        for d in range(1, N_DEV):
            g = g + p_ref[d].astype(F32)
        m_new = ADAM_B1 * m_ref[...] + (1.0 - ADAM_B1) * g
        v_new = ADAM_B2 * v_ref[...] + (1.0 - ADAM_B2) * (g * g)
        m_hat = m_new / (1.0 - ADAM_B1 ** ADAM_STEP)
        v_hat = v_new / (1.0 - ADAM_B2 ** ADAM_STEP)
        g_ref[...] = g
        d_ref[...] = -ADAM_LR * (m_hat / (jnp.sqrt(v_hat) + ADAM_EPS) + ADAM_WD * w_ref[...])
        mo_ref[...] = m_new
        vo_ref[...] = v_new

    blk = pl.BlockSpec((tr, C), lambda i: (i, 0))
    return pl.pallas_call(
        body, name=name, grid=(R // tr,),
        in_specs=[pl.BlockSpec((N_DEV, tr, C), lambda i: (0, i, 0)), blk, blk, blk],
        out_specs=[blk, blk, blk, blk],
        out_shape=[jax.ShapeDtypeStruct((R, C), F32)] * 4,
        compiler_params=_params(1),
    )(parts, w, m, v)


def _mesh_place():
    x, y, c = (lax.axis_index(n) for n in MESH_AXES)
    return x, y, c, 4 * x + 2 * y + c


def _peer(x, y, c, d):
    px = 1 - x if d & 4 else x
    py = 1 - y if d & 2 else y
    pc = 1 - c if d & 1 else c
    return (px, py, pc), 4 * px + 2 * py + pc


def _exchange(srcs, name, scatter):
    n = len(srcs)
    any_spec = pl.BlockSpec(memory_space=pl.ANY)

    def body(*refs):
        ins, outs = refs[:n], refs[n:2 * n]
        send_sems, recv_sems, local_sems = refs[2 * n:]
        x, y, c, me = _mesh_place()
        for a in range(n):
            mine = ins[a].at[me] if scatter else ins[a]
            pltpu.make_async_copy(mine, outs[a].at[me], local_sems.at[a]).start()
            for d in range(1, N_DEV):
                peer, peer_lin = _peer(x, y, c, d)
                pltpu.make_async_remote_copy(
                    src_ref=ins[a].at[peer_lin] if scatter else ins[a], dst_ref=outs[a].at[me],
                    send_sem=send_sems.at[a], recv_sem=recv_sems.at[a],
                    device_id=peer, device_id_type=pl.DeviceIdType.MESH).start()
        for a in range(n):
            mine = ins[a].at[me] if scatter else ins[a]
            pltpu.make_async_copy(mine, outs[a].at[me], local_sems.at[a]).wait()
            seven = outs[a].at[pl.ds(0, N_DEV - 1)]
            drain = pltpu.make_async_remote_copy(
                src_ref=seven, dst_ref=seven, send_sem=send_sems.at[a], recv_sem=recv_sems.at[a],
                device_id=(x, y, c), device_id_type=pl.DeviceIdType.MESH)
            drain.wait_send()
            drain.wait_recv()

    block = (lambda s: s.shape[1:]) if scatter else (lambda s: s.shape)
    return pl.pallas_call(
        body, name=name,
        in_specs=[any_spec] * n, out_specs=[any_spec] * n,
        out_shape=[jax.ShapeDtypeStruct((N_DEV,) + tuple(block(s)), s.dtype) for s in srcs],
        scratch_shapes=[pltpu.SemaphoreType.DMA((n,)), pltpu.SemaphoreType.DMA((n,)), pltpu.SemaphoreType.DMA((n,))],
    )(*srcs)


_ANY = pl.BlockSpec(memory_space=pl.ANY)
_HBM = pl.BlockSpec(memory_space=pltpu.HBM)
_SEM = pl.BlockSpec(memory_space=pltpu.SEMAPHORE)


def _in_hbm(arrays):
    return [pltpu.with_memory_space_constraint(a, pltpu.HBM) for a in arrays]


def place_own(shards, after, *, name):
    n = len(shards)

    def body(*refs):
        ins, outs, sems = refs[:n], refs[n + 1:2 * n + 1], refs[2 * n + 1]
        me = _mesh_place()[3]
        copies = [pltpu.make_async_copy(ins[a], outs[a].at[me], sems.at[a]) for a in range(n)]
        for cp in copies:
            cp.start()
        for cp in copies:
            cp.wait()

    return pl.pallas_call(
        body, name=name, in_specs=[_ANY] * (n + 1), out_specs=[_ANY] * n,
        out_shape=[jax.ShapeDtypeStruct((N_DEV,) + s.shape, s.dtype) for s in shards],
        scratch_shapes=[pltpu.SemaphoreType.DMA((n,))],
    )(*shards, after)


def exchange_start(srcs, lands, slots, *, name):
    n, m = len(srcs), len(lands)

    def body(*refs):
        ins, zones = refs[:n], refs[n:n + m]
        send_sems, recv_sems, token = refs[n + m], refs[n + m + 1], refs[-1]
        x, y, c, me = _mesh_place()
        for a in range(n):
            for d in range(1, N_DEV):
                peer, peer_lin = _peer(x, y, c, d)
                src = ins[a] if slots is None else ins[a].at[peer_lin]
                dst = zones[a].at[me] if slots is None else zones[slots[a][0]].at[me, slots[a][1]]
                pltpu.make_async_remote_copy(
                    src_ref=src, dst_ref=dst, send_sem=send_sems.at[a], recv_sem=recv_sems.at[a],
                    device_id=peer, device_id_type=pl.DeviceIdType.MESH).start()
        token[...] = jnp.zeros_like(token)

    both = list(srcs) + list(lands)
    out = pl.pallas_call(
        body, name=name,
        in_specs=[_HBM] * (n + m),
        out_specs=[_SEM, _SEM] + [_HBM] * (n + m) + [pl.BlockSpec(memory_space=pltpu.VMEM)],
        out_shape=[pltpu.SemaphoreType.DMA((n,)), pltpu.SemaphoreType.DMA((n,))]
        + [pltpu.HBM(a.shape, a.dtype) for a in both] + [jax.ShapeDtypeStruct((8, 128), F32)],
        input_output_aliases={i: 2 + i for i in range(n + m)},
        compiler_params=pltpu.CompilerParams(has_side_effects=pltpu.SideEffectType.DATAFLOW_SIDE_EFFECTING),
    )(*_in_hbm(both))
    return out[0], out[1], out[2:2 + n], out[2 + n:2 + n + m], out[-1]


def exchange_wait(send_sems, recv_sems, srcs, lands, slots, after, *, name):
    n, m = len(srcs), len(lands)

    def body(*refs):
        ins, zones = refs[:n], refs[n:n + m]
        send_ref, recv_ref = refs[n + m], refs[n + m + 1]
        x, y, c, _ = _mesh_place()
        for a in range(n):
            seven = (zones[a] if slots is None else ins[a]).at[pl.ds(0, N_DEV - 1)]
            drain = pltpu.make_async_remote_copy(
                src_ref=seven, dst_ref=seven, send_sem=send_ref.at[a], recv_sem=recv_ref.at[a],
                device_id=(x, y, c), device_id_type=pl.DeviceIdType.MESH)
            drain.wait_send()
            drain.wait_recv()

    both = list(srcs) + list(lands)
    out = pl.pallas_call(
        body, name=name,
        in_specs=[_HBM] * (n + m) + [_SEM, _SEM, _ANY],
        out_specs=[_HBM] * (n + m),
        out_shape=[pltpu.HBM(a.shape, a.dtype) for a in both],
        input_output_aliases={i: i for i in range(n + m)},
        compiler_params=pltpu.CompilerParams(has_side_effects=pltpu.SideEffectType.DATAFLOW_SIDE_EFFECTING),
    )(*both, send_sems, recv_sems, after)
    return out[:n], out[n:]


def scatter_finish(remote, local, lands, vec, *, name):
    every = list(remote) + list(local)
    n, r, m = len(every), len(remote), len(lands)

    def body(*refs):
        ins, vec_ref, zones_in = refs[:n], refs[n], refs[n + 1:n + 1 + m]
        vec_out = refs[n + 1 + 2 * m]
        send_sems, recv_sems, local_sems = refs[n + 2 + 2 * m:]
        x, y, c, me = _mesh_place()

        def own(a):
            if a == n:
                return pltpu.make_async_copy(vec_ref, vec_out.at[me], local_sems.at[a])
            return pltpu.make_async_copy(ins[a].at[me], zones_in[every[a][1]].at[me, every[a][2]], local_sems.at[a])

        for a in range(n + 1):
            own(a).start()
        for a in list(range(r)) + [n]:
            for d in range(1, N_DEV):
                peer, peer_lin = _peer(x, y, c, d)
                src = vec_ref if a == n else ins[a].at[peer_lin]
                dst = vec_out.at[me] if a == n else zones_in[every[a][1]].at[me, every[a][2]]
                pltpu.make_async_remote_copy(
                    src_ref=src, dst_ref=dst, send_sem=send_sems.at[min(a, r)], recv_sem=recv_sems.at[min(a, r)],
                    device_id=peer, device_id_type=pl.DeviceIdType.MESH).start()
        for a in range(n + 1):
            own(a).wait()
        for a in list(range(r)) + [n]:
            seven = (vec_out if a == n else ins[a]).at[pl.ds(0, N_DEV - 1)]
            drain = pltpu.make_async_remote_copy(
                src_ref=seven, dst_ref=seven, send_sem=send_sems.at[min(a, r)], recv_sem=recv_sems.at[min(a, r)],
                device_id=(x, y, c), device_id_type=pl.DeviceIdType.MESH)
            drain.wait_send()
            drain.wait_recv()

    out = pl.pallas_call(
        body, name=name,
        in_specs=[_ANY] * (n + 1 + m), out_specs=[_ANY] * (m + 1),
        out_shape=[jax.ShapeDtypeStruct(z.shape, z.dtype) for z in lands]
        + [jax.ShapeDtypeStruct((N_DEV,) + vec.shape, vec.dtype)],
        input_output_aliases={n + 1 + i: i for i in range(m)},
        scratch_shapes=[pltpu.SemaphoreType.DMA((r + 1,)), pltpu.SemaphoreType.DMA((r + 1,)),
                        pltpu.SemaphoreType.DMA((n + 1,))],
    )(*[e[0] for e in every], vec, *lands)
    return out[:m], out[m]


def _rope_tables(pos):
    inv_freq = ROPE_THETA ** (-jnp.arange(0, ROPE, 2, dtype=F32) / ROPE)
    ang = pos.astype(F32)[:, None] * inv_freq
    cos, sin = jnp.cos(ang), jnp.sin(ang)
    pad = jnp.zeros((pos.shape[0], 128 - ROPE), F32)
    return jnp.concatenate([cos, cos, pad + 1.0], axis=1), jnp.concatenate([-sin, sin, pad], axis=1)


def _pad_last(w, n):
    return jnp.pad(w, [(0, 0)] * (w.ndim - 1) + [(0, n - w.shape[-1])])


def kernel(x, positions, g_mix, g_mlp, attn_w_down, attn_g_q_a, attn_g_kv_a, attn_w_uq, attn_w_ukv, attn_g_qnorm, attn_g_knorm, attn_w_o, conv_w_in, conv_w, conv_w_out, mlp_w1, mlp_w2, loss_target, m_g_mix, m_g_mlp, m_attn_w_down, m_attn_g_q_a, m_attn_g_kv_a, m_attn_w_uq, m_attn_w_ukv, m_attn_g_qnorm, m_attn_g_knorm, m_attn_w_o, m_conv_w_in, m_conv_w, m_conv_w_out, m_mlp_w1, m_mlp_w2, v_g_mix, v_g_mlp, v_attn_w_down, v_attn_g_q_a, v_attn_g_kv_a, v_attn_w_uq, v_attn_w_ukv, v_attn_g_qnorm, v_attn_g_knorm, v_attn_w_o, v_conv_w_in, v_conv_w, v_conv_w_out, v_mlp_w1, v_mlp_w2):
    weights = dict(g_mix=g_mix, g_mlp=g_mlp, attn_w_down=attn_w_down, attn_g_q_a=attn_g_q_a, attn_g_kv_a=attn_g_kv_a,
                   attn_w_uq=attn_w_uq, attn_w_ukv=attn_w_ukv, attn_g_qnorm=attn_g_qnorm, attn_g_knorm=attn_g_knorm,
                   attn_w_o=attn_w_o, conv_w_in=conv_w_in, conv_w=conv_w, conv_w_out=conv_w_out, mlp_w1=mlp_w1, mlp_w2=mlp_w2)
    mom1 = dict(g_mix=m_g_mix, g_mlp=m_g_mlp, attn_w_down=m_attn_w_down, attn_g_q_a=m_attn_g_q_a, attn_g_kv_a=m_attn_g_kv_a,
                attn_w_uq=m_attn_w_uq, attn_w_ukv=m_attn_w_ukv, attn_g_qnorm=m_attn_g_qnorm, attn_g_knorm=m_attn_g_knorm,
                attn_w_o=m_attn_w_o, conv_w_in=m_conv_w_in, conv_w=m_conv_w, conv_w_out=m_conv_w_out, mlp_w1=m_mlp_w1, mlp_w2=m_mlp_w2)
    mom2 = dict(g_mix=v_g_mix, g_mlp=v_g_mlp, attn_w_down=v_attn_w_down, attn_g_q_a=v_attn_g_q_a, attn_g_kv_a=v_attn_g_kv_a,
                attn_w_uq=v_attn_w_uq, attn_w_ukv=v_attn_w_ukv, attn_g_qnorm=v_attn_g_qnorm, attn_g_knorm=v_attn_g_knorm,
                attn_w_o=v_attn_w_o, conv_w_in=v_conv_w_in, conv_w=v_conv_w, conv_w_out=v_conv_w_out, mlp_w1=v_mlp_w1, mlp_w2=v_mlp_w2)
    big = ["attn_w_down", "attn_w_uq", "attn_w_ukv", "attn_w_o", "conv_w_in", "conv_w", "conv_w_out", "mlp_w1", "mlp_w2"]
    small = ["g_mix", "g_mlp", "attn_g_q_a", "attn_g_kv_a", "attn_g_qnorm", "attn_g_knorm"]
    order = ["g_mix", "g_mlp", "attn_w_down", "attn_g_q_a", "attn_g_kv_a", "attn_w_uq", "attn_w_ukv", "attn_g_qnorm",
             "attn_g_knorm", "attn_w_o", "conv_w_in", "conv_w", "conv_w_out", "mlp_w1", "mlp_w2"]

    xs = x[0]
    pos = positions[0]
    target = loss_target[0]
    S = xs.shape[0]
    depth = g_mix.shape[0]
    cos_t, sin_t = _rope_tables(pos)
    pos_col, pos_row = pos.reshape(S, 1), pos.reshape(1, S)

    keys, shards = [], []
    for name in big:
        for l in range(weights[name].shape[0]):
            keys.append((name, l))
            shards.append(weights[name][l] if name == "conv_w" else weights[name][l].astype(WIRE_DTYPE))
    first = [j for j, (name, l) in enumerate(keys) if l == 0 and name not in ("conv_w_in", "conv_w", "conv_w_out")]
    later = [j for j in range(len(keys)) if j not in first]
    full = dict(zip([keys[j] for j in first], _exchange([shards[j] for j in first], "gather_first", scatter=False)))
    later_shards = [shards[j] for j in later]
    zones = place_own(later_shards, full[keys[first[0]]], name="gather_rest_own")
    g_send, g_recv, g_srcs, g_zones, token = exchange_start(later_shards, zones, None, name="gather_rest_start")
    g_mix_0 = g_mix[0] + token[0, 0]

    def rows(name, l):
        g = full[(name, l)]
        return g.reshape(g.shape[0] * g.shape[1], g.shape[2])

    saved = []
    for i in range(depth):
        l = i // 2
        rec = {"x0": xs}
        if i == 1:
            arrived = exchange_wait(g_send, g_recv, g_srcs, g_zones, None, xs, name="gather_rest_wait")[1]
            full.update(zip([keys[j] for j in later], arrived))
        if i % 2 == 0:
            wd3 = _pad_last(rows("attn_w_down", l), DOWN_PAD)[None]
            wuq3 = _pad_last(full[("attn_w_uq", l)], QK_PAD)
            wukv3 = full[("attn_w_ukv", l)]
            gqn = _pad_last(attn_g_qnorm[l][None], QK_PAD)
            gkn = _pad_last(attn_g_knorm[l][None], QK_PAD)
            gqa, gkva = attn_g_q_a[l][None], attn_g_kv_a[l][None]
            h, a = norm_matmul(xs, g_mix_0 if i == 0 else g_mix[i], wd3, name="attn_down")
            q, k, v, cq, ckv = mla_pre_fwd(a, gqa, gkva, wuq3, wukv3, gqn, gkn, cos_t, sin_t)
            o, lse = attn_fwd(q, k, v, pos_col, pos_row)
            x1 = matmul_residual(o, rows("attn_w_o", l), xs, name="attn_out")
            rec.update(h=h, a=a, q=q, k=k, v=v, cq=cq, ckv=ckv, o=o, lse=lse, wd3=wd3, wuq3=wuq3, wukv3=wukv3,
                       gqn=gqn, gkn=gkn, gqa=gqa, gkva=gkva)
        else:
            cw = full[("conv_w", l)].transpose(1, 0, 2).reshape(3, D_MODEL)
            h, bcu = norm_matmul(xs, g_mix[i], full[("conv_w_in", l)], name="conv_in")
            z = conv_fwd(bcu, cw)
            x1 = matmul_residual(z, rows("conv_w_out", l), xs, name="conv_out")
            rec.update(h=h, bcu=bcu, z=z, cw=cw)
        h2, act = norm_matmul(x1, g_mlp[i], full[("mlp_w1", i)], name="mlp_up", mlp=True)
        xs = matmul_residual(act, rows("mlp_w2", i), x1, name="mlp_down")
        rec.update(x1=x1, h2=h2, act=act)
        saved.append(rec)

    sq, dx = loss_head(xs, target)
    loss = lax.psum(sq[0, 0] * (0.5 / D_MODEL), MESH_AXES)

    grads = {name: [None] * weights[name].shape[0] for name in order}
    token = None
    for i in reversed(range(depth)):
        l = i // 2
        rec = saved[i]
        if i == 0:
            flying = [keys[j] for j in later]
            srcs = [grads[name][l_] for name, l_ in flying]
            slots = [(big.index(name), l_) for name, l_ in flying]
            zones = [lax.empty((N_DEV, weights[name].shape[0]) + grads[name][-1].shape[1:], grads[name][-1].dtype)
                     for name in big]
            s_send, s_recv, s_srcs, s_zones, token = exchange_start(srcs, zones, slots, name="scatter_rest_start")
        grads["mlp_w2"][i] = mm_tn(rec["act"], dx, name="mlp_down_dw", G=1, out_dtype=WIRE_DTYPE,
                                   after=token).reshape(N_DEV, -1, D_MODEL)
        du = matmul_nt(dx, rows("mlp_w2", i)[None], name="mlp_down_dx", epi="mlp_du", u=rec["act"])
        grads["mlp_w1"][i] = mm_tn(rec["h2"], du, name="mlp_up_dw", G=N_DEV, out_dtype=WIRE_DTYPE)
        dx1, dg = matmul_nt(du, full[("mlp_w1", i)], name="mlp_up_dx", epi="rms_bwd", x=rec["x1"], g=g_mlp[i], dx=dx)
        grads["g_mlp"][i] = dg[0]
        if i % 2 == 0:
            grads["attn_w_o"][l] = mm_tn(rec["o"], dx1, name="attn_out_dw", G=1, out_dtype=WIRE_DTYPE).reshape(N_DEV, -1, D_MODEL)
            do = matmul_nt(dx1, rows("attn_w_o", l)[None], name="attn_out_dx")
            dq, dk, dv = attn_bwd(rec["q"], rec["k"], rec["v"], do, rec["lse"], attn_delta(rec["o"], do), pos_col, pos_row)
            da, dwuq, dwukv, dgqn, dgkn, dgqa, dgkva = mla_pre_bwd(
                dq, dk, dv, rec["a"], rec["cq"], rec["ckv"], rec["gqa"], rec["gkva"], rec["wuq3"], rec["wukv3"],
                rec["gqn"], rec["gkn"], cos_t, sin_t)
            grads["attn_w_uq"][l] = dwuq[:, :, :QK_DIM].astype(WIRE_DTYPE)
            grads["attn_w_ukv"][l] = dwukv.astype(WIRE_DTYPE)
            grads["attn_g_qnorm"][l] = dgqn[0, :QK_DIM]
            grads["attn_g_knorm"][l] = dgkn[0, :QK_DIM]
            grads["attn_g_q_a"][l] = dgqa[0]
            grads["attn_g_kv_a"][l] = dgkva[0]
            dwd = mm_tn(rec["h"], da, name="attn_down_dw", G=1, out_dtype=WIRE_DTYPE)
            grads["attn_w_down"][l] = dwd[0, :, :DOWN].reshape(N_DEV, -1, DOWN)
            dx, dg = matmul_nt(da, rec["wd3"], name="attn_down_dx", epi="rms_bwd", x=rec["x0"], g=g_mix[i], dx=dx1)
        else:
            grads["conv_w_out"][l] = mm_tn(rec["z"], dx1, name="conv_out_dw", G=1, out_dtype=WIRE_DTYPE).reshape(N_DEV, -1, D_MODEL)
            dz = matmul_nt(dx1, rows("conv_w_out", l)[None], name="conv_out_dx")
            dgb, dgc, du_, dcw = conv_bwd(dz, rec["bcu"], rec["cw"])
            grads["conv_w"][l] = dcw.reshape(3, N_DEV, -1).transpose(1, 0, 2)
            dbcu = jnp.concatenate([dgb, dgc, du_], axis=1)
            grads["conv_w_in"][l] = mm_tn(rec["h"], dbcu, name="conv_in_dw", G=N_DEV, out_dtype=WIRE_DTYPE)
            dx, dg = matmul_nt(dbcu, full[("conv_w_in", l)], name="conv_in_dx", epi="rms_bwd", x=rec["x0"], g=g_mix[i], dx=dx1)
        grads["g_mix"][i] = dg[0]

    sizes = [weights[name].size for name in small]
    n_small = sum(sizes)
    rows_small = -(-n_small // (8 * 128)) * 8

    def pack(tree):
        flat = jnp.concatenate([jnp.stack(tree[name]).reshape(-1) if isinstance(tree[name], list) else tree[name].reshape(-1)
                                for name in small])
        return jnp.pad(flat, (0, rows_small * 128 - n_small)).reshape(rows_small, 128)

    s_srcs, s_zones = exchange_wait(s_send, s_recv, s_srcs, s_zones, slots, dx, name="scatter_rest_wait")
    remote = [(grads[name][l_], big.index(name), l_) for name, l_ in (keys[j] for j in first)]
    local = [(src, k, l_) for src, (k, l_) in zip(s_srcs, slots)]
    parts, gain_parts = scatter_finish(remote, local, s_zones, pack(grads), name="scatter_last")

    out = {}
    for name, part in zip(big, parts):
        w = weights[name]
        flat = lambda t: t.reshape(-1, t.shape[-1])
        res = adamw(part.reshape(N_DEV, -1, w.shape[-1]), flat(w), flat(mom1[name]), flat(mom2[name]), name="adamw_" + name)
        out[name] = [r.reshape(w.shape) for r in res]
    res = adamw(gain_parts, pack(weights), pack(mom1), pack(mom2), name="adamw_gains")
    offset = 0
    for name, size in zip(small, sizes):
        out[name] = [r.reshape(-1)[offset:offset + size].reshape(weights[name].shape) for r in res]
        offset += size

    return (loss, dx[None], *[out[n][0] for n in order], *[out[n][1] for n in order],
            *[out[n][2] for n in order], *[out[n][3] for n in order])
```

```python
import jax
import jax.numpy as jnp
import numpy as np
from jax import lax
from jax.experimental import pallas as pl
from jax.experimental.pallas import tpu as pltpu

F32 = jnp.float32
MXU_DTYPE = jnp.bfloat16
WIRE_DTYPE = jnp.bfloat16

D_MODEL = 1024
N_HEADS = 8
NOPE = 128
ROPE = 64
QK_DIM = NOPE + ROPE
QK_PAD = 256
V_DIM = 128
Q_LORA = 256
KV_LORA = 128
DOWN = Q_LORA + KV_LORA + ROPE
DOWN_PAD = 512
ROPE_THETA = 10000.0
EPS = 1e-6
SM_SCALE = QK_DIM ** -0.5
LOG2E = 1.4426950408889634
Q_PRESCALE = SM_SCALE * LOG2E
ADAM_LR, ADAM_B1, ADAM_B2, ADAM_EPS, ADAM_WD, ADAM_STEP = 0.001, 0.9, 0.999, 1e-08, 0.01, 10
N_DEV = 8
MESH_AXES = ("x", "y", "c")

TM = 512
TM_WIDE = 1024
TM_TOKENS_TN = 2048
TQ = 512
HEADS_FWD = 8
TQ_BWD = 1024
BWD_CHUNK = 256
TROW = 512
TCH = 512
VMEM_LIMIT = 48 << 20

NN = (((1,), (0,)), ((), ()))
NT = (((1,), (1,)), ((), ()))
TN = (((0,), (0,)), ((), ()))


def _dot(a, b, dims=NN):
    return lax.dot_general(a.astype(MXU_DTYPE), b.astype(MXU_DTYPE), dims, preferred_element_type=F32)


def _params(n_axes):
    return pltpu.CompilerParams(dimension_semantics=("arbitrary",) * n_axes, vmem_limit_bytes=VMEM_LIMIT)


def _rms(xv, n):
    r = lax.rsqrt(jnp.sum(xv * xv, axis=-1, keepdims=True) / n + EPS)
    return xv * r, r


def _rms_bwd(dy, xhat, r, g, n):
    dg = jnp.sum(dy * xhat, axis=0, keepdims=True)
    dxh = dy * g
    dx = r * (dxh - xhat * (jnp.sum(dxh * xhat, axis=-1, keepdims=True) / n))
    return dx, dg


def _swap_halves(t):
    lane = lax.broadcasted_iota(jnp.int32, t.shape, 1)
    return jnp.where(lane < ROPE // 2, pltpu.roll(t, 128 - ROPE // 2, 1), pltpu.roll(t, ROPE // 2, 1))


def _rope(t, cos_t, sin_t):
    return t * cos_t + _swap_halves(t) * sin_t


def _rope_bwd(dout, cos_t, sin_t):
    return dout * cos_t + _swap_halves(dout * sin_t)


def rms_mm(x, g, w3, *, name, mlp=False):
    S, D = x.shape
    G, _, Nb = w3.shape
    tm = min(TM_WIDE, S)
    tn = Nb if Nb <= 512 else 512
    nj = Nb // tn
    N = G * Nb

    def body(x_ref, g_ref, w_ref, h_ref, *rest):
        hs = rest[-1]

        @pl.when(pl.program_id(1) == 0)
        def _():
            xv = x_ref[...]
            r = lax.rsqrt(jnp.mean(xv * xv, axis=-1, keepdims=True) + EPS)
            h = (xv * r * g_ref[...]).astype(hs.dtype)
            hs[...] = h
            h_ref[...] = h

        acc = lax.dot_general(hs[...], w_ref[...].astype(hs.dtype), NN, preferred_element_type=F32)
        if mlp:
            rl = jnp.maximum(acc, 0.0)
            rest[0][...] = (rl * rl).astype(rest[0].dtype)
        else:
            rest[0][...] = acc

    out_shape = [jax.ShapeDtypeStruct((S, D), MXU_DTYPE), jax.ShapeDtypeStruct((S, N), MXU_DTYPE if mlp else F32)]
    out_specs = [pl.BlockSpec((tm, D), lambda i, j: (i, 0)), pl.BlockSpec((tm, tn), lambda i, j: (i, j))]
    return pl.pallas_call(
        body, name=name, grid=(S // tm, G * nj),
        in_specs=[pl.BlockSpec((tm, D), lambda i, j: (i, 0)),
                  pl.BlockSpec((1, D), lambda i, j: (0, 0)),
                  pl.BlockSpec((None, D, tn), lambda i, j: (j // nj, 0, j % nj))],
        out_specs=out_specs, out_shape=out_shape,
        scratch_shapes=[pltpu.VMEM((tm, D), MXU_DTYPE)],
        compiler_params=_params(2),
    )(x, g.reshape(1, D), w3)


def mm_res(a, w, res, *, name):
    S, K = a.shape
    _, N = w.shape
    tm = min(TM_WIDE, S)
    tk = min(K, 1024)
    nk = K // tk

    def body(a_ref, w_ref, r_ref, o_ref, acc):
        k = pl.program_id(1)

        @pl.when(k == 0)
        def _():
            acc[...] = jnp.zeros_like(acc)

        acc[...] += _dot(a_ref[...], w_ref[...])

        @pl.when(k == nk - 1)
        def _():
            o_ref[...] = r_ref[...] + acc[...]

    return pl.pallas_call(
        body, name=name, grid=(S // tm, nk),
        in_specs=[pl.BlockSpec((tm, tk), lambda i, k: (i, k)),
                  pl.BlockSpec((tk, N), lambda i, k: (k, 0)),
                  pl.BlockSpec((tm, N), lambda i, k: (i, 0))],
        out_specs=pl.BlockSpec((tm, N), lambda i, k: (i, 0)),
        out_shape=jax.ShapeDtypeStruct((S, N), F32),
        scratch_shapes=[pltpu.VMEM((tm, N), F32)],
        compiler_params=_params(2),
    )(a, w, res)


def mm_nt(a, w3, *, name, epi="plain", u=None, x=None, g=None, dx=None):
    S, N = a.shape
    G, Ko, Nb = w3.shape
    assert N == G * Nb
    tm = min(TM if epi == "rms_bwd" else TM_WIDE, S)
    tkk = Nb if Nb <= 1024 else 1024
    nk = Nb // tkk
    ks = G * nk
    tko = Ko if epi == "rms_bwd" else min(Ko, 512)

    def body(a_ref, w_ref, *rest):
        acc = rest[-1]
        i, k = pl.program_id(0), pl.program_id(2)

        @pl.when(k == 0)
        def _():
            acc[...] = jnp.zeros_like(acc)

        acc[...] += _dot(a_ref[...], w_ref[...], NT)

        @pl.when(k == ks - 1)
        def _():
            if epi == "plain":
                rest[0][...] = acc[...]
            elif epi == "mlp_du":
                u_ref, o_ref = rest[0], rest[1]
                o_ref[...] = (acc[...] * (2.0 * jnp.sqrt(u_ref[...].astype(F32)))).astype(o_ref.dtype)
            else:
                x_ref, g_ref, dx_ref, o_ref, dg_ref = rest[:5]
                xhat, r = _rms(x_ref[...], Ko)
                dxb, dg = _rms_bwd(acc[...], xhat, r, g_ref[...], Ko)
                o_ref[...] = dx_ref[...] + dxb

                @pl.when(i == 0)
                def _():
                    dg_ref[...] = dg

                @pl.when(i != 0)
                def _():
                    dg_ref[...] += dg

    in_specs = [pl.BlockSpec((tm, tkk), lambda i, j, k: (i, k)),
                pl.BlockSpec((None, tko, tkk), lambda i, j, k: (k // nk, j, k % nk))]
    args = [a, w3]
    tile = pl.BlockSpec((tm, tko), lambda i, j, k: (i, j))
    if epi == "plain":
        out_shape, out_specs = jax.ShapeDtypeStruct((S, Ko), F32), tile
    elif epi == "mlp_du":
        in_specs.append(tile)
        args.append(u)
        out_shape, out_specs = jax.ShapeDtypeStruct((S, Ko), MXU_DTYPE), tile
    else:
        vec = pl.BlockSpec((1, Ko), lambda i, j, k: (0, 0))
        in_specs += [tile, vec, tile]
        args += [x, g.reshape(1, Ko), dx]
        out_shape = [jax.ShapeDtypeStruct((S, Ko), F32), jax.ShapeDtypeStruct((1, Ko), F32)]
        out_specs = [tile, vec]
    return pl.pallas_call(
        body, name=name, grid=(S // tm, Ko // tko, ks),
        in_specs=in_specs, out_specs=out_specs, out_shape=out_shape,
        scratch_shapes=[pltpu.VMEM((tm, tko), F32)],
        compiler_params=_params(3),
    )(*args)


def _resident(shape):
    return pl.BlockSpec(shape, lambda i: (0,) * len(shape))


def norm_matmul(x, g, w3, *, name, mlp=False):
    S, D = x.shape
    G, _, Nb = w3.shape
    tm = min(TM, S)
    N = G * Nb

    def body(x_ref, g_ref, w_ref, h_ref, o_ref):
        xv = x_ref[...]
        r = lax.rsqrt(jnp.mean(xv * xv, axis=-1, keepdims=True) + EPS)
        h = (xv * r * g_ref[...]).astype(h_ref.dtype)
        h_ref[...] = h
        for gi in range(G):
            acc = _dot(h, w_ref[gi])
            if mlp:
                acc = jnp.square(jnp.maximum(acc, 0.0))
            o_ref[:, gi * Nb:(gi + 1) * Nb] = acc.astype(o_ref.dtype)

    rows = lambda w: pl.BlockSpec((tm, w), lambda i: (i, 0))
    return pl.pallas_call(
        body, name=name, grid=(S // tm,),
        in_specs=[rows(D), _resident((1, D)), _resident((G, D, Nb))],
        out_specs=[rows(D), rows(N)],
        out_shape=[jax.ShapeDtypeStruct((S, D), MXU_DTYPE), jax.ShapeDtypeStruct((S, N), MXU_DTYPE if mlp else F32)],
        compiler_params=_params(1),
    )(x, g.reshape(1, D), w3)


def matmul_residual(a, w, res, *, name):
    S, K = a.shape
    _, N = w.shape
    tm = min(TM, S)

    def body(a_ref, w_ref, r_ref, o_ref):
        o_ref[...] = r_ref[...] + _dot(a_ref[...], w_ref[...])

    rows = lambda w_: pl.BlockSpec((tm, w_), lambda i: (i, 0))
    return pl.pallas_call(
        body, name=name, grid=(S // tm,),
        in_specs=[rows(K), _resident((K, N)), rows(N)],
        out_specs=rows(N), out_shape=jax.ShapeDtypeStruct((S, N), F32),
        compiler_params=_params(1),
    )(a, w, res)


def matmul_nt(a, w3, *, name, epi="plain", u=None, x=None, g=None, dx=None):
    S, N = a.shape
    G, Ko, Nb = w3.shape
    assert N == G * Nb
    tm = min(TM, S)
    tko = min(Ko, 512)

    def body(a_ref, w_ref, *rest):
        if epi == "mlp_du":
            u_ref, o_ref = rest
            av = a_ref[...].astype(MXU_DTYPE)
            for j in range(Ko // tko):
                cols = slice(j * tko, (j + 1) * tko)
                da = _dot(av, w_ref[0, cols, :], NT)
                o_ref[:, cols] = (da * (2.0 * jnp.sqrt(u_ref[:, cols].astype(F32)))).astype(o_ref.dtype)
            return
        acc = _dot(a_ref[:, :Nb], w_ref[0], NT)
        for gi in range(1, G):
            acc = acc + _dot(a_ref[:, gi * Nb:(gi + 1) * Nb], w_ref[gi], NT)
        if epi == "plain":
            rest[0][...] = acc
        else:
            x_ref, g_ref, dx_ref, o_ref, dg_ref = rest
            xhat, r = _rms(x_ref[...], Ko)
            dxb, dg = _rms_bwd(acc, xhat, r, g_ref[...], Ko)
            o_ref[...] = dx_ref[...] + dxb

            @pl.when(pl.program_id(0) == 0)
            def _():
                dg_ref[...] = dg

            @pl.when(pl.program_id(0) != 0)
            def _():
                dg_ref[...] += dg

    rows = lambda w_: pl.BlockSpec((tm, w_), lambda i: (i, 0))
    in_specs = [rows(N), _resident((G, Ko, Nb))]
    args = [a, w3]
    if epi == "plain":
        out_shape, out_specs = jax.ShapeDtypeStruct((S, Ko), F32), rows(Ko)
    elif epi == "mlp_du":
        in_specs.append(rows(Ko))
        args.append(u)
        out_shape, out_specs = jax.ShapeDtypeStruct((S, Ko), MXU_DTYPE), rows(Ko)
    else:
        in_specs += [rows(Ko), _resident((1, Ko)), rows(Ko)]
        args += [x, g.reshape(1, Ko), dx]
        out_shape = [jax.ShapeDtypeStruct((S, Ko), F32), jax.ShapeDtypeStruct((1, Ko), F32)]
        out_specs = [rows(Ko), _resident((1, Ko))]
    return pl.pallas_call(
        body, name=name, grid=(S // tm,),
        in_specs=in_specs, out_specs=out_specs, out_shape=out_shape,
        compiler_params=_params(1),
    )(*args)


def mm_tn(a, b, *, name, G, out_dtype, after=None):
    order = [] if after is None else [after]
    S, Ka = a.shape
    _, N = b.shape
    Nb = N // G
    tm = min(TM_TOKENS_TN if b.dtype.itemsize == 2 else TM_TOKENS_TN // 2, S)
    tka = min(Ka, 1024)
    tnb = Nb if Nb <= 1024 else 1024
    nj = Nb // tnb
    ns = S // tm

    def body(a_ref, b_ref, *rest):
        o_ref, acc = rest[-2:]
        s = pl.program_id(2)

        @pl.when(s == 0)
        def _():
            acc[...] = jnp.zeros_like(acc)

        acc[...] += _dot(a_ref[...], b_ref[...], TN)

        @pl.when(s == ns - 1)
        def _():
            o_ref[...] = acc[...].astype(o_ref.dtype)

    return pl.pallas_call(
        body, name=name, grid=(Ka // tka, G * nj, ns),
        in_specs=[pl.BlockSpec((tm, tka), lambda i, j, s: (s, i)),
                  pl.BlockSpec((tm, tnb), lambda i, j, s: (s, j))] + [pl.BlockSpec(memory_space=pl.ANY)] * len(order),
        out_specs=pl.BlockSpec((None, tka, tnb), lambda i, j, s: (j // nj, i, j % nj)),
        out_shape=jax.ShapeDtypeStruct((G, Ka, Nb), out_dtype),
        scratch_shapes=[pltpu.VMEM((tka, tnb), F32)],
        compiler_params=_params(3),
    )(a, b, *order)


def mla_pre_fwd(a, gqa, gkva, wuq3, wukv3, gqn, gkn, cos_t, sin_t):
    S = a.shape[0]
    tm = min(TM, S)

    def body(a_ref, gqa_ref, gkva_ref, wuq_ref, wukv_ref, gqn_ref, gkn_ref, cos_ref, sin_ref,
             q_ref, k_ref, v_ref, cq_ref, ckv_ref, cqs, ckvs):
        @pl.when(pl.program_id(1) == 0)
        def _():
            av = a_ref[...]
            cq = (_rms(av[:, :Q_LORA], Q_LORA)[0] * gqa_ref[...]).astype(cqs.dtype)
            ckv = (_rms(av[:, Q_LORA:Q_LORA + KV_LORA], KV_LORA)[0] * gkva_ref[...]).astype(ckvs.dtype)
            cqs[...] = cq
            cq_ref[...] = cq
            ckvs[...] = ckv
            ckv_ref[...] = ckv

        cos_v, sin_v = cos_ref[...], sin_ref[...]
        qn = _rms(_dot(cqs[...], wuq_ref[...]), QK_DIM)[0] * gqn_ref[...]
        qr = jnp.concatenate([qn[:, :NOPE], _rope(qn[:, NOPE:], cos_v, sin_v)], axis=1)
        q_ref[...] = (qr * Q_PRESCALE).astype(q_ref.dtype)
        kvp = _dot(ckvs[...], wukv_ref[...])
        kk = jnp.concatenate([kvp[:, :NOPE], a_ref[:, Q_LORA + KV_LORA:]], axis=1)
        kn = _rms(kk, QK_DIM)[0] * gkn_ref[...]
        k_ref[...] = jnp.concatenate([kn[:, :NOPE], _rope(kn[:, NOPE:], cos_v, sin_v)], axis=1).astype(k_ref.dtype)
        v_ref[...] = kvp[:, NOPE:].astype(v_ref.dtype)

    row = lambda w: pl.BlockSpec((tm, w), lambda i, h: (i, 0))
    vec = lambda w: pl.BlockSpec((1, w), lambda i, h: (0, 0))
    head = lambda w: pl.BlockSpec((None, tm, w), lambda i, h: (h, i, 0))
    return pl.pallas_call(
        body, name="mla_pre_fwd", grid=(S // tm, N_HEADS),
        in_specs=[row(DOWN_PAD), vec(Q_LORA), vec(KV_LORA),
                  pl.BlockSpec((None, Q_LORA, QK_PAD), lambda i, h: (h, 0, 0)),
                  pl.BlockSpec((None, KV_LORA, NOPE + V_DIM), lambda i, h: (h, 0, 0)),
                  vec(QK_PAD), vec(QK_PAD), row(128), row(128)],
        out_specs=[head(QK_PAD), head(QK_PAD), head(V_DIM), row(Q_LORA), row(KV_LORA)],
        out_shape=[jax.ShapeDtypeStruct((N_HEADS, S, QK_PAD), MXU_DTYPE),
                   jax.ShapeDtypeStruct((N_HEADS, S, QK_PAD), MXU_DTYPE),
                   jax.ShapeDtypeStruct((N_HEADS, S, V_DIM), MXU_DTYPE),
                   jax.ShapeDtypeStruct((S, Q_LORA), MXU_DTYPE),
                   jax.ShapeDtypeStruct((S, KV_LORA), MXU_DTYPE)],
        scratch_shapes=[pltpu.VMEM((tm, Q_LORA), MXU_DTYPE), pltpu.VMEM((tm, KV_LORA), MXU_DTYPE)],
        compiler_params=_params(2),
    )(a, gqa, gkva, wuq3, wukv3, gqn, gkn, cos_t, sin_t)


def mla_pre_bwd(dq, dk, dv, a, cq, ckv, gqa, gkva, wuq3, wukv3, gqn, gkn, cos_t, sin_t):
    S = a.shape[0]
    tm = min(TM, S)
    H = N_HEADS

    def body(dq_ref, dk_ref, dv_ref, a_ref, cq_ref, ckv_ref, gqa_ref, gkva_ref, wuq_ref, wukv_ref, gqn_ref, gkn_ref,
             cos_ref, sin_ref, da_ref, dwuq_ref, dwukv_ref, dgqn_ref, dgkn_ref, dgqa_ref, dgkva_ref,
             dcq, dckv, dkpe):
        i, h = pl.program_id(0), pl.program_id(1)

        @pl.when((i == 0) & (h == 0))
        def _():
            for ref in (dwuq_ref, dwukv_ref, dgqn_ref, dgkn_ref, dgqa_ref, dgkva_ref):
                ref[...] = jnp.zeros_like(ref)

        @pl.when(h == 0)
        def _():
            dcq[...] = jnp.zeros_like(dcq)
            dckv[...] = jnp.zeros_like(dckv)
            dkpe[...] = jnp.zeros_like(dkpe)

        cos_v, sin_v = cos_ref[...], sin_ref[...]
        cqv, ckvv = cq_ref[...], ckv_ref[...]
        wuq, wukv = wuq_ref[...], wukv_ref[...]

        qhat, rq = _rms(_dot(cqv, wuq), QK_DIM)
        dqr = dq_ref[...] * SM_SCALE
        dqn = jnp.concatenate([dqr[:, :NOPE], _rope_bwd(dqr[:, NOPE:], cos_v, sin_v)], axis=1)
        dqp, dg = _rms_bwd(dqn, qhat, rq, gqn_ref[...], QK_DIM)
        dgqn_ref[...] += dg
        dqp = dqp.astype(MXU_DTYPE)
        dwuq_ref[h] += _dot(cqv, dqp, TN)
        dcq[...] += _dot(dqp, wuq, NT)

        kvp = _dot(ckvv, wukv)
        kk = jnp.concatenate([kvp[:, :NOPE], a_ref[:, Q_LORA + KV_LORA:]], axis=1)
        khat, rk = _rms(kk, QK_DIM)
        dkr = dk_ref[...] * (1.0 / LOG2E)
        dkn = jnp.concatenate([dkr[:, :NOPE], _rope_bwd(dkr[:, NOPE:], cos_v, sin_v)], axis=1)
        dkk, dg = _rms_bwd(dkn, khat, rk, gkn_ref[...], QK_DIM)
        dgkn_ref[...] += dg
        dkpe[...] += dkk[:, NOPE:]
        dkvp = jnp.concatenate([dkk[:, :NOPE], dv_ref[...]], axis=1).astype(MXU_DTYPE)
        dwukv_ref[h] += _dot(ckvv, dkvp, TN)
        dckv[...] += _dot(dkvp, wukv, NT)

        @pl.when(h == H - 1)
        def _():
            av = a_ref[...]
            ahat, r = _rms(av[:, :Q_LORA], Q_LORA)
            daq, dg = _rms_bwd(dcq[...], ahat, r, gqa_ref[...], Q_LORA)
            dgqa_ref[...] += dg
            ahat, r = _rms(av[:, Q_LORA:Q_LORA + KV_LORA], KV_LORA)
            dakv, dg = _rms_bwd(dckv[...], ahat, r, gkva_ref[...], KV_LORA)
            dgkva_ref[...] += dg
            da_ref[...] = jnp.concatenate([daq, dakv, dkpe[...]], axis=1)

    row = lambda w: pl.BlockSpec((tm, w), lambda i, h: (i, 0))
    vec = lambda w: pl.BlockSpec((1, w), lambda i, h: (0, 0))
    head = lambda w: pl.BlockSpec((None, tm, w), lambda i, h: (h, i, 0))
    full3 = lambda s: pl.BlockSpec(s, lambda i, h: (0, 0, 0))
    return pl.pallas_call(
        body, name="mla_pre_bwd", grid=(S // tm, H),
        in_specs=[head(QK_PAD), head(QK_PAD), head(V_DIM), row(DOWN_PAD), row(Q_LORA), row(KV_LORA),
                  vec(Q_LORA), vec(KV_LORA),
                  pl.BlockSpec((None, Q_LORA, QK_PAD), lambda i, h: (h, 0, 0)),
                  pl.BlockSpec((None, KV_LORA, NOPE + V_DIM), lambda i, h: (h, 0, 0)),
                  vec(QK_PAD), vec(QK_PAD), row(128), row(128)],
        out_specs=[row(DOWN_PAD), full3((H, Q_LORA, QK_PAD)), full3((H, KV_LORA, NOPE + V_DIM)),
                   vec(QK_PAD), vec(QK_PAD), vec(Q_LORA), vec(KV_LORA)],
        out_shape=[jax.ShapeDtypeStruct((S, DOWN_PAD), F32),
                   jax.ShapeDtypeStruct((H, Q_LORA, QK_PAD), F32),
                   jax.ShapeDtypeStruct((H, KV_LORA, NOPE + V_DIM), F32),
                   jax.ShapeDtypeStruct((1, QK_PAD), F32), jax.ShapeDtypeStruct((1, QK_PAD), F32),
                   jax.ShapeDtypeStruct((1, Q_LORA), F32), jax.ShapeDtypeStruct((1, KV_LORA), F32)],
        scratch_shapes=[pltpu.VMEM((tm, Q_LORA), F32), pltpu.VMEM((tm, KV_LORA), F32), pltpu.VMEM((tm, 128), F32)],
        compiler_params=_params(2),
    )(dq, dk, dv, a, cq, ckv, gqa, gkva, wuq3, wukv3, gqn, gkn, cos_t, sin_t)


def _pair_tables(nb, key_major):
    if key_major:
        pairs = [(qi, kj) for kj in range(nb) for qi in range(kj, nb)]
    else:
        pairs = [(qi, ki) for qi in range(nb) for ki in range(qi + 1)]
    return (jnp.asarray(np.array([p[0] for p in pairs], np.int32)),
            jnp.asarray(np.array([p[1] for p in pairs], np.int32)))


def _scores_t(k, q, pk_col, pq_row, masked):
    s = _dot(k, q, NT)
    return jnp.where(pq_row >= pk_col, s, jnp.finfo(F32).min) if masked else s


def attn_fwd(q, k, v, pos_col, pos_row):
    H, S, _ = q.shape
    t = min(TQ, S)
    nb = S // t
    hb = HEADS_FWD
    qt, kt = _pair_tables(nb, key_major=False)

    def body(qt_ref, kt_ref, q_ref, k_ref, v_ref, pk_ref, pq_ref, o_ref, lse_ref, m_s, l_s, acc):
        step = pl.program_id(1)
        qi, ki = qt_ref[step], kt_ref[step]

        @pl.when(ki == 0)
        def _():
            m_s[...] = jnp.full_like(m_s, -jnp.inf)
            l_s[...] = jnp.zeros_like(l_s)
            acc[...] = jnp.zeros_like(acc)

        def update(masked):
            scores = lambda hh: _scores_t(k_ref[hh], q_ref[hh], pk_ref[...], pq_ref[...], masked)
            s_next = scores(0)
            for hh in range(hb):
                s = s_next
                if hh + 1 < hb:
                    s_next = scores(hh + 1)
                m_old = m_s[hh]
                m_new = jnp.maximum(m_old, jnp.max(s, axis=0, keepdims=True))
                p = jnp.exp2(s - m_new)
                alpha = jnp.exp2(m_old - m_new)
                l_s[hh] = alpha * l_s[hh] + jnp.sum(p, axis=0, keepdims=True)
                acc[hh] = alpha * acc[hh] + _dot(v_ref[hh], p, TN)
                m_s[hh] = m_new

        @pl.when(ki < qi)
        def _():
            update(False)

        @pl.when(ki == qi)
        def _():
            update(True)
            for hh in range(hb):
                o_ref[:, hh * V_DIM:(hh + 1) * V_DIM] = (acc[hh] / l_s[hh]).T
                lse_ref[hh] = m_s[hh] + jnp.log(l_s[hh]) * LOG2E

    grid_spec = pltpu.PrefetchScalarGridSpec(
        num_scalar_prefetch=2, grid=(H // hb, qt.shape[0]),
        in_specs=[pl.BlockSpec((hb, t, QK_PAD), lambda h, s, qt, kt: (h, qt[s], 0)),
                  pl.BlockSpec((hb, t, QK_PAD), lambda h, s, qt, kt: (h, kt[s], 0)),
                  pl.BlockSpec((hb, t, V_DIM), lambda h, s, qt, kt: (h, kt[s], 0)),
                  pl.BlockSpec((t, 1), lambda h, s, qt, kt: (kt[s], 0)),
                  pl.BlockSpec((1, t), lambda h, s, qt, kt: (0, qt[s]))],
        out_specs=[pl.BlockSpec((t, hb * V_DIM), lambda h, s, qt, kt: (qt[s], h)),
                   pl.BlockSpec((hb, 1, t), lambda h, s, qt, kt: (h, 0, qt[s]))],
        scratch_shapes=[pltpu.VMEM((hb, 1, t), F32), pltpu.VMEM((hb, 1, t), F32), pltpu.VMEM((hb, V_DIM, t), F32)])
    return pl.pallas_call(
        body, name="attn_fwd", grid_spec=grid_spec,
        out_shape=[jax.ShapeDtypeStruct((S, H * V_DIM), F32), jax.ShapeDtypeStruct((H, 1, S), F32)],
        compiler_params=_params(2),
    )(qt, kt, q, k, v, pos_col, pos_row)


def attn_delta(o, do):
    S = o.shape[0]
    t = min(TQ, S)

    def body(o_ref, do_ref, d_ref):
        for h in range(N_HEADS):
            cols = slice(h * V_DIM, (h + 1) * V_DIM)
            d_ref[h] = jnp.sum((o_ref[:, cols] * do_ref[:, cols]).T, axis=0, keepdims=True)

    blk = pl.BlockSpec((t, N_HEADS * V_DIM), lambda i: (i, 0))
    return pl.pallas_call(
        body, name="attn_delta", grid=(S // t,),
        in_specs=[blk, blk],
        out_specs=pl.BlockSpec((N_HEADS, 1, t), lambda i: (0, 0, i)),
        out_shape=jax.ShapeDtypeStruct((N_HEADS, 1, S), F32),
        compiler_params=_params(1),
    )(o, do)


def attn_bwd(q, k, v, do, lse, delta, pos_col, pos_row):
    H, S, _ = q.shape
    t = min(TQ_BWD, S)
    nb = S // t
    qt, kt = _pair_tables(nb, key_major=True)
    tc = min(BWD_CHUNK, t)

    def body(qt_ref, kt_ref, q_ref, k_ref, v_ref, do_ref, lse_ref, dl_ref, pk_ref, pq_ref, dq_ref, dk_ref, dv_ref):
        step = pl.program_id(1)
        qi, kj = qt_ref[step], kt_ref[step]

        @pl.when(step == 0)
        def _():
            dq_ref[...] = jnp.zeros_like(dq_ref)

        @pl.when(qi == kj)
        def _():
            dk_ref[...] = jnp.zeros_like(dk_ref)
            dv_ref[...] = jnp.zeros_like(dv_ref)

        def update(masked):
            kv_, vv = k_ref[...], v_ref[...]

            def first_matmuls(c):
                cols = slice(c * tc, (c + 1) * tc)
                qc = q_ref[cols, :]
                doc = do_ref[cols, :].astype(MXU_DTYPE)
                return qc, doc, _scores_t(kv_, qc, pk_ref[...], pq_ref[:, cols], masked), _dot(vv, doc, NT)

            nxt = first_matmuls(0)
            for c in range(t // tc):
                qc, doc, s, dp = nxt
                if c + 1 < t // tc:
                    nxt = first_matmuls(c + 1)
                cols = slice(c * tc, (c + 1) * tc)
                p = jnp.exp2(s - lse_ref[:, cols])
                ds = (p * (dp - dl_ref[:, cols])).astype(MXU_DTYPE)
                dv_ref[...] += _dot(p, doc)
                dk_ref[...] += _dot(ds, qc)
                rows = pl.ds(pl.multiple_of(qi * t + c * tc, tc), tc)
                dq_ref[rows, :] += _dot(ds, kv_, TN)

        @pl.when(qi == kj)
        def _():
            update(True)

        @pl.when(qi != kj)
        def _():
            update(False)

    q_idx = lambda h, s, qt, kt: (h, qt[s], 0)
    k_idx = lambda h, s, qt, kt: (h, kt[s], 0)
    row_idx = lambda h, s, qt, kt: (h, 0, qt[s])
    grid_spec = pltpu.PrefetchScalarGridSpec(
        num_scalar_prefetch=2, grid=(H, qt.shape[0]),
        in_specs=[pl.BlockSpec((None, t, QK_PAD), q_idx),
                  pl.BlockSpec((None, t, QK_PAD), k_idx),
                  pl.BlockSpec((None, t, V_DIM), k_idx),
                  pl.BlockSpec((t, V_DIM), lambda h, s, qt, kt: (qt[s], h)),
                  pl.BlockSpec((None, 1, t), row_idx),
                  pl.BlockSpec((None, 1, t), row_idx),
                  pl.BlockSpec((t, 1), lambda h, s, qt, kt: (kt[s], 0)),
                  pl.BlockSpec((1, t), lambda h, s, qt, kt: (0, qt[s]))],
        out_specs=[pl.BlockSpec((None, S, QK_PAD), lambda h, s, qt, kt: (h, 0, 0)),
                   pl.BlockSpec((None, t, QK_PAD), k_idx),
                   pl.BlockSpec((None, t, V_DIM), k_idx)])
    return pl.pallas_call(
        body, name="attn_bwd", grid_spec=grid_spec,
        out_shape=[jax.ShapeDtypeStruct((H, S, QK_PAD), F32), jax.ShapeDtypeStruct((H, S, QK_PAD), F32),
                   jax.ShapeDtypeStruct((H, S, V_DIM), F32)],
        compiler_params=_params(2),
    )(qt, kt, q, k, v, do, lse, delta, pos_col, pos_row)


def _conv_specs(S, tr, tc):
    nc = D_MODEL // tc
    hb = tr // 8
    main = lambda third: pl.BlockSpec((tr, tc), lambda c, r: (r, third * nc + c))
    prev = lambda third: pl.BlockSpec((8, tc), lambda c, r: (jnp.maximum(r * hb - 1, 0), third * nc + c))
    nxt = lambda third: pl.BlockSpec((8, tc), lambda c, r: (jnp.minimum((r + 1) * hb, S // 8 - 1), third * nc + c))
    return main, prev, nxt


def _conv_taps(gc, uu, w_ref, first):
    u2 = gc * uu
    rows = lax.broadcasted_iota(jnp.int32, u2.shape, 0)
    u2 = jnp.where((rows < 8) & first, 0.0, u2)
    s1 = pltpu.roll(u2, 1, 0)
    s2 = pltpu.roll(u2, 2, 0)
    u3 = w_ref[2:3, :] * u2 + w_ref[1:2, :] * s1 + w_ref[0:1, :] * s2
    return u2, s1, s2, u3


def conv_fwd(bcu, cw):
    S = bcu.shape[0]
    tr, tc = min(TROW, S), TCH
    main, prev, _ = _conv_specs(S, tr, tc)

    def body(gb_ref, gc_ref, u_ref, gch_ref, uh_ref, w_ref, z_ref):
        gc = jnp.concatenate([gch_ref[...], gc_ref[...]], axis=0)
        uu = jnp.concatenate([uh_ref[...], u_ref[...]], axis=0)
        u3 = _conv_taps(gc, uu, w_ref, pl.program_id(1) == 0)[3]
        z_ref[...] = (gb_ref[...] * u3[8:]).astype(z_ref.dtype)

    return pl.pallas_call(
        body, name="conv_fwd", grid=(D_MODEL // tc, S // tr),
        in_specs=[main(0), main(1), main(2), prev(1), prev(2), pl.BlockSpec((3, tc), lambda c, r: (0, c))],
        out_specs=pl.BlockSpec((tr, tc), lambda c, r: (r, c)),
        out_shape=jax.ShapeDtypeStruct((S, D_MODEL), MXU_DTYPE),
        compiler_params=_params(2),
    )(bcu, bcu, bcu, bcu, bcu, cw)


def conv_bwd(dz, bcu, cw):
    S = bcu.shape[0]
    tr, tc = min(TROW, S), TCH
    nr = S // tr
    main, prev, nxt = _conv_specs(S, tr, tc)

    def body(dz_ref, dzn_ref, gb_ref, gbn_ref, gc_ref, u_ref, gch_ref, uh_ref, w_ref,
             dgb_ref, dgc_ref, du_ref, dw_ref):
        r = pl.program_id(1)
        gcv, uv = gc_ref[...], u_ref[...]
        gc = jnp.concatenate([gch_ref[...], gcv], axis=0)
        uu = jnp.concatenate([uh_ref[...], uv], axis=0)
        u2, s1, s2, u3 = _conv_taps(gc, uu, w_ref, r == 0)
        dzv = dz_ref[...]
        du3 = jnp.concatenate([dzv * gb_ref[...], dzn_ref[...] * gbn_ref[...]], axis=0)
        rows = lax.broadcasted_iota(jnp.int32, du3.shape, 0)
        du3 = jnp.where((rows >= tr) & (r == nr - 1), 0.0, du3)
        n1 = pltpu.roll(du3, tr + 8 - 1, 0)
        n2 = pltpu.roll(du3, tr + 8 - 2, 0)
        du2 = (w_ref[2:3, :] * du3 + w_ref[1:2, :] * n1 + w_ref[0:1, :] * n2)[:tr]
        dgb_ref[...] = (dzv * u3[8:]).astype(dgb_ref.dtype)
        dgc_ref[...] = (du2 * uv).astype(dgc_ref.dtype)
        du_ref[...] = (du2 * gcv).astype(du_ref.dtype)
        d3 = du3[:tr]
        taps = [jnp.sum(d3 * t[8:], axis=0, keepdims=True) for t in (s2, s1, u2)]

        @pl.when(r == 0)
        def _():
            for kk in range(3):
                dw_ref[kk:kk + 1, :] = taps[kk]

        @pl.when(r != 0)
        def _():
            for kk in range(3):
                dw_ref[kk:kk + 1, :] += taps[kk]

    out = pl.BlockSpec((tr, tc), lambda c, r: (r, c))
    nxt_dz = pl.BlockSpec((8, tc), lambda c, r: (jnp.minimum((r + 1) * (tr // 8), S // 8 - 1), c))
    act = jax.ShapeDtypeStruct((S, D_MODEL), MXU_DTYPE)
    return pl.pallas_call(
        body, name="conv_bwd", grid=(D_MODEL // tc, nr),
        in_specs=[out, nxt_dz, main(0), nxt(0), main(1), main(2), prev(1), prev(2),
                  pl.BlockSpec((3, tc), lambda c, r: (0, c))],
        out_specs=[out, out, out, pl.BlockSpec((3, tc), lambda c, r: (0, c))],
        out_shape=[act, act, act, jax.ShapeDtypeStruct((3, D_MODEL), F32)],
        compiler_params=_params(2),
    )(dz, dz, bcu, bcu, bcu, bcu, bcu, bcu, cw)


def loss_head(y, target):
    S, D = y.shape
    tm = min(TM, S)

    def body(y_ref, t_ref, l_ref, dy_ref):
        err = y_ref[...] - t_ref[...]
        dy_ref[...] = err / D
        part = jnp.full((1, 128), jnp.sum(err * err), F32)

        @pl.when(pl.program_id(0) == 0)
        def _():
            l_ref[...] = part

        @pl.when(pl.program_id(0) != 0)
        def _():
            l_ref[...] += part

    blk = pl.BlockSpec((tm, D), lambda i: (i, 0))
    return pl.pallas_call(
        body, name="loss_head", grid=(S // tm,),
        in_specs=[blk, blk],
        out_specs=[pl.BlockSpec((1, 128), lambda i: (0, 0)), blk],
        out_shape=[jax.ShapeDtypeStruct((1, 128), F32), jax.ShapeDtypeStruct((S, D), F32)],
        compiler_params=_params(1),
    )(y, target)


def adamw(parts, w, m, v, *, name):
    R, C = w.shape
    tr = R
    while tr * C * 4 > (1 << 20) and tr % 32 == 0:
        tr //= 2

    def body(p_ref, w_ref, m_ref, v_ref, g_ref, d_ref, mo_ref, vo_ref):
        g = p_ref[0].astype(F32)
        for d in range(1, N_DEV):
            g = g + p_ref[d].astype(F32)
        m_new = ADAM_B1 * m_ref[...] + (1.0 - ADAM_B1) * g
        v_new = ADAM_B2 * v_ref[...] + (1.0 - ADAM_B2) * (g * g)
        m_hat = m_new / (1.0 - ADAM_B1 ** ADAM_STEP)
        v_hat = v_new / (1.0 - ADAM_B2 ** ADAM_STEP)
        g_ref[...] = g
        d_ref[...] = -ADAM_LR * (m_hat / (jnp.sqrt(v_hat) + ADAM_EPS) + ADAM_WD * w_ref[...])
        mo_ref[...] = m_new
        vo_ref[...] = v_new

    blk = pl.BlockSpec((tr, C), lambda i: (i, 0))
    return pl.pallas_call(
        body, name=name, grid=(R // tr,),
        in_specs=[pl.BlockSpec((N_DEV, tr, C), lambda i: (0, i, 0)), blk, blk, blk],
        out_specs=[blk, blk, blk, blk],
        out_shape=[jax.ShapeDtypeStruct((R, C), F32)] * 4,
        compiler_params=_params(1),
    )(parts, w, m, v)


def _mesh_place():
    x, y, c = (lax.axis_index(n) for n in MESH_AXES)
    return x, y, c, 4 * x + 2 * y + c


def _peer(x, y, c, d):
    px = 1 - x if d & 4 else x
    py = 1 - y if d & 2 else y
    pc = 1 - c if d & 1 else c
    return (px, py, pc), 4 * px + 2 * py + pc


def _exchange(srcs, name, scatter):
    n = len(srcs)
    any_spec = pl.BlockSpec(memory_space=pl.ANY)

    def body(*refs):
        ins, outs, token = refs[:n], refs[n:2 * n], refs[2 * n]
        send_sems, recv_sems, local_sems = refs[2 * n + 1:]
        token[...] = jnp.zeros_like(token)
        x, y, c, me = _mesh_place()
        for a in range(n):
            mine = ins[a].at[me] if scatter else ins[a]
            pltpu.make_async_copy(mine, outs[a].at[me], local_sems.at[a]).start()
            for d in range(1, N_DEV):
                peer, peer_lin = _peer(x, y, c, d)
                pltpu.make_async_remote_copy(
                    src_ref=ins[a].at[peer_lin] if scatter else ins[a], dst_ref=outs[a].at[me],
                    send_sem=send_sems.at[a], recv_sem=recv_sems.at[a],
                    device_id=peer, device_id_type=pl.DeviceIdType.MESH).start()
        for a in range(n):
            mine = ins[a].at[me] if scatter else ins[a]
            pltpu.make_async_copy(mine, outs[a].at[me], local_sems.at[a]).wait()
            seven = outs[a].at[pl.ds(0, N_DEV - 1)]
            drain = pltpu.make_async_remote_copy(
                src_ref=seven, dst_ref=seven, send_sem=send_sems.at[a], recv_sem=recv_sems.at[a],
                device_id=(x, y, c), device_id_type=pl.DeviceIdType.MESH)
            drain.wait_send()
            drain.wait_recv()

    block = (lambda s: s.shape[1:]) if scatter else (lambda s: s.shape)
    out = pl.pallas_call(
        body, name=name,
        in_specs=[any_spec] * n, out_specs=[any_spec] * n + [pl.BlockSpec(memory_space=pltpu.VMEM)],
        out_shape=[jax.ShapeDtypeStruct((N_DEV,) + tuple(block(s)), s.dtype) for s in srcs]
        + [jax.ShapeDtypeStruct((8, 128), F32)],
        scratch_shapes=[pltpu.SemaphoreType.DMA((n,)), pltpu.SemaphoreType.DMA((n,)), pltpu.SemaphoreType.DMA((n,))],
    )(*srcs)
    return out[:n], out[n]


_ANY = pl.BlockSpec(memory_space=pl.ANY)
_HBM = pl.BlockSpec(memory_space=pltpu.HBM)
_SEM = pl.BlockSpec(memory_space=pltpu.SEMAPHORE)


def _in_hbm(arrays):
    return [pltpu.with_memory_space_constraint(a, pltpu.HBM) for a in arrays]


def place_own(shards, after, *, name):
    n = len(shards)

    def body(*refs):
        ins, outs, sems = refs[:n], refs[n + 1:2 * n + 1], refs[2 * n + 1]
        me = _mesh_place()[3]
        copies = [pltpu.make_async_copy(ins[a], outs[a].at[me], sems.at[a]) for a in range(n)]
        for cp in copies:
            cp.start()
        for cp in copies:
            cp.wait()

    return pl.pallas_call(
        body, name=name, in_specs=[_ANY] * (n + 1), out_specs=[_ANY] * n,
        out_shape=[jax.ShapeDtypeStruct((N_DEV,) + s.shape, s.dtype) for s in shards],
        scratch_shapes=[pltpu.SemaphoreType.DMA((n,))],
    )(*shards, after)


def exchange_start(srcs, lands, slots, *, name):
    n, m = len(srcs), len(lands)

    def body(*refs):
        ins, zones = refs[:n], refs[n:n + m]
        send_sems, recv_sems, token = refs[n + m], refs[n + m + 1], refs[-1]
        x, y, c, me = _mesh_place()
        for a in range(n):
            for d in range(1, N_DEV):
                peer, peer_lin = _peer(x, y, c, d)
                src = ins[a] if slots is None else ins[a].at[peer_lin]
                dst = zones[a].at[me] if slots is None else zones[slots[a][0]].at[me, slots[a][1]]
                pltpu.make_async_remote_copy(
                    src_ref=src, dst_ref=dst, send_sem=send_sems.at[a], recv_sem=recv_sems.at[a],
                    device_id=peer, device_id_type=pl.DeviceIdType.MESH).start()
        token[...] = jnp.zeros_like(token)

    both = list(srcs) + list(lands)
    out = pl.pallas_call(
        body, name=name,
        in_specs=[_HBM] * (n + m),
        out_specs=[_SEM, _SEM] + [_HBM] * (n + m) + [pl.BlockSpec(memory_space=pltpu.VMEM)],
        out_shape=[pltpu.SemaphoreType.DMA((n,)), pltpu.SemaphoreType.DMA((n,))]
        + [pltpu.HBM(a.shape, a.dtype) for a in both] + [jax.ShapeDtypeStruct((8, 128), F32)],
        input_output_aliases={i: 2 + i for i in range(n + m)},
        compiler_params=pltpu.CompilerParams(has_side_effects=pltpu.SideEffectType.DATAFLOW_SIDE_EFFECTING),
    )(*_in_hbm(both))
    return out[0], out[1], out[2:2 + n], out[2 + n:2 + n + m], out[-1]


def exchange_wait(send_sems, recv_sems, srcs, lands, slots, after, *, name):
    n, m = len(srcs), len(lands)

    def body(*refs):
        ins, zones = refs[:n], refs[n:n + m]
        send_ref, recv_ref = refs[n + m], refs[n + m + 1]
        x, y, c, _ = _mesh_place()
        for a in range(n):
            seven = (zones[a] if slots is None else ins[a]).at[pl.ds(0, N_DEV - 1)]
            drain = pltpu.make_async_remote_copy(
                src_ref=seven, dst_ref=seven, send_sem=send_ref.at[a], recv_sem=recv_ref.at[a],
                device_id=(x, y, c), device_id_type=pl.DeviceIdType.MESH)
            drain.wait_send()
            drain.wait_recv()

    both = list(srcs) + list(lands)
    out = pl.pallas_call(
        body, name=name,
        in_specs=[_HBM] * (n + m) + [_SEM, _SEM, _ANY],
        out_specs=[_HBM] * (n + m),
        out_shape=[pltpu.HBM(a.shape, a.dtype) for a in both],
        input_output_aliases={i: i for i in range(n + m)},
        compiler_params=pltpu.CompilerParams(has_side_effects=pltpu.SideEffectType.DATAFLOW_SIDE_EFFECTING),
    )(*both, send_sems, recv_sems, after)
    return out[:n], out[n:]


def scatter_finish(remote, local, lands, vec, *, name):
    every = list(remote) + list(local)
    n, r, m = len(every), len(remote), len(lands)

    def body(*refs):
        ins, vec_ref, zones_in = refs[:n], refs[n], refs[n + 1:n + 1 + m]
        vec_out = refs[n + 1 + 2 * m]
        send_sems, recv_sems, local_sems = refs[n + 2 + 2 * m:]
        x, y, c, me = _mesh_place()

        def own(a):
            if a == n:
                return pltpu.make_async_copy(vec_ref, vec_out.at[me], local_sems.at[a])
            return pltpu.make_async_copy(ins[a].at[me], zones_in[every[a][1]].at[me, every[a][2]], local_sems.at[a])

        for a in range(n + 1):
            own(a).start()
        for a in list(range(r)) + [n]:
            for d in range(1, N_DEV):
                peer, peer_lin = _peer(x, y, c, d)
                src = vec_ref if a == n else ins[a].at[peer_lin]
                dst = vec_out.at[me] if a == n else zones_in[every[a][1]].at[me, every[a][2]]
                pltpu.make_async_remote_copy(
                    src_ref=src, dst_ref=dst, send_sem=send_sems.at[min(a, r)], recv_sem=recv_sems.at[min(a, r)],
                    device_id=peer, device_id_type=pl.DeviceIdType.MESH).start()
        for a in range(n + 1):
            own(a).wait()
        for a in list(range(r)) + [n]:
            seven = (vec_out if a == n else ins[a]).at[pl.ds(0, N_DEV - 1)]
            drain = pltpu.make_async_remote_copy(
                src_ref=seven, dst_ref=seven, send_sem=send_sems.at[min(a, r)], recv_sem=recv_sems.at[min(a, r)],
                device_id=(x, y, c), device_id_type=pl.DeviceIdType.MESH)
            drain.wait_send()
            drain.wait_recv()

    out = pl.pallas_call(
        body, name=name,
        in_specs=[_ANY] * (n + 1 + m), out_specs=[_ANY] * (m + 1),
        out_shape=[jax.ShapeDtypeStruct(z.shape, z.dtype) for z in lands]
        + [jax.ShapeDtypeStruct((N_DEV,) + vec.shape, vec.dtype)],
        input_output_aliases={n + 1 + i: i for i in range(m)},
        scratch_shapes=[pltpu.SemaphoreType.DMA((r + 1,)), pltpu.SemaphoreType.DMA((r + 1,)),
                        pltpu.SemaphoreType.DMA((n + 1,))],
    )(*[e[0] for e in every], vec, *lands)
    return out[:m], out[m]


def _rope_tables(pos):
    inv_freq = ROPE_THETA ** (-jnp.arange(0, ROPE, 2, dtype=F32) / ROPE)
    ang = pos.astype(F32)[:, None] * inv_freq
    cos, sin = jnp.cos(ang), jnp.sin(ang)
    pad = jnp.zeros((pos.shape[0], 128 - ROPE), F32)
    return jnp.concatenate([cos, cos, pad + 1.0], axis=1), jnp.concatenate([-sin, sin, pad], axis=1)


def _pad_last(w, n):
    return jnp.pad(w, [(0, 0)] * (w.ndim - 1) + [(0, n - w.shape[-1])])


def kernel(x, positions, g_mix, g_mlp, attn_w_down, attn_g_q_a, attn_g_kv_a, attn_w_uq, attn_w_ukv, attn_g_qnorm, attn_g_knorm, attn_w_o, conv_w_in, conv_w, conv_w_out, mlp_w1, mlp_w2, loss_target, m_g_mix, m_g_mlp, m_attn_w_down, m_attn_g_q_a, m_attn_g_kv_a, m_attn_w_uq, m_attn_w_ukv, m_attn_g_qnorm, m_attn_g_knorm, m_attn_w_o, m_conv_w_in, m_conv_w, m_conv_w_out, m_mlp_w1, m_mlp_w2, v_g_mix, v_g_mlp, v_attn_w_down, v_attn_g_q_a, v_attn_g_kv_a, v_attn_w_uq, v_attn_w_ukv, v_attn_g_qnorm, v_attn_g_knorm, v_attn_w_o, v_conv_w_in, v_conv_w, v_conv_w_out, v_mlp_w1, v_mlp_w2):
    weights = dict(g_mix=g_mix, g_mlp=g_mlp, attn_w_down=attn_w_down, attn_g_q_a=attn_g_q_a, attn_g_kv_a=attn_g_kv_a,
                   attn_w_uq=attn_w_uq, attn_w_ukv=attn_w_ukv, attn_g_qnorm=attn_g_qnorm, attn_g_knorm=attn_g_knorm,
                   attn_w_o=attn_w_o, conv_w_in=conv_w_in, conv_w=conv_w, conv_w_out=conv_w_out, mlp_w1=mlp_w1, mlp_w2=mlp_w2)
    mom1 = dict(g_mix=m_g_mix, g_mlp=m_g_mlp, attn_w_down=m_attn_w_down, attn_g_q_a=m_attn_g_q_a, attn_g_kv_a=m_attn_g_kv_a,
                attn_w_uq=m_attn_w_uq, attn_w_ukv=m_attn_w_ukv, attn_g_qnorm=m_attn_g_qnorm, attn_g_knorm=m_attn_g_knorm,
                attn_w_o=m_attn_w_o, conv_w_in=m_conv_w_in, conv_w=m_conv_w, conv_w_out=m_conv_w_out, mlp_w1=m_mlp_w1, mlp_w2=m_mlp_w2)
    mom2 = dict(g_mix=v_g_mix, g_mlp=v_g_mlp, attn_w_down=v_attn_w_down, attn_g_q_a=v_attn_g_q_a, attn_g_kv_a=v_attn_g_kv_a,
                attn_w_uq=v_attn_w_uq, attn_w_ukv=v_attn_w_ukv, attn_g_qnorm=v_attn_g_qnorm, attn_g_knorm=v_attn_g_knorm,
                attn_w_o=v_attn_w_o, conv_w_in=v_conv_w_in, conv_w=v_conv_w, conv_w_out=v_conv_w_out, mlp_w1=v_mlp_w1, mlp_w2=v_mlp_w2)
    big = ["attn_w_down", "attn_w_uq", "attn_w_ukv", "attn_w_o", "conv_w_in", "conv_w", "conv_w_out", "mlp_w1", "mlp_w2"]
    small = ["g_mix", "g_mlp", "attn_g_q_a", "attn_g_kv_a", "attn_g_qnorm", "attn_g_knorm"]
    order = ["g_mix", "g_mlp", "attn_w_down", "attn_g_q_a", "attn_g_kv_a", "attn_w_uq", "attn_w_ukv", "attn_g_qnorm",
             "attn_g_knorm", "attn_w_o", "conv_w_in", "conv_w", "conv_w_out", "mlp_w1", "mlp_w2"]

    xs = x[0]
    pos = positions[0]
    target = loss_target[0]
    S = xs.shape[0]
    depth = g_mix.shape[0]
    cos_t, sin_t = _rope_tables(pos)
    pos_col, pos_row = pos.reshape(S, 1), pos.reshape(1, S)

    keys, shards = [], []
    for name in big:
        for l in range(weights[name].shape[0]):
            keys.append((name, l))
            shards.append(weights[name][l] if name == "conv_w" else weights[name][l].astype(WIRE_DTYPE))
    first = [j for j, (name, l) in enumerate(keys) if l == 0 and name.startswith("attn")]
    second = [j for j, (name, l) in enumerate(keys) if l == 0 and name.startswith("mlp")]
    later = [j for j in range(len(keys)) if j not in first + second]
    me = 4 * lax.axis_index("x") + 2 * lax.axis_index("y") + lax.axis_index("c")

    def zones_with_own(js, token):
        return [lax.dynamic_update_slice(lax.empty((N_DEV,) + shards[j].shape, shards[j].dtype),
                                         (shards[j] + token[0, 0].astype(shards[j].dtype))[None],
                                         (me,) + (0,) * shards[j].ndim) for j in js]

    arrived, token = _exchange([shards[j] for j in first], "gather_first", scatter=False)
    full = dict(zip([keys[j] for j in first], arrived))
    g1 = exchange_start([shards[j] for j in second], zones_with_own(second, token), None, name="gather_mlp0_start")
    g2 = exchange_start([shards[j] for j in later], zones_with_own(later, g1[4]), None, name="gather_rest_start")
    g_mix_0 = g_mix[0] + g2[4][0, 0]

    def rows(name, l):
        g = full[(name, l)]
        return g.reshape(g.shape[0] * g.shape[1], g.shape[2])

    saved = []
    for i in range(depth):
        l = i // 2
        rec = {"x0": xs}
        if i == 1:
            arrived = exchange_wait(*g2[:4], None, xs, name="gather_rest_wait")[1]
            full.update(zip([keys[j] for j in later], arrived))
        if i % 2 == 0:
            wd3 = _pad_last(rows("attn_w_down", l), DOWN_PAD)[None]
            wuq3 = _pad_last(full[("attn_w_uq", l)], QK_PAD)
            wukv3 = full[("attn_w_ukv", l)]
            gqn = _pad_last(attn_g_qnorm[l][None], QK_PAD)
            gkn = _pad_last(attn_g_knorm[l][None], QK_PAD)
            gqa, gkva = attn_g_q_a[l][None], attn_g_kv_a[l][None]
            h, a = norm_matmul(xs, g_mix_0 if i == 0 else g_mix[i], wd3, name="attn_down")
            q, k, v, cq, ckv = mla_pre_fwd(a, gqa, gkva, wuq3, wukv3, gqn, gkn, cos_t, sin_t)
            o, lse = attn_fwd(q, k, v, pos_col, pos_row)
            x1 = matmul_residual(o, rows("attn_w_o", l), xs, name="attn_out")
            rec.update(h=h, a=a, q=q, k=k, v=v, cq=cq, ckv=ckv, o=o, lse=lse, wd3=wd3, wuq3=wuq3, wukv3=wukv3,
                       gqn=gqn, gkn=gkn, gqa=gqa, gkva=gkva)
        else:
            cw = full[("conv_w", l)].transpose(1, 0, 2).reshape(3, D_MODEL)
            h, bcu = norm_matmul(xs, g_mix[i], full[("conv_w_in", l)], name="conv_in")
            z = conv_fwd(bcu, cw)
            x1 = matmul_residual(z, rows("conv_w_out", l), xs, name="conv_out")
            rec.update(h=h, bcu=bcu, z=z, cw=cw)
        if i == 0:
            arrived = exchange_wait(*g1[:4], None, x1, name="gather_mlp0_wait")[1]
            full.update(zip([keys[j] for j in second], arrived))
        h2, act = norm_matmul(x1, g_mlp[i], full[("mlp_w1", i)], name="mlp_up", mlp=True)
        xs = matmul_residual(act, rows("mlp_w2", i), x1, name="mlp_down")
        rec.update(x1=x1, h2=h2, act=act)
        saved.append(rec)

    sq, dx = loss_head(xs, target)
    loss = lax.psum(sq[0, 0] * (0.5 / D_MODEL), MESH_AXES)

    grads = {name: [None] * weights[name].shape[0] for name in order}
    token = None
    for i in reversed(range(depth)):
        l = i // 2
        rec = saved[i]
        grads["mlp_w2"][i] = mm_tn(rec["act"], dx, name="mlp_down_dw", G=1, out_dtype=WIRE_DTYPE).reshape(N_DEV, -1, D_MODEL)
        du = matmul_nt(dx, rows("mlp_w2", i)[None], name="mlp_down_dx", epi="mlp_du", u=rec["act"])
        grads["mlp_w1"][i] = mm_tn(rec["h2"], du, name="mlp_up_dw", G=N_DEV, out_dtype=WIRE_DTYPE)
        dx1, dg = matmul_nt(du, full[("mlp_w1", i)], name="mlp_up_dx", epi="rms_bwd", x=rec["x1"], g=g_mlp[i], dx=dx)
        grads["g_mlp"][i] = dg[0]
        if i == 0:
            flying = [keys[j] for j in second + later]
            srcs = [grads[name][l_] for name, l_ in flying]
            slots = [(big.index(name), l_) for name, l_ in flying]
            zones = [lax.empty((N_DEV, weights[name].shape[0]) + grads[name][-1].shape[1:], grads[name][-1].dtype)
                     for name in big]
            for src, (k, l_) in zip(srcs, slots):
                own = lax.dynamic_index_in_dim(src, me, 0, keepdims=True)[None]
                zones[k] = lax.dynamic_update_slice(zones[k], own, (me, l_) + (0,) * (src.ndim - 1))
            s_send, s_recv, s_srcs, s_zones, token = exchange_start(srcs, zones, slots, name="scatter_rest_start")
        if i % 2 == 0:
            grads["attn_w_o"][l] = mm_tn(rec["o"], dx1, name="attn_out_dw", G=1, out_dtype=WIRE_DTYPE,
                                         after=token).reshape(N_DEV, -1, D_MODEL)
            do = matmul_nt(dx1, rows("attn_w_o", l)[None], name="attn_out_dx")
            dq, dk, dv = attn_bwd(rec["q"], rec["k"], rec["v"], do, rec["lse"], attn_delta(rec["o"], do), pos_col, pos_row)
            da, dwuq, dwukv, dgqn, dgkn, dgqa, dgkva = mla_pre_bwd(
                dq, dk, dv, rec["a"], rec["cq"], rec["ckv"], rec["gqa"], rec["gkva"], rec["wuq3"], rec["wukv3"],
                rec["gqn"], rec["gkn"], cos_t, sin_t)
            grads["attn_w_uq"][l] = dwuq[:, :, :QK_DIM].astype(WIRE_DTYPE)
            grads["attn_w_ukv"][l] = dwukv.astype(WIRE_DTYPE)
            grads["attn_g_qnorm"][l] = dgqn[0, :QK_DIM]
            grads["attn_g_knorm"][l] = dgkn[0, :QK_DIM]
            grads["attn_g_q_a"][l] = dgqa[0]
            grads["attn_g_kv_a"][l] = dgkva[0]
            dwd = mm_tn(rec["h"], da, name="attn_down_dw", G=1, out_dtype=WIRE_DTYPE)
            grads["attn_w_down"][l] = dwd[0, :, :DOWN].reshape(N_DEV, -1, DOWN)
            dx, dg = matmul_nt(da, rec["wd3"], name="attn_down_dx", epi="rms_bwd", x=rec["x0"], g=g_mix[i], dx=dx1)
        else:
            grads["conv_w_out"][l] = mm_tn(rec["z"], dx1, name="conv_out_dw", G=1, out_dtype=WIRE_DTYPE).reshape(N_DEV, -1, D_MODEL)
            dz = matmul_nt(dx1, rows("conv_w_out", l)[None], name="conv_out_dx")
            dgb, dgc, du_, dcw = conv_bwd(dz, rec["bcu"], rec["cw"])
            grads["conv_w"][l] = dcw.reshape(3, N_DEV, -1).transpose(1, 0, 2)
            dbcu = jnp.concatenate([dgb, dgc, du_], axis=1)
            grads["conv_w_in"][l] = mm_tn(rec["h"], dbcu, name="conv_in_dw", G=N_DEV, out_dtype=WIRE_DTYPE)
            dx, dg = matmul_nt(dbcu, full[("conv_w_in", l)], name="conv_in_dx", epi="rms_bwd", x=rec["x0"], g=g_mix[i], dx=dx1)
        grads["g_mix"][i] = dg[0]

    sizes = [weights[name].size for name in small]
    n_small = sum(sizes)
    rows_small = -(-n_small // (8 * 128)) * 8

    def pack(tree):
        flat = jnp.concatenate([jnp.stack(tree[name]).reshape(-1) if isinstance(tree[name], list) else tree[name].reshape(-1)
                                for name in small])
        return jnp.pad(flat, (0, rows_small * 128 - n_small)).reshape(rows_small, 128)

    s_srcs, s_zones = exchange_wait(s_send, s_recv, s_srcs, s_zones, slots, dx, name="scatter_rest_wait")
    remote = [(grads[name][l_], big.index(name), l_) for name, l_ in (keys[j] for j in first)]
    parts, gain_parts = scatter_finish(remote, [], s_zones, pack(grads), name="scatter_last")

    out = {}
    for name, part in zip(big, parts):
        w = weights[name]
        flat = lambda t: t.reshape(-1, t.shape[-1])
        res = adamw(part.reshape(N_DEV, -1, w.shape[-1]), flat(w), flat(mom1[name]), flat(mom2[name]), name="adamw_" + name)
        out[name] = [r.reshape(w.shape) for r in res]
    res = adamw(gain_parts, pack(weights), pack(mom1), pack(mom2), name="adamw_gains")
    offset = 0
    for name, size in zip(small, sizes):
        out[name] = [r.reshape(-1)[offset:offset + size].reshape(weights[name].shape) for r in res]
        offset += size

    return (loss, dx[None], *[out[n][0] for n in order], *[out[n][1] for n in order],
            *[out[n][2] for n in order], *[out[n][3] for n in order])
```

```python
import jax
import jax.numpy as jnp
import numpy as np
from jax import lax
from jax.experimental import pallas as pl
from jax.experimental.pallas import tpu as pltpu

F32 = jnp.float32
MXU_DTYPE = jnp.bfloat16
WIRE_DTYPE = jnp.bfloat16

D_MODEL = 1024
N_HEADS = 8
NOPE = 128
ROPE = 64
QK_DIM = NOPE + ROPE
QK_PAD = 256
V_DIM = 128
Q_LORA = 256
KV_LORA = 128
DOWN = Q_LORA + KV_LORA + ROPE
DOWN_PAD = 512
ROPE_THETA = 10000.0
EPS = 1e-6
SM_SCALE = QK_DIM ** -0.5
LOG2E = 1.4426950408889634
Q_PRESCALE = SM_SCALE * LOG2E
ADAM_LR, ADAM_B1, ADAM_B2, ADAM_EPS, ADAM_WD, ADAM_STEP = 0.001, 0.9, 0.999, 1e-08, 0.01, 10
N_DEV = 8
MESH_AXES = ("x", "y", "c")

TM = 512
TM_WIDE = 1024
TM_TOKENS_TN = 2048
TQ = 512
HEADS_FWD = 8
TQ_BWD = 1024
BWD_CHUNK = 256
TROW = 512
TCH = 512
VMEM_LIMIT = 48 << 20

NN = (((1,), (0,)), ((), ()))
NT = (((1,), (1,)), ((), ()))
TN = (((0,), (0,)), ((), ()))


def _dot(a, b, dims=NN):
    return lax.dot_general(a.astype(MXU_DTYPE), b.astype(MXU_DTYPE), dims, preferred_element_type=F32)


def _params(n_axes):
    return pltpu.CompilerParams(dimension_semantics=("arbitrary",) * n_axes, vmem_limit_bytes=VMEM_LIMIT)


def _rms(xv, n):
    r = lax.rsqrt(jnp.sum(xv * xv, axis=-1, keepdims=True) / n + EPS)
    return xv * r, r


def _rms_bwd(dy, xhat, r, g, n):
    dg = jnp.sum(dy * xhat, axis=0, keepdims=True)
    dxh = dy * g
    dx = r * (dxh - xhat * (jnp.sum(dxh * xhat, axis=-1, keepdims=True) / n))
    return dx, dg


def _swap_halves(t):
    lane = lax.broadcasted_iota(jnp.int32, t.shape, 1)
    return jnp.where(lane < ROPE // 2, pltpu.roll(t, 128 - ROPE // 2, 1), pltpu.roll(t, ROPE // 2, 1))


def _rope(t, cos_t, sin_t):
    return t * cos_t + _swap_halves(t) * sin_t


def _rope_bwd(dout, cos_t, sin_t):
    return dout * cos_t + _swap_halves(dout * sin_t)


def rms_mm(x, g, w3, *, name, mlp=False):
    S, D = x.shape
    G, _, Nb = w3.shape
    tm = min(TM_WIDE, S)
    tn = Nb if Nb <= 512 else 512
    nj = Nb // tn
    N = G * Nb

    def body(x_ref, g_ref, w_ref, h_ref, *rest):
        hs = rest[-1]

        @pl.when(pl.program_id(1) == 0)
        def _():
            xv = x_ref[...]
            r = lax.rsqrt(jnp.mean(xv * xv, axis=-1, keepdims=True) + EPS)
            h = (xv * r * g_ref[...]).astype(hs.dtype)
            hs[...] = h
            h_ref[...] = h

        acc = lax.dot_general(hs[...], w_ref[...].astype(hs.dtype), NN, preferred_element_type=F32)
        if mlp:
            rl = jnp.maximum(acc, 0.0)
            rest[0][...] = (rl * rl).astype(rest[0].dtype)
        else:
            rest[0][...] = acc

    out_shape = [jax.ShapeDtypeStruct((S, D), MXU_DTYPE), jax.ShapeDtypeStruct((S, N), MXU_DTYPE if mlp else F32)]
    out_specs = [pl.BlockSpec((tm, D), lambda i, j: (i, 0)), pl.BlockSpec((tm, tn), lambda i, j: (i, j))]
    return pl.pallas_call(
        body, name=name, grid=(S // tm, G * nj),
        in_specs=[pl.BlockSpec((tm, D), lambda i, j: (i, 0)),
                  pl.BlockSpec((1, D), lambda i, j: (0, 0)),
                  pl.BlockSpec((None, D, tn), lambda i, j: (j // nj, 0, j % nj))],
        out_specs=out_specs, out_shape=out_shape,
        scratch_shapes=[pltpu.VMEM((tm, D), MXU_DTYPE)],
        compiler_params=_params(2),
    )(x, g.reshape(1, D), w3)


def mm_res(a, w, res, *, name):
    S, K = a.shape
    _, N = w.shape
    tm = min(TM_WIDE, S)
    tk = min(K, 1024)
    nk = K // tk

    def body(a_ref, w_ref, r_ref, o_ref, acc):
        k = pl.program_id(1)

        @pl.when(k == 0)
        def _():
            acc[...] = jnp.zeros_like(acc)

        acc[...] += _dot(a_ref[...], w_ref[...])

        @pl.when(k == nk - 1)
        def _():
            o_ref[...] = r_ref[...] + acc[...]

    return pl.pallas_call(
        body, name=name, grid=(S // tm, nk),
        in_specs=[pl.BlockSpec((tm, tk), lambda i, k: (i, k)),
                  pl.BlockSpec((tk, N), lambda i, k: (k, 0)),
                  pl.BlockSpec((tm, N), lambda i, k: (i, 0))],
        out_specs=pl.BlockSpec((tm, N), lambda i, k: (i, 0)),
        out_shape=jax.ShapeDtypeStruct((S, N), F32),
        scratch_shapes=[pltpu.VMEM((tm, N), F32)],
        compiler_params=_params(2),
    )(a, w, res)


def mm_nt(a, w3, *, name, epi="plain", u=None, x=None, g=None, dx=None):
    S, N = a.shape
    G, Ko, Nb = w3.shape
    assert N == G * Nb
    tm = min(TM if epi == "rms_bwd" else TM_WIDE, S)
    tkk = Nb if Nb <= 1024 else 1024
    nk = Nb // tkk
    ks = G * nk
    tko = Ko if epi == "rms_bwd" else min(Ko, 512)

    def body(a_ref, w_ref, *rest):
        acc = rest[-1]
        i, k = pl.program_id(0), pl.program_id(2)

        @pl.when(k == 0)
        def _():
            acc[...] = jnp.zeros_like(acc)

        acc[...] += _dot(a_ref[...], w_ref[...], NT)

        @pl.when(k == ks - 1)
        def _():
            if epi == "plain":
                rest[0][...] = acc[...]
            elif epi == "mlp_du":
                u_ref, o_ref = rest[0], rest[1]
                o_ref[...] = (acc[...] * (2.0 * jnp.sqrt(u_ref[...].astype(F32)))).astype(o_ref.dtype)
            else:
                x_ref, g_ref, dx_ref, o_ref, dg_ref = rest[:5]
                xhat, r = _rms(x_ref[...], Ko)
                dxb, dg = _rms_bwd(acc[...], xhat, r, g_ref[...], Ko)
                o_ref[...] = dx_ref[...] + dxb

                @pl.when(i == 0)
                def _():
                    dg_ref[...] = dg

                @pl.when(i != 0)
                def _():
                    dg_ref[...] += dg

    in_specs = [pl.BlockSpec((tm, tkk), lambda i, j, k: (i, k)),
                pl.BlockSpec((None, tko, tkk), lambda i, j, k: (k // nk, j, k % nk))]
    args = [a, w3]
    tile = pl.BlockSpec((tm, tko), lambda i, j, k: (i, j))
    if epi == "plain":
        out_shape, out_specs = jax.ShapeDtypeStruct((S, Ko), F32), tile
    elif epi == "mlp_du":
        in_specs.append(tile)
        args.append(u)
        out_shape, out_specs = jax.ShapeDtypeStruct((S, Ko), MXU_DTYPE), tile
    else:
        vec = pl.BlockSpec((1, Ko), lambda i, j, k: (0, 0))
        in_specs += [tile, vec, tile]
        args += [x, g.reshape(1, Ko), dx]
        out_shape = [jax.ShapeDtypeStruct((S, Ko), F32), jax.ShapeDtypeStruct((1, Ko), F32)]
        out_specs = [tile, vec]
    return pl.pallas_call(
        body, name=name, grid=(S // tm, Ko // tko, ks),
        in_specs=in_specs, out_specs=out_specs, out_shape=out_shape,
        scratch_shapes=[pltpu.VMEM((tm, tko), F32)],
        compiler_params=_params(3),
    )(*args)


def _resident(shape):
    return pl.BlockSpec(shape, lambda i: (0,) * len(shape))


def norm_matmul(x, g, w3, *, name, mlp=False):
    S, D = x.shape
    G, _, Nb = w3.shape
    tm = min(TM, S)
    N = G * Nb

    def body(x_ref, g_ref, w_ref, h_ref, o_ref):
        xv = x_ref[...]
        r = lax.rsqrt(jnp.mean(xv * xv, axis=-1, keepdims=True) + EPS)
        h = (xv * r * g_ref[...]).astype(h_ref.dtype)
        h_ref[...] = h
        for gi in range(G):
            acc = _dot(h, w_ref[gi])
            if mlp:
                acc = jnp.square(jnp.maximum(acc, 0.0))
            o_ref[:, gi * Nb:(gi + 1) * Nb] = acc.astype(o_ref.dtype)

    rows = lambda w: pl.BlockSpec((tm, w), lambda i: (i, 0))
    return pl.pallas_call(
        body, name=name, grid=(S // tm,),
        in_specs=[rows(D), _resident((1, D)), _resident((G, D, Nb))],
        out_specs=[rows(D), rows(N)],
        out_shape=[jax.ShapeDtypeStruct((S, D), MXU_DTYPE), jax.ShapeDtypeStruct((S, N), MXU_DTYPE if mlp else F32)],
        compiler_params=_params(1),
    )(x, g.reshape(1, D), w3)


def matmul_residual(a, w, res, *, name):
    S, K = a.shape
    _, N = w.shape
    tm = min(TM, S)

    def body(a_ref, w_ref, r_ref, o_ref):
        o_ref[...] = r_ref[...] + _dot(a_ref[...], w_ref[...])

    rows = lambda w_: pl.BlockSpec((tm, w_), lambda i: (i, 0))
    return pl.pallas_call(
        body, name=name, grid=(S // tm,),
        in_specs=[rows(K), _resident((K, N)), rows(N)],
        out_specs=rows(N), out_shape=jax.ShapeDtypeStruct((S, N), F32),
        compiler_params=_params(1),
    )(a, w, res)


def matmul_nt(a, w3, *, name, epi="plain", u=None, x=None, g=None, dx=None):
    S, N = a.shape
    G, Ko, Nb = w3.shape
    assert N == G * Nb
    tm = min(TM, S)
    tko = min(Ko, 512)

    def body(a_ref, w_ref, *rest):
        if epi == "mlp_du":
            u_ref, o_ref = rest
            av = a_ref[...].astype(MXU_DTYPE)
            for j in range(Ko // tko):
                cols = slice(j * tko, (j + 1) * tko)
                da = _dot(av, w_ref[0, cols, :], NT)
                o_ref[:, cols] = (da * (2.0 * jnp.sqrt(u_ref[:, cols].astype(F32)))).astype(o_ref.dtype)
            return
        acc = _dot(a_ref[:, :Nb], w_ref[0], NT)
        for gi in range(1, G):
            acc = acc + _dot(a_ref[:, gi * Nb:(gi + 1) * Nb], w_ref[gi], NT)
        if epi == "plain":
            rest[0][...] = acc
        else:
            x_ref, g_ref, dx_ref, o_ref, dg_ref = rest
            xhat, r = _rms(x_ref[...], Ko)
            dxb, dg = _rms_bwd(acc, xhat, r, g_ref[...], Ko)
            o_ref[...] = dx_ref[...] + dxb

            @pl.when(pl.program_id(0) == 0)
            def _():
                dg_ref[...] = dg

            @pl.when(pl.program_id(0) != 0)
            def _():
                dg_ref[...] += dg

    rows = lambda w_: pl.BlockSpec((tm, w_), lambda i: (i, 0))
    in_specs = [rows(N), _resident((G, Ko, Nb))]
    args = [a, w3]
    if epi == "plain":
        out_shape, out_specs = jax.ShapeDtypeStruct((S, Ko), F32), rows(Ko)
    elif epi == "mlp_du":
        in_specs.append(rows(Ko))
        args.append(u)
        out_shape, out_specs = jax.ShapeDtypeStruct((S, Ko), MXU_DTYPE), rows(Ko)
    else:
        in_specs += [rows(Ko), _resident((1, Ko)), rows(Ko)]
        args += [x, g.reshape(1, Ko), dx]
        out_shape = [jax.ShapeDtypeStruct((S, Ko), F32), jax.ShapeDtypeStruct((1, Ko), F32)]
        out_specs = [rows(Ko), _resident((1, Ko))]
    return pl.pallas_call(
        body, name=name, grid=(S // tm,),
        in_specs=in_specs, out_specs=out_specs, out_shape=out_shape,
        compiler_params=_params(1),
    )(*args)


def mm_tn(a, b, *, name, G, out_dtype, after=None):
    order = [] if after is None else [after]
    S, Ka = a.shape
    _, N = b.shape
    Nb = N // G
    tm = min(TM_TOKENS_TN if b.dtype.itemsize == 2 else TM_TOKENS_TN // 2, S)
    tka = min(Ka, 1024)
    tnb = Nb if Nb <= 1024 else 1024
    nj = Nb // tnb
    ns = S // tm

    def body(a_ref, b_ref, *rest):
        o_ref, acc = rest[-2:]
        s = pl.program_id(2)

        @pl.when(s == 0)
        def _():
            acc[...] = jnp.zeros_like(acc)

        acc[...] += _dot(a_ref[...], b_ref[...], TN)

        @pl.when(s == ns - 1)
        def _():
            o_ref[...] = acc[...].astype(o_ref.dtype)

    return pl.pallas_call(
        body, name=name, grid=(Ka // tka, G * nj, ns),
        in_specs=[pl.BlockSpec((tm, tka), lambda i, j, s: (s, i)),
                  pl.BlockSpec((tm, tnb), lambda i, j, s: (s, j))] + [pl.BlockSpec(memory_space=pl.ANY)] * len(order),
        out_specs=pl.BlockSpec((None, tka, tnb), lambda i, j, s: (j // nj, i, j % nj)),
        out_shape=jax.ShapeDtypeStruct((G, Ka, Nb), out_dtype),
        scratch_shapes=[pltpu.VMEM((tka, tnb), F32)],
        compiler_params=_params(3),
    )(a, b, *order)


def mla_pre_fwd(a, gqa, gkva, wuq3, wukv3, gqn, gkn, cos_t, sin_t):
    S = a.shape[0]
    tm = min(TM, S)
    H = N_HEADS

    def body(a_ref, gqa_ref, gkva_ref, wuq_ref, wukv_ref, gqn_ref, gkn_ref, cos_ref, sin_ref,
             q_ref, k_ref, v_ref, cq_ref, ckv_ref):
        av = a_ref[...]
        cq = (_rms(av[:, :Q_LORA], Q_LORA)[0] * gqa_ref[...]).astype(cq_ref.dtype)
        ckv = (_rms(av[:, Q_LORA:Q_LORA + KV_LORA], KV_LORA)[0] * gkva_ref[...]).astype(ckv_ref.dtype)
        cq_ref[...] = cq
        ckv_ref[...] = ckv
        kpe = av[:, Q_LORA + KV_LORA:]
        cos_v, sin_v = cos_ref[...], sin_ref[...]
        for h in range(H):
            qn = _rms(_dot(cq, wuq_ref[h]), QK_DIM)[0] * gqn_ref[...]
            qr = jnp.concatenate([qn[:, :NOPE], _rope(qn[:, NOPE:], cos_v, sin_v)], axis=1)
            q_ref[h] = (qr * Q_PRESCALE).astype(q_ref.dtype)
            kvp = _dot(ckv, wukv_ref[h])
            kn = _rms(jnp.concatenate([kvp[:, :NOPE], kpe], axis=1), QK_DIM)[0] * gkn_ref[...]
            k_ref[h] = jnp.concatenate([kn[:, :NOPE], _rope(kn[:, NOPE:], cos_v, sin_v)], axis=1).astype(k_ref.dtype)
            v_ref[h] = kvp[:, NOPE:].astype(v_ref.dtype)

    row = lambda w: pl.BlockSpec((tm, w), lambda i: (i, 0))
    heads = lambda w: pl.BlockSpec((H, tm, w), lambda i: (0, i, 0))
    return pl.pallas_call(
        body, name="mla_pre_fwd", grid=(S // tm,),
        in_specs=[row(DOWN_PAD), _resident((1, Q_LORA)), _resident((1, KV_LORA)),
                  _resident((H, Q_LORA, QK_PAD)), _resident((H, KV_LORA, NOPE + V_DIM)),
                  _resident((1, QK_PAD)), _resident((1, QK_PAD)), row(128), row(128)],
        out_specs=[heads(QK_PAD), heads(QK_PAD), heads(V_DIM), row(Q_LORA), row(KV_LORA)],
        out_shape=[jax.ShapeDtypeStruct((H, S, QK_PAD), MXU_DTYPE),
                   jax.ShapeDtypeStruct((H, S, QK_PAD), MXU_DTYPE),
                   jax.ShapeDtypeStruct((H, S, V_DIM), MXU_DTYPE),
                   jax.ShapeDtypeStruct((S, Q_LORA), MXU_DTYPE),
                   jax.ShapeDtypeStruct((S, KV_LORA), MXU_DTYPE)],
        compiler_params=_params(1),
    )(a, gqa, gkva, wuq3, wukv3, gqn, gkn, cos_t, sin_t)


def mla_pre_bwd(dq, dk, dv, a, cq, ckv, gqa, gkva, wuq3, wukv3, gqn, gkn, cos_t, sin_t):
    S = a.shape[0]
    tm = min(TM, S)
    H = N_HEADS

    def body(dq_ref, dk_ref, dv_ref, a_ref, cq_ref, ckv_ref, gqa_ref, gkva_ref, wuq_ref, wukv_ref, gqn_ref, gkn_ref,
             cos_ref, sin_ref, da_ref, dwuq_ref, dwukv_ref, dgqn_ref, dgkn_ref, dgqa_ref, dgkva_ref):
        @pl.when(pl.program_id(0) == 0)
        def _():
            for ref in (dwuq_ref, dwukv_ref, dgqn_ref, dgkn_ref, dgqa_ref, dgkva_ref):
                ref[...] = jnp.zeros_like(ref)

        av = a_ref[...]
        kpe = av[:, Q_LORA + KV_LORA:]
        cos_v, sin_v = cos_ref[...], sin_ref[...]
        cqv, ckvv = cq_ref[...], ckv_ref[...]
        dcq = jnp.zeros((tm, Q_LORA), F32)
        dckv = jnp.zeros((tm, KV_LORA), F32)
        dkpe = jnp.zeros((tm, 128), F32)
        dgqn = jnp.zeros((1, QK_PAD), F32)
        dgkn = jnp.zeros((1, QK_PAD), F32)
        up = lambda h: (_dot(cqv, wuq_ref[h]), _dot(ckvv, wukv_ref[h]))
        nxt = up(0)
        for h in range(H):
            wuq, wukv = wuq_ref[h], wukv_ref[h]
            qp, kvp = nxt
            if h + 1 < H:
                nxt = up(h + 1)
            qhat, rq = _rms(qp, QK_DIM)
            dqr = dq_ref[h] * SM_SCALE
            dqn = jnp.concatenate([dqr[:, :NOPE], _rope_bwd(dqr[:, NOPE:], cos_v, sin_v)], axis=1)
            dqp, dg = _rms_bwd(dqn, qhat, rq, gqn_ref[...], QK_DIM)
            dgqn = dgqn + dg
            dqp = dqp.astype(MXU_DTYPE)
            dwuq_ref[h] += _dot(cqv, dqp, TN)
            dcq = dcq + _dot(dqp, wuq, NT)
            khat, rk = _rms(jnp.concatenate([kvp[:, :NOPE], kpe], axis=1), QK_DIM)
            dkr = dk_ref[h] * (1.0 / LOG2E)
            dkn = jnp.concatenate([dkr[:, :NOPE], _rope_bwd(dkr[:, NOPE:], cos_v, sin_v)], axis=1)
            dkk, dg = _rms_bwd(dkn, khat, rk, gkn_ref[...], QK_DIM)
            dgkn = dgkn + dg
            dkpe = dkpe + dkk[:, NOPE:]
            dkvp = jnp.concatenate([dkk[:, :NOPE], dv_ref[h]], axis=1).astype(MXU_DTYPE)
            dwukv_ref[h] += _dot(ckvv, dkvp, TN)
            dckv = dckv + _dot(dkvp, wukv, NT)
        dgqn_ref[...] += dgqn
        dgkn_ref[...] += dgkn
        ahat, r = _rms(av[:, :Q_LORA], Q_LORA)
        daq, dg = _rms_bwd(dcq, ahat, r, gqa_ref[...], Q_LORA)
        dgqa_ref[...] += dg
        ahat, r = _rms(av[:, Q_LORA:Q_LORA + KV_LORA], KV_LORA)
        dakv, dg = _rms_bwd(dckv, ahat, r, gkva_ref[...], KV_LORA)
        dgkva_ref[...] += dg
        da_ref[...] = jnp.concatenate([daq, dakv, dkpe], axis=1)

    row = lambda w: pl.BlockSpec((tm, w), lambda i: (i, 0))
    heads = lambda w: pl.BlockSpec((H, tm, w), lambda i: (0, i, 0))
    return pl.pallas_call(
        body, name="mla_pre_bwd", grid=(S // tm,),
        in_specs=[heads(QK_PAD), heads(QK_PAD), heads(V_DIM), row(DOWN_PAD), row(Q_LORA), row(KV_LORA),
                  _resident((1, Q_LORA)), _resident((1, KV_LORA)),
                  _resident((H, Q_LORA, QK_PAD)), _resident((H, KV_LORA, NOPE + V_DIM)),
                  _resident((1, QK_PAD)), _resident((1, QK_PAD)), row(128), row(128)],
        out_specs=[row(DOWN_PAD), _resident((H, Q_LORA, QK_PAD)), _resident((H, KV_LORA, NOPE + V_DIM)),
                   _resident((1, QK_PAD)), _resident((1, QK_PAD)), _resident((1, Q_LORA)), _resident((1, KV_LORA))],
        out_shape=[jax.ShapeDtypeStruct((S, DOWN_PAD), F32),
                   jax.ShapeDtypeStruct((H, Q_LORA, QK_PAD), F32),
                   jax.ShapeDtypeStruct((H, KV_LORA, NOPE + V_DIM), F32),
                   jax.ShapeDtypeStruct((1, QK_PAD), F32), jax.ShapeDtypeStruct((1, QK_PAD), F32),
                   jax.ShapeDtypeStruct((1, Q_LORA), F32), jax.ShapeDtypeStruct((1, KV_LORA), F32)],
        compiler_params=_params(1),
    )(dq, dk, dv, a, cq, ckv, gqa, gkva, wuq3, wukv3, gqn, gkn, cos_t, sin_t)


def _pair_tables(nb, key_major):
    if key_major:
        pairs = [(qi, kj) for kj in range(nb) for qi in range(kj, nb)]
    else:
        pairs = [(qi, ki) for qi in range(nb) for ki in range(qi + 1)]
    return (jnp.asarray(np.array([p[0] for p in pairs], np.int32)),
            jnp.asarray(np.array([p[1] for p in pairs], np.int32)))


def _scores_t(k, q, pk_col, pq_row, masked):
    s = _dot(k, q, NT)
    return jnp.where(pq_row >= pk_col, s, jnp.finfo(F32).min) if masked else s


def attn_fwd(q, k, v, pos_col, pos_row):
    H, S, _ = q.shape
    t = min(TQ, S)
    nb = S // t
    hb = HEADS_FWD
    qt, kt = _pair_tables(nb, key_major=False)

    def body(qt_ref, kt_ref, q_ref, k_ref, v_ref, pk_ref, pq_ref, o_ref, lse_ref, m_s, l_s, acc):
        step = pl.program_id(1)
        qi, ki = qt_ref[step], kt_ref[step]

        @pl.when(ki == 0)
        def _():
            m_s[...] = jnp.full_like(m_s, -jnp.inf)
            l_s[...] = jnp.zeros_like(l_s)
            acc[...] = jnp.zeros_like(acc)

        def update(masked):
            scores = lambda hh: _scores_t(k_ref[hh], q_ref[hh], pk_ref[...], pq_ref[...], masked)
            def weighted_values(hh, p, alpha):
                acc[hh] = alpha * acc[hh] + _dot(v_ref[hh], p, TN)

            s_next = scores(0)
            pending = None
            for hh in range(hb):
                s = s_next
                if hh + 1 < hb:
                    s_next = scores(hh + 1)
                m_old = m_s[hh]
                m_new = jnp.maximum(m_old, jnp.max(s, axis=0, keepdims=True))
                p = jnp.exp2(s - m_new)
                alpha = jnp.exp2(m_old - m_new)
                l_s[hh] = alpha * l_s[hh] + jnp.sum(p, axis=0, keepdims=True)
                m_s[hh] = m_new
                if pending is not None:
                    weighted_values(*pending)
                pending = (hh, p, alpha)
            weighted_values(*pending)

        @pl.when(ki < qi)
        def _():
            update(False)

        @pl.when(ki == qi)
        def _():
            update(True)
            for hh in range(hb):
                o_ref[:, hh * V_DIM:(hh + 1) * V_DIM] = (acc[hh] / l_s[hh]).T
                lse_ref[hh] = m_s[hh] + jnp.log(l_s[hh]) * LOG2E

    grid_spec = pltpu.PrefetchScalarGridSpec(
        num_scalar_prefetch=2, grid=(H // hb, qt.shape[0]),
        in_specs=[pl.BlockSpec((hb, t, QK_PAD), lambda h, s, qt, kt: (h, qt[s], 0)),
                  pl.BlockSpec((hb, t, QK_PAD), lambda h, s, qt, kt: (h, kt[s], 0)),
                  pl.BlockSpec((hb, t, V_DIM), lambda h, s, qt, kt: (h, kt[s], 0)),
                  pl.BlockSpec((t, 1), lambda h, s, qt, kt: (kt[s], 0)),
                  pl.BlockSpec((1, t), lambda h, s, qt, kt: (0, qt[s]))],
        out_specs=[pl.BlockSpec((t, hb * V_DIM), lambda h, s, qt, kt: (qt[s], h)),
                   pl.BlockSpec((hb, 1, t), lambda h, s, qt, kt: (h, 0, qt[s]))],
        scratch_shapes=[pltpu.VMEM((hb, 1, t), F32), pltpu.VMEM((hb, 1, t), F32), pltpu.VMEM((hb, V_DIM, t), F32)])
    return pl.pallas_call(
        body, name="attn_fwd", grid_spec=grid_spec,
        out_shape=[jax.ShapeDtypeStruct((S, H * V_DIM), F32), jax.ShapeDtypeStruct((H, 1, S), F32)],
        compiler_params=_params(2),
    )(qt, kt, q, k, v, pos_col, pos_row)


def attn_delta(o, do):
    S = o.shape[0]
    t = min(TQ, S)

    def body(o_ref, do_ref, d_ref):
        for h in range(N_HEADS):
            cols = slice(h * V_DIM, (h + 1) * V_DIM)
            d_ref[h] = jnp.sum((o_ref[:, cols] * do_ref[:, cols]).T, axis=0, keepdims=True)

    blk = pl.BlockSpec((t, N_HEADS * V_DIM), lambda i: (i, 0))
    return pl.pallas_call(
        body, name="attn_delta", grid=(S // t,),
        in_specs=[blk, blk],
        out_specs=pl.BlockSpec((N_HEADS, 1, t), lambda i: (0, 0, i)),
        out_shape=jax.ShapeDtypeStruct((N_HEADS, 1, S), F32),
        compiler_params=_params(1),
    )(o, do)


def attn_bwd(q, k, v, do, lse, delta, pos_col, pos_row):
    H, S, _ = q.shape
    t = min(TQ_BWD, S)
    nb = S // t
    qt, kt = _pair_tables(nb, key_major=True)
    tc = min(BWD_CHUNK, t)

    def body(qt_ref, kt_ref, q_ref, k_ref, v_ref, do_ref, lse_ref, dl_ref, pk_ref, pq_ref, dq_ref, dk_ref, dv_ref):
        step = pl.program_id(1)
        qi, kj = qt_ref[step], kt_ref[step]

        @pl.when(step == 0)
        def _():
            dq_ref[...] = jnp.zeros_like(dq_ref)

        @pl.when(qi == kj)
        def _():
            dk_ref[...] = jnp.zeros_like(dk_ref)
            dv_ref[...] = jnp.zeros_like(dv_ref)

        def update(masked):
            kv_, vv = k_ref[...], v_ref[...]

            def first_matmuls(c):
                cols = slice(c * tc, (c + 1) * tc)
                qc = q_ref[cols, :]
                doc = do_ref[cols, :].astype(MXU_DTYPE)
                return qc, doc, _scores_t(kv_, qc, pk_ref[...], pq_ref[:, cols], masked), _dot(vv, doc, NT)

            nxt = first_matmuls(0)
            for c in range(t // tc):
                qc, doc, s, dp = nxt
                if c + 1 < t // tc:
                    nxt = first_matmuls(c + 1)
                cols = slice(c * tc, (c + 1) * tc)
                p = jnp.exp2(s - lse_ref[:, cols])
                ds = (p * (dp - dl_ref[:, cols])).astype(MXU_DTYPE)
                dv_ref[...] += _dot(p, doc)
                dk_ref[...] += _dot(ds, qc)
                rows = pl.ds(pl.multiple_of(qi * t + c * tc, tc), tc)
                dq_ref[rows, :] += _dot(ds, kv_, TN)

        @pl.when(qi == kj)
        def _():
            update(True)

        @pl.when(qi != kj)
        def _():
            update(False)

    q_idx = lambda h, s, qt, kt: (h, qt[s], 0)
    k_idx = lambda h, s, qt, kt: (h, kt[s], 0)
    row_idx = lambda h, s, qt, kt: (h, 0, qt[s])
    grid_spec = pltpu.PrefetchScalarGridSpec(
        num_scalar_prefetch=2, grid=(H, qt.shape[0]),
        in_specs=[pl.BlockSpec((None, t, QK_PAD), q_idx),
                  pl.BlockSpec((None, t, QK_PAD), k_idx),
                  pl.BlockSpec((None, t, V_DIM), k_idx),
                  pl.BlockSpec((t, V_DIM), lambda h, s, qt, kt: (qt[s], h)),
                  pl.BlockSpec((None, 1, t), row_idx),
                  pl.BlockSpec((None, 1, t), row_idx),
                  pl.BlockSpec((t, 1), lambda h, s, qt, kt: (kt[s], 0)),
                  pl.BlockSpec((1, t), lambda h, s, qt, kt: (0, qt[s]))],
        out_specs=[pl.BlockSpec((None, S, QK_PAD), lambda h, s, qt, kt: (h, 0, 0)),
                   pl.BlockSpec((None, t, QK_PAD), k_idx),
                   pl.BlockSpec((None, t, V_DIM), k_idx)])
    return pl.pallas_call(
        body, name="attn_bwd", grid_spec=grid_spec,
        out_shape=[jax.ShapeDtypeStruct((H, S, QK_PAD), F32), jax.ShapeDtypeStruct((H, S, QK_PAD), F32),
                   jax.ShapeDtypeStruct((H, S, V_DIM), F32)],
        compiler_params=_params(2),
    )(qt, kt, q, k, v, do, lse, delta, pos_col, pos_row)


def _conv_specs(S, tr, tc):
    nc = D_MODEL // tc
    hb = tr // 8
    main = lambda third: pl.BlockSpec((tr, tc), lambda c, r: (r, third * nc + c))
    prev = lambda third: pl.BlockSpec((8, tc), lambda c, r: (jnp.maximum(r * hb - 1, 0), third * nc + c))
    nxt = lambda third: pl.BlockSpec((8, tc), lambda c, r: (jnp.minimum((r + 1) * hb, S // 8 - 1), third * nc + c))
    return main, prev, nxt


def _conv_taps(gc, uu, w_ref, first):
    u2 = gc * uu
    rows = lax.broadcasted_iota(jnp.int32, u2.shape, 0)
    u2 = jnp.where((rows < 8) & first, 0.0, u2)
    s1 = pltpu.roll(u2, 1, 0)
    s2 = pltpu.roll(u2, 2, 0)
    u3 = w_ref[2:3, :] * u2 + w_ref[1:2, :] * s1 + w_ref[0:1, :] * s2
    return u2, s1, s2, u3


def conv_fwd(bcu, cw):
    S = bcu.shape[0]
    tr, tc = min(TROW, S), TCH
    main, prev, _ = _conv_specs(S, tr, tc)

    def body(gb_ref, gc_ref, u_ref, gch_ref, uh_ref, w_ref, z_ref):
        gc = jnp.concatenate([gch_ref[...], gc_ref[...]], axis=0)
        uu = jnp.concatenate([uh_ref[...], u_ref[...]], axis=0)
        u3 = _conv_taps(gc, uu, w_ref, pl.program_id(1) == 0)[3]
        z_ref[...] = (gb_ref[...] * u3[8:]).astype(z_ref.dtype)

    return pl.pallas_call(
        body, name="conv_fwd", grid=(D_MODEL // tc, S // tr),
        in_specs=[main(0), main(1), main(2), prev(1), prev(2), pl.BlockSpec((3, tc), lambda c, r: (0, c))],
        out_specs=pl.BlockSpec((tr, tc), lambda c, r: (r, c)),
        out_shape=jax.ShapeDtypeStruct((S, D_MODEL), MXU_DTYPE),
        compiler_params=_params(2),
    )(bcu, bcu, bcu, bcu, bcu, cw)


def conv_bwd(dz, bcu, cw):
    S = bcu.shape[0]
    tr, tc = min(TROW, S), TCH
    nr = S // tr
    main, prev, nxt = _conv_specs(S, tr, tc)

    def body(dz_ref, dzn_ref, gb_ref, gbn_ref, gc_ref, u_ref, gch_ref, uh_ref, w_ref,
             dgb_ref, dgc_ref, du_ref, dw_ref):
        r = pl.program_id(1)
        gcv, uv = gc_ref[...], u_ref[...]
        gc = jnp.concatenate([gch_ref[...], gcv], axis=0)
        uu = jnp.concatenate([uh_ref[...], uv], axis=0)
        u2, s1, s2, u3 = _conv_taps(gc, uu, w_ref, r == 0)
        dzv = dz_ref[...]
        du3 = jnp.concatenate([dzv * gb_ref[...], dzn_ref[...] * gbn_ref[...]], axis=0)
        rows = lax.broadcasted_iota(jnp.int32, du3.shape, 0)
        du3 = jnp.where((rows >= tr) & (r == nr - 1), 0.0, du3)
        n1 = pltpu.roll(du3, tr + 8 - 1, 0)
        n2 = pltpu.roll(du3, tr + 8 - 2, 0)
        du2 = (w_ref[2:3, :] * du3 + w_ref[1:2, :] * n1 + w_ref[0:1, :] * n2)[:tr]
        dgb_ref[...] = (dzv * u3[8:]).astype(dgb_ref.dtype)
        dgc_ref[...] = (du2 * uv).astype(dgc_ref.dtype)
        du_ref[...] = (du2 * gcv).astype(du_ref.dtype)
        d3 = du3[:tr]
        taps = [jnp.sum(d3 * t[8:], axis=0, keepdims=True) for t in (s2, s1, u2)]

        @pl.when(r == 0)
        def _():
            for kk in range(3):
                dw_ref[kk:kk + 1, :] = taps[kk]

        @pl.when(r != 0)
        def _():
            for kk in range(3):
                dw_ref[kk:kk + 1, :] += taps[kk]

    out = pl.BlockSpec((tr, tc), lambda c, r: (r, c))
    nxt_dz = pl.BlockSpec((8, tc), lambda c, r: (jnp.minimum((r + 1) * (tr // 8), S // 8 - 1), c))
    act = jax.ShapeDtypeStruct((S, D_MODEL), MXU_DTYPE)
    return pl.pallas_call(
        body, name="conv_bwd", grid=(D_MODEL // tc, nr),
        in_specs=[out, nxt_dz, main(0), nxt(0), main(1), main(2), prev(1), prev(2),
                  pl.BlockSpec((3, tc), lambda c, r: (0, c))],
        out_specs=[out, out, out, pl.BlockSpec((3, tc), lambda c, r: (0, c))],
        out_shape=[act, act, act, jax.ShapeDtypeStruct((3, D_MODEL), F32)],
        compiler_params=_params(2),
    )(dz, dz, bcu, bcu, bcu, bcu, bcu, bcu, cw)


def loss_head(y, target):
    S, D = y.shape
    tm = min(TM, S)

    def body(y_ref, t_ref, l_ref, dy_ref):
        err = y_ref[...] - t_ref[...]
        dy_ref[...] = err / D
        part = jnp.full((1, 128), jnp.sum(err * err), F32)

        @pl.when(pl.program_id(0) == 0)
        def _():
            l_ref[...] = part

        @pl.when(pl.program_id(0) != 0)
        def _():
            l_ref[...] += part

    blk = pl.BlockSpec((tm, D), lambda i: (i, 0))
    return pl.pallas_call(
        body, name="loss_head", grid=(S // tm,),
        in_specs=[blk, blk],
        out_specs=[pl.BlockSpec((1, 128), lambda i: (0, 0)), blk],
        out_shape=[jax.ShapeDtypeStruct((1, 128), F32), jax.ShapeDtypeStruct((S, D), F32)],
        compiler_params=_params(1),
    )(y, target)


def adamw(parts, w, m, v, *, name):
    R, C = w.shape
    tr = R
    while tr * C * 4 > (1 << 20) and tr % 32 == 0:
        tr //= 2

    def body(p_ref, w_ref, m_ref, v_ref, g_ref, d_ref, mo_ref, vo_ref):
        g = p_ref[0].astype(F32)
        for d in range(1, N_DEV):
            g = g + p_ref[d].astype(F32)
        m_new = ADAM_B1 * m_ref[...] + (1.0 - ADAM_B1) * g
        v_new = ADAM_B2 * v_ref[...] + (1.0 - ADAM_B2) * (g * g)
        m_hat = m_new / (1.0 - ADAM_B1 ** ADAM_STEP)
        v_hat = v_new / (1.0 - ADAM_B2 ** ADAM_STEP)
        g_ref[...] = g
        d_ref[...] = -ADAM_LR * (m_hat / (jnp.sqrt(v_hat) + ADAM_EPS) + ADAM_WD * w_ref[...])
        mo_ref[...] = m_new
        vo_ref[...] = v_new

    blk = pl.BlockSpec((tr, C), lambda i: (i, 0))
    return pl.pallas_call(
        body, name=name, grid=(R // tr,),
        in_specs=[pl.BlockSpec((N_DEV, tr, C), lambda i: (0, i, 0)), blk, blk, blk],
        out_specs=[blk, blk, blk, blk],
        out_shape=[jax.ShapeDtypeStruct((R, C), F32)] * 4,
        compiler_params=_params(1),
    )(parts, w, m, v)


def _mesh_place():
    x, y, c = (lax.axis_index(n) for n in MESH_AXES)
    return x, y, c, 4 * x + 2 * y + c


def _peer(x, y, c, d):
    px = 1 - x if d & 4 else x
    py = 1 - y if d & 2 else y
    pc = 1 - c if d & 1 else c
    return (px, py, pc), 4 * px + 2 * py + pc


def _exchange(srcs, name, scatter):
    n = len(srcs)
    any_spec = pl.BlockSpec(memory_space=pl.ANY)

    def body(*refs):
        ins, outs, token = refs[:n], refs[n:2 * n], refs[2 * n]
        send_sems, recv_sems, local_sems = refs[2 * n + 1:]
        token[...] = jnp.zeros_like(token)
        x, y, c, me = _mesh_place()
        for a in range(n):
            mine = ins[a].at[me] if scatter else ins[a]
            pltpu.make_async_copy(mine, outs[a].at[me], local_sems.at[a]).start()
            for d in range(1, N_DEV):
                peer, peer_lin = _peer(x, y, c, d)
                pltpu.make_async_remote_copy(
                    src_ref=ins[a].at[peer_lin] if scatter else ins[a], dst_ref=outs[a].at[me],
                    send_sem=send_sems.at[a], recv_sem=recv_sems.at[a],
                    device_id=peer, device_id_type=pl.DeviceIdType.MESH).start()
        for a in range(n):
            mine = ins[a].at[me] if scatter else ins[a]
            pltpu.make_async_copy(mine, outs[a].at[me], local_sems.at[a]).wait()
            seven = outs[a].at[pl.ds(0, N_DEV - 1)]
            drain = pltpu.make_async_remote_copy(
                src_ref=seven, dst_ref=seven, send_sem=send_sems.at[a], recv_sem=recv_sems.at[a],
                device_id=(x, y, c), device_id_type=pl.DeviceIdType.MESH)
            drain.wait_send()
            drain.wait_recv()

    block = (lambda s: s.shape[1:]) if scatter else (lambda s: s.shape)
    out = pl.pallas_call(
        body, name=name,
        in_specs=[any_spec] * n, out_specs=[any_spec] * n + [pl.BlockSpec(memory_space=pltpu.VMEM)],
        out_shape=[jax.ShapeDtypeStruct((N_DEV,) + tuple(block(s)), s.dtype) for s in srcs]
        + [jax.ShapeDtypeStruct((8, 128), F32)],
        scratch_shapes=[pltpu.SemaphoreType.DMA((n,)), pltpu.SemaphoreType.DMA((n,)), pltpu.SemaphoreType.DMA((n,))],
    )(*srcs)
    return out[:n], out[n]


_ANY = pl.BlockSpec(memory_space=pl.ANY)
_HBM = pl.BlockSpec(memory_space=pltpu.HBM)
_SEM = pl.BlockSpec(memory_space=pltpu.SEMAPHORE)


def _in_hbm(arrays):
    return [pltpu.with_memory_space_constraint(a, pltpu.HBM) for a in arrays]


def place_own(shards, after, *, name):
    n = len(shards)

    def body(*refs):
        ins, outs, sems = refs[:n], refs[n + 1:2 * n + 1], refs[2 * n + 1]
        me = _mesh_place()[3]
        copies = [pltpu.make_async_copy(ins[a], outs[a].at[me], sems.at[a]) for a in range(n)]
        for cp in copies:
            cp.start()
        for cp in copies:
            cp.wait()

    return pl.pallas_call(
        body, name=name, in_specs=[_ANY] * (n + 1), out_specs=[_ANY] * n,
        out_shape=[jax.ShapeDtypeStruct((N_DEV,) + s.shape, s.dtype) for s in shards],
        scratch_shapes=[pltpu.SemaphoreType.DMA((n,))],
    )(*shards, after)


def exchange_start(srcs, lands, slots, *, name):
    n, m = len(srcs), len(lands)

    def body(*refs):
        ins, zones = refs[:n], refs[n:n + m]
        send_sems, recv_sems, token = refs[n + m], refs[n + m + 1], refs[-1]
        x, y, c, me = _mesh_place()
        for a in range(n):
            for d in range(1, N_DEV):
                peer, peer_lin = _peer(x, y, c, d)
                src = ins[a] if slots is None else ins[a].at[peer_lin]
                dst = zones[a].at[me] if slots is None else zones[slots[a][0]].at[me, slots[a][1]]
                pltpu.make_async_remote_copy(
                    src_ref=src, dst_ref=dst, send_sem=send_sems.at[a], recv_sem=recv_sems.at[a],
                    device_id=peer, device_id_type=pl.DeviceIdType.MESH).start()
        token[...] = jnp.zeros_like(token)

    both = list(srcs) + list(lands)
    out = pl.pallas_call(
        body, name=name,
        in_specs=[_HBM] * (n + m),
        out_specs=[_SEM, _SEM] + [_HBM] * (n + m) + [pl.BlockSpec(memory_space=pltpu.VMEM)],
        out_shape=[pltpu.SemaphoreType.DMA((n,)), pltpu.SemaphoreType.DMA((n,))]
        + [pltpu.HBM(a.shape, a.dtype) for a in both] + [jax.ShapeDtypeStruct((8, 128), F32)],
        input_output_aliases={i: 2 + i for i in range(n + m)},
        compiler_params=pltpu.CompilerParams(has_side_effects=pltpu.SideEffectType.DATAFLOW_SIDE_EFFECTING),
    )(*_in_hbm(both))
    return out[0], out[1], out[2:2 + n], out[2 + n:2 + n + m], out[-1]


def exchange_wait(send_sems, recv_sems, srcs, lands, slots, after, *, name):
    n, m = len(srcs), len(lands)

    def body(*refs):
        ins, zones = refs[:n], refs[n:n + m]
        send_ref, recv_ref = refs[n + m], refs[n + m + 1]
        x, y, c, _ = _mesh_place()
        for a in range(n):
            seven = (zones[a] if slots is None else ins[a]).at[pl.ds(0, N_DEV - 1)]
            drain = pltpu.make_async_remote_copy(
                src_ref=seven, dst_ref=seven, send_sem=send_ref.at[a], recv_sem=recv_ref.at[a],
                device_id=(x, y, c), device_id_type=pl.DeviceIdType.MESH)
            drain.wait_send()
            drain.wait_recv()

    both = list(srcs) + list(lands)
    out = pl.pallas_call(
        body, name=name,
        in_specs=[_HBM] * (n + m) + [_SEM, _SEM, _ANY],
        out_specs=[_HBM] * (n + m),
        out_shape=[pltpu.HBM(a.shape, a.dtype) for a in both],
        input_output_aliases={i: i for i in range(n + m)},
        compiler_params=pltpu.CompilerParams(has_side_effects=pltpu.SideEffectType.DATAFLOW_SIDE_EFFECTING),
    )(*both, send_sems, recv_sems, after)
    return out[:n], out[n:]


def scatter_finish(remote, local, lands, vec, *, name):
    every = list(remote) + list(local)
    n, r, m = len(every), len(remote), len(lands)

    def body(*refs):
        ins, vec_ref, zones_in = refs[:n], refs[n], refs[n + 1:n + 1 + m]
        vec_out = refs[n + 1 + 2 * m]
        send_sems, recv_sems, local_sems = refs[n + 2 + 2 * m:]
        x, y, c, me = _mesh_place()

        def own(a):
            if a == n:
                return pltpu.make_async_copy(vec_ref, vec_out.at[me], local_sems.at[a])
            return pltpu.make_async_copy(ins[a].at[me], zones_in[every[a][1]].at[me, every[a][2]], local_sems.at[a])

        for a in range(n + 1):
            own(a).start()
        for a in list(range(r)) + [n]:
            for d in range(1, N_DEV):
                peer, peer_lin = _peer(x, y, c, d)
                src = vec_ref if a == n else ins[a].at[peer_lin]
                dst = vec_out.at[me] if a == n else zones_in[every[a][1]].at[me, every[a][2]]
                pltpu.make_async_remote_copy(
                    src_ref=src, dst_ref=dst, send_sem=send_sems.at[min(a, r)], recv_sem=recv_sems.at[min(a, r)],
                    device_id=peer, device_id_type=pl.DeviceIdType.MESH).start()
        for a in range(n + 1):
            own(a).wait()
        for a in list(range(r)) + [n]:
            seven = (vec_out if a == n else ins[a]).at[pl.ds(0, N_DEV - 1)]
            drain = pltpu.make_async_remote_copy(
                src_ref=seven, dst_ref=seven, send_sem=send_sems.at[min(a, r)], recv_sem=recv_sems.at[min(a, r)],
                device_id=(x, y, c), device_id_type=pl.DeviceIdType.MESH)
            drain.wait_send()
            drain.wait_recv()

    out = pl.pallas_call(
        body, name=name,
        in_specs=[_ANY] * (n + 1 + m), out_specs=[_ANY] * (m + 1),
        out_shape=[jax.ShapeDtypeStruct(z.shape, z.dtype) for z in lands]
        + [jax.ShapeDtypeStruct((N_DEV,) + vec.shape, vec.dtype)],
        input_output_aliases={n + 1 + i: i for i in range(m)},
        scratch_shapes=[pltpu.SemaphoreType.DMA((r + 1,)), pltpu.SemaphoreType.DMA((r + 1,)),
                        pltpu.SemaphoreType.DMA((n + 1,))],
    )(*[e[0] for e in every], vec, *lands)
    return out[:m], out[m]


def _rope_tables(pos):
    inv_freq = ROPE_THETA ** (-jnp.arange(0, ROPE, 2, dtype=F32) / ROPE)
    ang = pos.astype(F32)[:, None] * inv_freq
    cos, sin = jnp.cos(ang), jnp.sin(ang)
    pad = jnp.zeros((pos.shape[0], 128 - ROPE), F32)
    return jnp.concatenate([cos, cos, pad + 1.0], axis=1), jnp.concatenate([-sin, sin, pad], axis=1)


def _pad_last(w, n):
    return jnp.pad(w, [(0, 0)] * (w.ndim - 1) + [(0, n - w.shape[-1])])


def kernel(x, positions, g_mix, g_mlp, attn_w_down, attn_g_q_a, attn_g_kv_a, attn_w_uq, attn_w_ukv, attn_g_qnorm, attn_g_knorm, attn_w_o, conv_w_in, conv_w, conv_w_out, mlp_w1, mlp_w2, loss_target, m_g_mix, m_g_mlp, m_attn_w_down, m_attn_g_q_a, m_attn_g_kv_a, m_attn_w_uq, m_attn_w_ukv, m_attn_g_qnorm, m_attn_g_knorm, m_attn_w_o, m_conv_w_in, m_conv_w, m_conv_w_out, m_mlp_w1, m_mlp_w2, v_g_mix, v_g_mlp, v_attn_w_down, v_attn_g_q_a, v_attn_g_kv_a, v_attn_w_uq, v_attn_w_ukv, v_attn_g_qnorm, v_attn_g_knorm, v_attn_w_o, v_conv_w_in, v_conv_w, v_conv_w_out, v_mlp_w1, v_mlp_w2):
    weights = dict(g_mix=g_mix, g_mlp=g_mlp, attn_w_down=attn_w_down, attn_g_q_a=attn_g_q_a, attn_g_kv_a=attn_g_kv_a,
                   attn_w_uq=attn_w_uq, attn_w_ukv=attn_w_ukv, attn_g_qnorm=attn_g_qnorm, attn_g_knorm=attn_g_knorm,
                   attn_w_o=attn_w_o, conv_w_in=conv_w_in, conv_w=conv_w, conv_w_out=conv_w_out, mlp_w1=mlp_w1, mlp_w2=mlp_w2)
    mom1 = dict(g_mix=m_g_mix, g_mlp=m_g_mlp, attn_w_down=m_attn_w_down, attn_g_q_a=m_attn_g_q_a, attn_g_kv_a=m_attn_g_kv_a,
                attn_w_uq=m_attn_w_uq, attn_w_ukv=m_attn_w_ukv, attn_g_qnorm=m_attn_g_qnorm, attn_g_knorm=m_attn_g_knorm,
                attn_w_o=m_attn_w_o, conv_w_in=m_conv_w_in, conv_w=m_conv_w, conv_w_out=m_conv_w_out, mlp_w1=m_mlp_w1, mlp_w2=m_mlp_w2)
    mom2 = dict(g_mix=v_g_mix, g_mlp=v_g_mlp, attn_w_down=v_attn_w_down, attn_g_q_a=v_attn_g_q_a, attn_g_kv_a=v_attn_g_kv_a,
                attn_w_uq=v_attn_w_uq, attn_w_ukv=v_attn_w_ukv, attn_g_qnorm=v_attn_g_qnorm, attn_g_knorm=v_attn_g_knorm,
                attn_w_o=v_attn_w_o, conv_w_in=v_conv_w_in, conv_w=v_conv_w, conv_w_out=v_conv_w_out, mlp_w1=v_mlp_w1, mlp_w2=v_mlp_w2)
    big = ["attn_w_down", "attn_w_uq", "attn_w_ukv", "attn_w_o", "conv_w_in", "conv_w", "conv_w_out", "mlp_w1", "mlp_w2"]
    small = ["g_mix", "g_mlp", "attn_g_q_a", "attn_g_kv_a", "attn_g_qnorm", "attn_g_knorm"]
    order = ["g_mix", "g_mlp", "attn_w_down", "attn_g_q_a", "attn_g_kv_a", "attn_w_uq", "attn_w_ukv", "attn_g_qnorm",
             "attn_g_knorm", "attn_w_o", "conv_w_in", "conv_w", "conv_w_out", "mlp_w1", "mlp_w2"]

    xs = x[0]
    pos = positions[0]
    target = loss_target[0]
    S = xs.shape[0]
    depth = g_mix.shape[0]
    cos_t, sin_t = _rope_tables(pos)
    pos_col, pos_row = pos.reshape(S, 1), pos.reshape(1, S)

    keys, shards = [], []
    for name in big:
        for l in range(weights[name].shape[0]):
            keys.append((name, l))
            shards.append(weights[name][l] if name == "conv_w" else weights[name][l].astype(WIRE_DTYPE))
    first = [j for j, (name, l) in enumerate(keys) if l == 0 and name.startswith("attn")]
    second = [j for j, (name, l) in enumerate(keys) if l == 0 and name.startswith("mlp")]
    later = [j for j in range(len(keys)) if j not in first + second]
    me = 4 * lax.axis_index("x") + 2 * lax.axis_index("y") + lax.axis_index("c")

    def zones_with_own(js, token):
        return [lax.dynamic_update_slice(lax.empty((N_DEV,) + shards[j].shape, shards[j].dtype),
                                         (shards[j] + token[0, 0].astype(shards[j].dtype))[None],
                                         (me,) + (0,) * shards[j].ndim) for j in js]

    arrived, token = _exchange([shards[j] for j in first], "gather_first", scatter=False)
    full = dict(zip([keys[j] for j in first], arrived))
    g1 = exchange_start([shards[j] for j in second], zones_with_own(second, token), None, name="gather_mlp0_start")
    g2 = exchange_start([shards[j] for j in later], zones_with_own(later, g1[4]), None, name="gather_rest_start")
    g_mix_0 = g_mix[0] + g2[4][0, 0]

    def rows(name, l):
        g = full[(name, l)]
        return g.reshape(g.shape[0] * g.shape[1], g.shape[2])

    saved = []
    for i in range(depth):
        l = i // 2
        rec = {"x0": xs}
        if i == 1:
            arrived = exchange_wait(*g2[:4], None, xs, name="gather_rest_wait")[1]
            full.update(zip([keys[j] for j in later], arrived))
        if i % 2 == 0:
            wd3 = _pad_last(rows("attn_w_down", l), DOWN_PAD)[None]
            wuq3 = _pad_last(full[("attn_w_uq", l)], QK_PAD)
            wukv3 = full[("attn_w_ukv", l)]
            gqn = _pad_last(attn_g_qnorm[l][None], QK_PAD)
            gkn = _pad_last(attn_g_knorm[l][None], QK_PAD)
            gqa, gkva = attn_g_q_a[l][None], attn_g_kv_a[l][None]
            h, a = norm_matmul(xs, g_mix_0 if i == 0 else g_mix[i], wd3, name="attn_down")
            q, k, v, cq, ckv = mla_pre_fwd(a, gqa, gkva, wuq3, wukv3, gqn, gkn, cos_t, sin_t)
            o, lse = attn_fwd(q, k, v, pos_col, pos_row)
            x1 = matmul_residual(o, rows("attn_w_o", l), xs, name="attn_out")
            rec.update(h=h, a=a, q=q, k=k, v=v, cq=cq, ckv=ckv, o=o, lse=lse, wd3=wd3, wuq3=wuq3, wukv3=wukv3,
                       gqn=gqn, gkn=gkn, gqa=gqa, gkva=gkva)
        else:
            cw = full[("conv_w", l)].transpose(1, 0, 2).reshape(3, D_MODEL)
            h, bcu = norm_matmul(xs, g_mix[i], full[("conv_w_in", l)], name="conv_in")
            z = conv_fwd(bcu, cw)
            x1 = matmul_residual(z, rows("conv_w_out", l), xs, name="conv_out")
            rec.update(h=h, bcu=bcu, z=z, cw=cw)
        if i == 0:
            arrived = exchange_wait(*g1[:4], None, x1, name="gather_mlp0_wait")[1]
            full.update(zip([keys[j] for j in second], arrived))
        h2, act = norm_matmul(x1, g_mlp[i], full[("mlp_w1", i)], name="mlp_up", mlp=True)
        xs = matmul_residual(act, rows("mlp_w2", i), x1, name="mlp_down")
        rec.update(x1=x1, h2=h2, act=act)
        saved.append(rec)

    sq, dx = loss_head(xs, target)
    loss = lax.psum(sq[0, 0] * (0.5 / D_MODEL), MESH_AXES)

    grads = {name: [None] * weights[name].shape[0] for name in order}
    token = None
    for i in reversed(range(depth)):
        l = i // 2
        rec = saved[i]
        grads["mlp_w2"][i] = mm_tn(rec["act"], dx, name="mlp_down_dw", G=1, out_dtype=WIRE_DTYPE).reshape(N_DEV, -1, D_MODEL)
        du = matmul_nt(dx, rows("mlp_w2", i)[None], name="mlp_down_dx", epi="mlp_du", u=rec["act"])
        grads["mlp_w1"][i] = mm_tn(rec["h2"], du, name="mlp_up_dw", G=N_DEV, out_dtype=WIRE_DTYPE)
        dx1, dg = matmul_nt(du, full[("mlp_w1", i)], name="mlp_up_dx", epi="rms_bwd", x=rec["x1"], g=g_mlp[i], dx=dx)
        grads["g_mlp"][i] = dg[0]
        if i == 0:
            flying = [keys[j] for j in second + later]
            srcs = [grads[name][l_] for name, l_ in flying]
            slots = [(big.index(name), l_) for name, l_ in flying]
            zones = [lax.empty((N_DEV, weights[name].shape[0]) + grads[name][-1].shape[1:], grads[name][-1].dtype)
                     for name in big]
            for src, (k, l_) in zip(srcs, slots):
                own = lax.dynamic_index_in_dim(src, me, 0, keepdims=True)[None]
                zones[k] = lax.dynamic_update_slice(zones[k], own, (me, l_) + (0,) * (src.ndim - 1))
            s_send, s_recv, s_srcs, s_zones, token = exchange_start(srcs, zones, slots, name="scatter_rest_start")
        if i % 2 == 0:
            grads["attn_w_o"][l] = mm_tn(rec["o"], dx1, name="attn_out_dw", G=1, out_dtype=WIRE_DTYPE,
                                         after=token).reshape(N_DEV, -1, D_MODEL)
            do = matmul_nt(dx1, rows("attn_w_o", l)[None], name="attn_out_dx")
            dq, dk, dv = attn_bwd(rec["q"], rec["k"], rec["v"], do, rec["lse"], attn_delta(rec["o"], do), pos_col, pos_row)
            da, dwuq, dwukv, dgqn, dgkn, dgqa, dgkva = mla_pre_bwd(
                dq, dk, dv, rec["a"], rec["cq"], rec["ckv"], rec["gqa"], rec["gkva"], rec["wuq3"], rec["wukv3"],
                rec["gqn"], rec["gkn"], cos_t, sin_t)
            grads["attn_w_uq"][l] = dwuq[:, :, :QK_DIM].astype(WIRE_DTYPE)
            grads["attn_w_ukv"][l] = dwukv.astype(WIRE_DTYPE)
            grads["attn_g_qnorm"][l] = dgqn[0, :QK_DIM]
            grads["attn_g_knorm"][l] = dgkn[0, :QK_DIM]
            grads["attn_g_q_a"][l] = dgqa[0]
            grads["attn_g_kv_a"][l] = dgkva[0]
            dwd = mm_tn(rec["h"], da, name="attn_down_dw", G=1, out_dtype=WIRE_DTYPE)
            grads["attn_w_down"][l] = dwd[0, :, :DOWN].reshape(N_DEV, -1, DOWN)
            dx, dg = matmul_nt(da, rec["wd3"], name="attn_down_dx", epi="rms_bwd", x=rec["x0"], g=g_mix[i], dx=dx1)
        else:
            grads["conv_w_out"][l] = mm_tn(rec["z"], dx1, name="conv_out_dw", G=1, out_dtype=WIRE_DTYPE).reshape(N_DEV, -1, D_MODEL)
            dz = matmul_nt(dx1, rows("conv_w_out", l)[None], name="conv_out_dx")
            dgb, dgc, du_, dcw = conv_bwd(dz, rec["bcu"], rec["cw"])
            grads["conv_w"][l] = dcw.reshape(3, N_DEV, -1).transpose(1, 0, 2)
            dbcu = jnp.concatenate([dgb, dgc, du_], axis=1)
            grads["conv_w_in"][l] = mm_tn(rec["h"], dbcu, name="conv_in_dw", G=N_DEV, out_dtype=WIRE_DTYPE)
            dx, dg = matmul_nt(dbcu, full[("conv_w_in", l)], name="conv_in_dx", epi="rms_bwd", x=rec["x0"], g=g_mix[i], dx=dx1)
        grads["g_mix"][i] = dg[0]

    sizes = [weights[name].size for name in small]
    n_small = sum(sizes)
    rows_small = -(-n_small // (8 * 128)) * 8

    def pack(tree):
        flat = jnp.concatenate([jnp.stack(tree[name]).reshape(-1) if isinstance(tree[name], list) else tree[name].reshape(-1)
                                for name in small])
        return jnp.pad(flat, (0, rows_small * 128 - n_small)).reshape(rows_small, 128)

    s_srcs, s_zones = exchange_wait(s_send, s_recv, s_srcs, s_zones, slots, dx, name="scatter_rest_wait")
    remote = [(grads[name][l_], big.index(name), l_) for name, l_ in (keys[j] for j in first)]
    parts, gain_parts = scatter_finish(remote, [], s_zones, pack(grads), name="scatter_last")

    out = {}
    for name, part in zip(big, parts):
        w = weights[name]
        flat = lambda t: t.reshape(-1, t.shape[-1])
        res = adamw(part.reshape(N_DEV, -1, w.shape[-1]), flat(w), flat(mom1[name]), flat(mom2[name]), name="adamw_" + name)
        out[name] = [r.reshape(w.shape) for r in res]
    res = adamw(gain_parts, pack(weights), pack(mom1), pack(mom2), name="adamw_gains")
    offset = 0
    for name, size in zip(small, sizes):
        out[name] = [r.reshape(-1)[offset:offset + size].reshape(weights[name].shape) for r in res]
        offset += size

    return (loss, dx[None], *[out[n][0] for n in order], *[out[n][1] for n in order],
            *[out[n][2] for n in order], *[out[n][3] for n in order])
```

```python
import jax
import jax.numpy as jnp
import numpy as np
from jax import lax
from jax.experimental import pallas as pl
from jax.experimental.pallas import tpu as pltpu

F32 = jnp.float32
MXU_DTYPE = jnp.bfloat16
WIRE_DTYPE = jnp.bfloat16

D_MODEL = 1024
N_HEADS = 8
NOPE = 128
ROPE = 64
QK_DIM = NOPE + ROPE
QK_PAD = 256
V_DIM = 128
Q_LORA = 256
KV_LORA = 128
DOWN = Q_LORA + KV_LORA + ROPE
DOWN_PAD = 512
ROPE_THETA = 10000.0
EPS = 1e-6
SM_SCALE = QK_DIM ** -0.5
LOG2E = 1.4426950408889634
Q_PRESCALE = SM_SCALE * LOG2E
ADAM_LR, ADAM_B1, ADAM_B2, ADAM_EPS, ADAM_WD, ADAM_STEP = 0.001, 0.9, 0.999, 1e-08, 0.01, 10
N_DEV = 8
MESH_AXES = ("x", "y", "c")

TM = 512
TM_WIDE = 1024
TM_TOKENS_TN = 2048
TQ = 512
HEADS_FWD = 8
TQ_BWD = 2048
BWD_CHUNK = 256
TROW = 512
TCH = 512
VMEM_LIMIT = 48 << 20

NN = (((1,), (0,)), ((), ()))
NT = (((1,), (1,)), ((), ()))
TN = (((0,), (0,)), ((), ()))


def _dot(a, b, dims=NN):
    return lax.dot_general(a.astype(MXU_DTYPE), b.astype(MXU_DTYPE), dims, preferred_element_type=F32)


def _params(n_axes):
    return pltpu.CompilerParams(dimension_semantics=("arbitrary",) * n_axes, vmem_limit_bytes=VMEM_LIMIT)


def _rms(xv, n):
    r = lax.rsqrt(jnp.sum(xv * xv, axis=-1, keepdims=True) / n + EPS)
    return xv * r, r


def _rms_bwd(dy, xhat, r, g, n):
    dg = jnp.sum(dy * xhat, axis=0, keepdims=True)
    dxh = dy * g
    dx = r * (dxh - xhat * (jnp.sum(dxh * xhat, axis=-1, keepdims=True) / n))
    return dx, dg


def _swap_halves(t):
    lane = lax.broadcasted_iota(jnp.int32, t.shape, 1)
    return jnp.where(lane < ROPE // 2, pltpu.roll(t, 128 - ROPE // 2, 1), pltpu.roll(t, ROPE // 2, 1))


def _rope(t, cos_t, sin_t):
    return t * cos_t + _swap_halves(t) * sin_t


def _rope_bwd(dout, cos_t, sin_t):
    return dout * cos_t + _swap_halves(dout * sin_t)


def rms_mm(x, g, w3, *, name, mlp=False):
    S, D = x.shape
    G, _, Nb = w3.shape
    tm = min(TM_WIDE, S)
    tn = Nb if Nb <= 512 else 512
    nj = Nb // tn
    N = G * Nb

    def body(x_ref, g_ref, w_ref, h_ref, *rest):
        hs = rest[-1]

        @pl.when(pl.program_id(1) == 0)
        def _():
            xv = x_ref[...]
            r = lax.rsqrt(jnp.mean(xv * xv, axis=-1, keepdims=True) + EPS)
            h = (xv * r * g_ref[...]).astype(hs.dtype)
            hs[...] = h
            h_ref[...] = h

        acc = lax.dot_general(hs[...], w_ref[...].astype(hs.dtype), NN, preferred_element_type=F32)
        if mlp:
            rl = jnp.maximum(acc, 0.0)
            rest[0][...] = (rl * rl).astype(rest[0].dtype)
        else:
            rest[0][...] = acc

    out_shape = [jax.ShapeDtypeStruct((S, D), MXU_DTYPE), jax.ShapeDtypeStruct((S, N), MXU_DTYPE if mlp else F32)]
    out_specs = [pl.BlockSpec((tm, D), lambda i, j: (i, 0)), pl.BlockSpec((tm, tn), lambda i, j: (i, j))]
    return pl.pallas_call(
        body, name=name, grid=(S // tm, G * nj),
        in_specs=[pl.BlockSpec((tm, D), lambda i, j: (i, 0)),
                  pl.BlockSpec((1, D), lambda i, j: (0, 0)),
                  pl.BlockSpec((None, D, tn), lambda i, j: (j // nj, 0, j % nj))],
        out_specs=out_specs, out_shape=out_shape,
        scratch_shapes=[pltpu.VMEM((tm, D), MXU_DTYPE)],
        compiler_params=_params(2),
    )(x, g.reshape(1, D), w3)


def mm_res(a, w, res, *, name):
    S, K = a.shape
    _, N = w.shape
    tm = min(TM_WIDE, S)
    tk = min(K, 1024)
    nk = K // tk

    def body(a_ref, w_ref, r_ref, o_ref, acc):
        k = pl.program_id(1)

        @pl.when(k == 0)
        def _():
            acc[...] = jnp.zeros_like(acc)

        acc[...] += _dot(a_ref[...], w_ref[...])

        @pl.when(k == nk - 1)
        def _():
            o_ref[...] = r_ref[...] + acc[...]

    return pl.pallas_call(
        body, name=name, grid=(S // tm, nk),
        in_specs=[pl.BlockSpec((tm, tk), lambda i, k: (i, k)),
                  pl.BlockSpec((tk, N), lambda i, k: (k, 0)),
                  pl.BlockSpec((tm, N), lambda i, k: (i, 0))],
        out_specs=pl.BlockSpec((tm, N), lambda i, k: (i, 0)),
        out_shape=jax.ShapeDtypeStruct((S, N), F32),
        scratch_shapes=[pltpu.VMEM((tm, N), F32)],
        compiler_params=_params(2),
    )(a, w, res)


def mm_nt(a, w3, *, name, epi="plain", u=None, x=None, g=None, dx=None):
    S, N = a.shape
    G, Ko, Nb = w3.shape
    assert N == G * Nb
    tm = min(TM if epi == "rms_bwd" else TM_WIDE, S)
    tkk = Nb if Nb <= 1024 else 1024
    nk = Nb // tkk
    ks = G * nk
    tko = Ko if epi == "rms_bwd" else min(Ko, 512)

    def body(a_ref, w_ref, *rest):
        acc = rest[-1]
        i, k = pl.program_id(0), pl.program_id(2)

        @pl.when(k == 0)
        def _():
            acc[...] = jnp.zeros_like(acc)

        acc[...] += _dot(a_ref[...], w_ref[...], NT)

        @pl.when(k == ks - 1)
        def _():
            if epi == "plain":
                rest[0][...] = acc[...]
            elif epi == "mlp_du":
                u_ref, o_ref = rest[0], rest[1]
                o_ref[...] = (acc[...] * (2.0 * jnp.sqrt(u_ref[...].astype(F32)))).astype(o_ref.dtype)
            else:
                x_ref, g_ref, dx_ref, o_ref, dg_ref = rest[:5]
                xhat, r = _rms(x_ref[...], Ko)
                dxb, dg = _rms_bwd(acc[...], xhat, r, g_ref[...], Ko)
                o_ref[...] = dx_ref[...] + dxb

                @pl.when(i == 0)
                def _():
                    dg_ref[...] = dg

                @pl.when(i != 0)
                def _():
                    dg_ref[...] += dg

    in_specs = [pl.BlockSpec((tm, tkk), lambda i, j, k: (i, k)),
                pl.BlockSpec((None, tko, tkk), lambda i, j, k: (k // nk, j, k % nk))]
    args = [a, w3]
    tile = pl.BlockSpec((tm, tko), lambda i, j, k: (i, j))
    if epi == "plain":
        out_shape, out_specs = jax.ShapeDtypeStruct((S, Ko), F32), tile
    elif epi == "mlp_du":
        in_specs.append(tile)
        args.append(u)
        out_shape, out_specs = jax.ShapeDtypeStruct((S, Ko), MXU_DTYPE), tile
    else:
        vec = pl.BlockSpec((1, Ko), lambda i, j, k: (0, 0))
        in_specs += [tile, vec, tile]
        args += [x, g.reshape(1, Ko), dx]
        out_shape = [jax.ShapeDtypeStruct((S, Ko), F32), jax.ShapeDtypeStruct((1, Ko), F32)]
        out_specs = [tile, vec]
    return pl.pallas_call(
        body, name=name, grid=(S // tm, Ko // tko, ks),
        in_specs=in_specs, out_specs=out_specs, out_shape=out_shape,
        scratch_shapes=[pltpu.VMEM((tm, tko), F32)],
        compiler_params=_params(3),
    )(*args)


def _resident(shape):
    return pl.BlockSpec(shape, lambda i: (0,) * len(shape))


def norm_matmul(x, g, w3, *, name, mlp=False):
    S, D = x.shape
    G, _, Nb = w3.shape
    tm = min(TM, S)
    N = G * Nb

    def body(x_ref, g_ref, w_ref, h_ref, o_ref):
        xv = x_ref[...]
        r = lax.rsqrt(jnp.mean(xv * xv, axis=-1, keepdims=True) + EPS)
        h = (xv * r * g_ref[...]).astype(h_ref.dtype)
        h_ref[...] = h
        for gi in range(G):
            acc = _dot(h, w_ref[gi])
            if mlp:
                acc = jnp.square(jnp.maximum(acc, 0.0))
            o_ref[:, gi * Nb:(gi + 1) * Nb] = acc.astype(o_ref.dtype)

    rows = lambda w: pl.BlockSpec((tm, w), lambda i: (i, 0))
    return pl.pallas_call(
        body, name=name, grid=(S // tm,),
        in_specs=[rows(D), _resident((1, D)), _resident((G, D, Nb))],
        out_specs=[rows(D), rows(N)],
        out_shape=[jax.ShapeDtypeStruct((S, D), MXU_DTYPE), jax.ShapeDtypeStruct((S, N), MXU_DTYPE if mlp else F32)],
        compiler_params=_params(1),
    )(x, g.reshape(1, D), w3)


def matmul_residual(a, w, res, *, name):
    S, K = a.shape
    _, N = w.shape
    tm = min(TM, S)

    def body(a_ref, w_ref, r_ref, o_ref):
        o_ref[...] = r_ref[...] + _dot(a_ref[...], w_ref[...])

    rows = lambda w_: pl.BlockSpec((tm, w_), lambda i: (i, 0))
    return pl.pallas_call(
        body, name=name, grid=(S // tm,),
        in_specs=[rows(K), _resident((K, N)), rows(N)],
        out_specs=rows(N), out_shape=jax.ShapeDtypeStruct((S, N), F32),
        compiler_params=_params(1),
    )(a, w, res)


def matmul_nt(a, w3, *, name, epi="plain", u=None, x=None, g=None, dx=None):
    S, N = a.shape
    G, Ko, Nb = w3.shape
    assert N == G * Nb
    tm = min(TM, S)
    tko = min(Ko, 512)

    def body(a_ref, w_ref, *rest):
        if epi == "mlp_du":
            u_ref, o_ref = rest
            av = a_ref[...].astype(MXU_DTYPE)
            for j in range(Ko // tko):
                cols = slice(j * tko, (j + 1) * tko)
                da = _dot(av, w_ref[0, cols, :], NT)
                o_ref[:, cols] = (da * (2.0 * jnp.sqrt(u_ref[:, cols].astype(F32)))).astype(o_ref.dtype)
            return
        acc = _dot(a_ref[:, :Nb], w_ref[0], NT)
        for gi in range(1, G):
            acc = acc + _dot(a_ref[:, gi * Nb:(gi + 1) * Nb], w_ref[gi], NT)
        if epi == "plain":
            rest[0][...] = acc
        else:
            x_ref, g_ref, dx_ref, o_ref, dg_ref = rest
            xhat, r = _rms(x_ref[...], Ko)
            dxb, dg = _rms_bwd(acc, xhat, r, g_ref[...], Ko)
            o_ref[...] = dx_ref[...] + dxb

            @pl.when(pl.program_id(0) == 0)
            def _():
                dg_ref[...] = dg

            @pl.when(pl.program_id(0) != 0)
            def _():
                dg_ref[...] += dg

    rows = lambda w_: pl.BlockSpec((tm, w_), lambda i: (i, 0))
    in_specs = [rows(N), _resident((G, Ko, Nb))]
    args = [a, w3]
    if epi == "plain":
        out_shape, out_specs = jax.ShapeDtypeStruct((S, Ko), F32), rows(Ko)
    elif epi == "mlp_du":
        in_specs.append(rows(Ko))
        args.append(u)
        out_shape, out_specs = jax.ShapeDtypeStruct((S, Ko), MXU_DTYPE), rows(Ko)
    else:
        in_specs += [rows(Ko), _resident((1, Ko)), rows(Ko)]
        args += [x, g.reshape(1, Ko), dx]
        out_shape = [jax.ShapeDtypeStruct((S, Ko), F32), jax.ShapeDtypeStruct((1, Ko), F32)]
        out_specs = [rows(Ko), _resident((1, Ko))]
    return pl.pallas_call(
        body, name=name, grid=(S // tm,),
        in_specs=in_specs, out_specs=out_specs, out_shape=out_shape,
        compiler_params=_params(1),
    )(*args)


def mm_tn(a, b, *, name, G, out_dtype, after=None):
    order = [] if after is None else [after]
    S, Ka = a.shape
    _, N = b.shape
    Nb = N // G
    tm = min(TM_TOKENS_TN if b.dtype.itemsize == 2 else TM_TOKENS_TN // 2, S)
    tka = min(Ka, 1024)
    tnb = Nb if Nb <= 1024 else 1024
    nj = Nb // tnb
    ns = S // tm

    def body(a_ref, b_ref, *rest):
        o_ref, acc = rest[-2:]
        s = pl.program_id(2)

        @pl.when(s == 0)
        def _():
            acc[...] = jnp.zeros_like(acc)

        acc[...] += _dot(a_ref[...], b_ref[...], TN)

        @pl.when(s == ns - 1)
        def _():
            o_ref[...] = acc[...].astype(o_ref.dtype)

    return pl.pallas_call(
        body, name=name, grid=(Ka // tka, G * nj, ns),
        in_specs=[pl.BlockSpec((tm, tka), lambda i, j, s: (s, i)),
                  pl.BlockSpec((tm, tnb), lambda i, j, s: (s, j))] + [pl.BlockSpec(memory_space=pl.ANY)] * len(order),
        out_specs=pl.BlockSpec((None, tka, tnb), lambda i, j, s: (j // nj, i, j % nj)),
        out_shape=jax.ShapeDtypeStruct((G, Ka, Nb), out_dtype),
        scratch_shapes=[pltpu.VMEM((tka, tnb), F32)],
        compiler_params=_params(3),
    )(a, b, *order)


def mla_pre_fwd(a, gqa, gkva, wuq3, wukv3, gqn, gkn, cos_t, sin_t):
    S = a.shape[0]
    tm = min(TM, S)
    H = N_HEADS

    def body(a_ref, gqa_ref, gkva_ref, wuq_ref, wukv_ref, gqn_ref, gkn_ref, cos_ref, sin_ref,
             q_ref, k_ref, v_ref, cq_ref, ckv_ref):
        av = a_ref[...]
        cq = (_rms(av[:, :Q_LORA], Q_LORA)[0] * gqa_ref[...]).astype(cq_ref.dtype)
        ckv = (_rms(av[:, Q_LORA:Q_LORA + KV_LORA], KV_LORA)[0] * gkva_ref[...]).astype(ckv_ref.dtype)
        cq_ref[...] = cq
        ckv_ref[...] = ckv
        kpe = av[:, Q_LORA + KV_LORA:]
        cos_v, sin_v = cos_ref[...], sin_ref[...]
        for h in range(H):
            qn = _rms(_dot(cq, wuq_ref[h]), QK_DIM)[0] * gqn_ref[...]
            qr = jnp.concatenate([qn[:, :NOPE], _rope(qn[:, NOPE:], cos_v, sin_v)], axis=1)
            q_ref[h] = (qr * Q_PRESCALE).astype(q_ref.dtype)
            kvp = _dot(ckv, wukv_ref[h])
            kn = _rms(jnp.concatenate([kvp[:, :NOPE], kpe], axis=1), QK_DIM)[0] * gkn_ref[...]
            k_ref[h] = jnp.concatenate([kn[:, :NOPE], _rope(kn[:, NOPE:], cos_v, sin_v)], axis=1).astype(k_ref.dtype)
            v_ref[h] = kvp[:, NOPE:].astype(v_ref.dtype)

    row = lambda w: pl.BlockSpec((tm, w), lambda i: (i, 0))
    heads = lambda w: pl.BlockSpec((H, tm, w), lambda i: (0, i, 0))
    return pl.pallas_call(
        body, name="mla_pre_fwd", grid=(S // tm,),
        in_specs=[row(DOWN_PAD), _resident((1, Q_LORA)), _resident((1, KV_LORA)),
                  _resident((H, Q_LORA, QK_PAD)), _resident((H, KV_LORA, NOPE + V_DIM)),
                  _resident((1, QK_PAD)), _resident((1, QK_PAD)), row(128), row(128)],
        out_specs=[heads(QK_PAD), heads(QK_PAD), heads(V_DIM), row(Q_LORA), row(KV_LORA)],
        out_shape=[jax.ShapeDtypeStruct((H, S, QK_PAD), MXU_DTYPE),
                   jax.ShapeDtypeStruct((H, S, QK_PAD), MXU_DTYPE),
                   jax.ShapeDtypeStruct((H, S, V_DIM), MXU_DTYPE),
                   jax.ShapeDtypeStruct((S, Q_LORA), MXU_DTYPE),
                   jax.ShapeDtypeStruct((S, KV_LORA), MXU_DTYPE)],
        compiler_params=_params(1),
    )(a, gqa, gkva, wuq3, wukv3, gqn, gkn, cos_t, sin_t)


def mla_pre_bwd(dq, dk, dv, a, cq, ckv, gqa, gkva, wuq3, wukv3, gqn, gkn, cos_t, sin_t):
    S = a.shape[0]
    tm = min(TM, S)
    H = N_HEADS

    def body(dq_ref, dk_ref, dv_ref, a_ref, cq_ref, ckv_ref, gqa_ref, gkva_ref, wuq_ref, wukv_ref, gqn_ref, gkn_ref,
             cos_ref, sin_ref, da_ref, dwuq_ref, dwukv_ref, dgqn_ref, dgkn_ref, dgqa_ref, dgkva_ref):
        @pl.when(pl.program_id(0) == 0)
        def _():
            for ref in (dwuq_ref, dwukv_ref, dgqn_ref, dgkn_ref, dgqa_ref, dgkva_ref):
                ref[...] = jnp.zeros_like(ref)

        av = a_ref[...]
        kpe = av[:, Q_LORA + KV_LORA:]
        cos_v, sin_v = cos_ref[...], sin_ref[...]
        cqv, ckvv = cq_ref[...], ckv_ref[...]
        dcq = jnp.zeros((tm, Q_LORA), F32)
        dckv = jnp.zeros((tm, KV_LORA), F32)
        dkpe = jnp.zeros((tm, 128), F32)
        dgqn = jnp.zeros((1, QK_PAD), F32)
        dgkn = jnp.zeros((1, QK_PAD), F32)
        up = lambda h: (_dot(cqv, wuq_ref[h]), _dot(ckvv, wukv_ref[h]))
        nxt = up(0)
        for h in range(H):
            wuq, wukv = wuq_ref[h], wukv_ref[h]
            qp, kvp = nxt
            if h + 1 < H:
                nxt = up(h + 1)
            qhat, rq = _rms(qp, QK_DIM)
            dqr = dq_ref[h] * SM_SCALE
            dqn = jnp.concatenate([dqr[:, :NOPE], _rope_bwd(dqr[:, NOPE:], cos_v, sin_v)], axis=1)
            dqp, dg = _rms_bwd(dqn, qhat, rq, gqn_ref[...], QK_DIM)
            dgqn = dgqn + dg
            dqp = dqp.astype(MXU_DTYPE)
            dwuq_ref[h] += _dot(cqv, dqp, TN)
            dcq = dcq + _dot(dqp, wuq, NT)
            khat, rk = _rms(jnp.concatenate([kvp[:, :NOPE], kpe], axis=1), QK_DIM)
            dkr = dk_ref[h] * (1.0 / LOG2E)
            dkn = jnp.concatenate([dkr[:, :NOPE], _rope_bwd(dkr[:, NOPE:], cos_v, sin_v)], axis=1)
            dkk, dg = _rms_bwd(dkn, khat, rk, gkn_ref[...], QK_DIM)
            dgkn = dgkn + dg
            dkpe = dkpe + dkk[:, NOPE:]
            dkvp = jnp.concatenate([dkk[:, :NOPE], dv_ref[h]], axis=1).astype(MXU_DTYPE)
            dwukv_ref[h] += _dot(ckvv, dkvp, TN)
            dckv = dckv + _dot(dkvp, wukv, NT)
        dgqn_ref[...] += dgqn
        dgkn_ref[...] += dgkn
        ahat, r = _rms(av[:, :Q_LORA], Q_LORA)
        daq, dg = _rms_bwd(dcq, ahat, r, gqa_ref[...], Q_LORA)
        dgqa_ref[...] += dg
        ahat, r = _rms(av[:, Q_LORA:Q_LORA + KV_LORA], KV_LORA)
        dakv, dg = _rms_bwd(dckv, ahat, r, gkva_ref[...], KV_LORA)
        dgkva_ref[...] += dg
        da_ref[...] = jnp.concatenate([daq, dakv, dkpe], axis=1)

    row = lambda w: pl.BlockSpec((tm, w), lambda i: (i, 0))
    heads = lambda w: pl.BlockSpec((H, tm, w), lambda i: (0, i, 0))
    return pl.pallas_call(
        body, name="mla_pre_bwd", grid=(S // tm,),
        in_specs=[heads(QK_PAD), heads(QK_PAD), heads(V_DIM), row(DOWN_PAD), row(Q_LORA), row(KV_LORA),
                  _resident((1, Q_LORA)), _resident((1, KV_LORA)),
                  _resident((H, Q_LORA, QK_PAD)), _resident((H, KV_LORA, NOPE + V_DIM)),
                  _resident((1, QK_PAD)), _resident((1, QK_PAD)), row(128), row(128)],
        out_specs=[row(DOWN_PAD), _resident((H, Q_LORA, QK_PAD)), _resident((H, KV_LORA, NOPE + V_DIM)),
                   _resident((1, QK_PAD)), _resident((1, QK_PAD)), _resident((1, Q_LORA)), _resident((1, KV_LORA))],
        out_shape=[jax.ShapeDtypeStruct((S, DOWN_PAD), F32),
                   jax.ShapeDtypeStruct((H, Q_LORA, QK_PAD), F32),
                   jax.ShapeDtypeStruct((H, KV_LORA, NOPE + V_DIM), F32),
                   jax.ShapeDtypeStruct((1, QK_PAD), F32), jax.ShapeDtypeStruct((1, QK_PAD), F32),
                   jax.ShapeDtypeStruct((1, Q_LORA), F32), jax.ShapeDtypeStruct((1, KV_LORA), F32)],
        compiler_params=_params(1),
    )(dq, dk, dv, a, cq, ckv, gqa, gkva, wuq3, wukv3, gqn, gkn, cos_t, sin_t)


def _pair_tables(nb, key_major):
    if key_major:
        pairs = [(qi, kj) for kj in range(nb) for qi in range(kj, nb)]
    else:
        pairs = [(qi, ki) for qi in range(nb) for ki in range(qi + 1)]
    return (jnp.asarray(np.array([p[0] for p in pairs], np.int32)),
            jnp.asarray(np.array([p[1] for p in pairs], np.int32)))


def _scores_t(k, q, pk_col, pq_row, masked):
    s = _dot(k, q, NT)
    return jnp.where(pq_row >= pk_col, s, jnp.finfo(F32).min) if masked else s


def attn_fwd(q, k, v, pos_col, pos_row):
    H, S, _ = q.shape
    t = min(TQ, S)
    nb = S // t
    hb = HEADS_FWD
    qt, kt = _pair_tables(nb, key_major=False)

    def body(qt_ref, kt_ref, q_ref, k_ref, v_ref, pk_ref, pq_ref, o_ref, lse_ref, m_s, l_s, acc):
        step = pl.program_id(1)
        qi, ki = qt_ref[step], kt_ref[step]

        @pl.when(ki == 0)
        def _():
            m_s[...] = jnp.full_like(m_s, -jnp.inf)
            l_s[...] = jnp.zeros_like(l_s)
            acc[...] = jnp.zeros_like(acc)

        def update(masked):
            scores = lambda hh: _scores_t(k_ref[hh], q_ref[hh], pk_ref[...], pq_ref[...], masked)
            def weighted_values(hh, p, alpha):
                acc[hh] = alpha * acc[hh] + _dot(v_ref[hh], p, TN)

            s_next = scores(0)
            pending = None
            for hh in range(hb):
                s = s_next
                if hh + 1 < hb:
                    s_next = scores(hh + 1)
                m_old = m_s[hh]
                m_new = jnp.maximum(m_old, jnp.max(s, axis=0, keepdims=True))
                p = jnp.exp2(s - m_new)
                alpha = jnp.exp2(m_old - m_new)
                l_s[hh] = alpha * l_s[hh] + jnp.sum(p, axis=0, keepdims=True)
                m_s[hh] = m_new
                if pending is not None:
                    weighted_values(*pending)
                pending = (hh, p, alpha)
            weighted_values(*pending)

        @pl.when(ki < qi)
        def _():
            update(False)

        @pl.when(ki == qi)
        def _():
            update(True)
            for hh in range(hb):
                o_ref[:, hh * V_DIM:(hh + 1) * V_DIM] = (acc[hh] / l_s[hh]).T
                lse_ref[hh] = m_s[hh] + jnp.log(l_s[hh]) * LOG2E

    grid_spec = pltpu.PrefetchScalarGridSpec(
        num_scalar_prefetch=2, grid=(H // hb, qt.shape[0]),
        in_specs=[pl.BlockSpec((hb, t, QK_PAD), lambda h, s, qt, kt: (h, qt[s], 0)),
                  pl.BlockSpec((hb, t, QK_PAD), lambda h, s, qt, kt: (h, kt[s], 0)),
                  pl.BlockSpec((hb, t, V_DIM), lambda h, s, qt, kt: (h, kt[s], 0)),
                  pl.BlockSpec((t, 1), lambda h, s, qt, kt: (kt[s], 0)),
                  pl.BlockSpec((1, t), lambda h, s, qt, kt: (0, qt[s]))],
        out_specs=[pl.BlockSpec((t, hb * V_DIM), lambda h, s, qt, kt: (qt[s], h)),
                   pl.BlockSpec((hb, 1, t), lambda h, s, qt, kt: (h, 0, qt[s]))],
        scratch_shapes=[pltpu.VMEM((hb, 1, t), F32), pltpu.VMEM((hb, 1, t), F32), pltpu.VMEM((hb, V_DIM, t), F32)])
    return pl.pallas_call(
        body, name="attn_fwd", grid_spec=grid_spec,
        out_shape=[jax.ShapeDtypeStruct((S, H * V_DIM), F32), jax.ShapeDtypeStruct((H, 1, S), F32)],
        compiler_params=_params(2),
    )(qt, kt, q, k, v, pos_col, pos_row)


def attn_delta(o, do):
    S = o.shape[0]
    t = min(TQ, S)

    def body(o_ref, do_ref, d_ref):
        for h in range(N_HEADS):
            cols = slice(h * V_DIM, (h + 1) * V_DIM)
            d_ref[h] = jnp.sum((o_ref[:, cols] * do_ref[:, cols]).T, axis=0, keepdims=True)

    blk = pl.BlockSpec((t, N_HEADS * V_DIM), lambda i: (i, 0))
    return pl.pallas_call(
        body, name="attn_delta", grid=(S // t,),
        in_specs=[blk, blk],
        out_specs=pl.BlockSpec((N_HEADS, 1, t), lambda i: (0, 0, i)),
        out_shape=jax.ShapeDtypeStruct((N_HEADS, 1, S), F32),
        compiler_params=_params(1),
    )(o, do)


def attn_bwd(q, k, v, do, lse, delta, pos_col, pos_row):
    H, S, _ = q.shape
    t = min(TQ_BWD, S)
    nb = S // t
    qt, kt = _pair_tables(nb, key_major=True)
    tc = min(BWD_CHUNK, t)

    def body(qt_ref, kt_ref, q_ref, k_ref, v_ref, do_ref, lse_ref, dl_ref, pk_ref, pq_ref, dq_ref, dk_ref, dv_ref):
        step = pl.program_id(1)
        qi, kj = qt_ref[step], kt_ref[step]

        @pl.when(step == 0)
        def _():
            dq_ref[...] = jnp.zeros_like(dq_ref)

        @pl.when(qi == kj)
        def _():
            dk_ref[...] = jnp.zeros_like(dk_ref)
            dv_ref[...] = jnp.zeros_like(dv_ref)

        def update(masked):
            seen = lambda c: (c + 1) * tc if masked else t

            def first_matmuls(c):
                cols, ke = slice(c * tc, (c + 1) * tc), seen(c)
                qc = q_ref[cols, :]
                doc = do_ref[cols, :].astype(MXU_DTYPE)
                s = _scores_t(k_ref[:ke, :], qc, pk_ref[:ke, :], pq_ref[:, cols], masked)
                return qc, doc, s, _dot(v_ref[:ke, :], doc, NT)

            nxt = first_matmuls(0)
            for c in range(t // tc):
                qc, doc, s, dp = nxt
                if c + 1 < t // tc:
                    nxt = first_matmuls(c + 1)
                cols, ke = slice(c * tc, (c + 1) * tc), seen(c)
                p = jnp.exp2(s - lse_ref[:, cols])
                ds = (p * (dp - dl_ref[:, cols])).astype(MXU_DTYPE)
                dv_ref[:ke, :] += _dot(p, doc)
                dk_ref[:ke, :] += _dot(ds, qc)
                rows = pl.ds(pl.multiple_of(qi * t + c * tc, tc), tc)
                dq_ref[rows, :] += _dot(ds, k_ref[:ke, :], TN)

        @pl.when(qi == kj)
        def _():
            update(True)

        @pl.when(qi != kj)
        def _():
            update(False)

    q_idx = lambda h, s, qt, kt: (h, qt[s], 0)
    k_idx = lambda h, s, qt, kt: (h, kt[s], 0)
    row_idx = lambda h, s, qt, kt: (h, 0, qt[s])
    grid_spec = pltpu.PrefetchScalarGridSpec(
        num_scalar_prefetch=2, grid=(H, qt.shape[0]),
        in_specs=[pl.BlockSpec((None, t, QK_PAD), q_idx),
                  pl.BlockSpec((None, t, QK_PAD), k_idx),
                  pl.BlockSpec((None, t, V_DIM), k_idx),
                  pl.BlockSpec((t, V_DIM), lambda h, s, qt, kt: (qt[s], h)),
                  pl.BlockSpec((None, 1, t), row_idx),
                  pl.BlockSpec((None, 1, t), row_idx),
                  pl.BlockSpec((t, 1), lambda h, s, qt, kt: (kt[s], 0)),
                  pl.BlockSpec((1, t), lambda h, s, qt, kt: (0, qt[s]))],
        out_specs=[pl.BlockSpec((None, S, QK_PAD), lambda h, s, qt, kt: (h, 0, 0)),
                   pl.BlockSpec((None, t, QK_PAD), k_idx),
                   pl.BlockSpec((None, t, V_DIM), k_idx)])
    return pl.pallas_call(
        body, name="attn_bwd", grid_spec=grid_spec,
        out_shape=[jax.ShapeDtypeStruct((H, S, QK_PAD), F32), jax.ShapeDtypeStruct((H, S, QK_PAD), F32),
                   jax.ShapeDtypeStruct((H, S, V_DIM), F32)],
        compiler_params=_params(2),
    )(qt, kt, q, k, v, do, lse, delta, pos_col, pos_row)


def _conv_specs(S, tr, tc):
    nc = D_MODEL // tc
    hb = tr // 8
    main = lambda third: pl.BlockSpec((tr, tc), lambda c, r: (r, third * nc + c))
    prev = lambda third: pl.BlockSpec((8, tc), lambda c, r: (jnp.maximum(r * hb - 1, 0), third * nc + c))
    nxt = lambda third: pl.BlockSpec((8, tc), lambda c, r: (jnp.minimum((r + 1) * hb, S // 8 - 1), third * nc + c))
    return main, prev, nxt


def _conv_taps(gc, uu, w_ref, first):
    u2 = gc * uu
    rows = lax.broadcasted_iota(jnp.int32, u2.shape, 0)
    u2 = jnp.where((rows < 8) & first, 0.0, u2)
    s1 = pltpu.roll(u2, 1, 0)
    s2 = pltpu.roll(u2, 2, 0)
    u3 = w_ref[2:3, :] * u2 + w_ref[1:2, :] * s1 + w_ref[0:1, :] * s2
    return u2, s1, s2, u3


def conv_fwd(bcu, cw):
    S = bcu.shape[0]
    tr, tc = min(TROW, S), TCH
    main, prev, _ = _conv_specs(S, tr, tc)

    def body(gb_ref, gc_ref, u_ref, gch_ref, uh_ref, w_ref, z_ref):
        gc = jnp.concatenate([gch_ref[...], gc_ref[...]], axis=0)
        uu = jnp.concatenate([uh_ref[...], u_ref[...]], axis=0)
        u3 = _conv_taps(gc, uu, w_ref, pl.program_id(1) == 0)[3]
        z_ref[...] = (gb_ref[...] * u3[8:]).astype(z_ref.dtype)

    return pl.pallas_call(
        body, name="conv_fwd", grid=(D_MODEL // tc, S // tr),
        in_specs=[main(0), main(1), main(2), prev(1), prev(2), pl.BlockSpec((3, tc), lambda c, r: (0, c))],
        out_specs=pl.BlockSpec((tr, tc), lambda c, r: (r, c)),
        out_shape=jax.ShapeDtypeStruct((S, D_MODEL), MXU_DTYPE),
        compiler_params=_params(2),
    )(bcu, bcu, bcu, bcu, bcu, cw)


def conv_bwd(dz, bcu, cw):
    S = bcu.shape[0]
    tr, tc = min(TROW, S), TCH
    nr = S // tr
    main, prev, nxt = _conv_specs(S, tr, tc)

    def body(dz_ref, dzn_ref, gb_ref, gbn_ref, gc_ref, u_ref, gch_ref, uh_ref, w_ref,
             dgb_ref, dgc_ref, du_ref, dw_ref):
        r = pl.program_id(1)
        gcv, uv = gc_ref[...], u_ref[...]
        gc = jnp.concatenate([gch_ref[...], gcv], axis=0)
        uu = jnp.concatenate([uh_ref[...], uv], axis=0)
        u2, s1, s2, u3 = _conv_taps(gc, uu, w_ref, r == 0)
        dzv = dz_ref[...]
        du3 = jnp.concatenate([dzv * gb_ref[...], dzn_ref[...] * gbn_ref[...]], axis=0)
        rows = lax.broadcasted_iota(jnp.int32, du3.shape, 0)
        du3 = jnp.where((rows >= tr) & (r == nr - 1), 0.0, du3)
        n1 = pltpu.roll(du3, tr + 8 - 1, 0)
        n2 = pltpu.roll(du3, tr + 8 - 2, 0)
        du2 = (w_ref[2:3, :] * du3 + w_ref[1:2, :] * n1 + w_ref[0:1, :] * n2)[:tr]
        dgb_ref[...] = (dzv * u3[8:]).astype(dgb_ref.dtype)
        dgc_ref[...] = (du2 * uv).astype(dgc_ref.dtype)
        du_ref[...] = (du2 * gcv).astype(du_ref.dtype)
        d3 = du3[:tr]
        taps = [jnp.sum(d3 * t[8:], axis=0, keepdims=True) for t in (s2, s1, u2)]

        @pl.when(r == 0)
        def _():
            for kk in range(3):
                dw_ref[kk:kk + 1, :] = taps[kk]

        @pl.when(r != 0)
        def _():
            for kk in range(3):
                dw_ref[kk:kk + 1, :] += taps[kk]

    out = pl.BlockSpec((tr, tc), lambda c, r: (r, c))
    nxt_dz = pl.BlockSpec((8, tc), lambda c, r: (jnp.minimum((r + 1) * (tr // 8), S // 8 - 1), c))
    act = jax.ShapeDtypeStruct((S, D_MODEL), MXU_DTYPE)
    return pl.pallas_call(
        body, name="conv_bwd", grid=(D_MODEL // tc, nr),
        in_specs=[out, nxt_dz, main(0), nxt(0), main(1), main(2), prev(1), prev(2),
                  pl.BlockSpec((3, tc), lambda c, r: (0, c))],
        out_specs=[out, out, out, pl.BlockSpec((3, tc), lambda c, r: (0, c))],
        out_shape=[act, act, act, jax.ShapeDtypeStruct((3, D_MODEL), F32)],
        compiler_params=_params(2),
    )(dz, dz, bcu, bcu, bcu, bcu, bcu, bcu, cw)


def loss_head(y, target):
    S, D = y.shape
    tm = min(TM, S)

    def body(y_ref, t_ref, l_ref, dy_ref):
        err = y_ref[...] - t_ref[...]
        dy_ref[...] = err / D
        part = jnp.full((1, 128), jnp.sum(err * err), F32)

        @pl.when(pl.program_id(0) == 0)
        def _():
            l_ref[...] = part

        @pl.when(pl.program_id(0) != 0)
        def _():
            l_ref[...] += part

    blk = pl.BlockSpec((tm, D), lambda i: (i, 0))
    return pl.pallas_call(
        body, name="loss_head", grid=(S // tm,),
        in_specs=[blk, blk],
        out_specs=[pl.BlockSpec((1, 128), lambda i: (0, 0)), blk],
        out_shape=[jax.ShapeDtypeStruct((1, 128), F32), jax.ShapeDtypeStruct((S, D), F32)],
        compiler_params=_params(1),
    )(y, target)


def adamw(parts, w, m, v, *, name):
    R, C = w.shape
    tr = R
    while tr * C * 4 > (1 << 20) and tr % 32 == 0:
        tr //= 2

    def body(p_ref, w_ref, m_ref, v_ref, g_ref, d_ref, mo_ref, vo_ref):
        g = p_ref[0].astype(F32)
        for d in range(1, N_DEV):
            g = g + p_ref[d].astype(F32)
        m_new = ADAM_B1 * m_ref[...] + (1.0 - ADAM_B1) * g
        v_new = ADAM_B2 * v_ref[...] + (1.0 - ADAM_B2) * (g * g)
        m_hat = m_new / (1.0 - ADAM_B1 ** ADAM_STEP)
        v_hat = v_new / (1.0 - ADAM_B2 ** ADAM_STEP)
        g_ref[...] = g
        d_ref[...] = -ADAM_LR * (m_hat / (jnp.sqrt(v_hat) + ADAM_EPS) + ADAM_WD * w_ref[...])
        mo_ref[...] = m_new
        vo_ref[...] = v_new

    blk = pl.BlockSpec((tr, C), lambda i: (i, 0))
    return pl.pallas_call(
        body, name=name, grid=(R // tr,),
        in_specs=[pl.BlockSpec((N_DEV, tr, C), lambda i: (0, i, 0)), blk, blk, blk],
        out_specs=[blk, blk, blk, blk],
        out_shape=[jax.ShapeDtypeStruct((R, C), F32)] * 4,
        compiler_params=_params(1),
    )(parts, w, m, v)


def _mesh_place():
    x, y, c = (lax.axis_index(n) for n in MESH_AXES)
    return x, y, c, 4 * x + 2 * y + c


def _peer(x, y, c, d):
    px = 1 - x if d & 4 else x
    py = 1 - y if d & 2 else y
    pc = 1 - c if d & 1 else c
    return (px, py, pc), 4 * px + 2 * py + pc


def _exchange(srcs, name, scatter):
    n = len(srcs)
    any_spec = pl.BlockSpec(memory_space=pl.ANY)

    def body(*refs):
        ins, outs, token = refs[:n], refs[n:2 * n], refs[2 * n]
        send_sems, recv_sems, local_sems = refs[2 * n + 1:]
        token[...] = jnp.zeros_like(token)
        x, y, c, me = _mesh_place()
        for a in range(n):
            mine = ins[a].at[me] if scatter else ins[a]
            pltpu.make_async_copy(mine, outs[a].at[me], local_sems.at[a]).start()
            for d in range(1, N_DEV):
                peer, peer_lin = _peer(x, y, c, d)
                pltpu.make_async_remote_copy(
                    src_ref=ins[a].at[peer_lin] if scatter else ins[a], dst_ref=outs[a].at[me],
                    send_sem=send_sems.at[a], recv_sem=recv_sems.at[a],
                    device_id=peer, device_id_type=pl.DeviceIdType.MESH).start()
        for a in range(n):
            mine = ins[a].at[me] if scatter else ins[a]
            pltpu.make_async_copy(mine, outs[a].at[me], local_sems.at[a]).wait()
            seven = outs[a].at[pl.ds(0, N_DEV - 1)]
            drain = pltpu.make_async_remote_copy(
                src_ref=seven, dst_ref=seven, send_sem=send_sems.at[a], recv_sem=recv_sems.at[a],
                device_id=(x, y, c), device_id_type=pl.DeviceIdType.MESH)
            drain.wait_send()
            drain.wait_recv()

    block = (lambda s: s.shape[1:]) if scatter else (lambda s: s.shape)
    out = pl.pallas_call(
        body, name=name,
        in_specs=[any_spec] * n, out_specs=[any_spec] * n + [pl.BlockSpec(memory_space=pltpu.VMEM)],
        out_shape=[jax.ShapeDtypeStruct((N_DEV,) + tuple(block(s)), s.dtype) for s in srcs]
        + [jax.ShapeDtypeStruct((8, 128), F32)],
        scratch_shapes=[pltpu.SemaphoreType.DMA((n,)), pltpu.SemaphoreType.DMA((n,)), pltpu.SemaphoreType.DMA((n,))],
    )(*srcs)
    return out[:n], out[n]


_ANY = pl.BlockSpec(memory_space=pl.ANY)
_HBM = pl.BlockSpec(memory_space=pltpu.HBM)
_SEM = pl.BlockSpec(memory_space=pltpu.SEMAPHORE)


def _in_hbm(arrays):
    return [pltpu.with_memory_space_constraint(a, pltpu.HBM) for a in arrays]


def place_own(shards, after, *, name):
    n = len(shards)

    def body(*refs):
        ins, outs, sems = refs[:n], refs[n + 1:2 * n + 1], refs[2 * n + 1]
        me = _mesh_place()[3]
        copies = [pltpu.make_async_copy(ins[a], outs[a].at[me], sems.at[a]) for a in range(n)]
        for cp in copies:
            cp.start()
        for cp in copies:
            cp.wait()

    return pl.pallas_call(
        body, name=name, in_specs=[_ANY] * (n + 1), out_specs=[_ANY] * n,
        out_shape=[jax.ShapeDtypeStruct((N_DEV,) + s.shape, s.dtype) for s in shards],
        scratch_shapes=[pltpu.SemaphoreType.DMA((n,))],
    )(*shards, after)


def exchange_start(srcs, lands, slots, *, name):
    n, m = len(srcs), len(lands)

    def body(*refs):
        ins, zones = refs[:n], refs[n:n + m]
        send_sems, recv_sems, token = refs[n + m], refs[n + m + 1], refs[-1]
        x, y, c, me = _mesh_place()
        for a in range(n):
            for d in range(1, N_DEV):
                peer, peer_lin = _peer(x, y, c, d)
                src = ins[a] if slots is None else ins[a].at[peer_lin]
                dst = zones[a].at[me] if slots is None else zones[slots[a][0]].at[me, slots[a][1]]
                pltpu.make_async_remote_copy(
                    src_ref=src, dst_ref=dst, send_sem=send_sems.at[a], recv_sem=recv_sems.at[a],
                    device_id=peer, device_id_type=pl.DeviceIdType.MESH).start()
        token[...] = jnp.zeros_like(token)

    both = list(srcs) + list(lands)
    out = pl.pallas_call(
        body, name=name,
        in_specs=[_HBM] * (n + m),
        out_specs=[_SEM, _SEM] + [_HBM] * (n + m) + [pl.BlockSpec(memory_space=pltpu.VMEM)],
        out_shape=[pltpu.SemaphoreType.DMA((n,)), pltpu.SemaphoreType.DMA((n,))]
        + [pltpu.HBM(a.shape, a.dtype) for a in both] + [jax.ShapeDtypeStruct((8, 128), F32)],
        input_output_aliases={i: 2 + i for i in range(n + m)},
        compiler_params=pltpu.CompilerParams(has_side_effects=pltpu.SideEffectType.DATAFLOW_SIDE_EFFECTING),
    )(*_in_hbm(both))
    return out[0], out[1], out[2:2 + n], out[2 + n:2 + n + m], out[-1]


def exchange_wait(send_sems, recv_sems, srcs, lands, slots, after, *, name):
    n, m = len(srcs), len(lands)

    def body(*refs):
        ins, zones = refs[:n], refs[n:n + m]
        send_ref, recv_ref = refs[n + m], refs[n + m + 1]
        x, y, c, _ = _mesh_place()
        for a in range(n):
            seven = (zones[a] if slots is None else ins[a]).at[pl.ds(0, N_DEV - 1)]
            drain = pltpu.make_async_remote_copy(
                src_ref=seven, dst_ref=seven, send_sem=send_ref.at[a], recv_sem=recv_ref.at[a],
                device_id=(x, y, c), device_id_type=pl.DeviceIdType.MESH)
            drain.wait_send()
            drain.wait_recv()

    both = list(srcs) + list(lands)
    out = pl.pallas_call(
        body, name=name,
        in_specs=[_HBM] * (n + m) + [_SEM, _SEM, _ANY],
        out_specs=[_HBM] * (n + m),
        out_shape=[pltpu.HBM(a.shape, a.dtype) for a in both],
        input_output_aliases={i: i for i in range(n + m)},
        compiler_params=pltpu.CompilerParams(has_side_effects=pltpu.SideEffectType.DATAFLOW_SIDE_EFFECTING),
    )(*both, send_sems, recv_sems, after)
    return out[:n], out[n:]


def scatter_finish(remote, local, lands, vec, *, name):
    every = list(remote) + list(local)
    n, r, m = len(every), len(remote), len(lands)

    def body(*refs):
        ins, vec_ref, zones_in = refs[:n], refs[n], refs[n + 1:n + 1 + m]
        vec_out = refs[n + 1 + 2 * m]
        send_sems, recv_sems, local_sems = refs[n + 2 + 2 * m:]
        x, y, c, me = _mesh_place()

        def own(a):
            if a == n:
                return pltpu.make_async_copy(vec_ref, vec_out.at[me], local_sems.at[a])
            return pltpu.make_async_copy(ins[a].at[me], zones_in[every[a][1]].at[me, every[a][2]], local_sems.at[a])

        for a in range(n + 1):
            own(a).start()
        for a in list(range(r)) + [n]:
            for d in range(1, N_DEV):
                peer, peer_lin = _peer(x, y, c, d)
                src = vec_ref if a == n else ins[a].at[peer_lin]
                dst = vec_out.at[me] if a == n else zones_in[every[a][1]].at[me, every[a][2]]
                pltpu.make_async_remote_copy(
                    src_ref=src, dst_ref=dst, send_sem=send_sems.at[min(a, r)], recv_sem=recv_sems.at[min(a, r)],
                    device_id=peer, device_id_type=pl.DeviceIdType.MESH).start()
        for a in range(n + 1):
            own(a).wait()
        for a in list(range(r)) + [n]:
            seven = (vec_out if a == n else ins[a]).at[pl.ds(0, N_DEV - 1)]
            drain = pltpu.make_async_remote_copy(
                src_ref=seven, dst_ref=seven, send_sem=send_sems.at[min(a, r)], recv_sem=recv_sems.at[min(a, r)],
                device_id=(x, y, c), device_id_type=pl.DeviceIdType.MESH)
            drain.wait_send()
            drain.wait_recv()

    out = pl.pallas_call(
        body, name=name,
        in_specs=[_ANY] * (n + 1 + m), out_specs=[_ANY] * (m + 1),
        out_shape=[jax.ShapeDtypeStruct(z.shape, z.dtype) for z in lands]
        + [jax.ShapeDtypeStruct((N_DEV,) + vec.shape, vec.dtype)],
        input_output_aliases={n + 1 + i: i for i in range(m)},
        scratch_shapes=[pltpu.SemaphoreType.DMA((r + 1,)), pltpu.SemaphoreType.DMA((r + 1,)),
                        pltpu.SemaphoreType.DMA((n + 1,))],
    )(*[e[0] for e in every], vec, *lands)
    return out[:m], out[m]


def _rope_tables(pos):
    inv_freq = ROPE_THETA ** (-jnp.arange(0, ROPE, 2, dtype=F32) / ROPE)
    ang = pos.astype(F32)[:, None] * inv_freq
    cos, sin = jnp.cos(ang), jnp.sin(ang)
    pad = jnp.zeros((pos.shape[0], 128 - ROPE), F32)
    return jnp.concatenate([cos, cos, pad + 1.0], axis=1), jnp.concatenate([-sin, sin, pad], axis=1)


def _pad_last(w, n):
    return jnp.pad(w, [(0, 0)] * (w.ndim - 1) + [(0, n - w.shape[-1])])


def kernel(x, positions, g_mix, g_mlp, attn_w_down, attn_g_q_a, attn_g_kv_a, attn_w_uq, attn_w_ukv, attn_g_qnorm, attn_g_knorm, attn_w_o, conv_w_in, conv_w, conv_w_out, mlp_w1, mlp_w2, loss_target, m_g_mix, m_g_mlp, m_attn_w_down, m_attn_g_q_a, m_attn_g_kv_a, m_attn_w_uq, m_attn_w_ukv, m_attn_g_qnorm, m_attn_g_knorm, m_attn_w_o, m_conv_w_in, m_conv_w, m_conv_w_out, m_mlp_w1, m_mlp_w2, v_g_mix, v_g_mlp, v_attn_w_down, v_attn_g_q_a, v_attn_g_kv_a, v_attn_w_uq, v_attn_w_ukv, v_attn_g_qnorm, v_attn_g_knorm, v_attn_w_o, v_conv_w_in, v_conv_w, v_conv_w_out, v_mlp_w1, v_mlp_w2):
    weights = dict(g_mix=g_mix, g_mlp=g_mlp, attn_w_down=attn_w_down, attn_g_q_a=attn_g_q_a, attn_g_kv_a=attn_g_kv_a,
                   attn_w_uq=attn_w_uq, attn_w_ukv=attn_w_ukv, attn_g_qnorm=attn_g_qnorm, attn_g_knorm=attn_g_knorm,
                   attn_w_o=attn_w_o, conv_w_in=conv_w_in, conv_w=conv_w, conv_w_out=conv_w_out, mlp_w1=mlp_w1, mlp_w2=mlp_w2)
    mom1 = dict(g_mix=m_g_mix, g_mlp=m_g_mlp, attn_w_down=m_attn_w_down, attn_g_q_a=m_attn_g_q_a, attn_g_kv_a=m_attn_g_kv_a,
                attn_w_uq=m_attn_w_uq, attn_w_ukv=m_attn_w_ukv, attn_g_qnorm=m_attn_g_qnorm, attn_g_knorm=m_attn_g_knorm,
                attn_w_o=m_attn_w_o, conv_w_in=m_conv_w_in, conv_w=m_conv_w, conv_w_out=m_conv_w_out, mlp_w1=m_mlp_w1, mlp_w2=m_mlp_w2)
    mom2 = dict(g_mix=v_g_mix, g_mlp=v_g_mlp, attn_w_down=v_attn_w_down, attn_g_q_a=v_attn_g_q_a, attn_g_kv_a=v_attn_g_kv_a,
                attn_w_uq=v_attn_w_uq, attn_w_ukv=v_attn_w_ukv, attn_g_qnorm=v_attn_g_qnorm, attn_g_knorm=v_attn_g_knorm,
                attn_w_o=v_attn_w_o, conv_w_in=v_conv_w_in, conv_w=v_conv_w, conv_w_out=v_conv_w_out, mlp_w1=v_mlp_w1, mlp_w2=v_mlp_w2)
    big = ["attn_w_down", "attn_w_uq", "attn_w_ukv", "attn_w_o", "conv_w_in", "conv_w", "conv_w_out", "mlp_w1", "mlp_w2"]
    small = ["g_mix", "g_mlp", "attn_g_q_a", "attn_g_kv_a", "attn_g_qnorm", "attn_g_knorm"]
    order = ["g_mix", "g_mlp", "attn_w_down", "attn_g_q_a", "attn_g_kv_a", "attn_w_uq", "attn_w_ukv", "attn_g_qnorm",
             "attn_g_knorm", "attn_w_o", "conv_w_in", "conv_w", "conv_w_out", "mlp_w1", "mlp_w2"]

    xs = x[0]
    pos = positions[0]
    target = loss_target[0]
    S = xs.shape[0]
    depth = g_mix.shape[0]
    cos_t, sin_t = _rope_tables(pos)
    pos_col, pos_row = pos.reshape(S, 1), pos.reshape(1, S)

    keys, shards = [], []
    for name in big:
        for l in range(weights[name].shape[0]):
            keys.append((name, l))
            shards.append(weights[name][l] if name == "conv_w" else weights[name][l].astype(WIRE_DTYPE))
    first = [j for j, (name, l) in enumerate(keys) if l == 0 and name.startswith("attn")]
    second = [j for j, (name, l) in enumerate(keys) if l == 0 and name.startswith("mlp")]
    later = [j for j in range(len(keys)) if j not in first + second]
    me = 4 * lax.axis_index("x") + 2 * lax.axis_index("y") + lax.axis_index("c")

    def zones_with_own(js, token):
        return [lax.dynamic_update_slice(lax.empty((N_DEV,) + shards[j].shape, shards[j].dtype),
                                         (shards[j] + token[0, 0].astype(shards[j].dtype))[None],
                                         (me,) + (0,) * shards[j].ndim) for j in js]

    arrived, token = _exchange([shards[j] for j in first], "gather_first", scatter=False)
    full = dict(zip([keys[j] for j in first], arrived))
    g1 = exchange_start([shards[j] for j in second], zones_with_own(second, token), None, name="gather_mlp0_start")
    g2 = exchange_start([shards[j] for j in later], zones_with_own(later, g1[4]), None, name="gather_rest_start")
    g_mix_0 = g_mix[0] + g2[4][0, 0]

    def rows(name, l):
        g = full[(name, l)]
        return g.reshape(g.shape[0] * g.shape[1], g.shape[2])

    saved = []
    for i in range(depth):
        l = i // 2
        rec = {"x0": xs}
        if i == 1:
            arrived = exchange_wait(*g2[:4], None, xs, name="gather_rest_wait")[1]
            full.update(zip([keys[j] for j in later], arrived))
        if i % 2 == 0:
            wd3 = _pad_last(rows("attn_w_down", l), DOWN_PAD)[None]
            wuq3 = _pad_last(full[("attn_w_uq", l)], QK_PAD)
            wukv3 = full[("attn_w_ukv", l)]
            gqn = _pad_last(attn_g_qnorm[l][None], QK_PAD)
            gkn = _pad_last(attn_g_knorm[l][None], QK_PAD)
            gqa, gkva = attn_g_q_a[l][None], attn_g_kv_a[l][None]
            h, a = norm_matmul(xs, g_mix_0 if i == 0 else g_mix[i], wd3, name="attn_down")
            q, k, v, cq, ckv = mla_pre_fwd(a, gqa, gkva, wuq3, wukv3, gqn, gkn, cos_t, sin_t)
            o, lse = attn_fwd(q, k, v, pos_col, pos_row)
            x1 = matmul_residual(o, rows("attn_w_o", l), xs, name="attn_out")
            rec.update(h=h, a=a, q=q, k=k, v=v, cq=cq, ckv=ckv, o=o, lse=lse, wd3=wd3, wuq3=wuq3, wukv3=wukv3,
                       gqn=gqn, gkn=gkn, gqa=gqa, gkva=gkva)
        else:
            cw = full[("conv_w", l)].transpose(1, 0, 2).reshape(3, D_MODEL)
            h, bcu = norm_matmul(xs, g_mix[i], full[("conv_w_in", l)], name="conv_in")
            z = conv_fwd(bcu, cw)
            x1 = matmul_residual(z, rows("conv_w_out", l), xs, name="conv_out")
            rec.update(h=h, bcu=bcu, z=z, cw=cw)
        if i == 0:
            arrived = exchange_wait(*g1[:4], None, x1, name="gather_mlp0_wait")[1]
            full.update(zip([keys[j] for j in second], arrived))
        h2, act = norm_matmul(x1, g_mlp[i], full[("mlp_w1", i)], name="mlp_up", mlp=True)
        xs = matmul_residual(act, rows("mlp_w2", i), x1, name="mlp_down")
        rec.update(x1=x1, h2=h2, act=act)
        saved.append(rec)

    sq, dx = loss_head(xs, target)
    loss = lax.psum(sq[0, 0] * (0.5 / D_MODEL), MESH_AXES)

    grads = {name: [None] * weights[name].shape[0] for name in order}
    token = None
    for i in reversed(range(depth)):
        l = i // 2
        rec = saved[i]
        grads["mlp_w2"][i] = mm_tn(rec["act"], dx, name="mlp_down_dw", G=1, out_dtype=WIRE_DTYPE).reshape(N_DEV, -1, D_MODEL)
        du = matmul_nt(dx, rows("mlp_w2", i)[None], name="mlp_down_dx", epi="mlp_du", u=rec["act"])
        grads["mlp_w1"][i] = mm_tn(rec["h2"], du, name="mlp_up_dw", G=N_DEV, out_dtype=WIRE_DTYPE)
        dx1, dg = matmul_nt(du, full[("mlp_w1", i)], name="mlp_up_dx", epi="rms_bwd", x=rec["x1"], g=g_mlp[i], dx=dx)
        grads["g_mlp"][i] = dg[0]
        if i == 0:
            flying = [keys[j] for j in second + later]
            srcs = [grads[name][l_] for name, l_ in flying]
            slots = [(big.index(name), l_) for name, l_ in flying]
            zones = [lax.empty((N_DEV, weights[name].shape[0]) + grads[name][-1].shape[1:], grads[name][-1].dtype)
                     for name in big]
            for src, (k, l_) in zip(srcs, slots):
                own = lax.dynamic_index_in_dim(src, me, 0, keepdims=True)[None]
                zones[k] = lax.dynamic_update_slice(zones[k], own, (me, l_) + (0,) * (src.ndim - 1))
            s_send, s_recv, s_srcs, s_zones, token = exchange_start(srcs, zones, slots, name="scatter_rest_start")
        if i % 2 == 0:
            grads["attn_w_o"][l] = mm_tn(rec["o"], dx1, name="attn_out_dw", G=1, out_dtype=WIRE_DTYPE,
                                         after=token).reshape(N_DEV, -1, D_MODEL)
            do = matmul_nt(dx1, rows("attn_w_o", l)[None], name="attn_out_dx")
            dq, dk, dv = attn_bwd(rec["q"], rec["k"], rec["v"], do, rec["lse"], attn_delta(rec["o"], do), pos_col, pos_row)
            da, dwuq, dwukv, dgqn, dgkn, dgqa, dgkva = mla_pre_bwd(
                dq, dk, dv, rec["a"], rec["cq"], rec["ckv"], rec["gqa"], rec["gkva"], rec["wuq3"], rec["wukv3"],
                rec["gqn"], rec["gkn"], cos_t, sin_t)
            grads["attn_w_uq"][l] = dwuq[:, :, :QK_DIM].astype(WIRE_DTYPE)
            grads["attn_w_ukv"][l] = dwukv.astype(WIRE_DTYPE)
            grads["attn_g_qnorm"][l] = dgqn[0, :QK_DIM]
            grads["attn_g_knorm"][l] = dgkn[0, :QK_DIM]
            grads["attn_g_q_a"][l] = dgqa[0]
            grads["attn_g_kv_a"][l] = dgkva[0]
            dwd = mm_tn(rec["h"], da, name="attn_down_dw", G=1, out_dtype=WIRE_DTYPE)
            grads["attn_w_down"][l] = dwd[0, :, :DOWN].reshape(N_DEV, -1, DOWN)
            dx, dg = matmul_nt(da, rec["wd3"], name="attn_down_dx", epi="rms_bwd", x=rec["x0"], g=g_mix[i], dx=dx1)
        else:
            grads["conv_w_out"][l] = mm_tn(rec["z"], dx1, name="conv_out_dw", G=1, out_dtype=WIRE_DTYPE).reshape(N_DEV, -1, D_MODEL)
            dz = matmul_nt(dx1, rows("conv_w_out", l)[None], name="conv_out_dx")
            dgb, dgc, du_, dcw = conv_bwd(dz, rec["bcu"], rec["cw"])
            grads["conv_w"][l] = dcw.reshape(3, N_DEV, -1).transpose(1, 0, 2)
            dbcu = jnp.concatenate([dgb, dgc, du_], axis=1)
            grads["conv_w_in"][l] = mm_tn(rec["h"], dbcu, name="conv_in_dw", G=N_DEV, out_dtype=WIRE_DTYPE)
            dx, dg = matmul_nt(dbcu, full[("conv_w_in", l)], name="conv_in_dx", epi="rms_bwd", x=rec["x0"], g=g_mix[i], dx=dx1)
        grads["g_mix"][i] = dg[0]

    sizes = [weights[name].size for name in small]
    n_small = sum(sizes)
    rows_small = -(-n_small // (8 * 128)) * 8

    def pack(tree):
        flat = jnp.concatenate([jnp.stack(tree[name]).reshape(-1) if isinstance(tree[name], list) else tree[name].reshape(-1)
                                for name in small])
        return jnp.pad(flat, (0, rows_small * 128 - n_small)).reshape(rows_small, 128)

    s_srcs, s_zones = exchange_wait(s_send, s_recv, s_srcs, s_zones, slots, dx, name="scatter_rest_wait")
    remote = [(grads[name][l_], big.index(name), l_) for name, l_ in (keys[j] for j in first)]
    parts, gain_parts = scatter_finish(remote, [], s_zones, pack(grads), name="scatter_last")

    out = {}
    for name, part in zip(big, parts):
        w = weights[name]
        flat = lambda t: t.reshape(-1, t.shape[-1])
        res = adamw(part.reshape(N_DEV, -1, w.shape[-1]), flat(w), flat(mom1[name]), flat(mom2[name]), name="adamw_" + name)
        out[name] = [r.reshape(w.shape) for r in res]
    res = adamw(gain_parts, pack(weights), pack(mom1), pack(mom2), name="adamw_gains")
    offset = 0
    for name, size in zip(small, sizes):
        out[name] = [r.reshape(-1)[offset:offset + size].reshape(weights[name].shape) for r in res]
        offset += size

    return (loss, dx[None], *[out[n][0] for n in order], *[out[n][1] for n in order],
            *[out[n][2] for n in order], *[out[n][3] for n in order])
```

```python
import jax
import jax.numpy as jnp
import numpy as np
from jax import lax
from jax.experimental import pallas as pl
from jax.experimental.pallas import tpu as pltpu

F32 = jnp.float32
MXU_DTYPE = jnp.bfloat16
WIRE_DTYPE = jnp.bfloat16

D_MODEL = 1024
N_HEADS = 8
NOPE = 128
ROPE = 64
QK_DIM = NOPE + ROPE
QK_PAD = 256
V_DIM = 128
Q_LORA = 256
KV_LORA = 128
DOWN = Q_LORA + KV_LORA + ROPE
DOWN_PAD = 512
ROPE_THETA = 10000.0
EPS = 1e-6
SM_SCALE = QK_DIM ** -0.5
LOG2E = 1.4426950408889634
Q_PRESCALE = SM_SCALE * LOG2E
ADAM_LR, ADAM_B1, ADAM_B2, ADAM_EPS, ADAM_WD, ADAM_STEP = 0.001, 0.9, 0.999, 1e-08, 0.01, 10
N_DEV = 8
MESH_AXES = ("x", "y", "c")

TM = 512
TM_WIDE = 1024
TM_TOKENS_TN = 2048
TQ = 512
HEADS_FWD = 8
TQ_BWD = 2048
BWD_CHUNK = 256
TROW = 256
HALO = 16
VMEM_LIMIT = 48 << 20

NN = (((1,), (0,)), ((), ()))
NT = (((1,), (1,)), ((), ()))
TN = (((0,), (0,)), ((), ()))


def _dot(a, b, dims=NN):
    return lax.dot_general(a.astype(MXU_DTYPE), b.astype(MXU_DTYPE), dims, preferred_element_type=F32)


def _params(n_axes):
    return pltpu.CompilerParams(dimension_semantics=("arbitrary",) * n_axes, vmem_limit_bytes=VMEM_LIMIT)


def _rms(xv, n):
    r = lax.rsqrt(jnp.sum(xv * xv, axis=-1, keepdims=True) / n + EPS)
    return xv * r, r


def _rms_bwd(dy, xhat, r, g, n):
    dg = jnp.sum(dy * xhat, axis=0, keepdims=True)
    dxh = dy * g
    dx = r * (dxh - xhat * (jnp.sum(dxh * xhat, axis=-1, keepdims=True) / n))
    return dx, dg


def _swap_halves(t):
    lane = lax.broadcasted_iota(jnp.int32, t.shape, 1)
    return jnp.where(lane < ROPE // 2, pltpu.roll(t, 128 - ROPE // 2, 1), pltpu.roll(t, ROPE // 2, 1))


def _rope(t, cos_t, sin_t):
    return t * cos_t + _swap_halves(t) * sin_t


def _rope_bwd(dout, cos_t, sin_t):
    return dout * cos_t + _swap_halves(dout * sin_t)


def rms_mm(x, g, w3, *, name, mlp=False):
    S, D = x.shape
    G, _, Nb = w3.shape
    tm = min(TM_WIDE, S)
    tn = Nb if Nb <= 512 else 512
    nj = Nb // tn
    N = G * Nb

    def body(x_ref, g_ref, w_ref, h_ref, *rest):
        hs = rest[-1]

        @pl.when(pl.program_id(1) == 0)
        def _():
            xv = x_ref[...]
            r = lax.rsqrt(jnp.mean(xv * xv, axis=-1, keepdims=True) + EPS)
            h = (xv * r * g_ref[...]).astype(hs.dtype)
            hs[...] = h
            h_ref[...] = h

        acc = lax.dot_general(hs[...], w_ref[...].astype(hs.dtype), NN, preferred_element_type=F32)
        if mlp:
            rl = jnp.maximum(acc, 0.0)
            rest[0][...] = (rl * rl).astype(rest[0].dtype)
        else:
            rest[0][...] = acc

    out_shape = [jax.ShapeDtypeStruct((S, D), MXU_DTYPE), jax.ShapeDtypeStruct((S, N), MXU_DTYPE if mlp else F32)]
    out_specs = [pl.BlockSpec((tm, D), lambda i, j: (i, 0)), pl.BlockSpec((tm, tn), lambda i, j: (i, j))]
    return pl.pallas_call(
        body, name=name, grid=(S // tm, G * nj),
        in_specs=[pl.BlockSpec((tm, D), lambda i, j: (i, 0)),
                  pl.BlockSpec((1, D), lambda i, j: (0, 0)),
                  pl.BlockSpec((None, D, tn), lambda i, j: (j // nj, 0, j % nj))],
        out_specs=out_specs, out_shape=out_shape,
        scratch_shapes=[pltpu.VMEM((tm, D), MXU_DTYPE)],
        compiler_params=_params(2),
    )(x, g.reshape(1, D), w3)


def mm_res(a, w, res, *, name):
    S, K = a.shape
    _, N = w.shape
    tm = min(TM_WIDE, S)
    tk = min(K, 1024)
    nk = K // tk

    def body(a_ref, w_ref, r_ref, o_ref, acc):
        k = pl.program_id(1)

        @pl.when(k == 0)
        def _():
            acc[...] = jnp.zeros_like(acc)

        acc[...] += _dot(a_ref[...], w_ref[...])

        @pl.when(k == nk - 1)
        def _():
            o_ref[...] = r_ref[...] + acc[...]

    return pl.pallas_call(
        body, name=name, grid=(S // tm, nk),
        in_specs=[pl.BlockSpec((tm, tk), lambda i, k: (i, k)),
                  pl.BlockSpec((tk, N), lambda i, k: (k, 0)),
                  pl.BlockSpec((tm, N), lambda i, k: (i, 0))],
        out_specs=pl.BlockSpec((tm, N), lambda i, k: (i, 0)),
        out_shape=jax.ShapeDtypeStruct((S, N), F32),
        scratch_shapes=[pltpu.VMEM((tm, N), F32)],
        compiler_params=_params(2),
    )(a, w, res)


def mm_nt(a, w3, *, name, epi="plain", u=None, x=None, g=None, dx=None):
    S, N = a.shape
    G, Ko, Nb = w3.shape
    assert N == G * Nb
    tm = min(TM if epi == "rms_bwd" else TM_WIDE, S)
    tkk = Nb if Nb <= 1024 else 1024
    nk = Nb // tkk
    ks = G * nk
    tko = Ko if epi == "rms_bwd" else min(Ko, 512)

    def body(a_ref, w_ref, *rest):
        acc = rest[-1]
        i, k = pl.program_id(0), pl.program_id(2)

        @pl.when(k == 0)
        def _():
            acc[...] = jnp.zeros_like(acc)

        acc[...] += _dot(a_ref[...], w_ref[...], NT)

        @pl.when(k == ks - 1)
        def _():
            if epi == "plain":
                rest[0][...] = acc[...]
            elif epi == "mlp_du":
                u_ref, o_ref = rest[0], rest[1]
                o_ref[...] = (acc[...] * (2.0 * jnp.sqrt(u_ref[...].astype(F32)))).astype(o_ref.dtype)
            else:
                x_ref, g_ref, dx_ref, o_ref, dg_ref = rest[:5]
                xhat, r = _rms(x_ref[...], Ko)
                dxb, dg = _rms_bwd(acc[...], xhat, r, g_ref[...], Ko)
                o_ref[...] = dx_ref[...] + dxb

                @pl.when(i == 0)
                def _():
                    dg_ref[...] = dg

                @pl.when(i != 0)
                def _():
                    dg_ref[...] += dg

    in_specs = [pl.BlockSpec((tm, tkk), lambda i, j, k: (i, k)),
                pl.BlockSpec((None, tko, tkk), lambda i, j, k: (k // nk, j, k % nk))]
    args = [a, w3]
    tile = pl.BlockSpec((tm, tko), lambda i, j, k: (i, j))
    if epi == "plain":
        out_shape, out_specs = jax.ShapeDtypeStruct((S, Ko), F32), tile
    elif epi == "mlp_du":
        in_specs.append(tile)
        args.append(u)
        out_shape, out_specs = jax.ShapeDtypeStruct((S, Ko), MXU_DTYPE), tile
    else:
        vec = pl.BlockSpec((1, Ko), lambda i, j, k: (0, 0))
        in_specs += [tile, vec, tile]
        args += [x, g.reshape(1, Ko), dx]
        out_shape = [jax.ShapeDtypeStruct((S, Ko), F32), jax.ShapeDtypeStruct((1, Ko), F32)]
        out_specs = [tile, vec]
    return pl.pallas_call(
        body, name=name, grid=(S // tm, Ko // tko, ks),
        in_specs=in_specs, out_specs=out_specs, out_shape=out_shape,
        scratch_shapes=[pltpu.VMEM((tm, tko), F32)],
        compiler_params=_params(3),
    )(*args)


def _resident(shape):
    return pl.BlockSpec(shape, lambda i: (0,) * len(shape))


def norm_matmul(x, g, w3, *, name, mlp=False, out_dtype=F32):
    if mlp:
        out_dtype = MXU_DTYPE
    S, D = x.shape
    G, _, Nb = w3.shape
    tm = min(TM, S)
    N = G * Nb

    def body(x_ref, g_ref, w_ref, h_ref, o_ref):
        xv = x_ref[...]
        r = lax.rsqrt(jnp.mean(xv * xv, axis=-1, keepdims=True) + EPS)
        h = (xv * r * g_ref[...]).astype(h_ref.dtype)
        h_ref[...] = h
        for gi in range(G):
            acc = _dot(h, w_ref[gi])
            if mlp:
                acc = jnp.square(jnp.maximum(acc, 0.0))
            o_ref[:, gi * Nb:(gi + 1) * Nb] = acc.astype(o_ref.dtype)

    rows = lambda w: pl.BlockSpec((tm, w), lambda i: (i, 0))
    return pl.pallas_call(
        body, name=name, grid=(S // tm,),
        in_specs=[rows(D), _resident((1, D)), _resident((G, D, Nb))],
        out_specs=[rows(D), rows(N)],
        out_shape=[jax.ShapeDtypeStruct((S, D), MXU_DTYPE), jax.ShapeDtypeStruct((S, N), out_dtype)],
        compiler_params=_params(1),
    )(x, g.reshape(1, D), w3)


def matmul_residual(a, w, res, *, name):
    S, K = a.shape
    _, N = w.shape
    tm = min(TM, S)

    def body(a_ref, w_ref, r_ref, o_ref):
        o_ref[...] = r_ref[...] + _dot(a_ref[...], w_ref[...])

    rows = lambda w_: pl.BlockSpec((tm, w_), lambda i: (i, 0))
    return pl.pallas_call(
        body, name=name, grid=(S // tm,),
        in_specs=[rows(K), _resident((K, N)), rows(N)],
        out_specs=rows(N), out_shape=jax.ShapeDtypeStruct((S, N), F32),
        compiler_params=_params(1),
    )(a, w, res)


def matmul_nt(a, w3, *, name, epi="plain", u=None, x=None, g=None, dx=None):
    S, N = a.shape
    G, Ko, Nb = w3.shape
    assert N == G * Nb
    tm = min(TM, S)
    tko = min(Ko, 512)

    def body(a_ref, w_ref, *rest):
        if epi == "mlp_du":
            u_ref, o_ref = rest
            av = a_ref[...].astype(MXU_DTYPE)
            for j in range(Ko // tko):
                cols = slice(j * tko, (j + 1) * tko)
                da = _dot(av, w_ref[0, cols, :], NT)
                o_ref[:, cols] = (da * (2.0 * jnp.sqrt(u_ref[:, cols].astype(F32)))).astype(o_ref.dtype)
            return
        acc = _dot(a_ref[:, :Nb], w_ref[0], NT)
        for gi in range(1, G):
            acc = acc + _dot(a_ref[:, gi * Nb:(gi + 1) * Nb], w_ref[gi], NT)
        if epi == "plain":
            rest[0][...] = acc
        else:
            x_ref, g_ref, dx_ref, o_ref, dg_ref = rest
            xhat, r = _rms(x_ref[...], Ko)
            dxb, dg = _rms_bwd(acc, xhat, r, g_ref[...], Ko)
            o_ref[...] = dx_ref[...] + dxb

            @pl.when(pl.program_id(0) == 0)
            def _():
                dg_ref[...] = dg

            @pl.when(pl.program_id(0) != 0)
            def _():
                dg_ref[...] += dg

    rows = lambda w_: pl.BlockSpec((tm, w_), lambda i: (i, 0))
    in_specs = [rows(N), _resident((G, Ko, Nb))]
    args = [a, w3]
    if epi == "plain":
        out_shape, out_specs = jax.ShapeDtypeStruct((S, Ko), F32), rows(Ko)
    elif epi == "mlp_du":
        in_specs.append(rows(Ko))
        args.append(u)
        out_shape, out_specs = jax.ShapeDtypeStruct((S, Ko), MXU_DTYPE), rows(Ko)
    else:
        in_specs += [rows(Ko), _resident((1, Ko)), rows(Ko)]
        args += [x, g.reshape(1, Ko), dx]
        out_shape = [jax.ShapeDtypeStruct((S, Ko), F32), jax.ShapeDtypeStruct((1, Ko), F32)]
        out_specs = [rows(Ko), _resident((1, Ko))]
    return pl.pallas_call(
        body, name=name, grid=(S // tm,),
        in_specs=in_specs, out_specs=out_specs, out_shape=out_shape,
        compiler_params=_params(1),
    )(*args)


def mm_tn(a, b, *, name, G, out_dtype, after=None):
    order = [] if after is None else [after]
    S, Ka = a.shape
    _, N = b.shape
    Nb = N // G
    tm = min(TM_TOKENS_TN if b.dtype.itemsize == 2 else TM_TOKENS_TN // 2, S)
    tka = min(Ka, 1024)
    tnb = Nb if Nb <= 1024 else 1024
    nj = Nb // tnb
    ns = S // tm

    def body(a_ref, b_ref, *rest):
        o_ref, acc = rest[-2:]
        s = pl.program_id(2)

        @pl.when(s == 0)
        def _():
            acc[...] = jnp.zeros_like(acc)

        acc[...] += _dot(a_ref[...], b_ref[...], TN)

        @pl.when(s == ns - 1)
        def _():
            o_ref[...] = acc[...].astype(o_ref.dtype)

    return pl.pallas_call(
        body, name=name, grid=(Ka // tka, G * nj, ns),
        in_specs=[pl.BlockSpec((tm, tka), lambda i, j, s: (s, i)),
                  pl.BlockSpec((tm, tnb), lambda i, j, s: (s, j))] + [pl.BlockSpec(memory_space=pl.ANY)] * len(order),
        out_specs=pl.BlockSpec((None, tka, tnb), lambda i, j, s: (j // nj, i, j % nj)),
        out_shape=jax.ShapeDtypeStruct((G, Ka, Nb), out_dtype),
        scratch_shapes=[pltpu.VMEM((tka, tnb), F32)],
        compiler_params=_params(3),
    )(a, b, *order)


def mla_pre_fwd(a, gqa, gkva, wuq3, wukv3, gqn, gkn, cos_t, sin_t):
    S = a.shape[0]
    tm = min(TM, S)
    H = N_HEADS

    def body(a_ref, gqa_ref, gkva_ref, wuq_ref, wukv_ref, gqn_ref, gkn_ref, cos_ref, sin_ref,
             q_ref, k_ref, v_ref, cq_ref, ckv_ref):
        av = a_ref[...]
        cq = (_rms(av[:, :Q_LORA], Q_LORA)[0] * gqa_ref[...]).astype(cq_ref.dtype)
        ckv = (_rms(av[:, Q_LORA:Q_LORA + KV_LORA], KV_LORA)[0] * gkva_ref[...]).astype(ckv_ref.dtype)
        cq_ref[...] = cq
        ckv_ref[...] = ckv
        kpe = av[:, Q_LORA + KV_LORA:]
        cos_v, sin_v = cos_ref[...], sin_ref[...]
        for h in range(H):
            qn = _rms(_dot(cq, wuq_ref[h]), QK_DIM)[0] * gqn_ref[...]
            qr = jnp.concatenate([qn[:, :NOPE], _rope(qn[:, NOPE:], cos_v, sin_v)], axis=1)
            q_ref[h] = (qr * Q_PRESCALE).astype(q_ref.dtype)
            kvp = _dot(ckv, wukv_ref[h])
            kn = _rms(jnp.concatenate([kvp[:, :NOPE], kpe], axis=1), QK_DIM)[0] * gkn_ref[...]
            k_ref[h] = jnp.concatenate([kn[:, :NOPE], _rope(kn[:, NOPE:], cos_v, sin_v)], axis=1).astype(k_ref.dtype)
            v_ref[h] = kvp[:, NOPE:].astype(v_ref.dtype)

    row = lambda w: pl.BlockSpec((tm, w), lambda i: (i, 0))
    heads = lambda w: pl.BlockSpec((H, tm, w), lambda i: (0, i, 0))
    return pl.pallas_call(
        body, name="mla_pre_fwd", grid=(S // tm,),
        in_specs=[row(DOWN_PAD), _resident((1, Q_LORA)), _resident((1, KV_LORA)),
                  _resident((H, Q_LORA, QK_PAD)), _resident((H, KV_LORA, NOPE + V_DIM)),
                  _resident((1, QK_PAD)), _resident((1, QK_PAD)), row(128), row(128)],
        out_specs=[heads(QK_PAD), heads(QK_PAD), heads(V_DIM), row(Q_LORA), row(KV_LORA)],
        out_shape=[jax.ShapeDtypeStruct((H, S, QK_PAD), MXU_DTYPE),
                   jax.ShapeDtypeStruct((H, S, QK_PAD), MXU_DTYPE),
                   jax.ShapeDtypeStruct((H, S, V_DIM), MXU_DTYPE),
                   jax.ShapeDtypeStruct((S, Q_LORA), MXU_DTYPE),
                   jax.ShapeDtypeStruct((S, KV_LORA), MXU_DTYPE)],
        compiler_params=_params(1),
    )(a, gqa, gkva, wuq3, wukv3, gqn, gkn, cos_t, sin_t)


def mla_pre_bwd(dq, dk, dv, a, cq, ckv, gqa, gkva, wuq3, wukv3, gqn, gkn, cos_t, sin_t):
    S = a.shape[0]
    tm = min(TM, S)
    H = N_HEADS

    def body(dq_ref, dk_ref, dv_ref, a_ref, cq_ref, ckv_ref, gqa_ref, gkva_ref, wuq_ref, wukv_ref, gqn_ref, gkn_ref,
             cos_ref, sin_ref, da_ref, dwuq_ref, dwukv_ref, dgqn_ref, dgkn_ref, dgqa_ref, dgkva_ref):
        @pl.when(pl.program_id(0) == 0)
        def _():
            for ref in (dwuq_ref, dwukv_ref, dgqn_ref, dgkn_ref, dgqa_ref, dgkva_ref):
                ref[...] = jnp.zeros_like(ref)

        av = a_ref[...]
        kpe = av[:, Q_LORA + KV_LORA:]
        cos_v, sin_v = cos_ref[...], sin_ref[...]
        cqv, ckvv = cq_ref[...], ckv_ref[...]
        dcq = jnp.zeros((tm, Q_LORA), F32)
        dckv = jnp.zeros((tm, KV_LORA), F32)
        dkpe = jnp.zeros((tm, 128), F32)
        dgqn = jnp.zeros((1, QK_PAD), F32)
        dgkn = jnp.zeros((1, QK_PAD), F32)
        up = lambda h: (_dot(cqv, wuq_ref[h]), _dot(ckvv, wukv_ref[h]))
        nxt = up(0)
        for h in range(H):
            wuq, wukv = wuq_ref[h], wukv_ref[h]
            qp, kvp = nxt
            if h + 1 < H:
                nxt = up(h + 1)
            qhat, rq = _rms(qp, QK_DIM)
            dqr = dq_ref[h] * SM_SCALE
            dqn = jnp.concatenate([dqr[:, :NOPE], _rope_bwd(dqr[:, NOPE:], cos_v, sin_v)], axis=1)
            dqp, dg = _rms_bwd(dqn, qhat, rq, gqn_ref[...], QK_DIM)
            dgqn = dgqn + dg
            dqp = dqp.astype(MXU_DTYPE)
            dwuq_ref[h] += _dot(cqv, dqp, TN)
            dcq = dcq + _dot(dqp, wuq, NT)
            khat, rk = _rms(jnp.concatenate([kvp[:, :NOPE], kpe], axis=1), QK_DIM)
            dkr = dk_ref[h] * (1.0 / LOG2E)
            dkn = jnp.concatenate([dkr[:, :NOPE], _rope_bwd(dkr[:, NOPE:], cos_v, sin_v)], axis=1)
            dkk, dg = _rms_bwd(dkn, khat, rk, gkn_ref[...], QK_DIM)
            dgkn = dgkn + dg
            dkpe = dkpe + dkk[:, NOPE:]
            dkvp = jnp.concatenate([dkk[:, :NOPE], dv_ref[h]], axis=1).astype(MXU_DTYPE)
            dwukv_ref[h] += _dot(ckvv, dkvp, TN)
            dckv = dckv + _dot(dkvp, wukv, NT)
        dgqn_ref[...] += dgqn
        dgkn_ref[...] += dgkn
        ahat, r = _rms(av[:, :Q_LORA], Q_LORA)
        daq, dg = _rms_bwd(dcq, ahat, r, gqa_ref[...], Q_LORA)
        dgqa_ref[...] += dg
        ahat, r = _rms(av[:, Q_LORA:Q_LORA + KV_LORA], KV_LORA)
        dakv, dg = _rms_bwd(dckv, ahat, r, gkva_ref[...], KV_LORA)
        dgkva_ref[...] += dg
        da_ref[...] = jnp.concatenate([daq, dakv, dkpe], axis=1)

    row = lambda w: pl.BlockSpec((tm, w), lambda i: (i, 0))
    heads = lambda w: pl.BlockSpec((H, tm, w), lambda i: (0, i, 0))
    return pl.pallas_call(
        body, name="mla_pre_bwd", grid=(S // tm,),
        in_specs=[heads(QK_PAD), heads(QK_PAD), heads(V_DIM), row(DOWN_PAD), row(Q_LORA), row(KV_LORA),
                  _resident((1, Q_LORA)), _resident((1, KV_LORA)),
                  _resident((H, Q_LORA, QK_PAD)), _resident((H, KV_LORA, NOPE + V_DIM)),
                  _resident((1, QK_PAD)), _resident((1, QK_PAD)), row(128), row(128)],
        out_specs=[row(DOWN_PAD), _resident((H, Q_LORA, QK_PAD)), _resident((H, KV_LORA, NOPE + V_DIM)),
                   _resident((1, QK_PAD)), _resident((1, QK_PAD)), _resident((1, Q_LORA)), _resident((1, KV_LORA))],
        out_shape=[jax.ShapeDtypeStruct((S, DOWN_PAD), F32),
                   jax.ShapeDtypeStruct((H, Q_LORA, QK_PAD), F32),
                   jax.ShapeDtypeStruct((H, KV_LORA, NOPE + V_DIM), F32),
                   jax.ShapeDtypeStruct((1, QK_PAD), F32), jax.ShapeDtypeStruct((1, QK_PAD), F32),
                   jax.ShapeDtypeStruct((1, Q_LORA), F32), jax.ShapeDtypeStruct((1, KV_LORA), F32)],
        compiler_params=_params(1),
    )(dq, dk, dv, a, cq, ckv, gqa, gkva, wuq3, wukv3, gqn, gkn, cos_t, sin_t)


def _pair_tables(nb, key_major):
    if key_major:
        pairs = [(qi, kj) for kj in range(nb) for qi in range(kj, nb)]
    else:
        pairs = [(qi, ki) for qi in range(nb) for ki in range(qi + 1)]
    return (jnp.asarray(np.array([p[0] for p in pairs], np.int32)),
            jnp.asarray(np.array([p[1] for p in pairs], np.int32)))


def _scores_t(k, q, pk_col, pq_row, masked):
    s = _dot(k, q, NT)
    return jnp.where(pq_row >= pk_col, s, jnp.finfo(F32).min) if masked else s


def attn_fwd(q, k, v, pos_col, pos_row):
    H, S, _ = q.shape
    t = min(TQ, S)
    nb = S // t
    hb = HEADS_FWD
    qt, kt = _pair_tables(nb, key_major=False)

    def body(qt_ref, kt_ref, q_ref, k_ref, v_ref, pk_ref, pq_ref, o_ref, lse_ref, m_s, l_s, acc):
        step = pl.program_id(1)
        qi, ki = qt_ref[step], kt_ref[step]

        @pl.when(ki == 0)
        def _():
            m_s[...] = jnp.full_like(m_s, -jnp.inf)
            l_s[...] = jnp.zeros_like(l_s)
            acc[...] = jnp.zeros_like(acc)

        def update(masked):
            scores = lambda hh: _scores_t(k_ref[hh], q_ref[hh], pk_ref[...], pq_ref[...], masked)
            def weighted_values(hh, p, alpha):
                acc[hh] = alpha * acc[hh] + _dot(v_ref[hh], p, TN)

            s_next = scores(0)
            pending = None
            for hh in range(hb):
                s = s_next
                if hh + 1 < hb:
                    s_next = scores(hh + 1)
                m_old = m_s[hh]
                m_new = jnp.maximum(m_old, jnp.max(s, axis=0, keepdims=True))
                p = jnp.exp2(s - m_new)
                alpha = jnp.exp2(m_old - m_new)
                l_s[hh] = alpha * l_s[hh] + jnp.sum(p, axis=0, keepdims=True)
                m_s[hh] = m_new
                if pending is not None:
                    weighted_values(*pending)
                pending = (hh, p, alpha)
            weighted_values(*pending)

        @pl.when(ki < qi)
        def _():
            update(False)

        @pl.when(ki == qi)
        def _():
            update(True)
            for hh in range(hb):
                o_ref[:, hh * V_DIM:(hh + 1) * V_DIM] = (acc[hh] / l_s[hh]).T
                lse_ref[hh] = m_s[hh] + jnp.log(l_s[hh]) * LOG2E

    grid_spec = pltpu.PrefetchScalarGridSpec(
        num_scalar_prefetch=2, grid=(H // hb, qt.shape[0]),
        in_specs=[pl.BlockSpec((hb, t, QK_PAD), lambda h, s, qt, kt: (h, qt[s], 0)),
                  pl.BlockSpec((hb, t, QK_PAD), lambda h, s, qt, kt: (h, kt[s], 0)),
                  pl.BlockSpec((hb, t, V_DIM), lambda h, s, qt, kt: (h, kt[s], 0)),
                  pl.BlockSpec((t, 1), lambda h, s, qt, kt: (kt[s], 0)),
                  pl.BlockSpec((1, t), lambda h, s, qt, kt: (0, qt[s]))],
        out_specs=[pl.BlockSpec((t, hb * V_DIM), lambda h, s, qt, kt: (qt[s], h)),
                   pl.BlockSpec((hb, 1, t), lambda h, s, qt, kt: (h, 0, qt[s]))],
        scratch_shapes=[pltpu.VMEM((hb, 1, t), F32), pltpu.VMEM((hb, 1, t), F32), pltpu.VMEM((hb, V_DIM, t), F32)])
    return pl.pallas_call(
        body, name="attn_fwd", grid_spec=grid_spec,
        out_shape=[jax.ShapeDtypeStruct((S, H * V_DIM), F32), jax.ShapeDtypeStruct((H, 1, S), F32)],
        compiler_params=_params(2),
    )(qt, kt, q, k, v, pos_col, pos_row)


def attn_delta(o, do):
    S = o.shape[0]
    t = min(TQ, S)

    def body(o_ref, do_ref, d_ref):
        for h in range(N_HEADS):
            cols = slice(h * V_DIM, (h + 1) * V_DIM)
            d_ref[h] = jnp.sum((o_ref[:, cols] * do_ref[:, cols]).T, axis=0, keepdims=True)

    blk = pl.BlockSpec((t, N_HEADS * V_DIM), lambda i: (i, 0))
    return pl.pallas_call(
        body, name="attn_delta", grid=(S // t,),
        in_specs=[blk, blk],
        out_specs=pl.BlockSpec((N_HEADS, 1, t), lambda i: (0, 0, i)),
        out_shape=jax.ShapeDtypeStruct((N_HEADS, 1, S), F32),
        compiler_params=_params(1),
    )(o, do)


def attn_bwd(q, k, v, do, lse, delta, pos_col, pos_row):
    H, S, _ = q.shape
    t = min(TQ_BWD, S)
    nb = S // t
    qt, kt = _pair_tables(nb, key_major=True)
    tc = min(BWD_CHUNK, t)

    def body(qt_ref, kt_ref, q_ref, k_ref, v_ref, do_ref, lse_ref, dl_ref, pk_ref, pq_ref, dq_ref, dk_ref, dv_ref):
        step = pl.program_id(1)
        qi, kj = qt_ref[step], kt_ref[step]

        @pl.when(step == 0)
        def _():
            dq_ref[...] = jnp.zeros_like(dq_ref)

        @pl.when(qi == kj)
        def _():
            dk_ref[...] = jnp.zeros_like(dk_ref)
            dv_ref[...] = jnp.zeros_like(dv_ref)

        def update(masked):
            seen = lambda c: (c + 1) * tc if masked else t

            def first_matmuls(c):
                cols, ke = slice(c * tc, (c + 1) * tc), seen(c)
                qc = q_ref[cols, :]
                doc = do_ref[cols, :].astype(MXU_DTYPE)
                s = _scores_t(k_ref[:ke, :], qc, pk_ref[:ke, :], pq_ref[:, cols], masked)
                return qc, doc, s, _dot(v_ref[:ke, :], doc, NT)

            nxt = first_matmuls(0)
            for c in range(t // tc):
                qc, doc, s, dp = nxt
                if c + 1 < t // tc:
                    nxt = first_matmuls(c + 1)
                cols, ke = slice(c * tc, (c + 1) * tc), seen(c)
                p = jnp.exp2(s - lse_ref[:, cols])
                ds = (p * (dp - dl_ref[:, cols])).astype(MXU_DTYPE)
                dv_ref[:ke, :] += _dot(p, doc)
                dk_ref[:ke, :] += _dot(ds, qc)
                rows = pl.ds(pl.multiple_of(qi * t + c * tc, tc), tc)
                dq_ref[rows, :] += _dot(ds, k_ref[:ke, :], TN)

        @pl.when(qi == kj)
        def _():
            update(True)

        @pl.when(qi != kj)
        def _():
            update(False)

    q_idx = lambda h, s, qt, kt: (h, qt[s], 0)
    k_idx = lambda h, s, qt, kt: (h, kt[s], 0)
    row_idx = lambda h, s, qt, kt: (h, 0, qt[s])
    grid_spec = pltpu.PrefetchScalarGridSpec(
        num_scalar_prefetch=2, grid=(H, qt.shape[0]),
        in_specs=[pl.BlockSpec((None, t, QK_PAD), q_idx),
                  pl.BlockSpec((None, t, QK_PAD), k_idx),
                  pl.BlockSpec((None, t, V_DIM), k_idx),
                  pl.BlockSpec((t, V_DIM), lambda h, s, qt, kt: (qt[s], h)),
                  pl.BlockSpec((None, 1, t), row_idx),
                  pl.BlockSpec((None, 1, t), row_idx),
                  pl.BlockSpec((t, 1), lambda h, s, qt, kt: (kt[s], 0)),
                  pl.BlockSpec((1, t), lambda h, s, qt, kt: (0, qt[s]))],
        out_specs=[pl.BlockSpec((None, S, QK_PAD), lambda h, s, qt, kt: (h, 0, 0)),
                   pl.BlockSpec((None, t, QK_PAD), k_idx),
                   pl.BlockSpec((None, t, V_DIM), k_idx)])
    return pl.pallas_call(
        body, name="attn_bwd", grid_spec=grid_spec,
        out_shape=[jax.ShapeDtypeStruct((H, S, QK_PAD), F32), jax.ShapeDtypeStruct((H, S, QK_PAD), F32),
                   jax.ShapeDtypeStruct((H, S, V_DIM), F32)],
        compiler_params=_params(2),
    )(qt, kt, q, k, v, do, lse, delta, pos_col, pos_row)


def _conv_specs(S, tr):
    hb = tr // HALO
    main = lambda third: pl.BlockSpec((tr, D_MODEL), lambda r: (r, third))
    prev = lambda third: pl.BlockSpec((HALO, D_MODEL), lambda r: (jnp.maximum(r * hb - 1, 0), third))
    nxt = lambda third: pl.BlockSpec((HALO, D_MODEL), lambda r: (jnp.minimum((r + 1) * hb, S // HALO - 1), third))
    return main, prev, nxt


def _f32(ref):
    return ref[...].astype(F32)


def _conv_taps(gc, uu, w_ref, first):
    u2 = gc * uu
    rows = lax.broadcasted_iota(jnp.int32, u2.shape, 0)
    u2 = jnp.where((rows < HALO) & first, 0.0, u2)
    s1 = pltpu.roll(u2, 1, 0)
    s2 = pltpu.roll(u2, 2, 0)
    u3 = w_ref[2:3, :] * u2 + w_ref[1:2, :] * s1 + w_ref[0:1, :] * s2
    return u2, s1, s2, u3


def conv_fwd(bcu, cw):
    S = bcu.shape[0]
    tr = min(TROW, S)
    main, prev, _ = _conv_specs(S, tr)

    def body(gb_ref, gc_ref, u_ref, gch_ref, uh_ref, w_ref, z_ref):
        gc = jnp.concatenate([_f32(gch_ref), _f32(gc_ref)], axis=0)
        uu = jnp.concatenate([_f32(uh_ref), _f32(u_ref)], axis=0)
        u3 = _conv_taps(gc, uu, w_ref, pl.program_id(0) == 0)[3]
        z_ref[...] = (_f32(gb_ref) * u3[HALO:]).astype(z_ref.dtype)

    return pl.pallas_call(
        body, name="conv_fwd", grid=(S // tr,),
        in_specs=[main(0), main(1), main(2), prev(1), prev(2), _resident((3, D_MODEL))],
        out_specs=main(0),
        out_shape=jax.ShapeDtypeStruct((S, D_MODEL), MXU_DTYPE),
        compiler_params=_params(1),
    )(bcu, bcu, bcu, bcu, bcu, cw)


def conv_bwd(dz, bcu, cw):
    S = bcu.shape[0]
    tr = min(TROW, S)
    nr = S // tr
    main, prev, nxt = _conv_specs(S, tr)

    def body(dz_ref, dzn_ref, gb_ref, gbn_ref, gc_ref, u_ref, gch_ref, uh_ref, w_ref, o_ref, dw_ref):
        r = pl.program_id(0)
        gcv, uv = _f32(gc_ref), _f32(u_ref)
        gc = jnp.concatenate([_f32(gch_ref), gcv], axis=0)
        uu = jnp.concatenate([_f32(uh_ref), uv], axis=0)
        u2, s1, s2, u3 = _conv_taps(gc, uu, w_ref, r == 0)
        dzv = dz_ref[...]
        du3 = jnp.concatenate([dzv * _f32(gb_ref), dzn_ref[...] * _f32(gbn_ref)], axis=0)
        rows = lax.broadcasted_iota(jnp.int32, du3.shape, 0)
        du3 = jnp.where((rows >= tr) & (r == nr - 1), 0.0, du3)
        n1 = pltpu.roll(du3, tr + HALO - 1, 0)
        n2 = pltpu.roll(du3, tr + HALO - 2, 0)
        du2 = (w_ref[2:3, :] * du3 + w_ref[1:2, :] * n1 + w_ref[0:1, :] * n2)[:tr]
        o_ref[:, :D_MODEL] = (dzv * u3[HALO:]).astype(o_ref.dtype)
        o_ref[:, D_MODEL:2 * D_MODEL] = (du2 * uv).astype(o_ref.dtype)
        o_ref[:, 2 * D_MODEL:] = (du2 * gcv).astype(o_ref.dtype)
        d3 = du3[:tr]
        taps = [jnp.sum(d3 * t[HALO:], axis=0, keepdims=True) for t in (s2, s1, u2)]

        @pl.when(r == 0)
        def _():
            for kk in range(3):
                dw_ref[kk:kk + 1, :] = taps[kk]

        @pl.when(r != 0)
        def _():
            for kk in range(3):
                dw_ref[kk:kk + 1, :] += taps[kk]

    return pl.pallas_call(
        body, name="conv_bwd", grid=(nr,),
        in_specs=[main(0), nxt(0), main(0), nxt(0), main(1), main(2), prev(1), prev(2), _resident((3, D_MODEL))],
        out_specs=[pl.BlockSpec((tr, 3 * D_MODEL), lambda r: (r, 0)), _resident((3, D_MODEL))],
        out_shape=[jax.ShapeDtypeStruct((S, 3 * D_MODEL), MXU_DTYPE), jax.ShapeDtypeStruct((3, D_MODEL), F32)],
        compiler_params=_params(1),
    )(dz, dz, bcu, bcu, bcu, bcu, bcu, bcu, cw)


def loss_head(y, target):
    S, D = y.shape
    tm = min(TM, S)

    def body(y_ref, t_ref, l_ref, dy_ref):
        err = y_ref[...] - t_ref[...]
        dy_ref[...] = err / D
        part = jnp.full((1, 128), jnp.sum(err * err), F32)

        @pl.when(pl.program_id(0) == 0)
        def _():
            l_ref[...] = part

        @pl.when(pl.program_id(0) != 0)
        def _():
            l_ref[...] += part

    blk = pl.BlockSpec((tm, D), lambda i: (i, 0))
    return pl.pallas_call(
        body, name="loss_head", grid=(S // tm,),
        in_specs=[blk, blk],
        out_specs=[pl.BlockSpec((1, 128), lambda i: (0, 0)), blk],
        out_shape=[jax.ShapeDtypeStruct((1, 128), F32), jax.ShapeDtypeStruct((S, D), F32)],
        compiler_params=_params(1),
    )(y, target)


def adamw(parts, w, m, v, *, name):
    R, C = w.shape
    tr = R
    while tr * C * 4 > (1 << 20) and tr % 32 == 0:
        tr //= 2

    def body(p_ref, w_ref, m_ref, v_ref, g_ref, d_ref, mo_ref, vo_ref):
        g = p_ref[0].astype(F32)
        for d in range(1, N_DEV):
            g = g + p_ref[d].astype(F32)
        m_new = ADAM_B1 * m_ref[...] + (1.0 - ADAM_B1) * g
        v_new = ADAM_B2 * v_ref[...] + (1.0 - ADAM_B2) * (g * g)
        m_hat = m_new / (1.0 - ADAM_B1 ** ADAM_STEP)
        v_hat = v_new / (1.0 - ADAM_B2 ** ADAM_STEP)
        g_ref[...] = g
        d_ref[...] = -ADAM_LR * (m_hat / (jnp.sqrt(v_hat) + ADAM_EPS) + ADAM_WD * w_ref[...])
        mo_ref[...] = m_new
        vo_ref[...] = v_new

    blk = pl.BlockSpec((tr, C), lambda i: (i, 0))
    return pl.pallas_call(
        body, name=name, grid=(R // tr,),
        in_specs=[pl.BlockSpec((N_DEV, tr, C), lambda i: (0, i, 0)), blk, blk, blk],
        out_specs=[blk, blk, blk, blk],
        out_shape=[jax.ShapeDtypeStruct((R, C), F32)] * 4,
        compiler_params=_params(1),
    )(parts, w, m, v)


def _mesh_place():
    x, y, c = (lax.axis_index(n) for n in MESH_AXES)
    return x, y, c, 4 * x + 2 * y + c


def _peer(x, y, c, d):
    px = 1 - x if d & 4 else x
    py = 1 - y if d & 2 else y
    pc = 1 - c if d & 1 else c
    return (px, py, pc), 4 * px + 2 * py + pc


def _exchange(srcs, name, scatter):
    n = len(srcs)
    any_spec = pl.BlockSpec(memory_space=pl.ANY)

    def body(*refs):
        ins, outs, token = refs[:n], refs[n:2 * n], refs[2 * n]
        send_sems, recv_sems, local_sems = refs[2 * n + 1:]
        token[...] = jnp.zeros_like(token)
        x, y, c, me = _mesh_place()
        for a in range(n):
            mine = ins[a].at[me] if scatter else ins[a]
            pltpu.make_async_copy(mine, outs[a].at[me], local_sems.at[a]).start()
            for d in range(1, N_DEV):
                peer, peer_lin = _peer(x, y, c, d)
                pltpu.make_async_remote_copy(
                    src_ref=ins[a].at[peer_lin] if scatter else ins[a], dst_ref=outs[a].at[me],
                    send_sem=send_sems.at[a], recv_sem=recv_sems.at[a],
                    device_id=peer, device_id_type=pl.DeviceIdType.MESH).start()
        for a in range(n):
            mine = ins[a].at[me] if scatter else ins[a]
            pltpu.make_async_copy(mine, outs[a].at[me], local_sems.at[a]).wait()
            seven = outs[a].at[pl.ds(0, N_DEV - 1)]
            drain = pltpu.make_async_remote_copy(
                src_ref=seven, dst_ref=seven, send_sem=send_sems.at[a], recv_sem=recv_sems.at[a],
                device_id=(x, y, c), device_id_type=pl.DeviceIdType.MESH)
            drain.wait_send()
            drain.wait_recv()

    block = (lambda s: s.shape[1:]) if scatter else (lambda s: s.shape)
    out = pl.pallas_call(
        body, name=name,
        in_specs=[any_spec] * n, out_specs=[any_spec] * n + [pl.BlockSpec(memory_space=pltpu.VMEM)],
        out_shape=[jax.ShapeDtypeStruct((N_DEV,) + tuple(block(s)), s.dtype) for s in srcs]
        + [jax.ShapeDtypeStruct((8, 128), F32)],
        scratch_shapes=[pltpu.SemaphoreType.DMA((n,)), pltpu.SemaphoreType.DMA((n,)), pltpu.SemaphoreType.DMA((n,))],
    )(*srcs)
    return out[:n], out[n]


_ANY = pl.BlockSpec(memory_space=pl.ANY)
_HBM = pl.BlockSpec(memory_space=pltpu.HBM)
_SEM = pl.BlockSpec(memory_space=pltpu.SEMAPHORE)


def _in_hbm(arrays):
    return [pltpu.with_memory_space_constraint(a, pltpu.HBM) for a in arrays]


def place_own(shards, after, *, name):
    n = len(shards)

    def body(*refs):
        ins, outs, sems = refs[:n], refs[n + 1:2 * n + 1], refs[2 * n + 1]
        me = _mesh_place()[3]
        copies = [pltpu.make_async_copy(ins[a], outs[a].at[me], sems.at[a]) for a in range(n)]
        for cp in copies:
            cp.start()
        for cp in copies:
            cp.wait()

    return pl.pallas_call(
        body, name=name, in_specs=[_ANY] * (n + 1), out_specs=[_ANY] * n,
        out_shape=[jax.ShapeDtypeStruct((N_DEV,) + s.shape, s.dtype) for s in shards],
        scratch_shapes=[pltpu.SemaphoreType.DMA((n,))],
    )(*shards, after)


def exchange_start(srcs, lands, slots, *, name):
    n, m = len(srcs), len(lands)

    def body(*refs):
        ins, zones = refs[:n], refs[n:n + m]
        send_sems, recv_sems, token = refs[n + m], refs[n + m + 1], refs[-1]
        x, y, c, me = _mesh_place()
        for a in range(n):
            for d in range(1, N_DEV):
                peer, peer_lin = _peer(x, y, c, d)
                src = ins[a] if slots is None else ins[a].at[peer_lin]
                dst = zones[a].at[me] if slots is None else zones[slots[a][0]].at[me, slots[a][1]]
                pltpu.make_async_remote_copy(
                    src_ref=src, dst_ref=dst, send_sem=send_sems.at[a], recv_sem=recv_sems.at[a],
                    device_id=peer, device_id_type=pl.DeviceIdType.MESH).start()
        token[...] = jnp.zeros_like(token)

    both = list(srcs) + list(lands)
    out = pl.pallas_call(
        body, name=name,
        in_specs=[_HBM] * (n + m),
        out_specs=[_SEM, _SEM] + [_HBM] * (n + m) + [pl.BlockSpec(memory_space=pltpu.VMEM)],
        out_shape=[pltpu.SemaphoreType.DMA((n,)), pltpu.SemaphoreType.DMA((n,))]
        + [pltpu.HBM(a.shape, a.dtype) for a in both] + [jax.ShapeDtypeStruct((8, 128), F32)],
        input_output_aliases={i: 2 + i for i in range(n + m)},
        compiler_params=pltpu.CompilerParams(has_side_effects=pltpu.SideEffectType.DATAFLOW_SIDE_EFFECTING),
    )(*_in_hbm(both))
    return out[0], out[1], out[2:2 + n], out[2 + n:2 + n + m], out[-1]


def exchange_wait(send_sems, recv_sems, srcs, lands, slots, after, *, name):
    n, m = len(srcs), len(lands)

    def body(*refs):
        ins, zones = refs[:n], refs[n:n + m]
        send_ref, recv_ref = refs[n + m], refs[n + m + 1]
        x, y, c, _ = _mesh_place()
        for a in range(n):
            seven = (zones[a] if slots is None else ins[a]).at[pl.ds(0, N_DEV - 1)]
            drain = pltpu.make_async_remote_copy(
                src_ref=seven, dst_ref=seven, send_sem=send_ref.at[a], recv_sem=recv_ref.at[a],
                device_id=(x, y, c), device_id_type=pl.DeviceIdType.MESH)
            drain.wait_send()
            drain.wait_recv()

    both = list(srcs) + list(lands)
    out = pl.pallas_call(
        body, name=name,
        in_specs=[_HBM] * (n + m) + [_SEM, _SEM, _ANY],
        out_specs=[_HBM] * (n + m),
        out_shape=[pltpu.HBM(a.shape, a.dtype) for a in both],
        input_output_aliases={i: i for i in range(n + m)},
        compiler_params=pltpu.CompilerParams(has_side_effects=pltpu.SideEffectType.DATAFLOW_SIDE_EFFECTING),
    )(*both, send_sems, recv_sems, after)
    return out[:n], out[n:]


def scatter_finish(remote, local, lands, vec, *, name):
    every = list(remote) + list(local)
    n, r, m = len(every), len(remote), len(lands)

    def body(*refs):
        ins, vec_ref, zones_in = refs[:n], refs[n], refs[n + 1:n + 1 + m]
        vec_out = refs[n + 1 + 2 * m]
        send_sems, recv_sems, local_sems = refs[n + 2 + 2 * m:]
        x, y, c, me = _mesh_place()

        def own(a):
            if a == n:
                return pltpu.make_async_copy(vec_ref, vec_out.at[me], local_sems.at[a])
            return pltpu.make_async_copy(ins[a].at[me], zones_in[every[a][1]].at[me, every[a][2]], local_sems.at[a])

        for a in range(n + 1):
            own(a).start()
        for a in list(range(r)) + [n]:
            for d in range(1, N_DEV):
                peer, peer_lin = _peer(x, y, c, d)
                src = vec_ref if a == n else ins[a].at[peer_lin]
                dst = vec_out.at[me] if a == n else zones_in[every[a][1]].at[me, every[a][2]]
                pltpu.make_async_remote_copy(
                    src_ref=src, dst_ref=dst, send_sem=send_sems.at[min(a, r)], recv_sem=recv_sems.at[min(a, r)],
                    device_id=peer, device_id_type=pl.DeviceIdType.MESH).start()
        for a in range(n + 1):
            own(a).wait()
        for a in list(range(r)) + [n]:
            seven = (vec_out if a == n else ins[a]).at[pl.ds(0, N_DEV - 1)]
            drain = pltpu.make_async_remote_copy(
                src_ref=seven, dst_ref=seven, send_sem=send_sems.at[min(a, r)], recv_sem=recv_sems.at[min(a, r)],
                device_id=(x, y, c), device_id_type=pl.DeviceIdType.MESH)
            drain.wait_send()
            drain.wait_recv()

    out = pl.pallas_call(
        body, name=name,
        in_specs=[_ANY] * (n + 1 + m), out_specs=[_ANY] * (m + 1),
        out_shape=[jax.ShapeDtypeStruct(z.shape, z.dtype) for z in lands]
        + [jax.ShapeDtypeStruct((N_DEV,) + vec.shape, vec.dtype)],
        input_output_aliases={n + 1 + i: i for i in range(m)},
        scratch_shapes=[pltpu.SemaphoreType.DMA((r + 1,)), pltpu.SemaphoreType.DMA((r + 1,)),
                        pltpu.SemaphoreType.DMA((n + 1,))],
    )(*[e[0] for e in every], vec, *lands)
    return out[:m], out[m]


def _rope_tables(pos):
    inv_freq = ROPE_THETA ** (-jnp.arange(0, ROPE, 2, dtype=F32) / ROPE)
    ang = pos.astype(F32)[:, None] * inv_freq
    cos, sin = jnp.cos(ang), jnp.sin(ang)
    pad = jnp.zeros((pos.shape[0], 128 - ROPE), F32)
    return jnp.concatenate([cos, cos, pad + 1.0], axis=1), jnp.concatenate([-sin, sin, pad], axis=1)


def _pad_last(w, n):
    return jnp.pad(w, [(0, 0)] * (w.ndim - 1) + [(0, n - w.shape[-1])])


def kernel(x, positions, g_mix, g_mlp, attn_w_down, attn_g_q_a, attn_g_kv_a, attn_w_uq, attn_w_ukv, attn_g_qnorm, attn_g_knorm, attn_w_o, conv_w_in, conv_w, conv_w_out, mlp_w1, mlp_w2, loss_target, m_g_mix, m_g_mlp, m_attn_w_down, m_attn_g_q_a, m_attn_g_kv_a, m_attn_w_uq, m_attn_w_ukv, m_attn_g_qnorm, m_attn_g_knorm, m_attn_w_o, m_conv_w_in, m_conv_w, m_conv_w_out, m_mlp_w1, m_mlp_w2, v_g_mix, v_g_mlp, v_attn_w_down, v_attn_g_q_a, v_attn_g_kv_a, v_attn_w_uq, v_attn_w_ukv, v_attn_g_qnorm, v_attn_g_knorm, v_attn_w_o, v_conv_w_in, v_conv_w, v_conv_w_out, v_mlp_w1, v_mlp_w2):
    weights = dict(g_mix=g_mix, g_mlp=g_mlp, attn_w_down=attn_w_down, attn_g_q_a=attn_g_q_a, attn_g_kv_a=attn_g_kv_a,
                   attn_w_uq=attn_w_uq, attn_w_ukv=attn_w_ukv, attn_g_qnorm=attn_g_qnorm, attn_g_knorm=attn_g_knorm,
                   attn_w_o=attn_w_o, conv_w_in=conv_w_in, conv_w=conv_w, conv_w_out=conv_w_out, mlp_w1=mlp_w1, mlp_w2=mlp_w2)
    mom1 = dict(g_mix=m_g_mix, g_mlp=m_g_mlp, attn_w_down=m_attn_w_down, attn_g_q_a=m_attn_g_q_a, attn_g_kv_a=m_attn_g_kv_a,
                attn_w_uq=m_attn_w_uq, attn_w_ukv=m_attn_w_ukv, attn_g_qnorm=m_attn_g_qnorm, attn_g_knorm=m_attn_g_knorm,
                attn_w_o=m_attn_w_o, conv_w_in=m_conv_w_in, conv_w=m_conv_w, conv_w_out=m_conv_w_out, mlp_w1=m_mlp_w1, mlp_w2=m_mlp_w2)
    mom2 = dict(g_mix=v_g_mix, g_mlp=v_g_mlp, attn_w_down=v_attn_w_down, attn_g_q_a=v_attn_g_q_a, attn_g_kv_a=v_attn_g_kv_a,
                attn_w_uq=v_attn_w_uq, attn_w_ukv=v_attn_w_ukv, attn_g_qnorm=v_attn_g_qnorm, attn_g_knorm=v_attn_g_knorm,
                attn_w_o=v_attn_w_o, conv_w_in=v_conv_w_in, conv_w=v_conv_w, conv_w_out=v_conv_w_out, mlp_w1=v_mlp_w1, mlp_w2=v_mlp_w2)
    big = ["attn_w_down", "attn_w_uq", "attn_w_ukv", "attn_w_o", "conv_w_in", "conv_w", "conv_w_out", "mlp_w1", "mlp_w2"]
    small = ["g_mix", "g_mlp", "attn_g_q_a", "attn_g_kv_a", "attn_g_qnorm", "attn_g_knorm"]
    order = ["g_mix", "g_mlp", "attn_w_down", "attn_g_q_a", "attn_g_kv_a", "attn_w_uq", "attn_w_ukv", "attn_g_qnorm",
             "attn_g_knorm", "attn_w_o", "conv_w_in", "conv_w", "conv_w_out", "mlp_w1", "mlp_w2"]

    xs = x[0]
    pos = positions[0]
    target = loss_target[0]
    S = xs.shape[0]
    depth = g_mix.shape[0]
    cos_t, sin_t = _rope_tables(pos)
    pos_col, pos_row = pos.reshape(S, 1), pos.reshape(1, S)

    keys, shards = [], []
    for name in big:
        for l in range(weights[name].shape[0]):
            keys.append((name, l))
            shards.append(weights[name][l] if name == "conv_w" else weights[name][l].astype(WIRE_DTYPE))
    first = [j for j, (name, l) in enumerate(keys) if l == 0 and name.startswith("attn")]
    second = [j for j, (name, l) in enumerate(keys) if l == 0 and name.startswith("mlp")]
    later = [j for j in range(len(keys)) if j not in first + second]
    me = 4 * lax.axis_index("x") + 2 * lax.axis_index("y") + lax.axis_index("c")

    def zones_with_own(js, token):
        return [lax.dynamic_update_slice(lax.empty((N_DEV,) + shards[j].shape, shards[j].dtype),
                                         (shards[j] + token[0, 0].astype(shards[j].dtype))[None],
                                         (me,) + (0,) * shards[j].ndim) for j in js]

    arrived, token = _exchange([shards[j] for j in first], "gather_first", scatter=False)
    full = dict(zip([keys[j] for j in first], arrived))
    g1 = exchange_start([shards[j] for j in second], zones_with_own(second, token), None, name="gather_mlp0_start")
    g2 = exchange_start([shards[j] for j in later], zones_with_own(later, g1[4]), None, name="gather_rest_start")
    g_mix_0 = g_mix[0] + g2[4][0, 0]

    def rows(name, l):
        g = full[(name, l)]
        return g.reshape(g.shape[0] * g.shape[1], g.shape[2])

    saved = []
    for i in range(depth):
        l = i // 2
        rec = {"x0": xs}
        if i == 1:
            arrived = exchange_wait(*g2[:4], None, xs, name="gather_rest_wait")[1]
            full.update(zip([keys[j] for j in later], arrived))
        if i % 2 == 0:
            wd3 = _pad_last(rows("attn_w_down", l), DOWN_PAD)[None]
            wuq3 = _pad_last(full[("attn_w_uq", l)], QK_PAD)
            wukv3 = full[("attn_w_ukv", l)]
            gqn = _pad_last(attn_g_qnorm[l][None], QK_PAD)
            gkn = _pad_last(attn_g_knorm[l][None], QK_PAD)
            gqa, gkva = attn_g_q_a[l][None], attn_g_kv_a[l][None]
            h, a = norm_matmul(xs, g_mix_0 if i == 0 else g_mix[i], wd3, name="attn_down")
            q, k, v, cq, ckv = mla_pre_fwd(a, gqa, gkva, wuq3, wukv3, gqn, gkn, cos_t, sin_t)
            o, lse = attn_fwd(q, k, v, pos_col, pos_row)
            x1 = matmul_residual(o, rows("attn_w_o", l), xs, name="attn_out")
            rec.update(h=h, a=a, q=q, k=k, v=v, cq=cq, ckv=ckv, o=o, lse=lse, wd3=wd3, wuq3=wuq3, wukv3=wukv3,
                       gqn=gqn, gkn=gkn, gqa=gqa, gkva=gkva)
        else:
            cw = full[("conv_w", l)].transpose(1, 0, 2).reshape(3, D_MODEL)
            h, bcu = norm_matmul(xs, g_mix[i], full[("conv_w_in", l)], name="conv_in", out_dtype=MXU_DTYPE)
            z = conv_fwd(bcu, cw)
            x1 = matmul_residual(z, rows("conv_w_out", l), xs, name="conv_out")
            rec.update(h=h, bcu=bcu, z=z, cw=cw)
        if i == 0:
            arrived = exchange_wait(*g1[:4], None, x1, name="gather_mlp0_wait")[1]
            full.update(zip([keys[j] for j in second], arrived))
        h2, act = norm_matmul(x1, g_mlp[i], full[("mlp_w1", i)], name="mlp_up", mlp=True)
        xs = matmul_residual(act, rows("mlp_w2", i), x1, name="mlp_down")
        rec.update(x1=x1, h2=h2, act=act)
        saved.append(rec)

    sq, dx = loss_head(xs, target)
    loss = lax.psum(sq[0, 0] * (0.5 / D_MODEL), MESH_AXES)

    grads = {name: [None] * weights[name].shape[0] for name in order}
    token = None
    for i in reversed(range(depth)):
        l = i // 2
        rec = saved[i]
        grads["mlp_w2"][i] = mm_tn(rec["act"], dx, name="mlp_down_dw", G=1, out_dtype=WIRE_DTYPE).reshape(N_DEV, -1, D_MODEL)
        du = matmul_nt(dx, rows("mlp_w2", i)[None], name="mlp_down_dx", epi="mlp_du", u=rec["act"])
        grads["mlp_w1"][i] = mm_tn(rec["h2"], du, name="mlp_up_dw", G=N_DEV, out_dtype=WIRE_DTYPE)
        dx1, dg = matmul_nt(du, full[("mlp_w1", i)], name="mlp_up_dx", epi="rms_bwd", x=rec["x1"], g=g_mlp[i], dx=dx)
        grads["g_mlp"][i] = dg[0]
        if i == 0:
            flying = [keys[j] for j in second + later]
            srcs = [grads[name][l_] for name, l_ in flying]
            slots = [(big.index(name), l_) for name, l_ in flying]
            zones = [lax.empty((N_DEV, weights[name].shape[0]) + grads[name][-1].shape[1:], grads[name][-1].dtype)
                     for name in big]
            for src, (k, l_) in zip(srcs, slots):
                own = lax.dynamic_index_in_dim(src, me, 0, keepdims=True)[None]
                zones[k] = lax.dynamic_update_slice(zones[k], own, (me, l_) + (0,) * (src.ndim - 1))
            s_send, s_recv, s_srcs, s_zones, token = exchange_start(srcs, zones, slots, name="scatter_rest_start")
        if i % 2 == 0:
            grads["attn_w_o"][l] = mm_tn(rec["o"], dx1, name="attn_out_dw", G=1, out_dtype=WIRE_DTYPE,
                                         after=token).reshape(N_DEV, -1, D_MODEL)
            do = matmul_nt(dx1, rows("attn_w_o", l)[None], name="attn_out_dx")
            dq, dk, dv = attn_bwd(rec["q"], rec["k"], rec["v"], do, rec["lse"], attn_delta(rec["o"], do), pos_col, pos_row)
            da, dwuq, dwukv, dgqn, dgkn, dgqa, dgkva = mla_pre_bwd(
                dq, dk, dv, rec["a"], rec["cq"], rec["ckv"], rec["gqa"], rec["gkva"], rec["wuq3"], rec["wukv3"],
                rec["gqn"], rec["gkn"], cos_t, sin_t)
            grads["attn_w_uq"][l] = dwuq[:, :, :QK_DIM].astype(WIRE_DTYPE)
            grads["attn_w_ukv"][l] = dwukv.astype(WIRE_DTYPE)
            grads["attn_g_qnorm"][l] = dgqn[0, :QK_DIM]
            grads["attn_g_knorm"][l] = dgkn[0, :QK_DIM]
            grads["attn_g_q_a"][l] = dgqa[0]
            grads["attn_g_kv_a"][l] = dgkva[0]
            dwd = mm_tn(rec["h"], da, name="attn_down_dw", G=1, out_dtype=WIRE_DTYPE)
            grads["attn_w_down"][l] = dwd[0, :, :DOWN].reshape(N_DEV, -1, DOWN)
            dx, dg = matmul_nt(da, rec["wd3"], name="attn_down_dx", epi="rms_bwd", x=rec["x0"], g=g_mix[i], dx=dx1)
        else:
            grads["conv_w_out"][l] = mm_tn(rec["z"], dx1, name="conv_out_dw", G=1, out_dtype=WIRE_DTYPE).reshape(N_DEV, -1, D_MODEL)
            dz = matmul_nt(dx1, rows("conv_w_out", l)[None], name="conv_out_dx")
            dbcu, dcw = conv_bwd(dz, rec["bcu"], rec["cw"])
            grads["conv_w"][l] = dcw.reshape(3, N_DEV, -1).transpose(1, 0, 2)
            grads["conv_w_in"][l] = mm_tn(rec["h"], dbcu, name="conv_in_dw", G=N_DEV, out_dtype=WIRE_DTYPE)
            dx, dg = matmul_nt(dbcu, full[("conv_w_in", l)], name="conv_in_dx", epi="rms_bwd", x=rec["x0"], g=g_mix[i], dx=dx1)
        grads["g_mix"][i] = dg[0]

    sizes = [weights[name].size for name in small]
    n_small = sum(sizes)
    rows_small = -(-n_small // (8 * 128)) * 8

    def pack(tree):
        flat = jnp.concatenate([jnp.stack(tree[name]).reshape(-1) if isinstance(tree[name], list) else tree[name].reshape(-1)
                                for name in small])
        return jnp.pad(flat, (0, rows_small * 128 - n_small)).reshape(rows_small, 128)

    s_srcs, s_zones = exchange_wait(s_send, s_recv, s_srcs, s_zones, slots, dx, name="scatter_rest_wait")
    remote = [(grads[name][l_], big.index(name), l_) for name, l_ in (keys[j] for j in first)]
    parts, gain_parts = scatter_finish(remote, [], s_zones, pack(grads), name="scatter_last")

    out = {}
    for name, part in zip(big, parts):
        w = weights[name]
        flat = lambda t: t.reshape(-1, t.shape[-1])
        res = adamw(part.reshape(N_DEV, -1, w.shape[-1]), flat(w), flat(mom1[name]), flat(mom2[name]), name="adamw_" + name)
        out[name] = [r.reshape(w.shape) for r in res]
    res = adamw(gain_parts, pack(weights), pack(mom1), pack(mom2), name="adamw_gains")
    offset = 0
    for name, size in zip(small, sizes):
        out[name] = [r.reshape(-1)[offset:offset + size].reshape(weights[name].shape) for r in res]
        offset += size

    return (loss, dx[None], *[out[n][0] for n in order], *[out[n][1] for n in order],
            *[out[n][2] for n in order], *[out[n][3] for n in order])
```

```python
import jax
import jax.numpy as jnp
import numpy as np
from jax import lax
from jax.experimental import pallas as pl
from jax.experimental.pallas import tpu as pltpu

F32 = jnp.float32
MXU_DTYPE = jnp.bfloat16
WIRE_DTYPE = jnp.bfloat16

D_MODEL = 1024
N_HEADS = 8
NOPE = 128
ROPE = 64
QK_DIM = NOPE + ROPE
QK_PAD = 256
V_DIM = 128
Q_LORA = 256
KV_LORA = 128
DOWN = Q_LORA + KV_LORA + ROPE
DOWN_PAD = 512
ROPE_THETA = 10000.0
EPS = 1e-6
SM_SCALE = QK_DIM ** -0.5
LOG2E = 1.4426950408889634
Q_PRESCALE = SM_SCALE * LOG2E
ADAM_LR, ADAM_B1, ADAM_B2, ADAM_EPS, ADAM_WD, ADAM_STEP = 0.001, 0.9, 0.999, 1e-08, 0.01, 10
N_DEV = 8
MESH_AXES = ("x", "y", "c")

TM = 512
TM_WIDE = 1024
TILE_BUDGET = 32 << 20
TM_TOKENS_TN = 2048
TQ = 512
HEADS_FWD = 8
TQ_BWD = 2048
BWD_CHUNK = 256
TROW = 256
HALO = 16
VMEM_LIMIT = 48 << 20

NN = (((1,), (0,)), ((), ()))
NT = (((1,), (1,)), ((), ()))
TN = (((0,), (0,)), ((), ()))


def _dot(a, b, dims=NN):
    return lax.dot_general(a.astype(MXU_DTYPE), b.astype(MXU_DTYPE), dims, preferred_element_type=F32)


def _params(n_axes):
    return pltpu.CompilerParams(dimension_semantics=("arbitrary",) * n_axes, vmem_limit_bytes=VMEM_LIMIT)


def _rms(xv, n):
    r = lax.rsqrt(jnp.sum(xv * xv, axis=-1, keepdims=True) / n + EPS)
    return xv * r, r


def _rms_bwd(dy, xhat, r, g, n):
    dg = jnp.sum(dy * xhat, axis=0, keepdims=True)
    dxh = dy * g
    dx = r * (dxh - xhat * (jnp.sum(dxh * xhat, axis=-1, keepdims=True) / n))
    return dx, dg


def _swap_halves(t):
    lane = lax.broadcasted_iota(jnp.int32, t.shape, 1)
    return jnp.where(lane < ROPE // 2, pltpu.roll(t, 128 - ROPE // 2, 1), pltpu.roll(t, ROPE // 2, 1))


def _rope(t, cos_t, sin_t):
    return t * cos_t + _swap_halves(t) * sin_t


def _rope_bwd(dout, cos_t, sin_t):
    return dout * cos_t + _swap_halves(dout * sin_t)


def _resident(shape):
    return pl.BlockSpec(shape, lambda i: (0,) * len(shape))


def _token_tile(S, row_bytes, resident_bytes):
    wide = min(TM_WIDE, S)
    return wide if 2 * (wide * row_bytes + resident_bytes) <= TILE_BUDGET else min(TM, S)


def norm_matmul(x, g, w3, *, name, mlp=False, out_dtype=F32):
    if mlp:
        out_dtype = MXU_DTYPE
    S, D = x.shape
    G, _, Nb = w3.shape
    N = G * Nb
    tm = _token_tile(S, D * 4 + D * 2 + N * jnp.dtype(out_dtype).itemsize, w3.size * w3.dtype.itemsize)

    def body(x_ref, g_ref, w_ref, h_ref, o_ref):
        xv = x_ref[...]
        r = lax.rsqrt(jnp.mean(xv * xv, axis=-1, keepdims=True) + EPS)
        h = (xv * r * g_ref[...]).astype(h_ref.dtype)
        h_ref[...] = h
        for gi in range(G):
            acc = _dot(h, w_ref[gi])
            if mlp:
                acc = jnp.square(jnp.maximum(acc, 0.0))
            o_ref[:, gi * Nb:(gi + 1) * Nb] = acc.astype(o_ref.dtype)

    rows = lambda w: pl.BlockSpec((tm, w), lambda i: (i, 0))
    return pl.pallas_call(
        body, name=name, grid=(S // tm,),
        in_specs=[rows(D), _resident((1, D)), _resident((G, D, Nb))],
        out_specs=[rows(D), rows(N)],
        out_shape=[jax.ShapeDtypeStruct((S, D), MXU_DTYPE), jax.ShapeDtypeStruct((S, N), out_dtype)],
        compiler_params=_params(1),
    )(x, g.reshape(1, D), w3)


def matmul_residual(a, w, res, *, name):
    S, K = a.shape
    _, N = w.shape
    tm = _token_tile(S, K * a.dtype.itemsize + 2 * N * 4, w.size * w.dtype.itemsize)

    def body(a_ref, w_ref, r_ref, o_ref):
        o_ref[...] = r_ref[...] + _dot(a_ref[...], w_ref[...])

    rows = lambda w_: pl.BlockSpec((tm, w_), lambda i: (i, 0))
    return pl.pallas_call(
        body, name=name, grid=(S // tm,),
        in_specs=[rows(K), _resident((K, N)), rows(N)],
        out_specs=rows(N), out_shape=jax.ShapeDtypeStruct((S, N), F32),
        compiler_params=_params(1),
    )(a, w, res)


def matmul_nt(a, w3, *, name, epi="plain", u=None, x=None, g=None, dx=None):
    S, N = a.shape
    G, Ko, Nb = w3.shape
    assert N == G * Nb
    row_bytes = N * a.dtype.itemsize + Ko * {"plain": 4, "mlp_du": 4, "rms_bwd": 12}[epi]
    tm = _token_tile(S, row_bytes, w3.size * w3.dtype.itemsize)
    tko = min(Ko, 512)

    def body(a_ref, w_ref, *rest):
        if epi == "mlp_du":
            u_ref, o_ref = rest
            av = a_ref[...].astype(MXU_DTYPE)
            for j in range(Ko // tko):
                cols = slice(j * tko, (j + 1) * tko)
                da = _dot(av, w_ref[0, cols, :], NT)
                o_ref[:, cols] = (da * (2.0 * jnp.sqrt(u_ref[:, cols].astype(F32)))).astype(o_ref.dtype)
            return
        acc = _dot(a_ref[:, :Nb], w_ref[0], NT)
        for gi in range(1, G):
            acc = acc + _dot(a_ref[:, gi * Nb:(gi + 1) * Nb], w_ref[gi], NT)
        if epi == "plain":
            rest[0][...] = acc
        else:
            x_ref, g_ref, dx_ref, o_ref, dg_ref = rest
            xhat, r = _rms(x_ref[...], Ko)
            dxb, dg = _rms_bwd(acc, xhat, r, g_ref[...], Ko)
            o_ref[...] = dx_ref[...] + dxb

            @pl.when(pl.program_id(0) == 0)
            def _():
                dg_ref[...] = dg

            @pl.when(pl.program_id(0) != 0)
            def _():
                dg_ref[...] += dg

    rows = lambda w_: pl.BlockSpec((tm, w_), lambda i: (i, 0))
    in_specs = [rows(N), _resident((G, Ko, Nb))]
    args = [a, w3]
    if epi == "plain":
        out_shape, out_specs = jax.ShapeDtypeStruct((S, Ko), F32), rows(Ko)
    elif epi == "mlp_du":
        in_specs.append(rows(Ko))
        args.append(u)
        out_shape, out_specs = jax.ShapeDtypeStruct((S, Ko), MXU_DTYPE), rows(Ko)
    else:
        in_specs += [rows(Ko), _resident((1, Ko)), rows(Ko)]
        args += [x, g.reshape(1, Ko), dx]
        out_shape = [jax.ShapeDtypeStruct((S, Ko), F32), jax.ShapeDtypeStruct((1, Ko), F32)]
        out_specs = [rows(Ko), _resident((1, Ko))]
    return pl.pallas_call(
        body, name=name, grid=(S // tm,),
        in_specs=in_specs, out_specs=out_specs, out_shape=out_shape,
        compiler_params=_params(1),
    )(*args)


def mm_tn(a, b, *, name, G, out_dtype, after=None):
    order = [] if after is None else [after]
    S, Ka = a.shape
    _, N = b.shape
    Nb = N // G
    tm = min(TM_TOKENS_TN if b.dtype.itemsize == 2 else TM_TOKENS_TN // 2, S)
    tka = min(Ka, 1024)
    tnb = Nb if Nb <= 1024 else 1024
    nj = Nb // tnb
    ns = S // tm

    def body(a_ref, b_ref, *rest):
        o_ref, acc = rest[-2:]
        s = pl.program_id(2)

        @pl.when(s == 0)
        def _():
            acc[...] = jnp.zeros_like(acc)

        acc[...] += _dot(a_ref[...], b_ref[...], TN)

        @pl.when(s == ns - 1)
        def _():
            o_ref[...] = acc[...].astype(o_ref.dtype)

    return pl.pallas_call(
        body, name=name, grid=(Ka // tka, G * nj, ns),
        in_specs=[pl.BlockSpec((tm, tka), lambda i, j, s: (s, i)),
                  pl.BlockSpec((tm, tnb), lambda i, j, s: (s, j))] + [pl.BlockSpec(memory_space=pl.ANY)] * len(order),
        out_specs=pl.BlockSpec((None, tka, tnb), lambda i, j, s: (j // nj, i, j % nj)),
        out_shape=jax.ShapeDtypeStruct((G, Ka, Nb), out_dtype),
        scratch_shapes=[pltpu.VMEM((tka, tnb), F32)],
        compiler_params=_params(3),
    )(a, b, *order)


def mla_pre_fwd(a, gqa, gkva, wuq3, wukv3, gqn, gkn, cos_t, sin_t):
    S = a.shape[0]
    tm = min(TM, S)
    H = N_HEADS

    def body(a_ref, gqa_ref, gkva_ref, wuq_ref, wukv_ref, gqn_ref, gkn_ref, cos_ref, sin_ref,
             q_ref, k_ref, v_ref, cq_ref, ckv_ref):
        av = a_ref[...]
        cq = (_rms(av[:, :Q_LORA], Q_LORA)[0] * gqa_ref[...]).astype(cq_ref.dtype)
        ckv = (_rms(av[:, Q_LORA:Q_LORA + KV_LORA], KV_LORA)[0] * gkva_ref[...]).astype(ckv_ref.dtype)
        cq_ref[...] = cq
        ckv_ref[...] = ckv
        kpe = av[:, Q_LORA + KV_LORA:]
        cos_v, sin_v = cos_ref[...], sin_ref[...]
        for h in range(H):
            qn = _rms(_dot(cq, wuq_ref[h]), QK_DIM)[0] * gqn_ref[...]
            qr = jnp.concatenate([qn[:, :NOPE], _rope(qn[:, NOPE:], cos_v, sin_v)], axis=1)
            q_ref[h] = (qr * Q_PRESCALE).astype(q_ref.dtype)
            kvp = _dot(ckv, wukv_ref[h])
            kn = _rms(jnp.concatenate([kvp[:, :NOPE], kpe], axis=1), QK_DIM)[0] * gkn_ref[...]
            k_ref[h] = jnp.concatenate([kn[:, :NOPE], _rope(kn[:, NOPE:], cos_v, sin_v)], axis=1).astype(k_ref.dtype)
            v_ref[h] = kvp[:, NOPE:].astype(v_ref.dtype)

    row = lambda w: pl.BlockSpec((tm, w), lambda i: (i, 0))
    heads = lambda w: pl.BlockSpec((H, tm, w), lambda i: (0, i, 0))
    return pl.pallas_call(
        body, name="mla_pre_fwd", grid=(S // tm,),
        in_specs=[row(DOWN_PAD), _resident((1, Q_LORA)), _resident((1, KV_LORA)),
                  _resident((H, Q_LORA, QK_PAD)), _resident((H, KV_LORA, NOPE + V_DIM)),
                  _resident((1, QK_PAD)), _resident((1, QK_PAD)), row(128), row(128)],
        out_specs=[heads(QK_PAD), heads(QK_PAD), heads(V_DIM), row(Q_LORA), row(KV_LORA)],
        out_shape=[jax.ShapeDtypeStruct((H, S, QK_PAD), MXU_DTYPE),
                   jax.ShapeDtypeStruct((H, S, QK_PAD), MXU_DTYPE),
                   jax.ShapeDtypeStruct((H, S, V_DIM), MXU_DTYPE),
                   jax.ShapeDtypeStruct((S, Q_LORA), MXU_DTYPE),
                   jax.ShapeDtypeStruct((S, KV_LORA), MXU_DTYPE)],
        compiler_params=_params(1),
    )(a, gqa, gkva, wuq3, wukv3, gqn, gkn, cos_t, sin_t)


def mla_pre_bwd(dq, dk, dv, a, cq, ckv, gqa, gkva, wuq3, wukv3, gqn, gkn, cos_t, sin_t):
    S = a.shape[0]
    tm = min(TM, S)
    H = N_HEADS

    def body(dq_ref, dk_ref, dv_ref, a_ref, cq_ref, ckv_ref, gqa_ref, gkva_ref, wuq_ref, wukv_ref, gqn_ref, gkn_ref,
             cos_ref, sin_ref, da_ref, dwuq_ref, dwukv_ref, dgqn_ref, dgkn_ref, dgqa_ref, dgkva_ref):
        @pl.when(pl.program_id(0) == 0)
        def _():
            for ref in (dwuq_ref, dwukv_ref, dgqn_ref, dgkn_ref, dgqa_ref, dgkva_ref):
                ref[...] = jnp.zeros_like(ref)

        av = a_ref[...]
        kpe = av[:, Q_LORA + KV_LORA:]
        cos_v, sin_v = cos_ref[...], sin_ref[...]
        cqv, ckvv = cq_ref[...], ckv_ref[...]
        dcq = jnp.zeros((tm, Q_LORA), F32)
        dckv = jnp.zeros((tm, KV_LORA), F32)
        dkpe = jnp.zeros((tm, 128), F32)
        dgqn = jnp.zeros((1, QK_PAD), F32)
        dgkn = jnp.zeros((1, QK_PAD), F32)
        up = lambda h: (_dot(cqv, wuq_ref[h]), _dot(ckvv, wukv_ref[h]))
        nxt = up(0)
        for h in range(H):
            wuq, wukv = wuq_ref[h], wukv_ref[h]
            qp, kvp = nxt
            if h + 1 < H:
                nxt = up(h + 1)
            qhat, rq = _rms(qp, QK_DIM)
            dqr = dq_ref[h] * SM_SCALE
            dqn = jnp.concatenate([dqr[:, :NOPE], _rope_bwd(dqr[:, NOPE:], cos_v, sin_v)], axis=1)
            dqp, dg = _rms_bwd(dqn, qhat, rq, gqn_ref[...], QK_DIM)
            dgqn = dgqn + dg
            dqp = dqp.astype(MXU_DTYPE)
            dwuq_ref[h] += _dot(cqv, dqp, TN)
            dcq = dcq + _dot(dqp, wuq, NT)
            khat, rk = _rms(jnp.concatenate([kvp[:, :NOPE], kpe], axis=1), QK_DIM)
            dkr = dk_ref[h] * (1.0 / LOG2E)
            dkn = jnp.concatenate([dkr[:, :NOPE], _rope_bwd(dkr[:, NOPE:], cos_v, sin_v)], axis=1)
            dkk, dg = _rms_bwd(dkn, khat, rk, gkn_ref[...], QK_DIM)
            dgkn = dgkn + dg
            dkpe = dkpe + dkk[:, NOPE:]
            dkvp = jnp.concatenate([dkk[:, :NOPE], dv_ref[h]], axis=1).astype(MXU_DTYPE)
            dwukv_ref[h] += _dot(ckvv, dkvp, TN)
            dckv = dckv + _dot(dkvp, wukv, NT)
        dgqn_ref[...] += dgqn
        dgkn_ref[...] += dgkn
        ahat, r = _rms(av[:, :Q_LORA], Q_LORA)
        daq, dg = _rms_bwd(dcq, ahat, r, gqa_ref[...], Q_LORA)
        dgqa_ref[...] += dg
        ahat, r = _rms(av[:, Q_LORA:Q_LORA + KV_LORA], KV_LORA)
        dakv, dg = _rms_bwd(dckv, ahat, r, gkva_ref[...], KV_LORA)
        dgkva_ref[...] += dg
        da_ref[...] = jnp.concatenate([daq, dakv, dkpe], axis=1)

    row = lambda w: pl.BlockSpec((tm, w), lambda i: (i, 0))
    heads = lambda w: pl.BlockSpec((H, tm, w), lambda i: (0, i, 0))
    return pl.pallas_call(
        body, name="mla_pre_bwd", grid=(S // tm,),
        in_specs=[heads(QK_PAD), heads(QK_PAD), heads(V_DIM), row(DOWN_PAD), row(Q_LORA), row(KV_LORA),
                  _resident((1, Q_LORA)), _resident((1, KV_LORA)),
                  _resident((H, Q_LORA, QK_PAD)), _resident((H, KV_LORA, NOPE + V_DIM)),
                  _resident((1, QK_PAD)), _resident((1, QK_PAD)), row(128), row(128)],
        out_specs=[row(DOWN_PAD), _resident((H, Q_LORA, QK_PAD)), _resident((H, KV_LORA, NOPE + V_DIM)),
                   _resident((1, QK_PAD)), _resident((1, QK_PAD)), _resident((1, Q_LORA)), _resident((1, KV_LORA))],
        out_shape=[jax.ShapeDtypeStruct((S, DOWN_PAD), F32),
                   jax.ShapeDtypeStruct((H, Q_LORA, QK_PAD), F32),
                   jax.ShapeDtypeStruct((H, KV_LORA, NOPE + V_DIM), F32),
                   jax.ShapeDtypeStruct((1, QK_PAD), F32), jax.ShapeDtypeStruct((1, QK_PAD), F32),
                   jax.ShapeDtypeStruct((1, Q_LORA), F32), jax.ShapeDtypeStruct((1, KV_LORA), F32)],
        compiler_params=_params(1),
    )(dq, dk, dv, a, cq, ckv, gqa, gkva, wuq3, wukv3, gqn, gkn, cos_t, sin_t)


def _pair_tables(nb, key_major):
    if key_major:
        pairs = [(qi, kj) for kj in range(nb) for qi in range(kj, nb)]
    else:
        pairs = [(qi, ki) for qi in range(nb) for ki in range(qi + 1)]
    return (jnp.asarray(np.array([p[0] for p in pairs], np.int32)),
            jnp.asarray(np.array([p[1] for p in pairs], np.int32)))


def _scores_t(k, q, pk_col, pq_row, masked):
    s = _dot(k, q, NT)
    return jnp.where(pq_row >= pk_col, s, jnp.finfo(F32).min) if masked else s


def attn_fwd(q, k, v, pos_col, pos_row):
    H, S, _ = q.shape
    t = min(TQ, S)
    nb = S // t
    hb = HEADS_FWD
    qt, kt = _pair_tables(nb, key_major=False)

    def body(qt_ref, kt_ref, q_ref, k_ref, v_ref, pk_ref, pq_ref, o_ref, lse_ref, m_s, l_s, acc):
        step = pl.program_id(1)
        qi, ki = qt_ref[step], kt_ref[step]

        @pl.when(ki == 0)
        def _():
            m_s[...] = jnp.full_like(m_s, -jnp.inf)
            l_s[...] = jnp.zeros_like(l_s)
            acc[...] = jnp.zeros_like(acc)

        def update(masked):
            scores = lambda hh: _scores_t(k_ref[hh], q_ref[hh], pk_ref[...], pq_ref[...], masked)
            def weighted_values(hh, p, alpha):
                acc[hh] = alpha * acc[hh] + _dot(v_ref[hh], p, TN)

            s_next = scores(0)
            pending = None
            for hh in range(hb):
                s = s_next
                if hh + 1 < hb:
                    s_next = scores(hh + 1)
                m_old = m_s[hh]
                m_new = jnp.maximum(m_old, jnp.max(s, axis=0, keepdims=True))
                p = jnp.exp2(s - m_new)
                alpha = jnp.exp2(m_old - m_new)
                l_s[hh] = alpha * l_s[hh] + jnp.sum(p, axis=0, keepdims=True)
                m_s[hh] = m_new
                if pending is not None:
                    weighted_values(*pending)
                pending = (hh, p, alpha)
            weighted_values(*pending)

        @pl.when(ki < qi)
        def _():
            update(False)

        @pl.when(ki == qi)
        def _():
            update(True)
            for hh in range(hb):
                o_ref[:, hh * V_DIM:(hh + 1) * V_DIM] = (acc[hh] / l_s[hh]).T
                lse_ref[hh] = m_s[hh] + jnp.log(l_s[hh]) * LOG2E

    grid_spec = pltpu.PrefetchScalarGridSpec(
        num_scalar_prefetch=2, grid=(H // hb, qt.shape[0]),
        in_specs=[pl.BlockSpec((hb, t, QK_PAD), lambda h, s, qt, kt: (h, qt[s], 0)),
                  pl.BlockSpec((hb, t, QK_PAD), lambda h, s, qt, kt: (h, kt[s], 0)),
                  pl.BlockSpec((hb, t, V_DIM), lambda h, s, qt, kt: (h, kt[s], 0)),
                  pl.BlockSpec((t, 1), lambda h, s, qt, kt: (kt[s], 0)),
                  pl.BlockSpec((1, t), lambda h, s, qt, kt: (0, qt[s]))],
        out_specs=[pl.BlockSpec((t, hb * V_DIM), lambda h, s, qt, kt: (qt[s], h)),
                   pl.BlockSpec((hb, 1, t), lambda h, s, qt, kt: (h, 0, qt[s]))],
        scratch_shapes=[pltpu.VMEM((hb, 1, t), F32), pltpu.VMEM((hb, 1, t), F32), pltpu.VMEM((hb, V_DIM, t), F32)])
    return pl.pallas_call(
        body, name="attn_fwd", grid_spec=grid_spec,
        out_shape=[jax.ShapeDtypeStruct((S, H * V_DIM), F32), jax.ShapeDtypeStruct((H, 1, S), F32)],
        compiler_params=_params(2),
    )(qt, kt, q, k, v, pos_col, pos_row)


def attn_delta(o, do):
    S = o.shape[0]
    t = min(TQ, S)

    def body(o_ref, do_ref, d_ref):
        for h in range(N_HEADS):
            cols = slice(h * V_DIM, (h + 1) * V_DIM)
            d_ref[h] = jnp.sum((o_ref[:, cols] * do_ref[:, cols]).T, axis=0, keepdims=True)

    blk = pl.BlockSpec((t, N_HEADS * V_DIM), lambda i: (i, 0))
    return pl.pallas_call(
        body, name="attn_delta", grid=(S // t,),
        in_specs=[blk, blk],
        out_specs=pl.BlockSpec((N_HEADS, 1, t), lambda i: (0, 0, i)),
        out_shape=jax.ShapeDtypeStruct((N_HEADS, 1, S), F32),
        compiler_params=_params(1),
    )(o, do)


def attn_bwd(q, k, v, do, lse, delta, pos_col, pos_row):
    H, S, _ = q.shape
    t = min(TQ_BWD, S)
    nb = S // t
    qt, kt = _pair_tables(nb, key_major=True)
    tc = min(BWD_CHUNK, t)

    def body(qt_ref, kt_ref, q_ref, k_ref, v_ref, do_ref, lse_ref, dl_ref, pk_ref, pq_ref, dq_ref, dk_ref, dv_ref):
        step = pl.program_id(1)
        qi, kj = qt_ref[step], kt_ref[step]

        @pl.when(step == 0)
        def _():
            dq_ref[...] = jnp.zeros_like(dq_ref)

        @pl.when(qi == kj)
        def _():
            dk_ref[...] = jnp.zeros_like(dk_ref)
            dv_ref[...] = jnp.zeros_like(dv_ref)

        def update(masked):
            seen = lambda c: (c + 1) * tc if masked else t

            def first_matmuls(c):
                cols, ke = slice(c * tc, (c + 1) * tc), seen(c)
                qc = q_ref[cols, :]
                doc = do_ref[cols, :].astype(MXU_DTYPE)
                s = _scores_t(k_ref[:ke, :], qc, pk_ref[:ke, :], pq_ref[:, cols], masked)
                return qc, doc, s, _dot(v_ref[:ke, :], doc, NT)

            nxt = first_matmuls(0)
            for c in range(t // tc):
                qc, doc, s, dp = nxt
                if c + 1 < t // tc:
                    nxt = first_matmuls(c + 1)
                cols, ke = slice(c * tc, (c + 1) * tc), seen(c)
                p = jnp.exp2(s - lse_ref[:, cols])
                ds = (p * (dp - dl_ref[:, cols])).astype(MXU_DTYPE)
                dv_ref[:ke, :] += _dot(p, doc)
                dk_ref[:ke, :] += _dot(ds, qc)
                rows = pl.ds(pl.multiple_of(qi * t + c * tc, tc), tc)
                dq_ref[rows, :] += _dot(ds, k_ref[:ke, :], TN)

        @pl.when(qi == kj)
        def _():
            update(True)

        @pl.when(qi != kj)
        def _():
            update(False)

    q_idx = lambda h, s, qt, kt: (h, qt[s], 0)
    k_idx = lambda h, s, qt, kt: (h, kt[s], 0)
    row_idx = lambda h, s, qt, kt: (h, 0, qt[s])
    grid_spec = pltpu.PrefetchScalarGridSpec(
        num_scalar_prefetch=2, grid=(H, qt.shape[0]),
        in_specs=[pl.BlockSpec((None, t, QK_PAD), q_idx),
                  pl.BlockSpec((None, t, QK_PAD), k_idx),
                  pl.BlockSpec((None, t, V_DIM), k_idx),
                  pl.BlockSpec((t, V_DIM), lambda h, s, qt, kt: (qt[s], h)),
                  pl.BlockSpec((None, 1, t), row_idx),
                  pl.BlockSpec((None, 1, t), row_idx),
                  pl.BlockSpec((t, 1), lambda h, s, qt, kt: (kt[s], 0)),
                  pl.BlockSpec((1, t), lambda h, s, qt, kt: (0, qt[s]))],
        out_specs=[pl.BlockSpec((None, S, QK_PAD), lambda h, s, qt, kt: (h, 0, 0)),
                   pl.BlockSpec((None, t, QK_PAD), k_idx),
                   pl.BlockSpec((None, t, V_DIM), k_idx)])
    return pl.pallas_call(
        body, name="attn_bwd", grid_spec=grid_spec,
        out_shape=[jax.ShapeDtypeStruct((H, S, QK_PAD), F32), jax.ShapeDtypeStruct((H, S, QK_PAD), F32),
                   jax.ShapeDtypeStruct((H, S, V_DIM), F32)],
        compiler_params=_params(2),
    )(qt, kt, q, k, v, do, lse, delta, pos_col, pos_row)


def _conv_specs(S, tr):
    hb = tr // HALO
    main = lambda third: pl.BlockSpec((tr, D_MODEL), lambda r: (r, third))
    prev = lambda third: pl.BlockSpec((HALO, D_MODEL), lambda r: (jnp.maximum(r * hb - 1, 0), third))
    nxt = lambda third: pl.BlockSpec((HALO, D_MODEL), lambda r: (jnp.minimum((r + 1) * hb, S // HALO - 1), third))
    return main, prev, nxt


def _f32(ref):
    return ref[...].astype(F32)


def _conv_taps(gc, uu, w_ref, first):
    u2 = gc * uu
    rows = lax.broadcasted_iota(jnp.int32, u2.shape, 0)
    u2 = jnp.where((rows < HALO) & first, 0.0, u2)
    s1 = pltpu.roll(u2, 1, 0)
    s2 = pltpu.roll(u2, 2, 0)
    u3 = w_ref[2:3, :] * u2 + w_ref[1:2, :] * s1 + w_ref[0:1, :] * s2
    return u2, s1, s2, u3


def conv_fwd(bcu, cw):
    S = bcu.shape[0]
    tr = min(TROW, S)
    main, prev, _ = _conv_specs(S, tr)

    def body(gb_ref, gc_ref, u_ref, gch_ref, uh_ref, w_ref, z_ref):
        gc = jnp.concatenate([_f32(gch_ref), _f32(gc_ref)], axis=0)
        uu = jnp.concatenate([_f32(uh_ref), _f32(u_ref)], axis=0)
        u3 = _conv_taps(gc, uu, w_ref, pl.program_id(0) == 0)[3]
        z_ref[...] = (_f32(gb_ref) * u3[HALO:]).astype(z_ref.dtype)

    return pl.pallas_call(
        body, name="conv_fwd", grid=(S // tr,),
        in_specs=[main(0), main(1), main(2), prev(1), prev(2), _resident((3, D_MODEL))],
        out_specs=main(0),
        out_shape=jax.ShapeDtypeStruct((S, D_MODEL), MXU_DTYPE),
        compiler_params=_params(1),
    )(bcu, bcu, bcu, bcu, bcu, cw)


def conv_bwd(dz, bcu, cw):
    S = bcu.shape[0]
    tr = min(TROW, S)
    nr = S // tr
    main, prev, nxt = _conv_specs(S, tr)

    def body(dz_ref, dzn_ref, gb_ref, gbn_ref, gc_ref, u_ref, gch_ref, uh_ref, w_ref, o_ref, dw_ref):
        r = pl.program_id(0)
        gcv, uv = _f32(gc_ref), _f32(u_ref)
        gc = jnp.concatenate([_f32(gch_ref), gcv], axis=0)
        uu = jnp.concatenate([_f32(uh_ref), uv], axis=0)
        u2, s1, s2, u3 = _conv_taps(gc, uu, w_ref, r == 0)
        dzv = dz_ref[...]
        du3 = jnp.concatenate([dzv * _f32(gb_ref), dzn_ref[...] * _f32(gbn_ref)], axis=0)
        rows = lax.broadcasted_iota(jnp.int32, du3.shape, 0)
        du3 = jnp.where((rows >= tr) & (r == nr - 1), 0.0, du3)
        n1 = pltpu.roll(du3, tr + HALO - 1, 0)
        n2 = pltpu.roll(du3, tr + HALO - 2, 0)
        du2 = (w_ref[2:3, :] * du3 + w_ref[1:2, :] * n1 + w_ref[0:1, :] * n2)[:tr]
        o_ref[:, :D_MODEL] = (dzv * u3[HALO:]).astype(o_ref.dtype)
        o_ref[:, D_MODEL:2 * D_MODEL] = (du2 * uv).astype(o_ref.dtype)
        o_ref[:, 2 * D_MODEL:] = (du2 * gcv).astype(o_ref.dtype)
        d3 = du3[:tr]
        taps = [jnp.sum(d3 * t[HALO:], axis=0, keepdims=True) for t in (s2, s1, u2)]

        @pl.when(r == 0)
        def _():
            for kk in range(3):
                dw_ref[kk:kk + 1, :] = taps[kk]

        @pl.when(r != 0)
        def _():
            for kk in range(3):
                dw_ref[kk:kk + 1, :] += taps[kk]

    return pl.pallas_call(
        body, name="conv_bwd", grid=(nr,),
        in_specs=[main(0), nxt(0), main(0), nxt(0), main(1), main(2), prev(1), prev(2), _resident((3, D_MODEL))],
        out_specs=[pl.BlockSpec((tr, 3 * D_MODEL), lambda r: (r, 0)), _resident((3, D_MODEL))],
        out_shape=[jax.ShapeDtypeStruct((S, 3 * D_MODEL), MXU_DTYPE), jax.ShapeDtypeStruct((3, D_MODEL), F32)],
        compiler_params=_params(1),
    )(dz, dz, bcu, bcu, bcu, bcu, bcu, bcu, cw)


def loss_head(y, target):
    S, D = y.shape
    tm = min(TM, S)

    def body(y_ref, t_ref, l_ref, dy_ref):
        err = y_ref[...] - t_ref[...]
        dy_ref[...] = err / D
        part = jnp.full((1, 128), jnp.sum(err * err), F32)

        @pl.when(pl.program_id(0) == 0)
        def _():
            l_ref[...] = part

        @pl.when(pl.program_id(0) != 0)
        def _():
            l_ref[...] += part

    blk = pl.BlockSpec((tm, D), lambda i: (i, 0))
    return pl.pallas_call(
        body, name="loss_head", grid=(S // tm,),
        in_specs=[blk, blk],
        out_specs=[pl.BlockSpec((1, 128), lambda i: (0, 0)), blk],
        out_shape=[jax.ShapeDtypeStruct((1, 128), F32), jax.ShapeDtypeStruct((S, D), F32)],
        compiler_params=_params(1),
    )(y, target)


def adamw(parts, w, m, v, *, name):
    R, C = w.shape
    tr = R
    while tr * C * 4 > (1 << 20) and tr % 32 == 0:
        tr //= 2

    def body(p_ref, w_ref, m_ref, v_ref, g_ref, d_ref, mo_ref, vo_ref):
        g = p_ref[0].astype(F32)
        for d in range(1, N_DEV):
            g = g + p_ref[d].astype(F32)
        m_new = ADAM_B1 * m_ref[...] + (1.0 - ADAM_B1) * g
        v_new = ADAM_B2 * v_ref[...] + (1.0 - ADAM_B2) * (g * g)
        m_hat = m_new / (1.0 - ADAM_B1 ** ADAM_STEP)
        v_hat = v_new / (1.0 - ADAM_B2 ** ADAM_STEP)
        g_ref[...] = g
        d_ref[...] = -ADAM_LR * (m_hat / (jnp.sqrt(v_hat) + ADAM_EPS) + ADAM_WD * w_ref[...])
        mo_ref[...] = m_new
        vo_ref[...] = v_new

    blk = pl.BlockSpec((tr, C), lambda i: (i, 0))
    return pl.pallas_call(
        body, name=name, grid=(R // tr,),
        in_specs=[pl.BlockSpec((N_DEV, tr, C), lambda i: (0, i, 0)), blk, blk, blk],
        out_specs=[blk, blk, blk, blk],
        out_shape=[jax.ShapeDtypeStruct((R, C), F32)] * 4,
        compiler_params=_params(1),
    )(parts, w, m, v)


def _mesh_place():
    x, y, c = (lax.axis_index(n) for n in MESH_AXES)
    return x, y, c, 4 * x + 2 * y + c


def _peer(x, y, c, d):
    px = 1 - x if d & 4 else x
    py = 1 - y if d & 2 else y
    pc = 1 - c if d & 1 else c
    return (px, py, pc), 4 * px + 2 * py + pc


def gather_now(srcs, *, name):
    n = len(srcs)
    any_spec = pl.BlockSpec(memory_space=pl.ANY)

    def body(*refs):
        ins, outs, token = refs[:n], refs[n:2 * n], refs[2 * n]
        send_sems, recv_sems, local_sems = refs[2 * n + 1:]
        token[...] = jnp.zeros_like(token)
        x, y, c, me = _mesh_place()
        for a in range(n):
            pltpu.make_async_copy(ins[a], outs[a].at[me], local_sems.at[a]).start()
            for d in range(1, N_DEV):
                pltpu.make_async_remote_copy(
                    src_ref=ins[a], dst_ref=outs[a].at[me], send_sem=send_sems.at[a], recv_sem=recv_sems.at[a],
                    device_id=_peer(x, y, c, d)[0], device_id_type=pl.DeviceIdType.MESH).start()
        for a in range(n):
            pltpu.make_async_copy(ins[a], outs[a].at[me], local_sems.at[a]).wait()
            seven = outs[a].at[pl.ds(0, N_DEV - 1)]
            drain = pltpu.make_async_remote_copy(
                src_ref=seven, dst_ref=seven, send_sem=send_sems.at[a], recv_sem=recv_sems.at[a],
                device_id=(x, y, c), device_id_type=pl.DeviceIdType.MESH)
            drain.wait_send()
            drain.wait_recv()

    out = pl.pallas_call(
        body, name=name,
        in_specs=[any_spec] * n, out_specs=[any_spec] * n + [pl.BlockSpec(memory_space=pltpu.VMEM)],
        out_shape=[jax.ShapeDtypeStruct((N_DEV,) + s.shape, s.dtype) for s in srcs] + [jax.ShapeDtypeStruct((8, 128), F32)],
        scratch_shapes=[pltpu.SemaphoreType.DMA((n,)), pltpu.SemaphoreType.DMA((n,)), pltpu.SemaphoreType.DMA((n,))],
    )(*srcs)
    return out[:n], out[n]


_ANY = pl.BlockSpec(memory_space=pl.ANY)
_HBM = pl.BlockSpec(memory_space=pltpu.HBM)
_SEM = pl.BlockSpec(memory_space=pltpu.SEMAPHORE)


def _in_hbm(arrays):
    return [pltpu.with_memory_space_constraint(a, pltpu.HBM) for a in arrays]


def exchange_start(srcs, lands, slots, *, name):
    n, m = len(srcs), len(lands)

    def body(*refs):
        ins, zones = refs[:n], refs[n:n + m]
        send_sems, recv_sems, token = refs[n + m], refs[n + m + 1], refs[-1]
        x, y, c, me = _mesh_place()
        for a in range(n):
            for d in range(1, N_DEV):
                peer, peer_lin = _peer(x, y, c, d)
                src = ins[a] if slots is None else ins[a].at[peer_lin]
                dst = zones[a].at[me] if slots is None else zones[slots[a][0]].at[me, slots[a][1]]
                pltpu.make_async_remote_copy(
                    src_ref=src, dst_ref=dst, send_sem=send_sems.at[a], recv_sem=recv_sems.at[a],
                    device_id=peer, device_id_type=pl.DeviceIdType.MESH).start()
        token[...] = jnp.zeros_like(token)

    both = list(srcs) + list(lands)
    out = pl.pallas_call(
        body, name=name,
        in_specs=[_HBM] * (n + m),
        out_specs=[_SEM, _SEM] + [_HBM] * (n + m) + [pl.BlockSpec(memory_space=pltpu.VMEM)],
        out_shape=[pltpu.SemaphoreType.DMA((n,)), pltpu.SemaphoreType.DMA((n,))]
        + [pltpu.HBM(a.shape, a.dtype) for a in both] + [jax.ShapeDtypeStruct((8, 128), F32)],
        input_output_aliases={i: 2 + i for i in range(n + m)},
        compiler_params=pltpu.CompilerParams(has_side_effects=pltpu.SideEffectType.DATAFLOW_SIDE_EFFECTING),
    )(*_in_hbm(both))
    return out[0], out[1], out[2:2 + n], out[2 + n:2 + n + m], out[-1]


def exchange_wait(send_sems, recv_sems, srcs, lands, slots, after, *, name):
    n, m = len(srcs), len(lands)

    def body(*refs):
        ins, zones = refs[:n], refs[n:n + m]
        send_ref, recv_ref = refs[n + m], refs[n + m + 1]
        x, y, c, _ = _mesh_place()
        for a in range(n):
            seven = (zones[a] if slots is None else ins[a]).at[pl.ds(0, N_DEV - 1)]
            drain = pltpu.make_async_remote_copy(
                src_ref=seven, dst_ref=seven, send_sem=send_ref.at[a], recv_sem=recv_ref.at[a],
                device_id=(x, y, c), device_id_type=pl.DeviceIdType.MESH)
            drain.wait_send()
            drain.wait_recv()

    both = list(srcs) + list(lands)
    out = pl.pallas_call(
        body, name=name,
        in_specs=[_HBM] * (n + m) + [_SEM, _SEM, _ANY],
        out_specs=[_HBM] * (n + m),
        out_shape=[pltpu.HBM(a.shape, a.dtype) for a in both],
        input_output_aliases={i: i for i in range(n + m)},
        compiler_params=pltpu.CompilerParams(has_side_effects=pltpu.SideEffectType.DATAFLOW_SIDE_EFFECTING),
    )(*both, send_sems, recv_sems, after)
    return out[:n], out[n:]


def scatter_finish(remote, lands, vec, *, name):
    n, m = len(remote), len(lands)

    def body(*refs):
        ins, vec_ref, zones_in = refs[:n], refs[n], refs[n + 1:n + 1 + m]
        vec_out = refs[n + 1 + 2 * m]
        send_sems, recv_sems, local_sems = refs[n + 2 + 2 * m:]
        x, y, c, me = _mesh_place()

        def ends(a, j):
            if a == n:
                return vec_ref, vec_out.at[me]
            return ins[a].at[j], zones_in[remote[a][1]].at[me, remote[a][2]]

        for a in range(n + 1):
            pltpu.make_async_copy(*ends(a, me), local_sems.at[a]).start()
            for d in range(1, N_DEV):
                peer, peer_lin = _peer(x, y, c, d)
                src, dst = ends(a, peer_lin)
                pltpu.make_async_remote_copy(
                    src_ref=src, dst_ref=dst, send_sem=send_sems.at[a], recv_sem=recv_sems.at[a],
                    device_id=peer, device_id_type=pl.DeviceIdType.MESH).start()
        for a in range(n + 1):
            pltpu.make_async_copy(*ends(a, me), local_sems.at[a]).wait()
            seven = (vec_out if a == n else ins[a]).at[pl.ds(0, N_DEV - 1)]
            drain = pltpu.make_async_remote_copy(
                src_ref=seven, dst_ref=seven, send_sem=send_sems.at[a], recv_sem=recv_sems.at[a],
                device_id=(x, y, c), device_id_type=pl.DeviceIdType.MESH)
            drain.wait_send()
            drain.wait_recv()

    out = pl.pallas_call(
        body, name=name,
        in_specs=[_ANY] * (n + 1 + m), out_specs=[_ANY] * (m + 1),
        out_shape=[jax.ShapeDtypeStruct(z.shape, z.dtype) for z in lands]
        + [jax.ShapeDtypeStruct((N_DEV,) + vec.shape, vec.dtype)],
        input_output_aliases={n + 1 + i: i for i in range(m)},
        scratch_shapes=[pltpu.SemaphoreType.DMA((n + 1,))] * 3,
    )(*[e[0] for e in remote], vec, *lands)
    return out[:m], out[m]


def _rope_tables(pos):
    inv_freq = ROPE_THETA ** (-jnp.arange(0, ROPE, 2, dtype=F32) / ROPE)
    ang = pos.astype(F32)[:, None] * inv_freq
    cos, sin = jnp.cos(ang), jnp.sin(ang)
    pad = jnp.zeros((pos.shape[0], 128 - ROPE), F32)
    return jnp.concatenate([cos, cos, pad + 1.0], axis=1), jnp.concatenate([-sin, sin, pad], axis=1)


def _pad_last(w, n):
    return jnp.pad(w, [(0, 0)] * (w.ndim - 1) + [(0, n - w.shape[-1])])


def kernel(x, positions, g_mix, g_mlp, attn_w_down, attn_g_q_a, attn_g_kv_a, attn_w_uq, attn_w_ukv, attn_g_qnorm, attn_g_knorm, attn_w_o, conv_w_in, conv_w, conv_w_out, mlp_w1, mlp_w2, loss_target, m_g_mix, m_g_mlp, m_attn_w_down, m_attn_g_q_a, m_attn_g_kv_a, m_attn_w_uq, m_attn_w_ukv, m_attn_g_qnorm, m_attn_g_knorm, m_attn_w_o, m_conv_w_in, m_conv_w, m_conv_w_out, m_mlp_w1, m_mlp_w2, v_g_mix, v_g_mlp, v_attn_w_down, v_attn_g_q_a, v_attn_g_kv_a, v_attn_w_uq, v_attn_w_ukv, v_attn_g_qnorm, v_attn_g_knorm, v_attn_w_o, v_conv_w_in, v_conv_w, v_conv_w_out, v_mlp_w1, v_mlp_w2):
    weights = dict(g_mix=g_mix, g_mlp=g_mlp, attn_w_down=attn_w_down, attn_g_q_a=attn_g_q_a, attn_g_kv_a=attn_g_kv_a,
                   attn_w_uq=attn_w_uq, attn_w_ukv=attn_w_ukv, attn_g_qnorm=attn_g_qnorm, attn_g_knorm=attn_g_knorm,
                   attn_w_o=attn_w_o, conv_w_in=conv_w_in, conv_w=conv_w, conv_w_out=conv_w_out, mlp_w1=mlp_w1, mlp_w2=mlp_w2)
    mom1 = dict(g_mix=m_g_mix, g_mlp=m_g_mlp, attn_w_down=m_attn_w_down, attn_g_q_a=m_attn_g_q_a, attn_g_kv_a=m_attn_g_kv_a,
                attn_w_uq=m_attn_w_uq, attn_w_ukv=m_attn_w_ukv, attn_g_qnorm=m_attn_g_qnorm, attn_g_knorm=m_attn_g_knorm,
                attn_w_o=m_attn_w_o, conv_w_in=m_conv_w_in, conv_w=m_conv_w, conv_w_out=m_conv_w_out, mlp_w1=m_mlp_w1, mlp_w2=m_mlp_w2)
    mom2 = dict(g_mix=v_g_mix, g_mlp=v_g_mlp, attn_w_down=v_attn_w_down, attn_g_q_a=v_attn_g_q_a, attn_g_kv_a=v_attn_g_kv_a,
                attn_w_uq=v_attn_w_uq, attn_w_ukv=v_attn_w_ukv, attn_g_qnorm=v_attn_g_qnorm, attn_g_knorm=v_attn_g_knorm,
                attn_w_o=v_attn_w_o, conv_w_in=v_conv_w_in, conv_w=v_conv_w, conv_w_out=v_conv_w_out, mlp_w1=v_mlp_w1, mlp_w2=v_mlp_w2)
    big = ["attn_w_down", "attn_w_uq", "attn_w_ukv", "attn_w_o", "conv_w_in", "conv_w", "conv_w_out", "mlp_w1", "mlp_w2"]
    small = ["g_mix", "g_mlp", "attn_g_q_a", "attn_g_kv_a", "attn_g_qnorm", "attn_g_knorm"]
    order = ["g_mix", "g_mlp", "attn_w_down", "attn_g_q_a", "attn_g_kv_a", "attn_w_uq", "attn_w_ukv", "attn_g_qnorm",
             "attn_g_knorm", "attn_w_o", "conv_w_in", "conv_w", "conv_w_out", "mlp_w1", "mlp_w2"]

    xs = x[0]
    pos = positions[0]
    target = loss_target[0]
    S = xs.shape[0]
    depth = g_mix.shape[0]
    cos_t, sin_t = _rope_tables(pos)
    pos_col, pos_row = pos.reshape(S, 1), pos.reshape(1, S)

    keys, shards = [], []
    for name in big:
        for l in range(weights[name].shape[0]):
            keys.append((name, l))
            shards.append(weights[name][l] if name == "conv_w" else weights[name][l].astype(WIRE_DTYPE))
    first = [j for j, (name, l) in enumerate(keys) if l == 0 and name.startswith("attn")]
    second = [j for j, (name, l) in enumerate(keys) if l == 0 and name.startswith("mlp")]
    later = [j for j in range(len(keys)) if j not in first + second]
    me = 4 * lax.axis_index("x") + 2 * lax.axis_index("y") + lax.axis_index("c")

    def zones_with_own(js, token):
        return [lax.dynamic_update_slice(lax.empty((N_DEV,) + shards[j].shape, shards[j].dtype),
                                         (shards[j] + token[0, 0].astype(shards[j].dtype))[None],
                                         (me,) + (0,) * shards[j].ndim) for j in js]

    arrived, token = gather_now([shards[j] for j in first], name="gather_first")
    full = dict(zip([keys[j] for j in first], arrived))
    g1 = exchange_start([shards[j] for j in second], zones_with_own(second, token), None, name="gather_mlp0_start")
    g2 = exchange_start([shards[j] for j in later], zones_with_own(later, g1[4]), None, name="gather_rest_start")
    g_mix_0 = g_mix[0] + g2[4][0, 0]

    def rows(name, l):
        g = full[(name, l)]
        return g.reshape(g.shape[0] * g.shape[1], g.shape[2])

    saved = []
    for i in range(depth):
        l = i // 2
        rec = {"x0": xs}
        if i == 1:
            arrived = exchange_wait(*g2[:4], None, xs, name="gather_rest_wait")[1]
            full.update(zip([keys[j] for j in later], arrived))
        if i % 2 == 0:
            wd3 = _pad_last(rows("attn_w_down", l), DOWN_PAD)[None]
            wuq3 = _pad_last(full[("attn_w_uq", l)], QK_PAD)
            wukv3 = full[("attn_w_ukv", l)]
            gqn = _pad_last(attn_g_qnorm[l][None], QK_PAD)
            gkn = _pad_last(attn_g_knorm[l][None], QK_PAD)
            gqa, gkva = attn_g_q_a[l][None], attn_g_kv_a[l][None]
            h, a = norm_matmul(xs, g_mix_0 if i == 0 else g_mix[i], wd3, name="attn_down")
            q, k, v, cq, ckv = mla_pre_fwd(a, gqa, gkva, wuq3, wukv3, gqn, gkn, cos_t, sin_t)
            o, lse = attn_fwd(q, k, v, pos_col, pos_row)
            x1 = matmul_residual(o, rows("attn_w_o", l), xs, name="attn_out")
            rec.update(h=h, a=a, q=q, k=k, v=v, cq=cq, ckv=ckv, o=o, lse=lse, wd3=wd3, wuq3=wuq3, wukv3=wukv3,
                       gqn=gqn, gkn=gkn, gqa=gqa, gkva=gkva)
        else:
            cw = full[("conv_w", l)].transpose(1, 0, 2).reshape(3, D_MODEL)
            h, bcu = norm_matmul(xs, g_mix[i], full[("conv_w_in", l)], name="conv_in", out_dtype=MXU_DTYPE)
            z = conv_fwd(bcu, cw)
            x1 = matmul_residual(z, rows("conv_w_out", l), xs, name="conv_out")
            rec.update(h=h, bcu=bcu, z=z, cw=cw)
        if i == 0:
            arrived = exchange_wait(*g1[:4], None, x1, name="gather_mlp0_wait")[1]
            full.update(zip([keys[j] for j in second], arrived))
        h2, act = norm_matmul(x1, g_mlp[i], full[("mlp_w1", i)], name="mlp_up", mlp=True)
        xs = matmul_residual(act, rows("mlp_w2", i), x1, name="mlp_down")
        rec.update(x1=x1, h2=h2, act=act)
        saved.append(rec)

    sq, dx = loss_head(xs, target)
    loss = lax.psum(sq[0, 0] * (0.5 / D_MODEL), MESH_AXES)

    grads = {name: [None] * weights[name].shape[0] for name in order}
    token = None
    for i in reversed(range(depth)):
        l = i // 2
        rec = saved[i]
        grads["mlp_w2"][i] = mm_tn(rec["act"], dx, name="mlp_down_dw", G=1, out_dtype=WIRE_DTYPE).reshape(N_DEV, -1, D_MODEL)
        du = matmul_nt(dx, rows("mlp_w2", i)[None], name="mlp_down_dx", epi="mlp_du", u=rec["act"])
        grads["mlp_w1"][i] = mm_tn(rec["h2"], du, name="mlp_up_dw", G=N_DEV, out_dtype=WIRE_DTYPE)
        dx1, dg = matmul_nt(du, full[("mlp_w1", i)], name="mlp_up_dx", epi="rms_bwd", x=rec["x1"], g=g_mlp[i], dx=dx)
        grads["g_mlp"][i] = dg[0]
        if i == 0:
            flying = [keys[j] for j in second + later]
            srcs = [grads[name][l_] for name, l_ in flying]
            slots = [(big.index(name), l_) for name, l_ in flying]
            zones = [lax.empty((N_DEV, weights[name].shape[0]) + grads[name][-1].shape[1:], grads[name][-1].dtype)
                     for name in big]
            for src, (k, l_) in zip(srcs, slots):
                own = lax.dynamic_index_in_dim(src, me, 0, keepdims=True)[None]
                zones[k] = lax.dynamic_update_slice(zones[k], own, (me, l_) + (0,) * (src.ndim - 1))
            s_send, s_recv, s_srcs, s_zones, token = exchange_start(srcs, zones, slots, name="scatter_rest_start")
        if i % 2 == 0:
            grads["attn_w_o"][l] = mm_tn(rec["o"], dx1, name="attn_out_dw", G=1, out_dtype=WIRE_DTYPE,
                                         after=token).reshape(N_DEV, -1, D_MODEL)
            do = matmul_nt(dx1, rows("attn_w_o", l)[None], name="attn_out_dx")
            dq, dk, dv = attn_bwd(rec["q"], rec["k"], rec["v"], do, rec["lse"], attn_delta(rec["o"], do), pos_col, pos_row)
            da, dwuq, dwukv, dgqn, dgkn, dgqa, dgkva = mla_pre_bwd(
                dq, dk, dv, rec["a"], rec["cq"], rec["ckv"], rec["gqa"], rec["gkva"], rec["wuq3"], rec["wukv3"],
                rec["gqn"], rec["gkn"], cos_t, sin_t)
            grads["attn_w_uq"][l] = dwuq[:, :, :QK_DIM].astype(WIRE_DTYPE)
            grads["attn_w_ukv"][l] = dwukv.astype(WIRE_DTYPE)
            grads["attn_g_qnorm"][l] = dgqn[0, :QK_DIM]
            grads["attn_g_knorm"][l] = dgkn[0, :QK_DIM]
            grads["attn_g_q_a"][l] = dgqa[0]
            grads["attn_g_kv_a"][l] = dgkva[0]
            dwd = mm_tn(rec["h"], da, name="attn_down_dw", G=1, out_dtype=WIRE_DTYPE)
            grads["attn_w_down"][l] = dwd[0, :, :DOWN].reshape(N_DEV, -1, DOWN)
            dx, dg = matmul_nt(da, rec["wd3"], name="attn_down_dx", epi="rms_bwd", x=rec["x0"], g=g_mix[i], dx=dx1)
        else:
            grads["conv_w_out"][l] = mm_tn(rec["z"], dx1, name="conv_out_dw", G=1, out_dtype=WIRE_DTYPE).reshape(N_DEV, -1, D_MODEL)
            dz = matmul_nt(dx1, rows("conv_w_out", l)[None], name="conv_out_dx")
            dbcu, dcw = conv_bwd(dz, rec["bcu"], rec["cw"])
            grads["conv_w"][l] = dcw.reshape(3, N_DEV, -1).transpose(1, 0, 2)
            grads["conv_w_in"][l] = mm_tn(rec["h"], dbcu, name="conv_in_dw", G=N_DEV, out_dtype=WIRE_DTYPE)
            dx, dg = matmul_nt(dbcu, full[("conv_w_in", l)], name="conv_in_dx", epi="rms_bwd", x=rec["x0"], g=g_mix[i], dx=dx1)
        grads["g_mix"][i] = dg[0]

    sizes = [weights[name].size for name in small]
    n_small = sum(sizes)
    rows_small = -(-n_small // (8 * 128)) * 8

    def pack(tree):
        flat = jnp.concatenate([jnp.stack(tree[name]).reshape(-1) if isinstance(tree[name], list) else tree[name].reshape(-1)
                                for name in small])
        return jnp.pad(flat, (0, rows_small * 128 - n_small)).reshape(rows_small, 128)

    s_srcs, s_zones = exchange_wait(s_send, s_recv, s_srcs, s_zones, slots, dx, name="scatter_rest_wait")
    remote = [(grads[name][l_], big.index(name), l_) for name, l_ in (keys[j] for j in first)]
    parts, gain_parts = scatter_finish(remote, s_zones, pack(grads), name="scatter_last")

    out = {}
    for name, part in zip(big, parts):
        w = weights[name]
        flat = lambda t: t.reshape(-1, t.shape[-1])
        res = adamw(part.reshape(N_DEV, -1, w.shape[-1]), flat(w), flat(mom1[name]), flat(mom2[name]), name="adamw_" + name)
        out[name] = [r.reshape(w.shape) for r in res]
    res = adamw(gain_parts, pack(weights), pack(mom1), pack(mom2), name="adamw_gains")
    offset = 0
    for name, size in zip(small, sizes):
        out[name] = [r.reshape(-1)[offset:offset + size].reshape(weights[name].shape) for r in res]
        offset += size

    return (loss, dx[None], *[out[n][0] for n in order], *[out[n][1] for n in order],
            *[out[n][2] for n in order], *[out[n][3] for n in order])
```

```python
import jax
import jax.numpy as jnp
import numpy as np
from jax import lax
from jax.experimental import pallas as pl
from jax.experimental.pallas import tpu as pltpu

F32 = jnp.float32
MXU_DTYPE = jnp.bfloat16
WIRE_DTYPE = jnp.bfloat16

D_MODEL = 1024
N_HEADS = 8
NOPE = 128
ROPE = 64
QK_DIM = NOPE + ROPE
QK_PAD = 256
V_DIM = 128
Q_LORA = 256
KV_LORA = 128
DOWN = Q_LORA + KV_LORA + ROPE
DOWN_PAD = 512
ROPE_THETA = 10000.0
EPS = 1e-6
SM_SCALE = QK_DIM ** -0.5
LOG2E = 1.4426950408889634
Q_PRESCALE = SM_SCALE * LOG2E
ADAM_LR, ADAM_B1, ADAM_B2, ADAM_EPS, ADAM_WD, ADAM_STEP = 0.001, 0.9, 0.999, 1e-08, 0.01, 10
N_DEV = 8
MESH_AXES = ("x", "y", "c")

TM = 512
TM_WIDE = 1024
TILE_BUDGET = 32 << 20
TM_TOKENS_TN = 2048
TQ = 512
HEADS_FWD = 8
TQ_BWD = 2048
BWD_CHUNK = 256
TROW = 256
HALO = 16
VMEM_LIMIT = 48 << 20

NN = (((1,), (0,)), ((), ()))
NT = (((1,), (1,)), ((), ()))
TN = (((0,), (0,)), ((), ()))


def _dot(a, b, dims=NN):
    return lax.dot_general(a.astype(MXU_DTYPE), b.astype(MXU_DTYPE), dims, preferred_element_type=F32)


def _params(n_axes):
    return pltpu.CompilerParams(dimension_semantics=("arbitrary",) * n_axes, vmem_limit_bytes=VMEM_LIMIT)


def _rms(xv, n):
    r = lax.rsqrt(jnp.sum(xv * xv, axis=-1, keepdims=True) / n + EPS)
    return xv * r, r


def _rms_bwd(dy, xhat, r, g, n):
    dg = jnp.sum(dy * xhat, axis=0, keepdims=True)
    dxh = dy * g
    dx = r * (dxh - xhat * (jnp.sum(dxh * xhat, axis=-1, keepdims=True) / n))
    return dx, dg


def _swap_halves(t):
    lane = lax.broadcasted_iota(jnp.int32, t.shape, 1)
    return jnp.where(lane < ROPE // 2, pltpu.roll(t, 128 - ROPE // 2, 1), pltpu.roll(t, ROPE // 2, 1))


def _rope(t, cos_t, sin_t):
    return t * cos_t + _swap_halves(t) * sin_t


def _rope_bwd(dout, cos_t, sin_t):
    return dout * cos_t + _swap_halves(dout * sin_t)


def _resident(shape):
    return pl.BlockSpec(shape, lambda i: (0,) * len(shape))


def _token_tile(S, row_bytes, resident_bytes):
    wide = min(TM_WIDE, S)
    return wide if 2 * (wide * row_bytes + resident_bytes) <= TILE_BUDGET else min(TM, S)


def norm_matmul(x, g, w3, *, name, mlp=False, out_dtype=F32):
    if mlp:
        out_dtype = MXU_DTYPE
    S, D = x.shape
    G, _, Nb = w3.shape
    N = G * Nb
    n_out = 2 if mlp else 1
    tm = _token_tile(S, D * 4 + D * 2 + n_out * N * jnp.dtype(out_dtype).itemsize, w3.size * w3.dtype.itemsize)

    def body(x_ref, g_ref, w_ref, h_ref, o_ref, *slope_ref):
        xv = x_ref[...]
        r = lax.rsqrt(jnp.mean(xv * xv, axis=-1, keepdims=True) + EPS)
        h = (xv * r * g_ref[...]).astype(h_ref.dtype)
        h_ref[...] = h
        for gi in range(G):
            cols = slice(gi * Nb, (gi + 1) * Nb)
            acc = _dot(h, w_ref[gi])
            if mlp:
                acc = jnp.maximum(acc, 0.0)
                slope_ref[0][:, cols] = (2.0 * acc).astype(out_dtype)
                acc = jnp.square(acc)
            o_ref[:, cols] = acc.astype(o_ref.dtype)

    rows = lambda w: pl.BlockSpec((tm, w), lambda i: (i, 0))
    return pl.pallas_call(
        body, name=name, grid=(S // tm,),
        in_specs=[rows(D), _resident((1, D)), _resident((G, D, Nb))],
        out_specs=[rows(D)] + [rows(N)] * n_out,
        out_shape=[jax.ShapeDtypeStruct((S, D), MXU_DTYPE)] + [jax.ShapeDtypeStruct((S, N), out_dtype)] * n_out,
        compiler_params=_params(1),
    )(x, g.reshape(1, D), w3)


def matmul_residual(a, w, res, *, name):
    S, K = a.shape
    _, N = w.shape
    tm = _token_tile(S, K * a.dtype.itemsize + 2 * N * 4, w.size * w.dtype.itemsize)

    def body(a_ref, w_ref, r_ref, o_ref):
        o_ref[...] = r_ref[...] + _dot(a_ref[...], w_ref[...])

    rows = lambda w_: pl.BlockSpec((tm, w_), lambda i: (i, 0))
    return pl.pallas_call(
        body, name=name, grid=(S // tm,),
        in_specs=[rows(K), _resident((K, N)), rows(N)],
        out_specs=rows(N), out_shape=jax.ShapeDtypeStruct((S, N), F32),
        compiler_params=_params(1),
    )(a, w, res)


def matmul_nt(a, w3, *, name, epi="plain", u=None, x=None, g=None, dx=None):
    S, N = a.shape
    G, Ko, Nb = w3.shape
    assert N == G * Nb
    row_bytes = N * a.dtype.itemsize + Ko * {"plain": 4, "mlp_du": 4, "rms_bwd": 14}[epi]
    tm = _token_tile(S, row_bytes, w3.size * w3.dtype.itemsize)
    tko = min(Ko, 512)

    def body(a_ref, w_ref, *rest):
        if epi == "mlp_du":
            u_ref, o_ref = rest
            av = a_ref[...].astype(MXU_DTYPE)
            for j in range(Ko // tko):
                cols = slice(j * tko, (j + 1) * tko)
                da = _dot(av, w_ref[0, cols, :], NT)
                o_ref[:, cols] = (da * u_ref[:, cols].astype(F32)).astype(o_ref.dtype)
            return
        acc = _dot(a_ref[:, :Nb], w_ref[0], NT)
        for gi in range(1, G):
            acc = acc + _dot(a_ref[:, gi * Nb:(gi + 1) * Nb], w_ref[gi], NT)
        if epi == "plain":
            rest[0][...] = acc
        else:
            x_ref, g_ref, dx_ref, o_ref, ob_ref, dg_ref = rest
            xhat, r = _rms(x_ref[...], Ko)
            dxb, dg = _rms_bwd(acc, xhat, r, g_ref[...], Ko)
            dx_new = dx_ref[...] + dxb
            o_ref[...] = dx_new
            ob_ref[...] = dx_new.astype(ob_ref.dtype)

            @pl.when(pl.program_id(0) == 0)
            def _():
                dg_ref[...] = dg

            @pl.when(pl.program_id(0) != 0)
            def _():
                dg_ref[...] += dg

    rows = lambda w_: pl.BlockSpec((tm, w_), lambda i: (i, 0))
    in_specs = [rows(N), _resident((G, Ko, Nb))]
    args = [a, w3]
    if epi == "plain":
        out_shape, out_specs = jax.ShapeDtypeStruct((S, Ko), F32), rows(Ko)
    elif epi == "mlp_du":
        in_specs.append(rows(Ko))
        args.append(u)
        out_shape, out_specs = jax.ShapeDtypeStruct((S, Ko), MXU_DTYPE), rows(Ko)
    else:
        in_specs += [rows(Ko), _resident((1, Ko)), rows(Ko)]
        args += [x, g.reshape(1, Ko), dx]
        out_shape = [jax.ShapeDtypeStruct((S, Ko), F32), jax.ShapeDtypeStruct((S, Ko), MXU_DTYPE),
                     jax.ShapeDtypeStruct((1, Ko), F32)]
        out_specs = [rows(Ko), rows(Ko), _resident((1, Ko))]
    return pl.pallas_call(
        body, name=name, grid=(S // tm,),
        in_specs=in_specs, out_specs=out_specs, out_shape=out_shape,
        compiler_params=_params(1),
    )(*args)


def mm_tn(a, b, *, name, G, out_dtype, after=None):
    order = [] if after is None else [after]
    S, Ka = a.shape
    _, N = b.shape
    Nb = N // G
    tm = min(TM_TOKENS_TN if b.dtype.itemsize == 2 else TM_TOKENS_TN // 2, S)
    tka = min(Ka, 1024)
    tnb = Nb if Nb <= 1024 else 1024
    nj = Nb // tnb
    ns = S // tm

    def body(a_ref, b_ref, *rest):
        o_ref, acc = rest[-2:]
        s = pl.program_id(2)

        @pl.when(s == 0)
        def _():
            acc[...] = jnp.zeros_like(acc)

        acc[...] += _dot(a_ref[...], b_ref[...], TN)

        @pl.when(s == ns - 1)
        def _():
            o_ref[...] = acc[...].astype(o_ref.dtype)

    return pl.pallas_call(
        body, name=name, grid=(Ka // tka, G * nj, ns),
        in_specs=[pl.BlockSpec((tm, tka), lambda i, j, s: (s, i)),
                  pl.BlockSpec((tm, tnb), lambda i, j, s: (s, j))] + [pl.BlockSpec(memory_space=pl.ANY)] * len(order),
        out_specs=pl.BlockSpec((None, tka, tnb), lambda i, j, s: (j // nj, i, j % nj)),
        out_shape=jax.ShapeDtypeStruct((G, Ka, Nb), out_dtype),
        scratch_shapes=[pltpu.VMEM((tka, tnb), F32)],
        compiler_params=_params(3),
    )(a, b, *order)


def mla_pre_fwd(a, gqa, gkva, wuq3, wukv3, gqn, gkn, cos_t, sin_t):
    S = a.shape[0]
    tm = min(TM, S)
    H = N_HEADS

    def body(a_ref, gqa_ref, gkva_ref, wuq_ref, wukv_ref, gqn_ref, gkn_ref, cos_ref, sin_ref,
             q_ref, k_ref, v_ref, cq_ref, ckv_ref):
        av = a_ref[...]
        cq = (_rms(av[:, :Q_LORA], Q_LORA)[0] * gqa_ref[...]).astype(cq_ref.dtype)
        ckv = (_rms(av[:, Q_LORA:Q_LORA + KV_LORA], KV_LORA)[0] * gkva_ref[...]).astype(ckv_ref.dtype)
        cq_ref[...] = cq
        ckv_ref[...] = ckv
        kpe = av[:, Q_LORA + KV_LORA:]
        cos_v, sin_v = cos_ref[...], sin_ref[...]
        for h in range(H):
            qn = _rms(_dot(cq, wuq_ref[h]), QK_DIM)[0] * gqn_ref[...]
            qr = jnp.concatenate([qn[:, :NOPE], _rope(qn[:, NOPE:], cos_v, sin_v)], axis=1)
            q_ref[h] = (qr * Q_PRESCALE).astype(q_ref.dtype)
            kvp = _dot(ckv, wukv_ref[h])
            kn = _rms(jnp.concatenate([kvp[:, :NOPE], kpe], axis=1), QK_DIM)[0] * gkn_ref[...]
            k_ref[h] = jnp.concatenate([kn[:, :NOPE], _rope(kn[:, NOPE:], cos_v, sin_v)], axis=1).astype(k_ref.dtype)
            v_ref[h] = kvp[:, NOPE:].astype(v_ref.dtype)

    row = lambda w: pl.BlockSpec((tm, w), lambda i: (i, 0))
    heads = lambda w: pl.BlockSpec((H, tm, w), lambda i: (0, i, 0))
    return pl.pallas_call(
        body, name="mla_pre_fwd", grid=(S // tm,),
        in_specs=[row(DOWN_PAD), _resident((1, Q_LORA)), _resident((1, KV_LORA)),
                  _resident((H, Q_LORA, QK_PAD)), _resident((H, KV_LORA, NOPE + V_DIM)),
                  _resident((1, QK_PAD)), _resident((1, QK_PAD)), row(128), row(128)],
        out_specs=[heads(QK_PAD), heads(QK_PAD), heads(V_DIM), row(Q_LORA), row(KV_LORA)],
        out_shape=[jax.ShapeDtypeStruct((H, S, QK_PAD), MXU_DTYPE),
                   jax.ShapeDtypeStruct((H, S, QK_PAD), MXU_DTYPE),
                   jax.ShapeDtypeStruct((H, S, V_DIM), MXU_DTYPE),
                   jax.ShapeDtypeStruct((S, Q_LORA), MXU_DTYPE),
                   jax.ShapeDtypeStruct((S, KV_LORA), MXU_DTYPE)],
        compiler_params=_params(1),
    )(a, gqa, gkva, wuq3, wukv3, gqn, gkn, cos_t, sin_t)


def mla_pre_bwd(dq, dk, dv, a, cq, ckv, gqa, gkva, wuq3, wukv3, gqn, gkn, cos_t, sin_t):
    S = a.shape[0]
    tm = min(TM, S)
    H = N_HEADS

    def body(dq_ref, dk_ref, dv_ref, a_ref, cq_ref, ckv_ref, gqa_ref, gkva_ref, wuq_ref, wukv_ref, gqn_ref, gkn_ref,
             cos_ref, sin_ref, da_ref, dwuq_ref, dwukv_ref, dgqn_ref, dgkn_ref, dgqa_ref, dgkva_ref):
        @pl.when(pl.program_id(0) == 0)
        def _():
            for ref in (dwuq_ref, dwukv_ref, dgqn_ref, dgkn_ref, dgqa_ref, dgkva_ref):
                ref[...] = jnp.zeros_like(ref)

        av = a_ref[...]
        kpe = av[:, Q_LORA + KV_LORA:]
        cos_v, sin_v = cos_ref[...], sin_ref[...]
        cqv, ckvv = cq_ref[...], ckv_ref[...]
        dcq = jnp.zeros((tm, Q_LORA), F32)
        dckv = jnp.zeros((tm, KV_LORA), F32)
        dkpe = jnp.zeros((tm, 128), F32)
        dgqn = jnp.zeros((1, QK_PAD), F32)
        dgkn = jnp.zeros((1, QK_PAD), F32)
        up = lambda h: (_dot(cqv, wuq_ref[h]), _dot(ckvv, wukv_ref[h]))
        nxt = up(0)
        for h in range(H):
            wuq, wukv = wuq_ref[h], wukv_ref[h]
            qp, kvp = nxt
            if h + 1 < H:
                nxt = up(h + 1)
            qhat, rq = _rms(qp, QK_DIM)
            dqr = dq_ref[h] * SM_SCALE
            dqn = jnp.concatenate([dqr[:, :NOPE], _rope_bwd(dqr[:, NOPE:], cos_v, sin_v)], axis=1)
            dqp, dg = _rms_bwd(dqn, qhat, rq, gqn_ref[...], QK_DIM)
            dgqn = dgqn + dg
            dqp = dqp.astype(MXU_DTYPE)
            dwuq_ref[h] += _dot(cqv, dqp, TN)
            dcq = dcq + _dot(dqp, wuq, NT)
            khat, rk = _rms(jnp.concatenate([kvp[:, :NOPE], kpe], axis=1), QK_DIM)
            dkr = dk_ref[h] * (1.0 / LOG2E)
            dkn = jnp.concatenate([dkr[:, :NOPE], _rope_bwd(dkr[:, NOPE:], cos_v, sin_v)], axis=1)
            dkk, dg = _rms_bwd(dkn, khat, rk, gkn_ref[...], QK_DIM)
            dgkn = dgkn + dg
            dkpe = dkpe + dkk[:, NOPE:]
            dkvp = jnp.concatenate([dkk[:, :NOPE], dv_ref[h]], axis=1).astype(MXU_DTYPE)
            dwukv_ref[h] += _dot(ckvv, dkvp, TN)
            dckv = dckv + _dot(dkvp, wukv, NT)
        dgqn_ref[...] += dgqn
        dgkn_ref[...] += dgkn
        ahat, r = _rms(av[:, :Q_LORA], Q_LORA)
        daq, dg = _rms_bwd(dcq, ahat, r, gqa_ref[...], Q_LORA)
        dgqa_ref[...] += dg
        ahat, r = _rms(av[:, Q_LORA:Q_LORA + KV_LORA], KV_LORA)
        dakv, dg = _rms_bwd(dckv, ahat, r, gkva_ref[...], KV_LORA)
        dgkva_ref[...] += dg
        da_ref[...] = jnp.concatenate([daq, dakv, dkpe], axis=1)

    row = lambda w: pl.BlockSpec((tm, w), lambda i: (i, 0))
    heads = lambda w: pl.BlockSpec((H, tm, w), lambda i: (0, i, 0))
    return pl.pallas_call(
        body, name="mla_pre_bwd", grid=(S // tm,),
        in_specs=[heads(QK_PAD), heads(QK_PAD), heads(V_DIM), row(DOWN_PAD), row(Q_LORA), row(KV_LORA),
                  _resident((1, Q_LORA)), _resident((1, KV_LORA)),
                  _resident((H, Q_LORA, QK_PAD)), _resident((H, KV_LORA, NOPE + V_DIM)),
                  _resident((1, QK_PAD)), _resident((1, QK_PAD)), row(128), row(128)],
        out_specs=[row(DOWN_PAD), _resident((H, Q_LORA, QK_PAD)), _resident((H, KV_LORA, NOPE + V_DIM)),
                   _resident((1, QK_PAD)), _resident((1, QK_PAD)), _resident((1, Q_LORA)), _resident((1, KV_LORA))],
        out_shape=[jax.ShapeDtypeStruct((S, DOWN_PAD), F32),
                   jax.ShapeDtypeStruct((H, Q_LORA, QK_PAD), F32),
                   jax.ShapeDtypeStruct((H, KV_LORA, NOPE + V_DIM), F32),
                   jax.ShapeDtypeStruct((1, QK_PAD), F32), jax.ShapeDtypeStruct((1, QK_PAD), F32),
                   jax.ShapeDtypeStruct((1, Q_LORA), F32), jax.ShapeDtypeStruct((1, KV_LORA), F32)],
        compiler_params=_params(1),
    )(dq, dk, dv, a, cq, ckv, gqa, gkva, wuq3, wukv3, gqn, gkn, cos_t, sin_t)


def _pair_tables(nb, key_major):
    if key_major:
        pairs = [(qi, kj) for kj in range(nb) for qi in range(kj, nb)]
    else:
        pairs = [(qi, ki) for qi in range(nb) for ki in range(qi + 1)]
    return (jnp.asarray(np.array([p[0] for p in pairs], np.int32)),
            jnp.asarray(np.array([p[1] for p in pairs], np.int32)))


def _scores_t(k, q, pk_col, pq_row, masked):
    s = _dot(k, q, NT)
    return jnp.where(pq_row >= pk_col, s, jnp.finfo(F32).min) if masked else s


def attn_fwd(q, k, v, pos_col, pos_row):
    H, S, _ = q.shape
    t = min(TQ, S)
    nb = S // t
    hb = HEADS_FWD
    qt, kt = _pair_tables(nb, key_major=False)

    def body(qt_ref, kt_ref, q_ref, k_ref, v_ref, pk_ref, pq_ref, o_ref, lse_ref, m_s, l_s, acc):
        step = pl.program_id(1)
        qi, ki = qt_ref[step], kt_ref[step]

        @pl.when(ki == 0)
        def _():
            m_s[...] = jnp.full_like(m_s, -jnp.inf)
            l_s[...] = jnp.zeros_like(l_s)
            acc[...] = jnp.zeros_like(acc)

        def update(masked):
            scores = lambda hh: _scores_t(k_ref[hh], q_ref[hh], pk_ref[...], pq_ref[...], masked)
            def weighted_values(hh, p, alpha):
                acc[hh] = alpha * acc[hh] + _dot(v_ref[hh], p, TN)

            s_next = scores(0)
            pending = None
            for hh in range(hb):
                s = s_next
                if hh + 1 < hb:
                    s_next = scores(hh + 1)
                m_old = m_s[hh]
                m_new = jnp.maximum(m_old, jnp.max(s, axis=0, keepdims=True))
                p = jnp.exp2(s - m_new)
                alpha = jnp.exp2(m_old - m_new)
                l_s[hh] = alpha * l_s[hh] + jnp.sum(p, axis=0, keepdims=True)
                m_s[hh] = m_new
                if pending is not None:
                    weighted_values(*pending)
                pending = (hh, p, alpha)
            weighted_values(*pending)

        @pl.when(ki < qi)
        def _():
            update(False)

        @pl.when(ki == qi)
        def _():
            update(True)
            for hh in range(hb):
                o_ref[:, hh * V_DIM:(hh + 1) * V_DIM] = (acc[hh] / l_s[hh]).T
                lse_ref[hh] = m_s[hh] + jnp.log(l_s[hh]) * LOG2E

    grid_spec = pltpu.PrefetchScalarGridSpec(
        num_scalar_prefetch=2, grid=(H // hb, qt.shape[0]),
        in_specs=[pl.BlockSpec((hb, t, QK_PAD), lambda h, s, qt, kt: (h, qt[s], 0)),
                  pl.BlockSpec((hb, t, QK_PAD), lambda h, s, qt, kt: (h, kt[s], 0)),
                  pl.BlockSpec((hb, t, V_DIM), lambda h, s, qt, kt: (h, kt[s], 0)),
                  pl.BlockSpec((t, 1), lambda h, s, qt, kt: (kt[s], 0)),
                  pl.BlockSpec((1, t), lambda h, s, qt, kt: (0, qt[s]))],
        out_specs=[pl.BlockSpec((t, hb * V_DIM), lambda h, s, qt, kt: (qt[s], h)),
                   pl.BlockSpec((hb, 1, t), lambda h, s, qt, kt: (h, 0, qt[s]))],
        scratch_shapes=[pltpu.VMEM((hb, 1, t), F32), pltpu.VMEM((hb, 1, t), F32), pltpu.VMEM((hb, V_DIM, t), F32)])
    return pl.pallas_call(
        body, name="attn_fwd", grid_spec=grid_spec,
        out_shape=[jax.ShapeDtypeStruct((S, H * V_DIM), F32), jax.ShapeDtypeStruct((H, 1, S), F32)],
        compiler_params=_params(2),
    )(qt, kt, q, k, v, pos_col, pos_row)


def attn_delta(o, do):
    S = o.shape[0]
    t = min(TQ, S)

    def body(o_ref, do_ref, d_ref):
        for h in range(N_HEADS):
            cols = slice(h * V_DIM, (h + 1) * V_DIM)
            d_ref[h] = jnp.sum((o_ref[:, cols] * do_ref[:, cols]).T, axis=0, keepdims=True)

    blk = pl.BlockSpec((t, N_HEADS * V_DIM), lambda i: (i, 0))
    return pl.pallas_call(
        body, name="attn_delta", grid=(S // t,),
        in_specs=[blk, blk],
        out_specs=pl.BlockSpec((N_HEADS, 1, t), lambda i: (0, 0, i)),
        out_shape=jax.ShapeDtypeStruct((N_HEADS, 1, S), F32),
        compiler_params=_params(1),
    )(o, do)


def attn_bwd(q, k, v, do, lse, delta, pos_col, pos_row):
    H, S, _ = q.shape
    t = min(TQ_BWD, S)
    nb = S // t
    qt, kt = _pair_tables(nb, key_major=True)
    tc = min(BWD_CHUNK, t)

    def body(qt_ref, kt_ref, q_ref, k_ref, v_ref, do_ref, lse_ref, dl_ref, pk_ref, pq_ref, dq_ref, dk_ref, dv_ref):
        step = pl.program_id(1)
        qi, kj = qt_ref[step], kt_ref[step]

        @pl.when(step == 0)
        def _():
            dq_ref[...] = jnp.zeros_like(dq_ref)

        @pl.when(qi == kj)
        def _():
            dk_ref[...] = jnp.zeros_like(dk_ref)
            dv_ref[...] = jnp.zeros_like(dv_ref)

        def update(masked):
            seen = lambda c: (c + 1) * tc if masked else t

            def first_matmuls(c):
                cols, ke = slice(c * tc, (c + 1) * tc), seen(c)
                qc = q_ref[cols, :]
                doc = do_ref[cols, :].astype(MXU_DTYPE)
                s = _scores_t(k_ref[:ke, :], qc, pk_ref[:ke, :], pq_ref[:, cols], masked)
                return qc, doc, s, _dot(v_ref[:ke, :], doc, NT)

            nxt = first_matmuls(0)
            for c in range(t // tc):
                qc, doc, s, dp = nxt
                if c + 1 < t // tc:
                    nxt = first_matmuls(c + 1)
                cols, ke = slice(c * tc, (c + 1) * tc), seen(c)
                p = jnp.exp2(s - lse_ref[:, cols])
                ds = (p * (dp - dl_ref[:, cols])).astype(MXU_DTYPE)
                dv_ref[:ke, :] += _dot(p, doc)
                dk_ref[:ke, :] += _dot(ds, qc)
                rows = pl.ds(pl.multiple_of(qi * t + c * tc, tc), tc)
                dq_ref[rows, :] += _dot(ds, k_ref[:ke, :], TN)

        @pl.when(qi == kj)
        def _():
            update(True)

        @pl.when(qi != kj)
        def _():
            update(False)

    q_idx = lambda h, s, qt, kt: (h, qt[s], 0)
    k_idx = lambda h, s, qt, kt: (h, kt[s], 0)
    row_idx = lambda h, s, qt, kt: (h, 0, qt[s])
    grid_spec = pltpu.PrefetchScalarGridSpec(
        num_scalar_prefetch=2, grid=(H, qt.shape[0]),
        in_specs=[pl.BlockSpec((None, t, QK_PAD), q_idx),
                  pl.BlockSpec((None, t, QK_PAD), k_idx),
                  pl.BlockSpec((None, t, V_DIM), k_idx),
                  pl.BlockSpec((t, V_DIM), lambda h, s, qt, kt: (qt[s], h)),
                  pl.BlockSpec((None, 1, t), row_idx),
                  pl.BlockSpec((None, 1, t), row_idx),
                  pl.BlockSpec((t, 1), lambda h, s, qt, kt: (kt[s], 0)),
                  pl.BlockSpec((1, t), lambda h, s, qt, kt: (0, qt[s]))],
        out_specs=[pl.BlockSpec((None, S, QK_PAD), lambda h, s, qt, kt: (h, 0, 0)),
                   pl.BlockSpec((None, t, QK_PAD), k_idx),
                   pl.BlockSpec((None, t, V_DIM), k_idx)])
    return pl.pallas_call(
        body, name="attn_bwd", grid_spec=grid_spec,
        out_shape=[jax.ShapeDtypeStruct((H, S, QK_PAD), F32), jax.ShapeDtypeStruct((H, S, QK_PAD), F32),
                   jax.ShapeDtypeStruct((H, S, V_DIM), F32)],
        compiler_params=_params(2),
    )(qt, kt, q, k, v, do, lse, delta, pos_col, pos_row)


def _conv_specs(S, tr):
    hb = tr // HALO
    main = lambda third: pl.BlockSpec((tr, D_MODEL), lambda r: (r, third))
    prev = lambda third: pl.BlockSpec((HALO, D_MODEL), lambda r: (jnp.maximum(r * hb - 1, 0), third))
    nxt = lambda third: pl.BlockSpec((HALO, D_MODEL), lambda r: (jnp.minimum((r + 1) * hb, S // HALO - 1), third))
    return main, prev, nxt


def _f32(ref):
    return ref[...].astype(F32)


def _conv_taps(gc, uu, w_ref, first):
    u2 = gc * uu
    rows = lax.broadcasted_iota(jnp.int32, u2.shape, 0)
    u2 = jnp.where((rows < HALO) & first, 0.0, u2)
    s1 = pltpu.roll(u2, 1, 0)
    s2 = pltpu.roll(u2, 2, 0)
    u3 = w_ref[2:3, :] * u2 + w_ref[1:2, :] * s1 + w_ref[0:1, :] * s2
    return u2, s1, s2, u3


def conv_fwd(bcu, cw):
    S = bcu.shape[0]
    tr = min(TROW, S)
    main, prev, _ = _conv_specs(S, tr)

    def body(gb_ref, gc_ref, u_ref, gch_ref, uh_ref, w_ref, z_ref):
        gc = jnp.concatenate([_f32(gch_ref), _f32(gc_ref)], axis=0)
        uu = jnp.concatenate([_f32(uh_ref), _f32(u_ref)], axis=0)
        u3 = _conv_taps(gc, uu, w_ref, pl.program_id(0) == 0)[3]
        z_ref[...] = (_f32(gb_ref) * u3[HALO:]).astype(z_ref.dtype)

    return pl.pallas_call(
        body, name="conv_fwd", grid=(S // tr,),
        in_specs=[main(0), main(1), main(2), prev(1), prev(2), _resident((3, D_MODEL))],
        out_specs=main(0),
        out_shape=jax.ShapeDtypeStruct((S, D_MODEL), MXU_DTYPE),
        compiler_params=_params(1),
    )(bcu, bcu, bcu, bcu, bcu, cw)


def conv_bwd(dz, bcu, cw):
    S = bcu.shape[0]
    tr = min(TROW, S)
    nr = S // tr
    main, prev, nxt = _conv_specs(S, tr)

    def body(dz_ref, dzn_ref, gb_ref, gbn_ref, gc_ref, u_ref, gch_ref, uh_ref, w_ref, o_ref, dw_ref):
        r = pl.program_id(0)
        gcv, uv = _f32(gc_ref), _f32(u_ref)
        gc = jnp.concatenate([_f32(gch_ref), gcv], axis=0)
        uu = jnp.concatenate([_f32(uh_ref), uv], axis=0)
        u2, s1, s2, u3 = _conv_taps(gc, uu, w_ref, r == 0)
        dzv = dz_ref[...]
        du3 = jnp.concatenate([dzv * _f32(gb_ref), dzn_ref[...] * _f32(gbn_ref)], axis=0)
        rows = lax.broadcasted_iota(jnp.int32, du3.shape, 0)
        du3 = jnp.where((rows >= tr) & (r == nr - 1), 0.0, du3)
        n1 = pltpu.roll(du3, tr + HALO - 1, 0)
        n2 = pltpu.roll(du3, tr + HALO - 2, 0)
        du2 = (w_ref[2:3, :] * du3 + w_ref[1:2, :] * n1 + w_ref[0:1, :] * n2)[:tr]
        o_ref[:, :D_MODEL] = (dzv * u3[HALO:]).astype(o_ref.dtype)
        o_ref[:, D_MODEL:2 * D_MODEL] = (du2 * uv).astype(o_ref.dtype)
        o_ref[:, 2 * D_MODEL:] = (du2 * gcv).astype(o_ref.dtype)
        d3 = du3[:tr]
        taps = [jnp.sum(d3 * t[HALO:], axis=0, keepdims=True) for t in (s2, s1, u2)]

        @pl.when(r == 0)
        def _():
            for kk in range(3):
                dw_ref[kk:kk + 1, :] = taps[kk]

        @pl.when(r != 0)
        def _():
            for kk in range(3):
                dw_ref[kk:kk + 1, :] += taps[kk]

    return pl.pallas_call(
        body, name="conv_bwd", grid=(nr,),
        in_specs=[main(0), nxt(0), main(0), nxt(0), main(1), main(2), prev(1), prev(2), _resident((3, D_MODEL))],
        out_specs=[pl.BlockSpec((tr, 3 * D_MODEL), lambda r: (r, 0)), _resident((3, D_MODEL))],
        out_shape=[jax.ShapeDtypeStruct((S, 3 * D_MODEL), MXU_DTYPE), jax.ShapeDtypeStruct((3, D_MODEL), F32)],
        compiler_params=_params(1),
    )(dz, dz, bcu, bcu, bcu, bcu, bcu, bcu, cw)


def loss_head(y, target):
    S, D = y.shape
    tm = min(TM, S)

    def body(y_ref, t_ref, l_ref, dy_ref, dyb_ref):
        err = y_ref[...] - t_ref[...]
        dy = err / D
        dy_ref[...] = dy
        dyb_ref[...] = dy.astype(dyb_ref.dtype)
        part = jnp.full((1, 128), jnp.sum(err * err), F32)

        @pl.when(pl.program_id(0) == 0)
        def _():
            l_ref[...] = part

        @pl.when(pl.program_id(0) != 0)
        def _():
            l_ref[...] += part

    blk = pl.BlockSpec((tm, D), lambda i: (i, 0))
    return pl.pallas_call(
        body, name="loss_head", grid=(S // tm,),
        in_specs=[blk, blk],
        out_specs=[pl.BlockSpec((1, 128), lambda i: (0, 0)), blk, blk],
        out_shape=[jax.ShapeDtypeStruct((1, 128), F32), jax.ShapeDtypeStruct((S, D), F32),
                   jax.ShapeDtypeStruct((S, D), MXU_DTYPE)],
        compiler_params=_params(1),
    )(y, target)


def adamw(parts, w, m, v, *, name):
    R, C = w.shape
    tr = R
    while tr * C * 4 > (1 << 20) and tr % 32 == 0:
        tr //= 2

    def body(p_ref, w_ref, m_ref, v_ref, g_ref, d_ref, mo_ref, vo_ref):
        g = p_ref[0].astype(F32)
        for d in range(1, N_DEV):
            g = g + p_ref[d].astype(F32)
        m_new = ADAM_B1 * m_ref[...] + (1.0 - ADAM_B1) * g
        v_new = ADAM_B2 * v_ref[...] + (1.0 - ADAM_B2) * (g * g)
        m_hat = m_new / (1.0 - ADAM_B1 ** ADAM_STEP)
        v_hat = v_new / (1.0 - ADAM_B2 ** ADAM_STEP)
        g_ref[...] = g
        d_ref[...] = -ADAM_LR * (m_hat / (jnp.sqrt(v_hat) + ADAM_EPS) + ADAM_WD * w_ref[...])
        mo_ref[...] = m_new
        vo_ref[...] = v_new

    blk = pl.BlockSpec((tr, C), lambda i: (i, 0))
    return pl.pallas_call(
        body, name=name, grid=(R // tr,),
        in_specs=[pl.BlockSpec((N_DEV, tr, C), lambda i: (0, i, 0)), blk, blk, blk],
        out_specs=[blk, blk, blk, blk],
        out_shape=[jax.ShapeDtypeStruct((R, C), F32)] * 4,
        compiler_params=_params(1),
    )(parts, w, m, v)


def _mesh_place():
    x, y, c = (lax.axis_index(n) for n in MESH_AXES)
    return x, y, c, 4 * x + 2 * y + c


def _peer(x, y, c, d):
    px = 1 - x if d & 4 else x
    py = 1 - y if d & 2 else y
    pc = 1 - c if d & 1 else c
    return (px, py, pc), 4 * px + 2 * py + pc


def gather_now(srcs, *, name):
    n = len(srcs)
    any_spec = pl.BlockSpec(memory_space=pl.ANY)

    def body(*refs):
        ins, outs, token = refs[:n], refs[n:2 * n], refs[2 * n]
        send_sems, recv_sems, local_sems = refs[2 * n + 1:]
        token[...] = jnp.zeros_like(token)
        x, y, c, me = _mesh_place()
        for a in range(n):
            pltpu.make_async_copy(ins[a], outs[a].at[me], local_sems.at[a]).start()
            for d in range(1, N_DEV):
                pltpu.make_async_remote_copy(
                    src_ref=ins[a], dst_ref=outs[a].at[me], send_sem=send_sems.at[a], recv_sem=recv_sems.at[a],
                    device_id=_peer(x, y, c, d)[0], device_id_type=pl.DeviceIdType.MESH).start()
        for a in range(n):
            pltpu.make_async_copy(ins[a], outs[a].at[me], local_sems.at[a]).wait()
            seven = outs[a].at[pl.ds(0, N_DEV - 1)]
            drain = pltpu.make_async_remote_copy(
                src_ref=seven, dst_ref=seven, send_sem=send_sems.at[a], recv_sem=recv_sems.at[a],
                device_id=(x, y, c), device_id_type=pl.DeviceIdType.MESH)
            drain.wait_send()
            drain.wait_recv()

    out = pl.pallas_call(
        body, name=name,
        in_specs=[any_spec] * n, out_specs=[any_spec] * n + [pl.BlockSpec(memory_space=pltpu.VMEM)],
        out_shape=[jax.ShapeDtypeStruct((N_DEV,) + s.shape, s.dtype) for s in srcs] + [jax.ShapeDtypeStruct((8, 128), F32)],
        scratch_shapes=[pltpu.SemaphoreType.DMA((n,)), pltpu.SemaphoreType.DMA((n,)), pltpu.SemaphoreType.DMA((n,))],
    )(*srcs)
    return out[:n], out[n]


_ANY = pl.BlockSpec(memory_space=pl.ANY)
_HBM = pl.BlockSpec(memory_space=pltpu.HBM)
_SEM = pl.BlockSpec(memory_space=pltpu.SEMAPHORE)


def _in_hbm(arrays):
    return [pltpu.with_memory_space_constraint(a, pltpu.HBM) for a in arrays]


def exchange_start(srcs, lands, slots, *, name):
    n, m = len(srcs), len(lands)

    def body(*refs):
        ins, zones = refs[:n], refs[n:n + m]
        send_sems, recv_sems, token = refs[n + m], refs[n + m + 1], refs[-1]
        x, y, c, me = _mesh_place()
        for a in range(n):
            for d in range(1, N_DEV):
                peer, peer_lin = _peer(x, y, c, d)
                src = ins[a] if slots is None else ins[a].at[peer_lin]
                dst = zones[a].at[me] if slots is None else zones[slots[a][0]].at[me, slots[a][1]]
                pltpu.make_async_remote_copy(
                    src_ref=src, dst_ref=dst, send_sem=send_sems.at[a], recv_sem=recv_sems.at[a],
                    device_id=peer, device_id_type=pl.DeviceIdType.MESH).start()
        token[...] = jnp.zeros_like(token)

    both = list(srcs) + list(lands)
    out = pl.pallas_call(
        body, name=name,
        in_specs=[_HBM] * (n + m),
        out_specs=[_SEM, _SEM] + [_HBM] * (n + m) + [pl.BlockSpec(memory_space=pltpu.VMEM)],
        out_shape=[pltpu.SemaphoreType.DMA((n,)), pltpu.SemaphoreType.DMA((n,))]
        + [pltpu.HBM(a.shape, a.dtype) for a in both] + [jax.ShapeDtypeStruct((8, 128), F32)],
        input_output_aliases={i: 2 + i for i in range(n + m)},
        compiler_params=pltpu.CompilerParams(has_side_effects=pltpu.SideEffectType.DATAFLOW_SIDE_EFFECTING),
    )(*_in_hbm(both))
    return out[0], out[1], out[2:2 + n], out[2 + n:2 + n + m], out[-1]


def exchange_wait(send_sems, recv_sems, srcs, lands, slots, after, *, name):
    n, m = len(srcs), len(lands)

    def body(*refs):
        ins, zones = refs[:n], refs[n:n + m]
        send_ref, recv_ref = refs[n + m], refs[n + m + 1]
        x, y, c, _ = _mesh_place()
        for a in range(n):
            seven = (zones[a] if slots is None else ins[a]).at[pl.ds(0, N_DEV - 1)]
            drain = pltpu.make_async_remote_copy(
                src_ref=seven, dst_ref=seven, send_sem=send_ref.at[a], recv_sem=recv_ref.at[a],
                device_id=(x, y, c), device_id_type=pl.DeviceIdType.MESH)
            drain.wait_send()
            drain.wait_recv()

    both = list(srcs) + list(lands)
    out = pl.pallas_call(
        body, name=name,
        in_specs=[_HBM] * (n + m) + [_SEM, _SEM, _ANY],
        out_specs=[_HBM] * (n + m),
        out_shape=[pltpu.HBM(a.shape, a.dtype) for a in both],
        input_output_aliases={i: i for i in range(n + m)},
        compiler_params=pltpu.CompilerParams(has_side_effects=pltpu.SideEffectType.DATAFLOW_SIDE_EFFECTING),
    )(*both, send_sems, recv_sems, after)
    return out[:n], out[n:]


def scatter_finish(remote, lands, vec, *, name):
    n, m = len(remote), len(lands)

    def body(*refs):
        ins, vec_ref, zones_in = refs[:n], refs[n], refs[n + 1:n + 1 + m]
        vec_out = refs[n + 1 + 2 * m]
        send_sems, recv_sems, local_sems = refs[n + 2 + 2 * m:]
        x, y, c, me = _mesh_place()

        def ends(a, j):
            if a == n:
                return vec_ref, vec_out.at[me]
            return ins[a].at[j], zones_in[remote[a][1]].at[me, remote[a][2]]

        for a in range(n + 1):
            pltpu.make_async_copy(*ends(a, me), local_sems.at[a]).start()
            for d in range(1, N_DEV):
                peer, peer_lin = _peer(x, y, c, d)
                src, dst = ends(a, peer_lin)
                pltpu.make_async_remote_copy(
                    src_ref=src, dst_ref=dst, send_sem=send_sems.at[a], recv_sem=recv_sems.at[a],
                    device_id=peer, device_id_type=pl.DeviceIdType.MESH).start()
        for a in range(n + 1):
            pltpu.make_async_copy(*ends(a, me), local_sems.at[a]).wait()
            seven = (vec_out if a == n else ins[a]).at[pl.ds(0, N_DEV - 1)]
            drain = pltpu.make_async_remote_copy(
                src_ref=seven, dst_ref=seven, send_sem=send_sems.at[a], recv_sem=recv_sems.at[a],
                device_id=(x, y, c), device_id_type=pl.DeviceIdType.MESH)
            drain.wait_send()
            drain.wait_recv()

    out = pl.pallas_call(
        body, name=name,
        in_specs=[_ANY] * (n + 1 + m), out_specs=[_ANY] * (m + 1),
        out_shape=[jax.ShapeDtypeStruct(z.shape, z.dtype) for z in lands]
        + [jax.ShapeDtypeStruct((N_DEV,) + vec.shape, vec.dtype)],
        input_output_aliases={n + 1 + i: i for i in range(m)},
        scratch_shapes=[pltpu.SemaphoreType.DMA((n + 1,))] * 3,
    )(*[e[0] for e in remote], vec, *lands)
    return out[:m], out[m]


def _rope_tables(pos):
    inv_freq = ROPE_THETA ** (-jnp.arange(0, ROPE, 2, dtype=F32) / ROPE)
    ang = pos.astype(F32)[:, None] * inv_freq
    cos, sin = jnp.cos(ang), jnp.sin(ang)
    pad = jnp.zeros((pos.shape[0], 128 - ROPE), F32)
    return jnp.concatenate([cos, cos, pad + 1.0], axis=1), jnp.concatenate([-sin, sin, pad], axis=1)


def _pad_last(w, n):
    return jnp.pad(w, [(0, 0)] * (w.ndim - 1) + [(0, n - w.shape[-1])])


def kernel(x, positions, g_mix, g_mlp, attn_w_down, attn_g_q_a, attn_g_kv_a, attn_w_uq, attn_w_ukv, attn_g_qnorm, attn_g_knorm, attn_w_o, conv_w_in, conv_w, conv_w_out, mlp_w1, mlp_w2, loss_target, m_g_mix, m_g_mlp, m_attn_w_down, m_attn_g_q_a, m_attn_g_kv_a, m_attn_w_uq, m_attn_w_ukv, m_attn_g_qnorm, m_attn_g_knorm, m_attn_w_o, m_conv_w_in, m_conv_w, m_conv_w_out, m_mlp_w1, m_mlp_w2, v_g_mix, v_g_mlp, v_attn_w_down, v_attn_g_q_a, v_attn_g_kv_a, v_attn_w_uq, v_attn_w_ukv, v_attn_g_qnorm, v_attn_g_knorm, v_attn_w_o, v_conv_w_in, v_conv_w, v_conv_w_out, v_mlp_w1, v_mlp_w2):
    weights = dict(g_mix=g_mix, g_mlp=g_mlp, attn_w_down=attn_w_down, attn_g_q_a=attn_g_q_a, attn_g_kv_a=attn_g_kv_a,
                   attn_w_uq=attn_w_uq, attn_w_ukv=attn_w_ukv, attn_g_qnorm=attn_g_qnorm, attn_g_knorm=attn_g_knorm,
                   attn_w_o=attn_w_o, conv_w_in=conv_w_in, conv_w=conv_w, conv_w_out=conv_w_out, mlp_w1=mlp_w1, mlp_w2=mlp_w2)
    mom1 = dict(g_mix=m_g_mix, g_mlp=m_g_mlp, attn_w_down=m_attn_w_down, attn_g_q_a=m_attn_g_q_a, attn_g_kv_a=m_attn_g_kv_a,
                attn_w_uq=m_attn_w_uq, attn_w_ukv=m_attn_w_ukv, attn_g_qnorm=m_attn_g_qnorm, attn_g_knorm=m_attn_g_knorm,
                attn_w_o=m_attn_w_o, conv_w_in=m_conv_w_in, conv_w=m_conv_w, conv_w_out=m_conv_w_out, mlp_w1=m_mlp_w1, mlp_w2=m_mlp_w2)
    mom2 = dict(g_mix=v_g_mix, g_mlp=v_g_mlp, attn_w_down=v_attn_w_down, attn_g_q_a=v_attn_g_q_a, attn_g_kv_a=v_attn_g_kv_a,
                attn_w_uq=v_attn_w_uq, attn_w_ukv=v_attn_w_ukv, attn_g_qnorm=v_attn_g_qnorm, attn_g_knorm=v_attn_g_knorm,
                attn_w_o=v_attn_w_o, conv_w_in=v_conv_w_in, conv_w=v_conv_w, conv_w_out=v_conv_w_out, mlp_w1=v_mlp_w1, mlp_w2=v_mlp_w2)
    big = ["attn_w_down", "attn_w_uq", "attn_w_ukv", "attn_w_o", "conv_w_in", "conv_w", "conv_w_out", "mlp_w1", "mlp_w2"]
    small = ["g_mix", "g_mlp", "attn_g_q_a", "attn_g_kv_a", "attn_g_qnorm", "attn_g_knorm"]
    order = ["g_mix", "g_mlp", "attn_w_down", "attn_g_q_a", "attn_g_kv_a", "attn_w_uq", "attn_w_ukv", "attn_g_qnorm",
             "attn_g_knorm", "attn_w_o", "conv_w_in", "conv_w", "conv_w_out", "mlp_w1", "mlp_w2"]

    xs = x[0]
    pos = positions[0]
    target = loss_target[0]
    S = xs.shape[0]
    depth = g_mix.shape[0]
    cos_t, sin_t = _rope_tables(pos)
    pos_col, pos_row = pos.reshape(S, 1), pos.reshape(1, S)

    keys, shards = [], []
    for name in big:
        for l in range(weights[name].shape[0]):
            keys.append((name, l))
            shards.append(weights[name][l] if name == "conv_w" else weights[name][l].astype(WIRE_DTYPE))
    first = [j for j, (name, l) in enumerate(keys) if l == 0 and name.startswith("attn")]
    second = [j for j, (name, l) in enumerate(keys) if l == 0 and name.startswith("mlp")]
    later = [j for j in range(len(keys)) if j not in first + second]
    me = 4 * lax.axis_index("x") + 2 * lax.axis_index("y") + lax.axis_index("c")

    def zones_with_own(js, token):
        return [lax.dynamic_update_slice(lax.empty((N_DEV,) + shards[j].shape, shards[j].dtype),
                                         (shards[j] + token[0, 0].astype(shards[j].dtype))[None],
                                         (me,) + (0,) * shards[j].ndim) for j in js]

    arrived, token = gather_now([shards[j] for j in first], name="gather_first")
    full = dict(zip([keys[j] for j in first], arrived))
    g1 = exchange_start([shards[j] for j in second], zones_with_own(second, token), None, name="gather_mlp0_start")
    g2 = exchange_start([shards[j] for j in later], zones_with_own(later, g1[4]), None, name="gather_rest_start")
    g_mix_0 = g_mix[0] + g2[4][0, 0]

    def rows(name, l):
        g = full[(name, l)]
        return g.reshape(g.shape[0] * g.shape[1], g.shape[2])

    saved = []
    for i in range(depth):
        l = i // 2
        rec = {"x0": xs}
        if i == 1:
            arrived = exchange_wait(*g2[:4], None, xs, name="gather_rest_wait")[1]
            full.update(zip([keys[j] for j in later], arrived))
        if i % 2 == 0:
            wd3 = _pad_last(rows("attn_w_down", l), DOWN_PAD)[None]
            wuq3 = _pad_last(full[("attn_w_uq", l)], QK_PAD)
            wukv3 = full[("attn_w_ukv", l)]
            gqn = _pad_last(attn_g_qnorm[l][None], QK_PAD)
            gkn = _pad_last(attn_g_knorm[l][None], QK_PAD)
            gqa, gkva = attn_g_q_a[l][None], attn_g_kv_a[l][None]
            h, a = norm_matmul(xs, g_mix_0 if i == 0 else g_mix[i], wd3, name="attn_down")
            q, k, v, cq, ckv = mla_pre_fwd(a, gqa, gkva, wuq3, wukv3, gqn, gkn, cos_t, sin_t)
            o, lse = attn_fwd(q, k, v, pos_col, pos_row)
            x1 = matmul_residual(o, rows("attn_w_o", l), xs, name="attn_out")
            rec.update(h=h, a=a, q=q, k=k, v=v, cq=cq, ckv=ckv, o=o, lse=lse, wd3=wd3, wuq3=wuq3, wukv3=wukv3,
                       gqn=gqn, gkn=gkn, gqa=gqa, gkva=gkva)
        else:
            cw = full[("conv_w", l)].transpose(1, 0, 2).reshape(3, D_MODEL)
            h, bcu = norm_matmul(xs, g_mix[i], full[("conv_w_in", l)], name="conv_in", out_dtype=MXU_DTYPE)
            z = conv_fwd(bcu, cw)
            x1 = matmul_residual(z, rows("conv_w_out", l), xs, name="conv_out")
            rec.update(h=h, bcu=bcu, z=z, cw=cw)
        if i == 0:
            arrived = exchange_wait(*g1[:4], None, x1, name="gather_mlp0_wait")[1]
            full.update(zip([keys[j] for j in second], arrived))
        h2, act, slope = norm_matmul(x1, g_mlp[i], full[("mlp_w1", i)], name="mlp_up", mlp=True)
        xs = matmul_residual(act, rows("mlp_w2", i), x1, name="mlp_down")
        rec.update(x1=x1, h2=h2, act=act, slope=slope)
        saved.append(rec)

    sq, dx, dxb = loss_head(xs, target)
    loss = lax.psum(sq[0, 0] * (0.5 / D_MODEL), MESH_AXES)

    grads = {name: [None] * weights[name].shape[0] for name in order}
    token = None
    for i in reversed(range(depth)):
        l = i // 2
        rec = saved[i]
        grads["mlp_w2"][i] = mm_tn(rec["act"], dxb, name="mlp_down_dw", G=1, out_dtype=WIRE_DTYPE).reshape(N_DEV, -1, D_MODEL)
        du = matmul_nt(dxb, rows("mlp_w2", i)[None], name="mlp_down_dx", epi="mlp_du", u=rec["slope"])
        grads["mlp_w1"][i] = mm_tn(rec["h2"], du, name="mlp_up_dw", G=N_DEV, out_dtype=WIRE_DTYPE)
        dx1, dx1b, dg = matmul_nt(du, full[("mlp_w1", i)], name="mlp_up_dx", epi="rms_bwd", x=rec["x1"], g=g_mlp[i], dx=dx)
        grads["g_mlp"][i] = dg[0]
        if i == 0:
            flying = [keys[j] for j in second + later]
            srcs = [grads[name][l_] for name, l_ in flying]
            slots = [(big.index(name), l_) for name, l_ in flying]
            zones = [lax.empty((N_DEV, weights[name].shape[0]) + grads[name][-1].shape[1:], grads[name][-1].dtype)
                     for name in big]
            for src, (k, l_) in zip(srcs, slots):
                own = lax.dynamic_index_in_dim(src, me, 0, keepdims=True)[None]
                zones[k] = lax.dynamic_update_slice(zones[k], own, (me, l_) + (0,) * (src.ndim - 1))
            s_send, s_recv, s_srcs, s_zones, token = exchange_start(srcs, zones, slots, name="scatter_rest_start")
        if i % 2 == 0:
            grads["attn_w_o"][l] = mm_tn(rec["o"], dx1b, name="attn_out_dw", G=1, out_dtype=WIRE_DTYPE,
                                         after=token).reshape(N_DEV, -1, D_MODEL)
            do = matmul_nt(dx1b, rows("attn_w_o", l)[None], name="attn_out_dx")
            dq, dk, dv = attn_bwd(rec["q"], rec["k"], rec["v"], do, rec["lse"], attn_delta(rec["o"], do), pos_col, pos_row)
            da, dwuq, dwukv, dgqn, dgkn, dgqa, dgkva = mla_pre_bwd(
                dq, dk, dv, rec["a"], rec["cq"], rec["ckv"], rec["gqa"], rec["gkva"], rec["wuq3"], rec["wukv3"],
                rec["gqn"], rec["gkn"], cos_t, sin_t)
            grads["attn_w_uq"][l] = dwuq[:, :, :QK_DIM].astype(WIRE_DTYPE)
            grads["attn_w_ukv"][l] = dwukv.astype(WIRE_DTYPE)
            grads["attn_g_qnorm"][l] = dgqn[0, :QK_DIM]
            grads["attn_g_knorm"][l] = dgkn[0, :QK_DIM]
            grads["attn_g_q_a"][l] = dgqa[0]
            grads["attn_g_kv_a"][l] = dgkva[0]
            dwd = mm_tn(rec["h"], da, name="attn_down_dw", G=1, out_dtype=WIRE_DTYPE)
            grads["attn_w_down"][l] = dwd[0, :, :DOWN].reshape(N_DEV, -1, DOWN)
            dx, dxb, dg = matmul_nt(da, rec["wd3"], name="attn_down_dx", epi="rms_bwd", x=rec["x0"], g=g_mix[i], dx=dx1)
        else:
            grads["conv_w_out"][l] = mm_tn(rec["z"], dx1b, name="conv_out_dw", G=1, out_dtype=WIRE_DTYPE).reshape(N_DEV, -1, D_MODEL)
            dz = matmul_nt(dx1b, rows("conv_w_out", l)[None], name="conv_out_dx")
            dbcu, dcw = conv_bwd(dz, rec["bcu"], rec["cw"])
            grads["conv_w"][l] = dcw.reshape(3, N_DEV, -1).transpose(1, 0, 2)
            grads["conv_w_in"][l] = mm_tn(rec["h"], dbcu, name="conv_in_dw", G=N_DEV, out_dtype=WIRE_DTYPE)
            dx, dxb, dg = matmul_nt(dbcu, full[("conv_w_in", l)], name="conv_in_dx", epi="rms_bwd", x=rec["x0"], g=g_mix[i], dx=dx1)
        grads["g_mix"][i] = dg[0]

    sizes = [weights[name].size for name in small]
    n_small = sum(sizes)
    rows_small = -(-n_small // (8 * 128)) * 8

    def pack(tree):
        flat = jnp.concatenate([jnp.stack(tree[name]).reshape(-1) if isinstance(tree[name], list) else tree[name].reshape(-1)
                                for name in small])
        return jnp.pad(flat, (0, rows_small * 128 - n_small)).reshape(rows_small, 128)

    s_srcs, s_zones = exchange_wait(s_send, s_recv, s_srcs, s_zones, slots, dx, name="scatter_rest_wait")
    remote = [(grads[name][l_], big.index(name), l_) for name, l_ in (keys[j] for j in first)]
    parts, gain_parts = scatter_finish(remote, s_zones, pack(grads), name="scatter_last")

    out = {}
    for name, part in zip(big, parts):
        w = weights[name]
        flat = lambda t: t.reshape(-1, t.shape[-1])
        res = adamw(part.reshape(N_DEV, -1, w.shape[-1]), flat(w), flat(mom1[name]), flat(mom2[name]), name="adamw_" + name)
        out[name] = [r.reshape(w.shape) for r in res]
    res = adamw(gain_parts, pack(weights), pack(mom1), pack(mom2), name="adamw_gains")
    offset = 0
    for name, size in zip(small, sizes):
        out[name] = [r.reshape(-1)[offset:offset + size].reshape(weights[name].shape) for r in res]
        offset += size

    return (loss, dx[None], *[out[n][0] for n in order], *[out[n][1] for n in order],
            *[out[n][2] for n in order], *[out[n][3] for n in order])
```

```python
import jax
import jax.numpy as jnp
import numpy as np
from jax import lax
from jax.experimental import pallas as pl
from jax.experimental.pallas import tpu as pltpu

F32 = jnp.float32
MXU_DTYPE = jnp.bfloat16
WIRE_DTYPE = jnp.bfloat16

D_MODEL = 1024
N_HEADS = 8
NOPE = 128
ROPE = 64
QK_DIM = NOPE + ROPE
QK_PAD = 256
V_DIM = 128
Q_LORA = 256
KV_LORA = 128
DOWN = Q_LORA + KV_LORA + ROPE
DOWN_PAD = 512
ROPE_THETA = 10000.0
EPS = 1e-6
SM_SCALE = QK_DIM ** -0.5
LOG2E = 1.4426950408889634
Q_PRESCALE = SM_SCALE * LOG2E
ADAM_LR, ADAM_B1, ADAM_B2, ADAM_EPS, ADAM_WD, ADAM_STEP = 0.001, 0.9, 0.999, 1e-08, 0.01, 10
N_DEV = 8
MESH_AXES = ("x", "y", "c")

TM = 512
TM_WIDE = 1024
TILE_BUDGET = 32 << 20
TM_TOKENS_TN = 2048
TQ = 512
HEADS_FWD = 8
TQ_BWD = 2048
BWD_CHUNK = 256
TROW = 256
HALO = 16
VMEM_LIMIT = 48 << 20

NN = (((1,), (0,)), ((), ()))
NT = (((1,), (1,)), ((), ()))
TN = (((0,), (0,)), ((), ()))


def _dot(a, b, dims=NN):
    return lax.dot_general(a.astype(MXU_DTYPE), b.astype(MXU_DTYPE), dims, preferred_element_type=F32)


def _params(n_axes):
    return pltpu.CompilerParams(dimension_semantics=("arbitrary",) * n_axes, vmem_limit_bytes=VMEM_LIMIT)


def _rms(xv, n):
    r = lax.rsqrt(jnp.sum(xv * xv, axis=-1, keepdims=True) / n + EPS)
    return xv * r, r


def _rms_bwd(dy, xhat, r, g, n):
    dg = jnp.sum(dy * xhat, axis=0, keepdims=True)
    dxh = dy * g
    dx = r * (dxh - xhat * (jnp.sum(dxh * xhat, axis=-1, keepdims=True) / n))
    return dx, dg


def _swap_halves(t):
    lane = lax.broadcasted_iota(jnp.int32, t.shape, 1)
    return jnp.where(lane < ROPE // 2, pltpu.roll(t, 128 - ROPE // 2, 1), pltpu.roll(t, ROPE // 2, 1))


def _rope(t, cos_t, sin_t):
    return t * cos_t + _swap_halves(t) * sin_t


def _rope_bwd(dout, cos_t, sin_t):
    return dout * cos_t + _swap_halves(dout * sin_t)


def _resident(shape):
    return pl.BlockSpec(shape, lambda i: (0,) * len(shape))


def _token_tile(S, row_bytes, resident_bytes):
    wide = min(TM_WIDE, S)
    return wide if 2 * (wide * row_bytes + resident_bytes) <= TILE_BUDGET else min(TM, S)


def norm_matmul(x, g, w3, *, name, mlp=False, out_dtype=F32):
    if mlp:
        out_dtype = MXU_DTYPE
    S, D = x.shape
    G, _, Nb = w3.shape
    N = G * Nb
    n_out = 2 if mlp else 1
    tm = _token_tile(S, D * 4 + D * 2 + n_out * N * jnp.dtype(out_dtype).itemsize, w3.size * w3.dtype.itemsize)

    def body(x_ref, g_ref, w_ref, h_ref, o_ref, *slope_ref):
        xv = x_ref[...]
        r = lax.rsqrt(jnp.mean(xv * xv, axis=-1, keepdims=True) + EPS)
        h = (xv * r * g_ref[...]).astype(h_ref.dtype)
        h_ref[...] = h.T
        for gi in range(G):
            cols = slice(gi * Nb, (gi + 1) * Nb)
            acc = _dot(h, w_ref[gi])
            if mlp:
                acc = jnp.maximum(acc, 0.0)
                slope_ref[0][:, cols] = (2.0 * acc).astype(out_dtype)
                acc = jnp.square(acc)
            o_ref[:, cols] = acc.astype(o_ref.dtype)

    rows = lambda w: pl.BlockSpec((tm, w), lambda i: (i, 0))
    return pl.pallas_call(
        body, name=name, grid=(S // tm,),
        in_specs=[rows(D), _resident((1, D)), _resident((G, D, Nb))],
        out_specs=[pl.BlockSpec((D, tm), lambda i: (0, i))] + [rows(N)] * n_out,
        out_shape=[jax.ShapeDtypeStruct((D, S), MXU_DTYPE)] + [jax.ShapeDtypeStruct((S, N), out_dtype)] * n_out,
        compiler_params=_params(1),
    )(x, g.reshape(1, D), w3)


def matmul_residual(a, w, res, *, name):
    S, K = a.shape
    _, N = w.shape
    tm = _token_tile(S, K * a.dtype.itemsize + 2 * N * 4, w.size * w.dtype.itemsize)

    def body(a_ref, w_ref, r_ref, o_ref):
        o_ref[...] = r_ref[...] + _dot(a_ref[...], w_ref[...])

    rows = lambda w_: pl.BlockSpec((tm, w_), lambda i: (i, 0))
    return pl.pallas_call(
        body, name=name, grid=(S // tm,),
        in_specs=[rows(K), _resident((K, N)), rows(N)],
        out_specs=rows(N), out_shape=jax.ShapeDtypeStruct((S, N), F32),
        compiler_params=_params(1),
    )(a, w, res)


def matmul_nt(a, w3, *, name, epi="plain", u=None, x=None, g=None, dx=None):
    S, N = a.shape
    G, Ko, Nb = w3.shape
    assert N == G * Nb
    row_bytes = N * a.dtype.itemsize + Ko * {"plain": 4, "mlp_du": 4, "rms_bwd": 14}[epi]
    tm = _token_tile(S, row_bytes, w3.size * w3.dtype.itemsize)
    tko = min(Ko, 512)

    def body(a_ref, w_ref, *rest):
        if epi == "mlp_du":
            u_ref, o_ref = rest
            av = a_ref[...].astype(MXU_DTYPE)
            for j in range(Ko // tko):
                cols = slice(j * tko, (j + 1) * tko)
                da = _dot(av, w_ref[0, cols, :], NT)
                o_ref[:, cols] = (da * u_ref[:, cols].astype(F32)).astype(o_ref.dtype)
            return
        acc = _dot(a_ref[:, :Nb], w_ref[0], NT)
        for gi in range(1, G):
            acc = acc + _dot(a_ref[:, gi * Nb:(gi + 1) * Nb], w_ref[gi], NT)
        if epi == "plain":
            rest[0][...] = acc
        else:
            x_ref, g_ref, dx_ref, o_ref, ob_ref, dg_ref = rest
            xhat, r = _rms(x_ref[...], Ko)
            dxb, dg = _rms_bwd(acc, xhat, r, g_ref[...], Ko)
            dx_new = dx_ref[...] + dxb
            o_ref[...] = dx_new
            ob_ref[...] = dx_new.astype(ob_ref.dtype)

            @pl.when(pl.program_id(0) == 0)
            def _():
                dg_ref[...] = dg

            @pl.when(pl.program_id(0) != 0)
            def _():
                dg_ref[...] += dg

    rows = lambda w_: pl.BlockSpec((tm, w_), lambda i: (i, 0))
    in_specs = [rows(N), _resident((G, Ko, Nb))]
    args = [a, w3]
    if epi == "plain":
        out_shape, out_specs = jax.ShapeDtypeStruct((S, Ko), F32), rows(Ko)
    elif epi == "mlp_du":
        in_specs.append(rows(Ko))
        args.append(u)
        out_shape, out_specs = jax.ShapeDtypeStruct((S, Ko), MXU_DTYPE), rows(Ko)
    else:
        in_specs += [rows(Ko), _resident((1, Ko)), rows(Ko)]
        args += [x, g.reshape(1, Ko), dx]
        out_shape = [jax.ShapeDtypeStruct((S, Ko), F32), jax.ShapeDtypeStruct((S, Ko), MXU_DTYPE),
                     jax.ShapeDtypeStruct((1, Ko), F32)]
        out_specs = [rows(Ko), rows(Ko), _resident((1, Ko))]
    return pl.pallas_call(
        body, name=name, grid=(S // tm,),
        in_specs=in_specs, out_specs=out_specs, out_shape=out_shape,
        compiler_params=_params(1),
    )(*args)


def mm_tn(a, b, *, name, G, out_dtype, after=None, a_transposed=False):
    order = [] if after is None else [after]
    Ka, S = a.shape if a_transposed else a.shape[::-1]
    _, N = b.shape
    Nb = N // G
    tm = min(TM_TOKENS_TN if b.dtype.itemsize == 2 else TM_TOKENS_TN // 2, S)
    tka = min(Ka, 1024)
    tnb = Nb if Nb <= 1024 else 1024
    nj = Nb // tnb
    ns = S // tm

    def body(a_ref, b_ref, *rest):
        o_ref, acc = rest[-2:]
        s = pl.program_id(2)

        @pl.when(s == 0)
        def _():
            acc[...] = jnp.zeros_like(acc)

        acc[...] += _dot(a_ref[...], b_ref[...], NN if a_transposed else TN)

        @pl.when(s == ns - 1)
        def _():
            o_ref[...] = acc[...].astype(o_ref.dtype)

    a_spec = pl.BlockSpec((tka, tm), lambda i, j, s: (i, s)) if a_transposed else pl.BlockSpec((tm, tka), lambda i, j, s: (s, i))
    return pl.pallas_call(
        body, name=name, grid=(Ka // tka, G * nj, ns),
        in_specs=[a_spec,
                  pl.BlockSpec((tm, tnb), lambda i, j, s: (s, j))] + [pl.BlockSpec(memory_space=pl.ANY)] * len(order),
        out_specs=pl.BlockSpec((None, tka, tnb), lambda i, j, s: (j // nj, i, j % nj)),
        out_shape=jax.ShapeDtypeStruct((G, Ka, Nb), out_dtype),
        scratch_shapes=[pltpu.VMEM((tka, tnb), F32)],
        compiler_params=_params(3),
    )(a, b, *order)


def mla_pre_fwd(a, gqa, gkva, wuq3, wukv3, gqn, gkn, cos_t, sin_t):
    S = a.shape[0]
    tm = min(TM, S)
    H = N_HEADS

    def body(a_ref, gqa_ref, gkva_ref, wuq_ref, wukv_ref, gqn_ref, gkn_ref, cos_ref, sin_ref,
             q_ref, k_ref, v_ref, cq_ref, ckv_ref):
        av = a_ref[...]
        cq = (_rms(av[:, :Q_LORA], Q_LORA)[0] * gqa_ref[...]).astype(cq_ref.dtype)
        ckv = (_rms(av[:, Q_LORA:Q_LORA + KV_LORA], KV_LORA)[0] * gkva_ref[...]).astype(ckv_ref.dtype)
        cq_ref[...] = cq
        ckv_ref[...] = ckv
        kpe = av[:, Q_LORA + KV_LORA:]
        cos_v, sin_v = cos_ref[...], sin_ref[...]
        for h in range(H):
            qn = _rms(_dot(cq, wuq_ref[h]), QK_DIM)[0] * gqn_ref[...]
            qr = jnp.concatenate([qn[:, :NOPE], _rope(qn[:, NOPE:], cos_v, sin_v)], axis=1)
            q_ref[h] = (qr * Q_PRESCALE).astype(q_ref.dtype)
            kvp = _dot(ckv, wukv_ref[h])
            kn = _rms(jnp.concatenate([kvp[:, :NOPE], kpe], axis=1), QK_DIM)[0] * gkn_ref[...]
            k_ref[h] = jnp.concatenate([kn[:, :NOPE], _rope(kn[:, NOPE:], cos_v, sin_v)], axis=1).astype(k_ref.dtype)
            v_ref[h] = kvp[:, NOPE:].astype(v_ref.dtype)

    row = lambda w: pl.BlockSpec((tm, w), lambda i: (i, 0))
    heads = lambda w: pl.BlockSpec((H, tm, w), lambda i: (0, i, 0))
    return pl.pallas_call(
        body, name="mla_pre_fwd", grid=(S // tm,),
        in_specs=[row(DOWN_PAD), _resident((1, Q_LORA)), _resident((1, KV_LORA)),
                  _resident((H, Q_LORA, QK_PAD)), _resident((H, KV_LORA, NOPE + V_DIM)),
                  _resident((1, QK_PAD)), _resident((1, QK_PAD)), row(128), row(128)],
        out_specs=[heads(QK_PAD), heads(QK_PAD), heads(V_DIM), row(Q_LORA), row(KV_LORA)],
        out_shape=[jax.ShapeDtypeStruct((H, S, QK_PAD), MXU_DTYPE),
                   jax.ShapeDtypeStruct((H, S, QK_PAD), MXU_DTYPE),
                   jax.ShapeDtypeStruct((H, S, V_DIM), MXU_DTYPE),
                   jax.ShapeDtypeStruct((S, Q_LORA), MXU_DTYPE),
                   jax.ShapeDtypeStruct((S, KV_LORA), MXU_DTYPE)],
        compiler_params=_params(1),
    )(a, gqa, gkva, wuq3, wukv3, gqn, gkn, cos_t, sin_t)


def mla_pre_bwd(dq, dk, dv, a, cq, ckv, gqa, gkva, wuq3, wukv3, gqn, gkn, cos_t, sin_t):
    S = a.shape[0]
    tm = min(TM, S)
    H = N_HEADS

    def body(dq_ref, dk_ref, dv_ref, a_ref, cq_ref, ckv_ref, gqa_ref, gkva_ref, wuq_ref, wukv_ref, gqn_ref, gkn_ref,
             cos_ref, sin_ref, da_ref, dwuq_ref, dwukv_ref, dgqn_ref, dgkn_ref, dgqa_ref, dgkva_ref):
        @pl.when(pl.program_id(0) == 0)
        def _():
            for ref in (dwuq_ref, dwukv_ref, dgqn_ref, dgkn_ref, dgqa_ref, dgkva_ref):
                ref[...] = jnp.zeros_like(ref)

        av = a_ref[...]
        kpe = av[:, Q_LORA + KV_LORA:]
        cos_v, sin_v = cos_ref[...], sin_ref[...]
        cqv, ckvv = cq_ref[...], ckv_ref[...]
        dcq = jnp.zeros((tm, Q_LORA), F32)
        dckv = jnp.zeros((tm, KV_LORA), F32)
        dkpe = jnp.zeros((tm, 128), F32)
        dgqn = jnp.zeros((1, QK_PAD), F32)
        dgkn = jnp.zeros((1, QK_PAD), F32)
        up = lambda h: (_dot(cqv, wuq_ref[h]), _dot(ckvv, wukv_ref[h]))
        nxt = up(0)
        for h in range(H):
            wuq, wukv = wuq_ref[h], wukv_ref[h]
            qp, kvp = nxt
            if h + 1 < H:
                nxt = up(h + 1)
            qhat, rq = _rms(qp, QK_DIM)
            dqr = dq_ref[h] * SM_SCALE
            dqn = jnp.concatenate([dqr[:, :NOPE], _rope_bwd(dqr[:, NOPE:], cos_v, sin_v)], axis=1)
            dqp, dg = _rms_bwd(dqn, qhat, rq, gqn_ref[...], QK_DIM)
            dgqn = dgqn + dg
            dqp = dqp.astype(MXU_DTYPE)
            dwuq_ref[h] += _dot(cqv, dqp, TN)
            dcq = dcq + _dot(dqp, wuq, NT)
            khat, rk = _rms(jnp.concatenate([kvp[:, :NOPE], kpe], axis=1), QK_DIM)
            dkr = dk_ref[h] * (1.0 / LOG2E)
            dkn = jnp.concatenate([dkr[:, :NOPE], _rope_bwd(dkr[:, NOPE:], cos_v, sin_v)], axis=1)
            dkk, dg = _rms_bwd(dkn, khat, rk, gkn_ref[...], QK_DIM)
            dgkn = dgkn + dg
            dkpe = dkpe + dkk[:, NOPE:]
            dkvp = jnp.concatenate([dkk[:, :NOPE], dv_ref[h]], axis=1).astype(MXU_DTYPE)
            dwukv_ref[h] += _dot(ckvv, dkvp, TN)
            dckv = dckv + _dot(dkvp, wukv, NT)
        dgqn_ref[...] += dgqn
        dgkn_ref[...] += dgkn
        ahat, r = _rms(av[:, :Q_LORA], Q_LORA)
        daq, dg = _rms_bwd(dcq, ahat, r, gqa_ref[...], Q_LORA)
        dgqa_ref[...] += dg
        ahat, r = _rms(av[:, Q_LORA:Q_LORA + KV_LORA], KV_LORA)
        dakv, dg = _rms_bwd(dckv, ahat, r, gkva_ref[...], KV_LORA)
        dgkva_ref[...] += dg
        da_ref[...] = jnp.concatenate([daq, dakv, dkpe], axis=1)

    row = lambda w: pl.BlockSpec((tm, w), lambda i: (i, 0))
    heads = lambda w: pl.BlockSpec((H, tm, w), lambda i: (0, i, 0))
    return pl.pallas_call(
        body, name="mla_pre_bwd", grid=(S // tm,),
        in_specs=[heads(QK_PAD), heads(QK_PAD), heads(V_DIM), row(DOWN_PAD), row(Q_LORA), row(KV_LORA),
                  _resident((1, Q_LORA)), _resident((1, KV_LORA)),
                  _resident((H, Q_LORA, QK_PAD)), _resident((H, KV_LORA, NOPE + V_DIM)),
                  _resident((1, QK_PAD)), _resident((1, QK_PAD)), row(128), row(128)],
        out_specs=[row(DOWN_PAD), _resident((H, Q_LORA, QK_PAD)), _resident((H, KV_LORA, NOPE + V_DIM)),
                   _resident((1, QK_PAD)), _resident((1, QK_PAD)), _resident((1, Q_LORA)), _resident((1, KV_LORA))],
        out_shape=[jax.ShapeDtypeStruct((S, DOWN_PAD), F32),
                   jax.ShapeDtypeStruct((H, Q_LORA, QK_PAD), F32),
                   jax.ShapeDtypeStruct((H, KV_LORA, NOPE + V_DIM), F32),
                   jax.ShapeDtypeStruct((1, QK_PAD), F32), jax.ShapeDtypeStruct((1, QK_PAD), F32),
                   jax.ShapeDtypeStruct((1, Q_LORA), F32), jax.ShapeDtypeStruct((1, KV_LORA), F32)],
        compiler_params=_params(1),
    )(dq, dk, dv, a, cq, ckv, gqa, gkva, wuq3, wukv3, gqn, gkn, cos_t, sin_t)


def _pair_tables(nb, key_major):
    if key_major:
        pairs = [(qi, kj) for kj in range(nb) for qi in range(kj, nb)]
    else:
        pairs = [(qi, ki) for qi in range(nb) for ki in range(qi + 1)]
    return (jnp.asarray(np.array([p[0] for p in pairs], np.int32)),
            jnp.asarray(np.array([p[1] for p in pairs], np.int32)))


def _scores_t(k, q, pk_col, pq_row, masked):
    s = _dot(k, q, NT)
    return jnp.where(pq_row >= pk_col, s, jnp.finfo(F32).min) if masked else s


def attn_fwd(q, k, v, pos_col, pos_row):
    H, S, _ = q.shape
    t = min(TQ, S)
    nb = S // t
    hb = HEADS_FWD
    qt, kt = _pair_tables(nb, key_major=False)

    def body(qt_ref, kt_ref, q_ref, k_ref, v_ref, pk_ref, pq_ref, o_ref, lse_ref, m_s, l_s, acc):
        step = pl.program_id(1)
        qi, ki = qt_ref[step], kt_ref[step]

        @pl.when(ki == 0)
        def _():
            m_s[...] = jnp.full_like(m_s, -jnp.inf)
            l_s[...] = jnp.zeros_like(l_s)
            acc[...] = jnp.zeros_like(acc)

        def update(masked):
            scores = lambda hh: _scores_t(k_ref[hh], q_ref[hh], pk_ref[...], pq_ref[...], masked)
            def weighted_values(hh, p, alpha):
                acc[hh] = alpha * acc[hh] + _dot(v_ref[hh], p, TN)

            s_next = scores(0)
            pending = None
            for hh in range(hb):
                s = s_next
                if hh + 1 < hb:
                    s_next = scores(hh + 1)
                m_old = m_s[hh]
                m_new = jnp.maximum(m_old, jnp.max(s, axis=0, keepdims=True))
                p = jnp.exp2(s - m_new)
                alpha = jnp.exp2(m_old - m_new)
                l_s[hh] = alpha * l_s[hh] + jnp.sum(p, axis=0, keepdims=True)
                m_s[hh] = m_new
                if pending is not None:
                    weighted_values(*pending)
                pending = (hh, p, alpha)
            weighted_values(*pending)

        @pl.when(ki < qi)
        def _():
            update(False)

        @pl.when(ki == qi)
        def _():
            update(True)
            for hh in range(hb):
                o_ref[:, hh * V_DIM:(hh + 1) * V_DIM] = (acc[hh] / l_s[hh]).T
                lse_ref[hh] = m_s[hh] + jnp.log(l_s[hh]) * LOG2E

    grid_spec = pltpu.PrefetchScalarGridSpec(
        num_scalar_prefetch=2, grid=(H // hb, qt.shape[0]),
        in_specs=[pl.BlockSpec((hb, t, QK_PAD), lambda h, s, qt, kt: (h, qt[s], 0)),
                  pl.BlockSpec((hb, t, QK_PAD), lambda h, s, qt, kt: (h, kt[s], 0)),
                  pl.BlockSpec((hb, t, V_DIM), lambda h, s, qt, kt: (h, kt[s], 0)),
                  pl.BlockSpec((t, 1), lambda h, s, qt, kt: (kt[s], 0)),
                  pl.BlockSpec((1, t), lambda h, s, qt, kt: (0, qt[s]))],
        out_specs=[pl.BlockSpec((t, hb * V_DIM), lambda h, s, qt, kt: (qt[s], h)),
                   pl.BlockSpec((hb, 1, t), lambda h, s, qt, kt: (h, 0, qt[s]))],
        scratch_shapes=[pltpu.VMEM((hb, 1, t), F32), pltpu.VMEM((hb, 1, t), F32), pltpu.VMEM((hb, V_DIM, t), F32)])
    return pl.pallas_call(
        body, name="attn_fwd", grid_spec=grid_spec,
        out_shape=[jax.ShapeDtypeStruct((S, H * V_DIM), F32), jax.ShapeDtypeStruct((H, 1, S), F32)],
        compiler_params=_params(2),
    )(qt, kt, q, k, v, pos_col, pos_row)


def attn_delta(o, do):
    S = o.shape[0]
    t = min(TQ, S)

    def body(o_ref, do_ref, d_ref):
        for h in range(N_HEADS):
            cols = slice(h * V_DIM, (h + 1) * V_DIM)
            d_ref[h] = jnp.sum((o_ref[:, cols] * do_ref[:, cols]).T, axis=0, keepdims=True)

    blk = pl.BlockSpec((t, N_HEADS * V_DIM), lambda i: (i, 0))
    return pl.pallas_call(
        body, name="attn_delta", grid=(S // t,),
        in_specs=[blk, blk],
        out_specs=pl.BlockSpec((N_HEADS, 1, t), lambda i: (0, 0, i)),
        out_shape=jax.ShapeDtypeStruct((N_HEADS, 1, S), F32),
        compiler_params=_params(1),
    )(o, do)


def attn_bwd(q, k, v, do, lse, delta, pos_col, pos_row):
    H, S, _ = q.shape
    t = min(TQ_BWD, S)
    nb = S // t
    qt, kt = _pair_tables(nb, key_major=True)
    tc = min(BWD_CHUNK, t)

    def body(qt_ref, kt_ref, q_ref, k_ref, v_ref, do_ref, lse_ref, dl_ref, pk_ref, pq_ref, dq_ref, dk_ref, dv_ref):
        step = pl.program_id(1)
        qi, kj = qt_ref[step], kt_ref[step]

        @pl.when(step == 0)
        def _():
            dq_ref[...] = jnp.zeros_like(dq_ref)

        @pl.when(qi == kj)
        def _():
            dk_ref[...] = jnp.zeros_like(dk_ref)
            dv_ref[...] = jnp.zeros_like(dv_ref)

        def update(masked):
            seen = lambda c: (c + 1) * tc if masked else t

            def first_matmuls(c):
                cols, ke = slice(c * tc, (c + 1) * tc), seen(c)
                qc = q_ref[cols, :]
                doc = do_ref[cols, :].astype(MXU_DTYPE)
                s = _scores_t(k_ref[:ke, :], qc, pk_ref[:ke, :], pq_ref[:, cols], masked)
                return qc, doc, s, _dot(v_ref[:ke, :], doc, NT)

            nxt = first_matmuls(0)
            for c in range(t // tc):
                qc, doc, s, dp = nxt
                if c + 1 < t // tc:
                    nxt = first_matmuls(c + 1)
                cols, ke = slice(c * tc, (c + 1) * tc), seen(c)
                p = jnp.exp2(s - lse_ref[:, cols])
                ds = (p * (dp - dl_ref[:, cols])).astype(MXU_DTYPE)
                dv_ref[:ke, :] += _dot(p, doc)
                dk_ref[:ke, :] += _dot(ds, qc)
                rows = pl.ds(pl.multiple_of(qi * t + c * tc, tc), tc)
                dq_ref[rows, :] += _dot(ds, k_ref[:ke, :], TN)

        @pl.when(qi == kj)
        def _():
            update(True)

        @pl.when(qi != kj)
        def _():
            update(False)

    q_idx = lambda h, s, qt, kt: (h, qt[s], 0)
    k_idx = lambda h, s, qt, kt: (h, kt[s], 0)
    row_idx = lambda h, s, qt, kt: (h, 0, qt[s])
    grid_spec = pltpu.PrefetchScalarGridSpec(
        num_scalar_prefetch=2, grid=(H, qt.shape[0]),
        in_specs=[pl.BlockSpec((None, t, QK_PAD), q_idx),
                  pl.BlockSpec((None, t, QK_PAD), k_idx),
                  pl.BlockSpec((None, t, V_DIM), k_idx),
                  pl.BlockSpec((t, V_DIM), lambda h, s, qt, kt: (qt[s], h)),
                  pl.BlockSpec((None, 1, t), row_idx),
                  pl.BlockSpec((None, 1, t), row_idx),
                  pl.BlockSpec((t, 1), lambda h, s, qt, kt: (kt[s], 0)),
                  pl.BlockSpec((1, t), lambda h, s, qt, kt: (0, qt[s]))],
        out_specs=[pl.BlockSpec((None, S, QK_PAD), lambda h, s, qt, kt: (h, 0, 0)),
                   pl.BlockSpec((None, t, QK_PAD), k_idx),
                   pl.BlockSpec((None, t, V_DIM), k_idx)])
    return pl.pallas_call(
        body, name="attn_bwd", grid_spec=grid_spec,
        out_shape=[jax.ShapeDtypeStruct((H, S, QK_PAD), F32), jax.ShapeDtypeStruct((H, S, QK_PAD), F32),
                   jax.ShapeDtypeStruct((H, S, V_DIM), F32)],
        compiler_params=_params(2),
    )(qt, kt, q, k, v, do, lse, delta, pos_col, pos_row)


def _conv_specs(S, tr):
    hb = tr // HALO
    main = lambda third: pl.BlockSpec((tr, D_MODEL), lambda r: (r, third))
    prev = lambda third: pl.BlockSpec((HALO, D_MODEL), lambda r: (jnp.maximum(r * hb - 1, 0), third))
    nxt = lambda third: pl.BlockSpec((HALO, D_MODEL), lambda r: (jnp.minimum((r + 1) * hb, S // HALO - 1), third))
    return main, prev, nxt


def _f32(ref):
    return ref[...].astype(F32)


def _conv_taps(gc, uu, w_ref, first):
    u2 = gc * uu
    rows = lax.broadcasted_iota(jnp.int32, u2.shape, 0)
    u2 = jnp.where((rows < HALO) & first, 0.0, u2)
    s1 = pltpu.roll(u2, 1, 0)
    s2 = pltpu.roll(u2, 2, 0)
    u3 = w_ref[2:3, :] * u2 + w_ref[1:2, :] * s1 + w_ref[0:1, :] * s2
    return u2, s1, s2, u3


def conv_fwd(bcu, cw):
    S = bcu.shape[0]
    tr = min(TROW, S)
    main, prev, _ = _conv_specs(S, tr)

    def body(gb_ref, gc_ref, u_ref, gch_ref, uh_ref, w_ref, z_ref):
        gc = jnp.concatenate([_f32(gch_ref), _f32(gc_ref)], axis=0)
        uu = jnp.concatenate([_f32(uh_ref), _f32(u_ref)], axis=0)
        u3 = _conv_taps(gc, uu, w_ref, pl.program_id(0) == 0)[3]
        z_ref[...] = (_f32(gb_ref) * u3[HALO:]).astype(z_ref.dtype)

    return pl.pallas_call(
        body, name="conv_fwd", grid=(S // tr,),
        in_specs=[main(0), main(1), main(2), prev(1), prev(2), _resident((3, D_MODEL))],
        out_specs=main(0),
        out_shape=jax.ShapeDtypeStruct((S, D_MODEL), MXU_DTYPE),
        compiler_params=_params(1),
    )(bcu, bcu, bcu, bcu, bcu, cw)


def conv_bwd(dz, bcu, cw):
    S = bcu.shape[0]
    tr = min(TROW, S)
    nr = S // tr
    main, prev, nxt = _conv_specs(S, tr)

    def body(dz_ref, dzn_ref, gb_ref, gbn_ref, gc_ref, u_ref, gch_ref, uh_ref, w_ref, o_ref, dw_ref):
        r = pl.program_id(0)
        gcv, uv = _f32(gc_ref), _f32(u_ref)
        gc = jnp.concatenate([_f32(gch_ref), gcv], axis=0)
        uu = jnp.concatenate([_f32(uh_ref), uv], axis=0)
        u2, s1, s2, u3 = _conv_taps(gc, uu, w_ref, r == 0)
        dzv = dz_ref[...]
        du3 = jnp.concatenate([dzv * _f32(gb_ref), dzn_ref[...] * _f32(gbn_ref)], axis=0)
        rows = lax.broadcasted_iota(jnp.int32, du3.shape, 0)
        du3 = jnp.where((rows >= tr) & (r == nr - 1), 0.0, du3)
        n1 = pltpu.roll(du3, tr + HALO - 1, 0)
        n2 = pltpu.roll(du3, tr + HALO - 2, 0)
        du2 = (w_ref[2:3, :] * du3 + w_ref[1:2, :] * n1 + w_ref[0:1, :] * n2)[:tr]
        o_ref[:, :D_MODEL] = (dzv * u3[HALO:]).astype(o_ref.dtype)
        o_ref[:, D_MODEL:2 * D_MODEL] = (du2 * uv).astype(o_ref.dtype)
        o_ref[:, 2 * D_MODEL:] = (du2 * gcv).astype(o_ref.dtype)
        d3 = du3[:tr]
        taps = [jnp.sum(d3 * t[HALO:], axis=0, keepdims=True) for t in (s2, s1, u2)]

        @pl.when(r == 0)
        def _():
            for kk in range(3):
                dw_ref[kk:kk + 1, :] = taps[kk]

        @pl.when(r != 0)
        def _():
            for kk in range(3):
                dw_ref[kk:kk + 1, :] += taps[kk]

    return pl.pallas_call(
        body, name="conv_bwd", grid=(nr,),
        in_specs=[main(0), nxt(0), main(0), nxt(0), main(1), main(2), prev(1), prev(2), _resident((3, D_MODEL))],
        out_specs=[pl.BlockSpec((tr, 3 * D_MODEL), lambda r: (r, 0)), _resident((3, D_MODEL))],
        out_shape=[jax.ShapeDtypeStruct((S, 3 * D_MODEL), MXU_DTYPE), jax.ShapeDtypeStruct((3, D_MODEL), F32)],
        compiler_params=_params(1),
    )(dz, dz, bcu, bcu, bcu, bcu, bcu, bcu, cw)


def loss_head(y, target):
    S, D = y.shape
    tm = min(TM, S)

    def body(y_ref, t_ref, l_ref, dy_ref, dyb_ref):
        err = y_ref[...] - t_ref[...]
        dy = err / D
        dy_ref[...] = dy
        dyb_ref[...] = dy.astype(dyb_ref.dtype)
        part = jnp.full((1, 128), jnp.sum(err * err), F32)

        @pl.when(pl.program_id(0) == 0)
        def _():
            l_ref[...] = part

        @pl.when(pl.program_id(0) != 0)
        def _():
            l_ref[...] += part

    blk = pl.BlockSpec((tm, D), lambda i: (i, 0))
    return pl.pallas_call(
        body, name="loss_head", grid=(S // tm,),
        in_specs=[blk, blk],
        out_specs=[pl.BlockSpec((1, 128), lambda i: (0, 0)), blk, blk],
        out_shape=[jax.ShapeDtypeStruct((1, 128), F32), jax.ShapeDtypeStruct((S, D), F32),
                   jax.ShapeDtypeStruct((S, D), MXU_DTYPE)],
        compiler_params=_params(1),
    )(y, target)


def adamw(parts, w, m, v, *, name):
    R, C = w.shape
    tr = R
    while tr * C * 4 > (1 << 20) and tr % 32 == 0:
        tr //= 2

    def body(p_ref, w_ref, m_ref, v_ref, g_ref, d_ref, mo_ref, vo_ref):
        g = p_ref[0].astype(F32)
        for d in range(1, N_DEV):
            g = g + p_ref[d].astype(F32)
        m_new = ADAM_B1 * m_ref[...] + (1.0 - ADAM_B1) * g
        v_new = ADAM_B2 * v_ref[...] + (1.0 - ADAM_B2) * (g * g)
        m_hat = m_new / (1.0 - ADAM_B1 ** ADAM_STEP)
        v_hat = v_new / (1.0 - ADAM_B2 ** ADAM_STEP)
        g_ref[...] = g
        d_ref[...] = -ADAM_LR * (m_hat / (jnp.sqrt(v_hat) + ADAM_EPS) + ADAM_WD * w_ref[...])
        mo_ref[...] = m_new
        vo_ref[...] = v_new

    blk = pl.BlockSpec((tr, C), lambda i: (i, 0))
    return pl.pallas_call(
        body, name=name, grid=(R // tr,),
        in_specs=[pl.BlockSpec((N_DEV, tr, C), lambda i: (0, i, 0)), blk, blk, blk],
        out_specs=[blk, blk, blk, blk],
        out_shape=[jax.ShapeDtypeStruct((R, C), F32)] * 4,
        compiler_params=_params(1),
    )(parts, w, m, v)


def _mesh_place():
    x, y, c = (lax.axis_index(n) for n in MESH_AXES)
    return x, y, c, 4 * x + 2 * y + c


def _peer(x, y, c, d):
    px = 1 - x if d & 4 else x
    py = 1 - y if d & 2 else y
    pc = 1 - c if d & 1 else c
    return (px, py, pc), 4 * px + 2 * py + pc


def gather_now(srcs, *, name):
    n = len(srcs)
    any_spec = pl.BlockSpec(memory_space=pl.ANY)

    def body(*refs):
        ins, outs, token = refs[:n], refs[n:2 * n], refs[2 * n]
        send_sems, recv_sems, local_sems = refs[2 * n + 1:]
        token[...] = jnp.zeros_like(token)
        x, y, c, me = _mesh_place()
        for a in range(n):
            pltpu.make_async_copy(ins[a], outs[a].at[me], local_sems.at[a]).start()
            for d in range(1, N_DEV):
                pltpu.make_async_remote_copy(
                    src_ref=ins[a], dst_ref=outs[a].at[me], send_sem=send_sems.at[a], recv_sem=recv_sems.at[a],
                    device_id=_peer(x, y, c, d)[0], device_id_type=pl.DeviceIdType.MESH).start()
        for a in range(n):
            pltpu.make_async_copy(ins[a], outs[a].at[me], local_sems.at[a]).wait()
            seven = outs[a].at[pl.ds(0, N_DEV - 1)]
            drain = pltpu.make_async_remote_copy(
                src_ref=seven, dst_ref=seven, send_sem=send_sems.at[a], recv_sem=recv_sems.at[a],
                device_id=(x, y, c), device_id_type=pl.DeviceIdType.MESH)
            drain.wait_send()
            drain.wait_recv()

    out = pl.pallas_call(
        body, name=name,
        in_specs=[any_spec] * n, out_specs=[any_spec] * n + [pl.BlockSpec(memory_space=pltpu.VMEM)],
        out_shape=[jax.ShapeDtypeStruct((N_DEV,) + s.shape, s.dtype) for s in srcs] + [jax.ShapeDtypeStruct((8, 128), F32)],
        scratch_shapes=[pltpu.SemaphoreType.DMA((n,)), pltpu.SemaphoreType.DMA((n,)), pltpu.SemaphoreType.DMA((n,))],
    )(*srcs)
    return out[:n], out[n]


_ANY = pl.BlockSpec(memory_space=pl.ANY)
_HBM = pl.BlockSpec(memory_space=pltpu.HBM)
_SEM = pl.BlockSpec(memory_space=pltpu.SEMAPHORE)


def _in_hbm(arrays):
    return [pltpu.with_memory_space_constraint(a, pltpu.HBM) for a in arrays]


def exchange_start(srcs, lands, slots, *, name):
    n, m = len(srcs), len(lands)

    def body(*refs):
        ins, zones = refs[:n], refs[n:n + m]
        send_sems, recv_sems, token = refs[n + m], refs[n + m + 1], refs[-1]
        x, y, c, me = _mesh_place()
        for a in range(n):
            for d in range(1, N_DEV):
                peer, peer_lin = _peer(x, y, c, d)
                src = ins[a] if slots is None else ins[a].at[peer_lin]
                dst = zones[a].at[me] if slots is None else zones[slots[a][0]].at[me, slots[a][1]]
                pltpu.make_async_remote_copy(
                    src_ref=src, dst_ref=dst, send_sem=send_sems.at[a], recv_sem=recv_sems.at[a],
                    device_id=peer, device_id_type=pl.DeviceIdType.MESH).start()
        token[...] = jnp.zeros_like(token)

    both = list(srcs) + list(lands)
    out = pl.pallas_call(
        body, name=name,
        in_specs=[_HBM] * (n + m),
        out_specs=[_SEM, _SEM] + [_HBM] * (n + m) + [pl.BlockSpec(memory_space=pltpu.VMEM)],
        out_shape=[pltpu.SemaphoreType.DMA((n,)), pltpu.SemaphoreType.DMA((n,))]
        + [pltpu.HBM(a.shape, a.dtype) for a in both] + [jax.ShapeDtypeStruct((8, 128), F32)],
        input_output_aliases={i: 2 + i for i in range(n + m)},
        compiler_params=pltpu.CompilerParams(has_side_effects=pltpu.SideEffectType.DATAFLOW_SIDE_EFFECTING),
    )(*_in_hbm(both))
    return out[0], out[1], out[2:2 + n], out[2 + n:2 + n + m], out[-1]


def exchange_wait(send_sems, recv_sems, srcs, lands, slots, after, *, name):
    n, m = len(srcs), len(lands)

    def body(*refs):
        ins, zones = refs[:n], refs[n:n + m]
        send_ref, recv_ref = refs[n + m], refs[n + m + 1]
        x, y, c, _ = _mesh_place()
        for a in range(n):
            seven = (zones[a] if slots is None else ins[a]).at[pl.ds(0, N_DEV - 1)]
            drain = pltpu.make_async_remote_copy(
                src_ref=seven, dst_ref=seven, send_sem=send_ref.at[a], recv_sem=recv_ref.at[a],
                device_id=(x, y, c), device_id_type=pl.DeviceIdType.MESH)
            drain.wait_send()
            drain.wait_recv()

    both = list(srcs) + list(lands)
    out = pl.pallas_call(
        body, name=name,
        in_specs=[_HBM] * (n + m) + [_SEM, _SEM, _ANY],
        out_specs=[_HBM] * (n + m),
        out_shape=[pltpu.HBM(a.shape, a.dtype) for a in both],
        input_output_aliases={i: i for i in range(n + m)},
        compiler_params=pltpu.CompilerParams(has_side_effects=pltpu.SideEffectType.DATAFLOW_SIDE_EFFECTING),
    )(*both, send_sems, recv_sems, after)
    return out[:n], out[n:]


def scatter_finish(remote, lands, vec, *, name):
    n, m = len(remote), len(lands)

    def body(*refs):
        ins, vec_ref, zones_in = refs[:n], refs[n], refs[n + 1:n + 1 + m]
        vec_out = refs[n + 1 + 2 * m]
        send_sems, recv_sems, local_sems = refs[n + 2 + 2 * m:]
        x, y, c, me = _mesh_place()

        def ends(a, j):
            if a == n:
                return vec_ref, vec_out.at[me]
            return ins[a].at[j], zones_in[remote[a][1]].at[me, remote[a][2]]

        for a in range(n + 1):
            pltpu.make_async_copy(*ends(a, me), local_sems.at[a]).start()
            for d in range(1, N_DEV):
                peer, peer_lin = _peer(x, y, c, d)
                src, dst = ends(a, peer_lin)
                pltpu.make_async_remote_copy(
                    src_ref=src, dst_ref=dst, send_sem=send_sems.at[a], recv_sem=recv_sems.at[a],
                    device_id=peer, device_id_type=pl.DeviceIdType.MESH).start()
        for a in range(n + 1):
            pltpu.make_async_copy(*ends(a, me), local_sems.at[a]).wait()
            seven = (vec_out if a == n else ins[a]).at[pl.ds(0, N_DEV - 1)]
            drain = pltpu.make_async_remote_copy(
                src_ref=seven, dst_ref=seven, send_sem=send_sems.at[a], recv_sem=recv_sems.at[a],
                device_id=(x, y, c), device_id_type=pl.DeviceIdType.MESH)
            drain.wait_send()
            drain.wait_recv()

    out = pl.pallas_call(
        body, name=name,
        in_specs=[_ANY] * (n + 1 + m), out_specs=[_ANY] * (m + 1),
        out_shape=[jax.ShapeDtypeStruct(z.shape, z.dtype) for z in lands]
        + [jax.ShapeDtypeStruct((N_DEV,) + vec.shape, vec.dtype)],
        input_output_aliases={n + 1 + i: i for i in range(m)},
        scratch_shapes=[pltpu.SemaphoreType.DMA((n + 1,))] * 3,
    )(*[e[0] for e in remote], vec, *lands)
    return out[:m], out[m]


def _rope_tables(pos):
    inv_freq = ROPE_THETA ** (-jnp.arange(0, ROPE, 2, dtype=F32) / ROPE)
    ang = pos.astype(F32)[:, None] * inv_freq
    cos, sin = jnp.cos(ang), jnp.sin(ang)
    pad = jnp.zeros((pos.shape[0], 128 - ROPE), F32)
    return jnp.concatenate([cos, cos, pad + 1.0], axis=1), jnp.concatenate([-sin, sin, pad], axis=1)


def _pad_last(w, n):
    return jnp.pad(w, [(0, 0)] * (w.ndim - 1) + [(0, n - w.shape[-1])])


def kernel(x, positions, g_mix, g_mlp, attn_w_down, attn_g_q_a, attn_g_kv_a, attn_w_uq, attn_w_ukv, attn_g_qnorm, attn_g_knorm, attn_w_o, conv_w_in, conv_w, conv_w_out, mlp_w1, mlp_w2, loss_target, m_g_mix, m_g_mlp, m_attn_w_down, m_attn_g_q_a, m_attn_g_kv_a, m_attn_w_uq, m_attn_w_ukv, m_attn_g_qnorm, m_attn_g_knorm, m_attn_w_o, m_conv_w_in, m_conv_w, m_conv_w_out, m_mlp_w1, m_mlp_w2, v_g_mix, v_g_mlp, v_attn_w_down, v_attn_g_q_a, v_attn_g_kv_a, v_attn_w_uq, v_attn_w_ukv, v_attn_g_qnorm, v_attn_g_knorm, v_attn_w_o, v_conv_w_in, v_conv_w, v_conv_w_out, v_mlp_w1, v_mlp_w2):
    weights = dict(g_mix=g_mix, g_mlp=g_mlp, attn_w_down=attn_w_down, attn_g_q_a=attn_g_q_a, attn_g_kv_a=attn_g_kv_a,
                   attn_w_uq=attn_w_uq, attn_w_ukv=attn_w_ukv, attn_g_qnorm=attn_g_qnorm, attn_g_knorm=attn_g_knorm,
                   attn_w_o=attn_w_o, conv_w_in=conv_w_in, conv_w=conv_w, conv_w_out=conv_w_out, mlp_w1=mlp_w1, mlp_w2=mlp_w2)
    mom1 = dict(g_mix=m_g_mix, g_mlp=m_g_mlp, attn_w_down=m_attn_w_down, attn_g_q_a=m_attn_g_q_a, attn_g_kv_a=m_attn_g_kv_a,
                attn_w_uq=m_attn_w_uq, attn_w_ukv=m_attn_w_ukv, attn_g_qnorm=m_attn_g_qnorm, attn_g_knorm=m_attn_g_knorm,
                attn_w_o=m_attn_w_o, conv_w_in=m_conv_w_in, conv_w=m_conv_w, conv_w_out=m_conv_w_out, mlp_w1=m_mlp_w1, mlp_w2=m_mlp_w2)
    mom2 = dict(g_mix=v_g_mix, g_mlp=v_g_mlp, attn_w_down=v_attn_w_down, attn_g_q_a=v_attn_g_q_a, attn_g_kv_a=v_attn_g_kv_a,
                attn_w_uq=v_attn_w_uq, attn_w_ukv=v_attn_w_ukv, attn_g_qnorm=v_attn_g_qnorm, attn_g_knorm=v_attn_g_knorm,
                attn_w_o=v_attn_w_o, conv_w_in=v_conv_w_in, conv_w=v_conv_w, conv_w_out=v_conv_w_out, mlp_w1=v_mlp_w1, mlp_w2=v_mlp_w2)
    big = ["attn_w_down", "attn_w_uq", "attn_w_ukv", "attn_w_o", "conv_w_in", "conv_w", "conv_w_out", "mlp_w1", "mlp_w2"]
    small = ["g_mix", "g_mlp", "attn_g_q_a", "attn_g_kv_a", "attn_g_qnorm", "attn_g_knorm"]
    order = ["g_mix", "g_mlp", "attn_w_down", "attn_g_q_a", "attn_g_kv_a", "attn_w_uq", "attn_w_ukv", "attn_g_qnorm",
             "attn_g_knorm", "attn_w_o", "conv_w_in", "conv_w", "conv_w_out", "mlp_w1", "mlp_w2"]

    xs = x[0]
    pos = positions[0]
    target = loss_target[0]
    S = xs.shape[0]
    depth = g_mix.shape[0]
    cos_t, sin_t = _rope_tables(pos)
    pos_col, pos_row = pos.reshape(S, 1), pos.reshape(1, S)

    keys, shards = [], []
    for name in big:
        for l in range(weights[name].shape[0]):
            keys.append((name, l))
            shards.append(weights[name][l] if name == "conv_w" else weights[name][l].astype(WIRE_DTYPE))
    first = [j for j, (name, l) in enumerate(keys) if l == 0 and name.startswith("attn")]
    second = [j for j, (name, l) in enumerate(keys) if l == 0 and name.startswith("mlp")]
    later = [j for j in range(len(keys)) if j not in first + second]
    me = 4 * lax.axis_index("x") + 2 * lax.axis_index("y") + lax.axis_index("c")

    def zones_with_own(js, token):
        return [lax.dynamic_update_slice(lax.empty((N_DEV,) + shards[j].shape, shards[j].dtype),
                                         (shards[j] + token[0, 0].astype(shards[j].dtype))[None],
                                         (me,) + (0,) * shards[j].ndim) for j in js]

    arrived, token = gather_now([shards[j] for j in first], name="gather_first")
    full = dict(zip([keys[j] for j in first], arrived))
    g1 = exchange_start([shards[j] for j in second], zones_with_own(second, token), None, name="gather_mlp0_start")
    g2 = exchange_start([shards[j] for j in later], zones_with_own(later, g1[4]), None, name="gather_rest_start")
    g_mix_0 = g_mix[0] + g2[4][0, 0]

    def rows(name, l):
        g = full[(name, l)]
        return g.reshape(g.shape[0] * g.shape[1], g.shape[2])

    saved = []
    for i in range(depth):
        l = i // 2
        rec = {"x0": xs}
        if i == 1:
            arrived = exchange_wait(*g2[:4], None, xs, name="gather_rest_wait")[1]
            full.update(zip([keys[j] for j in later], arrived))
        if i % 2 == 0:
            wd3 = _pad_last(rows("attn_w_down", l), DOWN_PAD)[None]
            wuq3 = _pad_last(full[("attn_w_uq", l)], QK_PAD)
            wukv3 = full[("attn_w_ukv", l)]
            gqn = _pad_last(attn_g_qnorm[l][None], QK_PAD)
            gkn = _pad_last(attn_g_knorm[l][None], QK_PAD)
            gqa, gkva = attn_g_q_a[l][None], attn_g_kv_a[l][None]
            h, a = norm_matmul(xs, g_mix_0 if i == 0 else g_mix[i], wd3, name="attn_down")
            q, k, v, cq, ckv = mla_pre_fwd(a, gqa, gkva, wuq3, wukv3, gqn, gkn, cos_t, sin_t)
            o, lse = attn_fwd(q, k, v, pos_col, pos_row)
            x1 = matmul_residual(o, rows("attn_w_o", l), xs, name="attn_out")
            rec.update(h=h, a=a, q=q, k=k, v=v, cq=cq, ckv=ckv, o=o, lse=lse, wd3=wd3, wuq3=wuq3, wukv3=wukv3,
                       gqn=gqn, gkn=gkn, gqa=gqa, gkva=gkva)
        else:
            cw = full[("conv_w", l)].transpose(1, 0, 2).reshape(3, D_MODEL)
            h, bcu = norm_matmul(xs, g_mix[i], full[("conv_w_in", l)], name="conv_in", out_dtype=MXU_DTYPE)
            z = conv_fwd(bcu, cw)
            x1 = matmul_residual(z, rows("conv_w_out", l), xs, name="conv_out")
            rec.update(h=h, bcu=bcu, z=z, cw=cw)
        if i == 0:
            arrived = exchange_wait(*g1[:4], None, x1, name="gather_mlp0_wait")[1]
            full.update(zip([keys[j] for j in second], arrived))
        h2, act, slope = norm_matmul(x1, g_mlp[i], full[("mlp_w1", i)], name="mlp_up", mlp=True)
        xs = matmul_residual(act, rows("mlp_w2", i), x1, name="mlp_down")
        rec.update(x1=x1, h2=h2, act=act, slope=slope)
        saved.append(rec)

    sq, dx, dxb = loss_head(xs, target)
    loss = lax.psum(sq[0, 0] * (0.5 / D_MODEL), MESH_AXES)

    grads = {name: [None] * weights[name].shape[0] for name in order}
    token = None
    for i in reversed(range(depth)):
        l = i // 2
        rec = saved[i]
        grads["mlp_w2"][i] = mm_tn(rec["act"], dxb, name="mlp_down_dw", G=1, out_dtype=WIRE_DTYPE).reshape(N_DEV, -1, D_MODEL)
        du = matmul_nt(dxb, rows("mlp_w2", i)[None], name="mlp_down_dx", epi="mlp_du", u=rec["slope"])
        grads["mlp_w1"][i] = mm_tn(rec["h2"], du, name="mlp_up_dw", G=N_DEV, out_dtype=WIRE_DTYPE, a_transposed=True)
        dx1, dx1b, dg = matmul_nt(du, full[("mlp_w1", i)], name="mlp_up_dx", epi="rms_bwd", x=rec["x1"], g=g_mlp[i], dx=dx)
        grads["g_mlp"][i] = dg[0]
        if i == 0:
            flying = [keys[j] for j in second + later]
            srcs = [grads[name][l_] for name, l_ in flying]
            slots = [(big.index(name), l_) for name, l_ in flying]
            zones = [lax.empty((N_DEV, weights[name].shape[0]) + grads[name][-1].shape[1:], grads[name][-1].dtype)
                     for name in big]
            for src, (k, l_) in zip(srcs, slots):
                own = lax.dynamic_index_in_dim(src, me, 0, keepdims=True)[None]
                zones[k] = lax.dynamic_update_slice(zones[k], own, (me, l_) + (0,) * (src.ndim - 1))
            s_send, s_recv, s_srcs, s_zones, token = exchange_start(srcs, zones, slots, name="scatter_rest_start")
        if i % 2 == 0:
            grads["attn_w_o"][l] = mm_tn(rec["o"], dx1b, name="attn_out_dw", G=1, out_dtype=WIRE_DTYPE,
                                         after=token).reshape(N_DEV, -1, D_MODEL)
            do = matmul_nt(dx1b, rows("attn_w_o", l)[None], name="attn_out_dx")
            dq, dk, dv = attn_bwd(rec["q"], rec["k"], rec["v"], do, rec["lse"], attn_delta(rec["o"], do), pos_col, pos_row)
            da, dwuq, dwukv, dgqn, dgkn, dgqa, dgkva = mla_pre_bwd(
                dq, dk, dv, rec["a"], rec["cq"], rec["ckv"], rec["gqa"], rec["gkva"], rec["wuq3"], rec["wukv3"],
                rec["gqn"], rec["gkn"], cos_t, sin_t)
            grads["attn_w_uq"][l] = dwuq[:, :, :QK_DIM].astype(WIRE_DTYPE)
            grads["attn_w_ukv"][l] = dwukv.astype(WIRE_DTYPE)
            grads["attn_g_qnorm"][l] = dgqn[0, :QK_DIM]
            grads["attn_g_knorm"][l] = dgkn[0, :QK_DIM]
            grads["attn_g_q_a"][l] = dgqa[0]
            grads["attn_g_kv_a"][l] = dgkva[0]
            dwd = mm_tn(rec["h"], da, name="attn_down_dw", G=1, out_dtype=WIRE_DTYPE, a_transposed=True)
            grads["attn_w_down"][l] = dwd[0, :, :DOWN].reshape(N_DEV, -1, DOWN)
            dx, dxb, dg = matmul_nt(da, rec["wd3"], name="attn_down_dx", epi="rms_bwd", x=rec["x0"], g=g_mix[i], dx=dx1)
        else:
            grads["conv_w_out"][l] = mm_tn(rec["z"], dx1b, name="conv_out_dw", G=1, out_dtype=WIRE_DTYPE).reshape(N_DEV, -1, D_MODEL)
            dz = matmul_nt(dx1b, rows("conv_w_out", l)[None], name="conv_out_dx")
            dbcu, dcw = conv_bwd(dz, rec["bcu"], rec["cw"])
            grads["conv_w"][l] = dcw.reshape(3, N_DEV, -1).transpose(1, 0, 2)
            grads["conv_w_in"][l] = mm_tn(rec["h"], dbcu, name="conv_in_dw", G=N_DEV, out_dtype=WIRE_DTYPE, a_transposed=True)
            dx, dxb, dg = matmul_nt(dbcu, full[("conv_w_in", l)], name="conv_in_dx", epi="rms_bwd", x=rec["x0"], g=g_mix[i], dx=dx1)
        grads["g_mix"][i] = dg[0]

    sizes = [weights[name].size for name in small]
    n_small = sum(sizes)
    rows_small = -(-n_small // (8 * 128)) * 8

    def pack(tree):
        flat = jnp.concatenate([jnp.stack(tree[name]).reshape(-1) if isinstance(tree[name], list) else tree[name].reshape(-1)
                                for name in small])
        return jnp.pad(flat, (0, rows_small * 128 - n_small)).reshape(rows_small, 128)

    s_srcs, s_zones = exchange_wait(s_send, s_recv, s_srcs, s_zones, slots, dx, name="scatter_rest_wait")
    remote = [(grads[name][l_], big.index(name), l_) for name, l_ in (keys[j] for j in first)]
    parts, gain_parts = scatter_finish(remote, s_zones, pack(grads), name="scatter_last")

    out = {}
    for name, part in zip(big, parts):
        w = weights[name]
        flat = lambda t: t.reshape(-1, t.shape[-1])
        res = adamw(part.reshape(N_DEV, -1, w.shape[-1]), flat(w), flat(mom1[name]), flat(mom2[name]), name="adamw_" + name)
        out[name] = [r.reshape(w.shape) for r in res]
    res = adamw(gain_parts, pack(weights), pack(mom1), pack(mom2), name="adamw_gains")
    offset = 0
    for name, size in zip(small, sizes):
        out[name] = [r.reshape(-1)[offset:offset + size].reshape(weights[name].shape) for r in res]
        offset += size

    return (loss, dx[None], *[out[n][0] for n in order], *[out[n][1] for n in order],
            *[out[n][2] for n in order], *[out[n][3] for n in order])
```

```python
import jax
import jax.numpy as jnp
import numpy as np
from jax import lax
from jax.experimental import pallas as pl
from jax.experimental.pallas import tpu as pltpu

F32 = jnp.float32
MXU_DTYPE = jnp.bfloat16
WIRE_DTYPE = jnp.bfloat16

D_MODEL = 1024
N_HEADS = 8
NOPE = 128
ROPE = 64
QK_DIM = NOPE + ROPE
QK_PAD = 256
V_DIM = 128
Q_LORA = 256
KV_LORA = 128
DOWN = Q_LORA + KV_LORA + ROPE
DOWN_PAD = 512
ROPE_THETA = 10000.0
EPS = 1e-6
SM_SCALE = QK_DIM ** -0.5
LOG2E = 1.4426950408889634
Q_PRESCALE = SM_SCALE * LOG2E
ADAM_LR, ADAM_B1, ADAM_B2, ADAM_EPS, ADAM_WD, ADAM_STEP = 0.001, 0.9, 0.999, 1e-08, 0.01, 10
N_DEV = 8
MESH_AXES = ("x", "y", "c")

TM = 512
TM_WIDE = 1024
TILE_BUDGET = 32 << 20
TM_TOKENS_TN = 2048
TQ = 512
HEADS_FWD = 8
TQ_BWD = 2048
BWD_CHUNK = 256
TROW = 256
HALO = 16
VMEM_LIMIT = 48 << 20

NN = (((1,), (0,)), ((), ()))
NT = (((1,), (1,)), ((), ()))
TN = (((0,), (0,)), ((), ()))


def _dot(a, b, dims=NN):
    return lax.dot_general(a.astype(MXU_DTYPE), b.astype(MXU_DTYPE), dims, preferred_element_type=F32)


def _params(n_axes):
    return pltpu.CompilerParams(dimension_semantics=("arbitrary",) * n_axes, vmem_limit_bytes=VMEM_LIMIT)


def _rms(xv, n):
    r = lax.rsqrt(jnp.sum(xv * xv, axis=-1, keepdims=True) / n + EPS)
    return xv * r, r


def _rms_bwd(dy, xhat, r, g, n):
    dg = jnp.sum(dy * xhat, axis=0, keepdims=True)
    dxh = dy * g
    dx = r * (dxh - xhat * (jnp.sum(dxh * xhat, axis=-1, keepdims=True) / n))
    return dx, dg


def _swap_halves(t):
    lane = lax.broadcasted_iota(jnp.int32, t.shape, 1)
    return jnp.where(lane < ROPE // 2, pltpu.roll(t, 128 - ROPE // 2, 1), pltpu.roll(t, ROPE // 2, 1))


def _rope(t, cos_t, sin_t):
    return t * cos_t + _swap_halves(t) * sin_t


def _rope_bwd(dout, cos_t, sin_t):
    return dout * cos_t + _swap_halves(dout * sin_t)


def _resident(shape):
    return pl.BlockSpec(shape, lambda i: (0,) * len(shape))


def _token_tile(S, row_bytes, resident_bytes):
    wide = min(TM_WIDE, S)
    return wide if 2 * (wide * row_bytes + resident_bytes) <= TILE_BUDGET else min(TM, S)


def norm_matmul(x, g, w3, *, name, mlp=False, out_dtype=F32):
    if mlp:
        out_dtype = MXU_DTYPE
    S, D = x.shape
    G, _, Nb = w3.shape
    N = G * Nb
    n_out = 2 if mlp else 1
    tm = _token_tile(S, D * 4 + D * 2 + n_out * N * jnp.dtype(out_dtype).itemsize, w3.size * w3.dtype.itemsize)

    def body(x_ref, g_ref, w_ref, h_ref, o_ref, *slope_ref):
        xv = x_ref[...]
        r = lax.rsqrt(jnp.mean(xv * xv, axis=-1, keepdims=True) + EPS)
        h = (xv * r * g_ref[...]).astype(h_ref.dtype)
        h_ref[...] = h.T
        for gi in range(G):
            cols = slice(gi * Nb, (gi + 1) * Nb)
            acc = _dot(h, w_ref[gi])
            if mlp:
                acc = jnp.maximum(acc, 0.0)
                slope_ref[0][:, cols] = (2.0 * acc).astype(out_dtype)
                acc = jnp.square(acc)
            o_ref[:, cols] = acc.astype(o_ref.dtype)

    rows = lambda w: pl.BlockSpec((tm, w), lambda i: (i, 0))
    return pl.pallas_call(
        body, name=name, grid=(S // tm,),
        in_specs=[rows(D), _resident((1, D)), _resident((G, D, Nb))],
        out_specs=[pl.BlockSpec((D, tm), lambda i: (0, i))] + [rows(N)] * n_out,
        out_shape=[jax.ShapeDtypeStruct((D, S), MXU_DTYPE)] + [jax.ShapeDtypeStruct((S, N), out_dtype)] * n_out,
        compiler_params=_params(1),
    )(x, g.reshape(1, D), w3)


def matmul_residual(a, w, res, *, name):
    S, K = a.shape
    _, N = w.shape
    tm = _token_tile(S, K * a.dtype.itemsize + 2 * N * 4, w.size * w.dtype.itemsize)

    def body(a_ref, w_ref, r_ref, o_ref):
        o_ref[...] = r_ref[...] + _dot(a_ref[...], w_ref[...])

    rows = lambda w_: pl.BlockSpec((tm, w_), lambda i: (i, 0))
    return pl.pallas_call(
        body, name=name, grid=(S // tm,),
        in_specs=[rows(K), _resident((K, N)), rows(N)],
        out_specs=rows(N), out_shape=jax.ShapeDtypeStruct((S, N), F32),
        compiler_params=_params(1),
    )(a, w, res)


def matmul_nt(a, w3, *, name, epi="plain", u=None, x=None, g=None, dx=None):
    S, N = a.shape
    G, Ko, Nb = w3.shape
    assert N == G * Nb
    row_bytes = N * a.dtype.itemsize + Ko * {"plain": 4, "mlp_du": 4, "rms_bwd": 14}[epi]
    tm = _token_tile(S, row_bytes, w3.size * w3.dtype.itemsize)
    tko = min(Ko, 512)

    def body(a_ref, w_ref, *rest):
        if epi == "mlp_du":
            u_ref, o_ref = rest
            av = a_ref[...].astype(MXU_DTYPE)
            for j in range(Ko // tko):
                cols = slice(j * tko, (j + 1) * tko)
                da = _dot(av, w_ref[0, cols, :], NT)
                o_ref[:, cols] = (da * u_ref[:, cols].astype(F32)).astype(o_ref.dtype)
            return
        acc = _dot(a_ref[:, :Nb], w_ref[0], NT)
        for gi in range(1, G):
            acc = acc + _dot(a_ref[:, gi * Nb:(gi + 1) * Nb], w_ref[gi], NT)
        if epi == "plain":
            rest[0][...] = acc
        else:
            x_ref, g_ref, dx_ref, o_ref, ob_ref, dg_ref = rest
            xhat, r = _rms(x_ref[...], Ko)
            dxb, dg = _rms_bwd(acc, xhat, r, g_ref[...], Ko)
            dx_new = dx_ref[...] + dxb
            o_ref[...] = dx_new
            ob_ref[...] = dx_new.astype(ob_ref.dtype)

            @pl.when(pl.program_id(0) == 0)
            def _():
                dg_ref[...] = dg

            @pl.when(pl.program_id(0) != 0)
            def _():
                dg_ref[...] += dg

    rows = lambda w_: pl.BlockSpec((tm, w_), lambda i: (i, 0))
    in_specs = [rows(N), _resident((G, Ko, Nb))]
    args = [a, w3]
    if epi == "plain":
        out_shape, out_specs = jax.ShapeDtypeStruct((S, Ko), F32), rows(Ko)
    elif epi == "mlp_du":
        in_specs.append(rows(Ko))
        args.append(u)
        out_shape, out_specs = jax.ShapeDtypeStruct((S, Ko), MXU_DTYPE), rows(Ko)
    else:
        in_specs += [rows(Ko), _resident((1, Ko)), rows(Ko)]
        args += [x, g.reshape(1, Ko), dx]
        out_shape = [jax.ShapeDtypeStruct((S, Ko), F32), jax.ShapeDtypeStruct((S, Ko), MXU_DTYPE),
                     jax.ShapeDtypeStruct((1, Ko), F32)]
        out_specs = [rows(Ko), rows(Ko), _resident((1, Ko))]
    return pl.pallas_call(
        body, name=name, grid=(S // tm,),
        in_specs=in_specs, out_specs=out_specs, out_shape=out_shape,
        compiler_params=_params(1),
    )(*args)


def mm_tn(a, b, *, name, G, out_dtype, after=None, a_transposed=False):
    order = [] if after is None else [after]
    Ka, S = a.shape if a_transposed else a.shape[::-1]
    _, N = b.shape
    Nb = N // G
    assert Nb <= 1024
    tm = min(TM_TOKENS_TN if b.dtype.itemsize == 2 else TM_TOKENS_TN // 2, S)
    tka = min(Ka, 1024)
    gb = 2 if G % 2 == 0 and Nb <= 512 else 1
    ns = S // tm

    def body(a_ref, b_ref, *rest):
        o_ref, acc = rest[-2:]
        s = pl.program_id(2)

        @pl.when(s == 0)
        def _():
            acc[...] = jnp.zeros_like(acc)

        av = a_ref[...]
        for gi in range(gb):
            acc[gi] += _dot(av, b_ref[:, gi * Nb:(gi + 1) * Nb], NN if a_transposed else TN)

        @pl.when(s == ns - 1)
        def _():
            o_ref[...] = acc[...].astype(o_ref.dtype)

    a_spec = pl.BlockSpec((tka, tm), lambda i, j, s: (i, s)) if a_transposed else pl.BlockSpec((tm, tka), lambda i, j, s: (s, i))
    return pl.pallas_call(
        body, name=name, grid=(Ka // tka, G // gb, ns),
        in_specs=[a_spec,
                  pl.BlockSpec((tm, gb * Nb), lambda i, j, s: (s, j))] + [pl.BlockSpec(memory_space=pl.ANY)] * len(order),
        out_specs=pl.BlockSpec((gb, tka, Nb), lambda i, j, s: (j, i, 0)),
        out_shape=jax.ShapeDtypeStruct((G, Ka, Nb), out_dtype),
        scratch_shapes=[pltpu.VMEM((gb, tka, Nb), F32)],
        compiler_params=_params(3),
    )(a, b, *order)


def mla_pre_fwd(a, gqa, gkva, wuq3, wukv3, gqn, gkn, cos_t, sin_t):
    S = a.shape[0]
    tm = min(TM, S)
    H = N_HEADS

    def body(a_ref, gqa_ref, gkva_ref, wuq_ref, wukv_ref, gqn_ref, gkn_ref, cos_ref, sin_ref,
             q_ref, k_ref, v_ref, cq_ref, ckv_ref):
        av = a_ref[...]
        cq = (_rms(av[:, :Q_LORA], Q_LORA)[0] * gqa_ref[...]).astype(cq_ref.dtype)
        ckv = (_rms(av[:, Q_LORA:Q_LORA + KV_LORA], KV_LORA)[0] * gkva_ref[...]).astype(ckv_ref.dtype)
        cq_ref[...] = cq
        ckv_ref[...] = ckv
        kpe = av[:, Q_LORA + KV_LORA:]
        cos_v, sin_v = cos_ref[...], sin_ref[...]
        for h in range(H):
            qn = _rms(_dot(cq, wuq_ref[h]), QK_DIM)[0] * gqn_ref[...]
            qr = jnp.concatenate([qn[:, :NOPE], _rope(qn[:, NOPE:], cos_v, sin_v)], axis=1)
            q_ref[h] = (qr * Q_PRESCALE).astype(q_ref.dtype)
            kvp = _dot(ckv, wukv_ref[h])
            kn = _rms(jnp.concatenate([kvp[:, :NOPE], kpe], axis=1), QK_DIM)[0] * gkn_ref[...]
            k_ref[h] = jnp.concatenate([kn[:, :NOPE], _rope(kn[:, NOPE:], cos_v, sin_v)], axis=1).astype(k_ref.dtype)
            v_ref[h] = kvp[:, NOPE:].astype(v_ref.dtype)

    row = lambda w: pl.BlockSpec((tm, w), lambda i: (i, 0))
    heads = lambda w: pl.BlockSpec((H, tm, w), lambda i: (0, i, 0))
    return pl.pallas_call(
        body, name="mla_pre_fwd", grid=(S // tm,),
        in_specs=[row(DOWN_PAD), _resident((1, Q_LORA)), _resident((1, KV_LORA)),
                  _resident((H, Q_LORA, QK_PAD)), _resident((H, KV_LORA, NOPE + V_DIM)),
                  _resident((1, QK_PAD)), _resident((1, QK_PAD)), row(128), row(128)],
        out_specs=[heads(QK_PAD), heads(QK_PAD), heads(V_DIM), row(Q_LORA), row(KV_LORA)],
        out_shape=[jax.ShapeDtypeStruct((H, S, QK_PAD), MXU_DTYPE),
                   jax.ShapeDtypeStruct((H, S, QK_PAD), MXU_DTYPE),
                   jax.ShapeDtypeStruct((H, S, V_DIM), MXU_DTYPE),
                   jax.ShapeDtypeStruct((S, Q_LORA), MXU_DTYPE),
                   jax.ShapeDtypeStruct((S, KV_LORA), MXU_DTYPE)],
        compiler_params=_params(1),
    )(a, gqa, gkva, wuq3, wukv3, gqn, gkn, cos_t, sin_t)


def mla_pre_bwd(dq, dk, dv, a, cq, ckv, gqa, gkva, wuq3, wukv3, gqn, gkn, cos_t, sin_t):
    S = a.shape[0]
    tm = min(TM, S)
    H = N_HEADS

    def body(dq_ref, dk_ref, dv_ref, a_ref, cq_ref, ckv_ref, gqa_ref, gkva_ref, wuq_ref, wukv_ref, gqn_ref, gkn_ref,
             cos_ref, sin_ref, da_ref, dwuq_ref, dwukv_ref, dgqn_ref, dgkn_ref, dgqa_ref, dgkva_ref):
        @pl.when(pl.program_id(0) == 0)
        def _():
            for ref in (dwuq_ref, dwukv_ref, dgqn_ref, dgkn_ref, dgqa_ref, dgkva_ref):
                ref[...] = jnp.zeros_like(ref)

        av = a_ref[...]
        kpe = av[:, Q_LORA + KV_LORA:]
        cos_v, sin_v = cos_ref[...], sin_ref[...]
        cqv, ckvv = cq_ref[...], ckv_ref[...]
        dcq = jnp.zeros((tm, Q_LORA), F32)
        dckv = jnp.zeros((tm, KV_LORA), F32)
        dkpe = jnp.zeros((tm, 128), F32)
        dgqn = jnp.zeros((1, QK_PAD), F32)
        dgkn = jnp.zeros((1, QK_PAD), F32)
        up = lambda h: (_dot(cqv, wuq_ref[h]), _dot(ckvv, wukv_ref[h]))
        nxt = up(0)
        for h in range(H):
            wuq, wukv = wuq_ref[h], wukv_ref[h]
            qp, kvp = nxt
            if h + 1 < H:
                nxt = up(h + 1)
            qhat, rq = _rms(qp, QK_DIM)
            dqr = dq_ref[h] * SM_SCALE
            dqn = jnp.concatenate([dqr[:, :NOPE], _rope_bwd(dqr[:, NOPE:], cos_v, sin_v)], axis=1)
            dqp, dg = _rms_bwd(dqn, qhat, rq, gqn_ref[...], QK_DIM)
            dgqn = dgqn + dg
            dqp = dqp.astype(MXU_DTYPE)
            dwuq_ref[h] += _dot(cqv, dqp, TN)
            dcq = dcq + _dot(dqp, wuq, NT)
            khat, rk = _rms(jnp.concatenate([kvp[:, :NOPE], kpe], axis=1), QK_DIM)
            dkr = dk_ref[h] * (1.0 / LOG2E)
            dkn = jnp.concatenate([dkr[:, :NOPE], _rope_bwd(dkr[:, NOPE:], cos_v, sin_v)], axis=1)
            dkk, dg = _rms_bwd(dkn, khat, rk, gkn_ref[...], QK_DIM)
            dgkn = dgkn + dg
            dkpe = dkpe + dkk[:, NOPE:]
            dkvp = jnp.concatenate([dkk[:, :NOPE], dv_ref[h]], axis=1).astype(MXU_DTYPE)
            dwukv_ref[h] += _dot(ckvv, dkvp, TN)
            dckv = dckv + _dot(dkvp, wukv, NT)
        dgqn_ref[...] += dgqn
        dgkn_ref[...] += dgkn
        ahat, r = _rms(av[:, :Q_LORA], Q_LORA)
        daq, dg = _rms_bwd(dcq, ahat, r, gqa_ref[...], Q_LORA)
        dgqa_ref[...] += dg
        ahat, r = _rms(av[:, Q_LORA:Q_LORA + KV_LORA], KV_LORA)
        dakv, dg = _rms_bwd(dckv, ahat, r, gkva_ref[...], KV_LORA)
        dgkva_ref[...] += dg
        da_ref[...] = jnp.concatenate([daq, dakv, dkpe], axis=1)

    row = lambda w: pl.BlockSpec((tm, w), lambda i: (i, 0))
    heads = lambda w: pl.BlockSpec((H, tm, w), lambda i: (0, i, 0))
    return pl.pallas_call(
        body, name="mla_pre_bwd", grid=(S // tm,),
        in_specs=[heads(QK_PAD), heads(QK_PAD), heads(V_DIM), row(DOWN_PAD), row(Q_LORA), row(KV_LORA),
                  _resident((1, Q_LORA)), _resident((1, KV_LORA)),
                  _resident((H, Q_LORA, QK_PAD)), _resident((H, KV_LORA, NOPE + V_DIM)),
                  _resident((1, QK_PAD)), _resident((1, QK_PAD)), row(128), row(128)],
        out_specs=[row(DOWN_PAD), _resident((H, Q_LORA, QK_PAD)), _resident((H, KV_LORA, NOPE + V_DIM)),
                   _resident((1, QK_PAD)), _resident((1, QK_PAD)), _resident((1, Q_LORA)), _resident((1, KV_LORA))],
        out_shape=[jax.ShapeDtypeStruct((S, DOWN_PAD), F32),
                   jax.ShapeDtypeStruct((H, Q_LORA, QK_PAD), F32),
                   jax.ShapeDtypeStruct((H, KV_LORA, NOPE + V_DIM), F32),
                   jax.ShapeDtypeStruct((1, QK_PAD), F32), jax.ShapeDtypeStruct((1, QK_PAD), F32),
                   jax.ShapeDtypeStruct((1, Q_LORA), F32), jax.ShapeDtypeStruct((1, KV_LORA), F32)],
        compiler_params=_params(1),
    )(dq, dk, dv, a, cq, ckv, gqa, gkva, wuq3, wukv3, gqn, gkn, cos_t, sin_t)


def _pair_tables(nb, key_major):
    if key_major:
        pairs = [(qi, kj) for kj in range(nb) for qi in range(kj, nb)]
    else:
        pairs = [(qi, ki) for qi in range(nb) for ki in range(qi + 1)]
    return (jnp.asarray(np.array([p[0] for p in pairs], np.int32)),
            jnp.asarray(np.array([p[1] for p in pairs], np.int32)))


def _scores_t(k, q, pk_col, pq_row, masked):
    s = _dot(k, q, NT)
    return jnp.where(pq_row >= pk_col, s, jnp.finfo(F32).min) if masked else s


def attn_fwd(q, k, v, pos_col, pos_row):
    H, S, _ = q.shape
    t = min(TQ, S)
    nb = S // t
    hb = HEADS_FWD
    qt, kt = _pair_tables(nb, key_major=False)

    def body(qt_ref, kt_ref, q_ref, k_ref, v_ref, pk_ref, pq_ref, o_ref, lse_ref, m_s, l_s, acc):
        step = pl.program_id(1)
        qi, ki = qt_ref[step], kt_ref[step]

        @pl.when(ki == 0)
        def _():
            m_s[...] = jnp.full_like(m_s, -jnp.inf)
            l_s[...] = jnp.zeros_like(l_s)
            acc[...] = jnp.zeros_like(acc)

        def update(masked):
            scores = lambda hh: _scores_t(k_ref[hh], q_ref[hh], pk_ref[...], pq_ref[...], masked)
            def weighted_values(hh, p, alpha):
                acc[hh] = alpha * acc[hh] + _dot(v_ref[hh], p, TN)

            s_next = scores(0)
            pending = None
            for hh in range(hb):
                s = s_next
                if hh + 1 < hb:
                    s_next = scores(hh + 1)
                m_old = m_s[hh]
                m_new = jnp.maximum(m_old, jnp.max(s, axis=0, keepdims=True))
                p = jnp.exp2(s - m_new)
                alpha = jnp.exp2(m_old - m_new)
                l_s[hh] = alpha * l_s[hh] + jnp.sum(p, axis=0, keepdims=True)
                m_s[hh] = m_new
                if pending is not None:
                    weighted_values(*pending)
                pending = (hh, p, alpha)
            weighted_values(*pending)

        @pl.when(ki < qi)
        def _():
            update(False)

        @pl.when(ki == qi)
        def _():
            update(True)
            for hh in range(hb):
                o_ref[:, hh * V_DIM:(hh + 1) * V_DIM] = (acc[hh] / l_s[hh]).T
                lse_ref[hh] = m_s[hh] + jnp.log(l_s[hh]) * LOG2E

    grid_spec = pltpu.PrefetchScalarGridSpec(
        num_scalar_prefetch=2, grid=(H // hb, qt.shape[0]),
        in_specs=[pl.BlockSpec((hb, t, QK_PAD), lambda h, s, qt, kt: (h, qt[s], 0)),
                  pl.BlockSpec((hb, t, QK_PAD), lambda h, s, qt, kt: (h, kt[s], 0)),
                  pl.BlockSpec((hb, t, V_DIM), lambda h, s, qt, kt: (h, kt[s], 0)),
                  pl.BlockSpec((t, 1), lambda h, s, qt, kt: (kt[s], 0)),
                  pl.BlockSpec((1, t), lambda h, s, qt, kt: (0, qt[s]))],
        out_specs=[pl.BlockSpec((t, hb * V_DIM), lambda h, s, qt, kt: (qt[s], h)),
                   pl.BlockSpec((hb, 1, t), lambda h, s, qt, kt: (h, 0, qt[s]))],
        scratch_shapes=[pltpu.VMEM((hb, 1, t), F32), pltpu.VMEM((hb, 1, t), F32), pltpu.VMEM((hb, V_DIM, t), F32)])
    return pl.pallas_call(
        body, name="attn_fwd", grid_spec=grid_spec,
        out_shape=[jax.ShapeDtypeStruct((S, H * V_DIM), F32), jax.ShapeDtypeStruct((H, 1, S), F32)],
        compiler_params=_params(2),
    )(qt, kt, q, k, v, pos_col, pos_row)


def attn_delta(o, do):
    S = o.shape[0]
    t = min(TQ, S)

    def body(o_ref, do_ref, d_ref):
        for h in range(N_HEADS):
            cols = slice(h * V_DIM, (h + 1) * V_DIM)
            d_ref[h] = jnp.sum((o_ref[:, cols] * do_ref[:, cols]).T, axis=0, keepdims=True)

    blk = pl.BlockSpec((t, N_HEADS * V_DIM), lambda i: (i, 0))
    return pl.pallas_call(
        body, name="attn_delta", grid=(S // t,),
        in_specs=[blk, blk],
        out_specs=pl.BlockSpec((N_HEADS, 1, t), lambda i: (0, 0, i)),
        out_shape=jax.ShapeDtypeStruct((N_HEADS, 1, S), F32),
        compiler_params=_params(1),
    )(o, do)


def attn_bwd(q, k, v, do, lse, delta, pos_col, pos_row):
    H, S, _ = q.shape
    t = min(TQ_BWD, S)
    nb = S // t
    qt, kt = _pair_tables(nb, key_major=True)
    tc = min(BWD_CHUNK, t)

    def body(qt_ref, kt_ref, q_ref, k_ref, v_ref, do_ref, lse_ref, dl_ref, pk_ref, pq_ref, dq_ref, dk_ref, dv_ref):
        step = pl.program_id(1)
        qi, kj = qt_ref[step], kt_ref[step]

        @pl.when(step == 0)
        def _():
            dq_ref[...] = jnp.zeros_like(dq_ref)

        @pl.when(qi == kj)
        def _():
            dk_ref[...] = jnp.zeros_like(dk_ref)
            dv_ref[...] = jnp.zeros_like(dv_ref)

        def update(masked):
            seen = lambda c: (c + 1) * tc if masked else t

            def first_matmuls(c):
                cols, ke = slice(c * tc, (c + 1) * tc), seen(c)
                qc = q_ref[cols, :]
                doc = do_ref[cols, :].astype(MXU_DTYPE)
                s = _scores_t(k_ref[:ke, :], qc, pk_ref[:ke, :], pq_ref[:, cols], masked)
                return qc, doc, s, _dot(v_ref[:ke, :], doc, NT)

            nxt = first_matmuls(0)
            for c in range(t // tc):
                qc, doc, s, dp = nxt
                if c + 1 < t // tc:
                    nxt = first_matmuls(c + 1)
                cols, ke = slice(c * tc, (c + 1) * tc), seen(c)
                p = jnp.exp2(s - lse_ref[:, cols])
                ds = (p * (dp - dl_ref[:, cols])).astype(MXU_DTYPE)
                dv_ref[:ke, :] += _dot(p, doc)
                dk_ref[:ke, :] += _dot(ds, qc)
                rows = pl.ds(pl.multiple_of(qi * t + c * tc, tc), tc)
                dq_ref[rows, :] += _dot(ds, k_ref[:ke, :], TN)

        @pl.when(qi == kj)
        def _():
            update(True)

        @pl.when(qi != kj)
        def _():
            update(False)

    q_idx = lambda h, s, qt, kt: (h, qt[s], 0)
    k_idx = lambda h, s, qt, kt: (h, kt[s], 0)
    row_idx = lambda h, s, qt, kt: (h, 0, qt[s])
    grid_spec = pltpu.PrefetchScalarGridSpec(
        num_scalar_prefetch=2, grid=(H, qt.shape[0]),
        in_specs=[pl.BlockSpec((None, t, QK_PAD), q_idx),
                  pl.BlockSpec((None, t, QK_PAD), k_idx),
                  pl.BlockSpec((None, t, V_DIM), k_idx),
                  pl.BlockSpec((t, V_DIM), lambda h, s, qt, kt: (qt[s], h)),
                  pl.BlockSpec((None, 1, t), row_idx),
                  pl.BlockSpec((None, 1, t), row_idx),
                  pl.BlockSpec((t, 1), lambda h, s, qt, kt: (kt[s], 0)),
                  pl.BlockSpec((1, t), lambda h, s, qt, kt: (0, qt[s]))],
        out_specs=[pl.BlockSpec((None, S, QK_PAD), lambda h, s, qt, kt: (h, 0, 0)),
                   pl.BlockSpec((None, t, QK_PAD), k_idx),
                   pl.BlockSpec((None, t, V_DIM), k_idx)])
    return pl.pallas_call(
        body, name="attn_bwd", grid_spec=grid_spec,
        out_shape=[jax.ShapeDtypeStruct((H, S, QK_PAD), F32), jax.ShapeDtypeStruct((H, S, QK_PAD), F32),
                   jax.ShapeDtypeStruct((H, S, V_DIM), F32)],
        compiler_params=_params(2),
    )(qt, kt, q, k, v, do, lse, delta, pos_col, pos_row)


def _conv_specs(S, tr):
    hb = tr // HALO
    main = lambda third: pl.BlockSpec((tr, D_MODEL), lambda r: (r, third))
    prev = lambda third: pl.BlockSpec((HALO, D_MODEL), lambda r: (jnp.maximum(r * hb - 1, 0), third))
    nxt = lambda third: pl.BlockSpec((HALO, D_MODEL), lambda r: (jnp.minimum((r + 1) * hb, S // HALO - 1), third))
    return main, prev, nxt


def _f32(ref):
    return ref[...].astype(F32)


def _conv_taps(gc, uu, w_ref, first):
    u2 = gc * uu
    rows = lax.broadcasted_iota(jnp.int32, u2.shape, 0)
    u2 = jnp.where((rows < HALO) & first, 0.0, u2)
    s1 = pltpu.roll(u2, 1, 0)
    s2 = pltpu.roll(u2, 2, 0)
    u3 = w_ref[2:3, :] * u2 + w_ref[1:2, :] * s1 + w_ref[0:1, :] * s2
    return u2, s1, s2, u3


def conv_fwd(bcu, cw):
    S = bcu.shape[0]
    tr = min(TROW, S)
    main, prev, _ = _conv_specs(S, tr)

    def body(gb_ref, gc_ref, u_ref, gch_ref, uh_ref, w_ref, z_ref):
        gc = jnp.concatenate([_f32(gch_ref), _f32(gc_ref)], axis=0)
        uu = jnp.concatenate([_f32(uh_ref), _f32(u_ref)], axis=0)
        u3 = _conv_taps(gc, uu, w_ref, pl.program_id(0) == 0)[3]
        z_ref[...] = (_f32(gb_ref) * u3[HALO:]).astype(z_ref.dtype)

    return pl.pallas_call(
        body, name="conv_fwd", grid=(S // tr,),
        in_specs=[main(0), main(1), main(2), prev(1), prev(2), _resident((3, D_MODEL))],
        out_specs=main(0),
        out_shape=jax.ShapeDtypeStruct((S, D_MODEL), MXU_DTYPE),
        compiler_params=_params(1),
    )(bcu, bcu, bcu, bcu, bcu, cw)


def conv_bwd(dz, bcu, cw):
    S = bcu.shape[0]
    tr = min(TROW, S)
    nr = S // tr
    main, prev, nxt = _conv_specs(S, tr)

    def body(dz_ref, dzn_ref, gb_ref, gbn_ref, gc_ref, u_ref, gch_ref, uh_ref, w_ref, o_ref, dw_ref):
        r = pl.program_id(0)
        gcv, uv = _f32(gc_ref), _f32(u_ref)
        gc = jnp.concatenate([_f32(gch_ref), gcv], axis=0)
        uu = jnp.concatenate([_f32(uh_ref), uv], axis=0)
        u2, s1, s2, u3 = _conv_taps(gc, uu, w_ref, r == 0)
        dzv = dz_ref[...]
        du3 = jnp.concatenate([dzv * _f32(gb_ref), dzn_ref[...] * _f32(gbn_ref)], axis=0)
        rows = lax.broadcasted_iota(jnp.int32, du3.shape, 0)
        du3 = jnp.where((rows >= tr) & (r == nr - 1), 0.0, du3)
        n1 = pltpu.roll(du3, tr + HALO - 1, 0)
        n2 = pltpu.roll(du3, tr + HALO - 2, 0)
        du2 = (w_ref[2:3, :] * du3 + w_ref[1:2, :] * n1 + w_ref[0:1, :] * n2)[:tr]
        o_ref[:, :D_MODEL] = (dzv * u3[HALO:]).astype(o_ref.dtype)
        o_ref[:, D_MODEL:2 * D_MODEL] = (du2 * uv).astype(o_ref.dtype)
        o_ref[:, 2 * D_MODEL:] = (du2 * gcv).astype(o_ref.dtype)
        d3 = du3[:tr]
        taps = [jnp.sum(d3 * t[HALO:], axis=0, keepdims=True) for t in (s2, s1, u2)]

        @pl.when(r == 0)
        def _():
            for kk in range(3):
                dw_ref[kk:kk + 1, :] = taps[kk]

        @pl.when(r != 0)
        def _():
            for kk in range(3):
                dw_ref[kk:kk + 1, :] += taps[kk]

    return pl.pallas_call(
        body, name="conv_bwd", grid=(nr,),
        in_specs=[main(0), nxt(0), main(0), nxt(0), main(1), main(2), prev(1), prev(2), _resident((3, D_MODEL))],
        out_specs=[pl.BlockSpec((tr, 3 * D_MODEL), lambda r: (r, 0)), _resident((3, D_MODEL))],
        out_shape=[jax.ShapeDtypeStruct((S, 3 * D_MODEL), MXU_DTYPE), jax.ShapeDtypeStruct((3, D_MODEL), F32)],
        compiler_params=_params(1),
    )(dz, dz, bcu, bcu, bcu, bcu, bcu, bcu, cw)


def loss_head(y, target):
    S, D = y.shape
    tm = min(TM, S)

    def body(y_ref, t_ref, l_ref, dy_ref, dyb_ref):
        err = y_ref[...] - t_ref[...]
        dy = err / D
        dy_ref[...] = dy
        dyb_ref[...] = dy.astype(dyb_ref.dtype)
        part = jnp.full((1, 128), jnp.sum(err * err), F32)

        @pl.when(pl.program_id(0) == 0)
        def _():
            l_ref[...] = part

        @pl.when(pl.program_id(0) != 0)
        def _():
            l_ref[...] += part

    blk = pl.BlockSpec((tm, D), lambda i: (i, 0))
    return pl.pallas_call(
        body, name="loss_head", grid=(S // tm,),
        in_specs=[blk, blk],
        out_specs=[pl.BlockSpec((1, 128), lambda i: (0, 0)), blk, blk],
        out_shape=[jax.ShapeDtypeStruct((1, 128), F32), jax.ShapeDtypeStruct((S, D), F32),
                   jax.ShapeDtypeStruct((S, D), MXU_DTYPE)],
        compiler_params=_params(1),
    )(y, target)


def adamw(parts, w, m, v, *, name):
    R, C = w.shape
    tr = R
    while tr * C * 4 > (1 << 20) and tr % 32 == 0:
        tr //= 2

    def body(p_ref, w_ref, m_ref, v_ref, g_ref, d_ref, mo_ref, vo_ref):
        g = p_ref[0].astype(F32)
        for d in range(1, N_DEV):
            g = g + p_ref[d].astype(F32)
        m_new = ADAM_B1 * m_ref[...] + (1.0 - ADAM_B1) * g
        v_new = ADAM_B2 * v_ref[...] + (1.0 - ADAM_B2) * (g * g)
        m_hat = m_new / (1.0 - ADAM_B1 ** ADAM_STEP)
        v_hat = v_new / (1.0 - ADAM_B2 ** ADAM_STEP)
        g_ref[...] = g
        d_ref[...] = -ADAM_LR * (m_hat / (jnp.sqrt(v_hat) + ADAM_EPS) + ADAM_WD * w_ref[...])
        mo_ref[...] = m_new
        vo_ref[...] = v_new

    blk = pl.BlockSpec((tr, C), lambda i: (i, 0))
    return pl.pallas_call(
        body, name=name, grid=(R // tr,),
        in_specs=[pl.BlockSpec((N_DEV, tr, C), lambda i: (0, i, 0)), blk, blk, blk],
        out_specs=[blk, blk, blk, blk],
        out_shape=[jax.ShapeDtypeStruct((R, C), F32)] * 4,
        compiler_params=_params(1),
    )(parts, w, m, v)


def _mesh_place():
    x, y, c = (lax.axis_index(n) for n in MESH_AXES)
    return x, y, c, 4 * x + 2 * y + c


def _peer(x, y, c, d):
    px = 1 - x if d & 4 else x
    py = 1 - y if d & 2 else y
    pc = 1 - c if d & 1 else c
    return (px, py, pc), 4 * px + 2 * py + pc


def gather_now(srcs, *, name):
    n = len(srcs)
    any_spec = pl.BlockSpec(memory_space=pl.ANY)

    def body(*refs):
        ins, outs, token = refs[:n], refs[n:2 * n], refs[2 * n]
        send_sems, recv_sems, local_sems = refs[2 * n + 1:]
        token[...] = jnp.zeros_like(token)
        x, y, c, me = _mesh_place()
        for a in range(n):
            pltpu.make_async_copy(ins[a], outs[a].at[me], local_sems.at[a]).start()
            for d in range(1, N_DEV):
                pltpu.make_async_remote_copy(
                    src_ref=ins[a], dst_ref=outs[a].at[me], send_sem=send_sems.at[a], recv_sem=recv_sems.at[a],
                    device_id=_peer(x, y, c, d)[0], device_id_type=pl.DeviceIdType.MESH).start()
        for a in range(n):
            pltpu.make_async_copy(ins[a], outs[a].at[me], local_sems.at[a]).wait()
            seven = outs[a].at[pl.ds(0, N_DEV - 1)]
            drain = pltpu.make_async_remote_copy(
                src_ref=seven, dst_ref=seven, send_sem=send_sems.at[a], recv_sem=recv_sems.at[a],
                device_id=(x, y, c), device_id_type=pl.DeviceIdType.MESH)
            drain.wait_send()
            drain.wait_recv()

    out = pl.pallas_call(
        body, name=name,
        in_specs=[any_spec] * n, out_specs=[any_spec] * n + [pl.BlockSpec(memory_space=pltpu.VMEM)],
        out_shape=[jax.ShapeDtypeStruct((N_DEV,) + s.shape, s.dtype) for s in srcs] + [jax.ShapeDtypeStruct((8, 128), F32)],
        scratch_shapes=[pltpu.SemaphoreType.DMA((n,)), pltpu.SemaphoreType.DMA((n,)), pltpu.SemaphoreType.DMA((n,))],
    )(*srcs)
    return out[:n], out[n]


_ANY = pl.BlockSpec(memory_space=pl.ANY)
_HBM = pl.BlockSpec(memory_space=pltpu.HBM)
_SEM = pl.BlockSpec(memory_space=pltpu.SEMAPHORE)


def _in_hbm(arrays):
    return [pltpu.with_memory_space_constraint(a, pltpu.HBM) for a in arrays]


def exchange_start(srcs, lands, slots, *, name):
    n, m = len(srcs), len(lands)

    def body(*refs):
        ins, zones = refs[:n], refs[n:n + m]
        send_sems, recv_sems, token = refs[n + m], refs[n + m + 1], refs[-1]
        x, y, c, me = _mesh_place()
        for a in range(n):
            for d in range(1, N_DEV):
                peer, peer_lin = _peer(x, y, c, d)
                src = ins[a] if slots is None else ins[a].at[peer_lin]
                dst = zones[a].at[me] if slots is None else zones[slots[a][0]].at[me, slots[a][1]]
                pltpu.make_async_remote_copy(
                    src_ref=src, dst_ref=dst, send_sem=send_sems.at[a], recv_sem=recv_sems.at[a],
                    device_id=peer, device_id_type=pl.DeviceIdType.MESH).start()
        token[...] = jnp.zeros_like(token)

    both = list(srcs) + list(lands)
    out = pl.pallas_call(
        body, name=name,
        in_specs=[_HBM] * (n + m),
        out_specs=[_SEM, _SEM] + [_HBM] * (n + m) + [pl.BlockSpec(memory_space=pltpu.VMEM)],
        out_shape=[pltpu.SemaphoreType.DMA((n,)), pltpu.SemaphoreType.DMA((n,))]
        + [pltpu.HBM(a.shape, a.dtype) for a in both] + [jax.ShapeDtypeStruct((8, 128), F32)],
        input_output_aliases={i: 2 + i for i in range(n + m)},
        compiler_params=pltpu.CompilerParams(has_side_effects=pltpu.SideEffectType.DATAFLOW_SIDE_EFFECTING),
    )(*_in_hbm(both))
    return out[0], out[1], out[2:2 + n], out[2 + n:2 + n + m], out[-1]


def exchange_wait(send_sems, recv_sems, srcs, lands, slots, after, *, name):
    n, m = len(srcs), len(lands)

    def body(*refs):
        ins, zones = refs[:n], refs[n:n + m]
        send_ref, recv_ref = refs[n + m], refs[n + m + 1]
        x, y, c, _ = _mesh_place()
        for a in range(n):
            seven = (zones[a] if slots is None else ins[a]).at[pl.ds(0, N_DEV - 1)]
            drain = pltpu.make_async_remote_copy(
                src_ref=seven, dst_ref=seven, send_sem=send_ref.at[a], recv_sem=recv_ref.at[a],
                device_id=(x, y, c), device_id_type=pl.DeviceIdType.MESH)
            drain.wait_send()
            drain.wait_recv()

    both = list(srcs) + list(lands)
    out = pl.pallas_call(
        body, name=name,
        in_specs=[_HBM] * (n + m) + [_SEM, _SEM, _ANY],
        out_specs=[_HBM] * (n + m),
        out_shape=[pltpu.HBM(a.shape, a.dtype) for a in both],
        input_output_aliases={i: i for i in range(n + m)},
        compiler_params=pltpu.CompilerParams(has_side_effects=pltpu.SideEffectType.DATAFLOW_SIDE_EFFECTING),
    )(*both, send_sems, recv_sems, after)
    return out[:n], out[n:]


def scatter_finish(remote, lands, vec, *, name):
    n, m = len(remote), len(lands)

    def body(*refs):
        ins, vec_ref, zones_in = refs[:n], refs[n], refs[n + 1:n + 1 + m]
        vec_out = refs[n + 1 + 2 * m]
        send_sems, recv_sems, local_sems = refs[n + 2 + 2 * m:]
        x, y, c, me = _mesh_place()

        def ends(a, j):
            if a == n:
                return vec_ref, vec_out.at[me]
            return ins[a].at[j], zones_in[remote[a][1]].at[me, remote[a][2]]

        for a in range(n + 1):
            pltpu.make_async_copy(*ends(a, me), local_sems.at[a]).start()
            for d in range(1, N_DEV):
                peer, peer_lin = _peer(x, y, c, d)
                src, dst = ends(a, peer_lin)
                pltpu.make_async_remote_copy(
                    src_ref=src, dst_ref=dst, send_sem=send_sems.at[a], recv_sem=recv_sems.at[a],
                    device_id=peer, device_id_type=pl.DeviceIdType.MESH).start()
        for a in range(n + 1):
            pltpu.make_async_copy(*ends(a, me), local_sems.at[a]).wait()
            seven = (vec_out if a == n else ins[a]).at[pl.ds(0, N_DEV - 1)]
            drain = pltpu.make_async_remote_copy(
                src_ref=seven, dst_ref=seven, send_sem=send_sems.at[a], recv_sem=recv_sems.at[a],
                device_id=(x, y, c), device_id_type=pl.DeviceIdType.MESH)
            drain.wait_send()
            drain.wait_recv()

    out = pl.pallas_call(
        body, name=name,
        in_specs=[_ANY] * (n + 1 + m), out_specs=[_ANY] * (m + 1),
        out_shape=[jax.ShapeDtypeStruct(z.shape, z.dtype) for z in lands]
        + [jax.ShapeDtypeStruct((N_DEV,) + vec.shape, vec.dtype)],
        input_output_aliases={n + 1 + i: i for i in range(m)},
        scratch_shapes=[pltpu.SemaphoreType.DMA((n + 1,))] * 3,
    )(*[e[0] for e in remote], vec, *lands)
    return out[:m], out[m]


def _rope_tables(pos):
    inv_freq = ROPE_THETA ** (-jnp.arange(0, ROPE, 2, dtype=F32) / ROPE)
    ang = pos.astype(F32)[:, None] * inv_freq
    cos, sin = jnp.cos(ang), jnp.sin(ang)
    pad = jnp.zeros((pos.shape[0], 128 - ROPE), F32)
    return jnp.concatenate([cos, cos, pad + 1.0], axis=1), jnp.concatenate([-sin, sin, pad], axis=1)


def _pad_last(w, n):
    return jnp.pad(w, [(0, 0)] * (w.ndim - 1) + [(0, n - w.shape[-1])])


def kernel(x, positions, g_mix, g_mlp, attn_w_down, attn_g_q_a, attn_g_kv_a, attn_w_uq, attn_w_ukv, attn_g_qnorm, attn_g_knorm, attn_w_o, conv_w_in, conv_w, conv_w_out, mlp_w1, mlp_w2, loss_target, m_g_mix, m_g_mlp, m_attn_w_down, m_attn_g_q_a, m_attn_g_kv_a, m_attn_w_uq, m_attn_w_ukv, m_attn_g_qnorm, m_attn_g_knorm, m_attn_w_o, m_conv_w_in, m_conv_w, m_conv_w_out, m_mlp_w1, m_mlp_w2, v_g_mix, v_g_mlp, v_attn_w_down, v_attn_g_q_a, v_attn_g_kv_a, v_attn_w_uq, v_attn_w_ukv, v_attn_g_qnorm, v_attn_g_knorm, v_attn_w_o, v_conv_w_in, v_conv_w, v_conv_w_out, v_mlp_w1, v_mlp_w2):
    weights = dict(g_mix=g_mix, g_mlp=g_mlp, attn_w_down=attn_w_down, attn_g_q_a=attn_g_q_a, attn_g_kv_a=attn_g_kv_a,
                   attn_w_uq=attn_w_uq, attn_w_ukv=attn_w_ukv, attn_g_qnorm=attn_g_qnorm, attn_g_knorm=attn_g_knorm,
                   attn_w_o=attn_w_o, conv_w_in=conv_w_in, conv_w=conv_w, conv_w_out=conv_w_out, mlp_w1=mlp_w1, mlp_w2=mlp_w2)
    mom1 = dict(g_mix=m_g_mix, g_mlp=m_g_mlp, attn_w_down=m_attn_w_down, attn_g_q_a=m_attn_g_q_a, attn_g_kv_a=m_attn_g_kv_a,
                attn_w_uq=m_attn_w_uq, attn_w_ukv=m_attn_w_ukv, attn_g_qnorm=m_attn_g_qnorm, attn_g_knorm=m_attn_g_knorm,
                attn_w_o=m_attn_w_o, conv_w_in=m_conv_w_in, conv_w=m_conv_w, conv_w_out=m_conv_w_out, mlp_w1=m_mlp_w1, mlp_w2=m_mlp_w2)
    mom2 = dict(g_mix=v_g_mix, g_mlp=v_g_mlp, attn_w_down=v_attn_w_down, attn_g_q_a=v_attn_g_q_a, attn_g_kv_a=v_attn_g_kv_a,
                attn_w_uq=v_attn_w_uq, attn_w_ukv=v_attn_w_ukv, attn_g_qnorm=v_attn_g_qnorm, attn_g_knorm=v_attn_g_knorm,
                attn_w_o=v_attn_w_o, conv_w_in=v_conv_w_in, conv_w=v_conv_w, conv_w_out=v_conv_w_out, mlp_w1=v_mlp_w1, mlp_w2=v_mlp_w2)
    big = ["attn_w_down", "attn_w_uq", "attn_w_ukv", "attn_w_o", "conv_w_in", "conv_w", "conv_w_out", "mlp_w1", "mlp_w2"]
    small = ["g_mix", "g_mlp", "attn_g_q_a", "attn_g_kv_a", "attn_g_qnorm", "attn_g_knorm"]
    order = ["g_mix", "g_mlp", "attn_w_down", "attn_g_q_a", "attn_g_kv_a", "attn_w_uq", "attn_w_ukv", "attn_g_qnorm",
             "attn_g_knorm", "attn_w_o", "conv_w_in", "conv_w", "conv_w_out", "mlp_w1", "mlp_w2"]

    xs = x[0]
    pos = positions[0]
    target = loss_target[0]
    S = xs.shape[0]
    depth = g_mix.shape[0]
    cos_t, sin_t = _rope_tables(pos)
    pos_col, pos_row = pos.reshape(S, 1), pos.reshape(1, S)

    keys, shards = [], []
    for name in big:
        for l in range(weights[name].shape[0]):
            keys.append((name, l))
            shards.append(weights[name][l] if name == "conv_w" else weights[name][l].astype(WIRE_DTYPE))
    first = [j for j, (name, l) in enumerate(keys) if l == 0 and name.startswith("attn")]
    second = [j for j, (name, l) in enumerate(keys) if l == 0 and name.startswith("mlp")]
    later = [j for j in range(len(keys)) if j not in first + second]
    me = 4 * lax.axis_index("x") + 2 * lax.axis_index("y") + lax.axis_index("c")

    def zones_with_own(js, token):
        return [lax.dynamic_update_slice(lax.empty((N_DEV,) + shards[j].shape, shards[j].dtype),
                                         (shards[j] + token[0, 0].astype(shards[j].dtype))[None],
                                         (me,) + (0,) * shards[j].ndim) for j in js]

    arrived, token = gather_now([shards[j] for j in first], name="gather_first")
    full = dict(zip([keys[j] for j in first], arrived))
    g1 = exchange_start([shards[j] for j in second], zones_with_own(second, token), None, name="gather_mlp0_start")
    g2 = exchange_start([shards[j] for j in later], zones_with_own(later, g1[4]), None, name="gather_rest_start")
    g_mix_0 = g_mix[0] + g2[4][0, 0]

    def rows(name, l):
        g = full[(name, l)]
        return g.reshape(g.shape[0] * g.shape[1], g.shape[2])

    saved = []
    for i in range(depth):
        l = i // 2
        rec = {"x0": xs}
        if i == 1:
            arrived = exchange_wait(*g2[:4], None, xs, name="gather_rest_wait")[1]
            full.update(zip([keys[j] for j in later], arrived))
        if i % 2 == 0:
            wd3 = _pad_last(rows("attn_w_down", l), DOWN_PAD)[None]
            wuq3 = _pad_last(full[("attn_w_uq", l)], QK_PAD)
            wukv3 = full[("attn_w_ukv", l)]
            gqn = _pad_last(attn_g_qnorm[l][None], QK_PAD)
            gkn = _pad_last(attn_g_knorm[l][None], QK_PAD)
            gqa, gkva = attn_g_q_a[l][None], attn_g_kv_a[l][None]
            h, a = norm_matmul(xs, g_mix_0 if i == 0 else g_mix[i], wd3, name="attn_down")
            q, k, v, cq, ckv = mla_pre_fwd(a, gqa, gkva, wuq3, wukv3, gqn, gkn, cos_t, sin_t)
            o, lse = attn_fwd(q, k, v, pos_col, pos_row)
            x1 = matmul_residual(o, rows("attn_w_o", l), xs, name="attn_out")
            rec.update(h=h, a=a, q=q, k=k, v=v, cq=cq, ckv=ckv, o=o, lse=lse, wd3=wd3, wuq3=wuq3, wukv3=wukv3,
                       gqn=gqn, gkn=gkn, gqa=gqa, gkva=gkva)
        else:
            cw = full[("conv_w", l)].transpose(1, 0, 2).reshape(3, D_MODEL)
            h, bcu = norm_matmul(xs, g_mix[i], full[("conv_w_in", l)], name="conv_in", out_dtype=MXU_DTYPE)
            z = conv_fwd(bcu, cw)
            x1 = matmul_residual(z, rows("conv_w_out", l), xs, name="conv_out")
            rec.update(h=h, bcu=bcu, z=z, cw=cw)
        if i == 0:
            arrived = exchange_wait(*g1[:4], None, x1, name="gather_mlp0_wait")[1]
            full.update(zip([keys[j] for j in second], arrived))
        h2, act, slope = norm_matmul(x1, g_mlp[i], full[("mlp_w1", i)], name="mlp_up", mlp=True)
        xs = matmul_residual(act, rows("mlp_w2", i), x1, name="mlp_down")
        rec.update(x1=x1, h2=h2, act=act, slope=slope)
        saved.append(rec)

    sq, dx, dxb = loss_head(xs, target)
    loss = lax.psum(sq[0, 0] * (0.5 / D_MODEL), MESH_AXES)

    grads = {name: [None] * weights[name].shape[0] for name in order}
    token = None
    for i in reversed(range(depth)):
        l = i // 2
        rec = saved[i]
        grads["mlp_w2"][i] = mm_tn(rec["act"], dxb, name="mlp_down_dw", G=1, out_dtype=WIRE_DTYPE).reshape(N_DEV, -1, D_MODEL)
        du = matmul_nt(dxb, rows("mlp_w2", i)[None], name="mlp_down_dx", epi="mlp_du", u=rec["slope"])
        grads["mlp_w1"][i] = mm_tn(rec["h2"], du, name="mlp_up_dw", G=N_DEV, out_dtype=WIRE_DTYPE, a_transposed=True)
        dx1, dx1b, dg = matmul_nt(du, full[("mlp_w1", i)], name="mlp_up_dx", epi="rms_bwd", x=rec["x1"], g=g_mlp[i], dx=dx)
        grads["g_mlp"][i] = dg[0]
        if i == 0:
            flying = [keys[j] for j in second + later]
            srcs = [grads[name][l_] for name, l_ in flying]
            slots = [(big.index(name), l_) for name, l_ in flying]
            zones = [lax.empty((N_DEV, weights[name].shape[0]) + grads[name][-1].shape[1:], grads[name][-1].dtype)
                     for name in big]
            for src, (k, l_) in zip(srcs, slots):
                own = lax.dynamic_index_in_dim(src, me, 0, keepdims=True)[None]
                zones[k] = lax.dynamic_update_slice(zones[k], own, (me, l_) + (0,) * (src.ndim - 1))
            s_send, s_recv, s_srcs, s_zones, token = exchange_start(srcs, zones, slots, name="scatter_rest_start")
        if i % 2 == 0:
            grads["attn_w_o"][l] = mm_tn(rec["o"], dx1b, name="attn_out_dw", G=1, out_dtype=WIRE_DTYPE,
                                         after=token).reshape(N_DEV, -1, D_MODEL)
            do = matmul_nt(dx1b, rows("attn_w_o", l)[None], name="attn_out_dx")
            dq, dk, dv = attn_bwd(rec["q"], rec["k"], rec["v"], do, rec["lse"], attn_delta(rec["o"], do), pos_col, pos_row)
            da, dwuq, dwukv, dgqn, dgkn, dgqa, dgkva = mla_pre_bwd(
                dq, dk, dv, rec["a"], rec["cq"], rec["ckv"], rec["gqa"], rec["gkva"], rec["wuq3"], rec["wukv3"],
                rec["gqn"], rec["gkn"], cos_t, sin_t)
            grads["attn_w_uq"][l] = dwuq[:, :, :QK_DIM].astype(WIRE_DTYPE)
            grads["attn_w_ukv"][l] = dwukv.astype(WIRE_DTYPE)
            grads["attn_g_qnorm"][l] = dgqn[0, :QK_DIM]
            grads["attn_g_knorm"][l] = dgkn[0, :QK_DIM]
            grads["attn_g_q_a"][l] = dgqa[0]
            grads["attn_g_kv_a"][l] = dgkva[0]
            dwd = mm_tn(rec["h"], da, name="attn_down_dw", G=1, out_dtype=WIRE_DTYPE, a_transposed=True)
            grads["attn_w_down"][l] = dwd[0, :, :DOWN].reshape(N_DEV, -1, DOWN)
            dx, dxb, dg = matmul_nt(da, rec["wd3"], name="attn_down_dx", epi="rms_bwd", x=rec["x0"], g=g_mix[i], dx=dx1)
        else:
            grads["conv_w_out"][l] = mm_tn(rec["z"], dx1b, name="conv_out_dw", G=1, out_dtype=WIRE_DTYPE).reshape(N_DEV, -1, D_MODEL)
            dz = matmul_nt(dx1b, rows("conv_w_out", l)[None], name="conv_out_dx")
            dbcu, dcw = conv_bwd(dz, rec["bcu"], rec["cw"])
            grads["conv_w"][l] = dcw.reshape(3, N_DEV, -1).transpose(1, 0, 2)
            grads["conv_w_in"][l] = mm_tn(rec["h"], dbcu, name="conv_in_dw", G=N_DEV, out_dtype=WIRE_DTYPE, a_transposed=True)
            dx, dxb, dg = matmul_nt(dbcu, full[("conv_w_in", l)], name="conv_in_dx", epi="rms_bwd", x=rec["x0"], g=g_mix[i], dx=dx1)
        grads["g_mix"][i] = dg[0]

    sizes = [weights[name].size for name in small]
    n_small = sum(sizes)
    rows_small = -(-n_small // (8 * 128)) * 8

    def pack(tree):
        flat = jnp.concatenate([jnp.stack(tree[name]).reshape(-1) if isinstance(tree[name], list) else tree[name].reshape(-1)
                                for name in small])
        return jnp.pad(flat, (0, rows_small * 128 - n_small)).reshape(rows_small, 128)

    s_srcs, s_zones = exchange_wait(s_send, s_recv, s_srcs, s_zones, slots, dx, name="scatter_rest_wait")
    remote = [(grads[name][l_], big.index(name), l_) for name, l_ in (keys[j] for j in first)]
    parts, gain_parts = scatter_finish(remote, s_zones, pack(grads), name="scatter_last")

    out = {}
    for name, part in zip(big, parts):
        w = weights[name]
        flat = lambda t: t.reshape(-1, t.shape[-1])
        res = adamw(part.reshape(N_DEV, -1, w.shape[-1]), flat(w), flat(mom1[name]), flat(mom2[name]), name="adamw_" + name)
        out[name] = [r.reshape(w.shape) for r in res]
    res = adamw(gain_parts, pack(weights), pack(mom1), pack(mom2), name="adamw_gains")
    offset = 0
    for name, size in zip(small, sizes):
        out[name] = [r.reshape(-1)[offset:offset + size].reshape(weights[name].shape) for r in res]
        offset += size

    return (loss, dx[None], *[out[n][0] for n in order], *[out[n][1] for n in order],
            *[out[n][2] for n in order], *[out[n][3] for n in order])
```

```python
import jax
import jax.numpy as jnp
import numpy as np
from jax import lax
from jax.experimental import pallas as pl
from jax.experimental.pallas import tpu as pltpu

F32 = jnp.float32
MXU_DTYPE = jnp.bfloat16
WIRE_DTYPE = jnp.bfloat16

D_MODEL = 1024
N_HEADS = 8
NOPE = 128
ROPE = 64
QK_DIM = NOPE + ROPE
QK_PAD = 256
V_DIM = 128
Q_LORA = 256
KV_LORA = 128
DOWN = Q_LORA + KV_LORA + ROPE
DOWN_PAD = 512
ROPE_THETA = 10000.0
EPS = 1e-6
SM_SCALE = QK_DIM ** -0.5
LOG2E = 1.4426950408889634
Q_PRESCALE = SM_SCALE * LOG2E
ADAM_LR, ADAM_B1, ADAM_B2, ADAM_EPS, ADAM_WD, ADAM_STEP = 0.001, 0.9, 0.999, 1e-08, 0.01, 10
N_DEV = 8
MESH_AXES = ("x", "y", "c")

TM = 512
TM_WIDE = 1024
TILE_BUDGET = 32 << 20
TM_TOKENS_TN = 2048
TQ = 512
HEADS_FWD = 8
TQ_BWD = 2048
BWD_CHUNK = 256
TROW = 256
HALO = 16
VMEM_LIMIT = 48 << 20

NN = (((1,), (0,)), ((), ()))
NT = (((1,), (1,)), ((), ()))
TN = (((0,), (0,)), ((), ()))


def _dot(a, b, dims=NN):
    return lax.dot_general(a.astype(MXU_DTYPE), b.astype(MXU_DTYPE), dims, preferred_element_type=F32)


def _params(n_axes):
    return pltpu.CompilerParams(dimension_semantics=("arbitrary",) * n_axes, vmem_limit_bytes=VMEM_LIMIT)


def _rms(xv, n):
    r = lax.rsqrt(jnp.sum(xv * xv, axis=-1, keepdims=True) / n + EPS)
    return xv * r, r


def _rms_bwd(dy, xhat, r, g, n):
    dg = jnp.sum(dy * xhat, axis=0, keepdims=True)
    dxh = dy * g
    dx = r * (dxh - xhat * (jnp.sum(dxh * xhat, axis=-1, keepdims=True) / n))
    return dx, dg


def _swap_halves(t):
    lane = lax.broadcasted_iota(jnp.int32, t.shape, 1)
    return jnp.where(lane < ROPE // 2, pltpu.roll(t, 128 - ROPE // 2, 1), pltpu.roll(t, ROPE // 2, 1))


def _rope(t, cos_t, sin_t):
    return t * cos_t + _swap_halves(t) * sin_t


def _rope_bwd(dout, cos_t, sin_t):
    return dout * cos_t + _swap_halves(dout * sin_t)


def _resident(shape):
    return pl.BlockSpec(shape, lambda i: (0,) * len(shape))


def _token_tile(S, row_bytes, resident_bytes):
    wide = min(TM_WIDE, S)
    return wide if 2 * (wide * row_bytes + resident_bytes) <= TILE_BUDGET else min(TM, S)


def norm_matmul(x, g, w3, *, name, mlp=False, out_dtype=F32):
    if mlp:
        out_dtype = MXU_DTYPE
    S, D = x.shape
    G, _, Nb = w3.shape
    N = G * Nb
    n_out = 2 if mlp else 1
    tm = _token_tile(S, D * 4 + D * 2 + n_out * N * jnp.dtype(out_dtype).itemsize, w3.size * w3.dtype.itemsize)

    def body(x_ref, g_ref, w_ref, h_ref, o_ref, *slope_ref):
        xv = x_ref[...]
        r = lax.rsqrt(jnp.mean(xv * xv, axis=-1, keepdims=True) + EPS)
        h = (xv * r * g_ref[...]).astype(h_ref.dtype)
        h_ref[...] = h.T
        for gi in range(G):
            cols = slice(gi * Nb, (gi + 1) * Nb)
            acc = _dot(h, w_ref[gi])
            if mlp:
                acc = jnp.maximum(acc, 0.0)
                slope_ref[0][:, cols] = (2.0 * acc).astype(out_dtype)
                acc = jnp.square(acc)
            o_ref[:, cols] = acc.astype(o_ref.dtype)

    rows = lambda w: pl.BlockSpec((tm, w), lambda i: (i, 0))
    return pl.pallas_call(
        body, name=name, grid=(S // tm,),
        in_specs=[rows(D), _resident((1, D)), _resident((G, D, Nb))],
        out_specs=[pl.BlockSpec((D, tm), lambda i: (0, i))] + [rows(N)] * n_out,
        out_shape=[jax.ShapeDtypeStruct((D, S), MXU_DTYPE)] + [jax.ShapeDtypeStruct((S, N), out_dtype)] * n_out,
        compiler_params=_params(1),
    )(x, g.reshape(1, D), w3)


def matmul_residual(a, w, res, *, name, target=None):
    S, K = a.shape
    _, N = w.shape
    tm = _token_tile(S, K * a.dtype.itemsize + 2 * N * 4 + (0 if target is None else N * 6), w.size * w.dtype.itemsize)

    def body(a_ref, w_ref, r_ref, *rest):
        y = r_ref[...] + _dot(a_ref[...], w_ref[...])
        if target is None:
            rest[0][...] = y
            return
        t_ref, l_ref, dy_ref, dyb_ref = rest
        err = y - t_ref[...]
        dy = err / N
        dy_ref[...] = dy
        dyb_ref[...] = dy.astype(dyb_ref.dtype)
        part = jnp.full((1, 128), jnp.sum(err * err), F32)

        @pl.when(pl.program_id(0) == 0)
        def _():
            l_ref[...] = part

        @pl.when(pl.program_id(0) != 0)
        def _():
            l_ref[...] += part

    rows = lambda w_: pl.BlockSpec((tm, w_), lambda i: (i, 0))
    in_specs, args = [rows(K), _resident((K, N)), rows(N)], [a, w, res]
    if target is None:
        out_specs, out_shape = rows(N), jax.ShapeDtypeStruct((S, N), F32)
    else:
        in_specs.append(rows(N))
        args.append(target)
        out_specs = [_resident((1, 128)), rows(N), rows(N)]
        out_shape = [jax.ShapeDtypeStruct((1, 128), F32), jax.ShapeDtypeStruct((S, N), F32),
                     jax.ShapeDtypeStruct((S, N), MXU_DTYPE)]
    return pl.pallas_call(
        body, name=name, grid=(S // tm,),
        in_specs=in_specs, out_specs=out_specs, out_shape=out_shape,
        compiler_params=_params(1),
    )(*args)


def matmul_nt(a, w3, *, name, epi="plain", u=None, x=None, g=None, dx=None):
    S, N = a.shape
    G, Ko, Nb = w3.shape
    assert N == G * Nb
    with_delta = epi == "plain" and u is not None
    row_bytes = N * a.dtype.itemsize + Ko * ({"plain": 4, "mlp_du": 4, "rms_bwd": 14}[epi] + (4 if with_delta else 0))
    tm = _token_tile(S, row_bytes, w3.size * w3.dtype.itemsize)
    tko = min(Ko, 512)

    def body(a_ref, w_ref, *rest):
        if epi == "mlp_du":
            u_ref, o_ref = rest
            av = a_ref[...].astype(MXU_DTYPE)
            for j in range(Ko // tko):
                cols = slice(j * tko, (j + 1) * tko)
                da = _dot(av, w_ref[0, cols, :], NT)
                o_ref[:, cols] = (da * u_ref[:, cols].astype(F32)).astype(o_ref.dtype)
            return
        acc = _dot(a_ref[:, :Nb], w_ref[0], NT)
        for gi in range(1, G):
            acc = acc + _dot(a_ref[:, gi * Nb:(gi + 1) * Nb], w_ref[gi], NT)
        if epi == "plain":
            if with_delta:
                u_ref, o_ref, d_ref = rest
                for h in range(N_HEADS):
                    cols = slice(h * V_DIM, (h + 1) * V_DIM)
                    d_ref[h] = jnp.sum((u_ref[:, cols] * acc[:, cols]).T, axis=0, keepdims=True)
            else:
                o_ref = rest[0]
            o_ref[...] = acc
        else:
            x_ref, g_ref, dx_ref, o_ref, ob_ref, dg_ref = rest
            xhat, r = _rms(x_ref[...], Ko)
            dxb, dg = _rms_bwd(acc, xhat, r, g_ref[...], Ko)
            dx_new = dx_ref[...] + dxb
            o_ref[...] = dx_new
            ob_ref[...] = dx_new.astype(ob_ref.dtype)

            @pl.when(pl.program_id(0) == 0)
            def _():
                dg_ref[...] = dg

            @pl.when(pl.program_id(0) != 0)
            def _():
                dg_ref[...] += dg

    rows = lambda w_: pl.BlockSpec((tm, w_), lambda i: (i, 0))
    in_specs = [rows(N), _resident((G, Ko, Nb))]
    args = [a, w3]
    if with_delta:
        in_specs.append(rows(Ko))
        args.append(u)
        out_shape = [jax.ShapeDtypeStruct((S, Ko), F32), jax.ShapeDtypeStruct((N_HEADS, 1, S), F32)]
        out_specs = [rows(Ko), pl.BlockSpec((N_HEADS, 1, tm), lambda i: (0, 0, i))]
    elif epi == "plain":
        out_shape, out_specs = jax.ShapeDtypeStruct((S, Ko), F32), rows(Ko)
    elif epi == "mlp_du":
        in_specs.append(rows(Ko))
        args.append(u)
        out_shape, out_specs = jax.ShapeDtypeStruct((S, Ko), MXU_DTYPE), rows(Ko)
    else:
        in_specs += [rows(Ko), _resident((1, Ko)), rows(Ko)]
        args += [x, g.reshape(1, Ko), dx]
        out_shape = [jax.ShapeDtypeStruct((S, Ko), F32), jax.ShapeDtypeStruct((S, Ko), MXU_DTYPE),
                     jax.ShapeDtypeStruct((1, Ko), F32)]
        out_specs = [rows(Ko), rows(Ko), _resident((1, Ko))]
    return pl.pallas_call(
        body, name=name, grid=(S // tm,),
        in_specs=in_specs, out_specs=out_specs, out_shape=out_shape,
        compiler_params=_params(1),
    )(*args)


def mm_tn(a, b, *, name, G, out_dtype, after=None, a_transposed=False):
    order = [] if after is None else [after]
    Ka, S = a.shape if a_transposed else a.shape[::-1]
    _, N = b.shape
    Nb = N // G
    assert Nb <= 1024
    tm = min(TM_TOKENS_TN if b.dtype.itemsize == 2 else TM_TOKENS_TN // 2, S)
    tka = min(Ka, 1024)
    gb = 2 if G % 2 == 0 and Nb <= 512 else 1
    ns = S // tm

    def body(a_ref, b_ref, *rest):
        o_ref, acc = rest[-2:]
        s = pl.program_id(2)

        @pl.when(s == 0)
        def _():
            acc[...] = jnp.zeros_like(acc)

        av = a_ref[...]
        for gi in range(gb):
            acc[gi] += _dot(av, b_ref[:, gi * Nb:(gi + 1) * Nb], NN if a_transposed else TN)

        @pl.when(s == ns - 1)
        def _():
            o_ref[...] = acc[...].astype(o_ref.dtype)

    a_spec = pl.BlockSpec((tka, tm), lambda i, j, s: (i, s)) if a_transposed else pl.BlockSpec((tm, tka), lambda i, j, s: (s, i))
    return pl.pallas_call(
        body, name=name, grid=(Ka // tka, G // gb, ns),
        in_specs=[a_spec,
                  pl.BlockSpec((tm, gb * Nb), lambda i, j, s: (s, j))] + [pl.BlockSpec(memory_space=pl.ANY)] * len(order),
        out_specs=pl.BlockSpec((gb, tka, Nb), lambda i, j, s: (j, i, 0)),
        out_shape=jax.ShapeDtypeStruct((G, Ka, Nb), out_dtype),
        scratch_shapes=[pltpu.VMEM((gb, tka, Nb), F32)],
        compiler_params=_params(3),
    )(a, b, *order)


def mla_pre_fwd(a, gqa, gkva, wuq3, wukv3, gqn, gkn, cos_t, sin_t):
    S = a.shape[0]
    tm = min(TM, S)
    H = N_HEADS

    def body(a_ref, gqa_ref, gkva_ref, wuq_ref, wukv_ref, gqn_ref, gkn_ref, cos_ref, sin_ref,
             q_ref, k_ref, v_ref, cq_ref, ckv_ref):
        av = a_ref[...]
        cq = (_rms(av[:, :Q_LORA], Q_LORA)[0] * gqa_ref[...]).astype(cq_ref.dtype)
        ckv = (_rms(av[:, Q_LORA:Q_LORA + KV_LORA], KV_LORA)[0] * gkva_ref[...]).astype(ckv_ref.dtype)
        cq_ref[...] = cq
        ckv_ref[...] = ckv
        kpe = av[:, Q_LORA + KV_LORA:]
        cos_v, sin_v = cos_ref[...], sin_ref[...]
        for h in range(H):
            qn = _rms(_dot(cq, wuq_ref[h]), QK_DIM)[0] * gqn_ref[...]
            qr = jnp.concatenate([qn[:, :NOPE], _rope(qn[:, NOPE:], cos_v, sin_v)], axis=1)
            q_ref[h] = (qr * Q_PRESCALE).astype(q_ref.dtype)
            kvp = _dot(ckv, wukv_ref[h])
            kn = _rms(jnp.concatenate([kvp[:, :NOPE], kpe], axis=1), QK_DIM)[0] * gkn_ref[...]
            k_ref[h] = jnp.concatenate([kn[:, :NOPE], _rope(kn[:, NOPE:], cos_v, sin_v)], axis=1).astype(k_ref.dtype)
            v_ref[h] = kvp[:, NOPE:].astype(v_ref.dtype)

    row = lambda w: pl.BlockSpec((tm, w), lambda i: (i, 0))
    heads = lambda w: pl.BlockSpec((H, tm, w), lambda i: (0, i, 0))
    return pl.pallas_call(
        body, name="mla_pre_fwd", grid=(S // tm,),
        in_specs=[row(DOWN_PAD), _resident((1, Q_LORA)), _resident((1, KV_LORA)),
                  _resident((H, Q_LORA, QK_PAD)), _resident((H, KV_LORA, NOPE + V_DIM)),
                  _resident((1, QK_PAD)), _resident((1, QK_PAD)), row(128), row(128)],
        out_specs=[heads(QK_PAD), heads(QK_PAD), heads(V_DIM), row(Q_LORA), row(KV_LORA)],
        out_shape=[jax.ShapeDtypeStruct((H, S, QK_PAD), MXU_DTYPE),
                   jax.ShapeDtypeStruct((H, S, QK_PAD), MXU_DTYPE),
                   jax.ShapeDtypeStruct((H, S, V_DIM), MXU_DTYPE),
                   jax.ShapeDtypeStruct((S, Q_LORA), MXU_DTYPE),
                   jax.ShapeDtypeStruct((S, KV_LORA), MXU_DTYPE)],
        compiler_params=_params(1),
    )(a, gqa, gkva, wuq3, wukv3, gqn, gkn, cos_t, sin_t)


def mla_pre_bwd(dq, dk, dv, a, cq, ckv, gqa, gkva, wuq3, wukv3, gqn, gkn, cos_t, sin_t):
    S = a.shape[0]
    tm = min(TM, S)
    H = N_HEADS

    def body(dq_ref, dk_ref, dv_ref, a_ref, cq_ref, ckv_ref, gqa_ref, gkva_ref, wuq_ref, wukv_ref, gqn_ref, gkn_ref,
             cos_ref, sin_ref, da_ref, dwuq_ref, dwukv_ref, dgqn_ref, dgkn_ref, dgqa_ref, dgkva_ref):
        @pl.when(pl.program_id(0) == 0)
        def _():
            for ref in (dwuq_ref, dwukv_ref, dgqn_ref, dgkn_ref, dgqa_ref, dgkva_ref):
                ref[...] = jnp.zeros_like(ref)

        av = a_ref[...]
        kpe = av[:, Q_LORA + KV_LORA:]
        cos_v, sin_v = cos_ref[...], sin_ref[...]
        cqv, ckvv = cq_ref[...], ckv_ref[...]
        dcq = jnp.zeros((tm, Q_LORA), F32)
        dckv = jnp.zeros((tm, KV_LORA), F32)
        dkpe = jnp.zeros((tm, 128), F32)
        dgqn = jnp.zeros((1, QK_PAD), F32)
        dgkn = jnp.zeros((1, QK_PAD), F32)
        up = lambda h: (_dot(cqv, wuq_ref[h]), _dot(ckvv, wukv_ref[h]))
        nxt = up(0)
        for h in range(H):
            wuq, wukv = wuq_ref[h], wukv_ref[h]
            qp, kvp = nxt
            if h + 1 < H:
                nxt = up(h + 1)
            qhat, rq = _rms(qp, QK_DIM)
            dqr = dq_ref[h] * SM_SCALE
            dqn = jnp.concatenate([dqr[:, :NOPE], _rope_bwd(dqr[:, NOPE:], cos_v, sin_v)], axis=1)
            dqp, dg = _rms_bwd(dqn, qhat, rq, gqn_ref[...], QK_DIM)
            dgqn = dgqn + dg
            dqp = dqp.astype(MXU_DTYPE)
            dwuq_ref[h] += _dot(cqv, dqp, TN)
            dcq = dcq + _dot(dqp, wuq, NT)
            khat, rk = _rms(jnp.concatenate([kvp[:, :NOPE], kpe], axis=1), QK_DIM)
            dkr = dk_ref[h] * (1.0 / LOG2E)
            dkn = jnp.concatenate([dkr[:, :NOPE], _rope_bwd(dkr[:, NOPE:], cos_v, sin_v)], axis=1)
            dkk, dg = _rms_bwd(dkn, khat, rk, gkn_ref[...], QK_DIM)
            dgkn = dgkn + dg
            dkpe = dkpe + dkk[:, NOPE:]
            dkvp = jnp.concatenate([dkk[:, :NOPE], dv_ref[h]], axis=1).astype(MXU_DTYPE)
            dwukv_ref[h] += _dot(ckvv, dkvp, TN)
            dckv = dckv + _dot(dkvp, wukv, NT)
        dgqn_ref[...] += dgqn
        dgkn_ref[...] += dgkn
        ahat, r = _rms(av[:, :Q_LORA], Q_LORA)
        daq, dg = _rms_bwd(dcq, ahat, r, gqa_ref[...], Q_LORA)
        dgqa_ref[...] += dg
        ahat, r = _rms(av[:, Q_LORA:Q_LORA + KV_LORA], KV_LORA)
        dakv, dg = _rms_bwd(dckv, ahat, r, gkva_ref[...], KV_LORA)
        dgkva_ref[...] += dg
        da_ref[...] = jnp.concatenate([daq, dakv, dkpe], axis=1)

    row = lambda w: pl.BlockSpec((tm, w), lambda i: (i, 0))
    heads = lambda w: pl.BlockSpec((H, tm, w), lambda i: (0, i, 0))
    return pl.pallas_call(
        body, name="mla_pre_bwd", grid=(S // tm,),
        in_specs=[heads(QK_PAD), heads(QK_PAD), heads(V_DIM), row(DOWN_PAD), row(Q_LORA), row(KV_LORA),
                  _resident((1, Q_LORA)), _resident((1, KV_LORA)),
                  _resident((H, Q_LORA, QK_PAD)), _resident((H, KV_LORA, NOPE + V_DIM)),
                  _resident((1, QK_PAD)), _resident((1, QK_PAD)), row(128), row(128)],
        out_specs=[row(DOWN_PAD), _resident((H, Q_LORA, QK_PAD)), _resident((H, KV_LORA, NOPE + V_DIM)),
                   _resident((1, QK_PAD)), _resident((1, QK_PAD)), _resident((1, Q_LORA)), _resident((1, KV_LORA))],
        out_shape=[jax.ShapeDtypeStruct((S, DOWN_PAD), F32),
                   jax.ShapeDtypeStruct((H, Q_LORA, QK_PAD), F32),
                   jax.ShapeDtypeStruct((H, KV_LORA, NOPE + V_DIM), F32),
                   jax.ShapeDtypeStruct((1, QK_PAD), F32), jax.ShapeDtypeStruct((1, QK_PAD), F32),
                   jax.ShapeDtypeStruct((1, Q_LORA), F32), jax.ShapeDtypeStruct((1, KV_LORA), F32)],
        compiler_params=_params(1),
    )(dq, dk, dv, a, cq, ckv, gqa, gkva, wuq3, wukv3, gqn, gkn, cos_t, sin_t)


def _pair_tables(nb, key_major):
    if key_major:
        pairs = [(qi, kj) for kj in range(nb) for qi in range(kj, nb)]
    else:
        pairs = [(qi, ki) for qi in range(nb) for ki in range(qi + 1)]
    return (jnp.asarray(np.array([p[0] for p in pairs], np.int32)),
            jnp.asarray(np.array([p[1] for p in pairs], np.int32)))


def _scores_t(k, q, pk_col, pq_row, masked):
    s = _dot(k, q, NT)
    return jnp.where(pq_row >= pk_col, s, jnp.finfo(F32).min) if masked else s


def attn_fwd(q, k, v, pos_col, pos_row):
    H, S, _ = q.shape
    t = min(TQ, S)
    nb = S // t
    hb = HEADS_FWD
    qt, kt = _pair_tables(nb, key_major=False)

    def body(qt_ref, kt_ref, q_ref, k_ref, v_ref, pk_ref, pq_ref, o_ref, lse_ref, m_s, l_s, acc):
        step = pl.program_id(1)
        qi, ki = qt_ref[step], kt_ref[step]

        @pl.when(ki == 0)
        def _():
            m_s[...] = jnp.full_like(m_s, -jnp.inf)
            l_s[...] = jnp.zeros_like(l_s)
            acc[...] = jnp.zeros_like(acc)

        def update(masked):
            scores = lambda hh: _scores_t(k_ref[hh], q_ref[hh], pk_ref[...], pq_ref[...], masked)
            def weighted_values(hh, p, alpha):
                acc[hh] = alpha * acc[hh] + _dot(v_ref[hh], p, TN)

            s_next = scores(0)
            pending = None
            for hh in range(hb):
                s = s_next
                if hh + 1 < hb:
                    s_next = scores(hh + 1)
                m_old = m_s[hh]
                m_new = jnp.maximum(m_old, jnp.max(s, axis=0, keepdims=True))
                p = jnp.exp2(s - m_new)
                alpha = jnp.exp2(m_old - m_new)
                l_s[hh] = alpha * l_s[hh] + jnp.sum(p, axis=0, keepdims=True)
                m_s[hh] = m_new
                if pending is not None:
                    weighted_values(*pending)
                pending = (hh, p, alpha)
            weighted_values(*pending)

        @pl.when(ki < qi)
        def _():
            update(False)

        @pl.when(ki == qi)
        def _():
            update(True)
            for hh in range(hb):
                o_ref[:, hh * V_DIM:(hh + 1) * V_DIM] = (acc[hh] / l_s[hh]).T
                lse_ref[hh] = m_s[hh] + jnp.log(l_s[hh]) * LOG2E

    grid_spec = pltpu.PrefetchScalarGridSpec(
        num_scalar_prefetch=2, grid=(H // hb, qt.shape[0]),
        in_specs=[pl.BlockSpec((hb, t, QK_PAD), lambda h, s, qt, kt: (h, qt[s], 0)),
                  pl.BlockSpec((hb, t, QK_PAD), lambda h, s, qt, kt: (h, kt[s], 0)),
                  pl.BlockSpec((hb, t, V_DIM), lambda h, s, qt, kt: (h, kt[s], 0)),
                  pl.BlockSpec((t, 1), lambda h, s, qt, kt: (kt[s], 0)),
                  pl.BlockSpec((1, t), lambda h, s, qt, kt: (0, qt[s]))],
        out_specs=[pl.BlockSpec((t, hb * V_DIM), lambda h, s, qt, kt: (qt[s], h)),
                   pl.BlockSpec((hb, 1, t), lambda h, s, qt, kt: (h, 0, qt[s]))],
        scratch_shapes=[pltpu.VMEM((hb, 1, t), F32), pltpu.VMEM((hb, 1, t), F32), pltpu.VMEM((hb, V_DIM, t), F32)])
    return pl.pallas_call(
        body, name="attn_fwd", grid_spec=grid_spec,
        out_shape=[jax.ShapeDtypeStruct((S, H * V_DIM), F32), jax.ShapeDtypeStruct((H, 1, S), F32)],
        compiler_params=_params(2),
    )(qt, kt, q, k, v, pos_col, pos_row)


def attn_bwd(q, k, v, do, lse, delta, pos_col, pos_row):
    H, S, _ = q.shape
    t = min(TQ_BWD, S)
    nb = S // t
    qt, kt = _pair_tables(nb, key_major=True)
    tc = min(BWD_CHUNK, t)

    def body(qt_ref, kt_ref, q_ref, k_ref, v_ref, do_ref, lse_ref, dl_ref, pk_ref, pq_ref, dq_ref, dk_ref, dv_ref):
        step = pl.program_id(1)
        qi, kj = qt_ref[step], kt_ref[step]

        @pl.when(step == 0)
        def _():
            dq_ref[...] = jnp.zeros_like(dq_ref)

        @pl.when(qi == kj)
        def _():
            dk_ref[...] = jnp.zeros_like(dk_ref)
            dv_ref[...] = jnp.zeros_like(dv_ref)

        def update(masked):
            seen = lambda c: (c + 1) * tc if masked else t

            def first_matmuls(c):
                cols, ke = slice(c * tc, (c + 1) * tc), seen(c)
                qc = q_ref[cols, :]
                doc = do_ref[cols, :].astype(MXU_DTYPE)
                s = _scores_t(k_ref[:ke, :], qc, pk_ref[:ke, :], pq_ref[:, cols], masked)
                return qc, doc, s, _dot(v_ref[:ke, :], doc, NT)

            nxt = first_matmuls(0)
            for c in range(t // tc):
                qc, doc, s, dp = nxt
                if c + 1 < t // tc:
                    nxt = first_matmuls(c + 1)
                cols, ke = slice(c * tc, (c + 1) * tc), seen(c)
                p = jnp.exp2(s - lse_ref[:, cols])
                ds = (p * (dp - dl_ref[:, cols])).astype(MXU_DTYPE)
                dv_ref[:ke, :] += _dot(p, doc)
                dk_ref[:ke, :] += _dot(ds, qc)
                rows = pl.ds(pl.multiple_of(qi * t + c * tc, tc), tc)
                dq_ref[rows, :] += _dot(ds, k_ref[:ke, :], TN)

        @pl.when(qi == kj)
        def _():
            update(True)

        @pl.when(qi != kj)
        def _():
            update(False)

    q_idx = lambda h, s, qt, kt: (h, qt[s], 0)
    k_idx = lambda h, s, qt, kt: (h, kt[s], 0)
    row_idx = lambda h, s, qt, kt: (h, 0, qt[s])
    grid_spec = pltpu.PrefetchScalarGridSpec(
        num_scalar_prefetch=2, grid=(H, qt.shape[0]),
        in_specs=[pl.BlockSpec((None, t, QK_PAD), q_idx),
                  pl.BlockSpec((None, t, QK_PAD), k_idx),
                  pl.BlockSpec((None, t, V_DIM), k_idx),
                  pl.BlockSpec((t, V_DIM), lambda h, s, qt, kt: (qt[s], h)),
                  pl.BlockSpec((None, 1, t), row_idx),
                  pl.BlockSpec((None, 1, t), row_idx),
                  pl.BlockSpec((t, 1), lambda h, s, qt, kt: (kt[s], 0)),
                  pl.BlockSpec((1, t), lambda h, s, qt, kt: (0, qt[s]))],
        out_specs=[pl.BlockSpec((None, S, QK_PAD), lambda h, s, qt, kt: (h, 0, 0)),
                   pl.BlockSpec((None, t, QK_PAD), k_idx),
                   pl.BlockSpec((None, t, V_DIM), k_idx)])
    return pl.pallas_call(
        body, name="attn_bwd", grid_spec=grid_spec,
        out_shape=[jax.ShapeDtypeStruct((H, S, QK_PAD), F32), jax.ShapeDtypeStruct((H, S, QK_PAD), F32),
                   jax.ShapeDtypeStruct((H, S, V_DIM), F32)],
        compiler_params=_params(2),
    )(qt, kt, q, k, v, do, lse, delta, pos_col, pos_row)


def _conv_specs(S, tr):
    hb = tr // HALO
    main = lambda third: pl.BlockSpec((tr, D_MODEL), lambda r: (r, third))
    prev = lambda third: pl.BlockSpec((HALO, D_MODEL), lambda r: (jnp.maximum(r * hb - 1, 0), third))
    nxt = lambda third: pl.BlockSpec((HALO, D_MODEL), lambda r: (jnp.minimum((r + 1) * hb, S // HALO - 1), third))
    return main, prev, nxt


def _f32(ref):
    return ref[...].astype(F32)


def _conv_taps(gc, uu, w_ref, first):
    u2 = gc * uu
    rows = lax.broadcasted_iota(jnp.int32, u2.shape, 0)
    u2 = jnp.where((rows < HALO) & first, 0.0, u2)
    s1 = pltpu.roll(u2, 1, 0)
    s2 = pltpu.roll(u2, 2, 0)
    u3 = w_ref[2:3, :] * u2 + w_ref[1:2, :] * s1 + w_ref[0:1, :] * s2
    return u2, s1, s2, u3


def conv_fwd(bcu, cw):
    S = bcu.shape[0]
    tr = min(TROW, S)
    main, prev, _ = _conv_specs(S, tr)

    def body(gb_ref, gc_ref, u_ref, gch_ref, uh_ref, w_ref, z_ref):
        gc = jnp.concatenate([_f32(gch_ref), _f32(gc_ref)], axis=0)
        uu = jnp.concatenate([_f32(uh_ref), _f32(u_ref)], axis=0)
        u3 = _conv_taps(gc, uu, w_ref, pl.program_id(0) == 0)[3]
        z_ref[...] = (_f32(gb_ref) * u3[HALO:]).astype(z_ref.dtype)

    return pl.pallas_call(
        body, name="conv_fwd", grid=(S // tr,),
        in_specs=[main(0), main(1), main(2), prev(1), prev(2), _resident((3, D_MODEL))],
        out_specs=main(0),
        out_shape=jax.ShapeDtypeStruct((S, D_MODEL), MXU_DTYPE),
        compiler_params=_params(1),
    )(bcu, bcu, bcu, bcu, bcu, cw)


def conv_bwd(dz, bcu, cw):
    S = bcu.shape[0]
    tr = min(TROW, S)
    nr = S // tr
    main, prev, nxt = _conv_specs(S, tr)

    def body(dz_ref, dzn_ref, gb_ref, gbn_ref, gc_ref, u_ref, gch_ref, uh_ref, w_ref, o_ref, dw_ref):
        r = pl.program_id(0)
        gcv, uv = _f32(gc_ref), _f32(u_ref)
        gc = jnp.concatenate([_f32(gch_ref), gcv], axis=0)
        uu = jnp.concatenate([_f32(uh_ref), uv], axis=0)
        u2, s1, s2, u3 = _conv_taps(gc, uu, w_ref, r == 0)
        dzv = dz_ref[...]
        du3 = jnp.concatenate([dzv * _f32(gb_ref), dzn_ref[...] * _f32(gbn_ref)], axis=0)
        rows = lax.broadcasted_iota(jnp.int32, du3.shape, 0)
        du3 = jnp.where((rows >= tr) & (r == nr - 1), 0.0, du3)
        n1 = pltpu.roll(du3, tr + HALO - 1, 0)
        n2 = pltpu.roll(du3, tr + HALO - 2, 0)
        du2 = (w_ref[2:3, :] * du3 + w_ref[1:2, :] * n1 + w_ref[0:1, :] * n2)[:tr]
        o_ref[:, :D_MODEL] = (dzv * u3[HALO:]).astype(o_ref.dtype)
        o_ref[:, D_MODEL:2 * D_MODEL] = (du2 * uv).astype(o_ref.dtype)
        o_ref[:, 2 * D_MODEL:] = (du2 * gcv).astype(o_ref.dtype)
        d3 = du3[:tr]
        taps = [jnp.sum(d3 * t[HALO:], axis=0, keepdims=True) for t in (s2, s1, u2)]

        @pl.when(r == 0)
        def _():
            for kk in range(3):
                dw_ref[kk:kk + 1, :] = taps[kk]

        @pl.when(r != 0)
        def _():
            for kk in range(3):
                dw_ref[kk:kk + 1, :] += taps[kk]

    return pl.pallas_call(
        body, name="conv_bwd", grid=(nr,),
        in_specs=[main(0), nxt(0), main(0), nxt(0), main(1), main(2), prev(1), prev(2), _resident((3, D_MODEL))],
        out_specs=[pl.BlockSpec((tr, 3 * D_MODEL), lambda r: (r, 0)), _resident((3, D_MODEL))],
        out_shape=[jax.ShapeDtypeStruct((S, 3 * D_MODEL), MXU_DTYPE), jax.ShapeDtypeStruct((3, D_MODEL), F32)],
        compiler_params=_params(1),
    )(dz, dz, bcu, bcu, bcu, bcu, bcu, bcu, cw)


def adamw(parts, w, m, v, *, name):
    R, C = w.shape
    tr = R
    while tr * C * 4 > (1 << 20) and tr % 32 == 0:
        tr //= 2

    def body(p_ref, w_ref, m_ref, v_ref, g_ref, d_ref, mo_ref, vo_ref):
        g = p_ref[0].astype(F32)
        for d in range(1, N_DEV):
            g = g + p_ref[d].astype(F32)
        m_new = ADAM_B1 * m_ref[...] + (1.0 - ADAM_B1) * g
        v_new = ADAM_B2 * v_ref[...] + (1.0 - ADAM_B2) * (g * g)
        m_hat = m_new / (1.0 - ADAM_B1 ** ADAM_STEP)
        v_hat = v_new / (1.0 - ADAM_B2 ** ADAM_STEP)
        g_ref[...] = g
        d_ref[...] = -ADAM_LR * (m_hat / (jnp.sqrt(v_hat) + ADAM_EPS) + ADAM_WD * w_ref[...])
        mo_ref[...] = m_new
        vo_ref[...] = v_new

    blk = pl.BlockSpec((tr, C), lambda i: (i, 0))
    return pl.pallas_call(
        body, name=name, grid=(R // tr,),
        in_specs=[pl.BlockSpec((N_DEV, tr, C), lambda i: (0, i, 0)), blk, blk, blk],
        out_specs=[blk, blk, blk, blk],
        out_shape=[jax.ShapeDtypeStruct((R, C), F32)] * 4,
        compiler_params=_params(1),
    )(parts, w, m, v)


def _mesh_place():
    x, y, c = (lax.axis_index(n) for n in MESH_AXES)
    return x, y, c, 4 * x + 2 * y + c


def _peer(x, y, c, d):
    px = 1 - x if d & 4 else x
    py = 1 - y if d & 2 else y
    pc = 1 - c if d & 1 else c
    return (px, py, pc), 4 * px + 2 * py + pc


def gather_now(srcs, *, name):
    n = len(srcs)
    any_spec = pl.BlockSpec(memory_space=pl.ANY)

    def body(*refs):
        ins, outs, token = refs[:n], refs[n:2 * n], refs[2 * n]
        send_sems, recv_sems, local_sems = refs[2 * n + 1:]
        token[...] = jnp.zeros_like(token)
        x, y, c, me = _mesh_place()
        for a in range(n):
            pltpu.make_async_copy(ins[a], outs[a].at[me], local_sems.at[a]).start()
            for d in range(1, N_DEV):
                pltpu.make_async_remote_copy(
                    src_ref=ins[a], dst_ref=outs[a].at[me], send_sem=send_sems.at[a], recv_sem=recv_sems.at[a],
                    device_id=_peer(x, y, c, d)[0], device_id_type=pl.DeviceIdType.MESH).start()
        for a in range(n):
            pltpu.make_async_copy(ins[a], outs[a].at[me], local_sems.at[a]).wait()
            seven = outs[a].at[pl.ds(0, N_DEV - 1)]
            drain = pltpu.make_async_remote_copy(
                src_ref=seven, dst_ref=seven, send_sem=send_sems.at[a], recv_sem=recv_sems.at[a],
                device_id=(x, y, c), device_id_type=pl.DeviceIdType.MESH)
            drain.wait_send()
            drain.wait_recv()

    out = pl.pallas_call(
        body, name=name,
        in_specs=[any_spec] * n, out_specs=[any_spec] * n + [pl.BlockSpec(memory_space=pltpu.VMEM)],
        out_shape=[jax.ShapeDtypeStruct((N_DEV,) + s.shape, s.dtype) for s in srcs] + [jax.ShapeDtypeStruct((8, 128), F32)],
        scratch_shapes=[pltpu.SemaphoreType.DMA((n,)), pltpu.SemaphoreType.DMA((n,)), pltpu.SemaphoreType.DMA((n,))],
    )(*srcs)
    return out[:n], out[n]


_ANY = pl.BlockSpec(memory_space=pl.ANY)
_HBM = pl.BlockSpec(memory_space=pltpu.HBM)
_SEM = pl.BlockSpec(memory_space=pltpu.SEMAPHORE)


def _in_hbm(arrays):
    return [pltpu.with_memory_space_constraint(a, pltpu.HBM) for a in arrays]


def exchange_start(srcs, lands, slots, *, name):
    n, m = len(srcs), len(lands)

    def body(*refs):
        ins, zones = refs[:n], refs[n:n + m]
        send_sems, recv_sems, token = refs[n + m], refs[n + m + 1], refs[-1]
        x, y, c, me = _mesh_place()
        for a in range(n):
            for d in range(1, N_DEV):
                peer, peer_lin = _peer(x, y, c, d)
                src = ins[a] if slots is None else ins[a].at[peer_lin]
                dst = zones[a].at[me] if slots is None else zones[slots[a][0]].at[me, slots[a][1]]
                pltpu.make_async_remote_copy(
                    src_ref=src, dst_ref=dst, send_sem=send_sems.at[a], recv_sem=recv_sems.at[a],
                    device_id=peer, device_id_type=pl.DeviceIdType.MESH).start()
        token[...] = jnp.zeros_like(token)

    both = list(srcs) + list(lands)
    out = pl.pallas_call(
        body, name=name,
        in_specs=[_HBM] * (n + m),
        out_specs=[_SEM, _SEM] + [_HBM] * (n + m) + [pl.BlockSpec(memory_space=pltpu.VMEM)],
        out_shape=[pltpu.SemaphoreType.DMA((n,)), pltpu.SemaphoreType.DMA((n,))]
        + [pltpu.HBM(a.shape, a.dtype) for a in both] + [jax.ShapeDtypeStruct((8, 128), F32)],
        input_output_aliases={i: 2 + i for i in range(n + m)},
        compiler_params=pltpu.CompilerParams(has_side_effects=pltpu.SideEffectType.DATAFLOW_SIDE_EFFECTING),
    )(*_in_hbm(both))
    return out[0], out[1], out[2:2 + n], out[2 + n:2 + n + m], out[-1]


def exchange_wait(send_sems, recv_sems, srcs, lands, slots, after, *, name):
    n, m = len(srcs), len(lands)

    def body(*refs):
        ins, zones = refs[:n], refs[n:n + m]
        send_ref, recv_ref = refs[n + m], refs[n + m + 1]
        x, y, c, _ = _mesh_place()
        for a in range(n):
            seven = (zones[a] if slots is None else ins[a]).at[pl.ds(0, N_DEV - 1)]
            drain = pltpu.make_async_remote_copy(
                src_ref=seven, dst_ref=seven, send_sem=send_ref.at[a], recv_sem=recv_ref.at[a],
                device_id=(x, y, c), device_id_type=pl.DeviceIdType.MESH)
            drain.wait_send()
            drain.wait_recv()

    both = list(srcs) + list(lands)
    out = pl.pallas_call(
        body, name=name,
        in_specs=[_HBM] * (n + m) + [_SEM, _SEM, _ANY],
        out_specs=[_HBM] * (n + m),
        out_shape=[pltpu.HBM(a.shape, a.dtype) for a in both],
        input_output_aliases={i: i for i in range(n + m)},
        compiler_params=pltpu.CompilerParams(has_side_effects=pltpu.SideEffectType.DATAFLOW_SIDE_EFFECTING),
    )(*both, send_sems, recv_sems, after)
    return out[:n], out[n:]


def scatter_finish(remote, lands, vec, *, name):
    n, m = len(remote), len(lands)

    def body(*refs):
        ins, vec_ref, zones_in = refs[:n], refs[n], refs[n + 1:n + 1 + m]
        vec_out = refs[n + 1 + 2 * m]
        send_sems, recv_sems, local_sems = refs[n + 2 + 2 * m:]
        x, y, c, me = _mesh_place()

        def ends(a, j):
            if a == n:
                return vec_ref, vec_out.at[me]
            return ins[a].at[j], zones_in[remote[a][1]].at[me, remote[a][2]]

        for a in range(n + 1):
            pltpu.make_async_copy(*ends(a, me), local_sems.at[a]).start()
            for d in range(1, N_DEV):
                peer, peer_lin = _peer(x, y, c, d)
                src, dst = ends(a, peer_lin)
                pltpu.make_async_remote_copy(
                    src_ref=src, dst_ref=dst, send_sem=send_sems.at[a], recv_sem=recv_sems.at[a],
                    device_id=peer, device_id_type=pl.DeviceIdType.MESH).start()
        for a in range(n + 1):
            pltpu.make_async_copy(*ends(a, me), local_sems.at[a]).wait()
            seven = (vec_out if a == n else ins[a]).at[pl.ds(0, N_DEV - 1)]
            drain = pltpu.make_async_remote_copy(
                src_ref=seven, dst_ref=seven, send_sem=send_sems.at[a], recv_sem=recv_sems.at[a],
                device_id=(x, y, c), device_id_type=pl.DeviceIdType.MESH)
            drain.wait_send()
            drain.wait_recv()

    out = pl.pallas_call(
        body, name=name,
        in_specs=[_ANY] * (n + 1 + m), out_specs=[_ANY] * (m + 1),
        out_shape=[jax.ShapeDtypeStruct(z.shape, z.dtype) for z in lands]
        + [jax.ShapeDtypeStruct((N_DEV,) + vec.shape, vec.dtype)],
        input_output_aliases={n + 1 + i: i for i in range(m)},
        scratch_shapes=[pltpu.SemaphoreType.DMA((n + 1,))] * 3,
    )(*[e[0] for e in remote], vec, *lands)
    return out[:m], out[m]


def _rope_tables(pos):
    inv_freq = ROPE_THETA ** (-jnp.arange(0, ROPE, 2, dtype=F32) / ROPE)
    ang = pos.astype(F32)[:, None] * inv_freq
    cos, sin = jnp.cos(ang), jnp.sin(ang)
    pad = jnp.zeros((pos.shape[0], 128 - ROPE), F32)
    return jnp.concatenate([cos, cos, pad + 1.0], axis=1), jnp.concatenate([-sin, sin, pad], axis=1)


def _pad_last(w, n):
    return jnp.pad(w, [(0, 0)] * (w.ndim - 1) + [(0, n - w.shape[-1])])


def kernel(x, positions, g_mix, g_mlp, attn_w_down, attn_g_q_a, attn_g_kv_a, attn_w_uq, attn_w_ukv, attn_g_qnorm, attn_g_knorm, attn_w_o, conv_w_in, conv_w, conv_w_out, mlp_w1, mlp_w2, loss_target, m_g_mix, m_g_mlp, m_attn_w_down, m_attn_g_q_a, m_attn_g_kv_a, m_attn_w_uq, m_attn_w_ukv, m_attn_g_qnorm, m_attn_g_knorm, m_attn_w_o, m_conv_w_in, m_conv_w, m_conv_w_out, m_mlp_w1, m_mlp_w2, v_g_mix, v_g_mlp, v_attn_w_down, v_attn_g_q_a, v_attn_g_kv_a, v_attn_w_uq, v_attn_w_ukv, v_attn_g_qnorm, v_attn_g_knorm, v_attn_w_o, v_conv_w_in, v_conv_w, v_conv_w_out, v_mlp_w1, v_mlp_w2):
    weights = dict(g_mix=g_mix, g_mlp=g_mlp, attn_w_down=attn_w_down, attn_g_q_a=attn_g_q_a, attn_g_kv_a=attn_g_kv_a,
                   attn_w_uq=attn_w_uq, attn_w_ukv=attn_w_ukv, attn_g_qnorm=attn_g_qnorm, attn_g_knorm=attn_g_knorm,
                   attn_w_o=attn_w_o, conv_w_in=conv_w_in, conv_w=conv_w, conv_w_out=conv_w_out, mlp_w1=mlp_w1, mlp_w2=mlp_w2)
    mom1 = dict(g_mix=m_g_mix, g_mlp=m_g_mlp, attn_w_down=m_attn_w_down, attn_g_q_a=m_attn_g_q_a, attn_g_kv_a=m_attn_g_kv_a,
                attn_w_uq=m_attn_w_uq, attn_w_ukv=m_attn_w_ukv, attn_g_qnorm=m_attn_g_qnorm, attn_g_knorm=m_attn_g_knorm,
                attn_w_o=m_attn_w_o, conv_w_in=m_conv_w_in, conv_w=m_conv_w, conv_w_out=m_conv_w_out, mlp_w1=m_mlp_w1, mlp_w2=m_mlp_w2)
    mom2 = dict(g_mix=v_g_mix, g_mlp=v_g_mlp, attn_w_down=v_attn_w_down, attn_g_q_a=v_attn_g_q_a, attn_g_kv_a=v_attn_g_kv_a,
                attn_w_uq=v_attn_w_uq, attn_w_ukv=v_attn_w_ukv, attn_g_qnorm=v_attn_g_qnorm, attn_g_knorm=v_attn_g_knorm,
                attn_w_o=v_attn_w_o, conv_w_in=v_conv_w_in, conv_w=v_conv_w, conv_w_out=v_conv_w_out, mlp_w1=v_mlp_w1, mlp_w2=v_mlp_w2)
    big = ["attn_w_down", "attn_w_uq", "attn_w_ukv", "attn_w_o", "conv_w_in", "conv_w", "conv_w_out", "mlp_w1", "mlp_w2"]
    small = ["g_mix", "g_mlp", "attn_g_q_a", "attn_g_kv_a", "attn_g_qnorm", "attn_g_knorm"]
    order = ["g_mix", "g_mlp", "attn_w_down", "attn_g_q_a", "attn_g_kv_a", "attn_w_uq", "attn_w_ukv", "attn_g_qnorm",
             "attn_g_knorm", "attn_w_o", "conv_w_in", "conv_w", "conv_w_out", "mlp_w1", "mlp_w2"]

    xs = x[0]
    pos = positions[0]
    target = loss_target[0]
    S = xs.shape[0]
    depth = g_mix.shape[0]
    cos_t, sin_t = _rope_tables(pos)
    pos_col, pos_row = pos.reshape(S, 1), pos.reshape(1, S)

    keys, shards = [], []
    for name in big:
        for l in range(weights[name].shape[0]):
            keys.append((name, l))
            shards.append(weights[name][l] if name == "conv_w" else weights[name][l].astype(WIRE_DTYPE))
    first = [j for j, (name, l) in enumerate(keys) if l == 0 and name.startswith("attn")]
    second = [j for j, (name, l) in enumerate(keys) if l == 0 and name.startswith("mlp")]
    later = [j for j in range(len(keys)) if j not in first + second]
    me = 4 * lax.axis_index("x") + 2 * lax.axis_index("y") + lax.axis_index("c")

    def zones_with_own(js, token):
        return [lax.dynamic_update_slice(lax.empty((N_DEV,) + shards[j].shape, shards[j].dtype),
                                         (shards[j] + token[0, 0].astype(shards[j].dtype))[None],
                                         (me,) + (0,) * shards[j].ndim) for j in js]

    arrived, token = gather_now([shards[j] for j in first], name="gather_first")
    full = dict(zip([keys[j] for j in first], arrived))
    g1 = exchange_start([shards[j] for j in second], zones_with_own(second, token), None, name="gather_mlp0_start")
    g2 = exchange_start([shards[j] for j in later], zones_with_own(later, g1[4]), None, name="gather_rest_start")
    g_mix_0 = g_mix[0] + g2[4][0, 0]

    def rows(name, l):
        g = full[(name, l)]
        return g.reshape(g.shape[0] * g.shape[1], g.shape[2])

    saved = []
    for i in range(depth):
        l = i // 2
        rec = {"x0": xs}
        if i == 1:
            arrived = exchange_wait(*g2[:4], None, xs, name="gather_rest_wait")[1]
            full.update(zip([keys[j] for j in later], arrived))
        if i % 2 == 0:
            wd3 = _pad_last(rows("attn_w_down", l), DOWN_PAD)[None]
            wuq3 = _pad_last(full[("attn_w_uq", l)], QK_PAD)
            wukv3 = full[("attn_w_ukv", l)]
            gqn = _pad_last(attn_g_qnorm[l][None], QK_PAD)
            gkn = _pad_last(attn_g_knorm[l][None], QK_PAD)
            gqa, gkva = attn_g_q_a[l][None], attn_g_kv_a[l][None]
            h, a = norm_matmul(xs, g_mix_0 if i == 0 else g_mix[i], wd3, name="attn_down")
            q, k, v, cq, ckv = mla_pre_fwd(a, gqa, gkva, wuq3, wukv3, gqn, gkn, cos_t, sin_t)
            o, lse = attn_fwd(q, k, v, pos_col, pos_row)
            x1 = matmul_residual(o, rows("attn_w_o", l), xs, name="attn_out")
            rec.update(h=h, a=a, q=q, k=k, v=v, cq=cq, ckv=ckv, o=o, lse=lse, wd3=wd3, wuq3=wuq3, wukv3=wukv3,
                       gqn=gqn, gkn=gkn, gqa=gqa, gkva=gkva)
        else:
            cw = full[("conv_w", l)].transpose(1, 0, 2).reshape(3, D_MODEL)
            h, bcu = norm_matmul(xs, g_mix[i], full[("conv_w_in", l)], name="conv_in", out_dtype=MXU_DTYPE)
            z = conv_fwd(bcu, cw)
            x1 = matmul_residual(z, rows("conv_w_out", l), xs, name="conv_out")
            rec.update(h=h, bcu=bcu, z=z, cw=cw)
        if i == 0:
            arrived = exchange_wait(*g1[:4], None, x1, name="gather_mlp0_wait")[1]
            full.update(zip([keys[j] for j in second], arrived))
        h2, act, slope = norm_matmul(x1, g_mlp[i], full[("mlp_w1", i)], name="mlp_up", mlp=True)
        if i < depth - 1:
            xs = matmul_residual(act, rows("mlp_w2", i), x1, name="mlp_down")
        else:
            sq, dx, dxb = matmul_residual(act, rows("mlp_w2", i), x1, name="mlp_down_loss", target=target)
        rec.update(x1=x1, h2=h2, act=act, slope=slope)
        saved.append(rec)

    loss = lax.psum(sq[0, 0] * (0.5 / D_MODEL), MESH_AXES)

    grads = {name: [None] * weights[name].shape[0] for name in order}
    token = None
    for i in reversed(range(depth)):
        l = i // 2
        rec = saved[i]
        grads["mlp_w2"][i] = mm_tn(rec["act"], dxb, name="mlp_down_dw", G=1, out_dtype=WIRE_DTYPE).reshape(N_DEV, -1, D_MODEL)
        du = matmul_nt(dxb, rows("mlp_w2", i)[None], name="mlp_down_dx", epi="mlp_du", u=rec["slope"])
        grads["mlp_w1"][i] = mm_tn(rec["h2"], du, name="mlp_up_dw", G=N_DEV, out_dtype=WIRE_DTYPE, a_transposed=True)
        dx1, dx1b, dg = matmul_nt(du, full[("mlp_w1", i)], name="mlp_up_dx", epi="rms_bwd", x=rec["x1"], g=g_mlp[i], dx=dx)
        grads["g_mlp"][i] = dg[0]
        if i == 0:
            flying = [keys[j] for j in second + later]
            srcs = [grads[name][l_] for name, l_ in flying]
            slots = [(big.index(name), l_) for name, l_ in flying]
            zones = [lax.empty((N_DEV, weights[name].shape[0]) + grads[name][-1].shape[1:], grads[name][-1].dtype)
                     for name in big]
            for src, (k, l_) in zip(srcs, slots):
                own = lax.dynamic_index_in_dim(src, me, 0, keepdims=True)[None]
                zones[k] = lax.dynamic_update_slice(zones[k], own, (me, l_) + (0,) * (src.ndim - 1))
            s_send, s_recv, s_srcs, s_zones, token = exchange_start(srcs, zones, slots, name="scatter_rest_start")
        if i % 2 == 0:
            grads["attn_w_o"][l] = mm_tn(rec["o"], dx1b, name="attn_out_dw", G=1, out_dtype=WIRE_DTYPE,
                                         after=token).reshape(N_DEV, -1, D_MODEL)
            do, delta = matmul_nt(dx1b, rows("attn_w_o", l)[None], name="attn_out_dx", u=rec["o"])
            dq, dk, dv = attn_bwd(rec["q"], rec["k"], rec["v"], do, rec["lse"], delta, pos_col, pos_row)
            da, dwuq, dwukv, dgqn, dgkn, dgqa, dgkva = mla_pre_bwd(
                dq, dk, dv, rec["a"], rec["cq"], rec["ckv"], rec["gqa"], rec["gkva"], rec["wuq3"], rec["wukv3"],
                rec["gqn"], rec["gkn"], cos_t, sin_t)
            grads["attn_w_uq"][l] = dwuq[:, :, :QK_DIM].astype(WIRE_DTYPE)
            grads["attn_w_ukv"][l] = dwukv.astype(WIRE_DTYPE)
            grads["attn_g_qnorm"][l] = dgqn[0, :QK_DIM]
            grads["attn_g_knorm"][l] = dgkn[0, :QK_DIM]
            grads["attn_g_q_a"][l] = dgqa[0]
            grads["attn_g_kv_a"][l] = dgkva[0]
            dwd = mm_tn(rec["h"], da, name="attn_down_dw", G=1, out_dtype=WIRE_DTYPE, a_transposed=True)
            grads["attn_w_down"][l] = dwd[0, :, :DOWN].reshape(N_DEV, -1, DOWN)
            dx, dxb, dg = matmul_nt(da, rec["wd3"], name="attn_down_dx", epi="rms_bwd", x=rec["x0"], g=g_mix[i], dx=dx1)
        else:
            grads["conv_w_out"][l] = mm_tn(rec["z"], dx1b, name="conv_out_dw", G=1, out_dtype=WIRE_DTYPE).reshape(N_DEV, -1, D_MODEL)
            dz = matmul_nt(dx1b, rows("conv_w_out", l)[None], name="conv_out_dx")
            dbcu, dcw = conv_bwd(dz, rec["bcu"], rec["cw"])
            grads["conv_w"][l] = dcw.reshape(3, N_DEV, -1).transpose(1, 0, 2)
            grads["conv_w_in"][l] = mm_tn(rec["h"], dbcu, name="conv_in_dw", G=N_DEV, out_dtype=WIRE_DTYPE, a_transposed=True)
            dx, dxb, dg = matmul_nt(dbcu, full[("conv_w_in", l)], name="conv_in_dx", epi="rms_bwd", x=rec["x0"], g=g_mix[i], dx=dx1)
        grads["g_mix"][i] = dg[0]

    sizes = [weights[name].size for name in small]
    n_small = sum(sizes)
    rows_small = -(-n_small // (8 * 128)) * 8

    def pack(tree):
        flat = jnp.concatenate([jnp.stack(tree[name]).reshape(-1) if isinstance(tree[name], list) else tree[name].reshape(-1)
                                for name in small])
        return jnp.pad(flat, (0, rows_small * 128 - n_small)).reshape(rows_small, 128)

    s_srcs, s_zones = exchange_wait(s_send, s_recv, s_srcs, s_zones, slots, dx, name="scatter_rest_wait")
    remote = [(grads[name][l_], big.index(name), l_) for name, l_ in (keys[j] for j in first)]
    parts, gain_parts = scatter_finish(remote, s_zones, pack(grads), name="scatter_last")

    out = {}
    for name, part in zip(big, parts):
        w = weights[name]
        flat = lambda t: t.reshape(-1, t.shape[-1])
        res = adamw(part.reshape(N_DEV, -1, w.shape[-1]), flat(w), flat(mom1[name]), flat(mom2[name]), name="adamw_" + name)
        out[name] = [r.reshape(w.shape) for r in res]
    res = adamw(gain_parts, pack(weights), pack(mom1), pack(mom2), name="adamw_gains")
    offset = 0
    for name, size in zip(small, sizes):
        out[name] = [r.reshape(-1)[offset:offset + size].reshape(weights[name].shape) for r in res]
        offset += size

    return (loss, dx[None], *[out[n][0] for n in order], *[out[n][1] for n in order],
            *[out[n][2] for n in order], *[out[n][3] for n in order])
```

```python
import jax
import jax.numpy as jnp
import numpy as np
from jax import lax
from jax.experimental import pallas as pl
from jax.experimental.pallas import tpu as pltpu

F32 = jnp.float32
MXU_DTYPE = jnp.bfloat16
WIRE_DTYPE = jnp.bfloat16

D_MODEL = 1024
N_HEADS = 8
NOPE = 128
ROPE = 64
QK_DIM = NOPE + ROPE
QK_PAD = 256
V_DIM = 128
Q_LORA = 256
KV_LORA = 128
DOWN = Q_LORA + KV_LORA + ROPE
DOWN_PAD = 512
ROPE_THETA = 10000.0
EPS = 1e-6
SM_SCALE = QK_DIM ** -0.5
LOG2E = 1.4426950408889634
Q_PRESCALE = SM_SCALE * LOG2E
ADAM_LR, ADAM_B1, ADAM_B2, ADAM_EPS, ADAM_WD, ADAM_STEP = 0.001, 0.9, 0.999, 1e-08, 0.01, 10
N_DEV = 8
MESH_AXES = ("x", "y", "c")

TM = 512
TM_WIDE = 1024
TILE_BUDGET = 32 << 20
TM_TOKENS_TN = 2048
TQ = 512
HEADS_FWD = 8
TQ_BWD = 2048
BWD_CHUNK = 256
TROW = 256
HALO = 16
VMEM_LIMIT = 48 << 20

NN = (((1,), (0,)), ((), ()))
NT = (((1,), (1,)), ((), ()))
TN = (((0,), (0,)), ((), ()))


def _dot(a, b, dims=NN):
    return lax.dot_general(a.astype(MXU_DTYPE), b.astype(MXU_DTYPE), dims, preferred_element_type=F32)


def _params(n_axes):
    return pltpu.CompilerParams(dimension_semantics=("arbitrary",) * n_axes, vmem_limit_bytes=VMEM_LIMIT)


def _rms(xv, n):
    r = lax.rsqrt(jnp.sum(xv * xv, axis=-1, keepdims=True) / n + EPS)
    return xv * r, r


def _rms_bwd(dy, xhat, r, g, n):
    dg = jnp.sum(dy * xhat, axis=0, keepdims=True)
    dxh = dy * g
    dx = r * (dxh - xhat * (jnp.sum(dxh * xhat, axis=-1, keepdims=True) / n))
    return dx, dg


def _swap_halves(t):
    lane = lax.broadcasted_iota(jnp.int32, t.shape, 1)
    return jnp.where(lane < ROPE // 2, pltpu.roll(t, 128 - ROPE // 2, 1), pltpu.roll(t, ROPE // 2, 1))


def _rope(t, cos_t, sin_t):
    return t * cos_t + _swap_halves(t) * sin_t


def _rope_bwd(dout, cos_t, sin_t):
    return dout * cos_t + _swap_halves(dout * sin_t)


def _resident(shape):
    return pl.BlockSpec(shape, lambda i: (0,) * len(shape))


def _token_tile(S, row_bytes, resident_bytes):
    wide = min(TM_WIDE, S)
    return wide if 2 * (wide * row_bytes + resident_bytes) <= TILE_BUDGET else min(TM, S)


def norm_matmul(x, g, w3, *, name, mlp=False, out_dtype=F32):
    if mlp:
        out_dtype = MXU_DTYPE
    S, D = x.shape
    G, _, Nb = w3.shape
    N = G * Nb
    n_out = 2 if mlp else 1
    tm = _token_tile(S, D * 4 + D * 2 + n_out * N * jnp.dtype(out_dtype).itemsize, w3.size * w3.dtype.itemsize)

    def body(x_ref, g_ref, w_ref, h_ref, o_ref, *slope_ref):
        xv = x_ref[...]
        r = lax.rsqrt(jnp.mean(xv * xv, axis=-1, keepdims=True) + EPS)
        h = (xv * r * g_ref[...]).astype(h_ref.dtype)
        h_ref[...] = h.T
        for gi in range(G):
            cols = slice(gi * Nb, (gi + 1) * Nb)
            acc = _dot(h, w_ref[gi])
            if mlp:
                acc = jnp.maximum(acc, 0.0)
                slope_ref[0][:, cols] = (2.0 * acc).astype(out_dtype)
                acc = jnp.square(acc)
            o_ref[:, cols] = acc.astype(o_ref.dtype)

    rows = lambda w: pl.BlockSpec((tm, w), lambda i: (i, 0))
    return pl.pallas_call(
        body, name=name, grid=(S // tm,),
        in_specs=[rows(D), _resident((1, D)), _resident((G, D, Nb))],
        out_specs=[pl.BlockSpec((D, tm), lambda i: (0, i))] + [rows(N)] * n_out,
        out_shape=[jax.ShapeDtypeStruct((D, S), MXU_DTYPE)] + [jax.ShapeDtypeStruct((S, N), out_dtype)] * n_out,
        compiler_params=_params(1),
    )(x, g.reshape(1, D), w3)


def matmul_residual(a, w, res, *, name, target=None):
    S, K = a.shape
    _, N = w.shape
    tm = _token_tile(S, K * a.dtype.itemsize + 2 * N * 4 + (0 if target is None else N * 6), w.size * w.dtype.itemsize)

    def body(a_ref, w_ref, r_ref, *rest):
        y = r_ref[...] + _dot(a_ref[...], w_ref[...])
        if target is None:
            rest[0][...] = y
            return
        t_ref, l_ref, dy_ref, dyb_ref = rest
        err = y - t_ref[...]
        dy = err / N
        dy_ref[...] = dy
        dyb_ref[...] = dy.astype(dyb_ref.dtype)
        part = jnp.full((1, 128), jnp.sum(err * err), F32)

        @pl.when(pl.program_id(0) == 0)
        def _():
            l_ref[...] = part

        @pl.when(pl.program_id(0) != 0)
        def _():
            l_ref[...] += part

    rows = lambda w_: pl.BlockSpec((tm, w_), lambda i: (i, 0))
    in_specs, args = [rows(K), _resident((K, N)), rows(N)], [a, w, res]
    if target is None:
        out_specs, out_shape = rows(N), jax.ShapeDtypeStruct((S, N), F32)
    else:
        in_specs.append(rows(N))
        args.append(target)
        out_specs = [_resident((1, 128)), rows(N), rows(N)]
        out_shape = [jax.ShapeDtypeStruct((1, 128), F32), jax.ShapeDtypeStruct((S, N), F32),
                     jax.ShapeDtypeStruct((S, N), MXU_DTYPE)]
    return pl.pallas_call(
        body, name=name, grid=(S // tm,),
        in_specs=in_specs, out_specs=out_specs, out_shape=out_shape,
        compiler_params=_params(1),
    )(*args)


def matmul_nt(a, w3, *, name, epi="plain", u=None, x=None, g=None, dx=None):
    S, N = a.shape
    G, Ko, Nb = w3.shape
    assert N == G * Nb
    with_delta = epi == "plain" and u is not None
    row_bytes = N * a.dtype.itemsize + Ko * ({"plain": 4, "mlp_du": 4, "rms_bwd": 14}[epi] + (4 if with_delta else 0))
    tm = _token_tile(S, row_bytes, w3.size * w3.dtype.itemsize)
    tko = min(Ko, 512)

    def body(a_ref, w_ref, *rest):
        if epi == "mlp_du":
            u_ref, o_ref = rest
            av = a_ref[...].astype(MXU_DTYPE)
            for j in range(Ko // tko):
                cols = slice(j * tko, (j + 1) * tko)
                da = _dot(av, w_ref[0, cols, :], NT)
                o_ref[:, cols] = (da * u_ref[:, cols].astype(F32)).astype(o_ref.dtype)
            return
        acc = _dot(a_ref[:, :Nb], w_ref[0], NT)
        for gi in range(1, G):
            acc = acc + _dot(a_ref[:, gi * Nb:(gi + 1) * Nb], w_ref[gi], NT)
        if epi == "plain":
            if with_delta:
                u_ref, o_ref, d_ref = rest
                for h in range(N_HEADS):
                    cols = slice(h * V_DIM, (h + 1) * V_DIM)
                    d_ref[h] = jnp.sum((u_ref[:, cols] * acc[:, cols]).T, axis=0, keepdims=True)
            else:
                o_ref = rest[0]
            o_ref[...] = acc
        else:
            x_ref, g_ref, dx_ref, o_ref, ob_ref, dg_ref = rest
            xhat, r = _rms(x_ref[...], Ko)
            dxb, dg = _rms_bwd(acc, xhat, r, g_ref[...], Ko)
            dx_new = dx_ref[...] + dxb
            o_ref[...] = dx_new
            ob_ref[...] = dx_new.astype(ob_ref.dtype)

            @pl.when(pl.program_id(0) == 0)
            def _():
                dg_ref[...] = dg

            @pl.when(pl.program_id(0) != 0)
            def _():
                dg_ref[...] += dg

    rows = lambda w_: pl.BlockSpec((tm, w_), lambda i: (i, 0))
    in_specs = [rows(N), _resident((G, Ko, Nb))]
    args = [a, w3]
    if with_delta:
        in_specs.append(rows(Ko))
        args.append(u)
        out_shape = [jax.ShapeDtypeStruct((S, Ko), F32), jax.ShapeDtypeStruct((N_HEADS, 1, S), F32)]
        out_specs = [rows(Ko), pl.BlockSpec((N_HEADS, 1, tm), lambda i: (0, 0, i))]
    elif epi == "plain":
        out_shape, out_specs = jax.ShapeDtypeStruct((S, Ko), F32), rows(Ko)
    elif epi == "mlp_du":
        in_specs.append(rows(Ko))
        args.append(u)
        out_shape, out_specs = jax.ShapeDtypeStruct((S, Ko), MXU_DTYPE), rows(Ko)
    else:
        in_specs += [rows(Ko), _resident((1, Ko)), rows(Ko)]
        args += [x, g.reshape(1, Ko), dx]
        out_shape = [jax.ShapeDtypeStruct((S, Ko), F32), jax.ShapeDtypeStruct((S, Ko), MXU_DTYPE),
                     jax.ShapeDtypeStruct((1, Ko), F32)]
        out_specs = [rows(Ko), rows(Ko), _resident((1, Ko))]
    return pl.pallas_call(
        body, name=name, grid=(S // tm,),
        in_specs=in_specs, out_specs=out_specs, out_shape=out_shape,
        compiler_params=_params(1),
    )(*args)


def mm_tn(a, b, *, name, G, out_dtype, after=None, a_transposed=False):
    order = [] if after is None else [after]
    Ka, S = a.shape if a_transposed else a.shape[::-1]
    _, N = b.shape
    Nb = N // G
    assert Nb <= 1024
    tm = min(TM_TOKENS_TN if b.dtype.itemsize == 2 else TM_TOKENS_TN // 2, S)
    tka = min(Ka, 1024)
    gb = 2 if G % 2 == 0 and Nb <= 512 else 1
    ns = S // tm

    def body(a_ref, b_ref, *rest):
        o_ref, acc = rest[-2:]
        s = pl.program_id(2)

        @pl.when(s == 0)
        def _():
            acc[...] = jnp.zeros_like(acc)

        av = a_ref[...]
        for gi in range(gb):
            acc[gi] += _dot(av, b_ref[:, gi * Nb:(gi + 1) * Nb], NN if a_transposed else TN)

        @pl.when(s == ns - 1)
        def _():
            o_ref[...] = acc[...].astype(o_ref.dtype)

    a_spec = pl.BlockSpec((tka, tm), lambda i, j, s: (i, s)) if a_transposed else pl.BlockSpec((tm, tka), lambda i, j, s: (s, i))
    return pl.pallas_call(
        body, name=name, grid=(Ka // tka, G // gb, ns),
        in_specs=[a_spec,
                  pl.BlockSpec((tm, gb * Nb), lambda i, j, s: (s, j))] + [pl.BlockSpec(memory_space=pl.ANY)] * len(order),
        out_specs=pl.BlockSpec((gb, tka, Nb), lambda i, j, s: (j, i, 0)),
        out_shape=jax.ShapeDtypeStruct((G, Ka, Nb), out_dtype),
        scratch_shapes=[pltpu.VMEM((gb, tka, Nb), F32)],
        compiler_params=_params(3),
    )(a, b, *order)


def mla_pre_fwd(a, gqa, gkva, wuq3, wukv3, gqn, gkn, cos_t, sin_t):
    S = a.shape[0]
    tm = min(TM, S)
    H = N_HEADS

    def body(a_ref, gqa_ref, gkva_ref, wuq_ref, wukv_ref, gqn_ref, gkn_ref, cos_ref, sin_ref,
             q_ref, k_ref, v_ref, cq_ref, ckv_ref):
        av = a_ref[...]
        cq = (_rms(av[:, :Q_LORA], Q_LORA)[0] * gqa_ref[...]).astype(cq_ref.dtype)
        ckv = (_rms(av[:, Q_LORA:Q_LORA + KV_LORA], KV_LORA)[0] * gkva_ref[...]).astype(ckv_ref.dtype)
        cq_ref[...] = cq
        ckv_ref[...] = ckv
        kpe = av[:, Q_LORA + KV_LORA:]
        cos_v, sin_v = cos_ref[...], sin_ref[...]
        for h in range(H):
            qn = _rms(_dot(cq, wuq_ref[h]), QK_DIM)[0] * gqn_ref[...]
            qr = jnp.concatenate([qn[:, :NOPE], _rope(qn[:, NOPE:], cos_v, sin_v)], axis=1)
            q_ref[h] = (qr * Q_PRESCALE).astype(q_ref.dtype)
            kvp = _dot(ckv, wukv_ref[h])
            kn = _rms(jnp.concatenate([kvp[:, :NOPE], kpe], axis=1), QK_DIM)[0] * gkn_ref[...]
            k_ref[h] = jnp.concatenate([kn[:, :NOPE], _rope(kn[:, NOPE:], cos_v, sin_v)], axis=1).astype(k_ref.dtype)
            v_ref[h] = kvp[:, NOPE:].astype(v_ref.dtype)

    row = lambda w: pl.BlockSpec((tm, w), lambda i: (i, 0))
    heads = lambda w: pl.BlockSpec((H, tm, w), lambda i: (0, i, 0))
    return pl.pallas_call(
        body, name="mla_pre_fwd", grid=(S // tm,),
        in_specs=[row(DOWN_PAD), _resident((1, Q_LORA)), _resident((1, KV_LORA)),
                  _resident((H, Q_LORA, QK_PAD)), _resident((H, KV_LORA, NOPE + V_DIM)),
                  _resident((1, QK_PAD)), _resident((1, QK_PAD)), row(128), row(128)],
        out_specs=[heads(QK_PAD), heads(QK_PAD), heads(V_DIM), row(Q_LORA), row(KV_LORA)],
        out_shape=[jax.ShapeDtypeStruct((H, S, QK_PAD), MXU_DTYPE),
                   jax.ShapeDtypeStruct((H, S, QK_PAD), MXU_DTYPE),
                   jax.ShapeDtypeStruct((H, S, V_DIM), MXU_DTYPE),
                   jax.ShapeDtypeStruct((S, Q_LORA), MXU_DTYPE),
                   jax.ShapeDtypeStruct((S, KV_LORA), MXU_DTYPE)],
        compiler_params=_params(1),
    )(a, gqa, gkva, wuq3, wukv3, gqn, gkn, cos_t, sin_t)


def mla_pre_bwd(dq, dk, dv, a, cq, ckv, gqa, gkva, wuq3, wukv3, gqn, gkn, cos_t, sin_t):
    S = a.shape[0]
    tm = min(TM, S)
    H = N_HEADS

    def body(dq_ref, dk_ref, dv_ref, a_ref, cq_ref, ckv_ref, gqa_ref, gkva_ref, wuq_ref, wukv_ref, gqn_ref, gkn_ref,
             cos_ref, sin_ref, da_ref, dwuq_ref, dwukv_ref, dgqn_ref, dgkn_ref, dgqa_ref, dgkva_ref):
        @pl.when(pl.program_id(0) == 0)
        def _():
            for ref in (dwuq_ref, dwukv_ref, dgqn_ref, dgkn_ref, dgqa_ref, dgkva_ref):
                ref[...] = jnp.zeros_like(ref)

        av = a_ref[...]
        kpe = av[:, Q_LORA + KV_LORA:]
        cos_v, sin_v = cos_ref[...], sin_ref[...]
        cqv, ckvv = cq_ref[...], ckv_ref[...]
        dcq = jnp.zeros((tm, Q_LORA), F32)
        dckv = jnp.zeros((tm, KV_LORA), F32)
        dkpe = jnp.zeros((tm, 128), F32)
        dgqn = jnp.zeros((1, QK_PAD), F32)
        dgkn = jnp.zeros((1, QK_PAD), F32)
        up = lambda h: (_dot(cqv, wuq_ref[h]), _dot(ckvv, wukv_ref[h]))
        nxt = up(0)
        for h in range(H):
            wuq, wukv = wuq_ref[h], wukv_ref[h]
            qp, kvp = nxt
            if h + 1 < H:
                nxt = up(h + 1)
            qhat, rq = _rms(qp, QK_DIM)
            dqr = dq_ref[h] * SM_SCALE
            dqn = jnp.concatenate([dqr[:, :NOPE], _rope_bwd(dqr[:, NOPE:], cos_v, sin_v)], axis=1)
            dqp, dg = _rms_bwd(dqn, qhat, rq, gqn_ref[...], QK_DIM)
            dgqn = dgqn + dg
            dqp = dqp.astype(MXU_DTYPE)
            dwuq_ref[h] += _dot(cqv, dqp, TN)
            dcq = dcq + _dot(dqp, wuq, NT)
            khat, rk = _rms(jnp.concatenate([kvp[:, :NOPE], kpe], axis=1), QK_DIM)
            dkr = dk_ref[h] * (1.0 / LOG2E)
            dkn = jnp.concatenate([dkr[:, :NOPE], _rope_bwd(dkr[:, NOPE:], cos_v, sin_v)], axis=1)
            dkk, dg = _rms_bwd(dkn, khat, rk, gkn_ref[...], QK_DIM)
            dgkn = dgkn + dg
            dkpe = dkpe + dkk[:, NOPE:]
            dkvp = jnp.concatenate([dkk[:, :NOPE], dv_ref[h]], axis=1).astype(MXU_DTYPE)
            dwukv_ref[h] += _dot(ckvv, dkvp, TN)
            dckv = dckv + _dot(dkvp, wukv, NT)
        dgqn_ref[...] += dgqn
        dgkn_ref[...] += dgkn
        ahat, r = _rms(av[:, :Q_LORA], Q_LORA)
        daq, dg = _rms_bwd(dcq, ahat, r, gqa_ref[...], Q_LORA)
        dgqa_ref[...] += dg
        ahat, r = _rms(av[:, Q_LORA:Q_LORA + KV_LORA], KV_LORA)
        dakv, dg = _rms_bwd(dckv, ahat, r, gkva_ref[...], KV_LORA)
        dgkva_ref[...] += dg
        da_ref[...] = jnp.concatenate([daq, dakv, dkpe], axis=1)

    row = lambda w: pl.BlockSpec((tm, w), lambda i: (i, 0))
    heads = lambda w: pl.BlockSpec((H, tm, w), lambda i: (0, i, 0))
    return pl.pallas_call(
        body, name="mla_pre_bwd", grid=(S // tm,),
        in_specs=[heads(QK_PAD), heads(QK_PAD), heads(V_DIM), row(DOWN_PAD), row(Q_LORA), row(KV_LORA),
                  _resident((1, Q_LORA)), _resident((1, KV_LORA)),
                  _resident((H, Q_LORA, QK_PAD)), _resident((H, KV_LORA, NOPE + V_DIM)),
                  _resident((1, QK_PAD)), _resident((1, QK_PAD)), row(128), row(128)],
        out_specs=[row(DOWN_PAD), _resident((H, Q_LORA, QK_PAD)), _resident((H, KV_LORA, NOPE + V_DIM)),
                   _resident((1, QK_PAD)), _resident((1, QK_PAD)), _resident((1, Q_LORA)), _resident((1, KV_LORA))],
        out_shape=[jax.ShapeDtypeStruct((S, DOWN_PAD), F32),
                   jax.ShapeDtypeStruct((H, Q_LORA, QK_PAD), F32),
                   jax.ShapeDtypeStruct((H, KV_LORA, NOPE + V_DIM), F32),
                   jax.ShapeDtypeStruct((1, QK_PAD), F32), jax.ShapeDtypeStruct((1, QK_PAD), F32),
                   jax.ShapeDtypeStruct((1, Q_LORA), F32), jax.ShapeDtypeStruct((1, KV_LORA), F32)],
        compiler_params=_params(1),
    )(dq, dk, dv, a, cq, ckv, gqa, gkva, wuq3, wukv3, gqn, gkn, cos_t, sin_t)


def _pair_tables(nb, key_major):
    if key_major:
        pairs = [(qi, kj) for kj in range(nb) for qi in range(kj, nb)]
    else:
        pairs = [(qi, ki) for qi in range(nb) for ki in range(qi + 1)]
    return (jnp.asarray(np.array([p[0] for p in pairs], np.int32)),
            jnp.asarray(np.array([p[1] for p in pairs], np.int32)))


def _scores_t(k, q, pk_col, pq_row, masked):
    s = _dot(k, q, NT)
    return jnp.where(pq_row >= pk_col, s, jnp.finfo(F32).min) if masked else s


def attn_fwd(q, k, v, pos_col, pos_row):
    H, S, _ = q.shape
    t = min(TQ, S)
    nb = S // t
    hb = HEADS_FWD
    qt, kt = _pair_tables(nb, key_major=False)

    def body(qt_ref, kt_ref, q_ref, k_ref, v_ref, pk_ref, pq_ref, o_ref, lse_ref, m_s, l_s, acc):
        step = pl.program_id(1)
        qi, ki = qt_ref[step], kt_ref[step]

        @pl.when(ki == 0)
        def _():
            m_s[...] = jnp.full_like(m_s, -jnp.inf)
            l_s[...] = jnp.zeros_like(l_s)
            acc[...] = jnp.zeros_like(acc)

        def update(masked):
            scores = lambda hh: _scores_t(k_ref[hh], q_ref[hh], pk_ref[...], pq_ref[...], masked)
            def weighted_values(hh, p, alpha):
                acc[hh] = alpha * acc[hh] + _dot(v_ref[hh], p, TN)

            s_next = scores(0)
            pending = None
            for hh in range(hb):
                s = s_next
                if hh + 1 < hb:
                    s_next = scores(hh + 1)
                m_old = m_s[hh]
                m_new = jnp.maximum(m_old, jnp.max(s, axis=0, keepdims=True))
                p = jnp.exp2(s - m_new)
                alpha = jnp.exp2(m_old - m_new)
                l_s[hh] = alpha * l_s[hh] + jnp.sum(p, axis=0, keepdims=True)
                m_s[hh] = m_new
                if pending is not None:
                    weighted_values(*pending)
                pending = (hh, p, alpha)
            weighted_values(*pending)

        @pl.when(ki < qi)
        def _():
            update(False)

        @pl.when(ki == qi)
        def _():
            update(True)
            for hh in range(hb):
                o_ref[:, hh * V_DIM:(hh + 1) * V_DIM] = (acc[hh] / l_s[hh]).T
                lse_ref[hh] = m_s[hh] + jnp.log(l_s[hh]) * LOG2E

    grid_spec = pltpu.PrefetchScalarGridSpec(
        num_scalar_prefetch=2, grid=(H // hb, qt.shape[0]),
        in_specs=[pl.BlockSpec((hb, t, QK_PAD), lambda h, s, qt, kt: (h, qt[s], 0)),
                  pl.BlockSpec((hb, t, QK_PAD), lambda h, s, qt, kt: (h, kt[s], 0)),
                  pl.BlockSpec((hb, t, V_DIM), lambda h, s, qt, kt: (h, kt[s], 0)),
                  pl.BlockSpec((t, 1), lambda h, s, qt, kt: (kt[s], 0)),
                  pl.BlockSpec((1, t), lambda h, s, qt, kt: (0, qt[s]))],
        out_specs=[pl.BlockSpec((t, hb * V_DIM), lambda h, s, qt, kt: (qt[s], h)),
                   pl.BlockSpec((hb, 1, t), lambda h, s, qt, kt: (h, 0, qt[s]))],
        scratch_shapes=[pltpu.VMEM((hb, 1, t), F32), pltpu.VMEM((hb, 1, t), F32), pltpu.VMEM((hb, V_DIM, t), F32)])
    return pl.pallas_call(
        body, name="attn_fwd", grid_spec=grid_spec,
        out_shape=[jax.ShapeDtypeStruct((S, H * V_DIM), F32), jax.ShapeDtypeStruct((H, 1, S), F32)],
        compiler_params=_params(2),
    )(qt, kt, q, k, v, pos_col, pos_row)


def attn_bwd(q, k, v, do, lse, delta, pos_col, pos_row):
    H, S, _ = q.shape
    t = min(TQ_BWD, S)
    nb = S // t
    qt, kt = _pair_tables(nb, key_major=True)
    tc = min(BWD_CHUNK, t)

    def body(qt_ref, kt_ref, q_ref, k_ref, v_ref, do_ref, lse_ref, dl_ref, pk_ref, pq_ref, dq_ref, dk_ref, dv_ref):
        step = pl.program_id(1)
        qi, kj = qt_ref[step], kt_ref[step]

        @pl.when(step == 0)
        def _():
            dq_ref[...] = jnp.zeros_like(dq_ref)

        @pl.when(qi == kj)
        def _():
            dk_ref[...] = jnp.zeros_like(dk_ref)
            dv_ref[...] = jnp.zeros_like(dv_ref)

        def update(masked):
            seen = lambda c: (c + 1) * tc if masked else t

            def first_matmuls(c):
                cols, ke = slice(c * tc, (c + 1) * tc), seen(c)
                qc = q_ref[cols, :]
                doc = do_ref[cols, :].astype(MXU_DTYPE)
                s = _scores_t(k_ref[:ke, :], qc, pk_ref[:ke, :], pq_ref[:, cols], masked)
                return qc, doc, s, _dot(v_ref[:ke, :], doc, NT)

            nxt = first_matmuls(0)
            for c in range(t // tc):
                qc, doc, s, dp = nxt
                if c + 1 < t // tc:
                    nxt = first_matmuls(c + 1)
                cols, ke = slice(c * tc, (c + 1) * tc), seen(c)
                p = jnp.exp2(s - lse_ref[:, cols])
                ds = (p * (dp - dl_ref[:, cols])).astype(MXU_DTYPE)
                dv_ref[:ke, :] += _dot(p, doc)
                dk_ref[:ke, :] += _dot(ds, qc)
                rows = pl.ds(pl.multiple_of(qi * t + c * tc, tc), tc)
                dq_ref[rows, :] += _dot(ds, k_ref[:ke, :], TN)

        @pl.when(qi == kj)
        def _():
            update(True)

        @pl.when(qi != kj)
        def _():
            update(False)

    q_idx = lambda h, s, qt, kt: (h, qt[s], 0)
    k_idx = lambda h, s, qt, kt: (h, kt[s], 0)
    row_idx = lambda h, s, qt, kt: (h, 0, qt[s])
    grid_spec = pltpu.PrefetchScalarGridSpec(
        num_scalar_prefetch=2, grid=(H, qt.shape[0]),
        in_specs=[pl.BlockSpec((None, t, QK_PAD), q_idx),
                  pl.BlockSpec((None, t, QK_PAD), k_idx),
                  pl.BlockSpec((None, t, V_DIM), k_idx),
                  pl.BlockSpec((t, V_DIM), lambda h, s, qt, kt: (qt[s], h)),
                  pl.BlockSpec((None, 1, t), row_idx),
                  pl.BlockSpec((None, 1, t), row_idx),
                  pl.BlockSpec((t, 1), lambda h, s, qt, kt: (kt[s], 0)),
                  pl.BlockSpec((1, t), lambda h, s, qt, kt: (0, qt[s]))],
        out_specs=[pl.BlockSpec((None, S, QK_PAD), lambda h, s, qt, kt: (h, 0, 0)),
                   pl.BlockSpec((None, t, QK_PAD), k_idx),
                   pl.BlockSpec((None, t, V_DIM), k_idx)])
    return pl.pallas_call(
        body, name="attn_bwd", grid_spec=grid_spec,
        out_shape=[jax.ShapeDtypeStruct((H, S, QK_PAD), F32), jax.ShapeDtypeStruct((H, S, QK_PAD), F32),
                   jax.ShapeDtypeStruct((H, S, V_DIM), F32)],
        compiler_params=_params(2),
    )(qt, kt, q, k, v, do, lse, delta, pos_col, pos_row)


def _conv_specs(S, tr):
    hb = tr // HALO
    main = lambda third: pl.BlockSpec((tr, D_MODEL), lambda r: (r, third))
    prev = lambda third: pl.BlockSpec((HALO, D_MODEL), lambda r: (jnp.maximum(r * hb - 1, 0), third))
    nxt = lambda third: pl.BlockSpec((HALO, D_MODEL), lambda r: (jnp.minimum((r + 1) * hb, S // HALO - 1), third))
    return main, prev, nxt


def _f32(ref):
    return ref[...].astype(F32)


def _conv_taps(gc, uu, w_ref, first):
    u2 = gc * uu
    rows = lax.broadcasted_iota(jnp.int32, u2.shape, 0)
    u2 = jnp.where((rows < HALO) & first, 0.0, u2)
    s1 = pltpu.roll(u2, 1, 0)
    s2 = pltpu.roll(u2, 2, 0)
    u3 = w_ref[2:3, :] * u2 + w_ref[1:2, :] * s1 + w_ref[0:1, :] * s2
    return u2, s1, s2, u3


def conv_fwd(bcu, cw):
    S = bcu.shape[0]
    tr = min(TROW, S)
    main, prev, _ = _conv_specs(S, tr)

    def body(gb_ref, gc_ref, u_ref, gch_ref, uh_ref, w_ref, z_ref):
        gc = jnp.concatenate([_f32(gch_ref), _f32(gc_ref)], axis=0)
        uu = jnp.concatenate([_f32(uh_ref), _f32(u_ref)], axis=0)
        u3 = _conv_taps(gc, uu, w_ref, pl.program_id(0) == 0)[3]
        z_ref[...] = (_f32(gb_ref) * u3[HALO:]).astype(z_ref.dtype)

    return pl.pallas_call(
        body, name="conv_fwd", grid=(S // tr,),
        in_specs=[main(0), main(1), main(2), prev(1), prev(2), _resident((3, D_MODEL))],
        out_specs=main(0),
        out_shape=jax.ShapeDtypeStruct((S, D_MODEL), MXU_DTYPE),
        compiler_params=_params(1),
    )(bcu, bcu, bcu, bcu, bcu, cw)


def conv_bwd(dz, bcu, cw):
    S = bcu.shape[0]
    tr = min(TROW, S)
    nr = S // tr
    main, prev, nxt = _conv_specs(S, tr)

    def body(dz_ref, dzn_ref, gb_ref, gbn_ref, gc_ref, u_ref, gch_ref, uh_ref, w_ref, o_ref, dw_ref):
        r = pl.program_id(0)
        gcv, uv = _f32(gc_ref), _f32(u_ref)
        gc = jnp.concatenate([_f32(gch_ref), gcv], axis=0)
        uu = jnp.concatenate([_f32(uh_ref), uv], axis=0)
        u2, s1, s2, u3 = _conv_taps(gc, uu, w_ref, r == 0)
        dzv = dz_ref[...]
        du3 = jnp.concatenate([dzv * _f32(gb_ref), dzn_ref[...] * _f32(gbn_ref)], axis=0)
        rows = lax.broadcasted_iota(jnp.int32, du3.shape, 0)
        du3 = jnp.where((rows >= tr) & (r == nr - 1), 0.0, du3)
        n1 = pltpu.roll(du3, tr + HALO - 1, 0)
        n2 = pltpu.roll(du3, tr + HALO - 2, 0)
        du2 = (w_ref[2:3, :] * du3 + w_ref[1:2, :] * n1 + w_ref[0:1, :] * n2)[:tr]
        o_ref[:, :D_MODEL] = (dzv * u3[HALO:]).astype(o_ref.dtype)
        o_ref[:, D_MODEL:2 * D_MODEL] = (du2 * uv).astype(o_ref.dtype)
        o_ref[:, 2 * D_MODEL:] = (du2 * gcv).astype(o_ref.dtype)
        d3 = du3[:tr]
        taps = [jnp.sum(d3 * t[HALO:], axis=0, keepdims=True) for t in (s2, s1, u2)]

        @pl.when(r == 0)
        def _():
            for kk in range(3):
                dw_ref[kk:kk + 1, :] = taps[kk]

        @pl.when(r != 0)
        def _():
            for kk in range(3):
                dw_ref[kk:kk + 1, :] += taps[kk]

    return pl.pallas_call(
        body, name="conv_bwd", grid=(nr,),
        in_specs=[main(0), nxt(0), main(0), nxt(0), main(1), main(2), prev(1), prev(2), _resident((3, D_MODEL))],
        out_specs=[pl.BlockSpec((tr, 3 * D_MODEL), lambda r: (r, 0)), _resident((3, D_MODEL))],
        out_shape=[jax.ShapeDtypeStruct((S, 3 * D_MODEL), MXU_DTYPE), jax.ShapeDtypeStruct((3, D_MODEL), F32)],
        compiler_params=_params(1),
    )(dz, dz, bcu, bcu, bcu, bcu, bcu, bcu, cw)


def adamw(parts, w, m, v, *, name):
    R, C = w.shape
    tr = R
    while tr * C * 4 > (1 << 20) and tr % 32 == 0:
        tr //= 2

    def body(p_ref, w_ref, m_ref, v_ref, g_ref, d_ref, mo_ref, vo_ref):
        g = p_ref[0].astype(F32)
        for d in range(1, N_DEV):
            g = g + p_ref[d].astype(F32)
        m_new = ADAM_B1 * m_ref[...] + (1.0 - ADAM_B1) * g
        v_new = ADAM_B2 * v_ref[...] + (1.0 - ADAM_B2) * (g * g)
        m_hat = m_new / (1.0 - ADAM_B1 ** ADAM_STEP)
        v_hat = v_new / (1.0 - ADAM_B2 ** ADAM_STEP)
        g_ref[...] = g
        d_ref[...] = -ADAM_LR * (m_hat / (jnp.sqrt(v_hat) + ADAM_EPS) + ADAM_WD * w_ref[...])
        mo_ref[...] = m_new
        vo_ref[...] = v_new

    blk = pl.BlockSpec((tr, C), lambda i: (i, 0))
    return pl.pallas_call(
        body, name=name, grid=(R // tr,),
        in_specs=[pl.BlockSpec((N_DEV, tr, C), lambda i: (0, i, 0)), blk, blk, blk],
        out_specs=[blk, blk, blk, blk],
        out_shape=[jax.ShapeDtypeStruct((R, C), F32)] * 4,
        compiler_params=_params(1),
    )(parts, w, m, v)


def _mesh_place():
    x, y, c = (lax.axis_index(n) for n in MESH_AXES)
    return x, y, c, 4 * x + 2 * y + c


def _peer(x, y, c, d):
    px = 1 - x if d & 4 else x
    py = 1 - y if d & 2 else y
    pc = 1 - c if d & 1 else c
    return (px, py, pc), 4 * px + 2 * py + pc


def gather_now(srcs, *, name):
    n = len(srcs)
    any_spec = pl.BlockSpec(memory_space=pl.ANY)

    def body(*refs):
        ins, outs, token = refs[:n], refs[n:2 * n], refs[2 * n]
        send_sems, recv_sems, local_sems = refs[2 * n + 1:]
        token[...] = jnp.zeros_like(token)
        x, y, c, me = _mesh_place()
        for a in range(n):
            pltpu.make_async_copy(ins[a], outs[a].at[me], local_sems.at[a]).start()
            for d in range(1, N_DEV):
                pltpu.make_async_remote_copy(
                    src_ref=ins[a], dst_ref=outs[a].at[me], send_sem=send_sems.at[a], recv_sem=recv_sems.at[a],
                    device_id=_peer(x, y, c, d)[0], device_id_type=pl.DeviceIdType.MESH).start()
        for a in range(n):
            pltpu.make_async_copy(ins[a], outs[a].at[me], local_sems.at[a]).wait()
            seven = outs[a].at[pl.ds(0, N_DEV - 1)]
            drain = pltpu.make_async_remote_copy(
                src_ref=seven, dst_ref=seven, send_sem=send_sems.at[a], recv_sem=recv_sems.at[a],
                device_id=(x, y, c), device_id_type=pl.DeviceIdType.MESH)
            drain.wait_send()
            drain.wait_recv()

    out = pl.pallas_call(
        body, name=name,
        in_specs=[any_spec] * n, out_specs=[any_spec] * n + [pl.BlockSpec(memory_space=pltpu.VMEM)],
        out_shape=[jax.ShapeDtypeStruct((N_DEV,) + s.shape, s.dtype) for s in srcs] + [jax.ShapeDtypeStruct((8, 128), F32)],
        scratch_shapes=[pltpu.SemaphoreType.DMA((n,)), pltpu.SemaphoreType.DMA((n,)), pltpu.SemaphoreType.DMA((n,))],
    )(*srcs)
    return out[:n], out[n]


_ANY = pl.BlockSpec(memory_space=pl.ANY)
_HBM = pl.BlockSpec(memory_space=pltpu.HBM)
_SEM = pl.BlockSpec(memory_space=pltpu.SEMAPHORE)


def _in_hbm(arrays):
    return [pltpu.with_memory_space_constraint(a, pltpu.HBM) for a in arrays]


def exchange_start(srcs, lands, slots, *, name):
    n, m = len(srcs), len(lands)

    def body(*refs):
        ins, zones = refs[:n], refs[n:n + m]
        send_sems, recv_sems, token = refs[n + m], refs[n + m + 1], refs[-1]
        x, y, c, me = _mesh_place()
        for a in range(n):
            for d in range(1, N_DEV):
                peer, peer_lin = _peer(x, y, c, d)
                src = ins[a] if slots is None else ins[a].at[peer_lin]
                dst = zones[a].at[me] if slots is None else zones[slots[a][0]].at[me, slots[a][1]]
                pltpu.make_async_remote_copy(
                    src_ref=src, dst_ref=dst, send_sem=send_sems.at[a], recv_sem=recv_sems.at[a],
                    device_id=peer, device_id_type=pl.DeviceIdType.MESH).start()
        token[...] = jnp.zeros_like(token)

    both = list(srcs) + list(lands)
    out = pl.pallas_call(
        body, name=name,
        in_specs=[_HBM] * (n + m),
        out_specs=[_SEM, _SEM] + [_HBM] * (n + m) + [pl.BlockSpec(memory_space=pltpu.VMEM)],
        out_shape=[pltpu.SemaphoreType.DMA((n,)), pltpu.SemaphoreType.DMA((n,))]
        + [pltpu.HBM(a.shape, a.dtype) for a in both] + [jax.ShapeDtypeStruct((8, 128), F32)],
        input_output_aliases={i: 2 + i for i in range(n + m)},
        compiler_params=pltpu.CompilerParams(has_side_effects=pltpu.SideEffectType.DATAFLOW_SIDE_EFFECTING),
    )(*_in_hbm(both))
    return out[0], out[1], out[2:2 + n], out[2 + n:2 + n + m], out[-1]


def exchange_wait(send_sems, recv_sems, srcs, lands, slots, after, *, name):
    n, m = len(srcs), len(lands)

    def body(*refs):
        ins, zones = refs[:n], refs[n:n + m]
        send_ref, recv_ref = refs[n + m], refs[n + m + 1]
        x, y, c, _ = _mesh_place()
        for a in range(n):
            seven = (zones[a] if slots is None else ins[a]).at[pl.ds(0, N_DEV - 1)]
            drain = pltpu.make_async_remote_copy(
                src_ref=seven, dst_ref=seven, send_sem=send_ref.at[a], recv_sem=recv_ref.at[a],
                device_id=(x, y, c), device_id_type=pl.DeviceIdType.MESH)
            drain.wait_send()
            drain.wait_recv()

    both = list(srcs) + list(lands)
    out = pl.pallas_call(
        body, name=name,
        in_specs=[_HBM] * (n + m) + [_SEM, _SEM, _ANY],
        out_specs=[_HBM] * (n + m),
        out_shape=[pltpu.HBM(a.shape, a.dtype) for a in both],
        input_output_aliases={i: i for i in range(n + m)},
        compiler_params=pltpu.CompilerParams(has_side_effects=pltpu.SideEffectType.DATAFLOW_SIDE_EFFECTING),
    )(*both, send_sems, recv_sems, after)
    return out[:n], out[n:]


def scatter_finish(remote, lands, vec, *, name):
    n, m = len(remote), len(lands)

    def body(*refs):
        ins, vec_ref, zones_in = refs[:n], refs[n], refs[n + 1:n + 1 + m]
        vec_out = refs[n + 1 + 2 * m]
        send_sems, recv_sems, local_sems = refs[n + 2 + 2 * m:]
        x, y, c, me = _mesh_place()

        def ends(a, j):
            if a == n:
                return vec_ref, vec_out.at[me]
            return ins[a].at[j], zones_in[remote[a][1]].at[me, remote[a][2]]

        for a in range(n + 1):
            pltpu.make_async_copy(*ends(a, me), local_sems.at[a]).start()
            for d in range(1, N_DEV):
                peer, peer_lin = _peer(x, y, c, d)
                src, dst = ends(a, peer_lin)
                pltpu.make_async_remote_copy(
                    src_ref=src, dst_ref=dst, send_sem=send_sems.at[a], recv_sem=recv_sems.at[a],
                    device_id=peer, device_id_type=pl.DeviceIdType.MESH).start()
        for a in range(n + 1):
            pltpu.make_async_copy(*ends(a, me), local_sems.at[a]).wait()
            seven = (vec_out if a == n else ins[a]).at[pl.ds(0, N_DEV - 1)]
            drain = pltpu.make_async_remote_copy(
                src_ref=seven, dst_ref=seven, send_sem=send_sems.at[a], recv_sem=recv_sems.at[a],
                device_id=(x, y, c), device_id_type=pl.DeviceIdType.MESH)
            drain.wait_send()
            drain.wait_recv()

    out = pl.pallas_call(
        body, name=name,
        in_specs=[_ANY] * (n + 1 + m), out_specs=[_ANY] * (m + 1),
        out_shape=[jax.ShapeDtypeStruct(z.shape, z.dtype) for z in lands]
        + [jax.ShapeDtypeStruct((N_DEV,) + vec.shape, vec.dtype)],
        input_output_aliases={n + 1 + i: i for i in range(m)},
        scratch_shapes=[pltpu.SemaphoreType.DMA((n + 1,))] * 3,
    )(*[e[0] for e in remote], vec, *lands)
    return out[:m], out[m]


def _rope_tables(pos):
    inv_freq = ROPE_THETA ** (-jnp.arange(0, ROPE, 2, dtype=F32) / ROPE)
    ang = pos.astype(F32)[:, None] * inv_freq
    cos, sin = jnp.cos(ang), jnp.sin(ang)
    pad = jnp.zeros((pos.shape[0], 128 - ROPE), F32)
    return jnp.concatenate([cos, cos, pad + 1.0], axis=1), jnp.concatenate([-sin, sin, pad], axis=1)


def _pad_last(w, n):
    return jnp.pad(w, [(0, 0)] * (w.ndim - 1) + [(0, n - w.shape[-1])])


def kernel(x, positions, g_mix, g_mlp, attn_w_down, attn_g_q_a, attn_g_kv_a, attn_w_uq, attn_w_ukv, attn_g_qnorm, attn_g_knorm, attn_w_o, conv_w_in, conv_w, conv_w_out, mlp_w1, mlp_w2, loss_target, m_g_mix, m_g_mlp, m_attn_w_down, m_attn_g_q_a, m_attn_g_kv_a, m_attn_w_uq, m_attn_w_ukv, m_attn_g_qnorm, m_attn_g_knorm, m_attn_w_o, m_conv_w_in, m_conv_w, m_conv_w_out, m_mlp_w1, m_mlp_w2, v_g_mix, v_g_mlp, v_attn_w_down, v_attn_g_q_a, v_attn_g_kv_a, v_attn_w_uq, v_attn_w_ukv, v_attn_g_qnorm, v_attn_g_knorm, v_attn_w_o, v_conv_w_in, v_conv_w, v_conv_w_out, v_mlp_w1, v_mlp_w2):
    weights = dict(g_mix=g_mix, g_mlp=g_mlp, attn_w_down=attn_w_down, attn_g_q_a=attn_g_q_a, attn_g_kv_a=attn_g_kv_a,
                   attn_w_uq=attn_w_uq, attn_w_ukv=attn_w_ukv, attn_g_qnorm=attn_g_qnorm, attn_g_knorm=attn_g_knorm,
                   attn_w_o=attn_w_o, conv_w_in=conv_w_in, conv_w=conv_w, conv_w_out=conv_w_out, mlp_w1=mlp_w1, mlp_w2=mlp_w2)
    mom1 = dict(g_mix=m_g_mix, g_mlp=m_g_mlp, attn_w_down=m_attn_w_down, attn_g_q_a=m_attn_g_q_a, attn_g_kv_a=m_attn_g_kv_a,
                attn_w_uq=m_attn_w_uq, attn_w_ukv=m_attn_w_ukv, attn_g_qnorm=m_attn_g_qnorm, attn_g_knorm=m_attn_g_knorm,
                attn_w_o=m_attn_w_o, conv_w_in=m_conv_w_in, conv_w=m_conv_w, conv_w_out=m_conv_w_out, mlp_w1=m_mlp_w1, mlp_w2=m_mlp_w2)
    mom2 = dict(g_mix=v_g_mix, g_mlp=v_g_mlp, attn_w_down=v_attn_w_down, attn_g_q_a=v_attn_g_q_a, attn_g_kv_a=v_attn_g_kv_a,
                attn_w_uq=v_attn_w_uq, attn_w_ukv=v_attn_w_ukv, attn_g_qnorm=v_attn_g_qnorm, attn_g_knorm=v_attn_g_knorm,
                attn_w_o=v_attn_w_o, conv_w_in=v_conv_w_in, conv_w=v_conv_w, conv_w_out=v_conv_w_out, mlp_w1=v_mlp_w1, mlp_w2=v_mlp_w2)
    big = ["attn_w_down", "attn_w_uq", "attn_w_ukv", "attn_w_o", "conv_w_in", "conv_w", "conv_w_out", "mlp_w1", "mlp_w2"]
    small = ["g_mix", "g_mlp", "attn_g_q_a", "attn_g_kv_a", "attn_g_qnorm", "attn_g_knorm"]
    order = ["g_mix", "g_mlp", "attn_w_down", "attn_g_q_a", "attn_g_kv_a", "attn_w_uq", "attn_w_ukv", "attn_g_qnorm",
             "attn_g_knorm", "attn_w_o", "conv_w_in", "conv_w", "conv_w_out", "mlp_w1", "mlp_w2"]

    xs = x[0]
    pos = positions[0]
    target = loss_target[0]
    S = xs.shape[0]
    depth = g_mix.shape[0]
    cos_t, sin_t = _rope_tables(pos)
    pos_col, pos_row = pos.reshape(S, 1), pos.reshape(1, S)

    keys, shards = [], []
    for name in big:
        for l in range(weights[name].shape[0]):
            keys.append((name, l))
            shards.append(weights[name][l] if name == "conv_w" else weights[name][l].astype(WIRE_DTYPE))
    mixer0 = [j for j, (name, l) in enumerate(keys) if l == 0 and name.startswith("attn")]
    first = [j for j in mixer0 if keys[j][0] == "attn_w_down"]
    first_rest = [j for j in mixer0 if j not in first]
    second = [j for j, (name, l) in enumerate(keys) if l == 0 and name.startswith("mlp")]
    later = [j for j in range(len(keys)) if j not in mixer0 + second]
    me = 4 * lax.axis_index("x") + 2 * lax.axis_index("y") + lax.axis_index("c")

    def zones_with_own(js, token):
        return [lax.dynamic_update_slice(lax.empty((N_DEV,) + shards[j].shape, shards[j].dtype),
                                         (shards[j] + token[0, 0].astype(shards[j].dtype))[None],
                                         (me,) + (0,) * shards[j].ndim) for j in js]

    arrived, token = gather_now([shards[j] for j in first], name="gather_first")
    full = dict(zip([keys[j] for j in first], arrived))
    g0 = exchange_start([shards[j] for j in first_rest], zones_with_own(first_rest, token), None, name="gather_mixer0_start")
    g1 = exchange_start([shards[j] for j in second], zones_with_own(second, g0[4]), None, name="gather_mlp0_start")
    g2 = exchange_start([shards[j] for j in later], zones_with_own(later, g1[4]), None, name="gather_rest_start")
    g_mix_0 = g_mix[0] + g2[4][0, 0]

    def rows(name, l):
        g = full[(name, l)]
        return g.reshape(g.shape[0] * g.shape[1], g.shape[2])

    saved = []
    for i in range(depth):
        l = i // 2
        rec = {"x0": xs}
        if i == 1:
            arrived = exchange_wait(*g2[:4], None, xs, name="gather_rest_wait")[1]
            full.update(zip([keys[j] for j in later], arrived))
        if i % 2 == 0:
            wd3 = _pad_last(rows("attn_w_down", l), DOWN_PAD)[None]
            gqn = _pad_last(attn_g_qnorm[l][None], QK_PAD)
            gkn = _pad_last(attn_g_knorm[l][None], QK_PAD)
            gqa, gkva = attn_g_q_a[l][None], attn_g_kv_a[l][None]
            h, a = norm_matmul(xs, g_mix_0 if i == 0 else g_mix[i], wd3, name="attn_down")
            if i == 0:
                arrived = exchange_wait(*g0[:4], None, a, name="gather_mixer0_wait")[1]
                full.update(zip([keys[j] for j in first_rest], arrived))
            wuq3 = _pad_last(full[("attn_w_uq", l)], QK_PAD)
            wukv3 = full[("attn_w_ukv", l)]
            q, k, v, cq, ckv = mla_pre_fwd(a, gqa, gkva, wuq3, wukv3, gqn, gkn, cos_t, sin_t)
            o, lse = attn_fwd(q, k, v, pos_col, pos_row)
            x1 = matmul_residual(o, rows("attn_w_o", l), xs, name="attn_out")
            rec.update(h=h, a=a, q=q, k=k, v=v, cq=cq, ckv=ckv, o=o, lse=lse, wd3=wd3, wuq3=wuq3, wukv3=wukv3,
                       gqn=gqn, gkn=gkn, gqa=gqa, gkva=gkva)
        else:
            cw = full[("conv_w", l)].transpose(1, 0, 2).reshape(3, D_MODEL)
            h, bcu = norm_matmul(xs, g_mix[i], full[("conv_w_in", l)], name="conv_in", out_dtype=MXU_DTYPE)
            z = conv_fwd(bcu, cw)
            x1 = matmul_residual(z, rows("conv_w_out", l), xs, name="conv_out")
            rec.update(h=h, bcu=bcu, z=z, cw=cw)
        if i == 0:
            arrived = exchange_wait(*g1[:4], None, x1, name="gather_mlp0_wait")[1]
            full.update(zip([keys[j] for j in second], arrived))
        h2, act, slope = norm_matmul(x1, g_mlp[i], full[("mlp_w1", i)], name="mlp_up", mlp=True)
        if i < depth - 1:
            xs = matmul_residual(act, rows("mlp_w2", i), x1, name="mlp_down")
        else:
            sq, dx, dxb = matmul_residual(act, rows("mlp_w2", i), x1, name="mlp_down_loss", target=target)
        rec.update(x1=x1, h2=h2, act=act, slope=slope)
        saved.append(rec)

    loss = lax.psum(sq[0, 0] * (0.5 / D_MODEL), MESH_AXES)

    grads = {name: [None] * weights[name].shape[0] for name in order}
    token = None
    for i in reversed(range(depth)):
        l = i // 2
        rec = saved[i]
        grads["mlp_w2"][i] = mm_tn(rec["act"], dxb, name="mlp_down_dw", G=1, out_dtype=WIRE_DTYPE).reshape(N_DEV, -1, D_MODEL)
        du = matmul_nt(dxb, rows("mlp_w2", i)[None], name="mlp_down_dx", epi="mlp_du", u=rec["slope"])
        grads["mlp_w1"][i] = mm_tn(rec["h2"], du, name="mlp_up_dw", G=N_DEV, out_dtype=WIRE_DTYPE, a_transposed=True)
        dx1, dx1b, dg = matmul_nt(du, full[("mlp_w1", i)], name="mlp_up_dx", epi="rms_bwd", x=rec["x1"], g=g_mlp[i], dx=dx)
        grads["g_mlp"][i] = dg[0]
        if i == 0:
            flying = [keys[j] for j in second + later]
            srcs = [grads[name][l_] for name, l_ in flying]
            slots = [(big.index(name), l_) for name, l_ in flying]
            zones = [lax.empty((N_DEV, weights[name].shape[0]) + grads[name][-1].shape[1:], grads[name][-1].dtype)
                     for name in big]
            for src, (k, l_) in zip(srcs, slots):
                own = lax.dynamic_index_in_dim(src, me, 0, keepdims=True)[None]
                zones[k] = lax.dynamic_update_slice(zones[k], own, (me, l_) + (0,) * (src.ndim - 1))
            s_send, s_recv, s_srcs, s_zones, token = exchange_start(srcs, zones, slots, name="scatter_rest_start")
        if i % 2 == 0:
            grads["attn_w_o"][l] = mm_tn(rec["o"], dx1b, name="attn_out_dw", G=1, out_dtype=WIRE_DTYPE,
                                         after=token).reshape(N_DEV, -1, D_MODEL)
            do, delta = matmul_nt(dx1b, rows("attn_w_o", l)[None], name="attn_out_dx", u=rec["o"])
            dq, dk, dv = attn_bwd(rec["q"], rec["k"], rec["v"], do, rec["lse"], delta, pos_col, pos_row)
            da, dwuq, dwukv, dgqn, dgkn, dgqa, dgkva = mla_pre_bwd(
                dq, dk, dv, rec["a"], rec["cq"], rec["ckv"], rec["gqa"], rec["gkva"], rec["wuq3"], rec["wukv3"],
                rec["gqn"], rec["gkn"], cos_t, sin_t)
            grads["attn_w_uq"][l] = dwuq[:, :, :QK_DIM].astype(WIRE_DTYPE)
            grads["attn_w_ukv"][l] = dwukv.astype(WIRE_DTYPE)
            grads["attn_g_qnorm"][l] = dgqn[0, :QK_DIM]
            grads["attn_g_knorm"][l] = dgkn[0, :QK_DIM]
            grads["attn_g_q_a"][l] = dgqa[0]
            grads["attn_g_kv_a"][l] = dgkva[0]
            dwd = mm_tn(rec["h"], da, name="attn_down_dw", G=1, out_dtype=WIRE_DTYPE, a_transposed=True)
            grads["attn_w_down"][l] = dwd[0, :, :DOWN].reshape(N_DEV, -1, DOWN)
            dx, dxb, dg = matmul_nt(da, rec["wd3"], name="attn_down_dx", epi="rms_bwd", x=rec["x0"], g=g_mix[i], dx=dx1)
        else:
            grads["conv_w_out"][l] = mm_tn(rec["z"], dx1b, name="conv_out_dw", G=1, out_dtype=WIRE_DTYPE).reshape(N_DEV, -1, D_MODEL)
            dz = matmul_nt(dx1b, rows("conv_w_out", l)[None], name="conv_out_dx")
            dbcu, dcw = conv_bwd(dz, rec["bcu"], rec["cw"])
            grads["conv_w"][l] = dcw.reshape(3, N_DEV, -1).transpose(1, 0, 2)
            grads["conv_w_in"][l] = mm_tn(rec["h"], dbcu, name="conv_in_dw", G=N_DEV, out_dtype=WIRE_DTYPE, a_transposed=True)
            dx, dxb, dg = matmul_nt(dbcu, full[("conv_w_in", l)], name="conv_in_dx", epi="rms_bwd", x=rec["x0"], g=g_mix[i], dx=dx1)
        grads["g_mix"][i] = dg[0]

    sizes = [weights[name].size for name in small]
    n_small = sum(sizes)
    rows_small = -(-n_small // (8 * 128)) * 8

    def pack(tree):
        flat = jnp.concatenate([jnp.stack(tree[name]).reshape(-1) if isinstance(tree[name], list) else tree[name].reshape(-1)
                                for name in small])
        return jnp.pad(flat, (0, rows_small * 128 - n_small)).reshape(rows_small, 128)

    s_srcs, s_zones = exchange_wait(s_send, s_recv, s_srcs, s_zones, slots, dx, name="scatter_rest_wait")
    remote = [(grads[name][l_], big.index(name), l_) for name, l_ in (keys[j] for j in mixer0)]
    parts, gain_parts = scatter_finish(remote, s_zones, pack(grads), name="scatter_last")

    out = {}
    for name, part in zip(big, parts):
        w = weights[name]
        flat = lambda t: t.reshape(-1, t.shape[-1])
        res = adamw(part.reshape(N_DEV, -1, w.shape[-1]), flat(w), flat(mom1[name]), flat(mom2[name]), name="adamw_" + name)
        out[name] = [r.reshape(w.shape) for r in res]
    res = adamw(gain_parts, pack(weights), pack(mom1), pack(mom2), name="adamw_gains")
    offset = 0
    for name, size in zip(small, sizes):
        out[name] = [r.reshape(-1)[offset:offset + size].reshape(weights[name].shape) for r in res]
        offset += size

    return (loss, dx[None], *[out[n][0] for n in order], *[out[n][1] for n in order],
            *[out[n][2] for n in order], *[out[n][3] for n in order])
```

```python
import jax
import jax.numpy as jnp
import numpy as np
from jax import lax
from jax.experimental import pallas as pl
from jax.experimental.pallas import tpu as pltpu

F32 = jnp.float32
MXU_DTYPE = jnp.bfloat16
WIRE_DTYPE = jnp.bfloat16

D_MODEL = 1024
N_HEADS = 8
NOPE = 128
ROPE = 64
QK_DIM = NOPE + ROPE
QK_PAD = 256
V_DIM = 128
Q_LORA = 256
KV_LORA = 128
DOWN = Q_LORA + KV_LORA + ROPE
DOWN_PAD = 512
ROPE_THETA = 10000.0
EPS = 1e-6
SM_SCALE = QK_DIM ** -0.5
LOG2E = 1.4426950408889634
Q_PRESCALE = SM_SCALE * LOG2E
ADAM_LR, ADAM_B1, ADAM_B2, ADAM_EPS, ADAM_WD, ADAM_STEP = 0.001, 0.9, 0.999, 1e-08, 0.01, 10
N_DEV = 8
MESH_AXES = ("x", "y", "c")

TM = 512
TM_WIDE = 1024
TILE_BUDGET = 32 << 20
TM_TOKENS_TN = 2048
TQ = 512
HEADS_FWD = 8
TQ_BWD = 2048
BWD_CHUNK = 256
TROW = 256
HALO = 16
VMEM_LIMIT = 48 << 20

NN = (((1,), (0,)), ((), ()))
NT = (((1,), (1,)), ((), ()))
TN = (((0,), (0,)), ((), ()))


def _dot(a, b, dims=NN):
    return lax.dot_general(a.astype(MXU_DTYPE), b.astype(MXU_DTYPE), dims, preferred_element_type=F32)


def _params(n_axes):
    return pltpu.CompilerParams(dimension_semantics=("arbitrary",) * n_axes, vmem_limit_bytes=VMEM_LIMIT)


def _rms(xv, n):
    r = lax.rsqrt(jnp.sum(xv * xv, axis=-1, keepdims=True) / n + EPS)
    return xv * r, r


def _rms_bwd(dy, xhat, r, g, n):
    dg = jnp.sum(dy * xhat, axis=0, keepdims=True)
    dxh = dy * g
    dx = r * (dxh - xhat * (jnp.sum(dxh * xhat, axis=-1, keepdims=True) / n))
    return dx, dg


def _swap_halves(t):
    lane = lax.broadcasted_iota(jnp.int32, t.shape, 1)
    return jnp.where(lane < ROPE // 2, pltpu.roll(t, 128 - ROPE // 2, 1), pltpu.roll(t, ROPE // 2, 1))


def _rope(t, cos_t, sin_t):
    return t * cos_t + _swap_halves(t) * sin_t


def _rope_bwd(dout, cos_t, sin_t):
    return dout * cos_t + _swap_halves(dout * sin_t)


def _resident(shape):
    return pl.BlockSpec(shape, lambda i: (0,) * len(shape))


def _token_tile(S, row_bytes, resident_bytes):
    wide = min(TM_WIDE, S)
    return wide if 2 * (wide * row_bytes + resident_bytes) <= TILE_BUDGET else min(TM, S)


def norm_matmul(x, g, w3, *, name, mlp=False, out_dtype=F32):
    if mlp:
        out_dtype = MXU_DTYPE
    S, D = x.shape
    G, _, Nb = w3.shape
    N = G * Nb
    n_out = 2 if mlp else 1
    tm = _token_tile(S, D * 4 + D * 2 + n_out * N * jnp.dtype(out_dtype).itemsize, w3.size * w3.dtype.itemsize)

    def body(x_ref, g_ref, w_ref, h_ref, o_ref, *slope_ref):
        xv = x_ref[...]
        r = lax.rsqrt(jnp.mean(xv * xv, axis=-1, keepdims=True) + EPS)
        h = (xv * r * g_ref[...]).astype(h_ref.dtype)
        h_ref[...] = h.T
        for gi in range(G):
            cols = slice(gi * Nb, (gi + 1) * Nb)
            acc = _dot(h, w_ref[gi])
            if mlp:
                acc = jnp.maximum(acc, 0.0)
                slope_ref[0][:, cols] = (2.0 * acc).astype(out_dtype)
                acc = jnp.square(acc)
            o_ref[:, cols] = acc.astype(o_ref.dtype)

    rows = lambda w: pl.BlockSpec((tm, w), lambda i: (i, 0))
    return pl.pallas_call(
        body, name=name, grid=(S // tm,),
        in_specs=[rows(D), _resident((1, D)), _resident((G, D, Nb))],
        out_specs=[pl.BlockSpec((D, tm), lambda i: (0, i))] + [rows(N)] * n_out,
        out_shape=[jax.ShapeDtypeStruct((D, S), MXU_DTYPE)] + [jax.ShapeDtypeStruct((S, N), out_dtype)] * n_out,
        compiler_params=_params(1),
    )(x, g.reshape(1, D), w3)


def matmul_residual(a, w, res, *, name, target=None):
    S, K = a.shape
    _, N = w.shape
    tm = _token_tile(S, K * a.dtype.itemsize + 2 * N * 4 + (0 if target is None else N * 6), w.size * w.dtype.itemsize)

    def body(a_ref, w_ref, r_ref, *rest):
        y = r_ref[...] + _dot(a_ref[...], w_ref[...])
        if target is None:
            rest[0][...] = y
            return
        t_ref, l_ref, dy_ref, dyb_ref = rest
        err = y - t_ref[...]
        dy = err / N
        dy_ref[...] = dy
        dyb_ref[...] = dy.astype(dyb_ref.dtype)
        part = jnp.full((1, 128), jnp.sum(err * err), F32)

        @pl.when(pl.program_id(0) == 0)
        def _():
            l_ref[...] = part

        @pl.when(pl.program_id(0) != 0)
        def _():
            l_ref[...] += part

    rows = lambda w_: pl.BlockSpec((tm, w_), lambda i: (i, 0))
    in_specs, args = [rows(K), _resident((K, N)), rows(N)], [a, w, res]
    if target is None:
        out_specs, out_shape = rows(N), jax.ShapeDtypeStruct((S, N), F32)
    else:
        in_specs.append(rows(N))
        args.append(target)
        out_specs = [_resident((1, 128)), rows(N), rows(N)]
        out_shape = [jax.ShapeDtypeStruct((1, 128), F32), jax.ShapeDtypeStruct((S, N), F32),
                     jax.ShapeDtypeStruct((S, N), MXU_DTYPE)]
    return pl.pallas_call(
        body, name=name, grid=(S // tm,),
        in_specs=in_specs, out_specs=out_specs, out_shape=out_shape,
        compiler_params=_params(1),
    )(*args)


def matmul_nt(a, w3, *, name, epi="plain", u=None, x=None, g=None, dx=None):
    S, N = a.shape
    G, Ko, Nb = w3.shape
    assert N == G * Nb
    with_delta = epi == "plain" and u is not None
    row_bytes = N * a.dtype.itemsize + Ko * ({"plain": 4, "mlp_du": 4, "rms_bwd": 14}[epi] + (4 if with_delta else 0))
    tm = _token_tile(S, row_bytes, w3.size * w3.dtype.itemsize)
    tko = min(Ko, 512)

    def body(a_ref, w_ref, *rest):
        if epi == "mlp_du":
            u_ref, o_ref = rest
            av = a_ref[...].astype(MXU_DTYPE)
            for j in range(Ko // tko):
                cols = slice(j * tko, (j + 1) * tko)
                da = _dot(av, w_ref[0, cols, :], NT)
                o_ref[:, cols] = (da * u_ref[:, cols].astype(F32)).astype(o_ref.dtype)
            return
        acc = _dot(a_ref[:, :Nb], w_ref[0], NT)
        for gi in range(1, G):
            acc = acc + _dot(a_ref[:, gi * Nb:(gi + 1) * Nb], w_ref[gi], NT)
        if epi == "plain":
            if with_delta:
                u_ref, o_ref, d_ref = rest
                for h in range(N_HEADS):
                    cols = slice(h * V_DIM, (h + 1) * V_DIM)
                    d_ref[h] = jnp.sum((u_ref[:, cols] * acc[:, cols]).T, axis=0, keepdims=True)
            else:
                o_ref = rest[0]
            o_ref[...] = acc
        else:
            x_ref, g_ref, dx_ref, o_ref, ob_ref, dg_ref = rest
            xhat, r = _rms(x_ref[...], Ko)
            dxb, dg = _rms_bwd(acc, xhat, r, g_ref[...], Ko)
            dx_new = dx_ref[...] + dxb
            o_ref[...] = dx_new
            ob_ref[...] = dx_new.astype(ob_ref.dtype)

            @pl.when(pl.program_id(0) == 0)
            def _():
                dg_ref[...] = dg

            @pl.when(pl.program_id(0) != 0)
            def _():
                dg_ref[...] += dg

    rows = lambda w_: pl.BlockSpec((tm, w_), lambda i: (i, 0))
    in_specs = [rows(N), _resident((G, Ko, Nb))]
    args = [a, w3]
    if with_delta:
        in_specs.append(rows(Ko))
        args.append(u)
        out_shape = [jax.ShapeDtypeStruct((S, Ko), F32), jax.ShapeDtypeStruct((N_HEADS, 1, S), F32)]
        out_specs = [rows(Ko), pl.BlockSpec((N_HEADS, 1, tm), lambda i: (0, 0, i))]
    elif epi == "plain":
        out_shape, out_specs = jax.ShapeDtypeStruct((S, Ko), F32), rows(Ko)
    elif epi == "mlp_du":
        in_specs.append(rows(Ko))
        args.append(u)
        out_shape, out_specs = jax.ShapeDtypeStruct((S, Ko), MXU_DTYPE), rows(Ko)
    else:
        in_specs += [rows(Ko), _resident((1, Ko)), rows(Ko)]
        args += [x, g.reshape(1, Ko), dx]
        out_shape = [jax.ShapeDtypeStruct((S, Ko), F32), jax.ShapeDtypeStruct((S, Ko), MXU_DTYPE),
                     jax.ShapeDtypeStruct((1, Ko), F32)]
        out_specs = [rows(Ko), rows(Ko), _resident((1, Ko))]
    return pl.pallas_call(
        body, name=name, grid=(S // tm,),
        in_specs=in_specs, out_specs=out_specs, out_shape=out_shape,
        compiler_params=_params(1),
    )(*args)


def mm_tn(a, b, *, name, G, out_dtype, after=None, a_transposed=False):
    order = [] if after is None else [after]
    Ka, S = a.shape if a_transposed else a.shape[::-1]
    _, N = b.shape
    Nb = N // G
    assert Nb <= 1024
    tm = min(TM_TOKENS_TN if b.dtype.itemsize == 2 else TM_TOKENS_TN // 2, S)
    tka = min(Ka, 1024)
    gb = 2 if G % 2 == 0 and Nb <= 512 else 1
    ns = S // tm

    def body(a_ref, b_ref, *rest):
        o_ref, acc = rest[-2:]
        s = pl.program_id(2)

        @pl.when(s == 0)
        def _():
            acc[...] = jnp.zeros_like(acc)

        av = a_ref[...]
        for gi in range(gb):
            acc[gi] += _dot(av, b_ref[:, gi * Nb:(gi + 1) * Nb], NN if a_transposed else TN)

        @pl.when(s == ns - 1)
        def _():
            o_ref[...] = acc[...].astype(o_ref.dtype)

    a_spec = pl.BlockSpec((tka, tm), lambda i, j, s: (i, s)) if a_transposed else pl.BlockSpec((tm, tka), lambda i, j, s: (s, i))
    return pl.pallas_call(
        body, name=name, grid=(Ka // tka, G // gb, ns),
        in_specs=[a_spec,
                  pl.BlockSpec((tm, gb * Nb), lambda i, j, s: (s, j))] + [pl.BlockSpec(memory_space=pl.ANY)] * len(order),
        out_specs=pl.BlockSpec((gb, tka, Nb), lambda i, j, s: (j, i, 0)),
        out_shape=jax.ShapeDtypeStruct((G, Ka, Nb), out_dtype),
        scratch_shapes=[pltpu.VMEM((gb, tka, Nb), F32)],
        compiler_params=_params(3),
    )(a, b, *order)


def mla_pre_fwd(a, gqa, gkva, wuq3, wukv3, gqn, gkn, cos_t, sin_t):
    S = a.shape[0]
    tm = min(TM, S)
    H = N_HEADS

    def body(a_ref, gqa_ref, gkva_ref, wuq_ref, wukv_ref, gqn_ref, gkn_ref, cos_ref, sin_ref,
             q_ref, k_ref, v_ref, cq_ref, ckv_ref):
        av = a_ref[...]
        cq = (_rms(av[:, :Q_LORA], Q_LORA)[0] * gqa_ref[...]).astype(cq_ref.dtype)
        ckv = (_rms(av[:, Q_LORA:Q_LORA + KV_LORA], KV_LORA)[0] * gkva_ref[...]).astype(ckv_ref.dtype)
        cq_ref[...] = cq
        ckv_ref[...] = ckv
        kpe = av[:, Q_LORA + KV_LORA:]
        cos_v, sin_v = cos_ref[...], sin_ref[...]
        for h in range(H):
            qn = _rms(_dot(cq, wuq_ref[h]), QK_DIM)[0] * gqn_ref[...]
            qr = jnp.concatenate([qn[:, :NOPE], _rope(qn[:, NOPE:], cos_v, sin_v)], axis=1)
            q_ref[h] = (qr * Q_PRESCALE).astype(q_ref.dtype)
            kvp = _dot(ckv, wukv_ref[h])
            kn = _rms(jnp.concatenate([kvp[:, :NOPE], kpe], axis=1), QK_DIM)[0] * gkn_ref[...]
            k_ref[h] = jnp.concatenate([kn[:, :NOPE], _rope(kn[:, NOPE:], cos_v, sin_v)], axis=1).astype(k_ref.dtype)
            v_ref[h] = kvp[:, NOPE:].astype(v_ref.dtype)

    row = lambda w: pl.BlockSpec((tm, w), lambda i: (i, 0))
    heads = lambda w: pl.BlockSpec((H, tm, w), lambda i: (0, i, 0))
    return pl.pallas_call(
        body, name="mla_pre_fwd", grid=(S // tm,),
        in_specs=[row(DOWN_PAD), _resident((1, Q_LORA)), _resident((1, KV_LORA)),
                  _resident((H, Q_LORA, QK_PAD)), _resident((H, KV_LORA, NOPE + V_DIM)),
                  _resident((1, QK_PAD)), _resident((1, QK_PAD)), row(128), row(128)],
        out_specs=[heads(QK_PAD), heads(QK_PAD), heads(V_DIM), row(Q_LORA), row(KV_LORA)],
        out_shape=[jax.ShapeDtypeStruct((H, S, QK_PAD), MXU_DTYPE),
                   jax.ShapeDtypeStruct((H, S, QK_PAD), MXU_DTYPE),
                   jax.ShapeDtypeStruct((H, S, V_DIM), MXU_DTYPE),
                   jax.ShapeDtypeStruct((S, Q_LORA), MXU_DTYPE),
                   jax.ShapeDtypeStruct((S, KV_LORA), MXU_DTYPE)],
        compiler_params=_params(1),
    )(a, gqa, gkva, wuq3, wukv3, gqn, gkn, cos_t, sin_t)


def mla_pre_bwd(dq, dk, dv, a, cq, ckv, gqa, gkva, wuq3, wukv3, gqn, gkn, cos_t, sin_t):
    S = a.shape[0]
    tm = min(TM, S)
    H = N_HEADS

    def body(dq_ref, dk_ref, dv_ref, a_ref, cq_ref, ckv_ref, gqa_ref, gkva_ref, wuq_ref, wukv_ref, gqn_ref, gkn_ref,
             cos_ref, sin_ref, da_ref, dwuq_ref, dwukv_ref, dgqn_ref, dgkn_ref, dgqa_ref, dgkva_ref):
        @pl.when(pl.program_id(0) == 0)
        def _():
            for ref in (dwuq_ref, dwukv_ref, dgqn_ref, dgkn_ref, dgqa_ref, dgkva_ref):
                ref[...] = jnp.zeros_like(ref)

        av = a_ref[...]
        kpe = av[:, Q_LORA + KV_LORA:]
        cos_v, sin_v = cos_ref[...], sin_ref[...]
        cqv, ckvv = cq_ref[...], ckv_ref[...]
        dcq = jnp.zeros((tm, Q_LORA), F32)
        dckv = jnp.zeros((tm, KV_LORA), F32)
        dkpe = jnp.zeros((tm, 128), F32)
        dgqn = jnp.zeros((1, QK_PAD), F32)
        dgkn = jnp.zeros((1, QK_PAD), F32)
        up = lambda h: (_dot(cqv, wuq_ref[h]), _dot(ckvv, wukv_ref[h]))
        nxt = up(0)
        for h in range(H):
            wuq, wukv = wuq_ref[h], wukv_ref[h]
            qp, kvp = nxt
            if h + 1 < H:
                nxt = up(h + 1)
            qhat, rq = _rms(qp, QK_DIM)
            dqr = dq_ref[h] * SM_SCALE
            dqn = jnp.concatenate([dqr[:, :NOPE], _rope_bwd(dqr[:, NOPE:], cos_v, sin_v)], axis=1)
            dqp, dg = _rms_bwd(dqn, qhat, rq, gqn_ref[...], QK_DIM)
            dgqn = dgqn + dg
            dqp = dqp.astype(MXU_DTYPE)
            dwuq_ref[h] += _dot(cqv, dqp, TN)
            dcq = dcq + _dot(dqp, wuq, NT)
            khat, rk = _rms(jnp.concatenate([kvp[:, :NOPE], kpe], axis=1), QK_DIM)
            dkr = dk_ref[h] * (1.0 / LOG2E)
            dkn = jnp.concatenate([dkr[:, :NOPE], _rope_bwd(dkr[:, NOPE:], cos_v, sin_v)], axis=1)
            dkk, dg = _rms_bwd(dkn, khat, rk, gkn_ref[...], QK_DIM)
            dgkn = dgkn + dg
            dkpe = dkpe + dkk[:, NOPE:]
            dkvp = jnp.concatenate([dkk[:, :NOPE], dv_ref[h]], axis=1).astype(MXU_DTYPE)
            dwukv_ref[h] += _dot(ckvv, dkvp, TN)
            dckv = dckv + _dot(dkvp, wukv, NT)
        dgqn_ref[...] += dgqn
        dgkn_ref[...] += dgkn
        ahat, r = _rms(av[:, :Q_LORA], Q_LORA)
        daq, dg = _rms_bwd(dcq, ahat, r, gqa_ref[...], Q_LORA)
        dgqa_ref[...] += dg
        ahat, r = _rms(av[:, Q_LORA:Q_LORA + KV_LORA], KV_LORA)
        dakv, dg = _rms_bwd(dckv, ahat, r, gkva_ref[...], KV_LORA)
        dgkva_ref[...] += dg
        da_ref[...] = jnp.concatenate([daq, dakv, dkpe], axis=1)

    row = lambda w: pl.BlockSpec((tm, w), lambda i: (i, 0))
    heads = lambda w: pl.BlockSpec((H, tm, w), lambda i: (0, i, 0))
    return pl.pallas_call(
        body, name="mla_pre_bwd", grid=(S // tm,),
        in_specs=[heads(QK_PAD), heads(QK_PAD), heads(V_DIM), row(DOWN_PAD), row(Q_LORA), row(KV_LORA),
                  _resident((1, Q_LORA)), _resident((1, KV_LORA)),
                  _resident((H, Q_LORA, QK_PAD)), _resident((H, KV_LORA, NOPE + V_DIM)),
                  _resident((1, QK_PAD)), _resident((1, QK_PAD)), row(128), row(128)],
        out_specs=[row(DOWN_PAD), _resident((H, Q_LORA, QK_PAD)), _resident((H, KV_LORA, NOPE + V_DIM)),
                   _resident((1, QK_PAD)), _resident((1, QK_PAD)), _resident((1, Q_LORA)), _resident((1, KV_LORA))],
        out_shape=[jax.ShapeDtypeStruct((S, DOWN_PAD), F32),
                   jax.ShapeDtypeStruct((H, Q_LORA, QK_PAD), F32),
                   jax.ShapeDtypeStruct((H, KV_LORA, NOPE + V_DIM), F32),
                   jax.ShapeDtypeStruct((1, QK_PAD), F32), jax.ShapeDtypeStruct((1, QK_PAD), F32),
                   jax.ShapeDtypeStruct((1, Q_LORA), F32), jax.ShapeDtypeStruct((1, KV_LORA), F32)],
        compiler_params=_params(1),
    )(dq, dk, dv, a, cq, ckv, gqa, gkva, wuq3, wukv3, gqn, gkn, cos_t, sin_t)


def _pair_tables(nb, key_major):
    if key_major:
        pairs = [(qi, kj) for kj in range(nb) for qi in range(kj, nb)]
    else:
        pairs = [(qi, ki) for qi in range(nb) for ki in range(qi + 1)]
    return (jnp.asarray(np.array([p[0] for p in pairs], np.int32)),
            jnp.asarray(np.array([p[1] for p in pairs], np.int32)))


def _scores_t(k, q, pk_col, pq_row, masked):
    s = _dot(k, q, NT)
    return jnp.where(pq_row >= pk_col, s, jnp.finfo(F32).min) if masked else s


def attn_fwd(q, k, v, pos_col, pos_row):
    H, S, _ = q.shape
    t = min(TQ, S)
    nb = S // t
    hb = HEADS_FWD
    qt, kt = _pair_tables(nb, key_major=False)

    def body(qt_ref, kt_ref, q_ref, k_ref, v_ref, pk_ref, pq_ref, o_ref, lse_ref, m_s, l_s, acc):
        step = pl.program_id(1)
        qi, ki = qt_ref[step], kt_ref[step]

        @pl.when(ki == 0)
        def _():
            m_s[...] = jnp.full_like(m_s, -jnp.inf)
            l_s[...] = jnp.zeros_like(l_s)
            acc[...] = jnp.zeros_like(acc)

        def update(masked):
            scores = lambda hh: _scores_t(k_ref[hh], q_ref[hh], pk_ref[...], pq_ref[...], masked)
            def weighted_values(hh, p, alpha):
                acc[hh] = alpha * acc[hh] + _dot(v_ref[hh], p, TN)

            s_next = scores(0)
            pending = None
            for hh in range(hb):
                s = s_next
                if hh + 1 < hb:
                    s_next = scores(hh + 1)
                m_old = m_s[hh]
                m_new = jnp.maximum(m_old, jnp.max(s, axis=0, keepdims=True))
                p = jnp.exp2(s - m_new)
                alpha = jnp.exp2(m_old - m_new)
                l_s[hh] = alpha * l_s[hh] + jnp.sum(p, axis=0, keepdims=True)
                m_s[hh] = m_new
                if pending is not None:
                    weighted_values(*pending)
                pending = (hh, p, alpha)
            weighted_values(*pending)

        @pl.when(ki < qi)
        def _():
            update(False)

        @pl.when(ki == qi)
        def _():
            update(True)
            for hh in range(hb):
                o_ref[:, hh * V_DIM:(hh + 1) * V_DIM] = (acc[hh] / l_s[hh]).T
                lse_ref[hh] = m_s[hh] + jnp.log(l_s[hh]) * LOG2E

    grid_spec = pltpu.PrefetchScalarGridSpec(
        num_scalar_prefetch=2, grid=(H // hb, qt.shape[0]),
        in_specs=[pl.BlockSpec((hb, t, QK_PAD), lambda h, s, qt, kt: (h, qt[s], 0)),
                  pl.BlockSpec((hb, t, QK_PAD), lambda h, s, qt, kt: (h, kt[s], 0)),
                  pl.BlockSpec((hb, t, V_DIM), lambda h, s, qt, kt: (h, kt[s], 0)),
                  pl.BlockSpec((t, 1), lambda h, s, qt, kt: (kt[s], 0)),
                  pl.BlockSpec((1, t), lambda h, s, qt, kt: (0, qt[s]))],
        out_specs=[pl.BlockSpec((t, hb * V_DIM), lambda h, s, qt, kt: (qt[s], h)),
                   pl.BlockSpec((hb, 1, t), lambda h, s, qt, kt: (h, 0, qt[s]))],
        scratch_shapes=[pltpu.VMEM((hb, 1, t), F32), pltpu.VMEM((hb, 1, t), F32), pltpu.VMEM((hb, V_DIM, t), F32)])
    return pl.pallas_call(
        body, name="attn_fwd", grid_spec=grid_spec,
        out_shape=[jax.ShapeDtypeStruct((S, H * V_DIM), F32), jax.ShapeDtypeStruct((H, 1, S), F32)],
        compiler_params=_params(2),
    )(qt, kt, q, k, v, pos_col, pos_row)


def attn_bwd(q, k, v, do, lse, delta, pos_col, pos_row):
    H, S, _ = q.shape
    t = min(TQ_BWD, S)
    nb = S // t
    qt, kt = _pair_tables(nb, key_major=True)
    tc = min(BWD_CHUNK, t)

    def body(qt_ref, kt_ref, q_ref, k_ref, v_ref, do_ref, lse_ref, dl_ref, pk_ref, pq_ref, dq_ref, dk_ref, dv_ref):
        step = pl.program_id(1)
        qi, kj = qt_ref[step], kt_ref[step]

        @pl.when(step == 0)
        def _():
            dq_ref[...] = jnp.zeros_like(dq_ref)

        @pl.when(qi == kj)
        def _():
            dk_ref[...] = jnp.zeros_like(dk_ref)
            dv_ref[...] = jnp.zeros_like(dv_ref)

        def update(masked):
            seen = lambda c: (c + 1) * tc if masked else t

            def first_matmuls(c):
                cols, ke = slice(c * tc, (c + 1) * tc), seen(c)
                qc = q_ref[cols, :]
                doc = do_ref[cols, :].astype(MXU_DTYPE)
                s = _scores_t(k_ref[:ke, :], qc, pk_ref[:ke, :], pq_ref[:, cols], masked)
                return qc, doc, s, _dot(v_ref[:ke, :], doc, NT)

            nxt = first_matmuls(0)
            for c in range(t // tc):
                qc, doc, s, dp = nxt
                if c + 1 < t // tc:
                    nxt = first_matmuls(c + 1)
                cols, ke = slice(c * tc, (c + 1) * tc), seen(c)
                p = jnp.exp2(s - lse_ref[:, cols])
                ds = (p * (dp - dl_ref[:, cols])).astype(MXU_DTYPE)
                dv_ref[:ke, :] += _dot(p, doc)
                dk_ref[:ke, :] += _dot(ds, qc)
                rows = pl.ds(pl.multiple_of(qi * t + c * tc, tc), tc)
                dq_ref[rows, :] += _dot(ds, k_ref[:ke, :], TN)

        @pl.when(qi == kj)
        def _():
            update(True)

        @pl.when(qi != kj)
        def _():
            update(False)

    q_idx = lambda h, s, qt, kt: (h, qt[s], 0)
    k_idx = lambda h, s, qt, kt: (h, kt[s], 0)
    row_idx = lambda h, s, qt, kt: (h, 0, qt[s])
    grid_spec = pltpu.PrefetchScalarGridSpec(
        num_scalar_prefetch=2, grid=(H, qt.shape[0]),
        in_specs=[pl.BlockSpec((None, t, QK_PAD), q_idx),
                  pl.BlockSpec((None, t, QK_PAD), k_idx),
                  pl.BlockSpec((None, t, V_DIM), k_idx),
                  pl.BlockSpec((t, V_DIM), lambda h, s, qt, kt: (qt[s], h)),
                  pl.BlockSpec((None, 1, t), row_idx),
                  pl.BlockSpec((None, 1, t), row_idx),
                  pl.BlockSpec((t, 1), lambda h, s, qt, kt: (kt[s], 0)),
                  pl.BlockSpec((1, t), lambda h, s, qt, kt: (0, qt[s]))],
        out_specs=[pl.BlockSpec((None, S, QK_PAD), lambda h, s, qt, kt: (h, 0, 0)),
                   pl.BlockSpec((None, t, QK_PAD), k_idx),
                   pl.BlockSpec((None, t, V_DIM), k_idx)])
    return pl.pallas_call(
        body, name="attn_bwd", grid_spec=grid_spec,
        out_shape=[jax.ShapeDtypeStruct((H, S, QK_PAD), F32), jax.ShapeDtypeStruct((H, S, QK_PAD), F32),
                   jax.ShapeDtypeStruct((H, S, V_DIM), F32)],
        compiler_params=_params(2),
    )(qt, kt, q, k, v, do, lse, delta, pos_col, pos_row)


def _conv_specs(S, tr):
    hb = tr // HALO
    main = lambda third: pl.BlockSpec((tr, D_MODEL), lambda r: (r, third))
    prev = lambda third: pl.BlockSpec((HALO, D_MODEL), lambda r: (jnp.maximum(r * hb - 1, 0), third))
    nxt = lambda third: pl.BlockSpec((HALO, D_MODEL), lambda r: (jnp.minimum((r + 1) * hb, S // HALO - 1), third))
    return main, prev, nxt


def _f32(ref):
    return ref[...].astype(F32)


def _conv_taps(gc, uu, w_ref, first):
    u2 = gc * uu
    rows = lax.broadcasted_iota(jnp.int32, u2.shape, 0)
    u2 = jnp.where((rows < HALO) & first, 0.0, u2)
    s1 = pltpu.roll(u2, 1, 0)
    s2 = pltpu.roll(u2, 2, 0)
    u3 = w_ref[2:3, :] * u2 + w_ref[1:2, :] * s1 + w_ref[0:1, :] * s2
    return u2, s1, s2, u3


def conv_fwd(bcu, cw):
    S = bcu.shape[0]
    tr = min(TROW, S)
    main, prev, _ = _conv_specs(S, tr)

    def body(gb_ref, gc_ref, u_ref, gch_ref, uh_ref, w_ref, z_ref):
        gc = jnp.concatenate([_f32(gch_ref), _f32(gc_ref)], axis=0)
        uu = jnp.concatenate([_f32(uh_ref), _f32(u_ref)], axis=0)
        u3 = _conv_taps(gc, uu, w_ref, pl.program_id(0) == 0)[3]
        z_ref[...] = (_f32(gb_ref) * u3[HALO:]).astype(z_ref.dtype)

    return pl.pallas_call(
        body, name="conv_fwd", grid=(S // tr,),
        in_specs=[main(0), main(1), main(2), prev(1), prev(2), _resident((3, D_MODEL))],
        out_specs=main(0),
        out_shape=jax.ShapeDtypeStruct((S, D_MODEL), MXU_DTYPE),
        compiler_params=_params(1),
    )(bcu, bcu, bcu, bcu, bcu, cw)


def conv_bwd(dz, bcu, cw):
    S = bcu.shape[0]
    tr = min(TROW, S)
    nr = S // tr
    main, prev, nxt = _conv_specs(S, tr)

    def body(dz_ref, dzn_ref, gb_ref, gbn_ref, gc_ref, u_ref, gch_ref, uh_ref, w_ref, o_ref, dw_ref):
        r = pl.program_id(0)
        gcv, uv = _f32(gc_ref), _f32(u_ref)
        gc = jnp.concatenate([_f32(gch_ref), gcv], axis=0)
        uu = jnp.concatenate([_f32(uh_ref), uv], axis=0)
        u2, s1, s2, u3 = _conv_taps(gc, uu, w_ref, r == 0)
        dzv = dz_ref[...]
        du3 = jnp.concatenate([dzv * _f32(gb_ref), dzn_ref[...] * _f32(gbn_ref)], axis=0)
        rows = lax.broadcasted_iota(jnp.int32, du3.shape, 0)
        du3 = jnp.where((rows >= tr) & (r == nr - 1), 0.0, du3)
        n1 = pltpu.roll(du3, tr + HALO - 1, 0)
        n2 = pltpu.roll(du3, tr + HALO - 2, 0)
        du2 = (w_ref[2:3, :] * du3 + w_ref[1:2, :] * n1 + w_ref[0:1, :] * n2)[:tr]
        o_ref[:, :D_MODEL] = (dzv * u3[HALO:]).astype(o_ref.dtype)
        o_ref[:, D_MODEL:2 * D_MODEL] = (du2 * uv).astype(o_ref.dtype)
        o_ref[:, 2 * D_MODEL:] = (du2 * gcv).astype(o_ref.dtype)
        d3 = du3[:tr]
        taps = [jnp.sum(d3 * t[HALO:], axis=0, keepdims=True) for t in (s2, s1, u2)]

        @pl.when(r == 0)
        def _():
            for kk in range(3):
                dw_ref[kk:kk + 1, :] = taps[kk]

        @pl.when(r != 0)
        def _():
            for kk in range(3):
                dw_ref[kk:kk + 1, :] += taps[kk]

    return pl.pallas_call(
        body, name="conv_bwd", grid=(nr,),
        in_specs=[main(0), nxt(0), main(0), nxt(0), main(1), main(2), prev(1), prev(2), _resident((3, D_MODEL))],
        out_specs=[pl.BlockSpec((tr, 3 * D_MODEL), lambda r: (r, 0)), _resident((3, D_MODEL))],
        out_shape=[jax.ShapeDtypeStruct((S, 3 * D_MODEL), MXU_DTYPE), jax.ShapeDtypeStruct((3, D_MODEL), F32)],
        compiler_params=_params(1),
    )(dz, dz, bcu, bcu, bcu, bcu, bcu, bcu, cw)


def adamw(parts, w, m, v, *, name):
    R, C = w.shape
    tr = R
    while tr * C * 4 > (1 << 20) and tr % 32 == 0:
        tr //= 2

    def body(p_ref, w_ref, m_ref, v_ref, g_ref, d_ref, mo_ref, vo_ref):
        g = p_ref[0].astype(F32)
        for d in range(1, N_DEV):
            g = g + p_ref[d].astype(F32)
        m_new = ADAM_B1 * m_ref[...] + (1.0 - ADAM_B1) * g
        v_new = ADAM_B2 * v_ref[...] + (1.0 - ADAM_B2) * (g * g)
        m_hat = m_new / (1.0 - ADAM_B1 ** ADAM_STEP)
        v_hat = v_new / (1.0 - ADAM_B2 ** ADAM_STEP)
        g_ref[...] = g
        d_ref[...] = -ADAM_LR * (m_hat / (jnp.sqrt(v_hat) + ADAM_EPS) + ADAM_WD * w_ref[...])
        mo_ref[...] = m_new
        vo_ref[...] = v_new

    blk = pl.BlockSpec((tr, C), lambda i: (i, 0))
    return pl.pallas_call(
        body, name=name, grid=(R // tr,),
        in_specs=[pl.BlockSpec((N_DEV, tr, C), lambda i: (0, i, 0)), blk, blk, blk],
        out_specs=[blk, blk, blk, blk],
        out_shape=[jax.ShapeDtypeStruct((R, C), F32)] * 4,
        compiler_params=_params(1),
    )(parts, w, m, v)


def _mesh_place():
    x, y, c = (lax.axis_index(n) for n in MESH_AXES)
    return x, y, c, 4 * x + 2 * y + c


def _peer(x, y, c, d):
    px = 1 - x if d & 4 else x
    py = 1 - y if d & 2 else y
    pc = 1 - c if d & 1 else c
    return (px, py, pc), 4 * px + 2 * py + pc


def gather_now(srcs, *, name):
    n = len(srcs)
    any_spec = pl.BlockSpec(memory_space=pl.ANY)

    def body(*refs):
        ins, outs, token = refs[:n], refs[n:2 * n], refs[2 * n]
        send_sems, recv_sems, local_sems = refs[2 * n + 1:]
        token[...] = jnp.zeros_like(token)
        x, y, c, me = _mesh_place()
        for a in range(n):
            pltpu.make_async_copy(ins[a], outs[a].at[me], local_sems.at[a]).start()
            for d in range(1, N_DEV):
                pltpu.make_async_remote_copy(
                    src_ref=ins[a], dst_ref=outs[a].at[me], send_sem=send_sems.at[a], recv_sem=recv_sems.at[a],
                    device_id=_peer(x, y, c, d)[0], device_id_type=pl.DeviceIdType.MESH).start()
        for a in range(n):
            pltpu.make_async_copy(ins[a], outs[a].at[me], local_sems.at[a]).wait()
            seven = outs[a].at[pl.ds(0, N_DEV - 1)]
            drain = pltpu.make_async_remote_copy(
                src_ref=seven, dst_ref=seven, send_sem=send_sems.at[a], recv_sem=recv_sems.at[a],
                device_id=(x, y, c), device_id_type=pl.DeviceIdType.MESH)
            drain.wait_send()
            drain.wait_recv()

    out = pl.pallas_call(
        body, name=name,
        in_specs=[any_spec] * n, out_specs=[any_spec] * n + [pl.BlockSpec(memory_space=pltpu.VMEM)],
        out_shape=[jax.ShapeDtypeStruct((N_DEV,) + s.shape, s.dtype) for s in srcs] + [jax.ShapeDtypeStruct((8, 128), F32)],
        scratch_shapes=[pltpu.SemaphoreType.DMA((n,)), pltpu.SemaphoreType.DMA((n,)), pltpu.SemaphoreType.DMA((n,))],
    )(*srcs)
    return out[:n], out[n]


_ANY = pl.BlockSpec(memory_space=pl.ANY)
_HBM = pl.BlockSpec(memory_space=pltpu.HBM)
_SEM = pl.BlockSpec(memory_space=pltpu.SEMAPHORE)


def _in_hbm(arrays):
    return [pltpu.with_memory_space_constraint(a, pltpu.HBM) for a in arrays]


def exchange_start(srcs, lands, slots, *, name):
    n, m = len(srcs), len(lands)

    def body(*refs):
        ins, zones = refs[:n], refs[n:n + m]
        send_sems, recv_sems, token = refs[n + m], refs[n + m + 1], refs[-1]
        x, y, c, me = _mesh_place()
        for a in range(n):
            for d in range(1, N_DEV):
                peer, peer_lin = _peer(x, y, c, d)
                src = ins[a] if slots is None else ins[a].at[peer_lin]
                dst = zones[a].at[me] if slots is None else zones[slots[a][0]].at[me, slots[a][1]]
                pltpu.make_async_remote_copy(
                    src_ref=src, dst_ref=dst, send_sem=send_sems.at[a], recv_sem=recv_sems.at[a],
                    device_id=peer, device_id_type=pl.DeviceIdType.MESH).start()
        token[...] = jnp.zeros_like(token)

    both = list(srcs) + list(lands)
    out = pl.pallas_call(
        body, name=name,
        in_specs=[_HBM] * (n + m),
        out_specs=[_SEM, _SEM] + [_HBM] * (n + m) + [pl.BlockSpec(memory_space=pltpu.VMEM)],
        out_shape=[pltpu.SemaphoreType.DMA((n,)), pltpu.SemaphoreType.DMA((n,))]
        + [pltpu.HBM(a.shape, a.dtype) for a in both] + [jax.ShapeDtypeStruct((8, 128), F32)],
        input_output_aliases={i: 2 + i for i in range(n + m)},
        compiler_params=pltpu.CompilerParams(has_side_effects=pltpu.SideEffectType.DATAFLOW_SIDE_EFFECTING),
    )(*_in_hbm(both))
    return out[0], out[1], out[2:2 + n], out[2 + n:2 + n + m], out[-1]


def exchange_wait(send_sems, recv_sems, srcs, lands, slots, after, *, name):
    n, m = len(srcs), len(lands)

    def body(*refs):
        ins, zones = refs[:n], refs[n:n + m]
        send_ref, recv_ref = refs[n + m], refs[n + m + 1]
        x, y, c, _ = _mesh_place()
        for a in range(n):
            seven = (zones[a] if slots is None else ins[a]).at[pl.ds(0, N_DEV - 1)]
            drain = pltpu.make_async_remote_copy(
                src_ref=seven, dst_ref=seven, send_sem=send_ref.at[a], recv_sem=recv_ref.at[a],
                device_id=(x, y, c), device_id_type=pl.DeviceIdType.MESH)
            drain.wait_send()
            drain.wait_recv()

    both = list(srcs) + list(lands)
    out = pl.pallas_call(
        body, name=name,
        in_specs=[_HBM] * (n + m) + [_SEM, _SEM, _ANY],
        out_specs=[_HBM] * (n + m),
        out_shape=[pltpu.HBM(a.shape, a.dtype) for a in both],
        input_output_aliases={i: i for i in range(n + m)},
        compiler_params=pltpu.CompilerParams(has_side_effects=pltpu.SideEffectType.DATAFLOW_SIDE_EFFECTING),
    )(*both, send_sems, recv_sems, after)
    return out[:n], out[n:]


def scatter_finish(remote, lands, vec, *, name):
    n, m = len(remote), len(lands)

    def body(*refs):
        ins, vec_ref, zones_in = refs[:n], refs[n], refs[n + 1:n + 1 + m]
        vec_out = refs[n + 1 + 2 * m]
        send_sems, recv_sems, local_sems = refs[n + 2 + 2 * m:]
        x, y, c, me = _mesh_place()

        def ends(a, j):
            if a == n:
                return vec_ref, vec_out.at[me]
            return ins[a].at[j], zones_in[remote[a][1]].at[me, remote[a][2]]

        for a in range(n + 1):
            pltpu.make_async_copy(*ends(a, me), local_sems.at[a]).start()
            for d in range(1, N_DEV):
                peer, peer_lin = _peer(x, y, c, d)
                src, dst = ends(a, peer_lin)
                pltpu.make_async_remote_copy(
                    src_ref=src, dst_ref=dst, send_sem=send_sems.at[a], recv_sem=recv_sems.at[a],
                    device_id=peer, device_id_type=pl.DeviceIdType.MESH).start()
        for a in range(n + 1):
            pltpu.make_async_copy(*ends(a, me), local_sems.at[a]).wait()
            seven = (vec_out if a == n else ins[a]).at[pl.ds(0, N_DEV - 1)]
            drain = pltpu.make_async_remote_copy(
                src_ref=seven, dst_ref=seven, send_sem=send_sems.at[a], recv_sem=recv_sems.at[a],
                device_id=(x, y, c), device_id_type=pl.DeviceIdType.MESH)
            drain.wait_send()
            drain.wait_recv()

    out = pl.pallas_call(
        body, name=name,
        in_specs=[_ANY] * (n + 1 + m), out_specs=[_ANY] * (m + 1),
        out_shape=[jax.ShapeDtypeStruct(z.shape, z.dtype) for z in lands]
        + [jax.ShapeDtypeStruct((N_DEV,) + vec.shape, vec.dtype)],
        input_output_aliases={n + 1 + i: i for i in range(m)},
        scratch_shapes=[pltpu.SemaphoreType.DMA((n + 1,))] * 3,
    )(*[e[0] for e in remote], vec, *lands)
    return out[:m], out[m]


def _rope_tables(pos):
    inv_freq = ROPE_THETA ** (-jnp.arange(0, ROPE, 2, dtype=F32) / ROPE)
    ang = pos.astype(F32)[:, None] * inv_freq
    cos, sin = jnp.cos(ang), jnp.sin(ang)
    pad = jnp.zeros((pos.shape[0], 128 - ROPE), F32)
    return jnp.concatenate([cos, cos, pad + 1.0], axis=1), jnp.concatenate([-sin, sin, pad], axis=1)


def _pad_last(w, n):
    return jnp.pad(w, [(0, 0)] * (w.ndim - 1) + [(0, n - w.shape[-1])])


def kernel(x, positions, g_mix, g_mlp, attn_w_down, attn_g_q_a, attn_g_kv_a, attn_w_uq, attn_w_ukv, attn_g_qnorm, attn_g_knorm, attn_w_o, conv_w_in, conv_w, conv_w_out, mlp_w1, mlp_w2, loss_target, m_g_mix, m_g_mlp, m_attn_w_down, m_attn_g_q_a, m_attn_g_kv_a, m_attn_w_uq, m_attn_w_ukv, m_attn_g_qnorm, m_attn_g_knorm, m_attn_w_o, m_conv_w_in, m_conv_w, m_conv_w_out, m_mlp_w1, m_mlp_w2, v_g_mix, v_g_mlp, v_attn_w_down, v_attn_g_q_a, v_attn_g_kv_a, v_attn_w_uq, v_attn_w_ukv, v_attn_g_qnorm, v_attn_g_knorm, v_attn_w_o, v_conv_w_in, v_conv_w, v_conv_w_out, v_mlp_w1, v_mlp_w2):
    weights = dict(g_mix=g_mix, g_mlp=g_mlp, attn_w_down=attn_w_down, attn_g_q_a=attn_g_q_a, attn_g_kv_a=attn_g_kv_a,
                   attn_w_uq=attn_w_uq, attn_w_ukv=attn_w_ukv, attn_g_qnorm=attn_g_qnorm, attn_g_knorm=attn_g_knorm,
                   attn_w_o=attn_w_o, conv_w_in=conv_w_in, conv_w=conv_w, conv_w_out=conv_w_out, mlp_w1=mlp_w1, mlp_w2=mlp_w2)
    mom1 = dict(g_mix=m_g_mix, g_mlp=m_g_mlp, attn_w_down=m_attn_w_down, attn_g_q_a=m_attn_g_q_a, attn_g_kv_a=m_attn_g_kv_a,
                attn_w_uq=m_attn_w_uq, attn_w_ukv=m_attn_w_ukv, attn_g_qnorm=m_attn_g_qnorm, attn_g_knorm=m_attn_g_knorm,
                attn_w_o=m_attn_w_o, conv_w_in=m_conv_w_in, conv_w=m_conv_w, conv_w_out=m_conv_w_out, mlp_w1=m_mlp_w1, mlp_w2=m_mlp_w2)
    mom2 = dict(g_mix=v_g_mix, g_mlp=v_g_mlp, attn_w_down=v_attn_w_down, attn_g_q_a=v_attn_g_q_a, attn_g_kv_a=v_attn_g_kv_a,
                attn_w_uq=v_attn_w_uq, attn_w_ukv=v_attn_w_ukv, attn_g_qnorm=v_attn_g_qnorm, attn_g_knorm=v_attn_g_knorm,
                attn_w_o=v_attn_w_o, conv_w_in=v_conv_w_in, conv_w=v_conv_w, conv_w_out=v_conv_w_out, mlp_w1=v_mlp_w1, mlp_w2=v_mlp_w2)
    big = ["attn_w_down", "attn_w_uq", "attn_w_ukv", "attn_w_o", "conv_w_in", "conv_w", "conv_w_out", "mlp_w1", "mlp_w2"]
    small = ["g_mix", "g_mlp", "attn_g_q_a", "attn_g_kv_a", "attn_g_qnorm", "attn_g_knorm"]
    order = ["g_mix", "g_mlp", "attn_w_down", "attn_g_q_a", "attn_g_kv_a", "attn_w_uq", "attn_w_ukv", "attn_g_qnorm",
             "attn_g_knorm", "attn_w_o", "conv_w_in", "conv_w", "conv_w_out", "mlp_w1", "mlp_w2"]

    xs = x[0]
    pos = positions[0]
    target = loss_target[0]
    S = xs.shape[0]
    depth = g_mix.shape[0]
    cos_t, sin_t = _rope_tables(pos)
    pos_col, pos_row = pos.reshape(S, 1), pos.reshape(1, S)

    keys, shards = [], []
    for name in big:
        for l in range(weights[name].shape[0]):
            keys.append((name, l))
            shards.append(weights[name][l] if name == "conv_w" else weights[name][l].astype(WIRE_DTYPE))
    mixer0 = [j for j, (name, l) in enumerate(keys) if l == 0 and name.startswith("attn")]
    first = [j for j in mixer0 if keys[j][0] == "attn_w_down"]
    first_rest = [j for j in mixer0 if j not in first]
    second = [j for j, (name, l) in enumerate(keys) if l == 0 and name.startswith("mlp")]
    later = [j for j in range(len(keys)) if j not in mixer0 + second]
    me = 4 * lax.axis_index("x") + 2 * lax.axis_index("y") + lax.axis_index("c")

    def zones_with_own(js, token):
        return [lax.dynamic_update_slice(lax.empty((N_DEV,) + shards[j].shape, shards[j].dtype),
                                         (shards[j] + token[0, 0].astype(shards[j].dtype))[None],
                                         (me,) + (0,) * shards[j].ndim) for j in js]

    arrived, token = gather_now([shards[j] for j in first], name="gather_first")
    full = dict(zip([keys[j] for j in first], arrived))
    g0 = exchange_start([shards[j] for j in first_rest], zones_with_own(first_rest, token), None, name="gather_mixer0_start")
    g1 = exchange_start([shards[j] for j in second], zones_with_own(second, g0[4]), None, name="gather_mlp0_start")
    layer_of = lambda j: keys[j][1] if keys[j][0].startswith("mlp") else 2 * keys[j][1] + int(keys[j][0].startswith("conv"))
    later1 = [j for j in later if layer_of(j) == 1]
    later2 = [j for j in later if layer_of(j) > 1]
    g2 = exchange_start([shards[j] for j in later1], zones_with_own(later1, g1[4]), None, name="gather_layer1_start")
    g3 = exchange_start([shards[j] for j in later2], zones_with_own(later2, g2[4]), None, name="gather_rest_start")
    g_mix_0 = g_mix[0] + g3[4][0, 0]

    def rows(name, l):
        g = full[(name, l)]
        return g.reshape(g.shape[0] * g.shape[1], g.shape[2])

    saved = []
    for i in range(depth):
        l = i // 2
        rec = {"x0": xs}
        if i == 1:
            arrived = exchange_wait(*g2[:4], None, xs, name="gather_layer1_wait")[1]
            full.update(zip([keys[j] for j in later1], arrived))
        if i == 2:
            arrived = exchange_wait(*g3[:4], None, xs, name="gather_rest_wait")[1]
            full.update(zip([keys[j] for j in later2], arrived))
        if i % 2 == 0:
            wd3 = _pad_last(rows("attn_w_down", l), DOWN_PAD)[None]
            gqn = _pad_last(attn_g_qnorm[l][None], QK_PAD)
            gkn = _pad_last(attn_g_knorm[l][None], QK_PAD)
            gqa, gkva = attn_g_q_a[l][None], attn_g_kv_a[l][None]
            h, a = norm_matmul(xs, g_mix_0 if i == 0 else g_mix[i], wd3, name="attn_down")
            if i == 0:
                arrived = exchange_wait(*g0[:4], None, a, name="gather_mixer0_wait")[1]
                full.update(zip([keys[j] for j in first_rest], arrived))
            wuq3 = _pad_last(full[("attn_w_uq", l)], QK_PAD)
            wukv3 = full[("attn_w_ukv", l)]
            q, k, v, cq, ckv = mla_pre_fwd(a, gqa, gkva, wuq3, wukv3, gqn, gkn, cos_t, sin_t)
            o, lse = attn_fwd(q, k, v, pos_col, pos_row)
            x1 = matmul_residual(o, rows("attn_w_o", l), xs, name="attn_out")
            rec.update(h=h, a=a, q=q, k=k, v=v, cq=cq, ckv=ckv, o=o, lse=lse, wd3=wd3, wuq3=wuq3, wukv3=wukv3,
                       gqn=gqn, gkn=gkn, gqa=gqa, gkva=gkva)
        else:
            cw = full[("conv_w", l)].transpose(1, 0, 2).reshape(3, D_MODEL)
            h, bcu = norm_matmul(xs, g_mix[i], full[("conv_w_in", l)], name="conv_in", out_dtype=MXU_DTYPE)
            z = conv_fwd(bcu, cw)
            x1 = matmul_residual(z, rows("conv_w_out", l), xs, name="conv_out")
            rec.update(h=h, bcu=bcu, z=z, cw=cw)
        if i == 0:
            arrived = exchange_wait(*g1[:4], None, x1, name="gather_mlp0_wait")[1]
            full.update(zip([keys[j] for j in second], arrived))
        h2, act, slope = norm_matmul(x1, g_mlp[i], full[("mlp_w1", i)], name="mlp_up", mlp=True)
        if i < depth - 1:
            xs = matmul_residual(act, rows("mlp_w2", i), x1, name="mlp_down")
        else:
            sq, dx, dxb = matmul_residual(act, rows("mlp_w2", i), x1, name="mlp_down_loss", target=target)
        rec.update(x1=x1, h2=h2, act=act, slope=slope)
        saved.append(rec)

    loss = lax.psum(sq[0, 0] * (0.5 / D_MODEL), MESH_AXES)

    grads = {name: [None] * weights[name].shape[0] for name in order}
    token = None
    for i in reversed(range(depth)):
        l = i // 2
        rec = saved[i]
        grads["mlp_w2"][i] = mm_tn(rec["act"], dxb, name="mlp_down_dw", G=1, out_dtype=WIRE_DTYPE).reshape(N_DEV, -1, D_MODEL)
        du = matmul_nt(dxb, rows("mlp_w2", i)[None], name="mlp_down_dx", epi="mlp_du", u=rec["slope"])
        grads["mlp_w1"][i] = mm_tn(rec["h2"], du, name="mlp_up_dw", G=N_DEV, out_dtype=WIRE_DTYPE, a_transposed=True)
        dx1, dx1b, dg = matmul_nt(du, full[("mlp_w1", i)], name="mlp_up_dx", epi="rms_bwd", x=rec["x1"], g=g_mlp[i], dx=dx)
        grads["g_mlp"][i] = dg[0]
        if i == 0:
            flying = [keys[j] for j in second + later]
            srcs = [grads[name][l_] for name, l_ in flying]
            slots = [(big.index(name), l_) for name, l_ in flying]
            zones = [lax.empty((N_DEV, weights[name].shape[0]) + grads[name][-1].shape[1:], grads[name][-1].dtype)
                     for name in big]
            for src, (k, l_) in zip(srcs, slots):
                own = lax.dynamic_index_in_dim(src, me, 0, keepdims=True)[None]
                zones[k] = lax.dynamic_update_slice(zones[k], own, (me, l_) + (0,) * (src.ndim - 1))
            s_send, s_recv, s_srcs, s_zones, token = exchange_start(srcs, zones, slots, name="scatter_rest_start")
        if i % 2 == 0:
            grads["attn_w_o"][l] = mm_tn(rec["o"], dx1b, name="attn_out_dw", G=1, out_dtype=WIRE_DTYPE,
                                         after=token).reshape(N_DEV, -1, D_MODEL)
            do, delta = matmul_nt(dx1b, rows("attn_w_o", l)[None], name="attn_out_dx", u=rec["o"])
            dq, dk, dv = attn_bwd(rec["q"], rec["k"], rec["v"], do, rec["lse"], delta, pos_col, pos_row)
            da, dwuq, dwukv, dgqn, dgkn, dgqa, dgkva = mla_pre_bwd(
                dq, dk, dv, rec["a"], rec["cq"], rec["ckv"], rec["gqa"], rec["gkva"], rec["wuq3"], rec["wukv3"],
                rec["gqn"], rec["gkn"], cos_t, sin_t)
            grads["attn_w_uq"][l] = dwuq[:, :, :QK_DIM].astype(WIRE_DTYPE)
            grads["attn_w_ukv"][l] = dwukv.astype(WIRE_DTYPE)
            grads["attn_g_qnorm"][l] = dgqn[0, :QK_DIM]
            grads["attn_g_knorm"][l] = dgkn[0, :QK_DIM]
            grads["attn_g_q_a"][l] = dgqa[0]
            grads["attn_g_kv_a"][l] = dgkva[0]
            dwd = mm_tn(rec["h"], da, name="attn_down_dw", G=1, out_dtype=WIRE_DTYPE, a_transposed=True)
            grads["attn_w_down"][l] = dwd[0, :, :DOWN].reshape(N_DEV, -1, DOWN)
            dx, dxb, dg = matmul_nt(da, rec["wd3"], name="attn_down_dx", epi="rms_bwd", x=rec["x0"], g=g_mix[i], dx=dx1)
        else:
            grads["conv_w_out"][l] = mm_tn(rec["z"], dx1b, name="conv_out_dw", G=1, out_dtype=WIRE_DTYPE).reshape(N_DEV, -1, D_MODEL)
            dz = matmul_nt(dx1b, rows("conv_w_out", l)[None], name="conv_out_dx")
            dbcu, dcw = conv_bwd(dz, rec["bcu"], rec["cw"])
            grads["conv_w"][l] = dcw.reshape(3, N_DEV, -1).transpose(1, 0, 2)
            grads["conv_w_in"][l] = mm_tn(rec["h"], dbcu, name="conv_in_dw", G=N_DEV, out_dtype=WIRE_DTYPE, a_transposed=True)
            dx, dxb, dg = matmul_nt(dbcu, full[("conv_w_in", l)], name="conv_in_dx", epi="rms_bwd", x=rec["x0"], g=g_mix[i], dx=dx1)
        grads["g_mix"][i] = dg[0]

    sizes = [weights[name].size for name in small]
    n_small = sum(sizes)
    rows_small = -(-n_small // (8 * 128)) * 8

    def pack(tree):
        flat = jnp.concatenate([jnp.stack(tree[name]).reshape(-1) if isinstance(tree[name], list) else tree[name].reshape(-1)
                                for name in small])
        return jnp.pad(flat, (0, rows_small * 128 - n_small)).reshape(rows_small, 128)

    s_srcs, s_zones = exchange_wait(s_send, s_recv, s_srcs, s_zones, slots, dx, name="scatter_rest_wait")
    remote = [(grads[name][l_], big.index(name), l_) for name, l_ in (keys[j] for j in mixer0)]
    parts, gain_parts = scatter_finish(remote, s_zones, pack(grads), name="scatter_last")

    out = {}
    for name, part in zip(big, parts):
        w = weights[name]
        flat = lambda t: t.reshape(-1, t.shape[-1])
        res = adamw(part.reshape(N_DEV, -1, w.shape[-1]), flat(w), flat(mom1[name]), flat(mom2[name]), name="adamw_" + name)
        out[name] = [r.reshape(w.shape) for r in res]
    res = adamw(gain_parts, pack(weights), pack(mom1), pack(mom2), name="adamw_gains")
    offset = 0
    for name, size in zip(small, sizes):
        out[name] = [r.reshape(-1)[offset:offset + size].reshape(weights[name].shape) for r in res]
        offset += size

    return (loss, dx[None], *[out[n][0] for n in order], *[out[n][1] for n in order],
            *[out[n][2] for n in order], *[out[n][3] for n in order])
```

```python
import jax
import jax.numpy as jnp
import numpy as np
from jax import lax
from jax.experimental import pallas as pl
from jax.experimental.pallas import tpu as pltpu

F32 = jnp.float32
MXU_DTYPE = jnp.bfloat16
WIRE_DTYPE = jnp.bfloat16

D_MODEL = 1024
N_HEADS = 8
NOPE = 128
ROPE = 64
QK_DIM = NOPE + ROPE
QK_PAD = 256
V_DIM = 128
Q_LORA = 256
KV_LORA = 128
DOWN = Q_LORA + KV_LORA + ROPE
DOWN_PAD = 512
ROPE_THETA = 10000.0
EPS = 1e-6
SM_SCALE = QK_DIM ** -0.5
LOG2E = 1.4426950408889634
Q_PRESCALE = SM_SCALE * LOG2E
ADAM_LR, ADAM_B1, ADAM_B2, ADAM_EPS, ADAM_WD, ADAM_STEP = 0.001, 0.9, 0.999, 1e-08, 0.01, 10
N_DEV = 8
MESH_AXES = ("x", "y", "c")

TM = 512
TM_WIDE = 1024
TILE_BUDGET = 32 << 20
TM_TOKENS_TN = 2048
TQ = 512
HEADS_FWD = 8
TQ_BWD = 2048
BWD_CHUNK = 256
TROW = 512
HALO = 16
VMEM_LIMIT = 48 << 20

NN = (((1,), (0,)), ((), ()))
NT = (((1,), (1,)), ((), ()))
TN = (((0,), (0,)), ((), ()))


def _dot(a, b, dims=NN):
    return lax.dot_general(a.astype(MXU_DTYPE), b.astype(MXU_DTYPE), dims, preferred_element_type=F32)


def _params(n_axes):
    return pltpu.CompilerParams(dimension_semantics=("arbitrary",) * n_axes, vmem_limit_bytes=VMEM_LIMIT)


def _rms(xv, n):
    r = lax.rsqrt(jnp.sum(xv * xv, axis=-1, keepdims=True) / n + EPS)
    return xv * r, r


def _rms_bwd(dy, xhat, r, g, n):
    dg = jnp.sum(dy * xhat, axis=0, keepdims=True)
    dxh = dy * g
    dx = r * (dxh - xhat * (jnp.sum(dxh * xhat, axis=-1, keepdims=True) / n))
    return dx, dg


def _swap_halves(t):
    lane = lax.broadcasted_iota(jnp.int32, t.shape, 1)
    return jnp.where(lane < ROPE // 2, pltpu.roll(t, 128 - ROPE // 2, 1), pltpu.roll(t, ROPE // 2, 1))


def _rope(t, cos_t, sin_t):
    return t * cos_t + _swap_halves(t) * sin_t


def _rope_bwd(dout, cos_t, sin_t):
    return dout * cos_t + _swap_halves(dout * sin_t)


def _resident(shape):
    return pl.BlockSpec(shape, lambda i: (0,) * len(shape))


def _token_tile(S, row_bytes, resident_bytes):
    wide = min(TM_WIDE, S)
    return wide if 2 * (wide * row_bytes + resident_bytes) <= TILE_BUDGET else min(TM, S)


def norm_matmul(x, g, w3, *, name, mlp=False, out_dtype=F32):
    if mlp:
        out_dtype = MXU_DTYPE
    S, D = x.shape
    G, _, Nb = w3.shape
    N = G * Nb
    n_out = 2 if mlp else 1
    tm = _token_tile(S, D * 4 + D * 2 + n_out * N * jnp.dtype(out_dtype).itemsize, w3.size * w3.dtype.itemsize)

    def body(x_ref, g_ref, w_ref, h_ref, o_ref, *slope_ref):
        xv = x_ref[...]
        r = lax.rsqrt(jnp.mean(xv * xv, axis=-1, keepdims=True) + EPS)
        h = (xv * r * g_ref[...]).astype(h_ref.dtype)
        h_ref[...] = h.T
        for gi in range(G):
            cols = slice(gi * Nb, (gi + 1) * Nb)
            acc = _dot(h, w_ref[gi])
            if mlp:
                acc = jnp.maximum(acc, 0.0)
                slope_ref[0][:, cols] = (2.0 * acc).astype(out_dtype)
                acc = jnp.square(acc)
            o_ref[:, cols] = acc.astype(o_ref.dtype)

    rows = lambda w: pl.BlockSpec((tm, w), lambda i: (i, 0))
    return pl.pallas_call(
        body, name=name, grid=(S // tm,),
        in_specs=[rows(D), _resident((1, D)), _resident((G, D, Nb))],
        out_specs=[pl.BlockSpec((D, tm), lambda i: (0, i))] + [rows(N)] * n_out,
        out_shape=[jax.ShapeDtypeStruct((D, S), MXU_DTYPE)] + [jax.ShapeDtypeStruct((S, N), out_dtype)] * n_out,
        compiler_params=_params(1),
    )(x, g.reshape(1, D), w3)


def matmul_residual(a, w, res, *, name, target=None):
    S, K = a.shape
    _, N = w.shape
    tm = _token_tile(S, K * a.dtype.itemsize + 2 * N * 4 + (0 if target is None else N * 6), w.size * w.dtype.itemsize)

    def body(a_ref, w_ref, r_ref, *rest):
        y = r_ref[...] + _dot(a_ref[...], w_ref[...])
        if target is None:
            rest[0][...] = y
            return
        t_ref, l_ref, dy_ref, dyb_ref = rest
        err = y - t_ref[...]
        dy = err / N
        dy_ref[...] = dy
        dyb_ref[...] = dy.astype(dyb_ref.dtype)
        part = jnp.full((1, 128), jnp.sum(err * err), F32)

        @pl.when(pl.program_id(0) == 0)
        def _():
            l_ref[...] = part

        @pl.when(pl.program_id(0) != 0)
        def _():
            l_ref[...] += part

    rows = lambda w_: pl.BlockSpec((tm, w_), lambda i: (i, 0))
    in_specs, args = [rows(K), _resident((K, N)), rows(N)], [a, w, res]
    if target is None:
        out_specs, out_shape = rows(N), jax.ShapeDtypeStruct((S, N), F32)
    else:
        in_specs.append(rows(N))
        args.append(target)
        out_specs = [_resident((1, 128)), rows(N), rows(N)]
        out_shape = [jax.ShapeDtypeStruct((1, 128), F32), jax.ShapeDtypeStruct((S, N), F32),
                     jax.ShapeDtypeStruct((S, N), MXU_DTYPE)]
    return pl.pallas_call(
        body, name=name, grid=(S // tm,),
        in_specs=in_specs, out_specs=out_specs, out_shape=out_shape,
        compiler_params=_params(1),
    )(*args)


def matmul_nt(a, w3, *, name, epi="plain", u=None, x=None, g=None, dx=None):
    S, N = a.shape
    G, Ko, Nb = w3.shape
    assert N == G * Nb
    with_delta = epi == "plain" and u is not None
    row_bytes = N * a.dtype.itemsize + Ko * ({"plain": 4, "mlp_du": 4, "rms_bwd": 14}[epi] + (4 if with_delta else 0))
    tm = _token_tile(S, row_bytes, w3.size * w3.dtype.itemsize)
    tko = min(Ko, 512)

    def body(a_ref, w_ref, *rest):
        if epi == "mlp_du":
            u_ref, o_ref = rest
            av = a_ref[...].astype(MXU_DTYPE)
            for j in range(Ko // tko):
                cols = slice(j * tko, (j + 1) * tko)
                da = _dot(av, w_ref[0, cols, :], NT)
                o_ref[:, cols] = (da * u_ref[:, cols].astype(F32)).astype(o_ref.dtype)
            return
        acc = _dot(a_ref[:, :Nb], w_ref[0], NT)
        for gi in range(1, G):
            acc = acc + _dot(a_ref[:, gi * Nb:(gi + 1) * Nb], w_ref[gi], NT)
        if epi == "plain":
            if with_delta:
                u_ref, o_ref, d_ref = rest
                for h in range(N_HEADS):
                    cols = slice(h * V_DIM, (h + 1) * V_DIM)
                    d_ref[h] = jnp.sum((u_ref[:, cols] * acc[:, cols]).T, axis=0, keepdims=True)
            else:
                o_ref = rest[0]
            o_ref[...] = acc
        else:
            x_ref, g_ref, dx_ref, o_ref, ob_ref, dg_ref = rest
            xhat, r = _rms(x_ref[...], Ko)
            dxb, dg = _rms_bwd(acc, xhat, r, g_ref[...], Ko)
            dx_new = dx_ref[...] + dxb
            o_ref[...] = dx_new
            ob_ref[...] = dx_new.astype(ob_ref.dtype)

            @pl.when(pl.program_id(0) == 0)
            def _():
                dg_ref[...] = dg

            @pl.when(pl.program_id(0) != 0)
            def _():
                dg_ref[...] += dg

    rows = lambda w_: pl.BlockSpec((tm, w_), lambda i: (i, 0))
    in_specs = [rows(N), _resident((G, Ko, Nb))]
    args = [a, w3]
    if with_delta:
        in_specs.append(rows(Ko))
        args.append(u)
        out_shape = [jax.ShapeDtypeStruct((S, Ko), F32), jax.ShapeDtypeStruct((N_HEADS, 1, S), F32)]
        out_specs = [rows(Ko), pl.BlockSpec((N_HEADS, 1, tm), lambda i: (0, 0, i))]
    elif epi == "plain":
        out_shape, out_specs = jax.ShapeDtypeStruct((S, Ko), F32), rows(Ko)
    elif epi == "mlp_du":
        in_specs.append(rows(Ko))
        args.append(u)
        out_shape, out_specs = jax.ShapeDtypeStruct((S, Ko), MXU_DTYPE), rows(Ko)
    else:
        in_specs += [rows(Ko), _resident((1, Ko)), rows(Ko)]
        args += [x, g.reshape(1, Ko), dx]
        out_shape = [jax.ShapeDtypeStruct((S, Ko), F32), jax.ShapeDtypeStruct((S, Ko), MXU_DTYPE),
                     jax.ShapeDtypeStruct((1, Ko), F32)]
        out_specs = [rows(Ko), rows(Ko), _resident((1, Ko))]
    return pl.pallas_call(
        body, name=name, grid=(S // tm,),
        in_specs=in_specs, out_specs=out_specs, out_shape=out_shape,
        compiler_params=_params(1),
    )(*args)


def mm_tn(a, b, *, name, G, out_dtype, after=None, a_transposed=False):
    order = [] if after is None else [after]
    Ka, S = a.shape if a_transposed else a.shape[::-1]
    _, N = b.shape
    Nb = N // G
    assert Nb <= 1024
    tm = min(TM_TOKENS_TN if b.dtype.itemsize == 2 else TM_TOKENS_TN // 2, S)
    tka = min(Ka, 1024)
    gb = 2 if G % 2 == 0 and Nb <= 512 else 1
    ns = S // tm

    def body(a_ref, b_ref, *rest):
        o_ref, acc = rest[-2:]
        s = pl.program_id(2)

        @pl.when(s == 0)
        def _():
            acc[...] = jnp.zeros_like(acc)

        av = a_ref[...]
        for gi in range(gb):
            acc[gi] += _dot(av, b_ref[:, gi * Nb:(gi + 1) * Nb], NN if a_transposed else TN)

        @pl.when(s == ns - 1)
        def _():
            o_ref[...] = acc[...].astype(o_ref.dtype)

    a_spec = pl.BlockSpec((tka, tm), lambda i, j, s: (i, s)) if a_transposed else pl.BlockSpec((tm, tka), lambda i, j, s: (s, i))
    return pl.pallas_call(
        body, name=name, grid=(Ka // tka, G // gb, ns),
        in_specs=[a_spec,
                  pl.BlockSpec((tm, gb * Nb), lambda i, j, s: (s, j))] + [pl.BlockSpec(memory_space=pl.ANY)] * len(order),
        out_specs=pl.BlockSpec((gb, tka, Nb), lambda i, j, s: (j, i, 0)),
        out_shape=jax.ShapeDtypeStruct((G, Ka, Nb), out_dtype),
        scratch_shapes=[pltpu.VMEM((gb, tka, Nb), F32)],
        compiler_params=_params(3),
    )(a, b, *order)


def mla_pre_fwd(a, gqa, gkva, wuq3, wukv3, gqn, gkn, cos_t, sin_t):
    S = a.shape[0]
    tm = min(TM, S)
    H = N_HEADS

    def body(a_ref, gqa_ref, gkva_ref, wuq_ref, wukv_ref, gqn_ref, gkn_ref, cos_ref, sin_ref,
             q_ref, k_ref, v_ref, cq_ref, ckv_ref):
        av = a_ref[...]
        cq = (_rms(av[:, :Q_LORA], Q_LORA)[0] * gqa_ref[...]).astype(cq_ref.dtype)
        ckv = (_rms(av[:, Q_LORA:Q_LORA + KV_LORA], KV_LORA)[0] * gkva_ref[...]).astype(ckv_ref.dtype)
        cq_ref[...] = cq
        ckv_ref[...] = ckv
        kpe = av[:, Q_LORA + KV_LORA:]
        cos_v, sin_v = cos_ref[...], sin_ref[...]
        for h in range(H):
            qn = _rms(_dot(cq, wuq_ref[h]), QK_DIM)[0] * gqn_ref[...]
            qr = jnp.concatenate([qn[:, :NOPE], _rope(qn[:, NOPE:], cos_v, sin_v)], axis=1)
            q_ref[h] = (qr * Q_PRESCALE).astype(q_ref.dtype)
            kvp = _dot(ckv, wukv_ref[h])
            kn = _rms(jnp.concatenate([kvp[:, :NOPE], kpe], axis=1), QK_DIM)[0] * gkn_ref[...]
            k_ref[h] = jnp.concatenate([kn[:, :NOPE], _rope(kn[:, NOPE:], cos_v, sin_v)], axis=1).astype(k_ref.dtype)
            v_ref[h] = kvp[:, NOPE:].astype(v_ref.dtype)

    row = lambda w: pl.BlockSpec((tm, w), lambda i: (i, 0))
    heads = lambda w: pl.BlockSpec((H, tm, w), lambda i: (0, i, 0))
    return pl.pallas_call(
        body, name="mla_pre_fwd", grid=(S // tm,),
        in_specs=[row(DOWN_PAD), _resident((1, Q_LORA)), _resident((1, KV_LORA)),
                  _resident((H, Q_LORA, QK_PAD)), _resident((H, KV_LORA, NOPE + V_DIM)),
                  _resident((1, QK_PAD)), _resident((1, QK_PAD)), row(128), row(128)],
        out_specs=[heads(QK_PAD), heads(QK_PAD), heads(V_DIM), row(Q_LORA), row(KV_LORA)],
        out_shape=[jax.ShapeDtypeStruct((H, S, QK_PAD), MXU_DTYPE),
                   jax.ShapeDtypeStruct((H, S, QK_PAD), MXU_DTYPE),
                   jax.ShapeDtypeStruct((H, S, V_DIM), MXU_DTYPE),
                   jax.ShapeDtypeStruct((S, Q_LORA), MXU_DTYPE),
                   jax.ShapeDtypeStruct((S, KV_LORA), MXU_DTYPE)],
        compiler_params=_params(1),
    )(a, gqa, gkva, wuq3, wukv3, gqn, gkn, cos_t, sin_t)


def mla_pre_bwd(dq, dk, dv, a, cq, ckv, gqa, gkva, wuq3, wukv3, gqn, gkn, cos_t, sin_t):
    S = a.shape[0]
    tm = min(TM, S)
    H = N_HEADS

    def body(dq_ref, dk_ref, dv_ref, a_ref, cq_ref, ckv_ref, gqa_ref, gkva_ref, wuq_ref, wukv_ref, gqn_ref, gkn_ref,
             cos_ref, sin_ref, da_ref, dwuq_ref, dwukv_ref, dgqn_ref, dgkn_ref, dgqa_ref, dgkva_ref):
        @pl.when(pl.program_id(0) == 0)
        def _():
            for ref in (dwuq_ref, dwukv_ref, dgqn_ref, dgkn_ref, dgqa_ref, dgkva_ref):
                ref[...] = jnp.zeros_like(ref)

        av = a_ref[...]
        kpe = av[:, Q_LORA + KV_LORA:]
        cos_v, sin_v = cos_ref[...], sin_ref[...]
        cqv, ckvv = cq_ref[...], ckv_ref[...]
        dcq = jnp.zeros((tm, Q_LORA), F32)
        dckv = jnp.zeros((tm, KV_LORA), F32)
        dkpe = jnp.zeros((tm, 128), F32)
        dgqn = jnp.zeros((1, QK_PAD), F32)
        dgkn = jnp.zeros((1, QK_PAD), F32)
        up = lambda h: (_dot(cqv, wuq_ref[h]), _dot(ckvv, wukv_ref[h]))
        nxt = up(0)
        for h in range(H):
            wuq, wukv = wuq_ref[h], wukv_ref[h]
            qp, kvp = nxt
            if h + 1 < H:
                nxt = up(h + 1)
            qhat, rq = _rms(qp, QK_DIM)
            dqr = dq_ref[h] * SM_SCALE
            dqn = jnp.concatenate([dqr[:, :NOPE], _rope_bwd(dqr[:, NOPE:], cos_v, sin_v)], axis=1)
            dqp, dg = _rms_bwd(dqn, qhat, rq, gqn_ref[...], QK_DIM)
            dgqn = dgqn + dg
            dqp = dqp.astype(MXU_DTYPE)
            dwuq_ref[h] += _dot(cqv, dqp, TN)
            dcq = dcq + _dot(dqp, wuq, NT)
            khat, rk = _rms(jnp.concatenate([kvp[:, :NOPE], kpe], axis=1), QK_DIM)
            dkr = dk_ref[h] * (1.0 / LOG2E)
            dkn = jnp.concatenate([dkr[:, :NOPE], _rope_bwd(dkr[:, NOPE:], cos_v, sin_v)], axis=1)
            dkk, dg = _rms_bwd(dkn, khat, rk, gkn_ref[...], QK_DIM)
            dgkn = dgkn + dg
            dkpe = dkpe + dkk[:, NOPE:]
            dkvp = jnp.concatenate([dkk[:, :NOPE], dv_ref[h]], axis=1).astype(MXU_DTYPE)
            dwukv_ref[h] += _dot(ckvv, dkvp, TN)
            dckv = dckv + _dot(dkvp, wukv, NT)
        dgqn_ref[...] += dgqn
        dgkn_ref[...] += dgkn
        ahat, r = _rms(av[:, :Q_LORA], Q_LORA)
        daq, dg = _rms_bwd(dcq, ahat, r, gqa_ref[...], Q_LORA)
        dgqa_ref[...] += dg
        ahat, r = _rms(av[:, Q_LORA:Q_LORA + KV_LORA], KV_LORA)
        dakv, dg = _rms_bwd(dckv, ahat, r, gkva_ref[...], KV_LORA)
        dgkva_ref[...] += dg
        da_ref[...] = jnp.concatenate([daq, dakv, dkpe], axis=1)

    row = lambda w: pl.BlockSpec((tm, w), lambda i: (i, 0))
    heads = lambda w: pl.BlockSpec((H, tm, w), lambda i: (0, i, 0))
    return pl.pallas_call(
        body, name="mla_pre_bwd", grid=(S // tm,),
        in_specs=[heads(QK_PAD), heads(QK_PAD), heads(V_DIM), row(DOWN_PAD), row(Q_LORA), row(KV_LORA),
                  _resident((1, Q_LORA)), _resident((1, KV_LORA)),
                  _resident((H, Q_LORA, QK_PAD)), _resident((H, KV_LORA, NOPE + V_DIM)),
                  _resident((1, QK_PAD)), _resident((1, QK_PAD)), row(128), row(128)],
        out_specs=[row(DOWN_PAD), _resident((H, Q_LORA, QK_PAD)), _resident((H, KV_LORA, NOPE + V_DIM)),
                   _resident((1, QK_PAD)), _resident((1, QK_PAD)), _resident((1, Q_LORA)), _resident((1, KV_LORA))],
        out_shape=[jax.ShapeDtypeStruct((S, DOWN_PAD), F32),
                   jax.ShapeDtypeStruct((H, Q_LORA, QK_PAD), F32),
                   jax.ShapeDtypeStruct((H, KV_LORA, NOPE + V_DIM), F32),
                   jax.ShapeDtypeStruct((1, QK_PAD), F32), jax.ShapeDtypeStruct((1, QK_PAD), F32),
                   jax.ShapeDtypeStruct((1, Q_LORA), F32), jax.ShapeDtypeStruct((1, KV_LORA), F32)],
        compiler_params=_params(1),
    )(dq, dk, dv, a, cq, ckv, gqa, gkva, wuq3, wukv3, gqn, gkn, cos_t, sin_t)


def _pair_tables(nb, key_major):
    if key_major:
        pairs = [(qi, kj) for kj in range(nb) for qi in range(kj, nb)]
    else:
        pairs = [(qi, ki) for qi in range(nb) for ki in range(qi + 1)]
    return (jnp.asarray(np.array([p[0] for p in pairs], np.int32)),
            jnp.asarray(np.array([p[1] for p in pairs], np.int32)))


def _scores_t(k, q, pk_col, pq_row, masked):
    s = _dot(k, q, NT)
    return jnp.where(pq_row >= pk_col, s, jnp.finfo(F32).min) if masked else s


def attn_fwd(q, k, v, pos_col, pos_row):
    H, S, _ = q.shape
    t = min(TQ, S)
    nb = S // t
    hb = HEADS_FWD
    qt, kt = _pair_tables(nb, key_major=False)

    def body(qt_ref, kt_ref, q_ref, k_ref, v_ref, pk_ref, pq_ref, o_ref, lse_ref, m_s, l_s, acc):
        step = pl.program_id(1)
        qi, ki = qt_ref[step], kt_ref[step]

        @pl.when(ki == 0)
        def _():
            m_s[...] = jnp.full_like(m_s, -jnp.inf)
            l_s[...] = jnp.zeros_like(l_s)
            acc[...] = jnp.zeros_like(acc)

        def update(masked):
            scores = lambda hh: _scores_t(k_ref[hh], q_ref[hh], pk_ref[...], pq_ref[...], masked)
            def weighted_values(hh, p, alpha):
                acc[hh] = alpha * acc[hh] + _dot(v_ref[hh], p, TN)

            s_next = scores(0)
            pending = None
            for hh in range(hb):
                s = s_next
                if hh + 1 < hb:
                    s_next = scores(hh + 1)
                m_old = m_s[hh]
                m_new = jnp.maximum(m_old, jnp.max(s, axis=0, keepdims=True))
                p = jnp.exp2(s - m_new)
                alpha = jnp.exp2(m_old - m_new)
                l_s[hh] = alpha * l_s[hh] + jnp.sum(p, axis=0, keepdims=True)
                m_s[hh] = m_new
                if pending is not None:
                    weighted_values(*pending)
                pending = (hh, p, alpha)
            weighted_values(*pending)

        @pl.when(ki < qi)
        def _():
            update(False)

        @pl.when(ki == qi)
        def _():
            update(True)
            for hh in range(hb):
                o_ref[:, hh * V_DIM:(hh + 1) * V_DIM] = (acc[hh] / l_s[hh]).T
                lse_ref[hh] = m_s[hh] + jnp.log(l_s[hh]) * LOG2E

    grid_spec = pltpu.PrefetchScalarGridSpec(
        num_scalar_prefetch=2, grid=(H // hb, qt.shape[0]),
        in_specs=[pl.BlockSpec((hb, t, QK_PAD), lambda h, s, qt, kt: (h, qt[s], 0)),
                  pl.BlockSpec((hb, t, QK_PAD), lambda h, s, qt, kt: (h, kt[s], 0)),
                  pl.BlockSpec((hb, t, V_DIM), lambda h, s, qt, kt: (h, kt[s], 0)),
                  pl.BlockSpec((t, 1), lambda h, s, qt, kt: (kt[s], 0)),
                  pl.BlockSpec((1, t), lambda h, s, qt, kt: (0, qt[s]))],
        out_specs=[pl.BlockSpec((t, hb * V_DIM), lambda h, s, qt, kt: (qt[s], h)),
                   pl.BlockSpec((hb, 1, t), lambda h, s, qt, kt: (h, 0, qt[s]))],
        scratch_shapes=[pltpu.VMEM((hb, 1, t), F32), pltpu.VMEM((hb, 1, t), F32), pltpu.VMEM((hb, V_DIM, t), F32)])
    return pl.pallas_call(
        body, name="attn_fwd", grid_spec=grid_spec,
        out_shape=[jax.ShapeDtypeStruct((S, H * V_DIM), F32), jax.ShapeDtypeStruct((H, 1, S), F32)],
        compiler_params=_params(2),
    )(qt, kt, q, k, v, pos_col, pos_row)


def attn_bwd(q, k, v, do, lse, delta, pos_col, pos_row):
    H, S, _ = q.shape
    t = min(TQ_BWD, S)
    nb = S // t
    qt, kt = _pair_tables(nb, key_major=True)
    tc = min(BWD_CHUNK, t)

    def body(qt_ref, kt_ref, q_ref, k_ref, v_ref, do_ref, lse_ref, dl_ref, pk_ref, pq_ref, dq_ref, dk_ref, dv_ref):
        step = pl.program_id(1)
        qi, kj = qt_ref[step], kt_ref[step]

        @pl.when(step == 0)
        def _():
            dq_ref[...] = jnp.zeros_like(dq_ref)

        @pl.when(qi == kj)
        def _():
            dk_ref[...] = jnp.zeros_like(dk_ref)
            dv_ref[...] = jnp.zeros_like(dv_ref)

        def update(masked):
            seen = lambda c: (c + 1) * tc if masked else t

            def first_matmuls(c):
                cols, ke = slice(c * tc, (c + 1) * tc), seen(c)
                qc = q_ref[cols, :]
                doc = do_ref[cols, :].astype(MXU_DTYPE)
                s = _scores_t(k_ref[:ke, :], qc, pk_ref[:ke, :], pq_ref[:, cols], masked)
                return qc, doc, s, _dot(v_ref[:ke, :], doc, NT)

            nxt = first_matmuls(0)
            for c in range(t // tc):
                qc, doc, s, dp = nxt
                if c + 1 < t // tc:
                    nxt = first_matmuls(c + 1)
                cols, ke = slice(c * tc, (c + 1) * tc), seen(c)
                p = jnp.exp2(s - lse_ref[:, cols])
                ds = (p * (dp - dl_ref[:, cols])).astype(MXU_DTYPE)
                dv_ref[:ke, :] += _dot(p, doc)
                dk_ref[:ke, :] += _dot(ds, qc)
                rows = pl.ds(pl.multiple_of(qi * t + c * tc, tc), tc)
                dq_ref[rows, :] += _dot(ds, k_ref[:ke, :], TN)

        @pl.when(qi == kj)
        def _():
            update(True)

        @pl.when(qi != kj)
        def _():
            update(False)

    q_idx = lambda h, s, qt, kt: (h, qt[s], 0)
    k_idx = lambda h, s, qt, kt: (h, kt[s], 0)
    row_idx = lambda h, s, qt, kt: (h, 0, qt[s])
    grid_spec = pltpu.PrefetchScalarGridSpec(
        num_scalar_prefetch=2, grid=(H, qt.shape[0]),
        in_specs=[pl.BlockSpec((None, t, QK_PAD), q_idx),
                  pl.BlockSpec((None, t, QK_PAD), k_idx),
                  pl.BlockSpec((None, t, V_DIM), k_idx),
                  pl.BlockSpec((t, V_DIM), lambda h, s, qt, kt: (qt[s], h)),
                  pl.BlockSpec((None, 1, t), row_idx),
                  pl.BlockSpec((None, 1, t), row_idx),
                  pl.BlockSpec((t, 1), lambda h, s, qt, kt: (kt[s], 0)),
                  pl.BlockSpec((1, t), lambda h, s, qt, kt: (0, qt[s]))],
        out_specs=[pl.BlockSpec((None, S, QK_PAD), lambda h, s, qt, kt: (h, 0, 0)),
                   pl.BlockSpec((None, t, QK_PAD), k_idx),
                   pl.BlockSpec((None, t, V_DIM), k_idx)])
    return pl.pallas_call(
        body, name="attn_bwd", grid_spec=grid_spec,
        out_shape=[jax.ShapeDtypeStruct((H, S, QK_PAD), F32), jax.ShapeDtypeStruct((H, S, QK_PAD), F32),
                   jax.ShapeDtypeStruct((H, S, V_DIM), F32)],
        compiler_params=_params(2),
    )(qt, kt, q, k, v, do, lse, delta, pos_col, pos_row)


def _conv_specs(S, tr):
    hb = tr // HALO
    main = lambda third: pl.BlockSpec((tr, D_MODEL), lambda r: (r, third))
    prev = lambda third: pl.BlockSpec((HALO, D_MODEL), lambda r: (jnp.maximum(r * hb - 1, 0), third))
    nxt = lambda third: pl.BlockSpec((HALO, D_MODEL), lambda r: (jnp.minimum((r + 1) * hb, S // HALO - 1), third))
    return main, prev, nxt


def _f32(ref):
    return ref[...].astype(F32)


def _conv_taps(gc, uu, w_ref, first):
    u2 = gc * uu
    rows = lax.broadcasted_iota(jnp.int32, u2.shape, 0)
    u2 = jnp.where((rows < HALO) & first, 0.0, u2)
    s1 = pltpu.roll(u2, 1, 0)
    s2 = pltpu.roll(u2, 2, 0)
    u3 = w_ref[2:3, :] * u2 + w_ref[1:2, :] * s1 + w_ref[0:1, :] * s2
    return u2, s1, s2, u3


def conv_fwd(bcu, cw):
    S = bcu.shape[0]
    tr = min(TROW, S)
    main, prev, _ = _conv_specs(S, tr)

    def body(gb_ref, gc_ref, u_ref, gch_ref, uh_ref, w_ref, z_ref):
        gc = jnp.concatenate([_f32(gch_ref), _f32(gc_ref)], axis=0)
        uu = jnp.concatenate([_f32(uh_ref), _f32(u_ref)], axis=0)
        u3 = _conv_taps(gc, uu, w_ref, pl.program_id(0) == 0)[3]
        z_ref[...] = (_f32(gb_ref) * u3[HALO:]).astype(z_ref.dtype)

    return pl.pallas_call(
        body, name="conv_fwd", grid=(S // tr,),
        in_specs=[main(0), main(1), main(2), prev(1), prev(2), _resident((3, D_MODEL))],
        out_specs=main(0),
        out_shape=jax.ShapeDtypeStruct((S, D_MODEL), MXU_DTYPE),
        compiler_params=_params(1),
    )(bcu, bcu, bcu, bcu, bcu, cw)


def conv_bwd(dz, bcu, cw):
    S = bcu.shape[0]
    tr = min(TROW, S)
    nr = S // tr
    main, prev, nxt = _conv_specs(S, tr)

    def body(dz_ref, dzn_ref, gb_ref, gbn_ref, gc_ref, u_ref, gch_ref, uh_ref, w_ref, o_ref, dw_ref):
        r = pl.program_id(0)
        gcv, uv = _f32(gc_ref), _f32(u_ref)
        gc = jnp.concatenate([_f32(gch_ref), gcv], axis=0)
        uu = jnp.concatenate([_f32(uh_ref), uv], axis=0)
        u2, s1, s2, u3 = _conv_taps(gc, uu, w_ref, r == 0)
        dzv = dz_ref[...]
        du3 = jnp.concatenate([dzv * _f32(gb_ref), dzn_ref[...] * _f32(gbn_ref)], axis=0)
        rows = lax.broadcasted_iota(jnp.int32, du3.shape, 0)
        du3 = jnp.where((rows >= tr) & (r == nr - 1), 0.0, du3)
        n1 = pltpu.roll(du3, tr + HALO - 1, 0)
        n2 = pltpu.roll(du3, tr + HALO - 2, 0)
        du2 = (w_ref[2:3, :] * du3 + w_ref[1:2, :] * n1 + w_ref[0:1, :] * n2)[:tr]
        o_ref[:, :D_MODEL] = (dzv * u3[HALO:]).astype(o_ref.dtype)
        o_ref[:, D_MODEL:2 * D_MODEL] = (du2 * uv).astype(o_ref.dtype)
        o_ref[:, 2 * D_MODEL:] = (du2 * gcv).astype(o_ref.dtype)
        d3 = du3[:tr]
        taps = [jnp.sum(d3 * t[HALO:], axis=0, keepdims=True) for t in (s2, s1, u2)]

        @pl.when(r == 0)
        def _():
            for kk in range(3):
                dw_ref[kk:kk + 1, :] = taps[kk]

        @pl.when(r != 0)
        def _():
            for kk in range(3):
                dw_ref[kk:kk + 1, :] += taps[kk]

    return pl.pallas_call(
        body, name="conv_bwd", grid=(nr,),
        in_specs=[main(0), nxt(0), main(0), nxt(0), main(1), main(2), prev(1), prev(2), _resident((3, D_MODEL))],
        out_specs=[pl.BlockSpec((tr, 3 * D_MODEL), lambda r: (r, 0)), _resident((3, D_MODEL))],
        out_shape=[jax.ShapeDtypeStruct((S, 3 * D_MODEL), MXU_DTYPE), jax.ShapeDtypeStruct((3, D_MODEL), F32)],
        compiler_params=_params(1),
    )(dz, dz, bcu, bcu, bcu, bcu, bcu, bcu, cw)


def adamw(parts, w, m, v, *, name):
    R, C = w.shape
    tr = R
    while tr * C * 4 > (1 << 20) and tr % 32 == 0:
        tr //= 2

    def body(p_ref, w_ref, m_ref, v_ref, g_ref, d_ref, mo_ref, vo_ref):
        g = p_ref[0].astype(F32)
        for d in range(1, N_DEV):
            g = g + p_ref[d].astype(F32)
        m_new = ADAM_B1 * m_ref[...] + (1.0 - ADAM_B1) * g
        v_new = ADAM_B2 * v_ref[...] + (1.0 - ADAM_B2) * (g * g)
        m_hat = m_new / (1.0 - ADAM_B1 ** ADAM_STEP)
        v_hat = v_new / (1.0 - ADAM_B2 ** ADAM_STEP)
        g_ref[...] = g
        d_ref[...] = -ADAM_LR * (m_hat / (jnp.sqrt(v_hat) + ADAM_EPS) + ADAM_WD * w_ref[...])
        mo_ref[...] = m_new
        vo_ref[...] = v_new

    blk = pl.BlockSpec((tr, C), lambda i: (i, 0))
    return pl.pallas_call(
        body, name=name, grid=(R // tr,),
        in_specs=[pl.BlockSpec((N_DEV, tr, C), lambda i: (0, i, 0)), blk, blk, blk],
        out_specs=[blk, blk, blk, blk],
        out_shape=[jax.ShapeDtypeStruct((R, C), F32)] * 4,
        compiler_params=_params(1),
    )(parts, w, m, v)


def _mesh_place():
    x, y, c = (lax.axis_index(n) for n in MESH_AXES)
    return x, y, c, 4 * x + 2 * y + c


def _peer(x, y, c, d):
    px = 1 - x if d & 4 else x
    py = 1 - y if d & 2 else y
    pc = 1 - c if d & 1 else c
    return (px, py, pc), 4 * px + 2 * py + pc


def gather_now(srcs, *, name):
    n = len(srcs)
    any_spec = pl.BlockSpec(memory_space=pl.ANY)

    def body(*refs):
        ins, outs, token = refs[:n], refs[n:2 * n], refs[2 * n]
        send_sems, recv_sems, local_sems = refs[2 * n + 1:]
        token[...] = jnp.zeros_like(token)
        x, y, c, me = _mesh_place()
        for a in range(n):
            pltpu.make_async_copy(ins[a], outs[a].at[me], local_sems.at[a]).start()
            for d in range(1, N_DEV):
                pltpu.make_async_remote_copy(
                    src_ref=ins[a], dst_ref=outs[a].at[me], send_sem=send_sems.at[a], recv_sem=recv_sems.at[a],
                    device_id=_peer(x, y, c, d)[0], device_id_type=pl.DeviceIdType.MESH).start()
        for a in range(n):
            pltpu.make_async_copy(ins[a], outs[a].at[me], local_sems.at[a]).wait()
            seven = outs[a].at[pl.ds(0, N_DEV - 1)]
            drain = pltpu.make_async_remote_copy(
                src_ref=seven, dst_ref=seven, send_sem=send_sems.at[a], recv_sem=recv_sems.at[a],
                device_id=(x, y, c), device_id_type=pl.DeviceIdType.MESH)
            drain.wait_send()
            drain.wait_recv()

    out = pl.pallas_call(
        body, name=name,
        in_specs=[any_spec] * n, out_specs=[any_spec] * n + [pl.BlockSpec(memory_space=pltpu.VMEM)],
        out_shape=[jax.ShapeDtypeStruct((N_DEV,) + s.shape, s.dtype) for s in srcs] + [jax.ShapeDtypeStruct((8, 128), F32)],
        scratch_shapes=[pltpu.SemaphoreType.DMA((n,)), pltpu.SemaphoreType.DMA((n,)), pltpu.SemaphoreType.DMA((n,))],
    )(*srcs)
    return out[:n], out[n]


_ANY = pl.BlockSpec(memory_space=pl.ANY)
_HBM = pl.BlockSpec(memory_space=pltpu.HBM)
_SEM = pl.BlockSpec(memory_space=pltpu.SEMAPHORE)


def _in_hbm(arrays):
    return [pltpu.with_memory_space_constraint(a, pltpu.HBM) for a in arrays]


def exchange_start(srcs, lands, slots, *, name):
    n, m = len(srcs), len(lands)

    def body(*refs):
        ins, zones = refs[:n], refs[n:n + m]
        send_sems, recv_sems, token = refs[n + m], refs[n + m + 1], refs[-1]
        x, y, c, me = _mesh_place()
        for a in range(n):
            for d in range(1, N_DEV):
                peer, peer_lin = _peer(x, y, c, d)
                src = ins[a] if slots is None else ins[a].at[peer_lin]
                dst = zones[a].at[me] if slots is None else zones[slots[a][0]].at[me, slots[a][1]]
                pltpu.make_async_remote_copy(
                    src_ref=src, dst_ref=dst, send_sem=send_sems.at[a], recv_sem=recv_sems.at[a],
                    device_id=peer, device_id_type=pl.DeviceIdType.MESH).start()
        token[...] = jnp.zeros_like(token)

    both = list(srcs) + list(lands)
    out = pl.pallas_call(
        body, name=name,
        in_specs=[_HBM] * (n + m),
        out_specs=[_SEM, _SEM] + [_HBM] * (n + m) + [pl.BlockSpec(memory_space=pltpu.VMEM)],
        out_shape=[pltpu.SemaphoreType.DMA((n,)), pltpu.SemaphoreType.DMA((n,))]
        + [pltpu.HBM(a.shape, a.dtype) for a in both] + [jax.ShapeDtypeStruct((8, 128), F32)],
        input_output_aliases={i: 2 + i for i in range(n + m)},
        compiler_params=pltpu.CompilerParams(has_side_effects=pltpu.SideEffectType.DATAFLOW_SIDE_EFFECTING),
    )(*_in_hbm(both))
    return out[0], out[1], out[2:2 + n], out[2 + n:2 + n + m], out[-1]


def exchange_wait(send_sems, recv_sems, srcs, lands, slots, after, *, name):
    n, m = len(srcs), len(lands)

    def body(*refs):
        ins, zones = refs[:n], refs[n:n + m]
        send_ref, recv_ref = refs[n + m], refs[n + m + 1]
        x, y, c, _ = _mesh_place()
        for a in range(n):
            seven = (zones[a] if slots is None else ins[a]).at[pl.ds(0, N_DEV - 1)]
            drain = pltpu.make_async_remote_copy(
                src_ref=seven, dst_ref=seven, send_sem=send_ref.at[a], recv_sem=recv_ref.at[a],
                device_id=(x, y, c), device_id_type=pl.DeviceIdType.MESH)
            drain.wait_send()
            drain.wait_recv()

    both = list(srcs) + list(lands)
    out = pl.pallas_call(
        body, name=name,
        in_specs=[_HBM] * (n + m) + [_SEM, _SEM, _ANY],
        out_specs=[_HBM] * (n + m),
        out_shape=[pltpu.HBM(a.shape, a.dtype) for a in both],
        input_output_aliases={i: i for i in range(n + m)},
        compiler_params=pltpu.CompilerParams(has_side_effects=pltpu.SideEffectType.DATAFLOW_SIDE_EFFECTING),
    )(*both, send_sems, recv_sems, after)
    return out[:n], out[n:]


def scatter_finish(remote, lands, vec, *, name):
    n, m = len(remote), len(lands)

    def body(*refs):
        ins, vec_ref, zones_in = refs[:n], refs[n], refs[n + 1:n + 1 + m]
        vec_out = refs[n + 1 + 2 * m]
        send_sems, recv_sems, local_sems = refs[n + 2 + 2 * m:]
        x, y, c, me = _mesh_place()

        def ends(a, j):
            if a == n:
                return vec_ref, vec_out.at[me]
            return ins[a].at[j], zones_in[remote[a][1]].at[me, remote[a][2]]

        for a in range(n + 1):
            pltpu.make_async_copy(*ends(a, me), local_sems.at[a]).start()
            for d in range(1, N_DEV):
                peer, peer_lin = _peer(x, y, c, d)
                src, dst = ends(a, peer_lin)
                pltpu.make_async_remote_copy(
                    src_ref=src, dst_ref=dst, send_sem=send_sems.at[a], recv_sem=recv_sems.at[a],
                    device_id=peer, device_id_type=pl.DeviceIdType.MESH).start()
        for a in range(n + 1):
            pltpu.make_async_copy(*ends(a, me), local_sems.at[a]).wait()
            seven = (vec_out if a == n else ins[a]).at[pl.ds(0, N_DEV - 1)]
            drain = pltpu.make_async_remote_copy(
                src_ref=seven, dst_ref=seven, send_sem=send_sems.at[a], recv_sem=recv_sems.at[a],
                device_id=(x, y, c), device_id_type=pl.DeviceIdType.MESH)
            drain.wait_send()
            drain.wait_recv()

    out = pl.pallas_call(
        body, name=name,
        in_specs=[_ANY] * (n + 1 + m), out_specs=[_ANY] * (m + 1),
        out_shape=[jax.ShapeDtypeStruct(z.shape, z.dtype) for z in lands]
        + [jax.ShapeDtypeStruct((N_DEV,) + vec.shape, vec.dtype)],
        input_output_aliases={n + 1 + i: i for i in range(m)},
        scratch_shapes=[pltpu.SemaphoreType.DMA((n + 1,))] * 3,
    )(*[e[0] for e in remote], vec, *lands)
    return out[:m], out[m]


def _rope_tables(pos):
    inv_freq = ROPE_THETA ** (-jnp.arange(0, ROPE, 2, dtype=F32) / ROPE)
    ang = pos.astype(F32)[:, None] * inv_freq
    cos, sin = jnp.cos(ang), jnp.sin(ang)
    pad = jnp.zeros((pos.shape[0], 128 - ROPE), F32)
    return jnp.concatenate([cos, cos, pad + 1.0], axis=1), jnp.concatenate([-sin, sin, pad], axis=1)


def _pad_last(w, n):
    return jnp.pad(w, [(0, 0)] * (w.ndim - 1) + [(0, n - w.shape[-1])])


def kernel(x, positions, g_mix, g_mlp, attn_w_down, attn_g_q_a, attn_g_kv_a, attn_w_uq, attn_w_ukv, attn_g_qnorm, attn_g_knorm, attn_w_o, conv_w_in, conv_w, conv_w_out, mlp_w1, mlp_w2, loss_target, m_g_mix, m_g_mlp, m_attn_w_down, m_attn_g_q_a, m_attn_g_kv_a, m_attn_w_uq, m_attn_w_ukv, m_attn_g_qnorm, m_attn_g_knorm, m_attn_w_o, m_conv_w_in, m_conv_w, m_conv_w_out, m_mlp_w1, m_mlp_w2, v_g_mix, v_g_mlp, v_attn_w_down, v_attn_g_q_a, v_attn_g_kv_a, v_attn_w_uq, v_attn_w_ukv, v_attn_g_qnorm, v_attn_g_knorm, v_attn_w_o, v_conv_w_in, v_conv_w, v_conv_w_out, v_mlp_w1, v_mlp_w2):
    weights = dict(g_mix=g_mix, g_mlp=g_mlp, attn_w_down=attn_w_down, attn_g_q_a=attn_g_q_a, attn_g_kv_a=attn_g_kv_a,
                   attn_w_uq=attn_w_uq, attn_w_ukv=attn_w_ukv, attn_g_qnorm=attn_g_qnorm, attn_g_knorm=attn_g_knorm,
                   attn_w_o=attn_w_o, conv_w_in=conv_w_in, conv_w=conv_w, conv_w_out=conv_w_out, mlp_w1=mlp_w1, mlp_w2=mlp_w2)
    mom1 = dict(g_mix=m_g_mix, g_mlp=m_g_mlp, attn_w_down=m_attn_w_down, attn_g_q_a=m_attn_g_q_a, attn_g_kv_a=m_attn_g_kv_a,
                attn_w_uq=m_attn_w_uq, attn_w_ukv=m_attn_w_ukv, attn_g_qnorm=m_attn_g_qnorm, attn_g_knorm=m_attn_g_knorm,
                attn_w_o=m_attn_w_o, conv_w_in=m_conv_w_in, conv_w=m_conv_w, conv_w_out=m_conv_w_out, mlp_w1=m_mlp_w1, mlp_w2=m_mlp_w2)
    mom2 = dict(g_mix=v_g_mix, g_mlp=v_g_mlp, attn_w_down=v_attn_w_down, attn_g_q_a=v_attn_g_q_a, attn_g_kv_a=v_attn_g_kv_a,
                attn_w_uq=v_attn_w_uq, attn_w_ukv=v_attn_w_ukv, attn_g_qnorm=v_attn_g_qnorm, attn_g_knorm=v_attn_g_knorm,
                attn_w_o=v_attn_w_o, conv_w_in=v_conv_w_in, conv_w=v_conv_w, conv_w_out=v_conv_w_out, mlp_w1=v_mlp_w1, mlp_w2=v_mlp_w2)
    big = ["attn_w_down", "attn_w_uq", "attn_w_ukv", "attn_w_o", "conv_w_in", "conv_w", "conv_w_out", "mlp_w1", "mlp_w2"]
    small = ["g_mix", "g_mlp", "attn_g_q_a", "attn_g_kv_a", "attn_g_qnorm", "attn_g_knorm"]
    order = ["g_mix", "g_mlp", "attn_w_down", "attn_g_q_a", "attn_g_kv_a", "attn_w_uq", "attn_w_ukv", "attn_g_qnorm",
             "attn_g_knorm", "attn_w_o", "conv_w_in", "conv_w", "conv_w_out", "mlp_w1", "mlp_w2"]

    xs = x[0]
    pos = positions[0]
    target = loss_target[0]
    S = xs.shape[0]
    depth = g_mix.shape[0]
    cos_t, sin_t = _rope_tables(pos)
    pos_col, pos_row = pos.reshape(S, 1), pos.reshape(1, S)

    keys, shards = [], []
    for name in big:
        for l in range(weights[name].shape[0]):
            keys.append((name, l))
            shards.append(weights[name][l] if name == "conv_w" else weights[name][l].astype(WIRE_DTYPE))
    mixer0 = [j for j, (name, l) in enumerate(keys) if l == 0 and name.startswith("attn")]
    first = [j for j in mixer0 if keys[j][0] == "attn_w_down"]
    first_rest = [j for j in mixer0 if j not in first]
    second = [j for j, (name, l) in enumerate(keys) if l == 0 and name.startswith("mlp")]
    later = [j for j in range(len(keys)) if j not in mixer0 + second]
    me = 4 * lax.axis_index("x") + 2 * lax.axis_index("y") + lax.axis_index("c")

    def zones_with_own(js, token):
        return [lax.dynamic_update_slice(lax.empty((N_DEV,) + shards[j].shape, shards[j].dtype),
                                         (shards[j] + token[0, 0].astype(shards[j].dtype))[None],
                                         (me,) + (0,) * shards[j].ndim) for j in js]

    arrived, token = gather_now([shards[j] for j in first], name="gather_first")
    full = dict(zip([keys[j] for j in first], arrived))
    g0 = exchange_start([shards[j] for j in first_rest], zones_with_own(first_rest, token), None, name="gather_mixer0_start")
    g1 = exchange_start([shards[j] for j in second], zones_with_own(second, g0[4]), None, name="gather_mlp0_start")
    layer_of = lambda j: keys[j][1] if keys[j][0].startswith("mlp") else 2 * keys[j][1] + int(keys[j][0].startswith("conv"))
    later1 = [j for j in later if layer_of(j) == 1]
    later2 = [j for j in later if layer_of(j) > 1]
    g2 = exchange_start([shards[j] for j in later1], zones_with_own(later1, g1[4]), None, name="gather_layer1_start")
    g3 = exchange_start([shards[j] for j in later2], zones_with_own(later2, g2[4]), None, name="gather_rest_start")
    g_mix_0 = g_mix[0] + g3[4][0, 0]

    def rows(name, l):
        g = full[(name, l)]
        return g.reshape(g.shape[0] * g.shape[1], g.shape[2])

    saved = []
    for i in range(depth):
        l = i // 2
        rec = {"x0": xs}
        if i == 1:
            arrived = exchange_wait(*g2[:4], None, xs, name="gather_layer1_wait")[1]
            full.update(zip([keys[j] for j in later1], arrived))
        if i == 2:
            arrived = exchange_wait(*g3[:4], None, xs, name="gather_rest_wait")[1]
            full.update(zip([keys[j] for j in later2], arrived))
        if i % 2 == 0:
            wd3 = _pad_last(rows("attn_w_down", l), DOWN_PAD)[None]
            gqn = _pad_last(attn_g_qnorm[l][None], QK_PAD)
            gkn = _pad_last(attn_g_knorm[l][None], QK_PAD)
            gqa, gkva = attn_g_q_a[l][None], attn_g_kv_a[l][None]
            h, a = norm_matmul(xs, g_mix_0 if i == 0 else g_mix[i], wd3, name="attn_down")
            if i == 0:
                arrived = exchange_wait(*g0[:4], None, a, name="gather_mixer0_wait")[1]
                full.update(zip([keys[j] for j in first_rest], arrived))
            wuq3 = _pad_last(full[("attn_w_uq", l)], QK_PAD)
            wukv3 = full[("attn_w_ukv", l)]
            q, k, v, cq, ckv = mla_pre_fwd(a, gqa, gkva, wuq3, wukv3, gqn, gkn, cos_t, sin_t)
            o, lse = attn_fwd(q, k, v, pos_col, pos_row)
            x1 = matmul_residual(o, rows("attn_w_o", l), xs, name="attn_out")
            rec.update(h=h, a=a, q=q, k=k, v=v, cq=cq, ckv=ckv, o=o, lse=lse, wd3=wd3, wuq3=wuq3, wukv3=wukv3,
                       gqn=gqn, gkn=gkn, gqa=gqa, gkva=gkva)
        else:
            cw = full[("conv_w", l)].transpose(1, 0, 2).reshape(3, D_MODEL)
            h, bcu = norm_matmul(xs, g_mix[i], full[("conv_w_in", l)], name="conv_in", out_dtype=MXU_DTYPE)
            z = conv_fwd(bcu, cw)
            x1 = matmul_residual(z, rows("conv_w_out", l), xs, name="conv_out")
            rec.update(h=h, bcu=bcu, z=z, cw=cw)
        if i == 0:
            arrived = exchange_wait(*g1[:4], None, x1, name="gather_mlp0_wait")[1]
            full.update(zip([keys[j] for j in second], arrived))
        h2, act, slope = norm_matmul(x1, g_mlp[i], full[("mlp_w1", i)], name="mlp_up", mlp=True)
        if i < depth - 1:
            xs = matmul_residual(act, rows("mlp_w2", i), x1, name="mlp_down")
        else:
            sq, dx, dxb = matmul_residual(act, rows("mlp_w2", i), x1, name="mlp_down_loss", target=target)
        rec.update(x1=x1, h2=h2, act=act, slope=slope)
        saved.append(rec)

    loss = lax.psum(sq[0, 0] * (0.5 / D_MODEL), MESH_AXES)

    grads = {name: [None] * weights[name].shape[0] for name in order}
    token = None
    for i in reversed(range(depth)):
        l = i // 2
        rec = saved[i]
        grads["mlp_w2"][i] = mm_tn(rec["act"], dxb, name="mlp_down_dw", G=1, out_dtype=WIRE_DTYPE).reshape(N_DEV, -1, D_MODEL)
        du = matmul_nt(dxb, rows("mlp_w2", i)[None], name="mlp_down_dx", epi="mlp_du", u=rec["slope"])
        grads["mlp_w1"][i] = mm_tn(rec["h2"], du, name="mlp_up_dw", G=N_DEV, out_dtype=WIRE_DTYPE, a_transposed=True)
        dx1, dx1b, dg = matmul_nt(du, full[("mlp_w1", i)], name="mlp_up_dx", epi="rms_bwd", x=rec["x1"], g=g_mlp[i], dx=dx)
        grads["g_mlp"][i] = dg[0]
        if i == 0:
            flying = [keys[j] for j in second + later]
            srcs = [grads[name][l_] for name, l_ in flying]
            slots = [(big.index(name), l_) for name, l_ in flying]
            zones = [lax.empty((N_DEV, weights[name].shape[0]) + grads[name][-1].shape[1:], grads[name][-1].dtype)
                     for name in big]
            for src, (k, l_) in zip(srcs, slots):
                own = lax.dynamic_index_in_dim(src, me, 0, keepdims=True)[None]
                zones[k] = lax.dynamic_update_slice(zones[k], own, (me, l_) + (0,) * (src.ndim - 1))
            s_send, s_recv, s_srcs, s_zones, token = exchange_start(srcs, zones, slots, name="scatter_rest_start")
        if i % 2 == 0:
            grads["attn_w_o"][l] = mm_tn(rec["o"], dx1b, name="attn_out_dw", G=1, out_dtype=WIRE_DTYPE,
                                         after=token).reshape(N_DEV, -1, D_MODEL)
            do, delta = matmul_nt(dx1b, rows("attn_w_o", l)[None], name="attn_out_dx", u=rec["o"])
            dq, dk, dv = attn_bwd(rec["q"], rec["k"], rec["v"], do, rec["lse"], delta, pos_col, pos_row)
            da, dwuq, dwukv, dgqn, dgkn, dgqa, dgkva = mla_pre_bwd(
                dq, dk, dv, rec["a"], rec["cq"], rec["ckv"], rec["gqa"], rec["gkva"], rec["wuq3"], rec["wukv3"],
                rec["gqn"], rec["gkn"], cos_t, sin_t)
            grads["attn_w_uq"][l] = dwuq[:, :, :QK_DIM].astype(WIRE_DTYPE)
            grads["attn_w_ukv"][l] = dwukv.astype(WIRE_DTYPE)
            grads["attn_g_qnorm"][l] = dgqn[0, :QK_DIM]
            grads["attn_g_knorm"][l] = dgkn[0, :QK_DIM]
            grads["attn_g_q_a"][l] = dgqa[0]
            grads["attn_g_kv_a"][l] = dgkva[0]
            dwd = mm_tn(rec["h"], da, name="attn_down_dw", G=1, out_dtype=WIRE_DTYPE, a_transposed=True)
            grads["attn_w_down"][l] = dwd[0, :, :DOWN].reshape(N_DEV, -1, DOWN)
            dx, dxb, dg = matmul_nt(da, rec["wd3"], name="attn_down_dx", epi="rms_bwd", x=rec["x0"], g=g_mix[i], dx=dx1)
        else:
            grads["conv_w_out"][l] = mm_tn(rec["z"], dx1b, name="conv_out_dw", G=1, out_dtype=WIRE_DTYPE).reshape(N_DEV, -1, D_MODEL)
            dz = matmul_nt(dx1b, rows("conv_w_out", l)[None], name="conv_out_dx")
            dbcu, dcw = conv_bwd(dz, rec["bcu"], rec["cw"])
            grads["conv_w"][l] = dcw.reshape(3, N_DEV, -1).transpose(1, 0, 2)
            grads["conv_w_in"][l] = mm_tn(rec["h"], dbcu, name="conv_in_dw", G=N_DEV, out_dtype=WIRE_DTYPE, a_transposed=True)
            dx, dxb, dg = matmul_nt(dbcu, full[("conv_w_in", l)], name="conv_in_dx", epi="rms_bwd", x=rec["x0"], g=g_mix[i], dx=dx1)
        grads["g_mix"][i] = dg[0]

    sizes = [weights[name].size for name in small]
    n_small = sum(sizes)
    rows_small = -(-n_small // (8 * 128)) * 8

    def pack(tree):
        flat = jnp.concatenate([jnp.stack(tree[name]).reshape(-1) if isinstance(tree[name], list) else tree[name].reshape(-1)
                                for name in small])
        return jnp.pad(flat, (0, rows_small * 128 - n_small)).reshape(rows_small, 128)

    s_srcs, s_zones = exchange_wait(s_send, s_recv, s_srcs, s_zones, slots, dx, name="scatter_rest_wait")
    remote = [(grads[name][l_], big.index(name), l_) for name, l_ in (keys[j] for j in mixer0)]
    parts, gain_parts = scatter_finish(remote, s_zones, pack(grads), name="scatter_last")

    out = {}
    for name, part in zip(big, parts):
        w = weights[name]
        flat = lambda t: t.reshape(-1, t.shape[-1])
        res = adamw(part.reshape(N_DEV, -1, w.shape[-1]), flat(w), flat(mom1[name]), flat(mom2[name]), name="adamw_" + name)
        out[name] = [r.reshape(w.shape) for r in res]
    res = adamw(gain_parts, pack(weights), pack(mom1), pack(mom2), name="adamw_gains")
    offset = 0
    for name, size in zip(small, sizes):
        out[name] = [r.reshape(-1)[offset:offset + size].reshape(weights[name].shape) for r in res]
        offset += size

    return (loss, dx[None], *[out[n][0] for n in order], *[out[n][1] for n in order],
            *[out[n][2] for n in order], *[out[n][3] for n in order])
```

```python
import jax
import jax.numpy as jnp
import numpy as np
from jax import lax
from jax.experimental import pallas as pl
from jax.experimental.pallas import tpu as pltpu

F32 = jnp.float32
MXU_DTYPE = jnp.bfloat16
WIRE_DTYPE = jnp.bfloat16

D_MODEL = 1024
N_HEADS = 8
NOPE = 128
ROPE = 64
QK_DIM = NOPE + ROPE
QK_PAD = 256
V_DIM = 128
Q_LORA = 256
KV_LORA = 128
DOWN = Q_LORA + KV_LORA + ROPE
DOWN_PAD = 512
ROPE_THETA = 10000.0
EPS = 1e-6
SM_SCALE = QK_DIM ** -0.5
LOG2E = 1.4426950408889634
Q_PRESCALE = SM_SCALE * LOG2E
ADAM_LR, ADAM_B1, ADAM_B2, ADAM_EPS, ADAM_WD, ADAM_STEP = 0.001, 0.9, 0.999, 1e-08, 0.01, 10
N_DEV = 8
MESH_AXES = ("x", "y", "c")

TM = 512
TM_WIDE = 1024
TILE_BUDGET = 32 << 20
TM_TOKENS_TN = 2048
TQ = 512
HEADS_FWD = 8
TQ_BWD = 2048
BWD_CHUNK = 256
TROW = 1024
HALO = 16
VMEM_LIMIT = 48 << 20

NN = (((1,), (0,)), ((), ()))
NT = (((1,), (1,)), ((), ()))
TN = (((0,), (0,)), ((), ()))


def _dot(a, b, dims=NN):
    return lax.dot_general(a.astype(MXU_DTYPE), b.astype(MXU_DTYPE), dims, preferred_element_type=F32)


def _params(n_axes):
    return pltpu.CompilerParams(dimension_semantics=("arbitrary",) * n_axes, vmem_limit_bytes=VMEM_LIMIT)


def _rms(xv, n):
    r = lax.rsqrt(jnp.sum(xv * xv, axis=-1, keepdims=True) / n + EPS)
    return xv * r, r


def _rms_bwd(dy, xhat, r, g, n):
    dg = jnp.sum(dy * xhat, axis=0, keepdims=True)
    dxh = dy * g
    dx = r * (dxh - xhat * (jnp.sum(dxh * xhat, axis=-1, keepdims=True) / n))
    return dx, dg


def _swap_halves(t):
    lane = lax.broadcasted_iota(jnp.int32, t.shape, 1)
    return jnp.where(lane < ROPE // 2, pltpu.roll(t, 128 - ROPE // 2, 1), pltpu.roll(t, ROPE // 2, 1))


def _rope(t, cos_t, sin_t):
    return t * cos_t + _swap_halves(t) * sin_t


def _rope_bwd(dout, cos_t, sin_t):
    return dout * cos_t + _swap_halves(dout * sin_t)


def _resident(shape):
    return pl.BlockSpec(shape, lambda i: (0,) * len(shape))


def _token_tile(S, row_bytes, resident_bytes):
    wide = min(TM_WIDE, S)
    return wide if 2 * (wide * row_bytes + resident_bytes) <= TILE_BUDGET else min(TM, S)


def norm_matmul(x, g, w3, *, name, mlp=False, out_dtype=F32):
    if mlp:
        out_dtype = MXU_DTYPE
    S, D = x.shape
    G, _, Nb = w3.shape
    N = G * Nb
    n_out = 2 if mlp else 1
    tm = _token_tile(S, D * 4 + D * 2 + n_out * N * jnp.dtype(out_dtype).itemsize, w3.size * w3.dtype.itemsize)

    def body(x_ref, g_ref, w_ref, h_ref, o_ref, *slope_ref):
        xv = x_ref[...]
        r = lax.rsqrt(jnp.mean(xv * xv, axis=-1, keepdims=True) + EPS)
        h = (xv * r * g_ref[...]).astype(h_ref.dtype)
        h_ref[...] = h.T
        for gi in range(G):
            cols = slice(gi * Nb, (gi + 1) * Nb)
            acc = _dot(h, w_ref[gi])
            if mlp:
                acc = jnp.maximum(acc, 0.0)
                slope_ref[0][:, cols] = (2.0 * acc).astype(out_dtype)
                acc = jnp.square(acc)
            o_ref[:, cols] = acc.astype(o_ref.dtype)

    rows = lambda w: pl.BlockSpec((tm, w), lambda i: (i, 0))
    return pl.pallas_call(
        body, name=name, grid=(S // tm,),
        in_specs=[rows(D), _resident((1, D)), _resident((G, D, Nb))],
        out_specs=[pl.BlockSpec((D, tm), lambda i: (0, i))] + [rows(N)] * n_out,
        out_shape=[jax.ShapeDtypeStruct((D, S), MXU_DTYPE)] + [jax.ShapeDtypeStruct((S, N), out_dtype)] * n_out,
        compiler_params=_params(1),
    )(x, g.reshape(1, D), w3)


def matmul_residual(a, w, res, *, name, target=None):
    S, K = a.shape
    _, N = w.shape
    tm = _token_tile(S, K * a.dtype.itemsize + 2 * N * 4 + (0 if target is None else N * 6), w.size * w.dtype.itemsize)

    def body(a_ref, w_ref, r_ref, *rest):
        y = r_ref[...] + _dot(a_ref[...], w_ref[...])
        if target is None:
            rest[0][...] = y
            return
        t_ref, l_ref, dy_ref, dyb_ref = rest
        err = y - t_ref[...]
        dy = err / N
        dy_ref[...] = dy
        dyb_ref[...] = dy.astype(dyb_ref.dtype)
        part = jnp.full((1, 128), jnp.sum(err * err), F32)

        @pl.when(pl.program_id(0) == 0)
        def _():
            l_ref[...] = part

        @pl.when(pl.program_id(0) != 0)
        def _():
            l_ref[...] += part

    rows = lambda w_: pl.BlockSpec((tm, w_), lambda i: (i, 0))
    in_specs, args = [rows(K), _resident((K, N)), rows(N)], [a, w, res]
    if target is None:
        out_specs, out_shape = rows(N), jax.ShapeDtypeStruct((S, N), F32)
    else:
        in_specs.append(rows(N))
        args.append(target)
        out_specs = [_resident((1, 128)), rows(N), rows(N)]
        out_shape = [jax.ShapeDtypeStruct((1, 128), F32), jax.ShapeDtypeStruct((S, N), F32),
                     jax.ShapeDtypeStruct((S, N), MXU_DTYPE)]
    return pl.pallas_call(
        body, name=name, grid=(S // tm,),
        in_specs=in_specs, out_specs=out_specs, out_shape=out_shape,
        compiler_params=_params(1),
    )(*args)


def matmul_nt(a, w3, *, name, epi="plain", u=None, x=None, g=None, dx=None):
    S, N = a.shape
    G, Ko, Nb = w3.shape
    assert N == G * Nb
    with_delta = epi == "plain" and u is not None
    row_bytes = N * a.dtype.itemsize + Ko * ({"plain": 4, "mlp_du": 4, "rms_bwd": 14}[epi] + (4 if with_delta else 0))
    tm = _token_tile(S, row_bytes, w3.size * w3.dtype.itemsize)
    tko = min(Ko, 512)

    def body(a_ref, w_ref, *rest):
        if epi == "mlp_du":
            u_ref, o_ref = rest
            av = a_ref[...].astype(MXU_DTYPE)
            for j in range(Ko // tko):
                cols = slice(j * tko, (j + 1) * tko)
                da = _dot(av, w_ref[0, cols, :], NT)
                o_ref[:, cols] = (da * u_ref[:, cols].astype(F32)).astype(o_ref.dtype)
            return
        acc = _dot(a_ref[:, :Nb], w_ref[0], NT)
        for gi in range(1, G):
            acc = acc + _dot(a_ref[:, gi * Nb:(gi + 1) * Nb], w_ref[gi], NT)
        if epi == "plain":
            if with_delta:
                u_ref, o_ref, d_ref = rest
                for h in range(N_HEADS):
                    cols = slice(h * V_DIM, (h + 1) * V_DIM)
                    d_ref[h] = jnp.sum((u_ref[:, cols] * acc[:, cols]).T, axis=0, keepdims=True)
            else:
                o_ref = rest[0]
            o_ref[...] = acc
        else:
            x_ref, g_ref, dx_ref, o_ref, ob_ref, dg_ref = rest
            xhat, r = _rms(x_ref[...], Ko)
            dxb, dg = _rms_bwd(acc, xhat, r, g_ref[...], Ko)
            dx_new = dx_ref[...] + dxb
            o_ref[...] = dx_new
            ob_ref[...] = dx_new.astype(ob_ref.dtype)

            @pl.when(pl.program_id(0) == 0)
            def _():
                dg_ref[...] = dg

            @pl.when(pl.program_id(0) != 0)
            def _():
                dg_ref[...] += dg

    rows = lambda w_: pl.BlockSpec((tm, w_), lambda i: (i, 0))
    in_specs = [rows(N), _resident((G, Ko, Nb))]
    args = [a, w3]
    if with_delta:
        in_specs.append(rows(Ko))
        args.append(u)
        out_shape = [jax.ShapeDtypeStruct((S, Ko), F32), jax.ShapeDtypeStruct((N_HEADS, 1, S), F32)]
        out_specs = [rows(Ko), pl.BlockSpec((N_HEADS, 1, tm), lambda i: (0, 0, i))]
    elif epi == "plain":
        out_shape, out_specs = jax.ShapeDtypeStruct((S, Ko), F32), rows(Ko)
    elif epi == "mlp_du":
        in_specs.append(rows(Ko))
        args.append(u)
        out_shape, out_specs = jax.ShapeDtypeStruct((S, Ko), MXU_DTYPE), rows(Ko)
    else:
        in_specs += [rows(Ko), _resident((1, Ko)), rows(Ko)]
        args += [x, g.reshape(1, Ko), dx]
        out_shape = [jax.ShapeDtypeStruct((S, Ko), F32), jax.ShapeDtypeStruct((S, Ko), MXU_DTYPE),
                     jax.ShapeDtypeStruct((1, Ko), F32)]
        out_specs = [rows(Ko), rows(Ko), _resident((1, Ko))]
    return pl.pallas_call(
        body, name=name, grid=(S // tm,),
        in_specs=in_specs, out_specs=out_specs, out_shape=out_shape,
        compiler_params=_params(1),
    )(*args)


def mm_tn(a, b, *, name, G, out_dtype, after=None, a_transposed=False):
    order = [] if after is None else [after]
    Ka, S = a.shape if a_transposed else a.shape[::-1]
    _, N = b.shape
    Nb = N // G
    assert Nb <= 1024
    tm = min(TM_TOKENS_TN if b.dtype.itemsize == 2 else TM_TOKENS_TN // 2, S)
    tka = min(Ka, 1024)
    gb = 2 if G % 2 == 0 and Nb <= 512 else 1
    ns = S // tm

    def body(a_ref, b_ref, *rest):
        o_ref, acc = rest[-2:]
        s = pl.program_id(2)

        @pl.when(s == 0)
        def _():
            acc[...] = jnp.zeros_like(acc)

        av = a_ref[...]
        for gi in range(gb):
            acc[gi] += _dot(av, b_ref[:, gi * Nb:(gi + 1) * Nb], NN if a_transposed else TN)

        @pl.when(s == ns - 1)
        def _():
            o_ref[...] = acc[...].astype(o_ref.dtype)

    a_spec = pl.BlockSpec((tka, tm), lambda i, j, s: (i, s)) if a_transposed else pl.BlockSpec((tm, tka), lambda i, j, s: (s, i))
    return pl.pallas_call(
        body, name=name, grid=(Ka // tka, G // gb, ns),
        in_specs=[a_spec,
                  pl.BlockSpec((tm, gb * Nb), lambda i, j, s: (s, j))] + [pl.BlockSpec(memory_space=pl.ANY)] * len(order),
        out_specs=pl.BlockSpec((gb, tka, Nb), lambda i, j, s: (j, i, 0)),
        out_shape=jax.ShapeDtypeStruct((G, Ka, Nb), out_dtype),
        scratch_shapes=[pltpu.VMEM((gb, tka, Nb), F32)],
        compiler_params=_params(3),
    )(a, b, *order)


def mla_pre_fwd(a, gqa, gkva, wuq3, wukv3, gqn, gkn, cos_t, sin_t):
    S = a.shape[0]
    tm = min(TM, S)
    H = N_HEADS

    def body(a_ref, gqa_ref, gkva_ref, wuq_ref, wukv_ref, gqn_ref, gkn_ref, cos_ref, sin_ref,
             q_ref, k_ref, v_ref, cq_ref, ckv_ref):
        av = a_ref[...]
        cq = (_rms(av[:, :Q_LORA], Q_LORA)[0] * gqa_ref[...]).astype(cq_ref.dtype)
        ckv = (_rms(av[:, Q_LORA:Q_LORA + KV_LORA], KV_LORA)[0] * gkva_ref[...]).astype(ckv_ref.dtype)
        cq_ref[...] = cq
        ckv_ref[...] = ckv
        kpe = av[:, Q_LORA + KV_LORA:]
        cos_v, sin_v = cos_ref[...], sin_ref[...]
        for h in range(H):
            qn = _rms(_dot(cq, wuq_ref[h]), QK_DIM)[0] * gqn_ref[...]
            qr = jnp.concatenate([qn[:, :NOPE], _rope(qn[:, NOPE:], cos_v, sin_v)], axis=1)
            q_ref[h] = (qr * Q_PRESCALE).astype(q_ref.dtype)
            kvp = _dot(ckv, wukv_ref[h])
            kn = _rms(jnp.concatenate([kvp[:, :NOPE], kpe], axis=1), QK_DIM)[0] * gkn_ref[...]
            k_ref[h] = jnp.concatenate([kn[:, :NOPE], _rope(kn[:, NOPE:], cos_v, sin_v)], axis=1).astype(k_ref.dtype)
            v_ref[h] = kvp[:, NOPE:].astype(v_ref.dtype)

    row = lambda w: pl.BlockSpec((tm, w), lambda i: (i, 0))
    heads = lambda w: pl.BlockSpec((H, tm, w), lambda i: (0, i, 0))
    return pl.pallas_call(
        body, name="mla_pre_fwd", grid=(S // tm,),
        in_specs=[row(DOWN_PAD), _resident((1, Q_LORA)), _resident((1, KV_LORA)),
                  _resident((H, Q_LORA, QK_PAD)), _resident((H, KV_LORA, NOPE + V_DIM)),
                  _resident((1, QK_PAD)), _resident((1, QK_PAD)), row(128), row(128)],
        out_specs=[heads(QK_PAD), heads(QK_PAD), heads(V_DIM), row(Q_LORA), row(KV_LORA)],
        out_shape=[jax.ShapeDtypeStruct((H, S, QK_PAD), MXU_DTYPE),
                   jax.ShapeDtypeStruct((H, S, QK_PAD), MXU_DTYPE),
                   jax.ShapeDtypeStruct((H, S, V_DIM), MXU_DTYPE),
                   jax.ShapeDtypeStruct((S, Q_LORA), MXU_DTYPE),
                   jax.ShapeDtypeStruct((S, KV_LORA), MXU_DTYPE)],
        compiler_params=_params(1),
    )(a, gqa, gkva, wuq3, wukv3, gqn, gkn, cos_t, sin_t)


def mla_pre_bwd(dq, dk, dv, a, cq, ckv, gqa, gkva, wuq3, wukv3, gqn, gkn, cos_t, sin_t):
    S = a.shape[0]
    tm = min(TM, S)
    H = N_HEADS

    def body(dq_ref, dk_ref, dv_ref, a_ref, cq_ref, ckv_ref, gqa_ref, gkva_ref, wuq_ref, wukv_ref, gqn_ref, gkn_ref,
             cos_ref, sin_ref, da_ref, dwuq_ref, dwukv_ref, dgqn_ref, dgkn_ref, dgqa_ref, dgkva_ref):
        @pl.when(pl.program_id(0) == 0)
        def _():
            for ref in (dwuq_ref, dwukv_ref, dgqn_ref, dgkn_ref, dgqa_ref, dgkva_ref):
                ref[...] = jnp.zeros_like(ref)

        av = a_ref[...]
        kpe = av[:, Q_LORA + KV_LORA:]
        cos_v, sin_v = cos_ref[...], sin_ref[...]
        cqv, ckvv = cq_ref[...], ckv_ref[...]
        dcq = jnp.zeros((tm, Q_LORA), F32)
        dckv = jnp.zeros((tm, KV_LORA), F32)
        dkpe = jnp.zeros((tm, 128), F32)
        dgqn = jnp.zeros((1, QK_PAD), F32)
        dgkn = jnp.zeros((1, QK_PAD), F32)
        up = lambda h: (_dot(cqv, wuq_ref[h]), _dot(ckvv, wukv_ref[h]))
        nxt = up(0)
        for h in range(H):
            wuq, wukv = wuq_ref[h], wukv_ref[h]
            qp, kvp = nxt
            if h + 1 < H:
                nxt = up(h + 1)
            qhat, rq = _rms(qp, QK_DIM)
            dqr = dq_ref[h] * SM_SCALE
            dqn = jnp.concatenate([dqr[:, :NOPE], _rope_bwd(dqr[:, NOPE:], cos_v, sin_v)], axis=1)
            dqp, dg = _rms_bwd(dqn, qhat, rq, gqn_ref[...], QK_DIM)
            dgqn = dgqn + dg
            dqp = dqp.astype(MXU_DTYPE)
            dwuq_ref[h] += _dot(cqv, dqp, TN)
            dcq = dcq + _dot(dqp, wuq, NT)
            khat, rk = _rms(jnp.concatenate([kvp[:, :NOPE], kpe], axis=1), QK_DIM)
            dkr = dk_ref[h] * (1.0 / LOG2E)
            dkn = jnp.concatenate([dkr[:, :NOPE], _rope_bwd(dkr[:, NOPE:], cos_v, sin_v)], axis=1)
            dkk, dg = _rms_bwd(dkn, khat, rk, gkn_ref[...], QK_DIM)
            dgkn = dgkn + dg
            dkpe = dkpe + dkk[:, NOPE:]
            dkvp = jnp.concatenate([dkk[:, :NOPE], dv_ref[h]], axis=1).astype(MXU_DTYPE)
            dwukv_ref[h] += _dot(ckvv, dkvp, TN)
            dckv = dckv + _dot(dkvp, wukv, NT)
        dgqn_ref[...] += dgqn
        dgkn_ref[...] += dgkn
        ahat, r = _rms(av[:, :Q_LORA], Q_LORA)
        daq, dg = _rms_bwd(dcq, ahat, r, gqa_ref[...], Q_LORA)
        dgqa_ref[...] += dg
        ahat, r = _rms(av[:, Q_LORA:Q_LORA + KV_LORA], KV_LORA)
        dakv, dg = _rms_bwd(dckv, ahat, r, gkva_ref[...], KV_LORA)
        dgkva_ref[...] += dg
        da_ref[...] = jnp.concatenate([daq, dakv, dkpe], axis=1)

    row = lambda w: pl.BlockSpec((tm, w), lambda i: (i, 0))
    heads = lambda w: pl.BlockSpec((H, tm, w), lambda i: (0, i, 0))
    return pl.pallas_call(
        body, name="mla_pre_bwd", grid=(S // tm,),
        in_specs=[heads(QK_PAD), heads(QK_PAD), heads(V_DIM), row(DOWN_PAD), row(Q_LORA), row(KV_LORA),
                  _resident((1, Q_LORA)), _resident((1, KV_LORA)),
                  _resident((H, Q_LORA, QK_PAD)), _resident((H, KV_LORA, NOPE + V_DIM)),
                  _resident((1, QK_PAD)), _resident((1, QK_PAD)), row(128), row(128)],
        out_specs=[row(DOWN_PAD), _resident((H, Q_LORA, QK_PAD)), _resident((H, KV_LORA, NOPE + V_DIM)),
                   _resident((1, QK_PAD)), _resident((1, QK_PAD)), _resident((1, Q_LORA)), _resident((1, KV_LORA))],
        out_shape=[jax.ShapeDtypeStruct((S, DOWN_PAD), F32),
                   jax.ShapeDtypeStruct((H, Q_LORA, QK_PAD), F32),
                   jax.ShapeDtypeStruct((H, KV_LORA, NOPE + V_DIM), F32),
                   jax.ShapeDtypeStruct((1, QK_PAD), F32), jax.ShapeDtypeStruct((1, QK_PAD), F32),
                   jax.ShapeDtypeStruct((1, Q_LORA), F32), jax.ShapeDtypeStruct((1, KV_LORA), F32)],
        compiler_params=_params(1),
    )(dq, dk, dv, a, cq, ckv, gqa, gkva, wuq3, wukv3, gqn, gkn, cos_t, sin_t)


def _pair_tables(nb, key_major):
    if key_major:
        pairs = [(qi, kj) for kj in range(nb) for qi in range(kj, nb)]
    else:
        pairs = [(qi, ki) for qi in range(nb) for ki in range(qi + 1)]
    return (jnp.asarray(np.array([p[0] for p in pairs], np.int32)),
            jnp.asarray(np.array([p[1] for p in pairs], np.int32)))


def _scores_t(k, q, pk_col, pq_row, masked):
    s = _dot(k, q, NT)
    return jnp.where(pq_row >= pk_col, s, jnp.finfo(F32).min) if masked else s


def attn_fwd(q, k, v, pos_col, pos_row):
    H, S, _ = q.shape
    t = min(TQ, S)
    nb = S // t
    hb = HEADS_FWD
    qt, kt = _pair_tables(nb, key_major=False)

    def body(qt_ref, kt_ref, q_ref, k_ref, v_ref, pk_ref, pq_ref, o_ref, lse_ref, m_s, l_s, acc):
        step = pl.program_id(1)
        qi, ki = qt_ref[step], kt_ref[step]

        @pl.when(ki == 0)
        def _():
            m_s[...] = jnp.full_like(m_s, -jnp.inf)
            l_s[...] = jnp.zeros_like(l_s)
            acc[...] = jnp.zeros_like(acc)

        def update(masked):
            scores = lambda hh: _scores_t(k_ref[hh], q_ref[hh], pk_ref[...], pq_ref[...], masked)
            def weighted_values(hh, p, alpha):
                acc[hh] = alpha * acc[hh] + _dot(v_ref[hh], p, TN)

            s_next = scores(0)
            pending = None
            for hh in range(hb):
                s = s_next
                if hh + 1 < hb:
                    s_next = scores(hh + 1)
                m_old = m_s[hh]
                m_new = jnp.maximum(m_old, jnp.max(s, axis=0, keepdims=True))
                p = jnp.exp2(s - m_new)
                alpha = jnp.exp2(m_old - m_new)
                l_s[hh] = alpha * l_s[hh] + jnp.sum(p, axis=0, keepdims=True)
                m_s[hh] = m_new
                if pending is not None:
                    weighted_values(*pending)
                pending = (hh, p, alpha)
            weighted_values(*pending)

        @pl.when(ki < qi)
        def _():
            update(False)

        @pl.when(ki == qi)
        def _():
            update(True)
            for hh in range(hb):
                o_ref[:, hh * V_DIM:(hh + 1) * V_DIM] = (acc[hh] / l_s[hh]).T
                lse_ref[hh] = m_s[hh] + jnp.log(l_s[hh]) * LOG2E

    grid_spec = pltpu.PrefetchScalarGridSpec(
        num_scalar_prefetch=2, grid=(H // hb, qt.shape[0]),
        in_specs=[pl.BlockSpec((hb, t, QK_PAD), lambda h, s, qt, kt: (h, qt[s], 0)),
                  pl.BlockSpec((hb, t, QK_PAD), lambda h, s, qt, kt: (h, kt[s], 0)),
                  pl.BlockSpec((hb, t, V_DIM), lambda h, s, qt, kt: (h, kt[s], 0)),
                  pl.BlockSpec((t, 1), lambda h, s, qt, kt: (kt[s], 0)),
                  pl.BlockSpec((1, t), lambda h, s, qt, kt: (0, qt[s]))],
        out_specs=[pl.BlockSpec((t, hb * V_DIM), lambda h, s, qt, kt: (qt[s], h)),
                   pl.BlockSpec((hb, 1, t), lambda h, s, qt, kt: (h, 0, qt[s]))],
        scratch_shapes=[pltpu.VMEM((hb, 1, t), F32), pltpu.VMEM((hb, 1, t), F32), pltpu.VMEM((hb, V_DIM, t), F32)])
    return pl.pallas_call(
        body, name="attn_fwd", grid_spec=grid_spec,
        out_shape=[jax.ShapeDtypeStruct((S, H * V_DIM), F32), jax.ShapeDtypeStruct((H, 1, S), F32)],
        compiler_params=_params(2),
    )(qt, kt, q, k, v, pos_col, pos_row)


def attn_bwd(q, k, v, do, lse, delta, pos_col, pos_row):
    H, S, _ = q.shape
    t = min(TQ_BWD, S)
    nb = S // t
    qt, kt = _pair_tables(nb, key_major=True)
    tc = min(BWD_CHUNK, t)

    def body(qt_ref, kt_ref, q_ref, k_ref, v_ref, do_ref, lse_ref, dl_ref, pk_ref, pq_ref, dq_ref, dk_ref, dv_ref):
        step = pl.program_id(1)
        qi, kj = qt_ref[step], kt_ref[step]

        @pl.when(step == 0)
        def _():
            dq_ref[...] = jnp.zeros_like(dq_ref)

        @pl.when(qi == kj)
        def _():
            dk_ref[...] = jnp.zeros_like(dk_ref)
            dv_ref[...] = jnp.zeros_like(dv_ref)

        def update(masked):
            seen = lambda c: (c + 1) * tc if masked else t

            def first_matmuls(c):
                cols, ke = slice(c * tc, (c + 1) * tc), seen(c)
                qc = q_ref[cols, :]
                doc = do_ref[cols, :].astype(MXU_DTYPE)
                s = _scores_t(k_ref[:ke, :], qc, pk_ref[:ke, :], pq_ref[:, cols], masked)
                return qc, doc, s, _dot(v_ref[:ke, :], doc, NT)

            nxt = first_matmuls(0)
            for c in range(t // tc):
                qc, doc, s, dp = nxt
                if c + 1 < t // tc:
                    nxt = first_matmuls(c + 1)
                cols, ke = slice(c * tc, (c + 1) * tc), seen(c)
                p = jnp.exp2(s - lse_ref[:, cols])
                ds = (p * (dp - dl_ref[:, cols])).astype(MXU_DTYPE)
                dv_ref[:ke, :] += _dot(p, doc)
                dk_ref[:ke, :] += _dot(ds, qc)
                rows = pl.ds(pl.multiple_of(qi * t + c * tc, tc), tc)
                dq_ref[rows, :] += _dot(ds, k_ref[:ke, :], TN)

        @pl.when(qi == kj)
        def _():
            update(True)

        @pl.when(qi != kj)
        def _():
            update(False)

    q_idx = lambda h, s, qt, kt: (h, qt[s], 0)
    k_idx = lambda h, s, qt, kt: (h, kt[s], 0)
    row_idx = lambda h, s, qt, kt: (h, 0, qt[s])
    grid_spec = pltpu.PrefetchScalarGridSpec(
        num_scalar_prefetch=2, grid=(H, qt.shape[0]),
        in_specs=[pl.BlockSpec((None, t, QK_PAD), q_idx),
                  pl.BlockSpec((None, t, QK_PAD), k_idx),
                  pl.BlockSpec((None, t, V_DIM), k_idx),
                  pl.BlockSpec((t, V_DIM), lambda h, s, qt, kt: (qt[s], h)),
                  pl.BlockSpec((None, 1, t), row_idx),
                  pl.BlockSpec((None, 1, t), row_idx),
                  pl.BlockSpec((t, 1), lambda h, s, qt, kt: (kt[s], 0)),
                  pl.BlockSpec((1, t), lambda h, s, qt, kt: (0, qt[s]))],
        out_specs=[pl.BlockSpec((None, S, QK_PAD), lambda h, s, qt, kt: (h, 0, 0)),
                   pl.BlockSpec((None, t, QK_PAD), k_idx),
                   pl.BlockSpec((None, t, V_DIM), k_idx)])
    return pl.pallas_call(
        body, name="attn_bwd", grid_spec=grid_spec,
        out_shape=[jax.ShapeDtypeStruct((H, S, QK_PAD), F32), jax.ShapeDtypeStruct((H, S, QK_PAD), F32),
                   jax.ShapeDtypeStruct((H, S, V_DIM), F32)],
        compiler_params=_params(2),
    )(qt, kt, q, k, v, do, lse, delta, pos_col, pos_row)


def _conv_specs(S, tr):
    hb = tr // HALO
    main = lambda third: pl.BlockSpec((tr, D_MODEL), lambda r: (r, third))
    prev = lambda third: pl.BlockSpec((HALO, D_MODEL), lambda r: (jnp.maximum(r * hb - 1, 0), third))
    nxt = lambda third: pl.BlockSpec((HALO, D_MODEL), lambda r: (jnp.minimum((r + 1) * hb, S // HALO - 1), third))
    return main, prev, nxt


def _f32(ref):
    return ref[...].astype(F32)


def _conv_taps(gc, uu, w_ref, first):
    u2 = gc * uu
    rows = lax.broadcasted_iota(jnp.int32, u2.shape, 0)
    u2 = jnp.where((rows < HALO) & first, 0.0, u2)
    s1 = pltpu.roll(u2, 1, 0)
    s2 = pltpu.roll(u2, 2, 0)
    u3 = w_ref[2:3, :] * u2 + w_ref[1:2, :] * s1 + w_ref[0:1, :] * s2
    return u2, s1, s2, u3


def conv_fwd(bcu, cw):
    S = bcu.shape[0]
    tr = min(TROW, S)
    main, prev, _ = _conv_specs(S, tr)

    def body(gb_ref, gc_ref, u_ref, gch_ref, uh_ref, w_ref, z_ref):
        gc = jnp.concatenate([_f32(gch_ref), _f32(gc_ref)], axis=0)
        uu = jnp.concatenate([_f32(uh_ref), _f32(u_ref)], axis=0)
        u3 = _conv_taps(gc, uu, w_ref, pl.program_id(0) == 0)[3]
        z_ref[...] = (_f32(gb_ref) * u3[HALO:]).astype(z_ref.dtype)

    return pl.pallas_call(
        body, name="conv_fwd", grid=(S // tr,),
        in_specs=[main(0), main(1), main(2), prev(1), prev(2), _resident((3, D_MODEL))],
        out_specs=main(0),
        out_shape=jax.ShapeDtypeStruct((S, D_MODEL), MXU_DTYPE),
        compiler_params=_params(1),
    )(bcu, bcu, bcu, bcu, bcu, cw)


def conv_bwd(dz, bcu, cw):
    S = bcu.shape[0]
    tr = min(TROW, S)
    nr = S // tr
    main, prev, nxt = _conv_specs(S, tr)

    def body(dz_ref, dzn_ref, gb_ref, gbn_ref, gc_ref, u_ref, gch_ref, uh_ref, w_ref, o_ref, dw_ref):
        r = pl.program_id(0)
        gcv, uv = _f32(gc_ref), _f32(u_ref)
        gc = jnp.concatenate([_f32(gch_ref), gcv], axis=0)
        uu = jnp.concatenate([_f32(uh_ref), uv], axis=0)
        u2, s1, s2, u3 = _conv_taps(gc, uu, w_ref, r == 0)
        dzv = dz_ref[...]
        du3 = jnp.concatenate([dzv * _f32(gb_ref), dzn_ref[...] * _f32(gbn_ref)], axis=0)
        rows = lax.broadcasted_iota(jnp.int32, du3.shape, 0)
        du3 = jnp.where((rows >= tr) & (r == nr - 1), 0.0, du3)
        n1 = pltpu.roll(du3, tr + HALO - 1, 0)
        n2 = pltpu.roll(du3, tr + HALO - 2, 0)
        du2 = (w_ref[2:3, :] * du3 + w_ref[1:2, :] * n1 + w_ref[0:1, :] * n2)[:tr]
        o_ref[:, :D_MODEL] = (dzv * u3[HALO:]).astype(o_ref.dtype)
        o_ref[:, D_MODEL:2 * D_MODEL] = (du2 * uv).astype(o_ref.dtype)
        o_ref[:, 2 * D_MODEL:] = (du2 * gcv).astype(o_ref.dtype)
        d3 = du3[:tr]
        taps = [jnp.sum(d3 * t[HALO:], axis=0, keepdims=True) for t in (s2, s1, u2)]

        @pl.when(r == 0)
        def _():
            for kk in range(3):
                dw_ref[kk:kk + 1, :] = taps[kk]

        @pl.when(r != 0)
        def _():
            for kk in range(3):
                dw_ref[kk:kk + 1, :] += taps[kk]

    return pl.pallas_call(
        body, name="conv_bwd", grid=(nr,),
        in_specs=[main(0), nxt(0), main(0), nxt(0), main(1), main(2), prev(1), prev(2), _resident((3, D_MODEL))],
        out_specs=[pl.BlockSpec((tr, 3 * D_MODEL), lambda r: (r, 0)), _resident((3, D_MODEL))],
        out_shape=[jax.ShapeDtypeStruct((S, 3 * D_MODEL), MXU_DTYPE), jax.ShapeDtypeStruct((3, D_MODEL), F32)],
        compiler_params=_params(1),
    )(dz, dz, bcu, bcu, bcu, bcu, bcu, bcu, cw)


def adamw(parts, w, m, v, *, name):
    R, C = w.shape
    tr = R
    while tr * C * 4 > (1 << 20) and tr % 32 == 0:
        tr //= 2

    def body(p_ref, w_ref, m_ref, v_ref, g_ref, d_ref, mo_ref, vo_ref):
        g = p_ref[0].astype(F32)
        for d in range(1, N_DEV):
            g = g + p_ref[d].astype(F32)
        m_new = ADAM_B1 * m_ref[...] + (1.0 - ADAM_B1) * g
        v_new = ADAM_B2 * v_ref[...] + (1.0 - ADAM_B2) * (g * g)
        m_hat = m_new / (1.0 - ADAM_B1 ** ADAM_STEP)
        v_hat = v_new / (1.0 - ADAM_B2 ** ADAM_STEP)
        g_ref[...] = g
        d_ref[...] = -ADAM_LR * (m_hat / (jnp.sqrt(v_hat) + ADAM_EPS) + ADAM_WD * w_ref[...])
        mo_ref[...] = m_new
        vo_ref[...] = v_new

    blk = pl.BlockSpec((tr, C), lambda i: (i, 0))
    return pl.pallas_call(
        body, name=name, grid=(R // tr,),
        in_specs=[pl.BlockSpec((N_DEV, tr, C), lambda i: (0, i, 0)), blk, blk, blk],
        out_specs=[blk, blk, blk, blk],
        out_shape=[jax.ShapeDtypeStruct((R, C), F32)] * 4,
        compiler_params=_params(1),
    )(parts, w, m, v)


def _mesh_place():
    x, y, c = (lax.axis_index(n) for n in MESH_AXES)
    return x, y, c, 4 * x + 2 * y + c


def _peer(x, y, c, d):
    px = 1 - x if d & 4 else x
    py = 1 - y if d & 2 else y
    pc = 1 - c if d & 1 else c
    return (px, py, pc), 4 * px + 2 * py + pc


def gather_now(srcs, *, name):
    n = len(srcs)
    any_spec = pl.BlockSpec(memory_space=pl.ANY)

    def body(*refs):
        ins, outs, token = refs[:n], refs[n:2 * n], refs[2 * n]
        send_sems, recv_sems, local_sems = refs[2 * n + 1:]
        token[...] = jnp.zeros_like(token)
        x, y, c, me = _mesh_place()
        for a in range(n):
            pltpu.make_async_copy(ins[a], outs[a].at[me], local_sems.at[a]).start()
            for d in range(1, N_DEV):
                pltpu.make_async_remote_copy(
                    src_ref=ins[a], dst_ref=outs[a].at[me], send_sem=send_sems.at[a], recv_sem=recv_sems.at[a],
                    device_id=_peer(x, y, c, d)[0], device_id_type=pl.DeviceIdType.MESH).start()
        for a in range(n):
            pltpu.make_async_copy(ins[a], outs[a].at[me], local_sems.at[a]).wait()
            seven = outs[a].at[pl.ds(0, N_DEV - 1)]
            drain = pltpu.make_async_remote_copy(
                src_ref=seven, dst_ref=seven, send_sem=send_sems.at[a], recv_sem=recv_sems.at[a],
                device_id=(x, y, c), device_id_type=pl.DeviceIdType.MESH)
            drain.wait_send()
            drain.wait_recv()

    out = pl.pallas_call(
        body, name=name,
        in_specs=[any_spec] * n, out_specs=[any_spec] * n + [pl.BlockSpec(memory_space=pltpu.VMEM)],
        out_shape=[jax.ShapeDtypeStruct((N_DEV,) + s.shape, s.dtype) for s in srcs] + [jax.ShapeDtypeStruct((8, 128), F32)],
        scratch_shapes=[pltpu.SemaphoreType.DMA((n,)), pltpu.SemaphoreType.DMA((n,)), pltpu.SemaphoreType.DMA((n,))],
    )(*srcs)
    return out[:n], out[n]


_ANY = pl.BlockSpec(memory_space=pl.ANY)
_HBM = pl.BlockSpec(memory_space=pltpu.HBM)
_SEM = pl.BlockSpec(memory_space=pltpu.SEMAPHORE)


def _in_hbm(arrays):
    return [pltpu.with_memory_space_constraint(a, pltpu.HBM) for a in arrays]


def exchange_start(srcs, lands, slots, *, name):
    n, m = len(srcs), len(lands)

    def body(*refs):
        ins, zones = refs[:n], refs[n:n + m]
        send_sems, recv_sems, token = refs[n + m], refs[n + m + 1], refs[-1]
        x, y, c, me = _mesh_place()
        for a in range(n):
            for d in range(1, N_DEV):
                peer, peer_lin = _peer(x, y, c, d)
                src = ins[a] if slots is None else ins[a].at[peer_lin]
                dst = zones[a].at[me] if slots is None else zones[slots[a][0]].at[me, slots[a][1]]
                pltpu.make_async_remote_copy(
                    src_ref=src, dst_ref=dst, send_sem=send_sems.at[a], recv_sem=recv_sems.at[a],
                    device_id=peer, device_id_type=pl.DeviceIdType.MESH).start()
        token[...] = jnp.zeros_like(token)

    both = list(srcs) + list(lands)
    out = pl.pallas_call(
        body, name=name,
        in_specs=[_HBM] * (n + m),
        out_specs=[_SEM, _SEM] + [_HBM] * (n + m) + [pl.BlockSpec(memory_space=pltpu.VMEM)],
        out_shape=[pltpu.SemaphoreType.DMA((n,)), pltpu.SemaphoreType.DMA((n,))]
        + [pltpu.HBM(a.shape, a.dtype) for a in both] + [jax.ShapeDtypeStruct((8, 128), F32)],
        input_output_aliases={i: 2 + i for i in range(n + m)},
        compiler_params=pltpu.CompilerParams(has_side_effects=pltpu.SideEffectType.DATAFLOW_SIDE_EFFECTING),
    )(*_in_hbm(both))
    return out[0], out[1], out[2:2 + n], out[2 + n:2 + n + m], out[-1]


def exchange_wait(send_sems, recv_sems, srcs, lands, slots, after, *, name):
    n, m = len(srcs), len(lands)

    def body(*refs):
        ins, zones = refs[:n], refs[n:n + m]
        send_ref, recv_ref = refs[n + m], refs[n + m + 1]
        x, y, c, _ = _mesh_place()
        for a in range(n):
            seven = (zones[a] if slots is None else ins[a]).at[pl.ds(0, N_DEV - 1)]
            drain = pltpu.make_async_remote_copy(
                src_ref=seven, dst_ref=seven, send_sem=send_ref.at[a], recv_sem=recv_ref.at[a],
                device_id=(x, y, c), device_id_type=pl.DeviceIdType.MESH)
            drain.wait_send()
            drain.wait_recv()

    both = list(srcs) + list(lands)
    out = pl.pallas_call(
        body, name=name,
        in_specs=[_HBM] * (n + m) + [_SEM, _SEM, _ANY],
        out_specs=[_HBM] * (n + m),
        out_shape=[pltpu.HBM(a.shape, a.dtype) for a in both],
        input_output_aliases={i: i for i in range(n + m)},
        compiler_params=pltpu.CompilerParams(has_side_effects=pltpu.SideEffectType.DATAFLOW_SIDE_EFFECTING),
    )(*both, send_sems, recv_sems, after)
    return out[:n], out[n:]


def scatter_finish(remote, lands, vec, *, name):
    n, m = len(remote), len(lands)

    def body(*refs):
        ins, vec_ref, zones_in = refs[:n], refs[n], refs[n + 1:n + 1 + m]
        vec_out = refs[n + 1 + 2 * m]
        send_sems, recv_sems, local_sems = refs[n + 2 + 2 * m:]
        x, y, c, me = _mesh_place()

        def ends(a, j):
            if a == n:
                return vec_ref, vec_out.at[me]
            return ins[a].at[j], zones_in[remote[a][1]].at[me, remote[a][2]]

        for a in range(n + 1):
            pltpu.make_async_copy(*ends(a, me), local_sems.at[a]).start()
            for d in range(1, N_DEV):
                peer, peer_lin = _peer(x, y, c, d)
                src, dst = ends(a, peer_lin)
                pltpu.make_async_remote_copy(
                    src_ref=src, dst_ref=dst, send_sem=send_sems.at[a], recv_sem=recv_sems.at[a],
                    device_id=peer, device_id_type=pl.DeviceIdType.MESH).start()
        for a in range(n + 1):
            pltpu.make_async_copy(*ends(a, me), local_sems.at[a]).wait()
            seven = (vec_out if a == n else ins[a]).at[pl.ds(0, N_DEV - 1)]
            drain = pltpu.make_async_remote_copy(
                src_ref=seven, dst_ref=seven, send_sem=send_sems.at[a], recv_sem=recv_sems.at[a],
                device_id=(x, y, c), device_id_type=pl.DeviceIdType.MESH)
            drain.wait_send()
            drain.wait_recv()

    out = pl.pallas_call(
        body, name=name,
        in_specs=[_ANY] * (n + 1 + m), out_specs=[_ANY] * (m + 1),
        out_shape=[jax.ShapeDtypeStruct(z.shape, z.dtype) for z in lands]
        + [jax.ShapeDtypeStruct((N_DEV,) + vec.shape, vec.dtype)],
        input_output_aliases={n + 1 + i: i for i in range(m)},
        scratch_shapes=[pltpu.SemaphoreType.DMA((n + 1,))] * 3,
    )(*[e[0] for e in remote], vec, *lands)
    return out[:m], out[m]


def _rope_tables(pos):
    inv_freq = ROPE_THETA ** (-jnp.arange(0, ROPE, 2, dtype=F32) / ROPE)
    ang = pos.astype(F32)[:, None] * inv_freq
    cos, sin = jnp.cos(ang), jnp.sin(ang)
    pad = jnp.zeros((pos.shape[0], 128 - ROPE), F32)
    return jnp.concatenate([cos, cos, pad + 1.0], axis=1), jnp.concatenate([-sin, sin, pad], axis=1)


def _pad_last(w, n):
    return jnp.pad(w, [(0, 0)] * (w.ndim - 1) + [(0, n - w.shape[-1])])


def kernel(x, positions, g_mix, g_mlp, attn_w_down, attn_g_q_a, attn_g_kv_a, attn_w_uq, attn_w_ukv, attn_g_qnorm, attn_g_knorm, attn_w_o, conv_w_in, conv_w, conv_w_out, mlp_w1, mlp_w2, loss_target, m_g_mix, m_g_mlp, m_attn_w_down, m_attn_g_q_a, m_attn_g_kv_a, m_attn_w_uq, m_attn_w_ukv, m_attn_g_qnorm, m_attn_g_knorm, m_attn_w_o, m_conv_w_in, m_conv_w, m_conv_w_out, m_mlp_w1, m_mlp_w2, v_g_mix, v_g_mlp, v_attn_w_down, v_attn_g_q_a, v_attn_g_kv_a, v_attn_w_uq, v_attn_w_ukv, v_attn_g_qnorm, v_attn_g_knorm, v_attn_w_o, v_conv_w_in, v_conv_w, v_conv_w_out, v_mlp_w1, v_mlp_w2):
    weights = dict(g_mix=g_mix, g_mlp=g_mlp, attn_w_down=attn_w_down, attn_g_q_a=attn_g_q_a, attn_g_kv_a=attn_g_kv_a,
                   attn_w_uq=attn_w_uq, attn_w_ukv=attn_w_ukv, attn_g_qnorm=attn_g_qnorm, attn_g_knorm=attn_g_knorm,
                   attn_w_o=attn_w_o, conv_w_in=conv_w_in, conv_w=conv_w, conv_w_out=conv_w_out, mlp_w1=mlp_w1, mlp_w2=mlp_w2)
    mom1 = dict(g_mix=m_g_mix, g_mlp=m_g_mlp, attn_w_down=m_attn_w_down, attn_g_q_a=m_attn_g_q_a, attn_g_kv_a=m_attn_g_kv_a,
                attn_w_uq=m_attn_w_uq, attn_w_ukv=m_attn_w_ukv, attn_g_qnorm=m_attn_g_qnorm, attn_g_knorm=m_attn_g_knorm,
                attn_w_o=m_attn_w_o, conv_w_in=m_conv_w_in, conv_w=m_conv_w, conv_w_out=m_conv_w_out, mlp_w1=m_mlp_w1, mlp_w2=m_mlp_w2)
    mom2 = dict(g_mix=v_g_mix, g_mlp=v_g_mlp, attn_w_down=v_attn_w_down, attn_g_q_a=v_attn_g_q_a, attn_g_kv_a=v_attn_g_kv_a,
                attn_w_uq=v_attn_w_uq, attn_w_ukv=v_attn_w_ukv, attn_g_qnorm=v_attn_g_qnorm, attn_g_knorm=v_attn_g_knorm,
                attn_w_o=v_attn_w_o, conv_w_in=v_conv_w_in, conv_w=v_conv_w, conv_w_out=v_conv_w_out, mlp_w1=v_mlp_w1, mlp_w2=v_mlp_w2)
    big = ["attn_w_down", "attn_w_uq", "attn_w_ukv", "attn_w_o", "conv_w_in", "conv_w", "conv_w_out", "mlp_w1", "mlp_w2"]
    small = ["g_mix", "g_mlp", "attn_g_q_a", "attn_g_kv_a", "attn_g_qnorm", "attn_g_knorm"]
    order = ["g_mix", "g_mlp", "attn_w_down", "attn_g_q_a", "attn_g_kv_a", "attn_w_uq", "attn_w_ukv", "attn_g_qnorm",
             "attn_g_knorm", "attn_w_o", "conv_w_in", "conv_w", "conv_w_out", "mlp_w1", "mlp_w2"]

    xs = x[0]
    pos = positions[0]
    target = loss_target[0]
    S = xs.shape[0]
    depth = g_mix.shape[0]
    cos_t, sin_t = _rope_tables(pos)
    pos_col, pos_row = pos.reshape(S, 1), pos.reshape(1, S)

    keys, shards = [], []
    for name in big:
        for l in range(weights[name].shape[0]):
            keys.append((name, l))
            shards.append(weights[name][l] if name == "conv_w" else weights[name][l].astype(WIRE_DTYPE))
    mixer0 = [j for j, (name, l) in enumerate(keys) if l == 0 and name.startswith("attn")]
    first = [j for j in mixer0 if keys[j][0] == "attn_w_down"]
    first_rest = [j for j in mixer0 if j not in first]
    second = [j for j, (name, l) in enumerate(keys) if l == 0 and name.startswith("mlp")]
    later = [j for j in range(len(keys)) if j not in mixer0 + second]
    me = 4 * lax.axis_index("x") + 2 * lax.axis_index("y") + lax.axis_index("c")

    def zones_with_own(js, token):
        return [lax.dynamic_update_slice(lax.empty((N_DEV,) + shards[j].shape, shards[j].dtype),
                                         (shards[j] + token[0, 0].astype(shards[j].dtype))[None],
                                         (me,) + (0,) * shards[j].ndim) for j in js]

    arrived, token = gather_now([shards[j] for j in first], name="gather_first")
    full = dict(zip([keys[j] for j in first], arrived))
    g0 = exchange_start([shards[j] for j in first_rest], zones_with_own(first_rest, token), None, name="gather_mixer0_start")
    g1 = exchange_start([shards[j] for j in second], zones_with_own(second, g0[4]), None, name="gather_mlp0_start")
    layer_of = lambda j: keys[j][1] if keys[j][0].startswith("mlp") else 2 * keys[j][1] + int(keys[j][0].startswith("conv"))
    later1 = [j for j in later if layer_of(j) == 1]
    later2 = [j for j in later if layer_of(j) > 1]
    g2 = exchange_start([shards[j] for j in later1], zones_with_own(later1, g1[4]), None, name="gather_layer1_start")
    g3 = exchange_start([shards[j] for j in later2], zones_with_own(later2, g2[4]), None, name="gather_rest_start")
    g_mix_0 = g_mix[0] + g3[4][0, 0]

    def rows(name, l):
        g = full[(name, l)]
        return g.reshape(g.shape[0] * g.shape[1], g.shape[2])

    saved = []
    for i in range(depth):
        l = i // 2
        rec = {"x0": xs}
        if i == 1:
            arrived = exchange_wait(*g2[:4], None, xs, name="gather_layer1_wait")[1]
            full.update(zip([keys[j] for j in later1], arrived))
        if i == 2:
            arrived = exchange_wait(*g3[:4], None, xs, name="gather_rest_wait")[1]
            full.update(zip([keys[j] for j in later2], arrived))
        if i % 2 == 0:
            wd3 = _pad_last(rows("attn_w_down", l), DOWN_PAD)[None]
            gqn = _pad_last(attn_g_qnorm[l][None], QK_PAD)
            gkn = _pad_last(attn_g_knorm[l][None], QK_PAD)
            gqa, gkva = attn_g_q_a[l][None], attn_g_kv_a[l][None]
            h, a = norm_matmul(xs, g_mix_0 if i == 0 else g_mix[i], wd3, name="attn_down")
            if i == 0:
                arrived = exchange_wait(*g0[:4], None, a, name="gather_mixer0_wait")[1]
                full.update(zip([keys[j] for j in first_rest], arrived))
            wuq3 = _pad_last(full[("attn_w_uq", l)], QK_PAD)
            wukv3 = full[("attn_w_ukv", l)]
            q, k, v, cq, ckv = mla_pre_fwd(a, gqa, gkva, wuq3, wukv3, gqn, gkn, cos_t, sin_t)
            o, lse = attn_fwd(q, k, v, pos_col, pos_row)
            x1 = matmul_residual(o, rows("attn_w_o", l), xs, name="attn_out")
            rec.update(h=h, a=a, q=q, k=k, v=v, cq=cq, ckv=ckv, o=o, lse=lse, wd3=wd3, wuq3=wuq3, wukv3=wukv3,
                       gqn=gqn, gkn=gkn, gqa=gqa, gkva=gkva)
        else:
            cw = full[("conv_w", l)].transpose(1, 0, 2).reshape(3, D_MODEL)
            h, bcu = norm_matmul(xs, g_mix[i], full[("conv_w_in", l)], name="conv_in", out_dtype=MXU_DTYPE)
            z = conv_fwd(bcu, cw)
            x1 = matmul_residual(z, rows("conv_w_out", l), xs, name="conv_out")
            rec.update(h=h, bcu=bcu, z=z, cw=cw)
        if i == 0:
            arrived = exchange_wait(*g1[:4], None, x1, name="gather_mlp0_wait")[1]
            full.update(zip([keys[j] for j in second], arrived))
        h2, act, slope = norm_matmul(x1, g_mlp[i], full[("mlp_w1", i)], name="mlp_up", mlp=True)
        if i < depth - 1:
            xs = matmul_residual(act, rows("mlp_w2", i), x1, name="mlp_down")
        else:
            sq, dx, dxb = matmul_residual(act, rows("mlp_w2", i), x1, name="mlp_down_loss", target=target)
        rec.update(x1=x1, h2=h2, act=act, slope=slope)
        saved.append(rec)

    loss = lax.psum(sq[0, 0] * (0.5 / D_MODEL), MESH_AXES)

    grads = {name: [None] * weights[name].shape[0] for name in order}
    token = None
    for i in reversed(range(depth)):
        l = i // 2
        rec = saved[i]
        grads["mlp_w2"][i] = mm_tn(rec["act"], dxb, name="mlp_down_dw", G=1, out_dtype=WIRE_DTYPE).reshape(N_DEV, -1, D_MODEL)
        du = matmul_nt(dxb, rows("mlp_w2", i)[None], name="mlp_down_dx", epi="mlp_du", u=rec["slope"])
        grads["mlp_w1"][i] = mm_tn(rec["h2"], du, name="mlp_up_dw", G=N_DEV, out_dtype=WIRE_DTYPE, a_transposed=True)
        dx1, dx1b, dg = matmul_nt(du, full[("mlp_w1", i)], name="mlp_up_dx", epi="rms_bwd", x=rec["x1"], g=g_mlp[i], dx=dx)
        grads["g_mlp"][i] = dg[0]
        if i == 0:
            flying = [keys[j] for j in second + later]
            srcs = [grads[name][l_] for name, l_ in flying]
            slots = [(big.index(name), l_) for name, l_ in flying]
            zones = [lax.empty((N_DEV, weights[name].shape[0]) + grads[name][-1].shape[1:], grads[name][-1].dtype)
                     for name in big]
            for src, (k, l_) in zip(srcs, slots):
                own = lax.dynamic_index_in_dim(src, me, 0, keepdims=True)[None]
                zones[k] = lax.dynamic_update_slice(zones[k], own, (me, l_) + (0,) * (src.ndim - 1))
            s_send, s_recv, s_srcs, s_zones, token = exchange_start(srcs, zones, slots, name="scatter_rest_start")
        if i % 2 == 0:
            grads["attn_w_o"][l] = mm_tn(rec["o"], dx1b, name="attn_out_dw", G=1, out_dtype=WIRE_DTYPE,
                                         after=token).reshape(N_DEV, -1, D_MODEL)
            do, delta = matmul_nt(dx1b, rows("attn_w_o", l)[None], name="attn_out_dx", u=rec["o"])
            dq, dk, dv = attn_bwd(rec["q"], rec["k"], rec["v"], do, rec["lse"], delta, pos_col, pos_row)
            da, dwuq, dwukv, dgqn, dgkn, dgqa, dgkva = mla_pre_bwd(
                dq, dk, dv, rec["a"], rec["cq"], rec["ckv"], rec["gqa"], rec["gkva"], rec["wuq3"], rec["wukv3"],
                rec["gqn"], rec["gkn"], cos_t, sin_t)
            grads["attn_w_uq"][l] = dwuq[:, :, :QK_DIM].astype(WIRE_DTYPE)
            grads["attn_w_ukv"][l] = dwukv.astype(WIRE_DTYPE)
            grads["attn_g_qnorm"][l] = dgqn[0, :QK_DIM]
            grads["attn_g_knorm"][l] = dgkn[0, :QK_DIM]
            grads["attn_g_q_a"][l] = dgqa[0]
            grads["attn_g_kv_a"][l] = dgkva[0]
            dwd = mm_tn(rec["h"], da, name="attn_down_dw", G=1, out_dtype=WIRE_DTYPE, a_transposed=True)
            grads["attn_w_down"][l] = dwd[0, :, :DOWN].reshape(N_DEV, -1, DOWN)
            dx, dxb, dg = matmul_nt(da, rec["wd3"], name="attn_down_dx", epi="rms_bwd", x=rec["x0"], g=g_mix[i], dx=dx1)
        else:
            grads["conv_w_out"][l] = mm_tn(rec["z"], dx1b, name="conv_out_dw", G=1, out_dtype=WIRE_DTYPE).reshape(N_DEV, -1, D_MODEL)
            dz = matmul_nt(dx1b, rows("conv_w_out", l)[None], name="conv_out_dx")
            dbcu, dcw = conv_bwd(dz, rec["bcu"], rec["cw"])
            grads["conv_w"][l] = dcw.reshape(3, N_DEV, -1).transpose(1, 0, 2)
            grads["conv_w_in"][l] = mm_tn(rec["h"], dbcu, name="conv_in_dw", G=N_DEV, out_dtype=WIRE_DTYPE, a_transposed=True)
            dx, dxb, dg = matmul_nt(dbcu, full[("conv_w_in", l)], name="conv_in_dx", epi="rms_bwd", x=rec["x0"], g=g_mix[i], dx=dx1)
        grads["g_mix"][i] = dg[0]

    sizes = [weights[name].size for name in small]
    n_small = sum(sizes)
    rows_small = -(-n_small // (8 * 128)) * 8

    def pack(tree):
        flat = jnp.concatenate([jnp.stack(tree[name]).reshape(-1) if isinstance(tree[name], list) else tree[name].reshape(-1)
                                for name in small])
        return jnp.pad(flat, (0, rows_small * 128 - n_small)).reshape(rows_small, 128)

    s_srcs, s_zones = exchange_wait(s_send, s_recv, s_srcs, s_zones, slots, dx, name="scatter_rest_wait")
    remote = [(grads[name][l_], big.index(name), l_) for name, l_ in (keys[j] for j in mixer0)]
    parts, gain_parts = scatter_finish(remote, s_zones, pack(grads), name="scatter_last")

    out = {}
    for name, part in zip(big, parts):
        w = weights[name]
        flat = lambda t: t.reshape(-1, t.shape[-1])
        res = adamw(part.reshape(N_DEV, -1, w.shape[-1]), flat(w), flat(mom1[name]), flat(mom2[name]), name="adamw_" + name)
        out[name] = [r.reshape(w.shape) for r in res]
    res = adamw(gain_parts, pack(weights), pack(mom1), pack(mom2), name="adamw_gains")
    offset = 0
    for name, size in zip(small, sizes):
        out[name] = [r.reshape(-1)[offset:offset + size].reshape(weights[name].shape) for r in res]
        offset += size

    return (loss, dx[None], *[out[n][0] for n in order], *[out[n][1] for n in order],
            *[out[n][2] for n in order], *[out[n][3] for n in order])
```
